```python
import math
import jax
import jax.numpy as jnp
from jax import lax
import numpy as np

D_MODEL = 2048
BATCH = 4
SEQ = 4096
DEPTH = 2

CTX_LEN = 256
GRID_W = 64
EPS = 1e-6
CONV_W = 3
SSD_WIDTH = D_MODEL
SSD_HEAD_DIM = 64
SSD_HEADS = SSD_WIDTH // SSD_HEAD_DIM
SSD_STATE = 128
SSD_GROUPS = 4
SSD_CHUNK = 128
SSD_XBC = SSD_WIDTH + 2 * SSD_GROUPS * SSD_STATE
HY_WIDTH = D_MODEL
HY_EMB = 33
HY_BANDS = (HY_EMB - 1) // 2
HY_ORDER = 64
HY_FAST = 0.3
HY_SLOW = 1.5
HY_TARGET = 1e-2
ML_HEADS = 8
ML_WIDTH = D_MODEL
ML_V_DIM = ML_WIDTH // ML_HEADS
ML_QK_DIM = ML_V_DIM // 2
ML_QK_WIDTH = ML_HEADS * ML_QK_DIM
ML_CHUNK = 64
ROPE_THETA = 10000.0
NA_WIDTH = D_MODEL
NA_HEAD_DIM = 128
NA_HEADS = NA_WIDTH // NA_HEAD_DIM
NA_ROWS = 8
NA_COLS = 16
D_FF = 256 * ((8 * D_MODEL // 3 + 255) // 256)
EV_CONV_CH = SSD_XBC + 3 * HY_WIDTH
EV_COLS = SSD_WIDTH + EV_CONV_CH + 2 * SSD_HEADS
OD_COLS = 2 * ML_QK_WIDTH + 2 * ML_WIDTH + 4 * ML_HEADS + 3 * NA_WIDTH
N_EVEN = (DEPTH + 1) // 2
N_ODD = DEPTH // 2

kernel_name = 'hybrid_ssd_hyena_mlstm_natten_dit'


def rmsnorm(x, w):
    x32 = x.astype(jnp.float32)
    y = x32 * lax.rsqrt(jnp.mean(x32 * x32, axis=-1, keepdims=True) + EPS)
    return (y * w).astype(x.dtype)


def group_rmsnorm(y, w, groups):
    sh = y.shape
    yg = y.reshape(*sh[:-1], groups, sh[-1] // groups)
    return rmsnorm(yg, w.reshape(groups, -1)).reshape(sh)


def modulate(h, shift, scale):
    return h * (1.0 + scale) + shift


def ada_mod(cvec, w, b):
    return jnp.split(jax.nn.silu(cvec) @ w + b, 6, axis=-1)


def flip(t):
    return jnp.flip(t, axis=1)


def dwconv(x, w, b):
    L = x.shape[1]
    pad = CONV_W // 2
    xp = jnp.pad(x, ((0, 0), (pad, pad), (0, 0)))
    return sum(xp[:, j:j + L] * w[j] for j in range(CONV_W)) + b


def axial_rope(x):
    L, dh = x.shape[1], x.shape[-1]
    nf = dh // 4
    t = jnp.arange(L)
    inv = ROPE_THETA ** (-jnp.arange(nf, dtype=jnp.float32) / nf)

    def rot(xh, pos):
        ang = pos.astype(jnp.float32)[:, None] * inv
        cos = jnp.cos(ang)[None, :, None, :]
        sin = jnp.sin(ang)[None, :, None, :]
        x1, x2 = xh[..., :nf], xh[..., nf:]
        return jnp.concatenate([x1 * cos - x2 * sin, x1 * sin + x2 * cos], axis=-1)

    out = jnp.concatenate([rot(x[..., :dh // 2], t // GRID_W), rot(x[..., dh // 2:], t % GRID_W)], axis=-1)
    return out.astype(x.dtype)


def ssd_chunked(xs, dt, a, bm, cm, h0, with_output):
    b, L, H, P = xs.shape
    G, N = bm.shape[2], bm.shape[3]
    R = H // G
    Q = SSD_CHUNK
    nc = L // Q
    dt = dt.astype(jnp.float32)
    xq = (xs * dt[..., None]).reshape(b, nc, Q, G, R, P)
    a_cum = jnp.cumsum((dt * a).reshape(b, nc, Q, G, R), axis=2)
    a_tot = a_cum[:, :, -1]
    bq = bm.reshape(b, nc, Q, G, N)
    states = jnp.einsum('bcsgn,bcsgr,bcsgrp->bcgrpn', bq, jnp.exp(a_tot[:, :, None] - a_cum), xq)

    def step(h, inp):
        at, st = inp
        return jnp.exp(at)[..., None, None] * h + st, h

    h_fin, h_prev = lax.scan(step, h0.reshape(b, G, R, P, N),
                             (jnp.moveaxis(a_tot, 1, 0), jnp.moveaxis(states, 1, 0)))
    h_fin = h_fin.reshape(b, H, P, N)
    if not with_output:
        return None, h_fin
    cq = cm.reshape(b, nc, Q, G, N)
    a_t = jnp.moveaxis(a_cum, 2, -1)
    tril = jnp.tril(jnp.ones((Q, Q), dtype=bool))
    decay_in = jnp.exp(jnp.where(tril, a_t[..., :, None] - a_t[..., None, :], -jnp.inf))
    y_diag = jnp.einsum('bclgn,bcsgn,bcgrls,bcsgrp->bclgrp', cq, bq, decay_in, xq)
    y_off = jnp.einsum('bclgn,bcgrpn,bclgr->bclgrp', cq, jnp.moveaxis(h_prev, 0, 1), jnp.exp(a_cum))
    return (y_diag + y_off).reshape(b, L, H, P), h_fin


def mlstm_chunked(q, k, v, i_pre, f_pre, state, with_output):
    b, L, H, dk = q.shape
    Q = ML_CHUNK
    nc = L // Q

    def chunks(t):
        return jnp.moveaxis(t.reshape(b, nc, Q, *t.shape[2:]), 1, 0)

    qs = chunks(q.astype(jnp.float32) * dk ** -0.5)
    ks = chunks(k.astype(jnp.float32))
    vs = chunks(v.astype(jnp.float32))
    li = chunks(i_pre.astype(jnp.float32))
    lf = chunks(jax.nn.log_sigmoid(f_pre.astype(jnp.float32)))
    tril = jnp.tril(jnp.ones((Q, Q), dtype=bool))[None, :, :, None]

    def step(carry, inp):
        c_s, n_s, m_s = carry
        qc, kc, vc, lic, lfc = inp
        bcum = jnp.cumsum(lfc, axis=1)
        b_tot = bcum[:, -1]
        w_state = b_tot[:, None] - bcum + lic
        m_new = jnp.maximum(b_tot + m_s, jnp.max(w_state, axis=1))
        h = None
        if with_output:
            dmat = jnp.where(tril, bcum[:, :, None, :] - bcum[:, None, :, :] + lic[:, None, :, :], -jnp.inf)
            inter = bcum + m_s[:, None]
            m_t = jnp.maximum(inter, jnp.max(dmat, axis=2))
            s = jnp.einsum('bthd,bshd->btsh', qc, kc) * jnp.exp(dmat - m_t[:, :, None])
            dec = jnp.exp(inter - m_t)
            num = jnp.einsum('btsh,bshv->bthv', s, vc) + dec[..., None] * jnp.einsum('bhvd,bthd->bthv', c_s, qc)
            den = jnp.sum(s, axis=2) + dec * jnp.einsum('bhd,bthd->bth', n_s, qc)
            h = num / jnp.maximum(jnp.abs(den), jnp.exp(-m_t))[..., None]
        decay_state = jnp.exp(b_tot + m_s - m_new)
        ws = jnp.exp(w_state - m_new[:, None])
        c_new = decay_state[..., None, None] * c_s + jnp.einsum('bsh,bshv,bshd->bhvd', ws, vc, kc)
        n_new = decay_state[..., None] * n_s + jnp.einsum('bsh,bshd->bhd', ws, kc)
        return (c_new, n_new, m_new), h

    state, hs = lax.scan(step, state, (qs, ks, vs, li, lf))
    if not with_output:
        return None, state
    return jnp.moveaxis(hs, 0, 1).reshape(b, L, H, v.shape[-1]), state


def hyena_filters(L, w1, b1, w2, b2, w3, freq):
    t = jnp.linspace(0.0, 1.0, L, dtype=jnp.float32)[:, None]
    w = 2.0 * math.pi * jnp.arange(L, dtype=jnp.float32)[:, None] / L
    f = jnp.linspace(1e-4, HY_BANDS - 1, HY_BANDS, dtype=jnp.float32)[None, :]
    feats = jnp.concatenate([t, jnp.cos(f * w), -jnp.sin(f * w)], axis=-1)
    h = jnp.sin(freq[0] * (feats @ w1 + b1))
    h = jnp.sin(freq[1] * (h @ w2 + b2))
    h = (h @ w3).reshape(L, 2, HY_WIDTH)
    deltas = jnp.abs(jnp.linspace(math.log(HY_TARGET) / HY_FAST, math.log(HY_TARGET) / HY_SLOW, HY_WIDTH, dtype=jnp.float32))
    h = h * jnp.exp(-t * deltas)[:, None, :]
    return h[:, 0], h[:, 1]


def long_conv_bidir(z, h_f, h_b, d_bias):
    L = z.shape[1]
    kern = jnp.concatenate([h_f, jnp.zeros_like(h_f[:1]), h_b[:0:-1]], axis=0)
    kf = jnp.fft.rfft(kern.astype(jnp.float32), n=2 * L, axis=0)
    zf = jnp.fft.rfft(z.astype(jnp.float32), n=2 * L, axis=1)
    y = jnp.fft.irfft(zf * kf[None], n=2 * L, axis=1)[:, :L]
    return (y + z.astype(jnp.float32) * d_bias).astype(z.dtype)


def na_latent(q, k, v, k_ctx, v_ctx, rpb):
    b, L, H, dh = q.shape
    rows = L // GRID_W
    kr = min(NA_ROWS, rows)
    qg = (q * dh ** -0.5).reshape(b, rows, GRID_W, H, dh)
    kg = k.reshape(b, rows, GRID_W, H, dh)
    vg = v.reshape(b, rows, GRID_W, H, dh)
    col = jnp.arange(GRID_W)
    cs = jnp.clip(col - NA_COLS // 2, 0, GRID_W - NA_COLS)
    in_win = (col[None, :] >= cs[:, None]) & (col[None, :] < cs[:, None] + NA_COLS)
    dc = jnp.clip(col[None, :] - col[:, None] + NA_COLS - 1, 0, 2 * NA_COLS - 2)

    def row_block(r):
        rs = jnp.clip(r - NA_ROWS // 2, 0, rows - kr)
        q_r = lax.dynamic_index_in_dim(qg, r, axis=1, keepdims=False)
        k_b = lax.dynamic_slice_in_dim(kg, rs, kr, axis=1)
        v_b = lax.dynamic_slice_in_dim(vg, rs, kr, axis=1)
        dr = rs + jnp.arange(kr) - r + NA_ROWS - 1
        bias = jnp.moveaxis(rpb[:, dr[:, None, None], dc[None, :, :]], 1, 2)
        s_win = jnp.einsum('bqhd,bikhd->bhqik', q_r, k_b).astype(jnp.float32) + bias[None].astype(jnp.float32)
        s_win = jnp.where(in_win[None, None, :, None, :], s_win, -jnp.inf)
        s_ctx = jnp.einsum('bqhd,bkhd->bhqk', q_r, k_ctx).astype(jnp.float32)
        p = jax.nn.softmax(jnp.concatenate([s_win.reshape(b, H, GRID_W, kr * GRID_W), s_ctx], axis=-1), axis=-1)
        p = p.astype(v.dtype)
        return (jnp.einsum('bhqk,bkhd->bqhd', p[..., :kr * GRID_W], v_b.reshape(b, kr * GRID_W, H, dh))
                + jnp.einsum('bhqk,bkhd->bqhd', p[..., kr * GRID_W:], v_ctx))

    out = lax.map(row_block, jnp.arange(rows))
    return jnp.moveaxis(out, 0, 1).reshape(b, L, H, dh)


def ctx_attention(q, k, v):
    s = jnp.einsum('bqhd,bkhd->bhqk', q * q.shape[-1] ** -0.5, k).astype(jnp.float32)
    p = jax.nn.softmax(s, axis=-1).astype(v.dtype)
    return jnp.einsum('bhqk,bkhd->bqhd', p, v)


def ssd_hyena_mixer(u_c, u_l, w_in, conv_w, conv_b, dt_bias, a_log, d_skip, norm_w,
                    hy_w1, hy_b1, hy_w2, hy_b2, hy_w3, hy_freq, hy_bias, w_out, ctx_out):
    def project(u):
        b, L, _ = u.shape
        pr = u @ w_in
        z = pr[..., :SSD_WIDTH]
        cv = dwconv(pr[..., SSD_WIDTH:SSD_WIDTH + EV_CONV_CH], conv_w, conv_b)
        dt = pr[..., SSD_WIDTH + EV_CONV_CH:]
        xbc = jax.nn.silu(cv[..., :SSD_XBC])
        xs = xbc[..., :SSD_WIDTH].reshape(b, L, SSD_HEADS, SSD_HEAD_DIM)
        bm = xbc[..., SSD_WIDTH:SSD_WIDTH + SSD_GROUPS * SSD_STATE].reshape(b, L, SSD_GROUPS, SSD_STATE)
        cm = xbc[..., SSD_WIDTH + SSD_GROUPS * SSD_STATE:].reshape(b, L, SSD_GROUPS, SSD_STATE)
        dt_f = jax.nn.softplus(dt[..., :SSD_HEADS] + dt_bias[0])
        dt_b = jax.nn.softplus(dt[..., SSD_HEADS:] + dt_bias[1])
        return z, xs, bm, cm, dt_f, dt_b, cv[..., SSD_XBC:]

    z_c, xs_c, b_c, c_c, df_c, db_c, hy_c = project(u_c)
    z_l, xs_l, b_l, c_l, df_l, db_l, hy_l = project(u_l)
    a_f = -jnp.exp(a_log[0].astype(jnp.float32))
    a_b = -jnp.exp(a_log[1].astype(jnp.float32))
    h0 = jnp.zeros((u_c.shape[0], SSD_HEADS, SSD_HEAD_DIM, SSD_STATE), jnp.float32)
    yc_f, hf = ssd_chunked(xs_c, df_c, a_f, b_c, c_c, h0, ctx_out)
    yl_f, _ = ssd_chunked(xs_l, df_l, a_f, b_l, c_l, hf, True)
    yc_b, hb = ssd_chunked(flip(xs_c), flip(db_c), a_b, flip(b_c), flip(c_c), h0, ctx_out)
    yl_b, _ = ssd_chunked(flip(xs_l), flip(db_l), a_b, flip(b_l), flip(c_l), hb, True)

    def ssd_out(yf, yb, xs, z):
        y = (yf + flip(yb) + xs * d_skip[:, None]).reshape(z.shape).astype(z.dtype)
        return group_rmsnorm(y * jax.nn.silu(z), norm_w, SSD_GROUPS)

    def hyena(hy):
        x0, x1, v = jnp.split(hy, 3, axis=-1)
        h_f, h_b = hyena_filters(hy.shape[1], hy_w1, hy_b1, hy_w2, hy_b2, hy_w3, hy_freq)
        return x0 * long_conv_bidir(x1 * v, h_f, h_b, hy_bias)

    y_l = jnp.concatenate([ssd_out(yl_f, yl_b, xs_l, z_l), hyena(hy_l)], axis=-1) @ w_out
    y_c = None
    if ctx_out:
        y_c = jnp.concatenate([ssd_out(yc_f, yc_b, xs_c, z_c), hyena(hy_c)], axis=-1) @ w_out
    return y_c, y_l


def mlstm_na_mixer(u_c, u_l, w_in, conv_w, conv_b, gate_b, ml_norm_w, q_norm_w, k_norm_w, rpb, w_out, ctx_out):
    o1 = 2 * ML_QK_WIDTH
    o2 = o1 + ML_WIDTH
    o3 = o2 + ML_WIDTH
    o4 = o3 + 4 * ML_HEADS

    def project(u, latent):
        b, L, _ = u.shape
        pr = u @ w_in
        qk = jax.nn.silu(dwconv(pr[..., :o1], conv_w, conv_b))
        q = qk[..., :ML_QK_WIDTH].reshape(b, L, ML_HEADS, ML_QK_DIM)
        k = qk[..., ML_QK_WIDTH:].reshape(b, L, ML_HEADS, ML_QK_DIM)
        if latent:
            q, k = axial_rope(q), axial_rope(k)
        v = pr[..., o1:o2].reshape(b, L, ML_HEADS, ML_V_DIM)
        o = pr[..., o2:o3]
        g = pr[..., o3:o4].reshape(b, L, 4, ML_HEADS) + gate_b
        rest = pr[..., o4:]
        qd = rmsnorm(rest[..., :NA_WIDTH].reshape(b, L, NA_HEADS, NA_HEAD_DIM), q_norm_w)
        kd = rmsnorm(rest[..., NA_WIDTH:2 * NA_WIDTH].reshape(b, L, NA_HEADS, NA_HEAD_DIM), k_norm_w)
        vd = rest[..., 2 * NA_WIDTH:].reshape(b, L, NA_HEADS, NA_HEAD_DIM)
        return q, k, v, o, g, qd, kd, vd

    q_c, k_c, v_c, o_c, g_c, qd_c, kd_c, vd_c = project(u_c, False)
    q_l, k_l, v_l, o_l, g_l, qd_l, kd_l, vd_l = project(u_l, True)
    b = u_c.shape[0]
    state0 = (jnp.zeros((b, ML_HEADS, ML_V_DIM, ML_QK_DIM), jnp.float32),
              jnp.zeros((b, ML_HEADS, ML_QK_DIM), jnp.float32),
              jnp.zeros((b, ML_HEADS), jnp.float32))
    hc_f, st_f = mlstm_chunked(q_c, k_c, v_c, g_c[..., 0, :], g_c[..., 1, :], state0, ctx_out)
    hl_f, _ = mlstm_chunked(q_l, k_l, v_l, g_l[..., 0, :], g_l[..., 1, :], st_f, True)
    hc_b, st_b = mlstm_chunked(flip(q_c), flip(k_c), flip(v_c), flip(g_c[..., 2, :]), flip(g_c[..., 3, :]), state0, ctx_out)
    hl_b, _ = mlstm_chunked(flip(q_l), flip(k_l), flip(v_l), flip(g_l[..., 2, :]), flip(g_l[..., 3, :]), st_b, True)

    def mlstm_out(hf, hb, o):
        h = rmsnorm((hf + flip(hb)).astype(o.dtype), ml_norm_w.reshape(ML_HEADS, ML_V_DIM))
        return h.reshape(o.shape) * jax.nn.sigmoid(o)

    bl, Ll = u_l.shape[0], u_l.shape[1]
    y_na_l = na_latent(qd_l, kd_l, vd_l, kd_c, vd_c, rpb).reshape(bl, Ll, NA_WIDTH)
    y_l = jnp.concatenate([mlstm_out(hl_f, hl_b, o_l), y_na_l], axis=-1) @ w_out
    y_c = None
    if ctx_out:
        y_na_c = ctx_attention(qd_c, kd_c, vd_c).reshape(b, u_c.shape[1], NA_WIDTH)
        y_c = jnp.concatenate([mlstm_out(hc_f, hc_b, o_c), y_na_c], axis=-1) @ w_out
    return y_c, y_l


def conv_ffn(u, w_up, conv_w, conv_b, w_down):
    a, g = jnp.split(u @ w_up, 2, axis=-1)
    return (a * jax.nn.silu(dwconv(g, conv_w, conv_b))) @ w_down


def setup_inputs(seed: int = 0) -> dict:
    key = jax.random.key(seed)
    ks = iter(jax.random.split(key, 48))
    D = D_MODEL

    def nrm(shape, scale=1.0):
        return scale * jax.random.normal(next(ks), shape, jnp.float32)

    def gain(shape):
        return 1.0 + nrm(shape, 0.02)

    dt0 = jnp.exp(jax.random.uniform(next(ks), (N_EVEN, 2, SSD_HEADS), jnp.float32,
                                     minval=math.log(1e-3), maxval=math.log(1e-1)))
    ssd_dt_bias = dt0 + jnp.log(-jnp.expm1(-dt0))
    ssd_a_log = jnp.log(jax.random.uniform(next(ks), (N_EVEN, 2, SSD_HEADS), jnp.float32, minval=1.0, maxval=16.0))
    gate_i = nrm((N_ODD, 2, ML_HEADS), 0.1)
    gate_f = jnp.linspace(3.0, 6.0, ML_HEADS, dtype=jnp.float32) + nrm((N_ODD, 2, ML_HEADS), 0.1)
    ml_gate_b = jnp.stack([gate_i[:, 0], gate_f[:, 0], gate_i[:, 1], gate_f[:, 1]], axis=1)
    return {
        'x': nrm((BATCH, SEQ, D)),
        'c': nrm((BATCH, D)),
        'ctx': nrm((BATCH, CTX_LEN, D)),
        'c_ctx': nrm((D,)),
        'ada_w': nrm((DEPTH, D, 6 * D), 0.5 * D ** -0.5),
        'ada_b': nrm((DEPTH, 6 * D), 0.02),
        'norm_w': gain((DEPTH, 2, D)),
        'ev_w_in': nrm((N_EVEN, D, EV_COLS), D ** -0.5),
        'ev_conv_w': nrm((N_EVEN, CONV_W, EV_CONV_CH), 0.5),
        'ev_conv_b': nrm((N_EVEN, EV_CONV_CH), 0.02),
        'ssd_dt_bias': ssd_dt_bias,
        'ssd_a_log': ssd_a_log,
        'ssd_d': gain((N_EVEN, SSD_HEADS)),
        'ssd_norm_w': gain((N_EVEN, SSD_WIDTH)),
        'hy_w1': nrm((N_EVEN, HY_EMB, HY_ORDER), HY_EMB ** -0.5),
        'hy_b1': nrm((N_EVEN, HY_ORDER), 0.1),
        'hy_w2': nrm((N_EVEN, HY_ORDER, HY_ORDER), HY_ORDER ** -0.5),
        'hy_b2': nrm((N_EVEN, HY_ORDER), 0.1),
        'hy_w3': nrm((N_EVEN, HY_ORDER, 2 * HY_WIDTH), 0.03 * HY_ORDER ** -0.5),
        'hy_freq': gain((N_EVEN, 2, HY_ORDER)),
        'hy_bias': nrm((N_EVEN, HY_WIDTH), 0.5),
        'ev_w_out': nrm((N_EVEN, SSD_WIDTH + HY_WIDTH, D), (SSD_WIDTH + HY_WIDTH) ** -0.5),
        'od_w_in': nrm((N_ODD, D, OD_COLS), D ** -0.5),
        'ml_conv_w': nrm((N_ODD, CONV_W, 2 * ML_QK_WIDTH), 0.5),
        'ml_conv_b': nrm((N_ODD, 2 * ML_QK_WIDTH), 0.02),
        'ml_gate_b': ml_gate_b,
        'ml_norm_w': gain((N_ODD, ML_WIDTH)),
        'na_q_norm_w': gain((N_ODD, NA_HEAD_DIM)),
        'na_k_norm_w': gain((N_ODD, NA_HEAD_DIM)),
        'na_rpb': nrm((N_ODD, NA_HEADS, 2 * NA_ROWS - 1, 2 * NA_COLS - 1), 0.05),
        'od_w_out': nrm((N_ODD, ML_WIDTH + NA_WIDTH, D), (ML_WIDTH + NA_WIDTH) ** -0.5),
        'ffn_w_up': nrm((DEPTH, D, 2 * D_FF), D ** -0.5),
        'ffn_conv_w': nrm((DEPTH, CONV_W, D_FF), 0.5),
        'ffn_conv_b': nrm((DEPTH, D_FF), 0.02),
        'ffn_w_down': nrm((DEPTH, D_FF, D), D_FF ** -0.5),
    }


def reference(x, c, ctx, c_ctx, ada_w, ada_b, norm_w,
              ev_w_in, ev_conv_w, ev_conv_b, ssd_dt_bias, ssd_a_log, ssd_d, ssd_norm_w,
              hy_w1, hy_b1, hy_w2, hy_b2, hy_w3, hy_freq, hy_bias, ev_w_out,
              od_w_in, ml_conv_w, ml_conv_b, ml_gate_b, ml_norm_w, na_q_norm_w, na_k_norm_w, na_rpb, od_w_out,
              ffn_w_up, ffn_conv_w, ffn_conv_b, ffn_w_down):
    x_lat, x_ctx = x, ctx
    for i in range(DEPTH):
        ctx_out = i < DEPTH - 1
        m_l = [m[:, None, :] for m in ada_mod(c, ada_w[i], ada_b[i])]
        m_c = ada_mod(c_ctx, ada_w[i], ada_b[i])
        u_l = modulate(rmsnorm(x_lat, norm_w[i, 0]), m_l[0], m_l[1])
        u_c = modulate(rmsnorm(x_ctx, norm_w[i, 0]), m_c[0], m_c[1])
        j = i // 2
        if i % 2 == 0:
            y_c, y_l = ssd_hyena_mixer(u_c, u_l, ev_w_in[j], ev_conv_w[j], ev_conv_b[j], ssd_dt_bias[j], ssd_a_log[j],
                                       ssd_d[j], ssd_norm_w[j], hy_w1[j], hy_b1[j], hy_w2[j], hy_b2[j], hy_w3[j],
                                       hy_freq[j], hy_bias[j], ev_w_out[j], ctx_out)
        else:
            y_c, y_l = mlstm_na_mixer(u_c, u_l, od_w_in[j], ml_conv_w[j], ml_conv_b[j], ml_gate_b[j], ml_norm_w[j],
                                      na_q_norm_w[j], na_k_norm_w[j], na_rpb[j], od_w_out[j], ctx_out)
        x_lat = x_lat + m_l[2] * y_l
        u_l = modulate(rmsnorm(x_lat, norm_w[i, 1]), m_l[3], m_l[4])
        x_lat = x_lat + m_l[5] * conv_ffn(u_l, ffn_w_up[i], ffn_conv_w[i], ffn_conv_b[i], ffn_w_down[i])
        if ctx_out:
            x_ctx = x_ctx + m_c[2] * y_c
            u_c = modulate(rmsnorm(x_ctx, norm_w[i, 1]), m_c[3], m_c[4])
            x_ctx = x_ctx + m_c[5] * conv_ffn(u_c, ffn_w_up[i], ffn_conv_w[i], ffn_conv_b[i], ffn_w_down[i])
    return x_lat
```

```python
import functools
import math

import numpy as np
import jax
import jax.numpy as jnp
from jax import lax
from jax.experimental import pallas as pl
from jax.experimental.pallas import tpu as pltpu

F32 = jnp.float32
BF16 = jnp.bfloat16
HI = lax.Precision.HIGHEST

D_MODEL = 2048
SEQ = 4096
CTX_LEN = 256
GRID_W = 64
EPS = 1e-6
CONV_W = 3
SSD_WIDTH = D_MODEL
SSD_HEAD_DIM = 64
SSD_HEADS = SSD_WIDTH // SSD_HEAD_DIM
SSD_STATE = 128
SSD_GROUPS = 4
SSD_HPG = SSD_HEADS // SSD_GROUPS
SSD_GW = SSD_WIDTH // SSD_GROUPS
SSD_XBC = SSD_WIDTH + 2 * SSD_GROUPS * SSD_STATE
HY_WIDTH = D_MODEL
HY_EMB = 33
HY_BANDS = (HY_EMB - 1) // 2
HY_ORDER = 64
HY_FAST = 0.3
HY_SLOW = 1.5
HY_TARGET = 1e-2
ML_HEADS = 8
ML_WIDTH = D_MODEL
ML_V_DIM = ML_WIDTH // ML_HEADS
ML_QK_DIM = ML_V_DIM // 2
ML_QK_WIDTH = ML_HEADS * ML_QK_DIM
ROPE_THETA = 10000.0
NA_WIDTH = D_MODEL
NA_HEAD_DIM = 128
NA_HEADS = NA_WIDTH // NA_HEAD_DIM
NA_ROWS = 8
NA_COLS = 16
NA_QROWS = 8
NA_KROWS = 16
D_FF = 256 * ((8 * D_MODEL // 3 + 255) // 256)
EV_CONV_CH = SSD_XBC + 3 * HY_WIDTH
SCAN_Q = 128
LANES = 128
PAD = 8
VMEM_LIMIT = 48 * 1024 * 1024


def _cp(sem, vmem=VMEM_LIMIT):
    return pltpu.CompilerParams(dimension_semantics=sem, vmem_limit_bytes=vmem)


def _sigmoid(x):
    return 1.0 / (1.0 + jnp.exp(-x))


def _silu(x):
    return x * _sigmoid(x)


def _softplus(x):
    return jnp.maximum(x, 0.0) + jnp.log(1.0 + jnp.exp(-jnp.abs(x)))


def _log_sigmoid(x):
    return -_softplus(-x)


def _dot(a, b, precision=None):
    return jnp.dot(a, b, preferred_element_type=F32, precision=precision)


def _dot_nt(a, b):
    return lax.dot_general(a, b, (((1,), (1,)), ((), ())), preferred_element_type=F32)


def _mm_body(*refs, nk, n_extra, epilogue, has_buf):
    a_ref, w_ref = refs[0], refs[1]
    extra = refs[2:2 + n_extra]
    pos = 2 + n_extra + (1 if has_buf else 0)
    o_ref = refs[pos]
    part = _dot(a_ref[...].astype(BF16), w_ref[...].astype(BF16))
    row_blk = pl.program_id(2)

    def finish(acc):
        val = acc if epilogue is None else epilogue(acc, row_blk, *extra)
        o_ref[...] = val.astype(o_ref.dtype)

    if nk == 1:
        finish(part)
    else:
        acc_ref = refs[pos + 1]
        k = pl.program_id(3)

        @pl.when(k == 0)
        def _():
            acc_ref[...] = part

        @pl.when(k > 0)
        def _():
            acc_ref[...] += part

        @pl.when(k == nk - 1)
        def _():
            finish(acc_ref[...])


def _matmul(a, w, *, tm, tn, tk, out_dtype, rows=None, extra=(), epilogue=None,
            out_buf=None, out_row_blk=0, out_col_blk=0, name="mm"):
    ba, ta, kdim = a.shape
    bw, kdim2, n = w.shape
    assert kdim == kdim2
    nb = max(ba, bw)
    rows = ta if rows is None else rows
    assert rows % tm == 0 and n % tn == 0 and kdim % tk == 0
    ni, nj, nk = rows // tm, n // tn, kdim // tk
    a_map = (lambda j, b, i, k: (b, i, k)) if ba > 1 else (lambda j, b, i, k: (0, i, k))
    w_map = (lambda j, b, i, k: (b, k, j)) if bw > 1 else (lambda j, b, i, k: (0, k, j))
    in_specs = [pl.BlockSpec((None, tm, tk), a_map), pl.BlockSpec((None, tk, tn), w_map)]
    args = [a, w]
    for arr, spec in extra:
        args.append(arr)
        in_specs.append(spec)
    aliases = {}
    if out_buf is not None:
        aliases = {len(args): 0}
        args.append(out_buf)
        in_specs.append(pl.BlockSpec(memory_space=pl.ANY))
        out_shape = jax.ShapeDtypeStruct(out_buf.shape, out_buf.dtype)
    else:
        out_shape = jax.ShapeDtypeStruct((nb, rows, n), out_dtype)
    out_spec = pl.BlockSpec((None, tm, tn), lambda j, b, i, k: (b, i + out_row_blk, j + out_col_blk))
    scratch = [pltpu.VMEM((tm, tn), F32)] if nk > 1 else []
    body = functools.partial(_mm_body, nk=nk, n_extra=len(extra), epilogue=epilogue,
                             has_buf=out_buf is not None)
    return pl.pallas_call(
        body, out_shape=out_shape, grid=(nj, nb, ni, nk), in_specs=in_specs, out_specs=out_spec,
        scratch_shapes=scratch, input_output_aliases=aliases, name=name,
        compiler_params=_cp(("parallel", "parallel", "parallel", "arbitrary")))(*args)


def _row_is_lat(row_blk, tm, n_lat):
    rows = row_blk * tm + lax.broadcasted_iota(jnp.int32, (tm, 1), 0)
    return rows < n_lat


def _gated_residual_epilogue(acc, row_blk, res_ref, gate_ref, *, tm, n_lat):
    gate = jnp.where(_row_is_lat(row_blk, tm, n_lat), gate_ref[0], gate_ref[1])
    return res_ref[...] + gate * acc


def _matmul_gated_residual(a, w, res, gate, *, tm, tn, tk, rows, n_lat, name):
    extra = [(res, pl.BlockSpec((None, tm, tn), lambda j, b, i, k: (b, i, j))),
             (gate, pl.BlockSpec((None, 2, 1, tn), lambda j, b, i, k: (b, 0, 0, j)))]
    epi = functools.partial(_gated_residual_epilogue, tm=tm, n_lat=n_lat)
    return _matmul(a, w, tm=tm, tn=tn, tk=tk, out_dtype=F32, rows=rows, extra=extra,
                   epilogue=epi, name=name)


def _rmsmod_body(x_ref, w_ref, mod_ref, o_ref, *, tm, n_lat):
    x = x_ref[...]
    y = x * lax.rsqrt(jnp.mean(x * x, axis=-1, keepdims=True) + EPS) * w_ref[...]
    is_lat = _row_is_lat(pl.program_id(2), tm, n_lat)
    shift = jnp.where(is_lat, mod_ref[0, 0], mod_ref[0, 1])
    scale = jnp.where(is_lat, mod_ref[1, 0], mod_ref[1, 1])
    o_ref[...] = (y * (1.0 + scale) + shift).astype(o_ref.dtype)


def _rmsmod(x, w, mod, *, tm, rows, n_lat):
    nb, _, d = x.shape
    tm = tm // 4
    assert rows % tm == 0 and tm % 16 == 0
    body = functools.partial(_rmsmod_body, tm=tm, n_lat=n_lat)
    return pl.pallas_call(
        body, out_shape=jax.ShapeDtypeStruct((nb, rows, d), BF16), grid=(1, nb, rows // tm),
        in_specs=[pl.BlockSpec((None, tm, d), lambda j, b, i: (b, i, 0)),
                  pl.BlockSpec((1, d), lambda j, b, i: (0, 0)),
                  pl.BlockSpec((None, 2, 2, 1, d), lambda j, b, i: (b, 0, 0, 0, 0))],
        out_specs=pl.BlockSpec((None, tm, d), lambda j, b, i: (b, i, 0)),
        name="rmsmod", compiler_params=_cp(("parallel", "parallel", "parallel")))(x, w, mod)


CONV_CH = 128


def _seg_bases(segs):
    bases, pos = [], 0
    for _, n in segs:
        bases.append(pos)
        pos += PAD + n
    return bases, pos + PAD


def _conv_fill(s_ref, x_ref, segs):
    bases, total = _seg_bases(segs)
    zeros = jnp.zeros((PAD, s_ref.shape[1]), F32)
    for (r0, n), base in zip(segs, bases):
        s_ref[base:base + PAD, :] = zeros

        def body(i, carry, r0=r0, base=base):
            t0 = pl.multiple_of(i * CONV_CH, CONV_CH)
            s_ref[pl.ds(base + PAD + t0, CONV_CH), :] = x_ref[pl.ds(r0 + t0, CONV_CH), :].astype(F32)
            return carry

        lax.fori_loop(0, n // CONV_CH, body, 0)
    s_ref[total - PAD:total, :] = zeros


def _conv_at(s_ref, w_ref, b_ref, base, t0):
    big = s_ref[pl.ds(base + t0, CONV_CH + 2 * PAD), :]
    return (w_ref[0:1, :] * big[PAD - 1:PAD - 1 + CONV_CH]
            + w_ref[1:2, :] * big[PAD:PAD + CONV_CH]
            + w_ref[2:3, :] * big[PAD + 1:PAD + 1 + CONV_CH] + b_ref[...])


def _conv_loop(segs, fn):
    bases, _ = _seg_bases(segs)
    for si, ((r0, n), base) in enumerate(zip(segs, bases)):
        def body(i, carry, r0=r0, base=base, si=si):
            t0 = pl.multiple_of(i * CONV_CH, CONV_CH)
            fn(si, r0, base, t0)
            return carry

        lax.fori_loop(0, n // CONV_CH, body, 0)


def _conv_silu_body(x_ref, w_ref, b_ref, o_ref, s_ref, *, segs):
    _conv_fill(s_ref, x_ref, segs)

    def fn(si, r0, base, t0):
        o_ref[pl.ds(r0 + t0, CONV_CH), :] = _silu(_conv_at(s_ref, w_ref, b_ref, base, t0)).astype(o_ref.dtype)

    _conv_loop(segs, fn)


def _conv_rope_body(x_ref, w_ref, b_ref, cos_ref, sin_ref, o_ref, s_ref, *, segs, tc):
    _conv_fill(s_ref, x_ref, segs)
    lane = lax.broadcasted_iota(jnp.int32, (CONV_CH, tc), 1)
    first = (lane % (ML_QK_DIM // 2)) < (ML_QK_DIM // 4)
    nf = ML_QK_DIM // 4

    def fn(si, r0, base, t0):
        y = _silu(_conv_at(s_ref, w_ref, b_ref, base, t0))
        partner = jnp.where(first, pltpu.roll(y, tc - nf, 1), pltpu.roll(y, nf, 1))
        rows = pl.ds(r0 + t0, CONV_CH)
        o_ref[rows, :] = (y * cos_ref[rows, :] + partner * sin_ref[rows, :]).astype(o_ref.dtype)

    _conv_loop(segs, fn)


def _conv_ffn_body(a_ref, x_ref, w_ref, b_ref, o_ref, s_ref, *, segs):
    _conv_fill(s_ref, x_ref, segs)

    def fn(si, r0, base, t0):
        rows = pl.ds(r0 + t0, CONV_CH)
        g = _silu(_conv_at(s_ref, w_ref, b_ref, base, t0))
        o_ref[rows, :] = (a_ref[rows, :].astype(F32) * g).astype(o_ref.dtype)

    _conv_loop(segs, fn)


def _conv_hyena_body(x0_ref, x1_ref, v_ref, w0_ref, b0_ref, w1_ref, b1_ref, w2_ref, b2_ref,
                     ox0_ref, ozl_ref, ozc_ref, s0_ref, s1_ref, s2_ref, *, segs):
    _conv_fill(s0_ref, x0_ref, segs)
    _conv_fill(s1_ref, x1_ref, segs)
    _conv_fill(s2_ref, v_ref, segs)
    outs = (ozl_ref, ozc_ref)

    def fn(si, r0, base, t0):
        ox0_ref[pl.ds(r0 + t0, CONV_CH), :] = _conv_at(s0_ref, w0_ref, b0_ref, base, t0).astype(ox0_ref.dtype)
        zin = _conv_at(s1_ref, w1_ref, b1_ref, base, t0) * _conv_at(s2_ref, w2_ref, b2_ref, base, t0)
        outs[si][pl.ds(t0, CONV_CH), :] = zin.astype(outs[si].dtype)

    _conv_loop(segs, fn)


def _conv_specs(t_in, tc, xcol_blk, wcol_blk):
    return [pl.BlockSpec((None, t_in, tc), lambda b, j: (b, 0, j + xcol_blk)),
            pl.BlockSpec((CONV_W, tc), lambda b, j: (0, j + wcol_blk)),
            pl.BlockSpec((1, tc), lambda b, j: (0, j + wcol_blk))]


def _conv_silu(x, w, b, *, xcol, wcol, width, segs, tc=256):
    nb, t_in, _ = x.shape
    _, total = _seg_bases(segs)
    return pl.pallas_call(
        functools.partial(_conv_silu_body, segs=segs),
        out_shape=jax.ShapeDtypeStruct((nb, t_in, width), BF16), grid=(nb, width // tc),
        in_specs=_conv_specs(t_in, tc, xcol // tc, wcol // tc),
        out_specs=pl.BlockSpec((None, t_in, tc), lambda b, j: (b, 0, j)),
        scratch_shapes=[pltpu.VMEM((total, tc), F32)],
        name="conv_silu", compiler_params=_cp(("parallel", "parallel")))(x, w, b)


def _conv_rope(x, w, b, cos, sin, *, width, segs, tc=256):
    nb, t_in, _ = x.shape
    _, total = _seg_bases(segs)
    tab = pl.BlockSpec((t_in, tc), lambda b, j: (0, 0))
    return pl.pallas_call(
        functools.partial(_conv_rope_body, segs=segs, tc=tc),
        out_shape=jax.ShapeDtypeStruct((nb, t_in, width), BF16), grid=(nb, width // tc),
        in_specs=_conv_specs(t_in, tc, 0, 0) + [tab, tab],
        out_specs=pl.BlockSpec((None, t_in, tc), lambda b, j: (b, 0, j)),
        scratch_shapes=[pltpu.VMEM((total, tc), F32)],
        name="conv_rope", compiler_params=_cp(("parallel", "parallel")))(x, w, b, cos, sin)


def _conv_ffn(up, w, b, *, segs, rows, tc=256):
    nb, t_in, two_ff = up.shape
    ff = two_ff // 2
    _, total = _seg_bases(segs)
    return pl.pallas_call(
        functools.partial(_conv_ffn_body, segs=segs),
        out_shape=jax.ShapeDtypeStruct((nb, rows, ff), BF16), grid=(nb, ff // tc),
        in_specs=[pl.BlockSpec((None, rows, tc), lambda b, j: (b, 0, j))] + _conv_specs(rows, tc, ff // tc, 0),
        out_specs=pl.BlockSpec((None, rows, tc), lambda b, j: (b, 0, j)),
        scratch_shapes=[pltpu.VMEM((total, tc), F32)],
        name="conv_ffn", compiler_params=_cp(("parallel", "parallel")))(up, up, w, b)


def _conv_hyena(pr, w, b, *, xcol, wcol, segs, tc=256):
    nb, t_in, _ = pr.shape
    (_, n_lat), (_, n_ctx) = segs
    _, total = _seg_bases(segs)
    wd = HY_WIDTH
    in_specs = [pl.BlockSpec((None, t_in, tc), functools.partial(lambda b, j, o: (b, 0, j + o), o=(xcol + m * wd) // tc))
                for m in range(3)]
    wargs = []
    for m in range(3):
        off = (wcol + m * wd) // tc
        in_specs.append(pl.BlockSpec((CONV_W, tc), functools.partial(lambda b, j, o: (0, j + o), o=off)))
        in_specs.append(pl.BlockSpec((1, tc), functools.partial(lambda b, j, o: (0, j + o), o=off)))
        wargs += [w, b]
    return pl.pallas_call(
        functools.partial(_conv_hyena_body, segs=segs),
        out_shape=(jax.ShapeDtypeStruct((nb, t_in, wd), BF16), jax.ShapeDtypeStruct((nb, n_lat, wd), BF16),
                   jax.ShapeDtypeStruct((nb, n_ctx, wd), BF16)),
        grid=(nb, wd // tc), in_specs=in_specs,
        out_specs=(pl.BlockSpec((None, t_in, tc), lambda b, j: (b, 0, j)),
                   pl.BlockSpec((None, n_lat, tc), lambda b, j: (b, 0, j)),
                   pl.BlockSpec((None, n_ctx, tc), lambda b, j: (b, 0, j))),
        scratch_shapes=[pltpu.VMEM((total, tc), F32)] * 3,
        name="conv_hyena", compiler_params=_cp(("parallel", "parallel")))(pr, pr, pr, *wargs)


def _scan_masks(reverse):
    q = SCAN_Q
    ii = lax.broadcasted_iota(jnp.int32, (q, q), 0)
    jj = lax.broadcasted_iota(jnp.int32, (q, q), 1)
    mask = (jj >= ii) if reverse else (jj <= ii)
    mask_t = (jj <= ii) if reverse else (jj >= ii)
    return mask, mask.astype(F32), mask_t.astype(F32)


def _chunk_map(reverse, n_chunks, n_lat_chunks):
    if reverse:
        return lambda k: n_chunks - 1 - k
    return lambda k: (k + n_lat_chunks) % n_chunks


def _ssd_body(*refs, reverse, final):
    (xs_ref, b_ref, c_ref, dtc_ref, dtr_ref, dbc_ref, dbr_ref, alc_ref, alr_ref) = refs[:9]
    if final:
        yf_ref, z_ref, dsk_ref, nw_ref, o_ref, h_ref = refs[9:]
    else:
        o_ref, h_ref = refs[9:]
    q, gw, hd = SCAN_Q, SSD_GW, SSD_HEAD_DIM

    @pl.when(pl.program_id(2) == 0)
    def _():
        h_ref[...] = jnp.zeros_like(h_ref)

    xs = xs_ref[...].astype(F32)
    bm = b_ref[...]
    cm = c_ref[...]
    dt_c = _softplus(dtc_ref[...] + dbc_ref[...])
    dt_r = _softplus(dtr_ref[...] + dbr_ref[...])
    da_c = dt_c * (-jnp.exp(alc_ref[...]))
    da_r = dt_r * (-jnp.exp(alr_ref[...]))
    mask, maskf, mask_tf = _scan_masks(reverse)
    acum_c = _dot(maskf, da_c, HI)
    acum_r = _dot(da_r, mask_tf, HI)
    atot = jnp.sum(da_c, axis=0, keepdims=True)

    er = lax.broadcasted_iota(jnp.int32, (SSD_HPG, gw), 0)
    ec = lax.broadcasted_iota(jnp.int32, (SSD_HPG, gw), 1)
    expand = (ec // hd == er).astype(F32)
    stack = jnp.concatenate([dt_c, jnp.exp(atot - acum_c), jnp.exp(acum_c),
                             jnp.broadcast_to(jnp.exp(atot), (PAD, SSD_HPG))], axis=0)
    wide = _dot(stack, expand, HI)
    dt_e, ws_e, eac_e, eat_e = wide[0:q], wide[q:2 * q], wide[2 * q:3 * q], wide[3 * q:3 * q + 1]

    xq = xs * dt_e
    h_prev = h_ref[...]
    cb = _dot_nt(cm, bm)
    y = eac_e * _dot(cm, h_prev.astype(BF16))
    lane = lax.broadcasted_iota(jnp.int32, (q, 2 * hd), 1)
    pieces = []
    for p in range(SSD_HPG // 2):
        ms = []
        for r in (2 * p, 2 * p + 1):
            seg = acum_c[:, r:r + 1] - acum_r[r:r + 1, :]
            ms.append((cb * jnp.where(mask, jnp.exp(seg), 0.0)).astype(BF16))
        xp = xq[:, 2 * hd * p:2 * hd * (p + 1)]
        rhs = jnp.concatenate([jnp.where(lane < hd, xp, 0.0), jnp.where(lane >= hd, xp, 0.0)], axis=0)
        pieces.append(_dot(jnp.concatenate(ms, axis=1), rhs.astype(BF16)))
    y = y + jnp.concatenate(pieces, axis=1)

    bt = bm.astype(F32).T.astype(BF16)
    h_ref[...] = eat_e * h_prev + _dot(bt, (xq * ws_e).astype(BF16))

    if final:
        ytot = y + yf_ref[...] + xs * dsk_ref[...]
        g = ytot * _silu(z_ref[...].astype(F32))
        o = g * lax.rsqrt(jnp.mean(g * g, axis=-1, keepdims=True) + EPS) * nw_ref[...]
        o_ref[...] = o.astype(o_ref.dtype)
    else:
        o_ref[...] = y


def _ssd(xbc, dt_c, dt_r, db_c, db_r, al_c, al_r, *, direction, n_lat, final_args=None, out_width=None):
    nb, t, _ = xbc.shape
    q, gw, hpg = SCAN_Q, SSD_GW, SSD_HPG
    nch = t // q
    reverse = direction == 1
    cmap = _chunk_map(reverse, nch, n_lat // q)
    nblk = SSD_STATE // LANES
    b0 = SSD_WIDTH // SSD_STATE
    c0 = b0 + SSD_GROUPS * nblk
    d = direction
    in_specs = [
        pl.BlockSpec((None, q, gw), lambda b, g, k: (b, cmap(k), g)),
        pl.BlockSpec((None, q, SSD_STATE), lambda b, g, k: (b, cmap(k), b0 + g)),
        pl.BlockSpec((None, q, SSD_STATE), lambda b, g, k: (b, cmap(k), c0 + g)),
        pl.BlockSpec((None, None, None, q, hpg), lambda b, g, k: (b, d, g, cmap(k), 0)),
        pl.BlockSpec((None, None, None, hpg, q), lambda b, g, k: (b, d, g, 0, cmap(k))),
        pl.BlockSpec((None, None, 1, hpg), lambda b, g, k: (d, g, 0, 0)),
        pl.BlockSpec((None, None, hpg, 1), lambda b, g, k: (d, g, 0, 0)),
        pl.BlockSpec((None, None, 1, hpg), lambda b, g, k: (d, g, 0, 0)),
        pl.BlockSpec((None, None, hpg, 1), lambda b, g, k: (d, g, 0, 0)),
    ]
    args = [xbc, xbc, xbc, dt_c, dt_r, db_c, db_r, al_c, al_r]
    final = final_args is not None
    if final:
        yf, pr, dskip, norm_w = final_args
        in_specs += [pl.BlockSpec((None, q, gw), lambda b, g, k: (b, cmap(k), g)),
                     pl.BlockSpec((None, q, gw), lambda b, g, k: (b, cmap(k), g)),
                     pl.BlockSpec((1, gw), lambda b, g, k: (0, g)),
                     pl.BlockSpec((1, gw), lambda b, g, k: (0, g))]
        args += [yf, pr, dskip, norm_w]
        out_shape = jax.ShapeDtypeStruct((nb, t, out_width), BF16)
    else:
        out_shape = jax.ShapeDtypeStruct((nb, t, SSD_WIDTH), F32)
    return pl.pallas_call(
        functools.partial(_ssd_body, reverse=reverse, final=final),
        out_shape=out_shape, grid=(nb, SSD_GROUPS, nch), in_specs=in_specs,
        out_specs=pl.BlockSpec((None, q, gw), lambda b, g, k: (b, cmap(k), g)),
        scratch_shapes=[pltpu.VMEM((SSD_STATE, gw), F32)],
        name="ssd_bwd" if reverse else "ssd_fwd",
        compiler_params=_cp(("parallel", "parallel", "arbitrary")))(*args)


def _hyfilt_body(f_ref, w1_ref, b1_ref, w2_ref, b2_ref, w3_ref, fr_ref, dl_ref, o_ref, *, tl):
    f = f_ref[...]
    h = jnp.sin(fr_ref[0:1, :] * (_dot(f, w1_ref[...], HI) + b1_ref[...]))
    h = jnp.sin(fr_ref[1:2, :] * (_dot(h, w2_ref[...], HI) + b2_ref[...]))
    h = _dot(h, w3_ref[...], HI)
    dec = jnp.exp(-f[:, 0:1] * dl_ref[...])
    row = pl.program_id(0) * tl + lax.broadcasted_iota(jnp.int32, (tl, 1), 0)
    wd = HY_WIDTH
    o_ref[:, 0:wd] = (h[:, 0:wd] * dec).astype(o_ref.dtype)
    o_ref[:, wd:2 * wd] = jnp.where(row == 0, 0.0, h[:, wd:2 * wd] * dec).astype(o_ref.dtype)


def _hyena_filters(length, w1, b1, w2, b2, w3, freq, *, tl=256):
    p = LANES
    t = jnp.linspace(0.0, 1.0, length, dtype=F32)[:, None]
    w = 2.0 * math.pi * jnp.arange(length, dtype=F32)[:, None] / length
    f = jnp.linspace(1e-4, HY_BANDS - 1, HY_BANDS, dtype=F32)[None, :]
    feats = jnp.concatenate([t, jnp.cos(f * w), -jnp.sin(f * w)], axis=-1)
    feats = jnp.pad(feats, ((0, 0), (0, p - HY_EMB)))
    po = p - HY_ORDER
    w1p = jnp.pad(w1, ((0, p - HY_EMB), (0, po)))
    w2p = jnp.pad(w2, ((0, po), (0, po)))
    w3p = jnp.pad(w3, ((0, po), (0, 0)))
    b1p = jnp.pad(b1[None, :], ((0, 0), (0, po)))
    b2p = jnp.pad(b2[None, :], ((0, 0), (0, po)))
    frp = jnp.pad(freq, ((0, 0), (0, po)))
    deltas = jnp.abs(jnp.linspace(math.log(HY_TARGET) / HY_FAST, math.log(HY_TARGET) / HY_SLOW, HY_WIDTH,
                                  dtype=F32))[None, :]
    full = lambda shape: pl.BlockSpec(shape, lambda i: (0,) * len(shape))
    out = pl.pallas_call(
        functools.partial(_hyfilt_body, tl=tl),
        out_shape=jax.ShapeDtypeStruct((length, 2 * HY_WIDTH), BF16), grid=(length // tl,),
        in_specs=[pl.BlockSpec((tl, p), lambda i: (i, 0)), full((p, p)), full((1, p)), full((p, p)), full((1, p)),
                  full((p, 2 * HY_WIDTH)), full((2, p)), full((1, HY_WIDTH))],
        out_specs=pl.BlockSpec((tl, 2 * HY_WIDTH), lambda i: (i, 0)),
        name="hyena_filters", compiler_params=_cp(("parallel",)))(feats, w1p, b1p, w2p, b2p, w3p, frp, deltas)
    return out[None]


def _dft_matrices(length):
    n = 2 * length
    k = jnp.arange(length, dtype=jnp.int32)[:, None]
    t = jnp.arange(length, dtype=jnp.int32)[None, :]
    ang = ((k * t) % n).astype(F32) * (2.0 * math.pi / n)
    c, s = jnp.cos(ang), jnp.sin(ang)
    alt_t = jnp.where(t % 2 == 0, 1.0, -1.0).astype(F32)
    fwd = jnp.concatenate([c, jnp.where(k == 0, alt_t, -s)], axis=0)
    alt_r = jnp.where(k % 2 == 0, 1.0, -1.0).astype(F32)
    inv_re = jnp.where(t == 0, 1.0 / n, (2.0 / n) * c)
    inv_im = jnp.where(t == 0, alt_r / n, (-2.0 / n) * s)
    inv = jnp.concatenate([inv_re, inv_im], axis=1)
    return fwd.astype(BF16)[None], inv.astype(BF16)[None]


def _specmul_body(z_ref, kf_ref, kb_ref, o_ref, *, tr):
    zr, zi = z_ref[0], z_ref[1]
    kr = kf_ref[0] + kb_ref[0]
    ki = kf_ref[1] - kb_ref[1]
    row = pl.program_id(1) * tr + lax.broadcasted_iota(jnp.int32, (tr, 1), 0)
    first = row == 0
    yr = jnp.where(first, zr * kr, zr * kr - zi * ki)
    yi = jnp.where(first, zi * (kf_ref[1] + kb_ref[1]), zr * ki + zi * kr)
    o_ref[0] = yr.astype(o_ref.dtype)
    o_ref[1] = yi.astype(o_ref.dtype)


def _specmul(z, ksp, *, tr, tc=512):
    nb, l2, c = z.shape
    ln = l2 // 2
    z4 = z.reshape(nb, 2, ln, c)
    k3 = ksp.reshape(2, ln, 2 * c)
    out = pl.pallas_call(
        functools.partial(_specmul_body, tr=tr),
        out_shape=jax.ShapeDtypeStruct((nb, 2, ln, c), BF16), grid=(nb, ln // tr, c // tc),
        in_specs=[pl.BlockSpec((None, 2, tr, tc), lambda b, i, j: (b, 0, i, j)),
                  pl.BlockSpec((2, tr, tc), lambda b, i, j: (0, i, j)),
                  pl.BlockSpec((2, tr, tc), lambda b, i, j: (0, i, j + c // tc))],
        out_specs=pl.BlockSpec((None, 2, tr, tc), lambda b, i, j: (b, 0, i, j)),
        name="specmul", compiler_params=_cp(("parallel", "parallel", "parallel")))(z4, k3, k3)
    return out.reshape(nb, l2, c)


def _hyena_out_epilogue(acc, row_blk, x0_ref, zin_ref, db_ref):
    del row_blk
    zin = zin_ref[...].astype(F32)
    return x0_ref[...].astype(F32) * (acc + zin * db_ref[...])


def _hyena_longconv(zin, x0, hfb, dbias, ybuf, *, row0, tn=1024):
    nb, ln, c = zin.shape
    tm_f, tm_i = min(1024, 2 * ln), min(1024, ln)
    tk_f, tk_i = min(2048, ln), min(2048, 2 * ln)
    assert row0 % tm_i == 0
    fwd, inv = _dft_matrices(ln)
    ksp = _matmul(fwd, hfb, tm=tm_f, tn=tn, tk=tk_f, out_dtype=F32, name="hy_filter_dft")
    z = _matmul(fwd, zin, tm=tm_f, tn=tn, tk=tk_f, out_dtype=F32, name="hy_dft")
    y = _specmul(z, ksp, tr=min(ln, 512))
    rb = row0 // tm_i
    extra = [(x0, pl.BlockSpec((None, tm_i, tn), lambda j, b, i, k: (b, i + rb, j))),
             (zin, pl.BlockSpec((None, tm_i, tn), lambda j, b, i, k: (b, i, j))),
             (dbias, pl.BlockSpec((1, tn), lambda j, b, i, k: (0, j)))]
    return _matmul(inv, y, tm=tm_i, tn=tn, tk=tk_i, out_dtype=BF16, extra=extra, epilogue=_hyena_out_epilogue,
                   out_buf=ybuf, out_row_blk=rb, out_col_blk=SSD_WIDTH // tn, name="hy_idft")


def _mlstm_body(*refs, reverse, final):
    q_ref, k_ref, v_ref, gc_ref, gr_ref = refs[:5]
    if final:
        hf_ref, og_ref, nw_ref, o_ref, ct_ref, n_ref, m_ref = refs[5:]
    else:
        o_ref, ct_ref, n_ref, m_ref = refs[5:]
    nh, dk, dv = ML_HEADS, ML_QK_DIM, ML_V_DIM
    scale = dk ** -0.5

    @pl.when(pl.program_id(1) == 0)
    def _():
        ct_ref[...] = jnp.zeros_like(ct_ref)
        n_ref[...] = jnp.zeros_like(n_ref)
        m_ref[...] = jnp.zeros_like(m_ref)

    gc, gr = gc_ref[...], gr_ref[...]
    li_c, lf_c = gc[:, 0:nh], _log_sigmoid(gc[:, nh:2 * nh])
    li_r, lf_r = gr[0:nh, :], _log_sigmoid(gr[nh:2 * nh, :])
    mask, maskf, mask_tf = _scan_masks(reverse)
    bc_c = _dot(maskf, lf_c, HI)
    bc_r = _dot(lf_r, mask_tf, HI)
    btot_c = jnp.sum(lf_c, axis=0, keepdims=True)

    for h in range(nh):
        qh = q_ref[:, dk * h:dk * (h + 1)]
        kh = k_ref[:, dk * h:dk * (h + 1)]
        vh = v_ref[:, dv * h:dv * (h + 1)]
        m_s = m_ref[h:h + 1, 0:1]
        b_c, b_r = bc_c[:, h:h + 1], bc_r[h:h + 1, :]
        i_c, i_r = li_c[:, h:h + 1], li_r[h:h + 1, :]
        btot = btot_c[:, h:h + 1]
        dmat = jnp.where(mask, b_c - b_r + i_r, -jnp.inf)
        inter = b_c + m_s
        m_t = jnp.maximum(inter, jnp.max(dmat, axis=1, keepdims=True))
        s = _dot_nt(qh, kh) * scale * jnp.exp(dmat - m_t)
        dec = jnp.exp(inter - m_t)
        ct = ct_ref[h]
        n_s = n_ref[h:h + 1, :]
        num = _dot(s.astype(BF16), vh) + (dec * scale) * _dot(qh, ct.astype(BF16))
        qn = jnp.sum(qh.astype(F32) * n_s, axis=1, keepdims=True) * scale
        den = jnp.sum(s, axis=1, keepdims=True) + dec * qn
        hh = num / jnp.maximum(jnp.abs(den), jnp.exp(-m_t))

        wst_c = btot - b_c + i_c
        wst_r = btot - b_r + i_r
        m_new = jnp.maximum(btot + m_s, jnp.max(wst_r, axis=1, keepdims=True))
        dstate = jnp.exp(btot + m_s - m_new)
        kw = kh.astype(F32) * jnp.exp(wst_c - m_new)
        ct_ref[h] = dstate * ct + _dot(kw.T.astype(BF16), vh)
        n_ref[h:h + 1, :] = dstate * n_s + jnp.sum(kw, axis=0, keepdims=True)
        m_ref[h:h + 1, :] = jnp.broadcast_to(m_new, (1, LANES))

        cols = slice(dv * h, dv * (h + 1))
        if final:
            hs = hh + hf_ref[:, cols]
            hn = hs * lax.rsqrt(jnp.mean(hs * hs, axis=-1, keepdims=True) + EPS) * nw_ref[:, cols]
            o_ref[:, cols] = (hn * _sigmoid(og_ref[:, cols].astype(F32))).astype(o_ref.dtype)
        else:
            o_ref[:, cols] = hh


def _mlstm(qk, pr, g_c, g_r, *, direction, n_lat, final_args=None, out_width=None):
    nb, t, _ = qk.shape
    q = SCAN_Q
    nch = t // q
    reverse = direction == 1
    cmap = _chunk_map(reverse, nch, n_lat // q)
    d = direction
    in_specs = [
        pl.BlockSpec((None, q, ML_QK_WIDTH), lambda b, k: (b, cmap(k), 0)),
        pl.BlockSpec((None, q, ML_QK_WIDTH), lambda b, k: (b, cmap(k), 1)),
        pl.BlockSpec((None, q, ML_WIDTH), lambda b, k: (b, cmap(k), 2 * ML_QK_WIDTH // ML_WIDTH)),
        pl.BlockSpec((None, None, q, 2 * ML_HEADS), lambda b, k: (b, d, cmap(k), 0)),
        pl.BlockSpec((None, None, 2 * ML_HEADS, q), lambda b, k: (b, d, 0, cmap(k))),
    ]
    args = [qk, qk, pr, g_c, g_r]
    final = final_args is not None
    if final:
        hf, norm_w = final_args
        in_specs += [pl.BlockSpec((None, q, ML_WIDTH), lambda b, k: (b, cmap(k), 0)),
                     pl.BlockSpec((None, q, ML_WIDTH), lambda b, k: (b, cmap(k), 2 * ML_QK_WIDTH // ML_WIDTH + 1)),
                     pl.BlockSpec((1, ML_WIDTH), lambda b, k: (0, 0))]
        args += [hf, pr, norm_w]
        out_shape = jax.ShapeDtypeStruct((nb, t, out_width), BF16)
    else:
        out_shape = jax.ShapeDtypeStruct((nb, t, ML_WIDTH), F32)
    return pl.pallas_call(
        functools.partial(_mlstm_body, reverse=reverse, final=final),
        out_shape=out_shape, grid=(nb, nch), in_specs=in_specs,
        out_specs=pl.BlockSpec((None, q, ML_WIDTH), lambda b, k: (b, cmap(k), 0)),
        scratch_shapes=[pltpu.VMEM((ML_HEADS, ML_QK_DIM, ML_V_DIM), F32), pltpu.VMEM((ML_HEADS, ML_QK_DIM), F32),
                        pltpu.VMEM((ML_HEADS, LANES), F32)],
        name="mlstm_bwd" if reverse else "mlstm_fwd",
        compiler_params=_cp(("parallel", "arbitrary")))(*args)


def _na_window_start(j, rows):
    return jnp.clip(j * NA_QROWS - NA_ROWS // 2, 0, rows - NA_KROWS)


def _na_body(q_ref, k_ref, v_ref, tab_ref, qw_ref, kw_ref, buf_ref, o_ref, kn_ref, *, n_lat, n_ctx):
    del buf_ref
    dh = NA_HEAD_DIM
    scale = dh ** -0.5
    rows = n_lat // GRID_W
    j = pl.program_id(2)
    nq, nk = NA_QROWS * GRID_W, NA_KROWS * GRID_W

    @pl.when(j == 0)
    def _():
        def body(i, carry):
            r = pl.ds(pl.multiple_of(i * LANES, LANES), LANES)
            kk = k_ref[r, :].astype(F32)
            kn = kk * lax.rsqrt(jnp.mean(kk * kk, axis=-1, keepdims=True) + EPS) * kw_ref[...]
            kn_ref[r, :] = kn.astype(kn_ref.dtype)
            return carry

        lax.fori_loop(0, (n_lat + n_ctx) // LANES, body, 0)

    qq = q_ref[...].astype(F32)
    qn = (qq * lax.rsqrt(jnp.mean(qq * qq, axis=-1, keepdims=True) + EPS) * qw_ref[...]).astype(BF16)
    start = _na_window_start(j, rows)
    win = pl.ds(pl.multiple_of(start * GRID_W, GRID_W), nk)
    s = _dot_nt(qn, kn_ref[win, :]) * scale
    pieces = []
    for qr in range(NA_QROWS):
        d0 = start - (j * NA_QROWS + qr) + NA_ROWS - 1 + NA_QROWS
        pieces.append(s[qr * GRID_W:(qr + 1) * GRID_W] + tab_ref[d0])
    s = jnp.concatenate(pieces, axis=0)
    ctx = pl.ds(n_lat, n_ctx)
    sc = _dot_nt(qn, kn_ref[ctx, :]) * scale
    m = jnp.maximum(jnp.max(s, axis=1, keepdims=True), jnp.max(sc, axis=1, keepdims=True))
    p, pc = jnp.exp(s - m), jnp.exp(sc - m)
    denom = jnp.sum(p, axis=1, keepdims=True) + jnp.sum(pc, axis=1, keepdims=True)
    o = _dot(p.astype(BF16), v_ref[win, :]) + _dot(pc.astype(BF16), v_ref[ctx, :])
    o_ref[...] = (o / denom).astype(o_ref.dtype)


def _na_bias_table(rpb, rows):
    nd = 2 * NA_QROWS
    valid = np.zeros((nd, NA_KROWS), bool)
    seen = np.zeros((nd,), bool)
    for j in range(rows // NA_QROWS):
        start = int(np.clip(j * NA_QROWS - NA_ROWS // 2, 0, rows - NA_KROWS))
        for qr in range(NA_QROWS):
            r = j * NA_QROWS + qr
            rs = int(np.clip(r - NA_ROWS // 2, 0, rows - NA_ROWS))
            d = start - r + NA_ROWS - 1 + NA_QROWS
            v = np.array([(rs <= start + i < rs + NA_ROWS) for i in range(NA_KROWS)])
            assert 0 <= d < nd and (not seen[d] or (valid[d] == v).all())
            valid[d], seen[d] = v, True
    col = np.arange(GRID_W)
    cs = np.clip(col - NA_COLS // 2, 0, GRID_W - NA_COLS)
    in_win = (col[None, :] >= cs[:, None]) & (col[None, :] < cs[:, None] + NA_COLS)
    dc = np.clip(col[None, :] - col[:, None] + NA_COLS - 1, 0, 2 * NA_COLS - 2)
    dr = (np.arange(nd)[:, None] - NA_QROWS) + np.arange(NA_KROWS)[None, :]
    ok = valid[:, None, :, None] & in_win[None, :, None, :] & (dr >= 0)[:, None, :, None] \
        & (dr <= 2 * NA_ROWS - 2)[:, None, :, None]
    dr_idx = np.broadcast_to(np.clip(dr, 0, 2 * NA_ROWS - 2)[:, None, :, None], ok.shape)
    dc_idx = np.broadcast_to(dc[None, :, None, :], ok.shape)
    vals = rpb[:, dr_idx, dc_idx]
    tab = jnp.where(ok[None], vals, -jnp.inf).astype(F32)
    return tab.reshape(rpb.shape[0], nd, GRID_W, NA_KROWS * GRID_W)


def _na(pr, tab, q_w, k_w, ybuf, *, n_lat, n_ctx, qcol, out_col):
    nb, t, _ = pr.shape
    dh, nh = NA_HEAD_DIM, NA_HEADS
    nq = NA_QROWS * GRID_W
    qb, ob = qcol // dh, out_col // dh
    nd = tab.shape[1]
    return pl.pallas_call(
        functools.partial(_na_body, n_lat=n_lat, n_ctx=n_ctx),
        out_shape=jax.ShapeDtypeStruct(ybuf.shape, ybuf.dtype), grid=(nb, nh, n_lat // nq),
        in_specs=[pl.BlockSpec((None, nq, dh), lambda b, h, j: (b, j, qb + h)),
                  pl.BlockSpec((None, t, dh), lambda b, h, j: (b, 0, qb + nh + h)),
                  pl.BlockSpec((None, t, dh), lambda b, h, j: (b, 0, qb + 2 * nh + h)),
                  pl.BlockSpec((None, nd, GRID_W, NA_KROWS * GRID_W), lambda b, h, j: (h, 0, 0, 0)),
                  pl.BlockSpec((1, dh), lambda b, h, j: (0, 0)),
                  pl.BlockSpec((1, dh), lambda b, h, j: (0, 0)),
                  pl.BlockSpec(memory_space=pl.ANY)],
        out_specs=pl.BlockSpec((None, nq, dh), lambda b, h, j: (b, j, ob + h)),
        scratch_shapes=[pltpu.VMEM((t, dh), BF16)],
        input_output_aliases={6: 0}, name="na",
        compiler_params=_cp(("parallel", "parallel", "arbitrary")))(pr, pr, pr, tab, q_w, k_w, ybuf)


def _rope_tables(n_lat, n_ctx, width):
    dh = ML_QK_DIM
    nf = dh // 4
    t = jnp.arange(n_lat)
    inv = ROPE_THETA ** (-jnp.arange(nf, dtype=F32) / nf)
    lane = np.arange(dh)
    pos = jnp.where((lane < dh // 2)[None, :], (t // GRID_W)[:, None], (t % GRID_W)[:, None]).astype(F32)
    ang = pos * inv[lane % nf][None, :]
    sign = np.where((lane % (dh // 2)) < nf, -1.0, 1.0).astype(np.float32)
    cos = jnp.concatenate([jnp.cos(ang), jnp.ones((n_ctx, dh), F32)], axis=0)
    sin = jnp.concatenate([jnp.sin(ang) * sign[None, :], jnp.zeros((n_ctx, dh), F32)], axis=0)
    reps = width // dh
    return jnp.tile(cos, (1, reps)), jnp.tile(sin, (1, reps))


def _ada_mod(c, c_ctx, w, b):
    nb, d = c.shape
    rows = 16
    cv = jnp.concatenate([c, c_ctx[None, :], jnp.zeros((rows - nb - 1, d), F32)], axis=0)
    a = (cv * _sigmoid(cv)).astype(BF16)[None]
    out = _matmul(a, w[None], tm=rows, tn=1024, tk=d, out_dtype=F32, name="ada_mod")[0]
    out = out[:nb + 1] + b[None, :]
    return out.reshape(nb + 1, 6, d).transpose(1, 0, 2)


def _lat_ctx(m, nb):
    ctx = jnp.broadcast_to(m[nb][None, :], (nb, m.shape[1]))
    return jnp.stack([m[:nb], ctx], axis=1)[:, :, None, :]


def _pad_cols(w, n):
    return jnp.pad(w, ((0, 0), (0, n - w.shape[1])))


def _col_row_forms(g, groups):
    nb, t, _ = g.shape
    g5 = g.reshape(nb, t, 2, groups, -1)
    return g5.transpose(0, 2, 3, 1, 4), g5.transpose(0, 2, 3, 4, 1)


ROW_TILES = (1088, 1024, 512, 256)


def _row_tile(rows):
    return next(tm for tm in ROW_TILES if rows % tm == 0)


def _conv_ffn_block(x, mods, norm_w, w_up, conv_w, conv_b, w_down, *, rows, n_lat, segs):
    nb = x.shape[0]
    tm = _row_tile(rows)
    mod = jnp.stack([_lat_ctx(mods[3], nb), _lat_ctx(mods[4], nb)], axis=1)
    u = _rmsmod(x, norm_w[None, :], mod, tm=tm, rows=rows, n_lat=n_lat)
    up = _matmul(u, w_up.astype(BF16)[None], tm=tm, tn=1024, tk=D_MODEL, out_dtype=BF16, name="ffn_up")
    hid = _conv_ffn(up, conv_w, conv_b[None, :], segs=segs, rows=rows)
    return _matmul_gated_residual(hid, w_down.astype(BF16)[None], x, _lat_ctx(mods[5], nb), tm=tm, tn=1024,
                                  tk=D_FF // 4, rows=rows, n_lat=n_lat, name="ffn_down")


def _even_layer(x, mods, norm_w, w_in, conv_w, conv_b, dt_bias, a_log, d_skip, ssd_norm_w,
                hy_w1, hy_b1, hy_w2, hy_b2, hy_w3, hy_freq, hy_bias, w_out, *, n_lat, n_ctx):
    nb, t, d = x.shape
    segs = ((0, n_lat), (n_lat, n_ctx))
    tm = _row_tile(t)
    mod = jnp.stack([_lat_ctx(mods[0], nb), _lat_ctx(mods[1], nb)], axis=1)
    u = _rmsmod(x, norm_w[0][None, :], mod, tm=tm, rows=t, n_lat=n_lat)
    n_main = SSD_WIDTH + EV_CONV_CH
    wb = w_in.astype(BF16)
    pr = _matmul(u, wb[None, :, :n_main], tm=tm, tn=1024, tk=d, out_dtype=BF16, name="ev_in")
    dtp = _matmul(u, _pad_cols(wb[:, n_main:], LANES)[None], tm=tm, tn=LANES, tk=d, out_dtype=F32,
                  name="ev_in_dt")[:, :, :2 * SSD_HEADS]
    cb = conv_b[None, :]
    xbc = _conv_silu(pr, conv_w, cb, xcol=SSD_WIDTH, wcol=0, width=SSD_XBC, segs=segs)
    x0, zin_l, zin_c = _conv_hyena(pr, conv_w, cb, xcol=SSD_WIDTH + SSD_XBC, wcol=SSD_XBC, segs=segs)

    dt_c, dt_r = _col_row_forms(dtp, SSD_GROUPS)
    db = dt_bias.reshape(2, SSD_GROUPS, 1, SSD_HPG)
    al = a_log.reshape(2, SSD_GROUPS, 1, SSD_HPG)
    scan_args = (xbc, dt_c, dt_r, db, db.transpose(0, 1, 3, 2), al, al.transpose(0, 1, 3, 2))
    yf = _ssd(*scan_args, direction=0, n_lat=n_lat)
    dsk = jnp.repeat(d_skip, SSD_HEAD_DIM)[None, :]
    ybuf = _ssd(*scan_args, direction=1, n_lat=n_lat, final_args=(yf, pr, dsk, ssd_norm_w[None, :]),
                out_width=SSD_WIDTH + HY_WIDTH)

    hyb = hy_bias[None, :]
    filt = functools.partial(_hyena_filters, w1=hy_w1, b1=hy_b1, w2=hy_w2, b2=hy_b2, w3=hy_w3, freq=hy_freq)
    ybuf = _hyena_longconv(zin_l, x0, filt(n_lat), hyb, ybuf, row0=0)
    ybuf = _hyena_longconv(zin_c, x0, filt(n_ctx), hyb, ybuf, row0=n_lat)
    return _matmul_gated_residual(ybuf, w_out.astype(BF16)[None], x, _lat_ctx(mods[2], nb), tm=tm, tn=1024,
                                  tk=2048, rows=t, n_lat=n_lat, name="ev_out")


def _odd_layer_latent(x, mods, norm_w, w_in, conv_w, conv_b, gate_b, ml_norm_w, q_norm_w, k_norm_w, rpb, w_out,
                      *, n_lat, n_ctx):
    nb, t, d = x.shape
    segs = ((0, n_lat), (n_lat, n_ctx))
    tm = _row_tile(t)
    mod = jnp.stack([_lat_ctx(mods[0], nb), _lat_ctx(mods[1], nb)], axis=1)
    u = _rmsmod(x, norm_w[0][None, :], mod, tm=tm, rows=t, n_lat=n_lat)
    o3 = 2 * ML_QK_WIDTH + 2 * ML_WIDTH
    o4 = o3 + 4 * ML_HEADS
    wb = w_in.astype(BF16)
    w_main = jnp.concatenate([wb[:, :o3], wb[:, o4:]], axis=1)
    pr = _matmul(u, w_main[None], tm=tm, tn=1024, tk=d, out_dtype=BF16, name="od_in")
    gates = _matmul(u, _pad_cols(wb[:, o3:o4], LANES)[None], tm=tm, tn=LANES, tk=d, out_dtype=F32,
                    name="od_in_gates")[:, :, :4 * ML_HEADS] + gate_b.reshape(-1)
    cos, sin = _rope_tables(n_lat, n_ctx, 256)
    qk = _conv_rope(pr, conv_w, conv_b[None, :], cos, sin, width=2 * ML_QK_WIDTH, segs=segs)
    g5 = gates.reshape(nb, t, 2, 2 * ML_HEADS)
    g_c, g_r = g5.transpose(0, 2, 1, 3), g5.transpose(0, 2, 3, 1)
    hf = _mlstm(qk, pr, g_c, g_r, direction=0, n_lat=n_lat)
    ybuf = _mlstm(qk, pr, g_c, g_r, direction=1, n_lat=n_lat, final_args=(hf, ml_norm_w[None, :]),
                  out_width=ML_WIDTH + NA_WIDTH)
    tab = _na_bias_table(rpb, n_lat // GRID_W)
    ybuf = _na(pr, tab, q_norm_w[None, :], k_norm_w[None, :], ybuf, n_lat=n_lat, n_ctx=n_ctx, qcol=o3,
               out_col=ML_WIDTH)
    return _matmul_gated_residual(ybuf, w_out.astype(BF16)[None], x, _lat_ctx(mods[2], nb), tm=_row_tile(n_lat),
                                  tn=1024, tk=2048, rows=n_lat, n_lat=n_lat, name="od_out")


def kernel(x, c, ctx, c_ctx, ada_w, ada_b, norm_w, ev_w_in, ev_conv_w, ev_conv_b, ssd_dt_bias, ssd_a_log, ssd_d,
           ssd_norm_w, hy_w1, hy_b1, hy_w2, hy_b2, hy_w3, hy_freq, hy_bias, ev_w_out, od_w_in, ml_conv_w, ml_conv_b,
           ml_gate_b, ml_norm_w, na_q_norm_w, na_k_norm_w, na_rpb, od_w_out, ffn_w_up, ffn_conv_w, ffn_conv_b,
           ffn_w_down):
    depth = ada_w.shape[0]
    assert depth == 2, "layer schedule below is written for one even and one odd layer"
    n_lat, n_ctx = x.shape[1], ctx.shape[1]
    t = n_lat + n_ctx
    both = ((0, n_lat), (n_lat, n_ctx))
    xs = jnp.concatenate([x, ctx], axis=1)

    mods = _ada_mod(c, c_ctx, ada_w[0], ada_b[0])
    xs = _even_layer(xs, mods, norm_w[0], ev_w_in[0], ev_conv_w[0], ev_conv_b[0], ssd_dt_bias[0], ssd_a_log[0],
                     ssd_d[0], ssd_norm_w[0], hy_w1[0], hy_b1[0], hy_w2[0], hy_b2[0], hy_w3[0], hy_freq[0],
                     hy_bias[0], ev_w_out[0], n_lat=n_lat, n_ctx=n_ctx)
    xs = _conv_ffn_block(xs, mods, norm_w[0, 1], ffn_w_up[0], ffn_conv_w[0], ffn_conv_b[0], ffn_w_down[0],
                         rows=t, n_lat=n_lat, segs=both)

    mods = _ada_mod(c, c_ctx, ada_w[1], ada_b[1])
    xl = _odd_layer_latent(xs, mods, norm_w[1], od_w_in[0], ml_conv_w[0], ml_conv_b[0], ml_gate_b[0], ml_norm_w[0],
                           na_q_norm_w[0], na_k_norm_w[0], na_rpb[0], od_w_out[0], n_lat=n_lat, n_ctx=n_ctx)
    return _conv_ffn_block(xl, mods, norm_w[1, 1], ffn_w_up[1], ffn_conv_w[1], ffn_conv_b[1], ffn_w_down[1],
                           rows=n_lat, n_lat=n_lat, segs=((0, n_lat),))
```

```python
import functools
import math

import numpy as np
import jax
import jax.numpy as jnp
from jax import lax
from jax.experimental import pallas as pl
from jax.experimental.pallas import tpu as pltpu

F32 = jnp.float32
BF16 = jnp.bfloat16
HI = lax.Precision.HIGHEST

D_MODEL = 2048
SEQ = 4096
CTX_LEN = 256
GRID_W = 64
EPS = 1e-6
CONV_W = 3
SSD_WIDTH = D_MODEL
SSD_HEAD_DIM = 64
SSD_HEADS = SSD_WIDTH // SSD_HEAD_DIM
SSD_STATE = 128
SSD_GROUPS = 4
SSD_HPG = SSD_HEADS // SSD_GROUPS
SSD_GW = SSD_WIDTH // SSD_GROUPS
SSD_XBC = SSD_WIDTH + 2 * SSD_GROUPS * SSD_STATE
HY_WIDTH = D_MODEL
HY_EMB = 33
HY_BANDS = (HY_EMB - 1) // 2
HY_ORDER = 64
HY_FAST = 0.3
HY_SLOW = 1.5
HY_TARGET = 1e-2
ML_HEADS = 8
ML_WIDTH = D_MODEL
ML_V_DIM = ML_WIDTH // ML_HEADS
ML_QK_DIM = ML_V_DIM // 2
ML_QK_WIDTH = ML_HEADS * ML_QK_DIM
ROPE_THETA = 10000.0
NA_WIDTH = D_MODEL
NA_HEAD_DIM = 128
NA_HEADS = NA_WIDTH // NA_HEAD_DIM
NA_ROWS = 8
NA_COLS = 16
NA_QROWS = 8
NA_KROWS = 16
D_FF = 256 * ((8 * D_MODEL // 3 + 255) // 256)
EV_CONV_CH = SSD_XBC + 3 * HY_WIDTH
SCAN_Q = 128
LANES = 128
PAD = 8
VMEM_LIMIT = 48 * 1024 * 1024


def _cp(sem, vmem=VMEM_LIMIT):
    return pltpu.CompilerParams(dimension_semantics=sem, vmem_limit_bytes=vmem)


def _sigmoid(x):
    return 1.0 / (1.0 + jnp.exp(-x))


def _silu(x):
    return x * _sigmoid(x)


def _softplus(x):
    return jnp.maximum(x, 0.0) + jnp.log(1.0 + jnp.exp(-jnp.abs(x)))


def _log_sigmoid(x):
    return -_softplus(-x)


def _dot(a, b, precision=None):
    return jnp.dot(a, b, preferred_element_type=F32, precision=precision)


def _dot_nt(a, b):
    return lax.dot_general(a, b, (((1,), (1,)), ((), ())), preferred_element_type=F32)


def _mm_body(*refs, nk, n_extra, epilogue, has_buf):
    a_ref, w_ref = refs[0], refs[1]
    extra = refs[2:2 + n_extra]
    pos = 2 + n_extra + (1 if has_buf else 0)
    o_ref = refs[pos]
    part = _dot(a_ref[...].astype(BF16), w_ref[...].astype(BF16))
    row_blk = pl.program_id(2)

    def finish(acc):
        val = acc if epilogue is None else epilogue(acc, row_blk, *extra)
        o_ref[...] = val.astype(o_ref.dtype)

    if nk == 1:
        finish(part)
    else:
        acc_ref = refs[pos + 1]
        k = pl.program_id(3)

        @pl.when(k == 0)
        def _():
            acc_ref[...] = part

        @pl.when(k > 0)
        def _():
            acc_ref[...] += part

        @pl.when(k == nk - 1)
        def _():
            finish(acc_ref[...])


def _matmul(a, w, *, tm, tn, tk, out_dtype, rows=None, extra=(), epilogue=None,
            out_buf=None, out_row_blk=0, out_col_blk=0, name="mm"):
    ba, ta, kdim = a.shape
    bw, kdim2, n = w.shape
    assert kdim == kdim2
    nb = max(ba, bw)
    rows = ta if rows is None else rows
    assert rows % tm == 0 and n % tn == 0 and kdim % tk == 0
    ni, nj, nk = rows // tm, n // tn, kdim // tk
    a_map = (lambda j, b, i, k: (b, i, k)) if ba > 1 else (lambda j, b, i, k: (0, i, k))
    w_map = (lambda j, b, i, k: (b, k, j)) if bw > 1 else (lambda j, b, i, k: (0, k, j))
    in_specs = [pl.BlockSpec((None, tm, tk), a_map), pl.BlockSpec((None, tk, tn), w_map)]
    args = [a, w]
    for arr, spec in extra:
        args.append(arr)
        in_specs.append(spec)
    aliases = {}
    if out_buf is not None:
        aliases = {len(args): 0}
        args.append(out_buf)
        in_specs.append(pl.BlockSpec(memory_space=pl.ANY))
        out_shape = jax.ShapeDtypeStruct(out_buf.shape, out_buf.dtype)
    else:
        out_shape = jax.ShapeDtypeStruct((nb, rows, n), out_dtype)
    out_spec = pl.BlockSpec((None, tm, tn), lambda j, b, i, k: (b, i + out_row_blk, j + out_col_blk))
    scratch = [pltpu.VMEM((tm, tn), F32)] if nk > 1 else []
    body = functools.partial(_mm_body, nk=nk, n_extra=len(extra), epilogue=epilogue,
                             has_buf=out_buf is not None)
    return pl.pallas_call(
        body, out_shape=out_shape, grid=(nj, nb, ni, nk), in_specs=in_specs, out_specs=out_spec,
        scratch_shapes=scratch, input_output_aliases=aliases, name=name,
        compiler_params=_cp(("parallel", "parallel", "parallel", "arbitrary")))(*args)


def _row_is_lat(row_blk, tm, n_lat):
    rows = row_blk * tm + lax.broadcasted_iota(jnp.int32, (tm, 1), 0)
    return rows < n_lat


def _gated_residual_epilogue(acc, row_blk, res_ref, gate_ref, *, tm, n_lat):
    gate = jnp.where(_row_is_lat(row_blk, tm, n_lat), gate_ref[0], gate_ref[1])
    return res_ref[...] + gate * acc


def _matmul_gated_residual(a, w, res, gate, *, tm, tn, tk, rows, n_lat, name):
    extra = [(res, pl.BlockSpec((None, tm, tn), lambda j, b, i, k: (b, i, j))),
             (gate, pl.BlockSpec((None, 2, 1, tn), lambda j, b, i, k: (b, 0, 0, j)))]
    epi = functools.partial(_gated_residual_epilogue, tm=tm, n_lat=n_lat)
    return _matmul(a, w, tm=tm, tn=tn, tk=tk, out_dtype=F32, rows=rows, extra=extra,
                   epilogue=epi, name=name)


def _rmsmod_body(x_ref, w_ref, mod_ref, o_ref, *, tm, n_lat):
    x = x_ref[...]
    y = x * lax.rsqrt(jnp.mean(x * x, axis=-1, keepdims=True) + EPS) * w_ref[...]
    is_lat = _row_is_lat(pl.program_id(2), tm, n_lat)
    shift = jnp.where(is_lat, mod_ref[0, 0], mod_ref[0, 1])
    scale = jnp.where(is_lat, mod_ref[1, 0], mod_ref[1, 1])
    o_ref[...] = (y * (1.0 + scale) + shift).astype(o_ref.dtype)


def _rmsmod(x, w, mod, *, tm, rows, n_lat):
    nb, _, d = x.shape
    tm = tm // 4
    assert rows % tm == 0 and tm % 16 == 0
    body = functools.partial(_rmsmod_body, tm=tm, n_lat=n_lat)
    return pl.pallas_call(
        body, out_shape=jax.ShapeDtypeStruct((nb, rows, d), BF16), grid=(1, nb, rows // tm),
        in_specs=[pl.BlockSpec((None, tm, d), lambda j, b, i: (b, i, 0)),
                  pl.BlockSpec((1, d), lambda j, b, i: (0, 0)),
                  pl.BlockSpec((None, 2, 2, 1, d), lambda j, b, i: (b, 0, 0, 0, 0))],
        out_specs=pl.BlockSpec((None, tm, d), lambda j, b, i: (b, i, 0)),
        name="rmsmod", compiler_params=_cp(("parallel", "parallel", "parallel")))(x, w, mod)


CONV_CH = 128


def _seg_bases(segs):
    bases, pos = [], 0
    for _, n in segs:
        bases.append(pos)
        pos += PAD + n
    return bases, pos + PAD


def _conv_fill(s_ref, x_ref, segs):
    bases, total = _seg_bases(segs)
    zeros = jnp.zeros((PAD, s_ref.shape[1]), F32)
    for (r0, n), base in zip(segs, bases):
        s_ref[base:base + PAD, :] = zeros

        def body(i, carry, r0=r0, base=base):
            t0 = pl.multiple_of(i * CONV_CH, CONV_CH)
            s_ref[pl.ds(base + PAD + t0, CONV_CH), :] = x_ref[pl.ds(r0 + t0, CONV_CH), :].astype(F32)
            return carry

        lax.fori_loop(0, n // CONV_CH, body, 0)
    s_ref[total - PAD:total, :] = zeros


def _conv_at(s_ref, w_ref, b_ref, base, t0):
    big = s_ref[pl.ds(base + t0, CONV_CH + 2 * PAD), :]
    return (w_ref[0:1, :] * big[PAD - 1:PAD - 1 + CONV_CH]
            + w_ref[1:2, :] * big[PAD:PAD + CONV_CH]
            + w_ref[2:3, :] * big[PAD + 1:PAD + 1 + CONV_CH] + b_ref[...])


def _conv_loop(segs, fn):
    bases, _ = _seg_bases(segs)
    for si, ((r0, n), base) in enumerate(zip(segs, bases)):
        def body(i, carry, r0=r0, base=base, si=si):
            t0 = pl.multiple_of(i * CONV_CH, CONV_CH)
            fn(si, r0, base, t0)
            return carry

        lax.fori_loop(0, n // CONV_CH, body, 0)


def _conv_silu_body(x_ref, w_ref, b_ref, o_ref, s_ref, *, segs):
    _conv_fill(s_ref, x_ref, segs)

    def fn(si, r0, base, t0):
        o_ref[pl.ds(r0 + t0, CONV_CH), :] = _silu(_conv_at(s_ref, w_ref, b_ref, base, t0)).astype(o_ref.dtype)

    _conv_loop(segs, fn)


def _conv_rope_body(x_ref, w_ref, b_ref, cos_ref, sin_ref, o_ref, s_ref, *, segs, tc):
    _conv_fill(s_ref, x_ref, segs)
    lane = lax.broadcasted_iota(jnp.int32, (CONV_CH, tc), 1)
    first = (lane % (ML_QK_DIM // 2)) < (ML_QK_DIM // 4)
    nf = ML_QK_DIM // 4

    def fn(si, r0, base, t0):
        y = _silu(_conv_at(s_ref, w_ref, b_ref, base, t0))
        partner = jnp.where(first, pltpu.roll(y, tc - nf, 1), pltpu.roll(y, nf, 1))
        rows = pl.ds(r0 + t0, CONV_CH)
        o_ref[rows, :] = (y * cos_ref[rows, :] + partner * sin_ref[rows, :]).astype(o_ref.dtype)

    _conv_loop(segs, fn)


def _conv_ffn_body(a_ref, x_ref, w_ref, b_ref, o_ref, s_ref, *, segs):
    _conv_fill(s_ref, x_ref, segs)

    def fn(si, r0, base, t0):
        rows = pl.ds(r0 + t0, CONV_CH)
        g = _silu(_conv_at(s_ref, w_ref, b_ref, base, t0))
        o_ref[rows, :] = (a_ref[rows, :].astype(F32) * g).astype(o_ref.dtype)

    _conv_loop(segs, fn)


def _conv_hyena_body(x0_ref, x1_ref, v_ref, w0_ref, b0_ref, w1_ref, b1_ref, w2_ref, b2_ref,
                     ox0_ref, ozl_ref, ozc_ref, s0_ref, s1_ref, s2_ref, *, segs):
    _conv_fill(s0_ref, x0_ref, segs)
    _conv_fill(s1_ref, x1_ref, segs)
    _conv_fill(s2_ref, v_ref, segs)
    outs = (ozl_ref, ozc_ref)

    def fn(si, r0, base, t0):
        ox0_ref[pl.ds(r0 + t0, CONV_CH), :] = _conv_at(s0_ref, w0_ref, b0_ref, base, t0).astype(ox0_ref.dtype)
        zin = _conv_at(s1_ref, w1_ref, b1_ref, base, t0) * _conv_at(s2_ref, w2_ref, b2_ref, base, t0)
        outs[si][pl.ds(t0, CONV_CH), :] = zin.astype(outs[si].dtype)

    _conv_loop(segs, fn)


def _conv_specs(t_in, tc, xcol_blk, wcol_blk):
    return [pl.BlockSpec((None, t_in, tc), lambda b, j: (b, 0, j + xcol_blk)),
            pl.BlockSpec((CONV_W, tc), lambda b, j: (0, j + wcol_blk)),
            pl.BlockSpec((1, tc), lambda b, j: (0, j + wcol_blk))]


def _conv_silu(x, w, b, *, xcol, wcol, width, segs, tc=256):
    nb, t_in, _ = x.shape
    _, total = _seg_bases(segs)
    return pl.pallas_call(
        functools.partial(_conv_silu_body, segs=segs),
        out_shape=jax.ShapeDtypeStruct((nb, t_in, width), BF16), grid=(nb, width // tc),
        in_specs=_conv_specs(t_in, tc, xcol // tc, wcol // tc),
        out_specs=pl.BlockSpec((None, t_in, tc), lambda b, j: (b, 0, j)),
        scratch_shapes=[pltpu.VMEM((total, tc), F32)],
        name="conv_silu", compiler_params=_cp(("parallel", "parallel")))(x, w, b)


def _conv_rope(x, w, b, cos, sin, *, width, segs, tc=256):
    nb, t_in, _ = x.shape
    _, total = _seg_bases(segs)
    tab = pl.BlockSpec((t_in, tc), lambda b, j: (0, 0))
    return pl.pallas_call(
        functools.partial(_conv_rope_body, segs=segs, tc=tc),
        out_shape=jax.ShapeDtypeStruct((nb, t_in, width), BF16), grid=(nb, width // tc),
        in_specs=_conv_specs(t_in, tc, 0, 0) + [tab, tab],
        out_specs=pl.BlockSpec((None, t_in, tc), lambda b, j: (b, 0, j)),
        scratch_shapes=[pltpu.VMEM((total, tc), F32)],
        name="conv_rope", compiler_params=_cp(("parallel", "parallel")))(x, w, b, cos, sin)


def _conv_ffn(up, w, b, *, segs, rows, tc=256):
    nb, t_in, two_ff = up.shape
    ff = two_ff // 2
    _, total = _seg_bases(segs)
    return pl.pallas_call(
        functools.partial(_conv_ffn_body, segs=segs),
        out_shape=jax.ShapeDtypeStruct((nb, rows, ff), BF16), grid=(nb, ff // tc),
        in_specs=[pl.BlockSpec((None, rows, tc), lambda b, j: (b, 0, j))] + _conv_specs(rows, tc, ff // tc, 0),
        out_specs=pl.BlockSpec((None, rows, tc), lambda b, j: (b, 0, j)),
        scratch_shapes=[pltpu.VMEM((total, tc), F32)],
        name="conv_ffn", compiler_params=_cp(("parallel", "parallel")))(up, up, w, b)


def _conv_hyena(pr, w, b, *, xcol, wcol, segs, tc=256):
    nb, t_in, _ = pr.shape
    (_, n_lat), (_, n_ctx) = segs
    _, total = _seg_bases(segs)
    wd = HY_WIDTH
    in_specs = [pl.BlockSpec((None, t_in, tc), functools.partial(lambda b, j, o: (b, 0, j + o), o=(xcol + m * wd) // tc))
                for m in range(3)]
    wargs = []
    for m in range(3):
        off = (wcol + m * wd) // tc
        in_specs.append(pl.BlockSpec((CONV_W, tc), functools.partial(lambda b, j, o: (0, j + o), o=off)))
        in_specs.append(pl.BlockSpec((1, tc), functools.partial(lambda b, j, o: (0, j + o), o=off)))
        wargs += [w, b]
    return pl.pallas_call(
        functools.partial(_conv_hyena_body, segs=segs),
        out_shape=(jax.ShapeDtypeStruct((nb, t_in, wd), BF16), jax.ShapeDtypeStruct((nb, n_lat, wd), BF16),
                   jax.ShapeDtypeStruct((nb, n_ctx, wd), BF16)),
        grid=(nb, wd // tc), in_specs=in_specs,
        out_specs=(pl.BlockSpec((None, t_in, tc), lambda b, j: (b, 0, j)),
                   pl.BlockSpec((None, n_lat, tc), lambda b, j: (b, 0, j)),
                   pl.BlockSpec((None, n_ctx, tc), lambda b, j: (b, 0, j))),
        scratch_shapes=[pltpu.VMEM((total, tc), F32)] * 3,
        name="conv_hyena", compiler_params=_cp(("parallel", "parallel")))(pr, pr, pr, *wargs)


def _scan_masks(reverse):
    q = SCAN_Q
    ii = lax.broadcasted_iota(jnp.int32, (q, q), 0)
    jj = lax.broadcasted_iota(jnp.int32, (q, q), 1)
    mask = (jj >= ii) if reverse else (jj <= ii)
    mask_t = (jj <= ii) if reverse else (jj >= ii)
    return mask, mask.astype(F32), mask_t.astype(F32)


def _chunk_map(reverse, n_chunks, n_lat_chunks):
    if reverse:
        return lambda k: n_chunks - 1 - k
    return lambda k: (k + n_lat_chunks) % n_chunks


def _ssd_body(*refs, reverse, final):
    (xs_ref, b_ref, c_ref, dtc_ref, dtr_ref, dbc_ref, dbr_ref, alc_ref, alr_ref) = refs[:9]
    if final:
        yf_ref, z_ref, dsk_ref, nw_ref, o_ref, h_ref = refs[9:]
    else:
        o_ref, h_ref = refs[9:]
    q, gw, hd = SCAN_Q, SSD_GW, SSD_HEAD_DIM

    @pl.when(pl.program_id(2) == 0)
    def _():
        h_ref[...] = jnp.zeros_like(h_ref)

    xs = xs_ref[...].astype(F32)
    bm = b_ref[...]
    cm = c_ref[...]
    dt_c = _softplus(dtc_ref[...] + dbc_ref[...])
    dt_r = _softplus(dtr_ref[...] + dbr_ref[...])
    da_c = dt_c * (-jnp.exp(alc_ref[...]))
    da_r = dt_r * (-jnp.exp(alr_ref[...]))
    mask, maskf, mask_tf = _scan_masks(reverse)
    acum_c = _dot(maskf, da_c, HI)
    acum_r = _dot(da_r, mask_tf, HI)
    atot = jnp.sum(da_c, axis=0, keepdims=True)

    er = lax.broadcasted_iota(jnp.int32, (SSD_HPG, gw), 0)
    ec = lax.broadcasted_iota(jnp.int32, (SSD_HPG, gw), 1)
    expand = (ec // hd == er).astype(F32)
    stack = jnp.concatenate([dt_c, jnp.exp(atot - acum_c), jnp.exp(acum_c),
                             jnp.broadcast_to(jnp.exp(atot), (PAD, SSD_HPG))], axis=0)
    wide = _dot(stack, expand, HI)
    dt_e, ws_e, eac_e, eat_e = wide[0:q], wide[q:2 * q], wide[2 * q:3 * q], wide[3 * q:3 * q + 1]

    xq = xs * dt_e
    h_prev = h_ref[...]
    cb = _dot_nt(cm, bm)
    y = eac_e * _dot(cm, h_prev.astype(BF16))
    lane = lax.broadcasted_iota(jnp.int32, (q, 2 * hd), 1)
    pieces = []
    for p in range(SSD_HPG // 2):
        ms = []
        for r in (2 * p, 2 * p + 1):
            seg = acum_c[:, r:r + 1] - acum_r[r:r + 1, :]
            ms.append((cb * jnp.where(mask, jnp.exp(seg), 0.0)).astype(BF16))
        xp = xq[:, 2 * hd * p:2 * hd * (p + 1)]
        rhs = jnp.concatenate([jnp.where(lane < hd, xp, 0.0), jnp.where(lane >= hd, xp, 0.0)], axis=0)
        pieces.append(_dot(jnp.concatenate(ms, axis=1), rhs.astype(BF16)))
    y = y + jnp.concatenate(pieces, axis=1)

    bt = bm.astype(F32).T.astype(BF16)
    h_ref[...] = eat_e * h_prev + _dot(bt, (xq * ws_e).astype(BF16))

    if final:
        ytot = y + yf_ref[...] + xs * dsk_ref[...]
        g = ytot * _silu(z_ref[...].astype(F32))
        o = g * lax.rsqrt(jnp.mean(g * g, axis=-1, keepdims=True) + EPS) * nw_ref[...]
        o_ref[...] = o.astype(o_ref.dtype)
    else:
        o_ref[...] = y


def _ssd(xbc, dt_c, dt_r, db_c, db_r, al_c, al_r, *, direction, n_lat, final_args=None, out_width=None):
    nb, t, _ = xbc.shape
    q, gw, hpg = SCAN_Q, SSD_GW, SSD_HPG
    nch = t // q
    reverse = direction == 1
    cmap = _chunk_map(reverse, nch, n_lat // q)
    nblk = SSD_STATE // LANES
    b0 = SSD_WIDTH // SSD_STATE
    c0 = b0 + SSD_GROUPS * nblk
    d = direction
    in_specs = [
        pl.BlockSpec((None, q, gw), lambda b, g, k: (b, cmap(k), g)),
        pl.BlockSpec((None, q, SSD_STATE), lambda b, g, k: (b, cmap(k), b0 + g)),
        pl.BlockSpec((None, q, SSD_STATE), lambda b, g, k: (b, cmap(k), c0 + g)),
        pl.BlockSpec((None, None, None, q, hpg), lambda b, g, k: (b, d, g, cmap(k), 0)),
        pl.BlockSpec((None, None, None, hpg, q), lambda b, g, k: (b, d, g, 0, cmap(k))),
        pl.BlockSpec((None, None, 1, hpg), lambda b, g, k: (d, g, 0, 0)),
        pl.BlockSpec((None, None, hpg, 1), lambda b, g, k: (d, g, 0, 0)),
        pl.BlockSpec((None, None, 1, hpg), lambda b, g, k: (d, g, 0, 0)),
        pl.BlockSpec((None, None, hpg, 1), lambda b, g, k: (d, g, 0, 0)),
    ]
    args = [xbc, xbc, xbc, dt_c, dt_r, db_c, db_r, al_c, al_r]
    final = final_args is not None
    if final:
        yf, pr, dskip, norm_w = final_args
        in_specs += [pl.BlockSpec((None, q, gw), lambda b, g, k: (b, cmap(k), g)),
                     pl.BlockSpec((None, q, gw), lambda b, g, k: (b, cmap(k), g)),
                     pl.BlockSpec((1, gw), lambda b, g, k: (0, g)),
                     pl.BlockSpec((1, gw), lambda b, g, k: (0, g))]
        args += [yf, pr, dskip, norm_w]
        out_shape = jax.ShapeDtypeStruct((nb, t, out_width), BF16)
    else:
        out_shape = jax.ShapeDtypeStruct((nb, t, SSD_WIDTH), F32)
    return pl.pallas_call(
        functools.partial(_ssd_body, reverse=reverse, final=final),
        out_shape=out_shape, grid=(nb, SSD_GROUPS, nch), in_specs=in_specs,
        out_specs=pl.BlockSpec((None, q, gw), lambda b, g, k: (b, cmap(k), g)),
        scratch_shapes=[pltpu.VMEM((SSD_STATE, gw), F32)],
        name="ssd_bwd" if reverse else "ssd_fwd",
        compiler_params=_cp(("parallel", "parallel", "arbitrary")))(*args)


def _hyfilt_body(f_ref, w1_ref, b1_ref, w2_ref, b2_ref, w3_ref, fr_ref, dl_ref, o_ref, *, tl):
    f = f_ref[...]
    h = jnp.sin(fr_ref[0:1, :] * (_dot(f, w1_ref[...], HI) + b1_ref[...]))
    h = jnp.sin(fr_ref[1:2, :] * (_dot(h, w2_ref[...], HI) + b2_ref[...]))
    h = _dot(h, w3_ref[...], HI)
    dec = jnp.exp(-f[:, 0:1] * dl_ref[...])
    row = pl.program_id(0) * tl + lax.broadcasted_iota(jnp.int32, (tl, 1), 0)
    wd = HY_WIDTH
    o_ref[:, 0:wd] = (h[:, 0:wd] * dec).astype(o_ref.dtype)
    o_ref[:, wd:2 * wd] = jnp.where(row == 0, 0.0, h[:, wd:2 * wd] * dec).astype(o_ref.dtype)


def _hyena_filters(length, w1, b1, w2, b2, w3, freq, *, tl=256):
    p = LANES
    t = jnp.linspace(0.0, 1.0, length, dtype=F32)[:, None]
    w = 2.0 * math.pi * jnp.arange(length, dtype=F32)[:, None] / length
    f = jnp.linspace(1e-4, HY_BANDS - 1, HY_BANDS, dtype=F32)[None, :]
    feats = jnp.concatenate([t, jnp.cos(f * w), -jnp.sin(f * w)], axis=-1)
    feats = jnp.pad(feats, ((0, 0), (0, p - HY_EMB)))
    po = p - HY_ORDER
    w1p = jnp.pad(w1, ((0, p - HY_EMB), (0, po)))
    w2p = jnp.pad(w2, ((0, po), (0, po)))
    w3p = jnp.pad(w3, ((0, po), (0, 0)))
    b1p = jnp.pad(b1[None, :], ((0, 0), (0, po)))
    b2p = jnp.pad(b2[None, :], ((0, 0), (0, po)))
    frp = jnp.pad(freq, ((0, 0), (0, po)))
    deltas = jnp.abs(jnp.linspace(math.log(HY_TARGET) / HY_FAST, math.log(HY_TARGET) / HY_SLOW, HY_WIDTH,
                                  dtype=F32))[None, :]
    full = lambda shape: pl.BlockSpec(shape, lambda i: (0,) * len(shape))
    out = pl.pallas_call(
        functools.partial(_hyfilt_body, tl=tl),
        out_shape=jax.ShapeDtypeStruct((length, 2 * HY_WIDTH), BF16), grid=(length // tl,),
        in_specs=[pl.BlockSpec((tl, p), lambda i: (i, 0)), full((p, p)), full((1, p)), full((p, p)), full((1, p)),
                  full((p, 2 * HY_WIDTH)), full((2, p)), full((1, HY_WIDTH))],
        out_specs=pl.BlockSpec((tl, 2 * HY_WIDTH), lambda i: (i, 0)),
        name="hyena_filters", compiler_params=_cp(("parallel",)))(feats, w1p, b1p, w2p, b2p, w3p, frp, deltas)
    return out[None]


def _dft_matrices(length):
    n = 2 * length
    k = jnp.arange(length, dtype=jnp.int32)[:, None]
    t = jnp.arange(length, dtype=jnp.int32)[None, :]
    step = 1 << (int(math.log2(length)) // 2)
    part = jnp.arange(length // step, dtype=jnp.int32)[:, None]
    ang_hi = ((part * step * t) % n).astype(F32) * (2.0 * math.pi / n)
    ang_lo = ((jnp.arange(step, dtype=jnp.int32)[:, None] * t) % n).astype(F32) * (2.0 * math.pi / n)
    ch, sh = jnp.cos(ang_hi)[:, None, :], jnp.sin(ang_hi)[:, None, :]
    cl, sl = jnp.cos(ang_lo)[None, :, :], jnp.sin(ang_lo)[None, :, :]
    c = (ch * cl - sh * sl).reshape(length, length)
    s = (sh * cl + ch * sl).reshape(length, length)
    alt_t = jnp.where(t % 2 == 0, 1.0, -1.0).astype(F32)
    fwd = jnp.concatenate([c, jnp.where(k == 0, alt_t, -s)], axis=0)
    alt_r = jnp.where(k % 2 == 0, 1.0, -1.0).astype(F32)
    inv_re = jnp.where(t == 0, 1.0 / n, (2.0 / n) * c)
    inv_im = jnp.where(t == 0, alt_r / n, (-2.0 / n) * s)
    inv = jnp.concatenate([inv_re, inv_im], axis=1)
    return fwd.astype(BF16)[None], inv.astype(BF16)[None]


def _specmul_body(z_ref, kf_ref, kb_ref, o_ref, *, tr):
    zr, zi = z_ref[0], z_ref[1]
    kr = kf_ref[0] + kb_ref[0]
    ki = kf_ref[1] - kb_ref[1]
    row = pl.program_id(1) * tr + lax.broadcasted_iota(jnp.int32, (tr, 1), 0)
    first = row == 0
    yr = jnp.where(first, zr * kr, zr * kr - zi * ki)
    yi = jnp.where(first, zi * (kf_ref[1] + kb_ref[1]), zr * ki + zi * kr)
    o_ref[0] = yr.astype(o_ref.dtype)
    o_ref[1] = yi.astype(o_ref.dtype)


def _specmul(z, ksp, *, tr, tc=512):
    nb, l2, c = z.shape
    ln = l2 // 2
    z4 = z.reshape(nb, 2, ln, c)
    k3 = ksp.reshape(2, ln, 2 * c)
    out = pl.pallas_call(
        functools.partial(_specmul_body, tr=tr),
        out_shape=jax.ShapeDtypeStruct((nb, 2, ln, c), BF16), grid=(nb, ln // tr, c // tc),
        in_specs=[pl.BlockSpec((None, 2, tr, tc), lambda b, i, j: (b, 0, i, j)),
                  pl.BlockSpec((2, tr, tc), lambda b, i, j: (0, i, j)),
                  pl.BlockSpec((2, tr, tc), lambda b, i, j: (0, i, j + c // tc))],
        out_specs=pl.BlockSpec((None, 2, tr, tc), lambda b, i, j: (b, 0, i, j)),
        name="specmul", compiler_params=_cp(("parallel", "parallel", "parallel")))(z4, k3, k3)
    return out.reshape(nb, l2, c)


def _hyena_out_epilogue(acc, row_blk, x0_ref, zin_ref, db_ref):
    del row_blk
    zin = zin_ref[...].astype(F32)
    return x0_ref[...].astype(F32) * (acc + zin * db_ref[...])


def _hyena_longconv(zin, x0, hfb, dbias, ybuf, *, row0, tn=1024):
    nb, ln, c = zin.shape
    tm_f, tm_i = min(1024, 2 * ln), min(1024, ln)
    tk_f, tk_i = min(2048, ln), min(2048, 2 * ln)
    assert row0 % tm_i == 0
    fwd, inv = _dft_matrices(ln)
    ksp = _matmul(fwd, hfb, tm=tm_f, tn=tn, tk=tk_f, out_dtype=F32, name="hy_filter_dft")
    z = _matmul(fwd, zin, tm=tm_f, tn=tn, tk=tk_f, out_dtype=F32, name="hy_dft")
    y = _specmul(z, ksp, tr=min(ln, 512))
    rb = row0 // tm_i
    extra = [(x0, pl.BlockSpec((None, tm_i, tn), lambda j, b, i, k: (b, i + rb, j))),
             (zin, pl.BlockSpec((None, tm_i, tn), lambda j, b, i, k: (b, i, j))),
             (dbias, pl.BlockSpec((1, tn), lambda j, b, i, k: (0, j)))]
    return _matmul(inv, y, tm=tm_i, tn=tn, tk=tk_i, out_dtype=BF16, extra=extra, epilogue=_hyena_out_epilogue,
                   out_buf=ybuf, out_row_blk=rb, out_col_blk=SSD_WIDTH // tn, name="hy_idft")


def _mlstm_body(*refs, reverse, final):
    q_ref, k_ref, v_ref, gc_ref, gr_ref = refs[:5]
    if final:
        hf_ref, og_ref, nw_ref, o_ref, ct_ref, n_ref, m_ref = refs[5:]
    else:
        o_ref, ct_ref, n_ref, m_ref = refs[5:]
    nh, dk, dv = ML_HEADS, ML_QK_DIM, ML_V_DIM
    scale = dk ** -0.5

    @pl.when(pl.program_id(1) == 0)
    def _():
        ct_ref[...] = jnp.zeros_like(ct_ref)
        n_ref[...] = jnp.zeros_like(n_ref)
        m_ref[...] = jnp.zeros_like(m_ref)

    gc, gr = gc_ref[...], gr_ref[...]
    li_c, lf_c = gc[:, 0:nh], _log_sigmoid(gc[:, nh:2 * nh])
    li_r, lf_r = gr[0:nh, :], _log_sigmoid(gr[nh:2 * nh, :])
    mask, maskf, mask_tf = _scan_masks(reverse)
    bc_c = _dot(maskf, lf_c, HI)
    bc_r = _dot(lf_r, mask_tf, HI)
    btot_c = jnp.sum(lf_c, axis=0, keepdims=True)

    for h in range(nh):
        qh = q_ref[:, dk * h:dk * (h + 1)]
        kh = k_ref[:, dk * h:dk * (h + 1)]
        vh = v_ref[:, dv * h:dv * (h + 1)]
        m_s = m_ref[h:h + 1, 0:1]
        b_c, b_r = bc_c[:, h:h + 1], bc_r[h:h + 1, :]
        i_c, i_r = li_c[:, h:h + 1], li_r[h:h + 1, :]
        btot = btot_c[:, h:h + 1]
        dmat = jnp.where(mask, b_c - b_r + i_r, -jnp.inf)
        inter = b_c + m_s
        m_t = jnp.maximum(inter, jnp.max(dmat, axis=1, keepdims=True))
        s = _dot_nt(qh, kh) * scale * jnp.exp(dmat - m_t)
        dec = jnp.exp(inter - m_t)
        ct = ct_ref[h]
        n_s = n_ref[h:h + 1, :]
        num = _dot(s.astype(BF16), vh) + (dec * scale) * _dot(qh, ct.astype(BF16))
        qn = jnp.sum(qh.astype(F32) * n_s, axis=1, keepdims=True) * scale
        den = jnp.sum(s, axis=1, keepdims=True) + dec * qn
        hh = num / jnp.maximum(jnp.abs(den), jnp.exp(-m_t))

        wst_c = btot - b_c + i_c
        wst_r = btot - b_r + i_r
        m_new = jnp.maximum(btot + m_s, jnp.max(wst_r, axis=1, keepdims=True))
        dstate = jnp.exp(btot + m_s - m_new)
        kw = kh.astype(F32) * jnp.exp(wst_c - m_new)
        ct_ref[h] = dstate * ct + _dot(kw.T.astype(BF16), vh)
        n_ref[h:h + 1, :] = dstate * n_s + jnp.sum(kw, axis=0, keepdims=True)
        m_ref[h:h + 1, :] = jnp.broadcast_to(m_new, (1, LANES))

        cols = slice(dv * h, dv * (h + 1))
        if final:
            hs = hh + hf_ref[:, cols]
            hn = hs * lax.rsqrt(jnp.mean(hs * hs, axis=-1, keepdims=True) + EPS) * nw_ref[:, cols]
            o_ref[:, cols] = (hn * _sigmoid(og_ref[:, cols].astype(F32))).astype(o_ref.dtype)
        else:
            o_ref[:, cols] = hh


def _mlstm(qk, pr, g_c, g_r, *, direction, n_lat, final_args=None, out_width=None):
    nb, t, _ = qk.shape
    q = SCAN_Q
    nch = t // q
    reverse = direction == 1
    cmap = _chunk_map(reverse, nch, n_lat // q)
    d = direction
    in_specs = [
        pl.BlockSpec((None, q, ML_QK_WIDTH), lambda b, k: (b, cmap(k), 0)),
        pl.BlockSpec((None, q, ML_QK_WIDTH), lambda b, k: (b, cmap(k), 1)),
        pl.BlockSpec((None, q, ML_WIDTH), lambda b, k: (b, cmap(k), 2 * ML_QK_WIDTH // ML_WIDTH)),
        pl.BlockSpec((None, None, q, 2 * ML_HEADS), lambda b, k: (b, d, cmap(k), 0)),
        pl.BlockSpec((None, None, 2 * ML_HEADS, q), lambda b, k: (b, d, 0, cmap(k))),
    ]
    args = [qk, qk, pr, g_c, g_r]
    final = final_args is not None
    if final:
        hf, norm_w = final_args
        in_specs += [pl.BlockSpec((None, q, ML_WIDTH), lambda b, k: (b, cmap(k), 0)),
                     pl.BlockSpec((None, q, ML_WIDTH), lambda b, k: (b, cmap(k), 2 * ML_QK_WIDTH // ML_WIDTH + 1)),
                     pl.BlockSpec((1, ML_WIDTH), lambda b, k: (0, 0))]
        args += [hf, pr, norm_w]
        out_shape = jax.ShapeDtypeStruct((nb, t, out_width), BF16)
    else:
        out_shape = jax.ShapeDtypeStruct((nb, t, ML_WIDTH), F32)
    return pl.pallas_call(
        functools.partial(_mlstm_body, reverse=reverse, final=final),
        out_shape=out_shape, grid=(nb, nch), in_specs=in_specs,
        out_specs=pl.BlockSpec((None, q, ML_WIDTH), lambda b, k: (b, cmap(k), 0)),
        scratch_shapes=[pltpu.VMEM((ML_HEADS, ML_QK_DIM, ML_V_DIM), F32), pltpu.VMEM((ML_HEADS, ML_QK_DIM), F32),
                        pltpu.VMEM((ML_HEADS, LANES), F32)],
        name="mlstm_bwd" if reverse else "mlstm_fwd",
        compiler_params=_cp(("parallel", "arbitrary")))(*args)


def _na_window_start(j, rows):
    return jnp.clip(j * NA_QROWS - NA_ROWS // 2, 0, rows - NA_KROWS)


def _na_body(q_ref, k_ref, v_ref, tab_ref, qw_ref, kw_ref, buf_ref, o_ref, kn_ref, *, n_lat, n_ctx):
    del buf_ref
    dh = NA_HEAD_DIM
    scale = dh ** -0.5
    rows = n_lat // GRID_W
    j = pl.program_id(2)
    nq, nk = NA_QROWS * GRID_W, NA_KROWS * GRID_W

    @pl.when(j == 0)
    def _():
        def body(i, carry):
            r = pl.ds(pl.multiple_of(i * LANES, LANES), LANES)
            kk = k_ref[r, :].astype(F32)
            kn = kk * lax.rsqrt(jnp.mean(kk * kk, axis=-1, keepdims=True) + EPS) * kw_ref[...]
            kn_ref[r, :] = kn.astype(kn_ref.dtype)
            return carry

        lax.fori_loop(0, (n_lat + n_ctx) // LANES, body, 0)

    qq = q_ref[...].astype(F32)
    qn = (qq * lax.rsqrt(jnp.mean(qq * qq, axis=-1, keepdims=True) + EPS) * qw_ref[...]).astype(BF16)
    start = _na_window_start(j, rows)
    win = pl.ds(pl.multiple_of(start * GRID_W, GRID_W), nk)
    s = _dot_nt(qn, kn_ref[win, :]) * scale
    pieces = []
    for qr in range(NA_QROWS):
        d0 = start - (j * NA_QROWS + qr) + NA_ROWS - 1 + NA_QROWS
        pieces.append(s[qr * GRID_W:(qr + 1) * GRID_W] + tab_ref[d0])
    s = jnp.concatenate(pieces, axis=0)
    ctx = pl.ds(n_lat, n_ctx)
    sc = _dot_nt(qn, kn_ref[ctx, :]) * scale
    m = jnp.maximum(jnp.max(s, axis=1, keepdims=True), jnp.max(sc, axis=1, keepdims=True))
    p, pc = jnp.exp(s - m), jnp.exp(sc - m)
    denom = jnp.sum(p, axis=1, keepdims=True) + jnp.sum(pc, axis=1, keepdims=True)
    o = _dot(p.astype(BF16), v_ref[win, :]) + _dot(pc.astype(BF16), v_ref[ctx, :])
    o_ref[...] = (o / denom).astype(o_ref.dtype)


def _na_bias_table(rpb, rows):
    nd = 2 * NA_QROWS
    valid = np.zeros((nd, NA_KROWS), bool)
    seen = np.zeros((nd,), bool)
    for j in range(rows // NA_QROWS):
        start = int(np.clip(j * NA_QROWS - NA_ROWS // 2, 0, rows - NA_KROWS))
        for qr in range(NA_QROWS):
            r = j * NA_QROWS + qr
            rs = int(np.clip(r - NA_ROWS // 2, 0, rows - NA_ROWS))
            d = start - r + NA_ROWS - 1 + NA_QROWS
            v = np.array([(rs <= start + i < rs + NA_ROWS) for i in range(NA_KROWS)])
            assert 0 <= d < nd and (not seen[d] or (valid[d] == v).all())
            valid[d], seen[d] = v, True
    col = np.arange(GRID_W)
    cs = np.clip(col - NA_COLS // 2, 0, GRID_W - NA_COLS)
    in_win = (col[None, :] >= cs[:, None]) & (col[None, :] < cs[:, None] + NA_COLS)
    dc = np.clip(col[None, :] - col[:, None] + NA_COLS - 1, 0, 2 * NA_COLS - 2)
    ndr, ndc, w = 2 * NA_ROWS - 1, 2 * NA_COLS - 1, GRID_W
    dr = (np.arange(nd)[:, None] - NA_QROWS) + np.arange(NA_KROWS)[None, :]
    ok = valid[:, None, :, None] & in_win[None, :, None, :] & (dr >= 0)[:, None, :, None] \
        & (dr < ndr)[:, None, :, None]
    ext = jnp.pad(rpb, ((0, 0), (0, 0), (w, w)), mode="edge")
    c0 = w + NA_COLS - 1
    t1 = jnp.stack([ext[:, :, c0 - qc:c0 - qc + w] for qc in range(w)], axis=2)
    t1p = jnp.pad(t1, ((0, 0), (NA_QROWS, nd + NA_KROWS - NA_QROWS - ndr), (0, 0), (0, 0)))
    vals = jnp.stack([t1p[:, d:d + NA_KROWS] for d in range(nd)], axis=1)
    tab = jnp.where(ok[None], vals.transpose(0, 1, 3, 2, 4), -jnp.inf).astype(F32)
    return tab.reshape(rpb.shape[0], nd, w, NA_KROWS * w)


def _na(pr, tab, q_w, k_w, ybuf, *, n_lat, n_ctx, qcol, out_col):
    nb, t, _ = pr.shape
    dh, nh = NA_HEAD_DIM, NA_HEADS
    nq = NA_QROWS * GRID_W
    qb, ob = qcol // dh, out_col // dh
    nd = tab.shape[1]
    return pl.pallas_call(
        functools.partial(_na_body, n_lat=n_lat, n_ctx=n_ctx),
        out_shape=jax.ShapeDtypeStruct(ybuf.shape, ybuf.dtype), grid=(nb, nh, n_lat // nq),
        in_specs=[pl.BlockSpec((None, nq, dh), lambda b, h, j: (b, j, qb + h)),
                  pl.BlockSpec((None, t, dh), lambda b, h, j: (b, 0, qb + nh + h)),
                  pl.BlockSpec((None, t, dh), lambda b, h, j: (b, 0, qb + 2 * nh + h)),
                  pl.BlockSpec((None, nd, GRID_W, NA_KROWS * GRID_W), lambda b, h, j: (h, 0, 0, 0)),
                  pl.BlockSpec((1, dh), lambda b, h, j: (0, 0)),
                  pl.BlockSpec((1, dh), lambda b, h, j: (0, 0)),
                  pl.BlockSpec(memory_space=pl.ANY)],
        out_specs=pl.BlockSpec((None, nq, dh), lambda b, h, j: (b, j, ob + h)),
        scratch_shapes=[pltpu.VMEM((t, dh), BF16)],
        input_output_aliases={6: 0}, name="na",
        compiler_params=_cp(("parallel", "parallel", "arbitrary")))(pr, pr, pr, tab, q_w, k_w, ybuf)


def _rope_tables(n_lat, n_ctx, width):
    dh = ML_QK_DIM
    nf = dh // 4
    t = jnp.arange(n_lat)
    inv = ROPE_THETA ** (-jnp.arange(nf, dtype=F32) / nf)
    lane = np.arange(dh)
    pos = jnp.where((lane < dh // 2)[None, :], (t // GRID_W)[:, None], (t % GRID_W)[:, None]).astype(F32)
    ang = pos * inv[lane % nf][None, :]
    sign = np.where((lane % (dh // 2)) < nf, -1.0, 1.0).astype(np.float32)
    cos = jnp.concatenate([jnp.cos(ang), jnp.ones((n_ctx, dh), F32)], axis=0)
    sin = jnp.concatenate([jnp.sin(ang) * sign[None, :], jnp.zeros((n_ctx, dh), F32)], axis=0)
    reps = width // dh
    return jnp.tile(cos, (1, reps)), jnp.tile(sin, (1, reps))


def _ada_mod(c, c_ctx, w, b):
    nb, d = c.shape
    rows = 16
    cv = jnp.concatenate([c, c_ctx[None, :], jnp.zeros((rows - nb - 1, d), F32)], axis=0)
    a = (cv * _sigmoid(cv)).astype(BF16)[None]
    out = _matmul(a, w[None], tm=rows, tn=1024, tk=d, out_dtype=F32, name="ada_mod")[0]
    out = out[:nb + 1] + b[None, :]
    return out.reshape(nb + 1, 6, d).transpose(1, 0, 2)


def _lat_ctx(m, nb):
    ctx = jnp.broadcast_to(m[nb][None, :], (nb, m.shape[1]))
    return jnp.stack([m[:nb], ctx], axis=1)[:, :, None, :]


def _pad_cols(w, n):
    return jnp.pad(w, ((0, 0), (0, n - w.shape[1])))


def _col_row_forms(g, groups):
    nb, t, _ = g.shape
    g5 = g.reshape(nb, t, 2, groups, -1)
    return g5.transpose(0, 2, 3, 1, 4), g5.transpose(0, 2, 3, 4, 1)


ROW_TILES = (1088, 1024, 512, 256)


def _row_tile(rows):
    return next(tm for tm in ROW_TILES if rows % tm == 0)


def _conv_ffn_block(x, mods, norm_w, w_up, conv_w, conv_b, w_down, *, rows, n_lat, segs):
    nb = x.shape[0]
    tm = _row_tile(rows)
    mod = jnp.stack([_lat_ctx(mods[3], nb), _lat_ctx(mods[4], nb)], axis=1)
    u = _rmsmod(x, norm_w[None, :], mod, tm=tm, rows=rows, n_lat=n_lat)
    up = _matmul(u, w_up.astype(BF16)[None], tm=tm, tn=1024, tk=D_MODEL, out_dtype=BF16, name="ffn_up")
    hid = _conv_ffn(up, conv_w, conv_b[None, :], segs=segs, rows=rows)
    return _matmul_gated_residual(hid, w_down.astype(BF16)[None], x, _lat_ctx(mods[5], nb), tm=tm, tn=1024,
                                  tk=D_FF // 4, rows=rows, n_lat=n_lat, name="ffn_down")


def _even_layer(x, mods, norm_w, w_in, conv_w, conv_b, dt_bias, a_log, d_skip, ssd_norm_w,
                hy_w1, hy_b1, hy_w2, hy_b2, hy_w3, hy_freq, hy_bias, w_out, *, n_lat, n_ctx):
    nb, t, d = x.shape
    segs = ((0, n_lat), (n_lat, n_ctx))
    tm = _row_tile(t)
    mod = jnp.stack([_lat_ctx(mods[0], nb), _lat_ctx(mods[1], nb)], axis=1)
    u = _rmsmod(x, norm_w[0][None, :], mod, tm=tm, rows=t, n_lat=n_lat)
    n_main = SSD_WIDTH + EV_CONV_CH
    wb = w_in.astype(BF16)
    pr = _matmul(u, wb[None, :, :n_main], tm=tm, tn=1024, tk=d, out_dtype=BF16, name="ev_in")
    dtp = _matmul(u, _pad_cols(wb[:, n_main:], LANES)[None], tm=tm, tn=LANES, tk=d, out_dtype=F32,
                  name="ev_in_dt")[:, :, :2 * SSD_HEADS]
    cb = conv_b[None, :]
    xbc = _conv_silu(pr, conv_w, cb, xcol=SSD_WIDTH, wcol=0, width=SSD_XBC, segs=segs)
    x0, zin_l, zin_c = _conv_hyena(pr, conv_w, cb, xcol=SSD_WIDTH + SSD_XBC, wcol=SSD_XBC, segs=segs)

    dt_c, dt_r = _col_row_forms(dtp, SSD_GROUPS)
    db = dt_bias.reshape(2, SSD_GROUPS, 1, SSD_HPG)
    al = a_log.reshape(2, SSD_GROUPS, 1, SSD_HPG)
    scan_args = (xbc, dt_c, dt_r, db, db.transpose(0, 1, 3, 2), al, al.transpose(0, 1, 3, 2))
    yf = _ssd(*scan_args, direction=0, n_lat=n_lat)
    dsk = jnp.repeat(d_skip, SSD_HEAD_DIM)[None, :]
    ybuf = _ssd(*scan_args, direction=1, n_lat=n_lat, final_args=(yf, pr, dsk, ssd_norm_w[None, :]),
                out_width=SSD_WIDTH + HY_WIDTH)

    hyb = hy_bias[None, :]
    filt = functools.partial(_hyena_filters, w1=hy_w1, b1=hy_b1, w2=hy_w2, b2=hy_b2, w3=hy_w3, freq=hy_freq)
    ybuf = _hyena_longconv(zin_l, x0, filt(n_lat), hyb, ybuf, row0=0)
    ybuf = _hyena_longconv(zin_c, x0, filt(n_ctx), hyb, ybuf, row0=n_lat)
    return _matmul_gated_residual(ybuf, w_out.astype(BF16)[None], x, _lat_ctx(mods[2], nb), tm=tm, tn=1024,
                                  tk=2048, rows=t, n_lat=n_lat, name="ev_out")


def _odd_layer_latent(x, mods, norm_w, w_in, conv_w, conv_b, gate_b, ml_norm_w, q_norm_w, k_norm_w, rpb, w_out,
                      *, n_lat, n_ctx):
    nb, t, d = x.shape
    segs = ((0, n_lat), (n_lat, n_ctx))
    tm = _row_tile(t)
    mod = jnp.stack([_lat_ctx(mods[0], nb), _lat_ctx(mods[1], nb)], axis=1)
    u = _rmsmod(x, norm_w[0][None, :], mod, tm=tm, rows=t, n_lat=n_lat)
    o3 = 2 * ML_QK_WIDTH + 2 * ML_WIDTH
    o4 = o3 + 4 * ML_HEADS
    wb = w_in.astype(BF16)
    w_main = jnp.concatenate([wb[:, :o3], wb[:, o4:]], axis=1)
    pr = _matmul(u, w_main[None], tm=tm, tn=1024, tk=d, out_dtype=BF16, name="od_in")
    gates = _matmul(u, _pad_cols(wb[:, o3:o4], LANES)[None], tm=tm, tn=LANES, tk=d, out_dtype=F32,
                    name="od_in_gates")[:, :, :4 * ML_HEADS] + gate_b.reshape(-1)
    cos, sin = _rope_tables(n_lat, n_ctx, 256)
    qk = _conv_rope(pr, conv_w, conv_b[None, :], cos, sin, width=2 * ML_QK_WIDTH, segs=segs)
    g5 = gates.reshape(nb, t, 2, 2 * ML_HEADS)
    g_c, g_r = g5.transpose(0, 2, 1, 3), g5.transpose(0, 2, 3, 1)
    hf = _mlstm(qk, pr, g_c, g_r, direction=0, n_lat=n_lat)
    ybuf = _mlstm(qk, pr, g_c, g_r, direction=1, n_lat=n_lat, final_args=(hf, ml_norm_w[None, :]),
                  out_width=ML_WIDTH + NA_WIDTH)
    tab = _na_bias_table(rpb, n_lat // GRID_W)
    ybuf = _na(pr, tab, q_norm_w[None, :], k_norm_w[None, :], ybuf, n_lat=n_lat, n_ctx=n_ctx, qcol=o3,
               out_col=ML_WIDTH)
    return _matmul_gated_residual(ybuf, w_out.astype(BF16)[None], x, _lat_ctx(mods[2], nb), tm=_row_tile(n_lat),
                                  tn=1024, tk=2048, rows=n_lat, n_lat=n_lat, name="od_out")


def kernel(x, c, ctx, c_ctx, ada_w, ada_b, norm_w, ev_w_in, ev_conv_w, ev_conv_b, ssd_dt_bias, ssd_a_log, ssd_d,
           ssd_norm_w, hy_w1, hy_b1, hy_w2, hy_b2, hy_w3, hy_freq, hy_bias, ev_w_out, od_w_in, ml_conv_w, ml_conv_b,
           ml_gate_b, ml_norm_w, na_q_norm_w, na_k_norm_w, na_rpb, od_w_out, ffn_w_up, ffn_conv_w, ffn_conv_b,
           ffn_w_down):
    depth = ada_w.shape[0]
    assert depth == 2, "layer schedule below is written for one even and one odd layer"
    n_lat, n_ctx = x.shape[1], ctx.shape[1]
    t = n_lat + n_ctx
    both = ((0, n_lat), (n_lat, n_ctx))
    xs = jnp.concatenate([x, ctx], axis=1)

    mods = _ada_mod(c, c_ctx, ada_w[0], ada_b[0])
    xs = _even_layer(xs, mods, norm_w[0], ev_w_in[0], ev_conv_w[0], ev_conv_b[0], ssd_dt_bias[0], ssd_a_log[0],
                     ssd_d[0], ssd_norm_w[0], hy_w1[0], hy_b1[0], hy_w2[0], hy_b2[0], hy_w3[0], hy_freq[0],
                     hy_bias[0], ev_w_out[0], n_lat=n_lat, n_ctx=n_ctx)
    xs = _conv_ffn_block(xs, mods, norm_w[0, 1], ffn_w_up[0], ffn_conv_w[0], ffn_conv_b[0], ffn_w_down[0],
                         rows=t, n_lat=n_lat, segs=both)

    mods = _ada_mod(c, c_ctx, ada_w[1], ada_b[1])
    xl = _odd_layer_latent(xs, mods, norm_w[1], od_w_in[0], ml_conv_w[0], ml_conv_b[0], ml_gate_b[0], ml_norm_w[0],
                           na_q_norm_w[0], na_k_norm_w[0], na_rpb[0], od_w_out[0], n_lat=n_lat, n_ctx=n_ctx)
    return _conv_ffn_block(xl, mods, norm_w[1, 1], ffn_w_up[1], ffn_conv_w[1], ffn_conv_b[1], ffn_w_down[1],
                           rows=n_lat, n_lat=n_lat, segs=((0, n_lat),))
```

```python
import functools
import math

import numpy as np
import jax
import jax.numpy as jnp
from jax import lax
from jax.experimental import pallas as pl
from jax.experimental.pallas import tpu as pltpu

F32 = jnp.float32
BF16 = jnp.bfloat16
HI = lax.Precision.HIGHEST

D_MODEL = 2048
SEQ = 4096
CTX_LEN = 256
GRID_W = 64
EPS = 1e-6
CONV_W = 3
SSD_WIDTH = D_MODEL
SSD_HEAD_DIM = 64
SSD_HEADS = SSD_WIDTH // SSD_HEAD_DIM
SSD_STATE = 128
SSD_GROUPS = 4
SSD_HPG = SSD_HEADS // SSD_GROUPS
SSD_GW = SSD_WIDTH // SSD_GROUPS
SSD_XBC = SSD_WIDTH + 2 * SSD_GROUPS * SSD_STATE
HY_WIDTH = D_MODEL
HY_EMB = 33
HY_BANDS = (HY_EMB - 1) // 2
HY_ORDER = 64
HY_FAST = 0.3
HY_SLOW = 1.5
HY_TARGET = 1e-2
ML_HEADS = 8
ML_WIDTH = D_MODEL
ML_V_DIM = ML_WIDTH // ML_HEADS
ML_QK_DIM = ML_V_DIM // 2
ML_QK_WIDTH = ML_HEADS * ML_QK_DIM
ROPE_THETA = 10000.0
NA_WIDTH = D_MODEL
NA_HEAD_DIM = 128
NA_HEADS = NA_WIDTH // NA_HEAD_DIM
NA_ROWS = 8
NA_COLS = 16
NA_QROWS = 8
NA_KROWS = 16
D_FF = 256 * ((8 * D_MODEL // 3 + 255) // 256)
EV_CONV_CH = SSD_XBC + 3 * HY_WIDTH
SCAN_Q = 128
LANES = 128
PAD = 8
VMEM_LIMIT = 48 * 1024 * 1024


def _cp(sem, vmem=VMEM_LIMIT):
    return pltpu.CompilerParams(dimension_semantics=sem, vmem_limit_bytes=vmem)


def _sigmoid(x):
    return 1.0 / (1.0 + jnp.exp(-x))


def _silu(x):
    return x * _sigmoid(x)


def _softplus(x):
    return jnp.maximum(x, 0.0) + jnp.log(1.0 + jnp.exp(-jnp.abs(x)))


def _log_sigmoid(x):
    return -_softplus(-x)


def _dot(a, b, precision=None):
    return jnp.dot(a, b, preferred_element_type=F32, precision=precision)


def _dot_nt(a, b):
    return lax.dot_general(a, b, (((1,), (1,)), ((), ())), preferred_element_type=F32)


def _split_bf16(x, terms):
    out = []
    for _ in range(terms):
        part = x.astype(BF16)
        out.append(part)
        x = x - part.astype(F32)
    return out


def _dot_sel(x, sel, terms):
    selb = sel.astype(BF16)
    return sum(_dot(part, selb) for part in _split_bf16(x, terms))


def _sel_dot(sel, x, terms):
    selb = sel.astype(BF16)
    return sum(_dot(selb, part) for part in _split_bf16(x, terms))


def _mm_body(*refs, nk, n_extra, epilogue, has_buf, cache_w):
    a_ref, w_ref = refs[0], refs[1]
    extra = refs[2:2 + n_extra]
    pos = 2 + n_extra + (1 if has_buf else 0)
    o_ref = refs[pos]
    row_blk = pl.program_id(2)
    if cache_w:
        wb_ref = refs[pos + 1]

        @pl.when((pl.program_id(1) == 0) & (row_blk == 0))
        def _():
            wb_ref[...] = w_ref[...].astype(BF16)

        w_val = wb_ref[...]
    else:
        w_val = w_ref[...].astype(BF16)
    part = _dot(a_ref[...].astype(BF16), w_val)

    def finish(acc):
        val = acc if epilogue is None else epilogue(acc, row_blk, *extra)
        o_ref[...] = val.astype(o_ref.dtype)

    if nk == 1:
        finish(part)
    else:
        acc_ref = refs[pos + 1]
        k = pl.program_id(3)

        @pl.when(k == 0)
        def _():
            acc_ref[...] = part

        @pl.when(k > 0)
        def _():
            acc_ref[...] += part

        @pl.when(k == nk - 1)
        def _():
            finish(acc_ref[...])


def _matmul(a, w, *, tm, tn, tk, out_dtype, rows=None, cols=None, w_index=None, extra=(), epilogue=None,
            out_buf=None, out_row_blk=0, out_col_blk=0, name="mm"):
    ba, ta, kdim = a.shape
    bw, kdim2, n = w.shape
    assert kdim == kdim2
    nb = ba if w_index is not None else max(ba, bw)
    rows = ta if rows is None else rows
    n = n if cols is None else cols
    assert rows % tm == 0 and n % tn == 0 and kdim % tk == 0
    ni, nj, nk = rows // tm, n // tn, kdim // tk
    a_map = (lambda j, b, i, k: (b, i, k)) if ba > 1 else (lambda j, b, i, k: (0, i, k))
    if w_index is not None:
        w_map = lambda j, b, i, k: (w_index, k, j)
    else:
        w_map = (lambda j, b, i, k: (b, k, j)) if bw > 1 else (lambda j, b, i, k: (0, k, j))
    cache_w = w.dtype != BF16 and nk == 1 and (w_index is not None or bw == 1)
    in_specs = [pl.BlockSpec((None, tm, tk), a_map), pl.BlockSpec((None, tk, tn), w_map)]
    args = [a, w]
    for arr, spec in extra:
        args.append(arr)
        in_specs.append(spec)
    aliases = {}
    if out_buf is not None:
        aliases = {len(args): 0}
        args.append(out_buf)
        in_specs.append(pl.BlockSpec(memory_space=pl.ANY))
        out_shape = jax.ShapeDtypeStruct(out_buf.shape, out_buf.dtype)
    else:
        out_shape = jax.ShapeDtypeStruct((nb, rows, n), out_dtype)
    out_spec = pl.BlockSpec((None, tm, tn), lambda j, b, i, k: (b, i + out_row_blk, j + out_col_blk))
    scratch = [pltpu.VMEM((tm, tn), F32)] if nk > 1 else []
    if cache_w:
        scratch = [pltpu.VMEM((tk, tn), BF16)]
    body = functools.partial(_mm_body, nk=nk, n_extra=len(extra), epilogue=epilogue,
                             has_buf=out_buf is not None, cache_w=cache_w)
    inner = "arbitrary" if cache_w else "parallel"
    return pl.pallas_call(
        body, out_shape=out_shape, grid=(nj, nb, ni, nk), in_specs=in_specs, out_specs=out_spec,
        scratch_shapes=scratch, input_output_aliases=aliases, name=name,
        compiler_params=_cp(("parallel", inner, inner, "arbitrary")))(*args)


def _row_is_lat(row_blk, tm, n_lat):
    rows = row_blk * tm + lax.broadcasted_iota(jnp.int32, (tm, 1), 0)
    return rows < n_lat


def _gated_residual_epilogue(acc, row_blk, res_ref, gate_ref, *, tm, n_lat):
    gate = jnp.where(_row_is_lat(row_blk, tm, n_lat), gate_ref[0], gate_ref[1])
    return res_ref[...] + gate * acc


def _matmul_gated_residual(a, w, res, gate, *, w_index, tm, tn, tk, rows, n_lat, name):
    extra = [(res, pl.BlockSpec((None, tm, tn), lambda j, b, i, k: (b, i, j))),
             (gate, pl.BlockSpec((None, 2, 1, tn), lambda j, b, i, k: (b, 0, 0, j)))]
    epi = functools.partial(_gated_residual_epilogue, tm=tm, n_lat=n_lat)
    return _matmul(a, w, w_index=w_index, tm=tm, tn=tn, tk=tk, out_dtype=F32, rows=rows, extra=extra,
                   epilogue=epi, name=name)


def _rmsmod_body(x_ref, w_ref, mod_ref, o_ref, *, tm, n_lat):
    x = x_ref[...]
    y = x * lax.rsqrt(jnp.mean(x * x, axis=-1, keepdims=True) + EPS) * w_ref[...]
    is_lat = _row_is_lat(pl.program_id(2), tm, n_lat)
    shift = jnp.where(is_lat, mod_ref[0, 0], mod_ref[0, 1])
    scale = jnp.where(is_lat, mod_ref[1, 0], mod_ref[1, 1])
    o_ref[...] = (y * (1.0 + scale) + shift).astype(o_ref.dtype)


def _rmsmod(x, w, mod, *, tm, rows, n_lat):
    nb, _, d = x.shape
    tm = tm // 4
    assert rows % tm == 0 and tm % 16 == 0
    body = functools.partial(_rmsmod_body, tm=tm, n_lat=n_lat)
    return pl.pallas_call(
        body, out_shape=jax.ShapeDtypeStruct((nb, rows, d), BF16), grid=(1, nb, rows // tm),
        in_specs=[pl.BlockSpec((None, tm, d), lambda j, b, i: (b, i, 0)),
                  pl.BlockSpec((1, d), lambda j, b, i: (0, 0)),
                  pl.BlockSpec((None, 2, 2, 1, d), lambda j, b, i: (b, 0, 0, 0, 0))],
        out_specs=pl.BlockSpec((None, tm, d), lambda j, b, i: (b, i, 0)),
        name="rmsmod", compiler_params=_cp(("parallel", "parallel", "parallel")))(x, w, mod)


CONV_CH = 128


def _seg_bases(segs):
    bases, pos = [], 0
    for _, n in segs:
        bases.append(pos)
        pos += PAD + n
    return bases, pos + PAD


def _conv_fill(s_ref, x_ref, segs):
    bases, total = _seg_bases(segs)
    zeros = jnp.zeros((PAD, s_ref.shape[1]), F32)
    for (r0, n), base in zip(segs, bases):
        s_ref[base:base + PAD, :] = zeros

        def body(i, carry, r0=r0, base=base):
            t0 = pl.multiple_of(i * CONV_CH, CONV_CH)
            s_ref[pl.ds(base + PAD + t0, CONV_CH), :] = x_ref[pl.ds(r0 + t0, CONV_CH), :].astype(F32)
            return carry

        lax.fori_loop(0, n // CONV_CH, body, 0)
    s_ref[total - PAD:total, :] = zeros


def _conv_at(s_ref, w_ref, b_ref, base, t0):
    big = s_ref[pl.ds(base + t0, CONV_CH + 2 * PAD), :]
    return (w_ref[0:1, :] * big[PAD - 1:PAD - 1 + CONV_CH]
            + w_ref[1:2, :] * big[PAD:PAD + CONV_CH]
            + w_ref[2:3, :] * big[PAD + 1:PAD + 1 + CONV_CH] + b_ref[...])


def _conv_loop(segs, fn):
    bases, _ = _seg_bases(segs)
    for si, ((r0, n), base) in enumerate(zip(segs, bases)):
        def body(i, carry, r0=r0, base=base, si=si):
            t0 = pl.multiple_of(i * CONV_CH, CONV_CH)
            fn(si, r0, base, t0)
            return carry

        lax.fori_loop(0, n // CONV_CH, body, 0)


def _conv_silu_body(x_ref, w_ref, b_ref, o_ref, s_ref, *, segs):
    _conv_fill(s_ref, x_ref, segs)

    def fn(si, r0, base, t0):
        o_ref[pl.ds(r0 + t0, CONV_CH), :] = _silu(_conv_at(s_ref, w_ref, b_ref, base, t0)).astype(o_ref.dtype)

    _conv_loop(segs, fn)


def _conv_rope_body(x_ref, w_ref, b_ref, cos_ref, sin_ref, o_ref, s_ref, *, segs, tc):
    _conv_fill(s_ref, x_ref, segs)
    lane = lax.broadcasted_iota(jnp.int32, (CONV_CH, tc), 1)
    first = (lane % (ML_QK_DIM // 2)) < (ML_QK_DIM // 4)
    nf = ML_QK_DIM // 4

    def fn(si, r0, base, t0):
        y = _silu(_conv_at(s_ref, w_ref, b_ref, base, t0))
        partner = jnp.where(first, pltpu.roll(y, tc - nf, 1), pltpu.roll(y, nf, 1))
        rows = pl.ds(r0 + t0, CONV_CH)
        o_ref[rows, :] = (y * cos_ref[rows, :] + partner * sin_ref[rows, :]).astype(o_ref.dtype)

    _conv_loop(segs, fn)


def _conv_ffn_body(a_ref, x_ref, w_ref, b_ref, o_ref, s_ref, *, segs):
    _conv_fill(s_ref, x_ref, segs)

    def fn(si, r0, base, t0):
        rows = pl.ds(r0 + t0, CONV_CH)
        g = _silu(_conv_at(s_ref, w_ref, b_ref, base, t0))
        o_ref[rows, :] = (a_ref[rows, :].astype(F32) * g).astype(o_ref.dtype)

    _conv_loop(segs, fn)


def _conv_hyena_body(x0_ref, x1_ref, v_ref, w0_ref, b0_ref, w1_ref, b1_ref, w2_ref, b2_ref,
                     ox0_ref, ozl_ref, ozc_ref, s0_ref, s1_ref, s2_ref, *, segs):
    _conv_fill(s0_ref, x0_ref, segs)
    _conv_fill(s1_ref, x1_ref, segs)
    _conv_fill(s2_ref, v_ref, segs)
    outs = (ozl_ref, ozc_ref)

    def fn(si, r0, base, t0):
        ox0_ref[pl.ds(r0 + t0, CONV_CH), :] = _conv_at(s0_ref, w0_ref, b0_ref, base, t0).astype(ox0_ref.dtype)
        zin = _conv_at(s1_ref, w1_ref, b1_ref, base, t0) * _conv_at(s2_ref, w2_ref, b2_ref, base, t0)
        outs[si][pl.ds(t0, CONV_CH), :] = zin.astype(outs[si].dtype)

    _conv_loop(segs, fn)


def _conv_specs(t_in, tc, xcol_blk, wcol_blk):
    return [pl.BlockSpec((None, t_in, tc), lambda b, j: (b, 0, j + xcol_blk)),
            pl.BlockSpec((CONV_W, tc), lambda b, j: (0, j + wcol_blk)),
            pl.BlockSpec((1, tc), lambda b, j: (0, j + wcol_blk))]


def _conv_silu(x, w, b, *, xcol, wcol, width, segs, tc=256):
    nb, t_in, _ = x.shape
    _, total = _seg_bases(segs)
    return pl.pallas_call(
        functools.partial(_conv_silu_body, segs=segs),
        out_shape=jax.ShapeDtypeStruct((nb, t_in, width), BF16), grid=(nb, width // tc),
        in_specs=_conv_specs(t_in, tc, xcol // tc, wcol // tc),
        out_specs=pl.BlockSpec((None, t_in, tc), lambda b, j: (b, 0, j)),
        scratch_shapes=[pltpu.VMEM((total, tc), F32)],
        name="conv_silu", compiler_params=_cp(("parallel", "parallel")))(x, w, b)


def _conv_rope(x, w, b, cos, sin, *, width, segs, tc=256):
    nb, t_in, _ = x.shape
    _, total = _seg_bases(segs)
    tab = pl.BlockSpec((t_in, tc), lambda b, j: (0, 0))
    return pl.pallas_call(
        functools.partial(_conv_rope_body, segs=segs, tc=tc),
        out_shape=jax.ShapeDtypeStruct((nb, t_in, width), BF16), grid=(nb, width // tc),
        in_specs=_conv_specs(t_in, tc, 0, 0) + [tab, tab],
        out_specs=pl.BlockSpec((None, t_in, tc), lambda b, j: (b, 0, j)),
        scratch_shapes=[pltpu.VMEM((total, tc), F32)],
        name="conv_rope", compiler_params=_cp(("parallel", "parallel")))(x, w, b, cos, sin)


def _conv_ffn(up, w, b, *, segs, rows, tc=256):
    nb, t_in, two_ff = up.shape
    ff = two_ff // 2
    _, total = _seg_bases(segs)
    return pl.pallas_call(
        functools.partial(_conv_ffn_body, segs=segs),
        out_shape=jax.ShapeDtypeStruct((nb, rows, ff), BF16), grid=(nb, ff // tc),
        in_specs=[pl.BlockSpec((None, rows, tc), lambda b, j: (b, 0, j))] + _conv_specs(rows, tc, ff // tc, 0),
        out_specs=pl.BlockSpec((None, rows, tc), lambda b, j: (b, 0, j)),
        scratch_shapes=[pltpu.VMEM((total, tc), F32)],
        name="conv_ffn", compiler_params=_cp(("parallel", "parallel")))(up, up, w, b)


def _conv_hyena(pr, w, b, *, xcol, wcol, segs, tc=256):
    nb, t_in, _ = pr.shape
    (_, n_lat), (_, n_ctx) = segs
    _, total = _seg_bases(segs)
    wd = HY_WIDTH
    in_specs = [pl.BlockSpec((None, t_in, tc), functools.partial(lambda b, j, o: (b, 0, j + o), o=(xcol + m * wd) // tc))
                for m in range(3)]
    wargs = []
    for m in range(3):
        off = (wcol + m * wd) // tc
        in_specs.append(pl.BlockSpec((CONV_W, tc), functools.partial(lambda b, j, o: (0, j + o), o=off)))
        in_specs.append(pl.BlockSpec((1, tc), functools.partial(lambda b, j, o: (0, j + o), o=off)))
        wargs += [w, b]
    return pl.pallas_call(
        functools.partial(_conv_hyena_body, segs=segs),
        out_shape=(jax.ShapeDtypeStruct((nb, t_in, wd), BF16), jax.ShapeDtypeStruct((nb, n_lat, wd), BF16),
                   jax.ShapeDtypeStruct((nb, n_ctx, wd), BF16)),
        grid=(nb, wd // tc), in_specs=in_specs,
        out_specs=(pl.BlockSpec((None, t_in, tc), lambda b, j: (b, 0, j)),
                   pl.BlockSpec((None, n_lat, tc), lambda b, j: (b, 0, j)),
                   pl.BlockSpec((None, n_ctx, tc), lambda b, j: (b, 0, j))),
        scratch_shapes=[pltpu.VMEM((total, tc), F32)] * 3,
        name="conv_hyena", compiler_params=_cp(("parallel", "parallel")))(pr, pr, pr, *wargs)


def _scan_masks(reverse):
    q = SCAN_Q
    ii = lax.broadcasted_iota(jnp.int32, (q, q), 0)
    jj = lax.broadcasted_iota(jnp.int32, (q, q), 1)
    mask = (jj >= ii) if reverse else (jj <= ii)
    mask_t = (jj <= ii) if reverse else (jj >= ii)
    return mask, mask.astype(F32), mask_t.astype(F32)


def _chunk_map(reverse, n_chunks, n_lat_chunks):
    if reverse:
        return lambda k: n_chunks - 1 - k
    return lambda k: (k + n_lat_chunks) % n_chunks


def _ssd_body(*refs, reverse, final):
    (xs_ref, b_ref, c_ref, dtc_ref, dtr_ref, dbc_ref, dbr_ref, alc_ref, alr_ref) = refs[:9]
    if final:
        yf_ref, z_ref, dsk_ref, nw_ref, o_ref, h_ref = refs[9:]
    else:
        o_ref, h_ref = refs[9:]
    q, gw, hd = SCAN_Q, SSD_GW, SSD_HEAD_DIM

    @pl.when(pl.program_id(2) == 0)
    def _():
        h_ref[...] = jnp.zeros_like(h_ref)

    xs = xs_ref[...].astype(F32)
    bm = b_ref[...]
    cm = c_ref[...]
    dt_c = _softplus(dtc_ref[...] + dbc_ref[...])
    dt_r = _softplus(dtr_ref[...] + dbr_ref[...])
    da_c = dt_c * (-jnp.exp(alc_ref[...]))
    da_r = dt_r * (-jnp.exp(alr_ref[...]))
    mask, maskf, mask_tf = _scan_masks(reverse)
    acum_c = _sel_dot(maskf, da_c, 3)
    acum_r = _dot_sel(da_r, mask_tf, 3)
    atot = jnp.sum(da_c, axis=0, keepdims=True)

    er = lax.broadcasted_iota(jnp.int32, (SSD_HPG, gw), 0)
    ec = lax.broadcasted_iota(jnp.int32, (SSD_HPG, gw), 1)
    expand = (ec // hd == er).astype(F32)
    stack = jnp.concatenate([dt_c, jnp.exp(atot - acum_c), jnp.exp(acum_c),
                             jnp.broadcast_to(jnp.exp(atot), (PAD, SSD_HPG))], axis=0)
    wide = _dot_sel(stack, expand, 2)
    dt_e, ws_e, eac_e, eat_e = wide[0:q], wide[q:2 * q], wide[2 * q:3 * q], wide[3 * q:3 * q + 1]

    xq = xs * dt_e
    h_prev = h_ref[...]
    cb = _dot_nt(cm, bm)
    y = eac_e * _dot(cm, h_prev.astype(BF16))
    lane = lax.broadcasted_iota(jnp.int32, (q, 2 * hd), 1)
    pieces = []
    for p in range(SSD_HPG // 2):
        ms = []
        for r in (2 * p, 2 * p + 1):
            seg = acum_c[:, r:r + 1] - acum_r[r:r + 1, :]
            ms.append((cb * jnp.where(mask, jnp.exp(seg), 0.0)).astype(BF16))
        xp = xq[:, 2 * hd * p:2 * hd * (p + 1)]
        rhs = jnp.concatenate([jnp.where(lane < hd, xp, 0.0), jnp.where(lane >= hd, xp, 0.0)], axis=0)
        pieces.append(_dot(jnp.concatenate(ms, axis=1), rhs.astype(BF16)))
    y = y + jnp.concatenate(pieces, axis=1)

    bt = bm.astype(F32).T.astype(BF16)
    h_ref[...] = eat_e * h_prev + _dot(bt, (xq * ws_e).astype(BF16))

    if final:
        ytot = y + yf_ref[...] + xs * dsk_ref[...]
        g = ytot * _silu(z_ref[...].astype(F32))
        o = g * lax.rsqrt(jnp.mean(g * g, axis=-1, keepdims=True) + EPS) * nw_ref[...]
        o_ref[...] = o.astype(o_ref.dtype)
    else:
        o_ref[...] = y


def _ssd(xbc, dt_c, dt_r, db_c, db_r, al_c, al_r, *, direction, n_lat, final_args=None, out_width=None):
    nb, t, _ = xbc.shape
    q, gw, hpg = SCAN_Q, SSD_GW, SSD_HPG
    nch = t // q
    reverse = direction == 1
    cmap = _chunk_map(reverse, nch, n_lat // q)
    nblk = SSD_STATE // LANES
    b0 = SSD_WIDTH // SSD_STATE
    c0 = b0 + SSD_GROUPS * nblk
    d = direction
    in_specs = [
        pl.BlockSpec((None, q, gw), lambda b, g, k: (b, cmap(k), g)),
        pl.BlockSpec((None, q, SSD_STATE), lambda b, g, k: (b, cmap(k), b0 + g)),
        pl.BlockSpec((None, q, SSD_STATE), lambda b, g, k: (b, cmap(k), c0 + g)),
        pl.BlockSpec((None, None, None, q, hpg), lambda b, g, k: (b, d, g, cmap(k), 0)),
        pl.BlockSpec((None, None, None, hpg, q), lambda b, g, k: (b, d, g, 0, cmap(k))),
        pl.BlockSpec((None, None, 1, hpg), lambda b, g, k: (d, g, 0, 0)),
        pl.BlockSpec((None, None, hpg, 1), lambda b, g, k: (d, g, 0, 0)),
        pl.BlockSpec((None, None, 1, hpg), lambda b, g, k: (d, g, 0, 0)),
        pl.BlockSpec((None, None, hpg, 1), lambda b, g, k: (d, g, 0, 0)),
    ]
    args = [xbc, xbc, xbc, dt_c, dt_r, db_c, db_r, al_c, al_r]
    final = final_args is not None
    if final:
        yf, pr, dskip, norm_w = final_args
        in_specs += [pl.BlockSpec((None, q, gw), lambda b, g, k: (b, cmap(k), g)),
                     pl.BlockSpec((None, q, gw), lambda b, g, k: (b, cmap(k), g)),
                     pl.BlockSpec((1, gw), lambda b, g, k: (0, g)),
                     pl.BlockSpec((1, gw), lambda b, g, k: (0, g))]
        args += [yf, pr, dskip, norm_w]
        out_shape = jax.ShapeDtypeStruct((nb, t, out_width), BF16)
    else:
        out_shape = jax.ShapeDtypeStruct((nb, t, SSD_WIDTH), F32)
    return pl.pallas_call(
        functools.partial(_ssd_body, reverse=reverse, final=final),
        out_shape=out_shape, grid=(nb, SSD_GROUPS, nch), in_specs=in_specs,
        out_specs=pl.BlockSpec((None, q, gw), lambda b, g, k: (b, cmap(k), g)),
        scratch_shapes=[pltpu.VMEM((SSD_STATE, gw), F32)],
        name="ssd_bwd" if reverse else "ssd_fwd",
        compiler_params=_cp(("parallel", "parallel", "arbitrary")))(*args)


def _hyfilt_body(f_ref, w1_ref, b1_ref, w2_ref, b2_ref, w3_ref, fr_ref, dl_ref, o_ref, *, tl):
    f = f_ref[...]
    h = jnp.sin(fr_ref[0:1, :] * (_dot(f, w1_ref[...], HI) + b1_ref[...]))
    h = jnp.sin(fr_ref[1:2, :] * (_dot(h, w2_ref[...], HI) + b2_ref[...]))
    h = _dot(h, w3_ref[...], HI)
    dec = jnp.exp(-f[:, 0:1] * dl_ref[...])
    row = pl.program_id(0) * tl + lax.broadcasted_iota(jnp.int32, (tl, 1), 0)
    wd = HY_WIDTH
    o_ref[:, 0:wd] = (h[:, 0:wd] * dec).astype(o_ref.dtype)
    o_ref[:, wd:2 * wd] = jnp.where(row == 0, 0.0, h[:, wd:2 * wd] * dec).astype(o_ref.dtype)


def _hyena_filters(length, w1, b1, w2, b2, w3, freq, *, tl=256):
    p = LANES
    t = jnp.linspace(0.0, 1.0, length, dtype=F32)[:, None]
    w = 2.0 * math.pi * jnp.arange(length, dtype=F32)[:, None] / length
    f = jnp.linspace(1e-4, HY_BANDS - 1, HY_BANDS, dtype=F32)[None, :]
    feats = jnp.concatenate([t, jnp.cos(f * w), -jnp.sin(f * w)], axis=-1)
    feats = jnp.pad(feats, ((0, 0), (0, p - HY_EMB)))
    po = p - HY_ORDER
    w1p = jnp.pad(w1, ((0, p - HY_EMB), (0, po)))
    w2p = jnp.pad(w2, ((0, po), (0, po)))
    w3p = jnp.pad(w3, ((0, po), (0, 0)))
    b1p = jnp.pad(b1[None, :], ((0, 0), (0, po)))
    b2p = jnp.pad(b2[None, :], ((0, 0), (0, po)))
    frp = jnp.pad(freq, ((0, 0), (0, po)))
    deltas = jnp.abs(jnp.linspace(math.log(HY_TARGET) / HY_FAST, math.log(HY_TARGET) / HY_SLOW, HY_WIDTH,
                                  dtype=F32))[None, :]
    full = lambda shape: pl.BlockSpec(shape, lambda i: (0,) * len(shape))
    out = pl.pallas_call(
        functools.partial(_hyfilt_body, tl=tl),
        out_shape=jax.ShapeDtypeStruct((length, 2 * HY_WIDTH), BF16), grid=(length // tl,),
        in_specs=[pl.BlockSpec((tl, p), lambda i: (i, 0)), full((p, p)), full((1, p)), full((p, p)), full((1, p)),
                  full((p, 2 * HY_WIDTH)), full((2, p)), full((1, HY_WIDTH))],
        out_specs=pl.BlockSpec((tl, 2 * HY_WIDTH), lambda i: (i, 0)),
        name="hyena_filters", compiler_params=_cp(("parallel",)))(feats, w1p, b1p, w2p, b2p, w3p, frp, deltas)
    return out[None]


def _dft_matrices(length):
    n = 2 * length
    k = jnp.arange(length, dtype=jnp.int32)[:, None]
    t = jnp.arange(length, dtype=jnp.int32)[None, :]
    step = 1 << (int(math.log2(length)) // 2)
    part = jnp.arange(length // step, dtype=jnp.int32)[:, None]
    ang_hi = ((part * step * t) % n).astype(F32) * (2.0 * math.pi / n)
    ang_lo = ((jnp.arange(step, dtype=jnp.int32)[:, None] * t) % n).astype(F32) * (2.0 * math.pi / n)
    ch, sh = jnp.cos(ang_hi)[:, None, :], jnp.sin(ang_hi)[:, None, :]
    cl, sl = jnp.cos(ang_lo)[None, :, :], jnp.sin(ang_lo)[None, :, :]
    c = (ch * cl - sh * sl).reshape(length, length)
    s = (sh * cl + ch * sl).reshape(length, length)
    alt_t = jnp.where(t % 2 == 0, 1.0, -1.0).astype(F32)
    fwd = jnp.concatenate([c, jnp.where(k == 0, alt_t, -s)], axis=0)
    alt_r = jnp.where(k % 2 == 0, 1.0, -1.0).astype(F32)
    inv_re = jnp.where(t == 0, 1.0 / n, (2.0 / n) * c)
    inv_im = jnp.where(t == 0, alt_r / n, (-2.0 / n) * s)
    inv = jnp.concatenate([inv_re, inv_im], axis=1)
    return fwd.astype(BF16)[None], inv.astype(BF16)[None]


def _specmul_body(z_ref, kf_ref, kb_ref, o_ref, *, tr):
    zr, zi = z_ref[0], z_ref[1]
    kr = kf_ref[0] + kb_ref[0]
    ki = kf_ref[1] - kb_ref[1]
    row = pl.program_id(1) * tr + lax.broadcasted_iota(jnp.int32, (tr, 1), 0)
    first = row == 0
    yr = jnp.where(first, zr * kr, zr * kr - zi * ki)
    yi = jnp.where(first, zi * (kf_ref[1] + kb_ref[1]), zr * ki + zi * kr)
    o_ref[0] = yr.astype(o_ref.dtype)
    o_ref[1] = yi.astype(o_ref.dtype)


def _specmul(z, ksp, *, tr, tc=512):
    nb, l2, c = z.shape
    ln = l2 // 2
    z4 = z.reshape(nb, 2, ln, c)
    k3 = ksp.reshape(2, ln, 2 * c)
    out = pl.pallas_call(
        functools.partial(_specmul_body, tr=tr),
        out_shape=jax.ShapeDtypeStruct((nb, 2, ln, c), BF16), grid=(nb, ln // tr, c // tc),
        in_specs=[pl.BlockSpec((None, 2, tr, tc), lambda b, i, j: (b, 0, i, j)),
                  pl.BlockSpec((2, tr, tc), lambda b, i, j: (0, i, j)),
                  pl.BlockSpec((2, tr, tc), lambda b, i, j: (0, i, j + c // tc))],
        out_specs=pl.BlockSpec((None, 2, tr, tc), lambda b, i, j: (b, 0, i, j)),
        name="specmul", compiler_params=_cp(("parallel", "parallel", "parallel")))(z4, k3, k3)
    return out.reshape(nb, l2, c)


def _hyena_out_epilogue(acc, row_blk, x0_ref, zin_ref, db_ref):
    del row_blk
    zin = zin_ref[...].astype(F32)
    return x0_ref[...].astype(F32) * (acc + zin * db_ref[...])


def _hyena_longconv(zin, x0, hfb, dbias, ybuf, *, row0, tn=1024):
    nb, ln, c = zin.shape
    tm_f, tm_i = min(1024, 2 * ln), min(1024, ln)
    tk_f, tk_i = min(2048, ln), min(2048, 2 * ln)
    assert row0 % tm_i == 0
    fwd, inv = _dft_matrices(ln)
    ksp = _matmul(fwd, hfb, tm=tm_f, tn=tn, tk=tk_f, out_dtype=F32, name="hy_filter_dft")
    z = _matmul(fwd, zin, tm=tm_f, tn=tn, tk=tk_f, out_dtype=F32, name="hy_dft")
    y = _specmul(z, ksp, tr=min(ln, 512))
    rb = row0 // tm_i
    extra = [(x0, pl.BlockSpec((None, tm_i, tn), lambda j, b, i, k: (b, i + rb, j))),
             (zin, pl.BlockSpec((None, tm_i, tn), lambda j, b, i, k: (b, i, j))),
             (dbias, pl.BlockSpec((1, tn), lambda j, b, i, k: (0, j)))]
    return _matmul(inv, y, tm=tm_i, tn=tn, tk=tk_i, out_dtype=BF16, extra=extra, epilogue=_hyena_out_epilogue,
                   out_buf=ybuf, out_row_blk=rb, out_col_blk=SSD_WIDTH // tn, name="hy_idft")


def _mlstm_body(*refs, reverse, final):
    q_ref, k_ref, v_ref, gc_ref, gr_ref = refs[:5]
    if final:
        hf_ref, og_ref, nw_ref, o_ref, ct_ref, n_ref, m_ref = refs[5:]
    else:
        o_ref, ct_ref, n_ref, m_ref = refs[5:]
    nh, dk, dv = ML_HEADS, ML_QK_DIM, ML_V_DIM
    scale = dk ** -0.5

    @pl.when(pl.program_id(1) == 0)
    def _():
        ct_ref[...] = jnp.zeros_like(ct_ref)
        n_ref[...] = jnp.zeros_like(n_ref)
        m_ref[...] = jnp.zeros_like(m_ref)

    gc, gr = gc_ref[...], gr_ref[...]
    li_c, lf_c = gc[:, 0:nh], _log_sigmoid(gc[:, nh:2 * nh])
    li_r, lf_r = gr[0:nh, :], _log_sigmoid(gr[nh:2 * nh, :])
    mask, maskf, mask_tf = _scan_masks(reverse)
    bc_c = _sel_dot(maskf, lf_c, 3)
    bc_r = _dot_sel(lf_r, mask_tf, 3)
    btot_c = jnp.sum(lf_c, axis=0, keepdims=True)

    for h in range(nh):
        qh = q_ref[:, dk * h:dk * (h + 1)]
        kh = k_ref[:, dk * h:dk * (h + 1)]
        vh = v_ref[:, dv * h:dv * (h + 1)]
        m_s = m_ref[h:h + 1, 0:1]
        b_c, b_r = bc_c[:, h:h + 1], bc_r[h:h + 1, :]
        i_c, i_r = li_c[:, h:h + 1], li_r[h:h + 1, :]
        btot = btot_c[:, h:h + 1]
        dmat = jnp.where(mask, b_c - b_r + i_r, -jnp.inf)
        inter = b_c + m_s
        m_t = jnp.maximum(inter, jnp.max(dmat, axis=1, keepdims=True))
        s = _dot_nt(qh, kh) * scale * jnp.exp(dmat - m_t)
        dec = jnp.exp(inter - m_t)
        ct = ct_ref[h]
        n_s = n_ref[h:h + 1, :]
        num = _dot(s.astype(BF16), vh) + (dec * scale) * _dot(qh, ct.astype(BF16))
        qn = jnp.sum(qh.astype(F32) * n_s, axis=1, keepdims=True) * scale
        den = jnp.sum(s, axis=1, keepdims=True) + dec * qn
        hh = num / jnp.maximum(jnp.abs(den), jnp.exp(-m_t))

        wst_c = btot - b_c + i_c
        wst_r = btot - b_r + i_r
        m_new = jnp.maximum(btot + m_s, jnp.max(wst_r, axis=1, keepdims=True))
        dstate = jnp.exp(btot + m_s - m_new)
        kw = kh.astype(F32) * jnp.exp(wst_c - m_new)
        ct_ref[h] = dstate * ct + _dot(kw.T.astype(BF16), vh)
        n_ref[h:h + 1, :] = dstate * n_s + jnp.sum(kw, axis=0, keepdims=True)
        m_ref[h:h + 1, :] = jnp.broadcast_to(m_new, (1, LANES))

        cols = slice(dv * h, dv * (h + 1))
        if final:
            hs = hh + hf_ref[:, cols]
            hn = hs * lax.rsqrt(jnp.mean(hs * hs, axis=-1, keepdims=True) + EPS) * nw_ref[:, cols]
            o_ref[:, cols] = (hn * _sigmoid(og_ref[:, cols].astype(F32))).astype(o_ref.dtype)
        else:
            o_ref[:, cols] = hh


def _mlstm(qk, pr, g_c, g_r, *, direction, n_lat, final_args=None, out_width=None):
    nb, t, _ = qk.shape
    q = SCAN_Q
    nch = t // q
    reverse = direction == 1
    cmap = _chunk_map(reverse, nch, n_lat // q)
    d = direction
    in_specs = [
        pl.BlockSpec((None, q, ML_QK_WIDTH), lambda b, k: (b, cmap(k), 0)),
        pl.BlockSpec((None, q, ML_QK_WIDTH), lambda b, k: (b, cmap(k), 1)),
        pl.BlockSpec((None, q, ML_WIDTH), lambda b, k: (b, cmap(k), 2 * ML_QK_WIDTH // ML_WIDTH)),
        pl.BlockSpec((None, None, q, 2 * ML_HEADS), lambda b, k: (b, d, cmap(k), 0)),
        pl.BlockSpec((None, None, 2 * ML_HEADS, q), lambda b, k: (b, d, 0, cmap(k))),
    ]
    args = [qk, qk, pr, g_c, g_r]
    final = final_args is not None
    if final:
        hf, norm_w = final_args
        in_specs += [pl.BlockSpec((None, q, ML_WIDTH), lambda b, k: (b, cmap(k), 0)),
                     pl.BlockSpec((None, q, ML_WIDTH), lambda b, k: (b, cmap(k), 2 * ML_QK_WIDTH // ML_WIDTH + 1)),
                     pl.BlockSpec((1, ML_WIDTH), lambda b, k: (0, 0))]
        args += [hf, pr, norm_w]
        out_shape = jax.ShapeDtypeStruct((nb, t, out_width), BF16)
    else:
        out_shape = jax.ShapeDtypeStruct((nb, t, ML_WIDTH), F32)
    return pl.pallas_call(
        functools.partial(_mlstm_body, reverse=reverse, final=final),
        out_shape=out_shape, grid=(nb, nch), in_specs=in_specs,
        out_specs=pl.BlockSpec((None, q, ML_WIDTH), lambda b, k: (b, cmap(k), 0)),
        scratch_shapes=[pltpu.VMEM((ML_HEADS, ML_QK_DIM, ML_V_DIM), F32), pltpu.VMEM((ML_HEADS, ML_QK_DIM), F32),
                        pltpu.VMEM((ML_HEADS, LANES), F32)],
        name="mlstm_bwd" if reverse else "mlstm_fwd",
        compiler_params=_cp(("parallel", "arbitrary")))(*args)


def _na_window_start(j, rows):
    return jnp.clip(j * NA_QROWS - NA_ROWS // 2, 0, rows - NA_KROWS)


def _na_body(q_ref, k_ref, v_ref, tab_ref, qw_ref, kw_ref, buf_ref, o_ref, kn_ref, *, n_lat, n_ctx):
    del buf_ref
    dh = NA_HEAD_DIM
    scale = dh ** -0.5
    rows = n_lat // GRID_W
    j = pl.program_id(2)
    nq, nk = NA_QROWS * GRID_W, NA_KROWS * GRID_W

    @pl.when(j == 0)
    def _():
        def body(i, carry):
            r = pl.ds(pl.multiple_of(i * LANES, LANES), LANES)
            kk = k_ref[r, :].astype(F32)
            kn = kk * lax.rsqrt(jnp.mean(kk * kk, axis=-1, keepdims=True) + EPS) * kw_ref[...]
            kn_ref[r, :] = kn.astype(kn_ref.dtype)
            return carry

        lax.fori_loop(0, (n_lat + n_ctx) // LANES, body, 0)

    qq = q_ref[...].astype(F32)
    qn = (qq * lax.rsqrt(jnp.mean(qq * qq, axis=-1, keepdims=True) + EPS) * (qw_ref[...] * scale)).astype(BF16)
    start = _na_window_start(j, rows)
    win = pl.ds(pl.multiple_of(start * GRID_W, GRID_W), nk)
    s = _dot_nt(qn, kn_ref[win, :])
    pieces = []
    for qr in range(NA_QROWS):
        d0 = start - (j * NA_QROWS + qr) + NA_ROWS - 1 + NA_QROWS
        pieces.append(s[qr * GRID_W:(qr + 1) * GRID_W] + tab_ref[d0])
    s = jnp.concatenate(pieces, axis=0)
    ctx = pl.ds(n_lat, n_ctx)
    sc = _dot_nt(qn, kn_ref[ctx, :])
    m = jnp.maximum(jnp.max(s, axis=1, keepdims=True), jnp.max(sc, axis=1, keepdims=True))
    p, pc = jnp.exp(s - m), jnp.exp(sc - m)
    denom = jnp.sum(p, axis=1, keepdims=True) + jnp.sum(pc, axis=1, keepdims=True)
    o = _dot(p.astype(BF16), v_ref[win, :]) + _dot(pc.astype(BF16), v_ref[ctx, :])
    o_ref[...] = (o / denom).astype(o_ref.dtype)


def _na_bias_table(rpb, rows):
    nd = 2 * NA_QROWS
    valid = np.zeros((nd, NA_KROWS), bool)
    seen = np.zeros((nd,), bool)
    for j in range(rows // NA_QROWS):
        start = int(np.clip(j * NA_QROWS - NA_ROWS // 2, 0, rows - NA_KROWS))
        for qr in range(NA_QROWS):
            r = j * NA_QROWS + qr
            rs = int(np.clip(r - NA_ROWS // 2, 0, rows - NA_ROWS))
            d = start - r + NA_ROWS - 1 + NA_QROWS
            v = np.array([(rs <= start + i < rs + NA_ROWS) for i in range(NA_KROWS)])
            assert 0 <= d < nd and (not seen[d] or (valid[d] == v).all())
            valid[d], seen[d] = v, True
    col = np.arange(GRID_W)
    cs = np.clip(col - NA_COLS // 2, 0, GRID_W - NA_COLS)
    in_win = (col[None, :] >= cs[:, None]) & (col[None, :] < cs[:, None] + NA_COLS)
    ndr, ndc, w = 2 * NA_ROWS - 1, 2 * NA_COLS - 1, GRID_W
    dr = (np.arange(nd)[:, None] - NA_QROWS) + np.arange(NA_KROWS)[None, :]
    ok = valid[:, None, :, None] & in_win[None, :, None, :] & (dr >= 0)[:, None, :, None] \
        & (dr < ndr)[:, None, :, None]
    ext = jnp.pad(rpb, ((0, 0), (0, 0), (w, w)), mode="edge")
    c0 = w + NA_COLS - 1
    t1 = jnp.stack([ext[:, :, c0 - qc:c0 - qc + w] for qc in range(w)], axis=2)
    t1p = jnp.pad(t1, ((0, 0), (NA_QROWS, nd + NA_KROWS - 1 - NA_QROWS - ndr), (0, 0), (0, 0)))
    wide = jnp.concatenate([t1p[:, m] for m in range(nd + NA_KROWS - 1)], axis=-1)
    vals = jnp.stack([wide[:, :, d * w:(d + NA_KROWS) * w] for d in range(nd)], axis=1)
    return jnp.where(ok.reshape(nd, w, NA_KROWS * w)[None], vals, -jnp.inf).astype(F32)


def _na(pr, tab, q_w, k_w, ybuf, *, n_lat, n_ctx, qcol, out_col):
    nb, t, _ = pr.shape
    dh, nh = NA_HEAD_DIM, NA_HEADS
    nq = NA_QROWS * GRID_W
    qb, ob = qcol // dh, out_col // dh
    nd = tab.shape[1]
    return pl.pallas_call(
        functools.partial(_na_body, n_lat=n_lat, n_ctx=n_ctx),
        out_shape=jax.ShapeDtypeStruct(ybuf.shape, ybuf.dtype), grid=(nb, nh, n_lat // nq),
        in_specs=[pl.BlockSpec((None, nq, dh), lambda b, h, j: (b, j, qb + h)),
                  pl.BlockSpec((None, t, dh), lambda b, h, j: (b, 0, qb + nh + h)),
                  pl.BlockSpec((None, t, dh), lambda b, h, j: (b, 0, qb + 2 * nh + h)),
                  pl.BlockSpec((None, nd, GRID_W, NA_KROWS * GRID_W), lambda b, h, j: (h, 0, 0, 0)),
                  pl.BlockSpec((1, dh), lambda b, h, j: (0, 0)),
                  pl.BlockSpec((1, dh), lambda b, h, j: (0, 0)),
                  pl.BlockSpec(memory_space=pl.ANY)],
        out_specs=pl.BlockSpec((None, nq, dh), lambda b, h, j: (b, j, ob + h)),
        scratch_shapes=[pltpu.VMEM((t, dh), BF16)],
        input_output_aliases={6: 0}, name="na",
        compiler_params=_cp(("parallel", "parallel", "arbitrary")))(pr, pr, pr, tab, q_w, k_w, ybuf)


def _rope_tables(n_lat, n_ctx, width):
    dh = ML_QK_DIM
    nf = dh // 4
    t = jnp.arange(n_lat)
    inv = ROPE_THETA ** (-jnp.arange(nf, dtype=F32) / nf)
    lane = np.arange(dh)
    pos = jnp.where((lane < dh // 2)[None, :], (t // GRID_W)[:, None], (t % GRID_W)[:, None]).astype(F32)
    ang = pos * inv[lane % nf][None, :]
    sign = np.where((lane % (dh // 2)) < nf, -1.0, 1.0).astype(np.float32)
    cos = jnp.concatenate([jnp.cos(ang), jnp.ones((n_ctx, dh), F32)], axis=0)
    sin = jnp.concatenate([jnp.sin(ang) * sign[None, :], jnp.zeros((n_ctx, dh), F32)], axis=0)
    reps = width // dh
    return jnp.tile(cos, (1, reps)), jnp.tile(sin, (1, reps))


def _ada_mod(c, c_ctx, w_all, b_all, layer):
    nb, d = c.shape
    rows = 16
    cv = jnp.concatenate([c, c_ctx[None, :], jnp.zeros((rows - nb - 1, d), F32)], axis=0)
    a = (cv * _sigmoid(cv)).astype(BF16)[None]
    out = _matmul(a, w_all, w_index=layer, tm=rows, tn=1024, tk=d, out_dtype=F32, name="ada_mod")[0]
    out = out[:nb + 1] + b_all[layer][None, :]
    return out.reshape(nb + 1, 6, d).transpose(1, 0, 2)


def _lat_ctx(m, nb):
    ctx = jnp.broadcast_to(m[nb][None, :], (nb, m.shape[1]))
    return jnp.stack([m[:nb], ctx], axis=1)[:, :, None, :]


def _pad_cols(w, n):
    return jnp.pad(w, ((0, 0), (0, n - w.shape[1])))


def _col_row_forms(g, groups):
    nb, t, _ = g.shape
    g5 = g.reshape(nb, t, 2, groups, -1)
    return g5.transpose(0, 2, 3, 1, 4), g5.transpose(0, 2, 3, 4, 1)


ROW_TILES = (1088, 1024, 512, 256)


def _row_tile(rows):
    return next(tm for tm in ROW_TILES if rows % tm == 0)


def _conv_ffn_block(x, mods, norm_w, w_up_all, conv_w, conv_b, w_down_all, layer, *, rows, n_lat, segs):
    nb = x.shape[0]
    tm = _row_tile(rows)
    mod = jnp.stack([_lat_ctx(mods[3], nb), _lat_ctx(mods[4], nb)], axis=1)
    u = _rmsmod(x, norm_w[None, :], mod, tm=tm, rows=rows, n_lat=n_lat)
    up = _matmul(u, w_up_all, w_index=layer, tm=tm, tn=1024, tk=D_MODEL, out_dtype=BF16, name="ffn_up")
    hid = _conv_ffn(up, conv_w, conv_b[None, :], segs=segs, rows=rows)
    return _matmul_gated_residual(hid, w_down_all, x, _lat_ctx(mods[5], nb), w_index=layer, tm=tm, tn=1024,
                                  tk=D_FF // 4, rows=rows, n_lat=n_lat, name="ffn_down")


def _even_layer(x, mods, norm_w, w_in_all, conv_w, conv_b, dt_bias, a_log, d_skip, ssd_norm_w,
                hy_w1, hy_b1, hy_w2, hy_b2, hy_w3, hy_freq, hy_bias, w_out_all, j, *, n_lat, n_ctx):
    nb, t, d = x.shape
    segs = ((0, n_lat), (n_lat, n_ctx))
    tm = _row_tile(t)
    mod = jnp.stack([_lat_ctx(mods[0], nb), _lat_ctx(mods[1], nb)], axis=1)
    u = _rmsmod(x, norm_w[0][None, :], mod, tm=tm, rows=t, n_lat=n_lat)
    n_main = SSD_WIDTH + EV_CONV_CH
    pr = _matmul(u, w_in_all, w_index=j, cols=n_main, tm=tm, tn=1024, tk=d, out_dtype=BF16, name="ev_in")
    w_dt = _pad_cols(w_in_all[j, :, n_main:], LANES).astype(BF16)
    dtp = _matmul(u, w_dt[None], tm=tm, tn=LANES, tk=d, out_dtype=F32, name="ev_in_dt")[:, :, :2 * SSD_HEADS]
    cb = conv_b[None, :]
    xbc = _conv_silu(pr, conv_w, cb, xcol=SSD_WIDTH, wcol=0, width=SSD_XBC, segs=segs)
    x0, zin_l, zin_c = _conv_hyena(pr, conv_w, cb, xcol=SSD_WIDTH + SSD_XBC, wcol=SSD_XBC, segs=segs)

    dt_c, dt_r = _col_row_forms(dtp, SSD_GROUPS)
    db = dt_bias.reshape(2, SSD_GROUPS, 1, SSD_HPG)
    al = a_log.reshape(2, SSD_GROUPS, 1, SSD_HPG)
    scan_args = (xbc, dt_c, dt_r, db, db.transpose(0, 1, 3, 2), al, al.transpose(0, 1, 3, 2))
    yf = _ssd(*scan_args, direction=0, n_lat=n_lat)
    dsk = jnp.repeat(d_skip, SSD_HEAD_DIM)[None, :]
    ybuf = _ssd(*scan_args, direction=1, n_lat=n_lat, final_args=(yf, pr, dsk, ssd_norm_w[None, :]),
                out_width=SSD_WIDTH + HY_WIDTH)

    hyb = hy_bias[None, :]
    filt = functools.partial(_hyena_filters, w1=hy_w1, b1=hy_b1, w2=hy_w2, b2=hy_b2, w3=hy_w3, freq=hy_freq)
    ybuf = _hyena_longconv(zin_l, x0, filt(n_lat), hyb, ybuf, row0=0)
    ybuf = _hyena_longconv(zin_c, x0, filt(n_ctx), hyb, ybuf, row0=n_lat)
    return _matmul_gated_residual(ybuf, w_out_all, x, _lat_ctx(mods[2], nb), w_index=j, tm=tm, tn=1024,
                                  tk=2048, rows=t, n_lat=n_lat, name="ev_out")


def _odd_layer_latent(x, mods, norm_w, w_in, conv_w, conv_b, gate_b, ml_norm_w, q_norm_w, k_norm_w, rpb,
                      w_out_all, j, *, n_lat, n_ctx):
    nb, t, d = x.shape
    segs = ((0, n_lat), (n_lat, n_ctx))
    tm = _row_tile(t)
    mod = jnp.stack([_lat_ctx(mods[0], nb), _lat_ctx(mods[1], nb)], axis=1)
    u = _rmsmod(x, norm_w[0][None, :], mod, tm=tm, rows=t, n_lat=n_lat)
    o3 = 2 * ML_QK_WIDTH + 2 * ML_WIDTH
    o4 = o3 + 4 * ML_HEADS
    wb = w_in.astype(BF16)
    w_main = jnp.concatenate([wb[:, :o3], wb[:, o4:]], axis=1)
    pr = _matmul(u, w_main[None], tm=tm, tn=1024, tk=d, out_dtype=BF16, name="od_in")
    gates = _matmul(u, _pad_cols(wb[:, o3:o4], LANES)[None], tm=tm, tn=LANES, tk=d, out_dtype=F32,
                    name="od_in_gates")[:, :, :4 * ML_HEADS] + gate_b.reshape(-1)
    cos, sin = _rope_tables(n_lat, n_ctx, 256)
    qk = _conv_rope(pr, conv_w, conv_b[None, :], cos, sin, width=2 * ML_QK_WIDTH, segs=segs)
    g5 = gates.reshape(nb, t, 2, 2 * ML_HEADS)
    g_c, g_r = g5.transpose(0, 2, 1, 3), g5.transpose(0, 2, 3, 1)
    hf = _mlstm(qk, pr, g_c, g_r, direction=0, n_lat=n_lat)
    ybuf = _mlstm(qk, pr, g_c, g_r, direction=1, n_lat=n_lat, final_args=(hf, ml_norm_w[None, :]),
                  out_width=ML_WIDTH + NA_WIDTH)
    tab = _na_bias_table(rpb, n_lat // GRID_W)
    ybuf = _na(pr, tab, q_norm_w[None, :], k_norm_w[None, :], ybuf, n_lat=n_lat, n_ctx=n_ctx, qcol=o3,
               out_col=ML_WIDTH)
    return _matmul_gated_residual(ybuf, w_out_all, x, _lat_ctx(mods[2], nb), w_index=j, tm=_row_tile(n_lat),
                                  tn=1024, tk=2048, rows=n_lat, n_lat=n_lat, name="od_out")


def kernel(x, c, ctx, c_ctx, ada_w, ada_b, norm_w, ev_w_in, ev_conv_w, ev_conv_b, ssd_dt_bias, ssd_a_log, ssd_d,
           ssd_norm_w, hy_w1, hy_b1, hy_w2, hy_b2, hy_w3, hy_freq, hy_bias, ev_w_out, od_w_in, ml_conv_w, ml_conv_b,
           ml_gate_b, ml_norm_w, na_q_norm_w, na_k_norm_w, na_rpb, od_w_out, ffn_w_up, ffn_conv_w, ffn_conv_b,
           ffn_w_down):
    depth = ada_w.shape[0]
    assert depth == 2, "layer schedule below is written for one even and one odd layer"
    n_lat, n_ctx = x.shape[1], ctx.shape[1]
    t = n_lat + n_ctx
    both = ((0, n_lat), (n_lat, n_ctx))
    xs = jnp.concatenate([x, ctx], axis=1)

    w_down = ffn_w_down.astype(BF16)

    mods = _ada_mod(c, c_ctx, ada_w, ada_b, 0)
    xs = _even_layer(xs, mods, norm_w[0], ev_w_in, ev_conv_w[0], ev_conv_b[0], ssd_dt_bias[0], ssd_a_log[0],
                     ssd_d[0], ssd_norm_w[0], hy_w1[0], hy_b1[0], hy_w2[0], hy_b2[0], hy_w3[0], hy_freq[0],
                     hy_bias[0], ev_w_out.astype(BF16), 0, n_lat=n_lat, n_ctx=n_ctx)
    xs = _conv_ffn_block(xs, mods, norm_w[0, 1], ffn_w_up, ffn_conv_w[0], ffn_conv_b[0], w_down, 0,
                         rows=t, n_lat=n_lat, segs=both)

    mods = _ada_mod(c, c_ctx, ada_w, ada_b, 1)
    xl = _odd_layer_latent(xs, mods, norm_w[1], od_w_in[0], ml_conv_w[0], ml_conv_b[0], ml_gate_b[0], ml_norm_w[0],
                           na_q_norm_w[0], na_k_norm_w[0], na_rpb[0], od_w_out.astype(BF16), 0,
                           n_lat=n_lat, n_ctx=n_ctx)
    return _conv_ffn_block(xl, mods, norm_w[1, 1], ffn_w_up, ffn_conv_w[1], ffn_conv_b[1], w_down, 1,
                           rows=n_lat, n_lat=n_lat, segs=((0, n_lat),))
```

```python
import functools
import math

import numpy as np
import jax
import jax.numpy as jnp
from jax import lax
from jax.experimental import pallas as pl
from jax.experimental.pallas import tpu as pltpu

F32 = jnp.float32
BF16 = jnp.bfloat16
HI = lax.Precision.HIGHEST

D_MODEL = 2048
SEQ = 4096
CTX_LEN = 256
GRID_W = 64
EPS = 1e-6
CONV_W = 3
SSD_WIDTH = D_MODEL
SSD_HEAD_DIM = 64
SSD_HEADS = SSD_WIDTH // SSD_HEAD_DIM
SSD_STATE = 128
SSD_GROUPS = 4
SSD_HPG = SSD_HEADS // SSD_GROUPS
SSD_GW = SSD_WIDTH // SSD_GROUPS
SSD_XBC = SSD_WIDTH + 2 * SSD_GROUPS * SSD_STATE
HY_WIDTH = D_MODEL
HY_EMB = 33
HY_BANDS = (HY_EMB - 1) // 2
HY_ORDER = 64
HY_FAST = 0.3
HY_SLOW = 1.5
HY_TARGET = 1e-2
ML_HEADS = 8
ML_WIDTH = D_MODEL
ML_V_DIM = ML_WIDTH // ML_HEADS
ML_QK_DIM = ML_V_DIM // 2
ML_QK_WIDTH = ML_HEADS * ML_QK_DIM
ROPE_THETA = 10000.0
NA_WIDTH = D_MODEL
NA_HEAD_DIM = 128
NA_HEADS = NA_WIDTH // NA_HEAD_DIM
NA_ROWS = 8
NA_COLS = 16
NA_QROWS = 8
NA_KROWS = 16
D_FF = 256 * ((8 * D_MODEL // 3 + 255) // 256)
EV_CONV_CH = SSD_XBC + 3 * HY_WIDTH
SCAN_Q = 128
LANES = 128
SUBLANES = 8
VMEM_LIMIT = 48 * 1024 * 1024
VMEM_LIMIT_WIDE = 56 * 1024 * 1024


def _cp(sem, vmem=VMEM_LIMIT):
    return pltpu.CompilerParams(dimension_semantics=sem, vmem_limit_bytes=vmem)


def _sigmoid(x):
    return 1.0 / (1.0 + jnp.exp(-x))


def _silu(x):
    return x * _sigmoid(x)


def _softplus(x):
    return jnp.maximum(x, 0.0) + jnp.log(1.0 + jnp.exp(-jnp.abs(x)))


def _log_sigmoid(x):
    return -_softplus(-x)


def _dot(a, b, precision=None):
    return jnp.dot(a, b, preferred_element_type=F32, precision=precision)


def _dot_nt(a, b):
    return lax.dot_general(a, b, (((1,), (1,)), ((), ())), preferred_element_type=F32)


def _split_bf16(x, terms):
    out = []
    for _ in range(terms):
        part = x.astype(BF16)
        out.append(part)
        x = x - part.astype(F32)
    return out


def _dot_sel(x, sel, terms):
    selb = sel.astype(BF16)
    return sum(_dot(part, selb) for part in _split_bf16(x, terms))


def _sel_dot(sel, x, terms):
    selb = sel.astype(BF16)
    return sum(_dot(selb, part) for part in _split_bf16(x, terms))


def _mm_body(*refs, nk, n_extra, epilogue, has_buf, cache_w, n_out):
    a_ref, w_ref = refs[0], refs[1]
    extra = refs[2:2 + n_extra]
    pos = 2 + n_extra + (1 if has_buf else 0)
    o_refs = refs[pos:pos + n_out]
    pos += n_out - 1
    row_blk = pl.program_id(2)
    if cache_w:
        wb_ref = refs[pos + 1]

        @pl.when((pl.program_id(1) == 0) & (row_blk == 0))
        def _():
            wb_ref[...] = w_ref[...].astype(BF16)

        w_val = wb_ref[...]
    else:
        w_val = w_ref[...].astype(BF16)
    part = _dot(a_ref[...].astype(BF16), w_val)

    def finish(acc):
        vals = acc if epilogue is None else epilogue(acc, row_blk, *extra)
        for o_ref, val in zip(o_refs, vals if n_out > 1 else (vals,)):
            o_ref[...] = val.astype(o_ref.dtype)

    if nk == 1:
        finish(part)
    else:
        acc_ref = refs[pos + 1]
        k = pl.program_id(3)

        @pl.when(k == 0)
        def _():
            acc_ref[...] = part

        @pl.when(k > 0)
        def _():
            acc_ref[...] += part

        @pl.when(k == nk - 1)
        def _():
            finish(acc_ref[...])


def _matmul(a, w, *, tm, tn, tk, out_dtype, rows=None, cols=None, w_index=None, extra=(), epilogue=None,
            out_buf=None, out_row_blk=0, out_col_blk=0, out2_dtype=None, vmem=None, name="mm"):
    ba, ta, kdim = a.shape
    bw, kdim2, n = w.shape
    assert kdim == kdim2
    nb = ba if w_index is not None else max(ba, bw)
    rows = ta if rows is None else rows
    n = n if cols is None else cols
    assert rows % tm == 0 and n % tn == 0 and kdim % tk == 0
    ni, nj, nk = rows // tm, n // tn, kdim // tk
    a_map = (lambda j, b, i, k: (b, i, k)) if ba > 1 else (lambda j, b, i, k: (0, i, k))
    if w_index is not None:
        w_map = lambda j, b, i, k: (w_index, k, j)
    else:
        w_map = (lambda j, b, i, k: (b, k, j)) if bw > 1 else (lambda j, b, i, k: (0, k, j))
    cache_w = w.dtype != BF16 and nk == 1 and (w_index is not None or bw == 1)
    in_specs = [pl.BlockSpec((None, tm, tk), a_map), pl.BlockSpec((None, tk, tn), w_map)]
    args = [a, w]
    for arr, spec in extra:
        args.append(arr)
        in_specs.append(spec)
    aliases = {}
    if out_buf is not None:
        aliases = {len(args): 0}
        args.append(out_buf)
        in_specs.append(pl.BlockSpec(memory_space=pl.ANY))
        out_shape = jax.ShapeDtypeStruct(out_buf.shape, out_buf.dtype)
    else:
        out_shape = jax.ShapeDtypeStruct((nb, rows, n), out_dtype)
    out_spec = pl.BlockSpec((None, tm, tn), lambda j, b, i, k: (b, i + out_row_blk, j + out_col_blk))
    if out2_dtype is not None:
        out_shape = (out_shape, jax.ShapeDtypeStruct((nb, rows, n), out2_dtype))
        out_spec = (out_spec, pl.BlockSpec((None, tm, tn), lambda j, b, i, k: (b, i, j)))
    scratch = [pltpu.VMEM((tm, tn), F32)] if nk > 1 else []
    if cache_w:
        scratch = [pltpu.VMEM((tk, tn), BF16)]
    body = functools.partial(_mm_body, nk=nk, n_extra=len(extra), epilogue=epilogue,
                             has_buf=out_buf is not None, cache_w=cache_w, n_out=1 if out2_dtype is None else 2)
    inner = "arbitrary" if cache_w else "parallel"
    return pl.pallas_call(
        body, out_shape=out_shape, grid=(nj, nb, ni, nk), in_specs=in_specs, out_specs=out_spec,
        scratch_shapes=scratch, input_output_aliases=aliases, name=name,
        compiler_params=_cp(("parallel", inner, inner, "arbitrary"), vmem or VMEM_LIMIT))(*args)


def _row_is_lat(row_blk, tm, n_lat):
    rows = row_blk * tm + lax.broadcasted_iota(jnp.int32, (tm, 1), 0)
    return rows < n_lat


def _gated_residual_epilogue(acc, row_blk, res_ref, gate_ref, *, tm, n_lat):
    gate = jnp.where(_row_is_lat(row_blk, tm, n_lat), gate_ref[0], gate_ref[1])
    return res_ref[...] + gate * acc


def _modulated_rmsnorm(x, is_lat, w_ref, mod_ref):
    y = x * lax.rsqrt(jnp.mean(x * x, axis=-1, keepdims=True) + EPS) * w_ref[...]
    shift = jnp.where(is_lat, mod_ref[0, 0], mod_ref[0, 1])
    scale = jnp.where(is_lat, mod_ref[1, 0], mod_ref[1, 1])
    return y * (1.0 + scale) + shift


def _gated_residual_norm_epilogue(acc, row_blk, res_ref, gate_ref, nw_ref, mod_ref, *, tm, n_lat):
    is_lat = _row_is_lat(row_blk, tm, n_lat)
    x = res_ref[...] + jnp.where(is_lat, gate_ref[0], gate_ref[1]) * acc
    return x, _modulated_rmsnorm(x, is_lat, nw_ref, mod_ref)


def _matmul_gated_residual(a, w, res, gate, *, w_index, tm, tn, tk, rows, n_lat, name, norm=None):
    extra = [(res, pl.BlockSpec((None, tm, tn), lambda j, b, i, k: (b, i, j))),
             (gate, pl.BlockSpec((None, 2, 1, tn), lambda j, b, i, k: (b, 0, 0, j)))]
    if norm is None:
        epi = functools.partial(_gated_residual_epilogue, tm=tm, n_lat=n_lat)
        return _matmul(a, w, w_index=w_index, tm=tm, tn=tn, tk=tk, out_dtype=F32, rows=rows, extra=extra,
                       epilogue=epi, name=name)
    norm_w, mod = norm
    assert tn == w.shape[2]
    extra += [(norm_w, pl.BlockSpec((1, tn), lambda j, b, i, k: (0, 0))),
              (mod, pl.BlockSpec((None, 2, 2, 1, tn), lambda j, b, i, k: (b, 0, 0, 0, 0)))]
    epi = functools.partial(_gated_residual_norm_epilogue, tm=tm, n_lat=n_lat)
    return _matmul(a, w, w_index=w_index, tm=tm, tn=tn, tk=tk, out_dtype=F32, rows=rows, extra=extra,
                   epilogue=epi, out2_dtype=BF16, vmem=VMEM_LIMIT_WIDE, name=name)


def _rmsmod_body(x_ref, w_ref, mod_ref, o_ref, *, tm, n_lat):
    is_lat = _row_is_lat(pl.program_id(2), tm, n_lat)
    o_ref[...] = _modulated_rmsnorm(x_ref[...], is_lat, w_ref, mod_ref).astype(o_ref.dtype)


def _rmsmod(x, w, mod, *, tm, rows, n_lat):
    nb, _, d = x.shape
    tm = tm // 4
    assert rows % tm == 0 and tm % 16 == 0
    body = functools.partial(_rmsmod_body, tm=tm, n_lat=n_lat)
    return pl.pallas_call(
        body, out_shape=jax.ShapeDtypeStruct((nb, rows, d), BF16), grid=(1, nb, rows // tm),
        in_specs=[pl.BlockSpec((None, tm, d), lambda j, b, i: (b, i, 0)),
                  pl.BlockSpec((1, d), lambda j, b, i: (0, 0)),
                  pl.BlockSpec((None, 2, 2, 1, d), lambda j, b, i: (b, 0, 0, 0, 0))],
        out_specs=pl.BlockSpec((None, tm, d), lambda j, b, i: (b, i, 0)),
        name="rmsmod", compiler_params=_cp(("parallel", "parallel", "parallel")))(x, w, mod)


CONV_CH = 128
CONV_HALO = 16


def _shift_matrices():
    r = lax.broadcasted_iota(jnp.int32, (CONV_CH, CONV_CH + 2 * CONV_HALO), 0)
    c = lax.broadcasted_iota(jnp.int32, (CONV_CH, CONV_CH + 2 * CONV_HALO), 1)
    return (c == r + CONV_HALO - 1).astype(BF16), (c == r + CONV_HALO + 1).astype(BF16)


def _conv_at(x_ref, w_ref, b_ref, shifts, r0, n, i):
    t0 = pl.multiple_of(i * CONV_CH, CONV_CH)
    cur = x_ref[pl.ds(r0 + t0, CONV_CH), :]
    lo = pl.multiple_of(jnp.maximum(t0 - CONV_HALO, 0), CONV_HALO)
    hi = pl.multiple_of(jnp.minimum(t0 + CONV_CH, n - CONV_HALO), CONV_HALO)
    zero = jnp.zeros((CONV_HALO, cur.shape[1]), cur.dtype)
    tail = jnp.where(i > 0, x_ref[pl.ds(r0 + lo, CONV_HALO), :], zero)
    head = jnp.where(i < n // CONV_CH - 1, x_ref[pl.ds(r0 + hi, CONV_HALO), :], zero)
    win = jnp.concatenate([tail, cur, head], axis=0)
    return (w_ref[0:1, :] * _dot(shifts[0], win) + w_ref[1:2, :] * cur.astype(F32)
            + w_ref[2:3, :] * _dot(shifts[1], win) + b_ref[...])


def _conv_loop(segs, fn):
    for si, (r0, n) in enumerate(segs):
        def body(i, carry, r0=r0, n=n, si=si):
            fn(si, r0, n, i)
            return carry

        lax.fori_loop(0, n // CONV_CH, body, 0, unroll=min(4, n // CONV_CH))


def _conv_silu_body(x_ref, w_ref, b_ref, o_ref, *, segs):
    shifts = _shift_matrices()

    def fn(si, r0, n, i):
        y = _silu(_conv_at(x_ref, w_ref, b_ref, shifts, r0, n, i))
        o_ref[pl.ds(r0 + pl.multiple_of(i * CONV_CH, CONV_CH), CONV_CH), :] = y.astype(o_ref.dtype)

    _conv_loop(segs, fn)


def _conv_rope_body(x_ref, w_ref, b_ref, cos_ref, sin_ref, o_ref, *, segs, tc):
    shifts = _shift_matrices()
    lane = lax.broadcasted_iota(jnp.int32, (CONV_CH, tc), 1)
    first = (lane % (ML_QK_DIM // 2)) < (ML_QK_DIM // 4)
    nf = ML_QK_DIM // 4

    def fn(si, r0, n, i):
        y = _silu(_conv_at(x_ref, w_ref, b_ref, shifts, r0, n, i))
        partner = jnp.where(first, pltpu.roll(y, tc - nf, 1), pltpu.roll(y, nf, 1))
        rows = pl.ds(r0 + pl.multiple_of(i * CONV_CH, CONV_CH), CONV_CH)
        o_ref[rows, :] = (y * cos_ref[rows, :] + partner * sin_ref[rows, :]).astype(o_ref.dtype)

    _conv_loop(segs, fn)


def _conv_ffn_body(a_ref, x_ref, w_ref, b_ref, o_ref, *, segs):
    shifts = _shift_matrices()

    def fn(si, r0, n, i):
        rows = pl.ds(r0 + pl.multiple_of(i * CONV_CH, CONV_CH), CONV_CH)
        g = _silu(_conv_at(x_ref, w_ref, b_ref, shifts, r0, n, i))
        o_ref[rows, :] = (a_ref[rows, :].astype(F32) * g).astype(o_ref.dtype)

    _conv_loop(segs, fn)


def _conv_hyena_body(x0_ref, x1_ref, v_ref, w0_ref, b0_ref, w1_ref, b1_ref, w2_ref, b2_ref,
                     ox0_ref, ozl_ref, ozc_ref, *, segs):
    shifts = _shift_matrices()
    outs = (ozl_ref, ozc_ref)

    def fn(si, r0, n, i):
        t0 = pl.multiple_of(i * CONV_CH, CONV_CH)
        x0 = _conv_at(x0_ref, w0_ref, b0_ref, shifts, r0, n, i)
        ox0_ref[pl.ds(r0 + t0, CONV_CH), :] = x0.astype(ox0_ref.dtype)
        zin = _conv_at(x1_ref, w1_ref, b1_ref, shifts, r0, n, i) * _conv_at(v_ref, w2_ref, b2_ref, shifts, r0, n, i)
        outs[si][pl.ds(t0, CONV_CH), :] = zin.astype(outs[si].dtype)

    _conv_loop(segs, fn)


def _conv_specs(t_in, tc, xcol_blk, wcol_blk):
    return [pl.BlockSpec((None, t_in, tc), lambda b, j: (b, 0, j + xcol_blk)),
            pl.BlockSpec((CONV_W, tc), lambda b, j: (0, j + wcol_blk)),
            pl.BlockSpec((1, tc), lambda b, j: (0, j + wcol_blk))]


def _conv_silu(x, w, b, *, xcol, wcol, width, segs, tc=256):
    nb, t_in, _ = x.shape
    return pl.pallas_call(
        functools.partial(_conv_silu_body, segs=segs),
        out_shape=jax.ShapeDtypeStruct((nb, t_in, width), BF16), grid=(nb, width // tc),
        in_specs=_conv_specs(t_in, tc, xcol // tc, wcol // tc),
        out_specs=pl.BlockSpec((None, t_in, tc), lambda b, j: (b, 0, j)),
        name="conv_silu", compiler_params=_cp(("parallel", "parallel")))(x, w, b)


def _conv_rope(x, w, b, cos, sin, *, width, segs, tc=256):
    nb, t_in, _ = x.shape
    tab = pl.BlockSpec((t_in, tc), lambda b, j: (0, 0))
    return pl.pallas_call(
        functools.partial(_conv_rope_body, segs=segs, tc=tc),
        out_shape=jax.ShapeDtypeStruct((nb, t_in, width), BF16), grid=(nb, width // tc),
        in_specs=_conv_specs(t_in, tc, 0, 0) + [tab, tab],
        out_specs=pl.BlockSpec((None, t_in, tc), lambda b, j: (b, 0, j)),
        name="conv_rope", compiler_params=_cp(("parallel", "parallel")))(x, w, b, cos, sin)


def _conv_ffn(up, w, b, *, segs, rows, tc=256):
    nb, t_in, two_ff = up.shape
    ff = two_ff // 2
    return pl.pallas_call(
        functools.partial(_conv_ffn_body, segs=segs),
        out_shape=jax.ShapeDtypeStruct((nb, rows, ff), BF16), grid=(nb, ff // tc),
        in_specs=[pl.BlockSpec((None, rows, tc), lambda b, j: (b, 0, j))] + _conv_specs(rows, tc, ff // tc, 0),
        out_specs=pl.BlockSpec((None, rows, tc), lambda b, j: (b, 0, j)),
        name="conv_ffn", compiler_params=_cp(("parallel", "parallel")))(up, up, w, b)


def _conv_hyena(pr, w, b, *, xcol, wcol, segs, tc=256):
    nb, t_in, _ = pr.shape
    (_, n_lat), (_, n_ctx) = segs
    wd = HY_WIDTH
    in_specs = [pl.BlockSpec((None, t_in, tc), functools.partial(lambda b, j, o: (b, 0, j + o), o=(xcol + m * wd) // tc))
                for m in range(3)]
    wargs = []
    for m in range(3):
        off = (wcol + m * wd) // tc
        in_specs.append(pl.BlockSpec((CONV_W, tc), functools.partial(lambda b, j, o: (0, j + o), o=off)))
        in_specs.append(pl.BlockSpec((1, tc), functools.partial(lambda b, j, o: (0, j + o), o=off)))
        wargs += [w, b]
    return pl.pallas_call(
        functools.partial(_conv_hyena_body, segs=segs),
        out_shape=(jax.ShapeDtypeStruct((nb, t_in, wd), BF16), jax.ShapeDtypeStruct((nb, n_lat, wd), BF16),
                   jax.ShapeDtypeStruct((nb, n_ctx, wd), BF16)),
        grid=(nb, wd // tc), in_specs=in_specs,
        out_specs=(pl.BlockSpec((None, t_in, tc), lambda b, j: (b, 0, j)),
                   pl.BlockSpec((None, n_lat, tc), lambda b, j: (b, 0, j)),
                   pl.BlockSpec((None, n_ctx, tc), lambda b, j: (b, 0, j))),
        name="conv_hyena", compiler_params=_cp(("parallel", "parallel")))(pr, pr, pr, *wargs)


def _scan_masks(reverse):
    q = SCAN_Q
    ii = lax.broadcasted_iota(jnp.int32, (q, q), 0)
    jj = lax.broadcasted_iota(jnp.int32, (q, q), 1)
    mask = (jj >= ii) if reverse else (jj <= ii)
    mask_t = (jj <= ii) if reverse else (jj >= ii)
    return mask, mask.astype(F32), mask_t.astype(F32)


def _chunk_map(reverse, n_chunks, n_lat_chunks):
    if reverse:
        return lambda k: n_chunks - 1 - k
    return lambda k: (k + n_lat_chunks) % n_chunks


def _ssd_body(*refs, reverse, final):
    (xs_ref, b_ref, c_ref, dtc_ref, dtr_ref, dbc_ref, dbr_ref, alc_ref, alr_ref) = refs[:9]
    if final:
        yf_ref, z_ref, dsk_ref, nw_ref, o_ref, h_ref = refs[9:]
    else:
        o_ref, h_ref = refs[9:]
    q, gw, hd = SCAN_Q, SSD_GW, SSD_HEAD_DIM

    @pl.when(pl.program_id(2) == 0)
    def _():
        h_ref[...] = jnp.zeros_like(h_ref)

    xs = xs_ref[...].astype(F32)
    bm = b_ref[...]
    cm = c_ref[...]
    dt_c = _softplus(dtc_ref[...] + dbc_ref[...])
    dt_r = _softplus(dtr_ref[...] + dbr_ref[...])
    da_c = dt_c * (-jnp.exp(alc_ref[...]))
    da_r = dt_r * (-jnp.exp(alr_ref[...]))
    mask, maskf, mask_tf = _scan_masks(reverse)
    acum_c = _sel_dot(maskf, da_c, 3)
    acum_r = _dot_sel(da_r, mask_tf, 3)
    atot = jnp.sum(da_c, axis=0, keepdims=True)

    er = lax.broadcasted_iota(jnp.int32, (SSD_HPG, gw), 0)
    ec = lax.broadcasted_iota(jnp.int32, (SSD_HPG, gw), 1)
    expand = (ec // hd == er).astype(F32)
    stack = jnp.concatenate([dt_c, jnp.exp(atot - acum_c), jnp.exp(acum_c),
                             jnp.broadcast_to(jnp.exp(atot), (SUBLANES, SSD_HPG))], axis=0)
    wide = _dot_sel(stack, expand, 2)
    dt_e, ws_e, eac_e, eat_e = wide[0:q], wide[q:2 * q], wide[2 * q:3 * q], wide[3 * q:3 * q + 1]

    xq = xs * dt_e
    h_prev = h_ref[...]
    cb = _dot_nt(cm, bm)
    y = eac_e * _dot(cm, h_prev.astype(BF16))
    lane = lax.broadcasted_iota(jnp.int32, (q, 2 * hd), 1)
    pieces = []
    for p in range(SSD_HPG // 2):
        ms = []
        for r in (2 * p, 2 * p + 1):
            seg = acum_c[:, r:r + 1] - acum_r[r:r + 1, :]
            ms.append((cb * jnp.where(mask, jnp.exp(seg), 0.0)).astype(BF16))
        xp = xq[:, 2 * hd * p:2 * hd * (p + 1)]
        rhs = jnp.concatenate([jnp.where(lane < hd, xp, 0.0), jnp.where(lane >= hd, xp, 0.0)], axis=0)
        pieces.append(_dot(jnp.concatenate(ms, axis=1), rhs.astype(BF16)))
    y = y + jnp.concatenate(pieces, axis=1)

    bt = bm.astype(F32).T.astype(BF16)
    h_ref[...] = eat_e * h_prev + _dot(bt, (xq * ws_e).astype(BF16))

    if final:
        ytot = y + yf_ref[...] + xs * dsk_ref[...]
        g = ytot * _silu(z_ref[...].astype(F32))
        o = g * lax.rsqrt(jnp.mean(g * g, axis=-1, keepdims=True) + EPS) * nw_ref[...]
        o_ref[...] = o.astype(o_ref.dtype)
    else:
        o_ref[...] = y


def _ssd(xbc, dt_c, dt_r, db_c, db_r, al_c, al_r, *, direction, n_lat, final_args=None, out_width=None):
    nb, t, _ = xbc.shape
    q, gw, hpg = SCAN_Q, SSD_GW, SSD_HPG
    nch = t // q
    reverse = direction == 1
    cmap = _chunk_map(reverse, nch, n_lat // q)
    nblk = SSD_STATE // LANES
    b0 = SSD_WIDTH // SSD_STATE
    c0 = b0 + SSD_GROUPS * nblk
    d = direction
    in_specs = [
        pl.BlockSpec((None, q, gw), lambda b, g, k: (b, cmap(k), g)),
        pl.BlockSpec((None, q, SSD_STATE), lambda b, g, k: (b, cmap(k), b0 + g)),
        pl.BlockSpec((None, q, SSD_STATE), lambda b, g, k: (b, cmap(k), c0 + g)),
        pl.BlockSpec((None, None, None, q, hpg), lambda b, g, k: (b, d, g, cmap(k), 0)),
        pl.BlockSpec((None, None, None, hpg, q), lambda b, g, k: (b, d, g, 0, cmap(k))),
        pl.BlockSpec((None, None, 1, hpg), lambda b, g, k: (d, g, 0, 0)),
        pl.BlockSpec((None, None, hpg, 1), lambda b, g, k: (d, g, 0, 0)),
        pl.BlockSpec((None, None, 1, hpg), lambda b, g, k: (d, g, 0, 0)),
        pl.BlockSpec((None, None, hpg, 1), lambda b, g, k: (d, g, 0, 0)),
    ]
    args = [xbc, xbc, xbc, dt_c, dt_r, db_c, db_r, al_c, al_r]
    final = final_args is not None
    if final:
        yf, pr, dskip, norm_w = final_args
        in_specs += [pl.BlockSpec((None, q, gw), lambda b, g, k: (b, cmap(k), g)),
                     pl.BlockSpec((None, q, gw), lambda b, g, k: (b, cmap(k), g)),
                     pl.BlockSpec((1, gw), lambda b, g, k: (0, g)),
                     pl.BlockSpec((1, gw), lambda b, g, k: (0, g))]
        args += [yf, pr, dskip, norm_w]
        out_shape = jax.ShapeDtypeStruct((nb, t, out_width), BF16)
    else:
        out_shape = jax.ShapeDtypeStruct((nb, t, SSD_WIDTH), F32)
    return pl.pallas_call(
        functools.partial(_ssd_body, reverse=reverse, final=final),
        out_shape=out_shape, grid=(nb, SSD_GROUPS, nch), in_specs=in_specs,
        out_specs=pl.BlockSpec((None, q, gw), lambda b, g, k: (b, cmap(k), g)),
        scratch_shapes=[pltpu.VMEM((SSD_STATE, gw), F32)],
        name="ssd_bwd" if reverse else "ssd_fwd",
        compiler_params=_cp(("parallel", "parallel", "arbitrary")))(*args)


def _hyfilt_body(f_ref, w1_ref, b1_ref, w2_ref, b2_ref, w3_ref, fr_ref, dl_ref, o_ref, *, tl):
    f = f_ref[...]
    h = jnp.sin(fr_ref[0:1, :] * (_dot(f, w1_ref[...], HI) + b1_ref[...]))
    h = jnp.sin(fr_ref[1:2, :] * (_dot(h, w2_ref[...], HI) + b2_ref[...]))
    h = _dot(h, w3_ref[...], HI)
    dec = jnp.exp(-f[:, 0:1] * dl_ref[...])
    row = pl.program_id(0) * tl + lax.broadcasted_iota(jnp.int32, (tl, 1), 0)
    wd = HY_WIDTH
    o_ref[:, 0:wd] = (h[:, 0:wd] * dec).astype(o_ref.dtype)
    o_ref[:, wd:2 * wd] = jnp.where(row == 0, 0.0, h[:, wd:2 * wd] * dec).astype(o_ref.dtype)


def _hyena_filters(length, w1, b1, w2, b2, w3, freq, *, tl=256):
    p = LANES
    t = jnp.linspace(0.0, 1.0, length, dtype=F32)[:, None]
    w = 2.0 * math.pi * jnp.arange(length, dtype=F32)[:, None] / length
    f = jnp.linspace(1e-4, HY_BANDS - 1, HY_BANDS, dtype=F32)[None, :]
    feats = jnp.concatenate([t, jnp.cos(f * w), -jnp.sin(f * w)], axis=-1)
    feats = jnp.pad(feats, ((0, 0), (0, p - HY_EMB)))
    po = p - HY_ORDER
    w1p = jnp.pad(w1, ((0, p - HY_EMB), (0, po)))
    w2p = jnp.pad(w2, ((0, po), (0, po)))
    w3p = jnp.pad(w3, ((0, po), (0, 0)))
    b1p = jnp.pad(b1[None, :], ((0, 0), (0, po)))
    b2p = jnp.pad(b2[None, :], ((0, 0), (0, po)))
    frp = jnp.pad(freq, ((0, 0), (0, po)))
    deltas = jnp.abs(jnp.linspace(math.log(HY_TARGET) / HY_FAST, math.log(HY_TARGET) / HY_SLOW, HY_WIDTH,
                                  dtype=F32))[None, :]
    full = lambda shape: pl.BlockSpec(shape, lambda i: (0,) * len(shape))
    out = pl.pallas_call(
        functools.partial(_hyfilt_body, tl=tl),
        out_shape=jax.ShapeDtypeStruct((length, 2 * HY_WIDTH), BF16), grid=(length // tl,),
        in_specs=[pl.BlockSpec((tl, p), lambda i: (i, 0)), full((p, p)), full((1, p)), full((p, p)), full((1, p)),
                  full((p, 2 * HY_WIDTH)), full((2, p)), full((1, HY_WIDTH))],
        out_specs=pl.BlockSpec((tl, 2 * HY_WIDTH), lambda i: (i, 0)),
        name="hyena_filters", compiler_params=_cp(("parallel",)))(feats, w1p, b1p, w2p, b2p, w3p, frp, deltas)
    return out[None]


def _dft_matrices(length):
    n = 2 * length
    k = jnp.arange(length, dtype=jnp.int32)[:, None]
    t = jnp.arange(length, dtype=jnp.int32)[None, :]
    step = 1 << (int(math.log2(length)) // 2)
    part = jnp.arange(length // step, dtype=jnp.int32)[:, None]
    ang_hi = ((part * step * t) % n).astype(F32) * (2.0 * math.pi / n)
    ang_lo = ((jnp.arange(step, dtype=jnp.int32)[:, None] * t) % n).astype(F32) * (2.0 * math.pi / n)
    ch, sh = jnp.cos(ang_hi)[:, None, :], jnp.sin(ang_hi)[:, None, :]
    cl, sl = jnp.cos(ang_lo)[None, :, :], jnp.sin(ang_lo)[None, :, :]
    c = (ch * cl - sh * sl).reshape(length, length)
    s = (sh * cl + ch * sl).reshape(length, length)
    alt_t = jnp.where(t % 2 == 0, 1.0, -1.0).astype(F32)
    fwd = jnp.concatenate([c, jnp.where(k == 0, alt_t, -s)], axis=0)
    alt_r = jnp.where(k % 2 == 0, 1.0, -1.0).astype(F32)
    inv_re = jnp.where(t == 0, 1.0 / n, (2.0 / n) * c)
    inv_im = jnp.where(t == 0, alt_r / n, (-2.0 / n) * s)
    inv = jnp.concatenate([inv_re, inv_im], axis=1)
    return fwd.astype(BF16)[None], inv.astype(BF16)[None]


def _specmul_body(z_ref, kf_ref, kb_ref, o_ref, *, tr):
    zr, zi = z_ref[0], z_ref[1]
    kr = kf_ref[0] + kb_ref[0]
    ki = kf_ref[1] - kb_ref[1]
    row = pl.program_id(1) * tr + lax.broadcasted_iota(jnp.int32, (tr, 1), 0)
    first = row == 0
    yr = jnp.where(first, zr * kr, zr * kr - zi * ki)
    yi = jnp.where(first, zi * (kf_ref[1] + kb_ref[1]), zr * ki + zi * kr)
    o_ref[0] = yr.astype(o_ref.dtype)
    o_ref[1] = yi.astype(o_ref.dtype)


def _specmul(z, ksp, *, tr, tc=512):
    nb, l2, c = z.shape
    ln = l2 // 2
    z4 = z.reshape(nb, 2, ln, c)
    k3 = ksp.reshape(2, ln, 2 * c)
    out = pl.pallas_call(
        functools.partial(_specmul_body, tr=tr),
        out_shape=jax.ShapeDtypeStruct((nb, 2, ln, c), BF16), grid=(nb, ln // tr, c // tc),
        in_specs=[pl.BlockSpec((None, 2, tr, tc), lambda b, i, j: (b, 0, i, j)),
                  pl.BlockSpec((2, tr, tc), lambda b, i, j: (0, i, j)),
                  pl.BlockSpec((2, tr, tc), lambda b, i, j: (0, i, j + c // tc))],
        out_specs=pl.BlockSpec((None, 2, tr, tc), lambda b, i, j: (b, 0, i, j)),
        name="specmul", compiler_params=_cp(("parallel", "parallel", "parallel")))(z4, k3, k3)
    return out.reshape(nb, l2, c)


def _hyena_out_epilogue(acc, row_blk, x0_ref, zin_ref, db_ref):
    del row_blk
    zin = zin_ref[...].astype(F32)
    return x0_ref[...].astype(F32) * (acc + zin * db_ref[...])


def _hyena_longconv(zin, x0, hfb, dbias, ybuf, *, row0, tn=1024):
    nb, ln, c = zin.shape
    tm_f, tm_i = min(1024, 2 * ln), min(1024, ln)
    tk_f, tk_i = min(2048, ln), min(2048, 2 * ln)
    assert row0 % tm_i == 0
    fwd, inv = _dft_matrices(ln)
    ksp = _matmul(fwd, hfb, tm=tm_f, tn=tn, tk=tk_f, out_dtype=F32, name="hy_filter_dft")
    z = _matmul(fwd, zin, tm=tm_f, tn=tn, tk=tk_f, out_dtype=F32, name="hy_dft")
    y = _specmul(z, ksp, tr=min(ln, 512))
    rb = row0 // tm_i
    extra = [(x0, pl.BlockSpec((None, tm_i, tn), lambda j, b, i, k: (b, i + rb, j))),
             (zin, pl.BlockSpec((None, tm_i, tn), lambda j, b, i, k: (b, i, j))),
             (dbias, pl.BlockSpec((1, tn), lambda j, b, i, k: (0, j)))]
    return _matmul(inv, y, tm=tm_i, tn=tn, tk=tk_i, out_dtype=BF16, extra=extra, epilogue=_hyena_out_epilogue,
                   out_buf=ybuf, out_row_blk=rb, out_col_blk=SSD_WIDTH // tn, name="hy_idft")


def _mlstm_body(*refs, reverse, final):
    q_ref, k_ref, v_ref, gc_ref, gr_ref = refs[:5]
    if final:
        hf_ref, og_ref, nw_ref, o_ref, ct_ref, n_ref, m_ref = refs[5:]
    else:
        o_ref, ct_ref, n_ref, m_ref = refs[5:]
    nh, dk, dv = ML_HEADS, ML_QK_DIM, ML_V_DIM
    scale = dk ** -0.5

    @pl.when(pl.program_id(1) == 0)
    def _():
        ct_ref[...] = jnp.zeros_like(ct_ref)
        n_ref[...] = jnp.zeros_like(n_ref)
        m_ref[...] = jnp.zeros_like(m_ref)

    gc, gr = gc_ref[...], gr_ref[...]
    li_c, lf_c = gc[:, 0:nh], _log_sigmoid(gc[:, nh:2 * nh])
    li_r, lf_r = gr[0:nh, :], _log_sigmoid(gr[nh:2 * nh, :])
    mask, maskf, mask_tf = _scan_masks(reverse)
    bc_c = _dot(maskf, lf_c, HI)
    bc_r = _dot(lf_r, mask_tf, HI)
    btot_c = jnp.sum(lf_c, axis=0, keepdims=True)

    for h in range(nh):
        qh = q_ref[:, dk * h:dk * (h + 1)]
        kh = k_ref[:, dk * h:dk * (h + 1)]
        vh = v_ref[:, dv * h:dv * (h + 1)]
        m_s = m_ref[h:h + 1, 0:1]
        b_c, b_r = bc_c[:, h:h + 1], bc_r[h:h + 1, :]
        i_c, i_r = li_c[:, h:h + 1], li_r[h:h + 1, :]
        btot = btot_c[:, h:h + 1]
        dmat = jnp.where(mask, b_c - b_r + i_r, -jnp.inf)
        inter = b_c + m_s
        m_t = jnp.maximum(inter, jnp.max(dmat, axis=1, keepdims=True))
        s = _dot_nt(qh, kh) * scale * jnp.exp(dmat - m_t)
        dec = jnp.exp(inter - m_t)
        ct = ct_ref[h]
        n_s = n_ref[h:h + 1, :]
        num = _dot(s.astype(BF16), vh) + (dec * scale) * _dot(qh, ct.astype(BF16))
        qn = jnp.sum(qh.astype(F32) * n_s, axis=1, keepdims=True) * scale
        den = jnp.sum(s, axis=1, keepdims=True) + dec * qn
        hh = num / jnp.maximum(jnp.abs(den), jnp.exp(-m_t))

        wst_c = btot - b_c + i_c
        wst_r = btot - b_r + i_r
        m_new = jnp.maximum(btot + m_s, jnp.max(wst_r, axis=1, keepdims=True))
        dstate = jnp.exp(btot + m_s - m_new)
        kw = kh.astype(F32) * jnp.exp(wst_c - m_new)
        ct_ref[h] = dstate * ct + _dot(kw.T.astype(BF16), vh)
        n_ref[h:h + 1, :] = dstate * n_s + jnp.sum(kw, axis=0, keepdims=True)
        m_ref[h:h + 1, :] = jnp.broadcast_to(m_new, (1, LANES))

        cols = slice(dv * h, dv * (h + 1))
        if final:
            hs = hh + hf_ref[:, cols]
            hn = hs * lax.rsqrt(jnp.mean(hs * hs, axis=-1, keepdims=True) + EPS) * nw_ref[:, cols]
            o_ref[:, cols] = (hn * _sigmoid(og_ref[:, cols].astype(F32))).astype(o_ref.dtype)
        else:
            o_ref[:, cols] = hh


def _mlstm(qk, pr, g_c, g_r, *, direction, n_lat, final_args=None, out_width=None):
    nb, t, _ = qk.shape
    q = SCAN_Q
    nch = t // q
    reverse = direction == 1
    cmap = _chunk_map(reverse, nch, n_lat // q)
    d = direction
    in_specs = [
        pl.BlockSpec((None, q, ML_QK_WIDTH), lambda b, k: (b, cmap(k), 0)),
        pl.BlockSpec((None, q, ML_QK_WIDTH), lambda b, k: (b, cmap(k), 1)),
        pl.BlockSpec((None, q, ML_WIDTH), lambda b, k: (b, cmap(k), 2 * ML_QK_WIDTH // ML_WIDTH)),
        pl.BlockSpec((None, None, q, 2 * ML_HEADS), lambda b, k: (b, d, cmap(k), 0)),
        pl.BlockSpec((None, None, 2 * ML_HEADS, q), lambda b, k: (b, d, 0, cmap(k))),
    ]
    args = [qk, qk, pr, g_c, g_r]
    final = final_args is not None
    if final:
        hf, norm_w = final_args
        in_specs += [pl.BlockSpec((None, q, ML_WIDTH), lambda b, k: (b, cmap(k), 0)),
                     pl.BlockSpec((None, q, ML_WIDTH), lambda b, k: (b, cmap(k), 2 * ML_QK_WIDTH // ML_WIDTH + 1)),
                     pl.BlockSpec((1, ML_WIDTH), lambda b, k: (0, 0))]
        args += [hf, pr, norm_w]
        out_shape = jax.ShapeDtypeStruct((nb, t, out_width), BF16)
    else:
        out_shape = jax.ShapeDtypeStruct((nb, t, ML_WIDTH), F32)
    return pl.pallas_call(
        functools.partial(_mlstm_body, reverse=reverse, final=final),
        out_shape=out_shape, grid=(nb, nch), in_specs=in_specs,
        out_specs=pl.BlockSpec((None, q, ML_WIDTH), lambda b, k: (b, cmap(k), 0)),
        scratch_shapes=[pltpu.VMEM((ML_HEADS, ML_QK_DIM, ML_V_DIM), F32), pltpu.VMEM((ML_HEADS, ML_QK_DIM), F32),
                        pltpu.VMEM((ML_HEADS, LANES), F32)],
        name="mlstm_bwd" if reverse else "mlstm_fwd",
        compiler_params=_cp(("parallel", "arbitrary")))(*args)


def _na_window_start(j, rows):
    return jnp.clip(j * NA_QROWS - NA_ROWS // 2, 0, rows - NA_KROWS)


def _na_body(q_ref, k_ref, v_ref, tab_ref, qw_ref, kw_ref, buf_ref, o_ref, kn_ref, *, n_lat, n_ctx):
    del buf_ref
    dh = NA_HEAD_DIM
    scale = dh ** -0.5
    rows = n_lat // GRID_W
    j = pl.program_id(2)
    nq, nk = NA_QROWS * GRID_W, NA_KROWS * GRID_W

    @pl.when(j == 0)
    def _():
        def body(i, carry):
            r = pl.ds(pl.multiple_of(i * LANES, LANES), LANES)
            kk = k_ref[r, :].astype(F32)
            kn = kk * lax.rsqrt(jnp.mean(kk * kk, axis=-1, keepdims=True) + EPS) * kw_ref[...]
            kn_ref[r, :] = kn.astype(kn_ref.dtype)
            return carry

        lax.fori_loop(0, (n_lat + n_ctx) // LANES, body, 0)

    qq = q_ref[...].astype(F32)
    qn = (qq * lax.rsqrt(jnp.mean(qq * qq, axis=-1, keepdims=True) + EPS) * (qw_ref[...] * scale)).astype(BF16)
    start = _na_window_start(j, rows)
    win = pl.ds(pl.multiple_of(start * GRID_W, GRID_W), nk)
    s = _dot_nt(qn, kn_ref[win, :])
    pieces = []
    for qr in range(NA_QROWS):
        d0 = start - (j * NA_QROWS + qr) + NA_ROWS - 1 + NA_QROWS
        pieces.append(s[qr * GRID_W:(qr + 1) * GRID_W] + tab_ref[d0])
    s = jnp.concatenate(pieces, axis=0)
    ctx = pl.ds(n_lat, n_ctx)
    sc = _dot_nt(qn, kn_ref[ctx, :])
    m = jnp.maximum(jnp.max(s, axis=1, keepdims=True), jnp.max(sc, axis=1, keepdims=True))
    p, pc = jnp.exp(s - m), jnp.exp(sc - m)
    denom = jnp.sum(p, axis=1, keepdims=True) + jnp.sum(pc, axis=1, keepdims=True)
    o = _dot(p.astype(BF16), v_ref[win, :]) + _dot(pc.astype(BF16), v_ref[ctx, :])
    o_ref[...] = (o / denom).astype(o_ref.dtype)


def _na_bias_table(rpb, rows):
    nd = 2 * NA_QROWS
    valid = np.zeros((nd, NA_KROWS), bool)
    seen = np.zeros((nd,), bool)
    for j in range(rows // NA_QROWS):
        start = int(np.clip(j * NA_QROWS - NA_ROWS // 2, 0, rows - NA_KROWS))
        for qr in range(NA_QROWS):
            r = j * NA_QROWS + qr
            rs = int(np.clip(r - NA_ROWS // 2, 0, rows - NA_ROWS))
            d = start - r + NA_ROWS - 1 + NA_QROWS
            v = np.array([(rs <= start + i < rs + NA_ROWS) for i in range(NA_KROWS)])
            assert 0 <= d < nd and (not seen[d] or (valid[d] == v).all())
            valid[d], seen[d] = v, True
    col = np.arange(GRID_W)
    cs = np.clip(col - NA_COLS // 2, 0, GRID_W - NA_COLS)
    in_win = (col[None, :] >= cs[:, None]) & (col[None, :] < cs[:, None] + NA_COLS)
    ndr, ndc, w = 2 * NA_ROWS - 1, 2 * NA_COLS - 1, GRID_W
    dr = (np.arange(nd)[:, None] - NA_QROWS) + np.arange(NA_KROWS)[None, :]
    ok = valid[:, None, :, None] & in_win[None, :, None, :] & (dr >= 0)[:, None, :, None] \
        & (dr < ndr)[:, None, :, None]
    ext = jnp.pad(rpb, ((0, 0), (0, 0), (w, w)), mode="edge")
    c0 = w + NA_COLS - 1
    t1 = jnp.stack([ext[:, :, c0 - qc:c0 - qc + w] for qc in range(w)], axis=2)
    t1p = jnp.pad(t1, ((0, 0), (NA_QROWS, nd + NA_KROWS - 1 - NA_QROWS - ndr), (0, 0), (0, 0)))
    wide = jnp.concatenate([t1p[:, m] for m in range(nd + NA_KROWS - 1)], axis=-1)
    vals = jnp.stack([wide[:, :, d * w:(d + NA_KROWS) * w] for d in range(nd)], axis=1)
    return jnp.where(ok.reshape(nd, w, NA_KROWS * w)[None], vals, -jnp.inf).astype(F32)


def _na(pr, tab, q_w, k_w, ybuf, *, n_lat, n_ctx, qcol, out_col):
    nb, t, _ = pr.shape
    dh, nh = NA_HEAD_DIM, NA_HEADS
    nq = NA_QROWS * GRID_W
    qb, ob = qcol // dh, out_col // dh
    nd = tab.shape[1]
    return pl.pallas_call(
        functools.partial(_na_body, n_lat=n_lat, n_ctx=n_ctx),
        out_shape=jax.ShapeDtypeStruct(ybuf.shape, ybuf.dtype), grid=(nb, nh, n_lat // nq),
        in_specs=[pl.BlockSpec((None, nq, dh), lambda b, h, j: (b, j, qb + h)),
                  pl.BlockSpec((None, t, dh), lambda b, h, j: (b, 0, qb + nh + h)),
                  pl.BlockSpec((None, t, dh), lambda b, h, j: (b, 0, qb + 2 * nh + h)),
                  pl.BlockSpec((None, nd, GRID_W, NA_KROWS * GRID_W), lambda b, h, j: (h, 0, 0, 0)),
                  pl.BlockSpec((1, dh), lambda b, h, j: (0, 0)),
                  pl.BlockSpec((1, dh), lambda b, h, j: (0, 0)),
                  pl.BlockSpec(memory_space=pl.ANY)],
        out_specs=pl.BlockSpec((None, nq, dh), lambda b, h, j: (b, j, ob + h)),
        scratch_shapes=[pltpu.VMEM((t, dh), BF16)],
        input_output_aliases={6: 0}, name="na",
        compiler_params=_cp(("parallel", "parallel", "arbitrary")))(pr, pr, pr, tab, q_w, k_w, ybuf)


def _rope_tables(n_lat, n_ctx, width):
    dh = ML_QK_DIM
    nf = dh // 4
    t = jnp.arange(n_lat)
    inv = ROPE_THETA ** (-jnp.arange(nf, dtype=F32) / nf)
    lane = np.arange(dh)
    pos = jnp.where((lane < dh // 2)[None, :], (t // GRID_W)[:, None], (t % GRID_W)[:, None]).astype(F32)
    ang = pos * inv[lane % nf][None, :]
    sign = np.where((lane % (dh // 2)) < nf, -1.0, 1.0).astype(np.float32)
    cos = jnp.concatenate([jnp.cos(ang), jnp.ones((n_ctx, dh), F32)], axis=0)
    sin = jnp.concatenate([jnp.sin(ang) * sign[None, :], jnp.zeros((n_ctx, dh), F32)], axis=0)
    reps = width // dh
    return jnp.tile(cos, (1, reps)), jnp.tile(sin, (1, reps))


def _ada_mod(c, c_ctx, w_all, b_all, layer):
    nb, d = c.shape
    rows = 16
    cv = jnp.concatenate([c, c_ctx[None, :], jnp.zeros((rows - nb - 1, d), F32)], axis=0)
    a = (cv * _sigmoid(cv)).astype(BF16)[None]
    out = _matmul(a, w_all, w_index=layer, tm=rows, tn=1024, tk=d, out_dtype=F32, name="ada_mod")[0]
    out = out[:nb + 1] + b_all[layer][None, :]
    return out.reshape(nb + 1, 6, d).transpose(1, 0, 2)


def _lat_ctx(m, nb):
    ctx = jnp.broadcast_to(m[nb][None, :], (nb, m.shape[1]))
    return jnp.stack([m[:nb], ctx], axis=1)[:, :, None, :]


def _pad_cols(w, n):
    return jnp.pad(w, ((0, 0), (0, n - w.shape[1])))


def _col_row_forms(g, groups):
    nb, t, _ = g.shape
    g5 = g.reshape(nb, t, 2, groups, -1)
    return g5.transpose(0, 2, 3, 1, 4), g5.transpose(0, 2, 3, 4, 1)


ROW_TILES = (1088, 1024, 512, 256)


def _row_tile(rows):
    return next(tm for tm in ROW_TILES if rows % tm == 0)


def _mod_pair(mods, shift, scale, nb):
    return jnp.stack([_lat_ctx(mods[shift], nb), _lat_ctx(mods[scale], nb)], axis=1)


def _out_projection(a, w_all, x, gate, layer, *, rows, n_lat, tk, next_norm, name):
    if next_norm is None:
        return _matmul_gated_residual(a, w_all, x, gate, w_index=layer, tm=_row_tile(rows), tn=1024, tk=tk,
                                      rows=rows, n_lat=n_lat, name=name), None
    return _matmul_gated_residual(a, w_all, x, gate, w_index=layer, tm=_row_tile(rows) // 4, tn=w_all.shape[2],
                                  tk=tk, rows=rows, n_lat=n_lat, norm=next_norm, name=name)


def _conv_ffn_block(x, u, mods, w_up_all, conv_w, conv_b, w_down_all, layer, *, rows, n_lat, segs, next_norm):
    nb = x.shape[0]
    up = _matmul(u, w_up_all, w_index=layer, tm=_row_tile(rows), tn=1024, tk=D_MODEL, out_dtype=BF16, name="ffn_up")
    hid = _conv_ffn(up, conv_w, conv_b[None, :], segs=segs, rows=rows)
    return _out_projection(hid, w_down_all, x, _lat_ctx(mods[5], nb), layer, rows=rows, n_lat=n_lat, tk=D_FF // 4,
                           next_norm=next_norm, name="ffn_down")


def _even_layer(x, u, mods, w_in_all, conv_w, conv_b, dt_bias, a_log, d_skip, ssd_norm_w,
                hy_w1, hy_b1, hy_w2, hy_b2, hy_w3, hy_freq, hy_bias, w_out_all, j, *, n_lat, n_ctx, next_norm):
    nb, t, d = x.shape
    segs = ((0, n_lat), (n_lat, n_ctx))
    tm = _row_tile(t)
    n_main = SSD_WIDTH + EV_CONV_CH
    pr = _matmul(u, w_in_all, w_index=j, cols=n_main, tm=tm, tn=1024, tk=d, out_dtype=BF16, name="ev_in")
    w_dt = _pad_cols(w_in_all[j, :, n_main:], LANES).astype(BF16)
    dtp = _matmul(u, w_dt[None], tm=tm, tn=LANES, tk=d, out_dtype=F32, name="ev_in_dt")[:, :, :2 * SSD_HEADS]
    cb = conv_b[None, :]
    xbc = _conv_silu(pr, conv_w, cb, xcol=SSD_WIDTH, wcol=0, width=SSD_XBC, segs=segs)
    x0, zin_l, zin_c = _conv_hyena(pr, conv_w, cb, xcol=SSD_WIDTH + SSD_XBC, wcol=SSD_XBC, segs=segs)

    dt_c, dt_r = _col_row_forms(dtp, SSD_GROUPS)
    db = dt_bias.reshape(2, SSD_GROUPS, 1, SSD_HPG)
    al = a_log.reshape(2, SSD_GROUPS, 1, SSD_HPG)
    scan_args = (xbc, dt_c, dt_r, db, db.transpose(0, 1, 3, 2), al, al.transpose(0, 1, 3, 2))
    yf = _ssd(*scan_args, direction=0, n_lat=n_lat)
    dsk = jnp.repeat(d_skip, SSD_HEAD_DIM)[None, :]
    ybuf = _ssd(*scan_args, direction=1, n_lat=n_lat, final_args=(yf, pr, dsk, ssd_norm_w[None, :]),
                out_width=SSD_WIDTH + HY_WIDTH)

    hyb = hy_bias[None, :]
    filt = functools.partial(_hyena_filters, w1=hy_w1, b1=hy_b1, w2=hy_w2, b2=hy_b2, w3=hy_w3, freq=hy_freq)
    ybuf = _hyena_longconv(zin_l, x0, filt(n_lat), hyb, ybuf, row0=0)
    ybuf = _hyena_longconv(zin_c, x0, filt(n_ctx), hyb, ybuf, row0=n_lat)
    return _out_projection(ybuf, w_out_all, x, _lat_ctx(mods[2], nb), j, rows=t, n_lat=n_lat, tk=1024,
                           next_norm=next_norm, name="ev_out")


def _odd_layer_latent(x, u, mods, w_in, conv_w, conv_b, gate_b, ml_norm_w, q_norm_w, k_norm_w, rpb,
                      w_out_all, j, *, n_lat, n_ctx, next_norm):
    nb, t, d = x.shape
    segs = ((0, n_lat), (n_lat, n_ctx))
    tm = _row_tile(t)
    o3 = 2 * ML_QK_WIDTH + 2 * ML_WIDTH
    o4 = o3 + 4 * ML_HEADS
    wb = w_in.astype(BF16)
    w_main = jnp.concatenate([wb[:, :o3], wb[:, o4:]], axis=1)
    pr = _matmul(u, w_main[None], tm=tm, tn=1024, tk=d, out_dtype=BF16, name="od_in")
    gates = _matmul(u, _pad_cols(wb[:, o3:o4], LANES)[None], tm=tm, tn=LANES, tk=d, out_dtype=F32,
                    name="od_in_gates")[:, :, :4 * ML_HEADS] + gate_b.reshape(-1)
    cos, sin = _rope_tables(n_lat, n_ctx, 256)
    qk = _conv_rope(pr, conv_w, conv_b[None, :], cos, sin, width=2 * ML_QK_WIDTH, segs=segs)
    g5 = gates.reshape(nb, t, 2, 2 * ML_HEADS)
    g_c, g_r = g5.transpose(0, 2, 1, 3), g5.transpose(0, 2, 3, 1)
    hf = _mlstm(qk, pr, g_c, g_r, direction=0, n_lat=n_lat)
    ybuf = _mlstm(qk, pr, g_c, g_r, direction=1, n_lat=n_lat, final_args=(hf, ml_norm_w[None, :]),
                  out_width=ML_WIDTH + NA_WIDTH)
    tab = _na_bias_table(rpb, n_lat // GRID_W)
    ybuf = _na(pr, tab, q_norm_w[None, :], k_norm_w[None, :], ybuf, n_lat=n_lat, n_ctx=n_ctx, qcol=o3,
               out_col=ML_WIDTH)
    return _out_projection(ybuf, w_out_all, x, _lat_ctx(mods[2], nb), j, rows=n_lat, n_lat=n_lat, tk=1024,
                           next_norm=next_norm, name="od_out")


def kernel(x, c, ctx, c_ctx, ada_w, ada_b, norm_w, ev_w_in, ev_conv_w, ev_conv_b, ssd_dt_bias, ssd_a_log, ssd_d,
           ssd_norm_w, hy_w1, hy_b1, hy_w2, hy_b2, hy_w3, hy_freq, hy_bias, ev_w_out, od_w_in, ml_conv_w, ml_conv_b,
           ml_gate_b, ml_norm_w, na_q_norm_w, na_k_norm_w, na_rpb, od_w_out, ffn_w_up, ffn_conv_w, ffn_conv_b,
           ffn_w_down):
    depth = ada_w.shape[0]
    assert depth == 2, "layer schedule below is written for one even and one odd layer"
    n_lat, n_ctx = x.shape[1], ctx.shape[1]
    t = n_lat + n_ctx
    both = ((0, n_lat), (n_lat, n_ctx))
    xs = jnp.concatenate([x, ctx], axis=1)

    nb = x.shape[0]
    w_down = ffn_w_down.astype(BF16)
    mods0 = _ada_mod(c, c_ctx, ada_w, ada_b, 0)
    mods1 = _ada_mod(c, c_ctx, ada_w, ada_b, 1)
    norm = lambda layer, sub, mods: (norm_w[layer, sub][None, :], _mod_pair(mods, 3 * sub, 3 * sub + 1, nb))

    w0, mod0 = norm(0, 0, mods0)
    u = _rmsmod(xs, w0, mod0, tm=_row_tile(t), rows=t, n_lat=n_lat)
    xs, u = _even_layer(xs, u, mods0, ev_w_in, ev_conv_w[0], ev_conv_b[0], ssd_dt_bias[0], ssd_a_log[0],
                        ssd_d[0], ssd_norm_w[0], hy_w1[0], hy_b1[0], hy_w2[0], hy_b2[0], hy_w3[0], hy_freq[0],
                        hy_bias[0], ev_w_out.astype(BF16), 0, n_lat=n_lat, n_ctx=n_ctx, next_norm=norm(0, 1, mods0))
    xs, u = _conv_ffn_block(xs, u, mods0, ffn_w_up, ffn_conv_w[0], ffn_conv_b[0], w_down, 0,
                            rows=t, n_lat=n_lat, segs=both, next_norm=norm(1, 0, mods1))
    xl, u = _odd_layer_latent(xs, u, mods1, od_w_in[0], ml_conv_w[0], ml_conv_b[0], ml_gate_b[0], ml_norm_w[0],
                              na_q_norm_w[0], na_k_norm_w[0], na_rpb[0], od_w_out.astype(BF16), 0,
                              n_lat=n_lat, n_ctx=n_ctx, next_norm=norm(1, 1, mods1))
    out, _ = _conv_ffn_block(xl, u, mods1, ffn_w_up, ffn_conv_w[1], ffn_conv_b[1], w_down, 1,
                             rows=n_lat, n_lat=n_lat, segs=((0, n_lat),), next_norm=None)
    return out
```

```python
import functools
import math

import numpy as np
import jax
import jax.numpy as jnp
from jax import lax
from jax.experimental import pallas as pl
from jax.experimental.pallas import tpu as pltpu

F32 = jnp.float32
BF16 = jnp.bfloat16
HI = lax.Precision.HIGHEST

D_MODEL = 2048
SEQ = 4096
CTX_LEN = 256
GRID_W = 64
EPS = 1e-6
CONV_W = 3
SSD_WIDTH = D_MODEL
SSD_HEAD_DIM = 64
SSD_HEADS = SSD_WIDTH // SSD_HEAD_DIM
SSD_STATE = 128
SSD_GROUPS = 4
SSD_HPG = SSD_HEADS // SSD_GROUPS
SSD_GW = SSD_WIDTH // SSD_GROUPS
SSD_XBC = SSD_WIDTH + 2 * SSD_GROUPS * SSD_STATE
HY_WIDTH = D_MODEL
HY_EMB = 33
HY_BANDS = (HY_EMB - 1) // 2
HY_ORDER = 64
HY_FAST = 0.3
HY_SLOW = 1.5
HY_TARGET = 1e-2
ML_HEADS = 8
ML_WIDTH = D_MODEL
ML_V_DIM = ML_WIDTH // ML_HEADS
ML_QK_DIM = ML_V_DIM // 2
ML_QK_WIDTH = ML_HEADS * ML_QK_DIM
ROPE_THETA = 10000.0
NA_WIDTH = D_MODEL
NA_HEAD_DIM = 128
NA_HEADS = NA_WIDTH // NA_HEAD_DIM
NA_ROWS = 8
NA_COLS = 16
NA_QROWS = 8
NA_KROWS = 16
D_FF = 256 * ((8 * D_MODEL // 3 + 255) // 256)
EV_CONV_CH = SSD_XBC + 3 * HY_WIDTH
SCAN_Q = 128
LANES = 128
SUBLANES = 8
VMEM_LIMIT = 48 * 1024 * 1024


def _cp(sem, vmem=VMEM_LIMIT):
    return pltpu.CompilerParams(dimension_semantics=sem, vmem_limit_bytes=vmem)


def _sigmoid(x):
    return 1.0 / (1.0 + jnp.exp(-x))


def _silu(x):
    return x * _sigmoid(x)


def _softplus(x):
    return jnp.maximum(x, 0.0) + jnp.log(1.0 + jnp.exp(-jnp.abs(x)))


def _log_sigmoid(x):
    return -_softplus(-x)


def _dot(a, b, precision=None):
    return jnp.dot(a, b, preferred_element_type=F32, precision=precision)


def _dot_nt(a, b):
    return lax.dot_general(a, b, (((1,), (1,)), ((), ())), preferred_element_type=F32)


def _split_bf16(x, terms):
    out = []
    for _ in range(terms):
        part = x.astype(BF16)
        out.append(part)
        x = x - part.astype(F32)
    return out


def _dot_sel(x, sel, terms):
    selb = sel.astype(BF16)
    return sum(_dot(part, selb) for part in _split_bf16(x, terms))


def _sel_dot(sel, x, terms):
    selb = sel.astype(BF16)
    return sum(_dot(selb, part) for part in _split_bf16(x, terms))


def _mm_body(*refs, nk, n_extra, epilogue, has_buf, cache_w):
    a_ref, w_ref = refs[0], refs[1]
    extra = refs[2:2 + n_extra]
    pos = 2 + n_extra + (1 if has_buf else 0)
    o_ref = refs[pos]
    row_blk = pl.program_id(2)
    if cache_w:
        wb_ref = refs[pos + 1]

        @pl.when((pl.program_id(1) == 0) & (row_blk == 0))
        def _():
            wb_ref[...] = w_ref[...].astype(BF16)

        w_val = wb_ref[...]
    else:
        w_val = w_ref[...].astype(BF16)
    part = _dot(a_ref[...].astype(BF16), w_val)

    def finish(acc):
        val = acc if epilogue is None else epilogue(acc, row_blk, *extra)
        o_ref[...] = val.astype(o_ref.dtype)

    if nk == 1:
        finish(part)
    else:
        acc_ref = refs[pos + 1]
        k = pl.program_id(3)

        @pl.when(k == 0)
        def _():
            acc_ref[...] = part

        @pl.when(k > 0)
        def _():
            acc_ref[...] += part

        @pl.when(k == nk - 1)
        def _():
            finish(acc_ref[...])


def _matmul(a, w, *, tm, tn, tk, out_dtype, rows=None, cols=None, w_index=None, extra=(), epilogue=None,
            out_buf=None, out_row_blk=0, out_col_blk=0, name="mm"):
    ba, ta, kdim = a.shape
    bw, kdim2, n = w.shape
    assert kdim == kdim2
    nb = ba if w_index is not None else max(ba, bw)
    rows = ta if rows is None else rows
    n = n if cols is None else cols
    assert rows % tm == 0 and n % tn == 0 and kdim % tk == 0
    ni, nj, nk = rows // tm, n // tn, kdim // tk
    a_map = (lambda j, b, i, k: (b, i, k)) if ba > 1 else (lambda j, b, i, k: (0, i, k))
    if w_index is not None:
        w_map = lambda j, b, i, k: (w_index, k, j)
    else:
        w_map = (lambda j, b, i, k: (b, k, j)) if bw > 1 else (lambda j, b, i, k: (0, k, j))
    cache_w = w.dtype != BF16 and nk == 1 and (w_index is not None or bw == 1)
    in_specs = [pl.BlockSpec((None, tm, tk), a_map), pl.BlockSpec((None, tk, tn), w_map)]
    args = [a, w]
    for arr, spec in extra:
        args.append(arr)
        in_specs.append(spec)
    aliases = {}
    if out_buf is not None:
        aliases = {len(args): 0}
        args.append(out_buf)
        in_specs.append(pl.BlockSpec(memory_space=pl.ANY))
        out_shape = jax.ShapeDtypeStruct(out_buf.shape, out_buf.dtype)
    else:
        out_shape = jax.ShapeDtypeStruct((nb, rows, n), out_dtype)
    out_spec = pl.BlockSpec((None, tm, tn), lambda j, b, i, k: (b, i + out_row_blk, j + out_col_blk))
    scratch = [pltpu.VMEM((tm, tn), F32)] if nk > 1 else []
    if cache_w:
        scratch = [pltpu.VMEM((tk, tn), BF16)]
    body = functools.partial(_mm_body, nk=nk, n_extra=len(extra), epilogue=epilogue,
                             has_buf=out_buf is not None, cache_w=cache_w)
    inner = "arbitrary" if cache_w else "parallel"
    return pl.pallas_call(
        body, out_shape=out_shape, grid=(nj, nb, ni, nk), in_specs=in_specs, out_specs=out_spec,
        scratch_shapes=scratch, input_output_aliases=aliases, name=name,
        compiler_params=_cp(("parallel", inner, inner, "arbitrary")))(*args)


def _row_is_lat(row_blk, tm, n_lat):
    rows = row_blk * tm + lax.broadcasted_iota(jnp.int32, (tm, 1), 0)
    return rows < n_lat


def _gated_residual_epilogue(acc, row_blk, res_ref, gate_ref, *, tm, n_lat):
    gate = jnp.where(_row_is_lat(row_blk, tm, n_lat), gate_ref[0], gate_ref[1])
    return res_ref[...] + gate * acc


def _modulated_rmsnorm(x, is_lat, w_ref, mod_ref):
    y = x * lax.rsqrt(jnp.mean(x * x, axis=-1, keepdims=True) + EPS) * w_ref[...]
    shift = jnp.where(is_lat, mod_ref[0, 0], mod_ref[0, 1])
    scale = jnp.where(is_lat, mod_ref[1, 0], mod_ref[1, 1])
    return y * (1.0 + scale) + shift


def _matmul_gated_residual(a, w, res, gate, *, w_index, tm, tn, tk, rows, n_lat, name):
    extra = [(res, pl.BlockSpec((None, tm, tn), lambda j, b, i, k: (b, i, j))),
             (gate, pl.BlockSpec((None, 2, 1, tn), lambda j, b, i, k: (b, 0, 0, j)))]
    epi = functools.partial(_gated_residual_epilogue, tm=tm, n_lat=n_lat)
    return _matmul(a, w, w_index=w_index, tm=tm, tn=tn, tk=tk, out_dtype=F32, rows=rows, extra=extra,
                   epilogue=epi, name=name)


def _rmsmod_body(x_ref, w_ref, mod_ref, o_ref, *, tm, n_lat):
    is_lat = _row_is_lat(pl.program_id(2), tm, n_lat)
    o_ref[...] = _modulated_rmsnorm(x_ref[...], is_lat, w_ref, mod_ref).astype(o_ref.dtype)


def _rmsmod(x, w, mod, *, tm, rows, n_lat):
    nb, _, d = x.shape
    tm = tm // 4
    assert rows % tm == 0 and tm % 16 == 0
    body = functools.partial(_rmsmod_body, tm=tm, n_lat=n_lat)
    return pl.pallas_call(
        body, out_shape=jax.ShapeDtypeStruct((nb, rows, d), BF16), grid=(1, nb, rows // tm),
        in_specs=[pl.BlockSpec((None, tm, d), lambda j, b, i: (b, i, 0)),
                  pl.BlockSpec((1, d), lambda j, b, i: (0, 0)),
                  pl.BlockSpec((None, 2, 2, 1, d), lambda j, b, i: (b, 0, 0, 0, 0))],
        out_specs=pl.BlockSpec((None, tm, d), lambda j, b, i: (b, i, 0)),
        name="rmsmod", compiler_params=_cp(("parallel", "parallel", "parallel")))(x, w, mod)


CONV_CH = 128
CONV_HALO = 16


def _shift_matrices():
    r = lax.broadcasted_iota(jnp.int32, (CONV_CH, CONV_CH + 2 * CONV_HALO), 0)
    c = lax.broadcasted_iota(jnp.int32, (CONV_CH, CONV_CH + 2 * CONV_HALO), 1)
    return (c == r + CONV_HALO - 1).astype(BF16), (c == r + CONV_HALO + 1).astype(BF16)


def _conv_at(x_ref, w_ref, b_ref, shifts, r0, n, i):
    t0 = pl.multiple_of(i * CONV_CH, CONV_CH)
    cur = x_ref[pl.ds(r0 + t0, CONV_CH), :]
    lo = pl.multiple_of(jnp.maximum(t0 - CONV_HALO, 0), CONV_HALO)
    hi = pl.multiple_of(jnp.minimum(t0 + CONV_CH, n - CONV_HALO), CONV_HALO)
    zero = jnp.zeros((CONV_HALO, cur.shape[1]), cur.dtype)
    tail = jnp.where(i > 0, x_ref[pl.ds(r0 + lo, CONV_HALO), :], zero)
    head = jnp.where(i < n // CONV_CH - 1, x_ref[pl.ds(r0 + hi, CONV_HALO), :], zero)
    win = jnp.concatenate([tail, cur, head], axis=0)
    return (w_ref[0:1, :] * _dot(shifts[0], win) + w_ref[1:2, :] * cur.astype(F32)
            + w_ref[2:3, :] * _dot(shifts[1], win) + b_ref[...])


def _conv_loop(segs, fn):
    for si, (r0, n) in enumerate(segs):
        def body(i, carry, r0=r0, n=n, si=si):
            fn(si, r0, n, i)
            return carry

        lax.fori_loop(0, n // CONV_CH, body, 0, unroll=min(4, n // CONV_CH))


def _conv_silu_body(x_ref, w_ref, b_ref, o_ref, *, segs):
    shifts = _shift_matrices()

    def fn(si, r0, n, i):
        y = _silu(_conv_at(x_ref, w_ref, b_ref, shifts, r0, n, i))
        o_ref[pl.ds(r0 + pl.multiple_of(i * CONV_CH, CONV_CH), CONV_CH), :] = y.astype(o_ref.dtype)

    _conv_loop(segs, fn)


def _conv_rope_body(x_ref, w_ref, b_ref, cos_ref, sin_ref, o_ref, *, segs, tc):
    shifts = _shift_matrices()
    lane = lax.broadcasted_iota(jnp.int32, (CONV_CH, tc), 1)
    first = (lane % (ML_QK_DIM // 2)) < (ML_QK_DIM // 4)
    nf = ML_QK_DIM // 4

    def fn(si, r0, n, i):
        y = _silu(_conv_at(x_ref, w_ref, b_ref, shifts, r0, n, i))
        partner = jnp.where(first, pltpu.roll(y, tc - nf, 1), pltpu.roll(y, nf, 1))
        rows = pl.ds(r0 + pl.multiple_of(i * CONV_CH, CONV_CH), CONV_CH)
        o_ref[rows, :] = (y * cos_ref[rows, :] + partner * sin_ref[rows, :]).astype(o_ref.dtype)

    _conv_loop(segs, fn)


def _conv_ffn_body(a_ref, x_ref, w_ref, b_ref, o_ref, *, segs):
    shifts = _shift_matrices()

    def fn(si, r0, n, i):
        rows = pl.ds(r0 + pl.multiple_of(i * CONV_CH, CONV_CH), CONV_CH)
        g = _silu(_conv_at(x_ref, w_ref, b_ref, shifts, r0, n, i))
        o_ref[rows, :] = (a_ref[rows, :].astype(F32) * g).astype(o_ref.dtype)

    _conv_loop(segs, fn)


def _conv_hyena_body(x0_ref, x1_ref, v_ref, w0_ref, b0_ref, w1_ref, b1_ref, w2_ref, b2_ref,
                     ox0_ref, ozl_ref, ozc_ref, *, segs):
    shifts = _shift_matrices()
    outs = (ozl_ref, ozc_ref)

    def fn(si, r0, n, i):
        t0 = pl.multiple_of(i * CONV_CH, CONV_CH)
        x0 = _conv_at(x0_ref, w0_ref, b0_ref, shifts, r0, n, i)
        ox0_ref[pl.ds(r0 + t0, CONV_CH), :] = x0.astype(ox0_ref.dtype)
        zin = _conv_at(x1_ref, w1_ref, b1_ref, shifts, r0, n, i) * _conv_at(v_ref, w2_ref, b2_ref, shifts, r0, n, i)
        outs[si][pl.ds(t0, CONV_CH), :] = zin.astype(outs[si].dtype)

    _conv_loop(segs, fn)


def _conv_specs(t_in, tc, xcol_blk, wcol_blk):
    return [pl.BlockSpec((None, t_in, tc), lambda b, j: (b, 0, j + xcol_blk)),
            pl.BlockSpec((CONV_W, tc), lambda b, j: (0, j + wcol_blk)),
            pl.BlockSpec((1, tc), lambda b, j: (0, j + wcol_blk))]


def _conv_silu(x, w, b, *, xcol, wcol, width, segs, tc=256):
    nb, t_in, _ = x.shape
    return pl.pallas_call(
        functools.partial(_conv_silu_body, segs=segs),
        out_shape=jax.ShapeDtypeStruct((nb, t_in, width), BF16), grid=(nb, width // tc),
        in_specs=_conv_specs(t_in, tc, xcol // tc, wcol // tc),
        out_specs=pl.BlockSpec((None, t_in, tc), lambda b, j: (b, 0, j)),
        name="conv_silu", compiler_params=_cp(("parallel", "parallel")))(x, w, b)


def _conv_rope(x, w, b, cos, sin, *, width, segs, tc=256):
    nb, t_in, _ = x.shape
    tab = pl.BlockSpec((t_in, tc), lambda b, j: (0, 0))
    return pl.pallas_call(
        functools.partial(_conv_rope_body, segs=segs, tc=tc),
        out_shape=jax.ShapeDtypeStruct((nb, t_in, width), BF16), grid=(nb, width // tc),
        in_specs=_conv_specs(t_in, tc, 0, 0) + [tab, tab],
        out_specs=pl.BlockSpec((None, t_in, tc), lambda b, j: (b, 0, j)),
        name="conv_rope", compiler_params=_cp(("parallel", "parallel")))(x, w, b, cos, sin)


def _conv_ffn(up, w, b, *, segs, rows, tc=256):
    nb, t_in, two_ff = up.shape
    ff = two_ff // 2
    return pl.pallas_call(
        functools.partial(_conv_ffn_body, segs=segs),
        out_shape=jax.ShapeDtypeStruct((nb, rows, ff), BF16), grid=(nb, ff // tc),
        in_specs=[pl.BlockSpec((None, rows, tc), lambda b, j: (b, 0, j))] + _conv_specs(rows, tc, ff // tc, 0),
        out_specs=pl.BlockSpec((None, rows, tc), lambda b, j: (b, 0, j)),
        name="conv_ffn", compiler_params=_cp(("parallel", "parallel")))(up, up, w, b)


def _conv_hyena(pr, w, b, *, xcol, wcol, segs, tc=256):
    nb, t_in, _ = pr.shape
    (_, n_lat), (_, n_ctx) = segs
    wd = HY_WIDTH
    in_specs = [pl.BlockSpec((None, t_in, tc), functools.partial(lambda b, j, o: (b, 0, j + o), o=(xcol + m * wd) // tc))
                for m in range(3)]
    wargs = []
    for m in range(3):
        off = (wcol + m * wd) // tc
        in_specs.append(pl.BlockSpec((CONV_W, tc), functools.partial(lambda b, j, o: (0, j + o), o=off)))
        in_specs.append(pl.BlockSpec((1, tc), functools.partial(lambda b, j, o: (0, j + o), o=off)))
        wargs += [w, b]
    return pl.pallas_call(
        functools.partial(_conv_hyena_body, segs=segs),
        out_shape=(jax.ShapeDtypeStruct((nb, t_in, wd), BF16), jax.ShapeDtypeStruct((nb, n_lat, wd), BF16),
                   jax.ShapeDtypeStruct((nb, n_ctx, wd), BF16)),
        grid=(nb, wd // tc), in_specs=in_specs,
        out_specs=(pl.BlockSpec((None, t_in, tc), lambda b, j: (b, 0, j)),
                   pl.BlockSpec((None, n_lat, tc), lambda b, j: (b, 0, j)),
                   pl.BlockSpec((None, n_ctx, tc), lambda b, j: (b, 0, j))),
        name="conv_hyena", compiler_params=_cp(("parallel", "parallel")))(pr, pr, pr, *wargs)


def _scan_masks(reverse):
    q = SCAN_Q
    ii = lax.broadcasted_iota(jnp.int32, (q, q), 0)
    jj = lax.broadcasted_iota(jnp.int32, (q, q), 1)
    mask = (jj >= ii) if reverse else (jj <= ii)
    mask_t = (jj <= ii) if reverse else (jj >= ii)
    return mask, mask.astype(F32), mask_t.astype(F32)


def _chunk_map(reverse, n_chunks, n_lat_chunks):
    if reverse:
        return lambda k: n_chunks - 1 - k
    return lambda k: (k + n_lat_chunks) % n_chunks


def _ssd_body(*refs, reverse, final):
    (xs_ref, b_ref, c_ref, dtc_ref, dtr_ref, dbc_ref, dbr_ref, alc_ref, alr_ref) = refs[:9]
    if final:
        yf_ref, z_ref, dsk_ref, nw_ref, o_ref, h_ref = refs[9:]
    else:
        o_ref, h_ref = refs[9:]
    q, gw, hd = SCAN_Q, SSD_GW, SSD_HEAD_DIM

    @pl.when(pl.program_id(2) == 0)
    def _():
        h_ref[...] = jnp.zeros_like(h_ref)

    xs = xs_ref[...].astype(F32)
    bm = b_ref[...]
    cm = c_ref[...]
    dt_c = _softplus(dtc_ref[...] + dbc_ref[...])
    dt_r = _softplus(dtr_ref[...] + dbr_ref[...])
    da_c = dt_c * (-jnp.exp(alc_ref[...]))
    da_r = dt_r * (-jnp.exp(alr_ref[...]))
    mask, maskf, mask_tf = _scan_masks(reverse)
    acum_c = _sel_dot(maskf, da_c, 3)
    acum_r = _dot_sel(da_r, mask_tf, 3)
    atot = jnp.sum(da_c, axis=0, keepdims=True)

    er = lax.broadcasted_iota(jnp.int32, (SSD_HPG, gw), 0)
    ec = lax.broadcasted_iota(jnp.int32, (SSD_HPG, gw), 1)
    expand = (ec // hd == er).astype(F32)
    stack = jnp.concatenate([dt_c, jnp.exp(atot - acum_c), jnp.exp(acum_c),
                             jnp.broadcast_to(jnp.exp(atot), (SUBLANES, SSD_HPG))], axis=0)
    wide = _dot_sel(stack, expand, 2)
    dt_e, ws_e, eac_e, eat_e = wide[0:q], wide[q:2 * q], wide[2 * q:3 * q], wide[3 * q:3 * q + 1]

    xq = xs * dt_e
    h_prev = h_ref[...]
    cb = _dot_nt(cm, bm)
    y = eac_e * _dot(cm, h_prev.astype(BF16))
    lane = lax.broadcasted_iota(jnp.int32, (q, 2 * hd), 1)
    pieces = []
    for p in range(SSD_HPG // 2):
        ms = []
        for r in (2 * p, 2 * p + 1):
            seg = acum_c[:, r:r + 1] - acum_r[r:r + 1, :]
            ms.append((cb * jnp.where(mask, jnp.exp(seg), 0.0)).astype(BF16))
        xp = xq[:, 2 * hd * p:2 * hd * (p + 1)]
        rhs = jnp.concatenate([jnp.where(lane < hd, xp, 0.0), jnp.where(lane >= hd, xp, 0.0)], axis=0)
        pieces.append(_dot(jnp.concatenate(ms, axis=1), rhs.astype(BF16)))
    y = y + jnp.concatenate(pieces, axis=1)

    bt = bm.astype(F32).T.astype(BF16)
    h_ref[...] = eat_e * h_prev + _dot(bt, (xq * ws_e).astype(BF16))

    if final:
        ytot = y + yf_ref[...] + xs * dsk_ref[...]
        g = ytot * _silu(z_ref[...].astype(F32))
        o = g * lax.rsqrt(jnp.mean(g * g, axis=-1, keepdims=True) + EPS) * nw_ref[...]
        o_ref[...] = o.astype(o_ref.dtype)
    else:
        o_ref[...] = y


def _ssd(xbc, dt_c, dt_r, db_c, db_r, al_c, al_r, *, direction, n_lat, final_args=None, out_width=None):
    nb, t, _ = xbc.shape
    q, gw, hpg = SCAN_Q, SSD_GW, SSD_HPG
    nch = t // q
    reverse = direction == 1
    cmap = _chunk_map(reverse, nch, n_lat // q)
    nblk = SSD_STATE // LANES
    b0 = SSD_WIDTH // SSD_STATE
    c0 = b0 + SSD_GROUPS * nblk
    d = direction
    in_specs = [
        pl.BlockSpec((None, q, gw), lambda b, g, k: (b, cmap(k), g)),
        pl.BlockSpec((None, q, SSD_STATE), lambda b, g, k: (b, cmap(k), b0 + g)),
        pl.BlockSpec((None, q, SSD_STATE), lambda b, g, k: (b, cmap(k), c0 + g)),
        pl.BlockSpec((None, None, None, q, hpg), lambda b, g, k: (b, d, g, cmap(k), 0)),
        pl.BlockSpec((None, None, None, hpg, q), lambda b, g, k: (b, d, g, 0, cmap(k))),
        pl.BlockSpec((None, None, 1, hpg), lambda b, g, k: (d, g, 0, 0)),
        pl.BlockSpec((None, None, hpg, 1), lambda b, g, k: (d, g, 0, 0)),
        pl.BlockSpec((None, None, 1, hpg), lambda b, g, k: (d, g, 0, 0)),
        pl.BlockSpec((None, None, hpg, 1), lambda b, g, k: (d, g, 0, 0)),
    ]
    args = [xbc, xbc, xbc, dt_c, dt_r, db_c, db_r, al_c, al_r]
    final = final_args is not None
    if final:
        yf, pr, dskip, norm_w = final_args
        in_specs += [pl.BlockSpec((None, q, gw), lambda b, g, k: (b, cmap(k), g)),
                     pl.BlockSpec((None, q, gw), lambda b, g, k: (b, cmap(k), g)),
                     pl.BlockSpec((1, gw), lambda b, g, k: (0, g)),
                     pl.BlockSpec((1, gw), lambda b, g, k: (0, g))]
        args += [yf, pr, dskip, norm_w]
        out_shape = jax.ShapeDtypeStruct((nb, t, out_width), BF16)
    else:
        out_shape = jax.ShapeDtypeStruct((nb, t, SSD_WIDTH), F32)
    return pl.pallas_call(
        functools.partial(_ssd_body, reverse=reverse, final=final),
        out_shape=out_shape, grid=(nb, SSD_GROUPS, nch), in_specs=in_specs,
        out_specs=pl.BlockSpec((None, q, gw), lambda b, g, k: (b, cmap(k), g)),
        scratch_shapes=[pltpu.VMEM((SSD_STATE, gw), F32)],
        name="ssd_bwd" if reverse else "ssd_fwd",
        compiler_params=_cp(("parallel", "parallel", "arbitrary")))(*args)


def _hyfilt_body(f_ref, w1_ref, b1_ref, w2_ref, b2_ref, w3_ref, fr_ref, dl_ref, o_ref, *, tl):
    f = f_ref[...]
    h = jnp.sin(fr_ref[0:1, :] * (_dot(f, w1_ref[...], HI) + b1_ref[...]))
    h = jnp.sin(fr_ref[1:2, :] * (_dot(h, w2_ref[...], HI) + b2_ref[...]))
    h = _dot(h, w3_ref[...], HI)
    dec = jnp.exp(-f[:, 0:1] * dl_ref[...])
    row = pl.program_id(0) * tl + lax.broadcasted_iota(jnp.int32, (tl, 1), 0)
    wd = HY_WIDTH
    o_ref[:, 0:wd] = (h[:, 0:wd] * dec).astype(o_ref.dtype)
    o_ref[:, wd:2 * wd] = jnp.where(row == 0, 0.0, h[:, wd:2 * wd] * dec).astype(o_ref.dtype)


def _hyena_filters(length, w1, b1, w2, b2, w3, freq, *, tl=256):
    p = LANES
    t = jnp.linspace(0.0, 1.0, length, dtype=F32)[:, None]
    w = 2.0 * math.pi * jnp.arange(length, dtype=F32)[:, None] / length
    f = jnp.linspace(1e-4, HY_BANDS - 1, HY_BANDS, dtype=F32)[None, :]
    feats = jnp.concatenate([t, jnp.cos(f * w), -jnp.sin(f * w)], axis=-1)
    feats = jnp.pad(feats, ((0, 0), (0, p - HY_EMB)))
    po = p - HY_ORDER
    w1p = jnp.pad(w1, ((0, p - HY_EMB), (0, po)))
    w2p = jnp.pad(w2, ((0, po), (0, po)))
    w3p = jnp.pad(w3, ((0, po), (0, 0)))
    b1p = jnp.pad(b1[None, :], ((0, 0), (0, po)))
    b2p = jnp.pad(b2[None, :], ((0, 0), (0, po)))
    frp = jnp.pad(freq, ((0, 0), (0, po)))
    deltas = jnp.abs(jnp.linspace(math.log(HY_TARGET) / HY_FAST, math.log(HY_TARGET) / HY_SLOW, HY_WIDTH,
                                  dtype=F32))[None, :]
    full = lambda shape: pl.BlockSpec(shape, lambda i: (0,) * len(shape))
    out = pl.pallas_call(
        functools.partial(_hyfilt_body, tl=tl),
        out_shape=jax.ShapeDtypeStruct((length, 2 * HY_WIDTH), BF16), grid=(length // tl,),
        in_specs=[pl.BlockSpec((tl, p), lambda i: (i, 0)), full((p, p)), full((1, p)), full((p, p)), full((1, p)),
                  full((p, 2 * HY_WIDTH)), full((2, p)), full((1, HY_WIDTH))],
        out_specs=pl.BlockSpec((tl, 2 * HY_WIDTH), lambda i: (i, 0)),
        name="hyena_filters", compiler_params=_cp(("parallel",)))(feats, w1p, b1p, w2p, b2p, w3p, frp, deltas)
    return out[None]


def _dft_matrices(length):
    n = 2 * length
    k = jnp.arange(length, dtype=jnp.int32)[:, None]
    t = jnp.arange(length, dtype=jnp.int32)[None, :]
    step = 1 << (int(math.log2(length)) // 2)
    part = jnp.arange(length // step, dtype=jnp.int32)[:, None]
    ang_hi = ((part * step * t) % n).astype(F32) * (2.0 * math.pi / n)
    ang_lo = ((jnp.arange(step, dtype=jnp.int32)[:, None] * t) % n).astype(F32) * (2.0 * math.pi / n)
    ch, sh = jnp.cos(ang_hi)[:, None, :], jnp.sin(ang_hi)[:, None, :]
    cl, sl = jnp.cos(ang_lo)[None, :, :], jnp.sin(ang_lo)[None, :, :]
    c = (ch * cl - sh * sl).reshape(length, length)
    s = (sh * cl + ch * sl).reshape(length, length)
    alt_t = jnp.where(t % 2 == 0, 1.0, -1.0).astype(F32)
    fwd = jnp.concatenate([c, jnp.where(k == 0, alt_t, -s)], axis=0)
    alt_r = jnp.where(k % 2 == 0, 1.0, -1.0).astype(F32)
    inv_re = jnp.where(t == 0, 1.0 / n, (2.0 / n) * c)
    inv_im = jnp.where(t == 0, alt_r / n, (-2.0 / n) * s)
    inv = jnp.concatenate([inv_re, inv_im], axis=1)
    return fwd.astype(BF16)[None], inv.astype(BF16)[None]


def _specmul_body(z_ref, kf_ref, kb_ref, o_ref, *, tr):
    zr, zi = z_ref[0], z_ref[1]
    kr = kf_ref[0] + kb_ref[0]
    ki = kf_ref[1] - kb_ref[1]
    row = pl.program_id(1) * tr + lax.broadcasted_iota(jnp.int32, (tr, 1), 0)
    first = row == 0
    yr = jnp.where(first, zr * kr, zr * kr - zi * ki)
    yi = jnp.where(first, zi * (kf_ref[1] + kb_ref[1]), zr * ki + zi * kr)
    o_ref[0] = yr.astype(o_ref.dtype)
    o_ref[1] = yi.astype(o_ref.dtype)


def _specmul(z, ksp, *, tr, tc=512):
    nb, l2, c = z.shape
    ln = l2 // 2
    z4 = z.reshape(nb, 2, ln, c)
    k3 = ksp.reshape(2, ln, 2 * c)
    out = pl.pallas_call(
        functools.partial(_specmul_body, tr=tr),
        out_shape=jax.ShapeDtypeStruct((nb, 2, ln, c), BF16), grid=(nb, ln // tr, c // tc),
        in_specs=[pl.BlockSpec((None, 2, tr, tc), lambda b, i, j: (b, 0, i, j)),
                  pl.BlockSpec((2, tr, tc), lambda b, i, j: (0, i, j)),
                  pl.BlockSpec((2, tr, tc), lambda b, i, j: (0, i, j + c // tc))],
        out_specs=pl.BlockSpec((None, 2, tr, tc), lambda b, i, j: (b, 0, i, j)),
        name="specmul", compiler_params=_cp(("parallel", "parallel", "parallel")))(z4, k3, k3)
    return out.reshape(nb, l2, c)


def _hyena_out_epilogue(acc, row_blk, x0_ref, zin_ref, db_ref):
    del row_blk
    zin = zin_ref[...].astype(F32)
    return x0_ref[...].astype(F32) * (acc + zin * db_ref[...])


def _hyena_longconv(zin, x0, hfb, dbias, ybuf, *, row0, tn=1024):
    nb, ln, c = zin.shape
    tm_f, tm_i = min(1024, 2 * ln), min(1024, ln)
    tk_f, tk_i = min(2048, ln), min(2048, 2 * ln)
    assert row0 % tm_i == 0
    fwd, inv = _dft_matrices(ln)
    ksp = _matmul(fwd, hfb, tm=tm_f, tn=tn, tk=tk_f, out_dtype=F32, name="hy_filter_dft")
    z = _matmul(fwd, zin, tm=tm_f, tn=tn, tk=tk_f, out_dtype=F32, name="hy_dft")
    y = _specmul(z, ksp, tr=min(ln, 512))
    rb = row0 // tm_i
    extra = [(x0, pl.BlockSpec((None, tm_i, tn), lambda j, b, i, k: (b, i + rb, j))),
             (zin, pl.BlockSpec((None, tm_i, tn), lambda j, b, i, k: (b, i, j))),
             (dbias, pl.BlockSpec((1, tn), lambda j, b, i, k: (0, j)))]
    return _matmul(inv, y, tm=tm_i, tn=tn, tk=tk_i, out_dtype=BF16, extra=extra, epilogue=_hyena_out_epilogue,
                   out_buf=ybuf, out_row_blk=rb, out_col_blk=SSD_WIDTH // tn, name="hy_idft")


def _mlstm_body(*refs, reverse, final):
    q_ref, k_ref, v_ref, gc_ref, gr_ref = refs[:5]
    if final:
        hf_ref, og_ref, nw_ref, o_ref, ct_ref, n_ref, m_ref = refs[5:]
    else:
        o_ref, ct_ref, n_ref, m_ref = refs[5:]
    nh, dk, dv = ML_HEADS, ML_QK_DIM, ML_V_DIM
    scale = dk ** -0.5

    @pl.when(pl.program_id(1) == 0)
    def _():
        ct_ref[...] = jnp.zeros_like(ct_ref)
        n_ref[...] = jnp.zeros_like(n_ref)
        m_ref[...] = jnp.zeros_like(m_ref)

    gc, gr = gc_ref[...], gr_ref[...]
    li_c, lf_c = gc[:, 0:nh], _log_sigmoid(gc[:, nh:2 * nh])
    li_r, lf_r = gr[0:nh, :], _log_sigmoid(gr[nh:2 * nh, :])
    mask, maskf, mask_tf = _scan_masks(reverse)
    bc_c = _dot(maskf, lf_c, HI)
    bc_r = _dot(lf_r, mask_tf, HI)
    btot_c = jnp.sum(lf_c, axis=0, keepdims=True)
    n_all, m_all = n_ref[...], m_ref[...]
    n_rows, m_rows = [], []

    for h in range(nh):
        qh = q_ref[:, dk * h:dk * (h + 1)]
        kh = k_ref[:, dk * h:dk * (h + 1)]
        vh = v_ref[:, dv * h:dv * (h + 1)]
        m_s = m_all[h:h + 1, 0:1]
        b_c, b_r = bc_c[:, h:h + 1], bc_r[h:h + 1, :]
        i_c, i_r = li_c[:, h:h + 1], li_r[h:h + 1, :]
        btot = btot_c[:, h:h + 1]
        dmat = jnp.where(mask, b_c - b_r + i_r, -jnp.inf)
        inter = b_c + m_s
        m_t = jnp.maximum(inter, jnp.max(dmat, axis=1, keepdims=True))
        s = _dot_nt(qh, kh) * scale * jnp.exp(dmat - m_t)
        dec = jnp.exp(inter - m_t)
        ct = ct_ref[h]
        n_s = n_all[h:h + 1, :]
        num = _dot(s.astype(BF16), vh) + (dec * scale) * _dot(qh, ct.astype(BF16))
        qn = jnp.sum(qh.astype(F32) * n_s, axis=1, keepdims=True) * scale
        den = jnp.sum(s, axis=1, keepdims=True) + dec * qn
        hh = num / jnp.maximum(jnp.abs(den), jnp.exp(-m_t))

        wst_c = btot - b_c + i_c
        wst_r = btot - b_r + i_r
        m_new = jnp.maximum(btot + m_s, jnp.max(wst_r, axis=1, keepdims=True))
        dstate = jnp.exp(btot + m_s - m_new)
        kw = kh.astype(F32) * jnp.exp(wst_c - m_new)
        ct_ref[h] = dstate * ct + _dot(kw.T.astype(BF16), vh)
        n_rows.append(dstate * n_s + jnp.sum(kw, axis=0, keepdims=True))
        m_rows.append(jnp.broadcast_to(m_new, (1, LANES)))

        cols = slice(dv * h, dv * (h + 1))
        if final:
            hs = hh + hf_ref[:, cols]
            hn = hs * lax.rsqrt(jnp.mean(hs * hs, axis=-1, keepdims=True) + EPS) * nw_ref[:, cols]
            o_ref[:, cols] = (hn * _sigmoid(og_ref[:, cols].astype(F32))).astype(o_ref.dtype)
        else:
            o_ref[:, cols] = hh

    n_ref[...] = jnp.concatenate(n_rows, axis=0)
    m_ref[...] = jnp.concatenate(m_rows, axis=0)


def _mlstm(qk, pr, g_c, g_r, *, direction, n_lat, final_args=None, out_width=None):
    nb, t, _ = qk.shape
    q = SCAN_Q
    nch = t // q
    reverse = direction == 1
    cmap = _chunk_map(reverse, nch, n_lat // q)
    d = direction
    in_specs = [
        pl.BlockSpec((None, q, ML_QK_WIDTH), lambda b, k: (b, cmap(k), 0)),
        pl.BlockSpec((None, q, ML_QK_WIDTH), lambda b, k: (b, cmap(k), 1)),
        pl.BlockSpec((None, q, ML_WIDTH), lambda b, k: (b, cmap(k), 2 * ML_QK_WIDTH // ML_WIDTH)),
        pl.BlockSpec((None, None, q, 2 * ML_HEADS), lambda b, k: (b, d, cmap(k), 0)),
        pl.BlockSpec((None, None, 2 * ML_HEADS, q), lambda b, k: (b, d, 0, cmap(k))),
    ]
    args = [qk, qk, pr, g_c, g_r]
    final = final_args is not None
    if final:
        hf, norm_w = final_args
        in_specs += [pl.BlockSpec((None, q, ML_WIDTH), lambda b, k: (b, cmap(k), 0)),
                     pl.BlockSpec((None, q, ML_WIDTH), lambda b, k: (b, cmap(k), 2 * ML_QK_WIDTH // ML_WIDTH + 1)),
                     pl.BlockSpec((1, ML_WIDTH), lambda b, k: (0, 0))]
        args += [hf, pr, norm_w]
        out_shape = jax.ShapeDtypeStruct((nb, t, out_width), BF16)
    else:
        out_shape = jax.ShapeDtypeStruct((nb, t, ML_WIDTH), F32)
    return pl.pallas_call(
        functools.partial(_mlstm_body, reverse=reverse, final=final),
        out_shape=out_shape, grid=(nb, nch), in_specs=in_specs,
        out_specs=pl.BlockSpec((None, q, ML_WIDTH), lambda b, k: (b, cmap(k), 0)),
        scratch_shapes=[pltpu.VMEM((ML_HEADS, ML_QK_DIM, ML_V_DIM), F32), pltpu.VMEM((ML_HEADS, ML_QK_DIM), F32),
                        pltpu.VMEM((ML_HEADS, LANES), F32)],
        name="mlstm_bwd" if reverse else "mlstm_fwd",
        compiler_params=_cp(("parallel", "arbitrary")))(*args)


def _na_window_start(j, rows):
    return jnp.clip(j * NA_QROWS - NA_ROWS // 2, 0, rows - NA_KROWS)


def _na_body(q_ref, k_ref, v_ref, tab_ref, qw_ref, kw_ref, buf_ref, o_ref, kn_ref, *, n_lat, n_ctx):
    del buf_ref
    dh = NA_HEAD_DIM
    scale = dh ** -0.5
    rows = n_lat // GRID_W
    j = pl.program_id(2)
    nq, nk = NA_QROWS * GRID_W, NA_KROWS * GRID_W

    @pl.when(j == 0)
    def _():
        def body(i, carry):
            r = pl.ds(pl.multiple_of(i * LANES, LANES), LANES)
            kk = k_ref[r, :].astype(F32)
            kn = kk * lax.rsqrt(jnp.mean(kk * kk, axis=-1, keepdims=True) + EPS) * kw_ref[...]
            kn_ref[r, :] = kn.astype(kn_ref.dtype)
            return carry

        lax.fori_loop(0, (n_lat + n_ctx) // LANES, body, 0)

    qq = q_ref[...].astype(F32)
    qn = (qq * lax.rsqrt(jnp.mean(qq * qq, axis=-1, keepdims=True) + EPS) * (qw_ref[...] * scale)).astype(BF16)
    start = _na_window_start(j, rows)
    win = pl.ds(pl.multiple_of(start * GRID_W, GRID_W), nk)
    s = _dot_nt(qn, kn_ref[win, :])
    pieces = []
    for qr in range(NA_QROWS):
        d0 = start - (j * NA_QROWS + qr) + NA_ROWS - 1 + NA_QROWS
        pieces.append(s[qr * GRID_W:(qr + 1) * GRID_W] + tab_ref[d0])
    s = jnp.concatenate(pieces, axis=0)
    ctx = pl.ds(n_lat, n_ctx)
    sc = _dot_nt(qn, kn_ref[ctx, :])
    m = jnp.maximum(jnp.max(s, axis=1, keepdims=True), jnp.max(sc, axis=1, keepdims=True))
    p, pc = jnp.exp(s - m), jnp.exp(sc - m)
    denom = jnp.sum(p, axis=1, keepdims=True) + jnp.sum(pc, axis=1, keepdims=True)
    o = _dot(p.astype(BF16), v_ref[win, :]) + _dot(pc.astype(BF16), v_ref[ctx, :])
    o_ref[...] = (o / denom).astype(o_ref.dtype)


def _na_bias_table(rpb, rows):
    nd = 2 * NA_QROWS
    valid = np.zeros((nd, NA_KROWS), bool)
    seen = np.zeros((nd,), bool)
    for j in range(rows // NA_QROWS):
        start = int(np.clip(j * NA_QROWS - NA_ROWS // 2, 0, rows - NA_KROWS))
        for qr in range(NA_QROWS):
            r = j * NA_QROWS + qr
            rs = int(np.clip(r - NA_ROWS // 2, 0, rows - NA_ROWS))
            d = start - r + NA_ROWS - 1 + NA_QROWS
            v = np.array([(rs <= start + i < rs + NA_ROWS) for i in range(NA_KROWS)])
            assert 0 <= d < nd and (not seen[d] or (valid[d] == v).all())
            valid[d], seen[d] = v, True
    col = np.arange(GRID_W)
    cs = np.clip(col - NA_COLS // 2, 0, GRID_W - NA_COLS)
    in_win = (col[None, :] >= cs[:, None]) & (col[None, :] < cs[:, None] + NA_COLS)
    ndr, ndc, w = 2 * NA_ROWS - 1, 2 * NA_COLS - 1, GRID_W
    dr = (np.arange(nd)[:, None] - NA_QROWS) + np.arange(NA_KROWS)[None, :]
    ok = valid[:, None, :, None] & in_win[None, :, None, :] & (dr >= 0)[:, None, :, None] \
        & (dr < ndr)[:, None, :, None]
    ext = jnp.pad(rpb, ((0, 0), (0, 0), (w, w)), mode="edge")
    c0 = w + NA_COLS - 1
    t1 = jnp.stack([ext[:, :, c0 - qc:c0 - qc + w] for qc in range(w)], axis=2)
    t1p = jnp.pad(t1, ((0, 0), (NA_QROWS, nd + NA_KROWS - 1 - NA_QROWS - ndr), (0, 0), (0, 0)))
    wide = jnp.concatenate([t1p[:, m] for m in range(nd + NA_KROWS - 1)], axis=-1)
    vals = jnp.stack([wide[:, :, d * w:(d + NA_KROWS) * w] for d in range(nd)], axis=1)
    return jnp.where(ok.reshape(nd, w, NA_KROWS * w)[None], vals, -jnp.inf).astype(F32)


def _na(pr, tab, q_w, k_w, ybuf, *, n_lat, n_ctx, qcol, out_col):
    nb, t, _ = pr.shape
    dh, nh = NA_HEAD_DIM, NA_HEADS
    nq = NA_QROWS * GRID_W
    qb, ob = qcol // dh, out_col // dh
    nd = tab.shape[1]
    return pl.pallas_call(
        functools.partial(_na_body, n_lat=n_lat, n_ctx=n_ctx),
        out_shape=jax.ShapeDtypeStruct(ybuf.shape, ybuf.dtype), grid=(nb, nh, n_lat // nq),
        in_specs=[pl.BlockSpec((None, nq, dh), lambda b, h, j: (b, j, qb + h)),
                  pl.BlockSpec((None, t, dh), lambda b, h, j: (b, 0, qb + nh + h)),
                  pl.BlockSpec((None, t, dh), lambda b, h, j: (b, 0, qb + 2 * nh + h)),
                  pl.BlockSpec((None, nd, GRID_W, NA_KROWS * GRID_W), lambda b, h, j: (h, 0, 0, 0)),
                  pl.BlockSpec((1, dh), lambda b, h, j: (0, 0)),
                  pl.BlockSpec((1, dh), lambda b, h, j: (0, 0)),
                  pl.BlockSpec(memory_space=pl.ANY)],
        out_specs=pl.BlockSpec((None, nq, dh), lambda b, h, j: (b, j, ob + h)),
        scratch_shapes=[pltpu.VMEM((t, dh), BF16)],
        input_output_aliases={6: 0}, name="na",
        compiler_params=_cp(("parallel", "parallel", "arbitrary")))(pr, pr, pr, tab, q_w, k_w, ybuf)


def _rope_tables(n_lat, n_ctx, width):
    dh = ML_QK_DIM
    nf = dh // 4
    t = jnp.arange(n_lat)
    inv = ROPE_THETA ** (-jnp.arange(nf, dtype=F32) / nf)
    lane = np.arange(dh)
    pos = jnp.where((lane < dh // 2)[None, :], (t // GRID_W)[:, None], (t % GRID_W)[:, None]).astype(F32)
    ang = pos * inv[lane % nf][None, :]
    sign = np.where((lane % (dh // 2)) < nf, -1.0, 1.0).astype(np.float32)
    cos = jnp.concatenate([jnp.cos(ang), jnp.ones((n_ctx, dh), F32)], axis=0)
    sin = jnp.concatenate([jnp.sin(ang) * sign[None, :], jnp.zeros((n_ctx, dh), F32)], axis=0)
    reps = width // dh
    return jnp.tile(cos, (1, reps)), jnp.tile(sin, (1, reps))


def _ada_mod(c, c_ctx, w_all, b_all, layer):
    nb, d = c.shape
    rows = 16
    cv = jnp.concatenate([c, c_ctx[None, :], jnp.zeros((rows - nb - 1, d), F32)], axis=0)
    a = (cv * _sigmoid(cv)).astype(BF16)[None]
    out = _matmul(a, w_all, w_index=layer, tm=rows, tn=1024, tk=d, out_dtype=F32, name="ada_mod")[0]
    out = out[:nb + 1] + b_all[layer][None, :]
    return out.reshape(nb + 1, 6, d).transpose(1, 0, 2)


def _lat_ctx(m, nb):
    ctx = jnp.broadcast_to(m[nb][None, :], (nb, m.shape[1]))
    return jnp.stack([m[:nb], ctx], axis=1)[:, :, None, :]


def _pad_cols(w, n):
    return jnp.pad(w, ((0, 0), (0, n - w.shape[1])))


def _col_row_forms(g, groups):
    nb, t, _ = g.shape
    g5 = g.reshape(nb, t, 2, groups, -1)
    return g5.transpose(0, 2, 3, 1, 4), g5.transpose(0, 2, 3, 4, 1)


ROW_TILES = (1088, 1024, 512, 256)


def _row_tile(rows):
    return next(tm for tm in ROW_TILES if rows % tm == 0)


def _mod_pair(mods, shift, scale, nb):
    return jnp.stack([_lat_ctx(mods[shift], nb), _lat_ctx(mods[scale], nb)], axis=1)


def _out_projection(a, w_all, x, gate, layer, *, rows, n_lat, tk, next_norm, name):
    tm = _row_tile(rows)
    x = _matmul_gated_residual(a, w_all, x, gate, w_index=layer, tm=tm, tn=1024, tk=tk, rows=rows, n_lat=n_lat,
                               name=name)
    if next_norm is None:
        return x, None
    return x, _rmsmod(x, *next_norm, tm=tm, rows=rows, n_lat=n_lat)


def _conv_ffn_block(x, u, mods, w_up_all, conv_w, conv_b, w_down_all, layer, *, rows, n_lat, segs, next_norm):
    nb = x.shape[0]
    up = _matmul(u, w_up_all, w_index=layer, tm=_row_tile(rows), tn=1024, tk=D_MODEL, out_dtype=BF16, name="ffn_up")
    hid = _conv_ffn(up, conv_w, conv_b[None, :], segs=segs, rows=rows)
    return _out_projection(hid, w_down_all, x, _lat_ctx(mods[5], nb), layer, rows=rows, n_lat=n_lat, tk=D_FF // 4,
                           next_norm=next_norm, name="ffn_down")


def _even_layer(x, u, mods, w_in_all, conv_w, conv_b, dt_bias, a_log, d_skip, ssd_norm_w,
                hy_w1, hy_b1, hy_w2, hy_b2, hy_w3, hy_freq, hy_bias, w_out_all, j, *, n_lat, n_ctx, next_norm):
    nb, t, d = x.shape
    segs = ((0, n_lat), (n_lat, n_ctx))
    tm = _row_tile(t)
    n_main = SSD_WIDTH + EV_CONV_CH
    pr = _matmul(u, w_in_all, w_index=j, cols=n_main, tm=tm, tn=1024, tk=d, out_dtype=BF16, name="ev_in")
    w_dt = _pad_cols(w_in_all[j, :, n_main:], LANES).astype(BF16)
    dtp = _matmul(u, w_dt[None], tm=tm, tn=LANES, tk=d, out_dtype=F32, name="ev_in_dt")[:, :, :2 * SSD_HEADS]
    cb = conv_b[None, :]
    xbc = _conv_silu(pr, conv_w, cb, xcol=SSD_WIDTH, wcol=0, width=SSD_XBC, segs=segs)
    x0, zin_l, zin_c = _conv_hyena(pr, conv_w, cb, xcol=SSD_WIDTH + SSD_XBC, wcol=SSD_XBC, segs=segs)

    dt_c, dt_r = _col_row_forms(dtp, SSD_GROUPS)
    db = dt_bias.reshape(2, SSD_GROUPS, 1, SSD_HPG)
    al = a_log.reshape(2, SSD_GROUPS, 1, SSD_HPG)
    scan_args = (xbc, dt_c, dt_r, db, db.transpose(0, 1, 3, 2), al, al.transpose(0, 1, 3, 2))
    yf = _ssd(*scan_args, direction=0, n_lat=n_lat)
    dsk = jnp.repeat(d_skip, SSD_HEAD_DIM)[None, :]
    ybuf = _ssd(*scan_args, direction=1, n_lat=n_lat, final_args=(yf, pr, dsk, ssd_norm_w[None, :]),
                out_width=SSD_WIDTH + HY_WIDTH)

    hyb = hy_bias[None, :]
    filt = functools.partial(_hyena_filters, w1=hy_w1, b1=hy_b1, w2=hy_w2, b2=hy_b2, w3=hy_w3, freq=hy_freq)
    ybuf = _hyena_longconv(zin_l, x0, filt(n_lat), hyb, ybuf, row0=0)
    ybuf = _hyena_longconv(zin_c, x0, filt(n_ctx), hyb, ybuf, row0=n_lat)
    return _out_projection(ybuf, w_out_all, x, _lat_ctx(mods[2], nb), j, rows=t, n_lat=n_lat, tk=2048,
                           next_norm=next_norm, name="ev_out")


def _odd_layer_latent(x, u, mods, w_in, conv_w, conv_b, gate_b, ml_norm_w, q_norm_w, k_norm_w, rpb,
                      w_out_all, j, *, n_lat, n_ctx, next_norm):
    nb, t, d = x.shape
    segs = ((0, n_lat), (n_lat, n_ctx))
    tm = _row_tile(t)
    o3 = 2 * ML_QK_WIDTH + 2 * ML_WIDTH
    o4 = o3 + 4 * ML_HEADS
    wb = w_in.astype(BF16)
    w_main = jnp.concatenate([wb[:, :o3], wb[:, o4:]], axis=1)
    pr = _matmul(u, w_main[None], tm=tm, tn=1024, tk=d, out_dtype=BF16, name="od_in")
    gates = _matmul(u, _pad_cols(wb[:, o3:o4], LANES)[None], tm=tm, tn=LANES, tk=d, out_dtype=F32,
                    name="od_in_gates")[:, :, :4 * ML_HEADS] + gate_b.reshape(-1)
    cos, sin = _rope_tables(n_lat, n_ctx, 256)
    qk = _conv_rope(pr, conv_w, conv_b[None, :], cos, sin, width=2 * ML_QK_WIDTH, segs=segs)
    g5 = gates.reshape(nb, t, 2, 2 * ML_HEADS)
    g_c, g_r = g5.transpose(0, 2, 1, 3), g5.transpose(0, 2, 3, 1)
    hf = _mlstm(qk, pr, g_c, g_r, direction=0, n_lat=n_lat)
    ybuf = _mlstm(qk, pr, g_c, g_r, direction=1, n_lat=n_lat, final_args=(hf, ml_norm_w[None, :]),
                  out_width=ML_WIDTH + NA_WIDTH)
    tab = _na_bias_table(rpb, n_lat // GRID_W)
    ybuf = _na(pr, tab, q_norm_w[None, :], k_norm_w[None, :], ybuf, n_lat=n_lat, n_ctx=n_ctx, qcol=o3,
               out_col=ML_WIDTH)
    return _out_projection(ybuf, w_out_all, x, _lat_ctx(mods[2], nb), j, rows=n_lat, n_lat=n_lat, tk=2048,
                           next_norm=next_norm, name="od_out")


def kernel(x, c, ctx, c_ctx, ada_w, ada_b, norm_w, ev_w_in, ev_conv_w, ev_conv_b, ssd_dt_bias, ssd_a_log, ssd_d,
           ssd_norm_w, hy_w1, hy_b1, hy_w2, hy_b2, hy_w3, hy_freq, hy_bias, ev_w_out, od_w_in, ml_conv_w, ml_conv_b,
           ml_gate_b, ml_norm_w, na_q_norm_w, na_k_norm_w, na_rpb, od_w_out, ffn_w_up, ffn_conv_w, ffn_conv_b,
           ffn_w_down):
    depth = ada_w.shape[0]
    assert depth == 2, "layer schedule below is written for one even and one odd layer"
    n_lat, n_ctx = x.shape[1], ctx.shape[1]
    t = n_lat + n_ctx
    both = ((0, n_lat), (n_lat, n_ctx))
    xs = jnp.concatenate([x, ctx], axis=1)

    nb = x.shape[0]
    w_down = ffn_w_down.astype(BF16)
    mods0 = _ada_mod(c, c_ctx, ada_w, ada_b, 0)
    mods1 = _ada_mod(c, c_ctx, ada_w, ada_b, 1)
    norm = lambda layer, sub, mods: (norm_w[layer, sub][None, :], _mod_pair(mods, 3 * sub, 3 * sub + 1, nb))

    w0, mod0 = norm(0, 0, mods0)
    u = _rmsmod(xs, w0, mod0, tm=_row_tile(t), rows=t, n_lat=n_lat)
    xs, u = _even_layer(xs, u, mods0, ev_w_in, ev_conv_w[0], ev_conv_b[0], ssd_dt_bias[0], ssd_a_log[0],
                        ssd_d[0], ssd_norm_w[0], hy_w1[0], hy_b1[0], hy_w2[0], hy_b2[0], hy_w3[0], hy_freq[0],
                        hy_bias[0], ev_w_out.astype(BF16), 0, n_lat=n_lat, n_ctx=n_ctx, next_norm=norm(0, 1, mods0))
    xs, u = _conv_ffn_block(xs, u, mods0, ffn_w_up, ffn_conv_w[0], ffn_conv_b[0], w_down, 0,
                            rows=t, n_lat=n_lat, segs=both, next_norm=norm(1, 0, mods1))
    xl, u = _odd_layer_latent(xs, u, mods1, od_w_in[0], ml_conv_w[0], ml_conv_b[0], ml_gate_b[0], ml_norm_w[0],
                              na_q_norm_w[0], na_k_norm_w[0], na_rpb[0], od_w_out.astype(BF16), 0,
                              n_lat=n_lat, n_ctx=n_ctx, next_norm=norm(1, 1, mods1))
    out, _ = _conv_ffn_block(xl, u, mods1, ffn_w_up, ffn_conv_w[1], ffn_conv_b[1], w_down, 1,
                             rows=n_lat, n_lat=n_lat, segs=((0, n_lat),), next_norm=None)
    return out
```

```python
import functools
import math

import numpy as np
import jax
import jax.numpy as jnp
from jax import lax
from jax.experimental import pallas as pl
from jax.experimental.pallas import tpu as pltpu

F32 = jnp.float32
BF16 = jnp.bfloat16
HI = lax.Precision.HIGHEST

D_MODEL = 2048
SEQ = 4096
CTX_LEN = 256
GRID_W = 64
EPS = 1e-6
CONV_W = 3
SSD_WIDTH = D_MODEL
SSD_HEAD_DIM = 64
SSD_HEADS = SSD_WIDTH // SSD_HEAD_DIM
SSD_STATE = 128
SSD_GROUPS = 4
SSD_HPG = SSD_HEADS // SSD_GROUPS
SSD_GW = SSD_WIDTH // SSD_GROUPS
SSD_XBC = SSD_WIDTH + 2 * SSD_GROUPS * SSD_STATE
HY_WIDTH = D_MODEL
HY_EMB = 33
HY_BANDS = (HY_EMB - 1) // 2
HY_ORDER = 64
HY_FAST = 0.3
HY_SLOW = 1.5
HY_TARGET = 1e-2
ML_HEADS = 8
ML_WIDTH = D_MODEL
ML_V_DIM = ML_WIDTH // ML_HEADS
ML_QK_DIM = ML_V_DIM // 2
ML_QK_WIDTH = ML_HEADS * ML_QK_DIM
ROPE_THETA = 10000.0
NA_WIDTH = D_MODEL
NA_HEAD_DIM = 128
NA_HEADS = NA_WIDTH // NA_HEAD_DIM
NA_ROWS = 8
NA_COLS = 16
NA_QROWS = 8
NA_KROWS = 16
D_FF = 256 * ((8 * D_MODEL // 3 + 255) // 256)
EV_CONV_CH = SSD_XBC + 3 * HY_WIDTH
SCAN_Q = 128
LANES = 128
SUBLANES = 8
VMEM_LIMIT = 48 * 1024 * 1024


def _cp(sem, vmem=VMEM_LIMIT):
    return pltpu.CompilerParams(dimension_semantics=sem, vmem_limit_bytes=vmem)


def _sigmoid(x):
    return 1.0 / (1.0 + jnp.exp(-x))


def _silu(x):
    return x * _sigmoid(x)


def _softplus(x):
    return jnp.maximum(x, 0.0) + jnp.log(1.0 + jnp.exp(-jnp.abs(x)))


def _log_sigmoid(x):
    return -_softplus(-x)


def _dot(a, b, precision=None):
    return jnp.dot(a, b, preferred_element_type=F32, precision=precision)


def _dot_nt(a, b):
    return lax.dot_general(a, b, (((1,), (1,)), ((), ())), preferred_element_type=F32)


def _split_bf16(x, terms):
    out = []
    for _ in range(terms):
        part = x.astype(BF16)
        out.append(part)
        x = x - part.astype(F32)
    return out


def _dot_sel(x, sel, terms):
    selb = sel.astype(BF16)
    return sum(_dot(part, selb) for part in _split_bf16(x, terms))


def _sel_dot(sel, x, terms):
    selb = sel.astype(BF16)
    return sum(_dot(selb, part) for part in _split_bf16(x, terms))


def _mm_body(*refs, nk, n_extra, epilogue, has_buf, cache_w, n_out):
    a_ref, w_ref = refs[0], refs[1]
    extra = refs[2:2 + n_extra]
    pos = 2 + n_extra + (1 if has_buf else 0)
    o_refs = refs[pos:pos + n_out]
    pos += n_out - 1
    row_blk = pl.program_id(2)
    if cache_w:
        wb_ref = refs[pos + 1]

        @pl.when((pl.program_id(1) == 0) & (row_blk == 0))
        def _():
            wb_ref[...] = w_ref[...].astype(BF16)

        w_val = wb_ref[...]
    else:
        w_val = w_ref[...].astype(BF16)
    part = _dot(a_ref[...].astype(BF16), w_val)

    def finish(acc):
        vals = acc if epilogue is None else epilogue(acc, row_blk, *extra)
        for o_ref, val in zip(o_refs, vals if n_out > 1 else (vals,)):
            o_ref[...] = val.astype(o_ref.dtype)

    if nk == 1:
        finish(part)
    else:
        acc_ref = refs[pos + 1]
        k = pl.program_id(3)

        @pl.when(k == 0)
        def _():
            acc_ref[...] = part

        @pl.when(k > 0)
        def _():
            acc_ref[...] += part

        @pl.when(k == nk - 1)
        def _():
            finish(acc_ref[...])


def _matmul(a, w, *, tm, tn, tk, out_dtype, rows=None, cols=None, w_index=None, extra=(), epilogue=None,
            out_buf=None, out_row_blk=0, out_col_blk=0, n_out=1, name="mm"):
    ba, ta, kdim = a.shape
    bw, kdim2, n = w.shape
    assert kdim == kdim2
    nb = ba if w_index is not None else max(ba, bw)
    rows = ta if rows is None else rows
    n = n if cols is None else cols
    assert rows % tm == 0 and n % tn == 0 and kdim % tk == 0
    ni, nj, nk = rows // tm, n // tn, kdim // tk
    a_map = (lambda j, b, i, k: (b, i, k)) if ba > 1 else (lambda j, b, i, k: (0, i, k))
    if w_index is not None:
        w_map = lambda j, b, i, k: (w_index, k, j)
    else:
        w_map = (lambda j, b, i, k: (b, k, j)) if bw > 1 else (lambda j, b, i, k: (0, k, j))
    cache_w = w.dtype != BF16 and nk == 1 and (w_index is not None or bw == 1)
    in_specs = [pl.BlockSpec((None, tm, tk), a_map), pl.BlockSpec((None, tk, tn), w_map)]
    args = [a, w]
    for arr, spec in extra:
        args.append(arr)
        in_specs.append(spec)
    aliases = {}
    if out_buf is not None:
        aliases = {len(args): 0}
        args.append(out_buf)
        in_specs.append(pl.BlockSpec(memory_space=pl.ANY))
        out_shape = jax.ShapeDtypeStruct(out_buf.shape, out_buf.dtype)
    else:
        out_shape = jax.ShapeDtypeStruct((nb, rows, n), out_dtype)
    out_spec = pl.BlockSpec((None, tm, tn), lambda j, b, i, k: (b, i + out_row_blk, j + out_col_blk))
    if n_out > 1:
        out_shape, out_spec = (out_shape,) * n_out, (out_spec,) * n_out
    scratch = [pltpu.VMEM((tm, tn), F32)] if nk > 1 else []
    if cache_w:
        scratch = [pltpu.VMEM((tk, tn), BF16)]
    body = functools.partial(_mm_body, nk=nk, n_extra=len(extra), epilogue=epilogue,
                             has_buf=out_buf is not None, cache_w=cache_w, n_out=n_out)
    inner = "arbitrary" if cache_w else "parallel"
    return pl.pallas_call(
        body, out_shape=out_shape, grid=(nj, nb, ni, nk), in_specs=in_specs, out_specs=out_spec,
        scratch_shapes=scratch, input_output_aliases=aliases, name=name,
        compiler_params=_cp(("parallel", inner, inner, "arbitrary")))(*args)


def _row_is_lat(row_blk, tm, n_lat):
    rows = row_blk * tm + lax.broadcasted_iota(jnp.int32, (tm, 1), 0)
    return rows < n_lat


def _gated_residual_epilogue(acc, row_blk, res_ref, gate_ref, *, tm, n_lat):
    gate = jnp.where(_row_is_lat(row_blk, tm, n_lat), gate_ref[0], gate_ref[1])
    return res_ref[...] + gate * acc


def _modulated_rmsnorm(x, is_lat, w_ref, mod_ref):
    y = x * lax.rsqrt(jnp.mean(x * x, axis=-1, keepdims=True) + EPS) * w_ref[...]
    shift = jnp.where(is_lat, mod_ref[0, 0], mod_ref[0, 1])
    scale = jnp.where(is_lat, mod_ref[1, 0], mod_ref[1, 1])
    return y * (1.0 + scale) + shift


def _matmul_gated_residual(a, w, res, gate, *, w_index, tm, tn, tk, rows, n_lat, name):
    extra = [(res, pl.BlockSpec((None, tm, tn), lambda j, b, i, k: (b, i, j))),
             (gate, pl.BlockSpec((None, 2, 1, tn), lambda j, b, i, k: (b, 0, 0, j)))]
    epi = functools.partial(_gated_residual_epilogue, tm=tm, n_lat=n_lat)
    return _matmul(a, w, w_index=w_index, tm=tm, tn=tn, tk=tk, out_dtype=F32, rows=rows, extra=extra,
                   epilogue=epi, name=name)


def _rmsmod_body(x_ref, w_ref, mod_ref, o_ref, *, tm, n_lat):
    is_lat = _row_is_lat(pl.program_id(2), tm, n_lat)
    o_ref[...] = _modulated_rmsnorm(x_ref[...], is_lat, w_ref, mod_ref).astype(o_ref.dtype)


def _rmsmod(x, w, mod, *, tm, rows, n_lat):
    nb, _, d = x.shape
    tm = tm // 4
    assert rows % tm == 0 and tm % 16 == 0
    body = functools.partial(_rmsmod_body, tm=tm, n_lat=n_lat)
    return pl.pallas_call(
        body, out_shape=jax.ShapeDtypeStruct((nb, rows, d), BF16), grid=(1, nb, rows // tm),
        in_specs=[pl.BlockSpec((None, tm, d), lambda j, b, i: (b, i, 0)),
                  pl.BlockSpec((1, d), lambda j, b, i: (0, 0)),
                  pl.BlockSpec((None, 2, 2, 1, d), lambda j, b, i: (b, 0, 0, 0, 0))],
        out_specs=pl.BlockSpec((None, tm, d), lambda j, b, i: (b, i, 0)),
        name="rmsmod", compiler_params=_cp(("parallel", "parallel", "parallel")))(x, w, mod)


CONV_CH = 128
CONV_HALO = 16


def _shift_matrices():
    r = lax.broadcasted_iota(jnp.int32, (CONV_CH, CONV_CH + 2 * CONV_HALO), 0)
    c = lax.broadcasted_iota(jnp.int32, (CONV_CH, CONV_CH + 2 * CONV_HALO), 1)
    return (c == r + CONV_HALO - 1).astype(BF16), (c == r + CONV_HALO + 1).astype(BF16)


def _conv_at(x_ref, w_ref, b_ref, shifts, r0, n, i):
    t0 = pl.multiple_of(i * CONV_CH, CONV_CH)
    cur = x_ref[pl.ds(r0 + t0, CONV_CH), :]
    lo = pl.multiple_of(jnp.maximum(t0 - CONV_HALO, 0), CONV_HALO)
    hi = pl.multiple_of(jnp.minimum(t0 + CONV_CH, n - CONV_HALO), CONV_HALO)
    zero = jnp.zeros((CONV_HALO, cur.shape[1]), cur.dtype)
    tail = jnp.where(i > 0, x_ref[pl.ds(r0 + lo, CONV_HALO), :], zero)
    head = jnp.where(i < n // CONV_CH - 1, x_ref[pl.ds(r0 + hi, CONV_HALO), :], zero)
    win = jnp.concatenate([tail, cur, head], axis=0)
    return (w_ref[0:1, :] * _dot(shifts[0], win) + w_ref[1:2, :] * cur.astype(F32)
            + w_ref[2:3, :] * _dot(shifts[1], win) + b_ref[...])


def _conv_loop(segs, fn):
    for si, (r0, n) in enumerate(segs):
        def body(i, carry, r0=r0, n=n, si=si):
            fn(si, r0, n, i)
            return carry

        lax.fori_loop(0, n // CONV_CH, body, 0, unroll=min(4, n // CONV_CH))


def _conv_silu_body(x_ref, w_ref, b_ref, o_ref, *, segs):
    shifts = _shift_matrices()

    def fn(si, r0, n, i):
        y = _silu(_conv_at(x_ref, w_ref, b_ref, shifts, r0, n, i))
        o_ref[pl.ds(r0 + pl.multiple_of(i * CONV_CH, CONV_CH), CONV_CH), :] = y.astype(o_ref.dtype)

    _conv_loop(segs, fn)


def _conv_rope_body(x_ref, w_ref, b_ref, cos_ref, sin_ref, o_ref, *, segs, tc):
    shifts = _shift_matrices()
    lane = lax.broadcasted_iota(jnp.int32, (CONV_CH, tc), 1)
    first = (lane % (ML_QK_DIM // 2)) < (ML_QK_DIM // 4)
    nf = ML_QK_DIM // 4

    def fn(si, r0, n, i):
        y = _silu(_conv_at(x_ref, w_ref, b_ref, shifts, r0, n, i))
        partner = jnp.where(first, pltpu.roll(y, tc - nf, 1), pltpu.roll(y, nf, 1))
        rows = pl.ds(r0 + pl.multiple_of(i * CONV_CH, CONV_CH), CONV_CH)
        o_ref[rows, :] = (y * cos_ref[rows, :] + partner * sin_ref[rows, :]).astype(o_ref.dtype)

    _conv_loop(segs, fn)


def _conv_ffn_body(a_ref, x_ref, w_ref, b_ref, o_ref, *, segs):
    shifts = _shift_matrices()

    def fn(si, r0, n, i):
        rows = pl.ds(r0 + pl.multiple_of(i * CONV_CH, CONV_CH), CONV_CH)
        g = _silu(_conv_at(x_ref, w_ref, b_ref, shifts, r0, n, i))
        o_ref[rows, :] = (a_ref[rows, :].astype(F32) * g).astype(o_ref.dtype)

    _conv_loop(segs, fn)


def _conv_hyena_body(x0_ref, x1_ref, v_ref, w0_ref, b0_ref, w1_ref, b1_ref, w2_ref, b2_ref,
                     ox0_ref, ozl_ref, ozc_ref, ozle_ref, ozlo_ref, ozce_ref, ozco_ref, *, segs):
    shifts = _shift_matrices()
    r = lax.broadcasted_iota(jnp.int32, (CONV_CH, CONV_CH), 1)
    c = lax.broadcasted_iota(jnp.int32, (CONV_CH, CONV_CH), 0)
    half = CONV_CH // 2
    split = (c == r // 2 + half * (r % 2)).astype(BF16)
    outs = ((ozl_ref, ozle_ref, ozlo_ref), (ozc_ref, ozce_ref, ozco_ref))

    def fn(si, r0, n, i):
        t0 = pl.multiple_of(i * CONV_CH, CONV_CH)
        x0 = _conv_at(x0_ref, w0_ref, b0_ref, shifts, r0, n, i)
        ox0_ref[pl.ds(r0 + t0, CONV_CH), :] = x0.astype(ox0_ref.dtype)
        zin = _conv_at(x1_ref, w1_ref, b1_ref, shifts, r0, n, i) * _conv_at(v_ref, w2_ref, b2_ref, shifts, r0, n, i)
        zb = zin.astype(BF16)
        nat_ref, even_ref, odd_ref = outs[si]
        nat_ref[pl.ds(t0, CONV_CH), :] = zb
        parity = _dot(split, zb)
        hrows = pl.ds(pl.multiple_of(i * half, half), half)
        even_ref[hrows, :] = parity[0:half].astype(BF16)
        odd_ref[hrows, :] = parity[half:].astype(BF16)

    _conv_loop(segs, fn)


def _conv_specs(t_in, tc, xcol_blk, wcol_blk):
    return [pl.BlockSpec((None, t_in, tc), lambda b, j: (b, 0, j + xcol_blk)),
            pl.BlockSpec((CONV_W, tc), lambda b, j: (0, j + wcol_blk)),
            pl.BlockSpec((1, tc), lambda b, j: (0, j + wcol_blk))]


def _conv_silu(x, w, b, *, xcol, wcol, width, segs, tc=256):
    nb, t_in, _ = x.shape
    return pl.pallas_call(
        functools.partial(_conv_silu_body, segs=segs),
        out_shape=jax.ShapeDtypeStruct((nb, t_in, width), BF16), grid=(nb, width // tc),
        in_specs=_conv_specs(t_in, tc, xcol // tc, wcol // tc),
        out_specs=pl.BlockSpec((None, t_in, tc), lambda b, j: (b, 0, j)),
        name="conv_silu", compiler_params=_cp(("parallel", "parallel")))(x, w, b)


def _conv_rope(x, w, b, cos, sin, *, width, segs, tc=256):
    nb, t_in, _ = x.shape
    tab = pl.BlockSpec((t_in, tc), lambda b, j: (0, 0))
    return pl.pallas_call(
        functools.partial(_conv_rope_body, segs=segs, tc=tc),
        out_shape=jax.ShapeDtypeStruct((nb, t_in, width), BF16), grid=(nb, width // tc),
        in_specs=_conv_specs(t_in, tc, 0, 0) + [tab, tab],
        out_specs=pl.BlockSpec((None, t_in, tc), lambda b, j: (b, 0, j)),
        name="conv_rope", compiler_params=_cp(("parallel", "parallel")))(x, w, b, cos, sin)


def _conv_ffn(up, w, b, *, segs, rows, tc=256):
    nb, t_in, two_ff = up.shape
    ff = two_ff // 2
    return pl.pallas_call(
        functools.partial(_conv_ffn_body, segs=segs),
        out_shape=jax.ShapeDtypeStruct((nb, rows, ff), BF16), grid=(nb, ff // tc),
        in_specs=[pl.BlockSpec((None, rows, tc), lambda b, j: (b, 0, j))] + _conv_specs(rows, tc, ff // tc, 0),
        out_specs=pl.BlockSpec((None, rows, tc), lambda b, j: (b, 0, j)),
        name="conv_ffn", compiler_params=_cp(("parallel", "parallel")))(up, up, w, b)


def _conv_hyena(pr, w, b, *, xcol, wcol, segs, tc=256):
    nb, t_in, _ = pr.shape
    (_, n_lat), (_, n_ctx) = segs
    wd = HY_WIDTH
    out_rows = (t_in, n_lat, n_ctx, n_lat // 2, n_lat // 2, n_ctx // 2, n_ctx // 2)
    in_specs = [pl.BlockSpec((None, t_in, tc), functools.partial(lambda b, j, o: (b, 0, j + o), o=(xcol + m * wd) // tc))
                for m in range(3)]
    wargs = []
    for m in range(3):
        off = (wcol + m * wd) // tc
        in_specs.append(pl.BlockSpec((CONV_W, tc), functools.partial(lambda b, j, o: (0, j + o), o=off)))
        in_specs.append(pl.BlockSpec((1, tc), functools.partial(lambda b, j, o: (0, j + o), o=off)))
        wargs += [w, b]
    return pl.pallas_call(
        functools.partial(_conv_hyena_body, segs=segs),
        out_shape=tuple(jax.ShapeDtypeStruct((nb, rows, wd), BF16) for rows in out_rows),
        grid=(nb, wd // tc), in_specs=in_specs,
        out_specs=tuple(pl.BlockSpec((None, rows, tc), lambda b, j: (b, 0, j)) for rows in out_rows),
        name="conv_hyena", compiler_params=_cp(("parallel", "parallel")))(pr, pr, pr, *wargs)


def _scan_masks(reverse):
    q = SCAN_Q
    ii = lax.broadcasted_iota(jnp.int32, (q, q), 0)
    jj = lax.broadcasted_iota(jnp.int32, (q, q), 1)
    mask = (jj >= ii) if reverse else (jj <= ii)
    mask_t = (jj <= ii) if reverse else (jj >= ii)
    return mask, mask.astype(F32), mask_t.astype(F32)


def _chunk_map(reverse, n_chunks, n_lat_chunks):
    if reverse:
        return lambda k: n_chunks - 1 - k
    return lambda k: (k + n_lat_chunks) % n_chunks


def _ssd_body(*refs, reverse, final):
    (xs_ref, b_ref, c_ref, dtc_ref, dtr_ref, dbc_ref, dbr_ref, alc_ref, alr_ref) = refs[:9]
    if final:
        yf_ref, z_ref, dsk_ref, nw_ref, o_ref, h_ref = refs[9:]
    else:
        o_ref, h_ref = refs[9:]
    q, gw, hd = SCAN_Q, SSD_GW, SSD_HEAD_DIM

    @pl.when(pl.program_id(2) == 0)
    def _():
        h_ref[...] = jnp.zeros_like(h_ref)

    xs = xs_ref[...].astype(F32)
    bm = b_ref[...]
    cm = c_ref[...]
    dt_c = _softplus(dtc_ref[...] + dbc_ref[...])
    dt_r = _softplus(dtr_ref[...] + dbr_ref[...])
    da_c = dt_c * (-jnp.exp(alc_ref[...]))
    da_r = dt_r * (-jnp.exp(alr_ref[...]))
    mask, maskf, mask_tf = _scan_masks(reverse)
    acum_c = _sel_dot(maskf, da_c, 3)
    acum_r = _dot_sel(da_r, mask_tf, 3)
    atot = jnp.sum(da_c, axis=0, keepdims=True)

    er = lax.broadcasted_iota(jnp.int32, (SSD_HPG, gw), 0)
    ec = lax.broadcasted_iota(jnp.int32, (SSD_HPG, gw), 1)
    expand = (ec // hd == er).astype(F32)
    stack = jnp.concatenate([dt_c, jnp.exp(atot - acum_c), jnp.exp(acum_c),
                             jnp.broadcast_to(jnp.exp(atot), (SUBLANES, SSD_HPG))], axis=0)
    wide = _dot_sel(stack, expand, 2)
    dt_e, ws_e, eac_e, eat_e = wide[0:q], wide[q:2 * q], wide[2 * q:3 * q], wide[3 * q:3 * q + 1]

    xq = xs * dt_e
    h_prev = h_ref[...]
    cb = _dot_nt(cm, bm)
    y = eac_e * _dot(cm, h_prev.astype(BF16))
    lane = lax.broadcasted_iota(jnp.int32, (q, 2 * hd), 1)
    pieces = []
    for p in range(SSD_HPG // 2):
        ms = []
        for r in (2 * p, 2 * p + 1):
            seg = acum_c[:, r:r + 1] - acum_r[r:r + 1, :]
            ms.append((cb * jnp.where(mask, jnp.exp(seg), 0.0)).astype(BF16))
        xp = xq[:, 2 * hd * p:2 * hd * (p + 1)]
        rhs = jnp.concatenate([jnp.where(lane < hd, xp, 0.0), jnp.where(lane >= hd, xp, 0.0)], axis=0)
        pieces.append(_dot(jnp.concatenate(ms, axis=1), rhs.astype(BF16)))
    y = y + jnp.concatenate(pieces, axis=1)

    bt = bm.astype(F32).T.astype(BF16)
    h_ref[...] = eat_e * h_prev + _dot(bt, (xq * ws_e).astype(BF16))

    if final:
        ytot = y + yf_ref[...] + xs * dsk_ref[...]
        g = ytot * _silu(z_ref[...].astype(F32))
        o = g * lax.rsqrt(jnp.mean(g * g, axis=-1, keepdims=True) + EPS) * nw_ref[...]
        o_ref[...] = o.astype(o_ref.dtype)
    else:
        o_ref[...] = y


def _ssd(xbc, dt_c, dt_r, db_c, db_r, al_c, al_r, *, direction, n_lat, final_args=None, out_width=None):
    nb, t, _ = xbc.shape
    q, gw, hpg = SCAN_Q, SSD_GW, SSD_HPG
    nch = t // q
    reverse = direction == 1
    cmap = _chunk_map(reverse, nch, n_lat // q)
    nblk = SSD_STATE // LANES
    b0 = SSD_WIDTH // SSD_STATE
    c0 = b0 + SSD_GROUPS * nblk
    d = direction
    in_specs = [
        pl.BlockSpec((None, q, gw), lambda b, g, k: (b, cmap(k), g)),
        pl.BlockSpec((None, q, SSD_STATE), lambda b, g, k: (b, cmap(k), b0 + g)),
        pl.BlockSpec((None, q, SSD_STATE), lambda b, g, k: (b, cmap(k), c0 + g)),
        pl.BlockSpec((None, None, None, q, hpg), lambda b, g, k: (b, d, g, cmap(k), 0)),
        pl.BlockSpec((None, None, None, hpg, q), lambda b, g, k: (b, d, g, 0, cmap(k))),
        pl.BlockSpec((None, None, 1, hpg), lambda b, g, k: (d, g, 0, 0)),
        pl.BlockSpec((None, None, hpg, 1), lambda b, g, k: (d, g, 0, 0)),
        pl.BlockSpec((None, None, 1, hpg), lambda b, g, k: (d, g, 0, 0)),
        pl.BlockSpec((None, None, hpg, 1), lambda b, g, k: (d, g, 0, 0)),
    ]
    args = [xbc, xbc, xbc, dt_c, dt_r, db_c, db_r, al_c, al_r]
    final = final_args is not None
    if final:
        yf, pr, dskip, norm_w = final_args
        in_specs += [pl.BlockSpec((None, q, gw), lambda b, g, k: (b, cmap(k), g)),
                     pl.BlockSpec((None, q, gw), lambda b, g, k: (b, cmap(k), g)),
                     pl.BlockSpec((1, gw), lambda b, g, k: (0, g)),
                     pl.BlockSpec((1, gw), lambda b, g, k: (0, g))]
        args += [yf, pr, dskip, norm_w]
        out_shape = jax.ShapeDtypeStruct((nb, t, out_width), BF16)
    else:
        out_shape = jax.ShapeDtypeStruct((nb, t, SSD_WIDTH), F32)
    return pl.pallas_call(
        functools.partial(_ssd_body, reverse=reverse, final=final),
        out_shape=out_shape, grid=(nb, SSD_GROUPS, nch), in_specs=in_specs,
        out_specs=pl.BlockSpec((None, q, gw), lambda b, g, k: (b, cmap(k), g)),
        scratch_shapes=[pltpu.VMEM((SSD_STATE, gw), F32)],
        name="ssd_bwd" if reverse else "ssd_fwd",
        compiler_params=_cp(("parallel", "parallel", "arbitrary")))(*args)


def _hyfilt_body(f_ref, w1_ref, b1_ref, w2_ref, b2_ref, w3_ref, fr_ref, dl_ref, o_ref, *, tl):
    f = f_ref[...]
    h = jnp.sin(fr_ref[0:1, :] * (_dot(f, w1_ref[...], HI) + b1_ref[...]))
    h = jnp.sin(fr_ref[1:2, :] * (_dot(h, w2_ref[...], HI) + b2_ref[...]))
    h = _dot(h, w3_ref[...], HI)
    dec = jnp.exp(-f[:, 0:1] * dl_ref[...])
    row = pl.program_id(0) * tl + lax.broadcasted_iota(jnp.int32, (tl, 1), 0)
    wd = HY_WIDTH
    o_ref[:, 0:wd] = (h[:, 0:wd] * dec).astype(o_ref.dtype)
    o_ref[:, wd:2 * wd] = jnp.where(row == 0, 0.0, h[:, wd:2 * wd] * dec).astype(o_ref.dtype)


def _hyena_filters(length, w1, b1, w2, b2, w3, freq, *, tl=256):
    p = LANES
    t = jnp.linspace(0.0, 1.0, length, dtype=F32)[:, None]
    w = 2.0 * math.pi * jnp.arange(length, dtype=F32)[:, None] / length
    f = jnp.linspace(1e-4, HY_BANDS - 1, HY_BANDS, dtype=F32)[None, :]
    feats = jnp.concatenate([t, jnp.cos(f * w), -jnp.sin(f * w)], axis=-1)
    feats = jnp.concatenate([feats[0::2], feats[1::2]], axis=0)
    feats = jnp.pad(feats, ((0, 0), (0, p - HY_EMB)))
    po = p - HY_ORDER
    w1p = jnp.pad(w1, ((0, p - HY_EMB), (0, po)))
    w2p = jnp.pad(w2, ((0, po), (0, po)))
    w3p = jnp.pad(w3, ((0, po), (0, 0)))
    b1p = jnp.pad(b1[None, :], ((0, 0), (0, po)))
    b2p = jnp.pad(b2[None, :], ((0, 0), (0, po)))
    frp = jnp.pad(freq, ((0, 0), (0, po)))
    deltas = jnp.abs(jnp.linspace(math.log(HY_TARGET) / HY_FAST, math.log(HY_TARGET) / HY_SLOW, HY_WIDTH,
                                  dtype=F32))[None, :]
    full = lambda shape: pl.BlockSpec(shape, lambda i: (0,) * len(shape))
    out = pl.pallas_call(
        functools.partial(_hyfilt_body, tl=tl),
        out_shape=jax.ShapeDtypeStruct((length, 2 * HY_WIDTH), BF16), grid=(length // tl,),
        in_specs=[pl.BlockSpec((tl, p), lambda i: (i, 0)), full((p, p)), full((1, p)), full((p, p)), full((1, p)),
                  full((p, 2 * HY_WIDTH)), full((2, p)), full((1, HY_WIDTH))],
        out_specs=pl.BlockSpec((tl, 2 * HY_WIDTH), lambda i: (i, 0)),
        name="hyena_filters", compiler_params=_cp(("parallel",)))(feats, w1p, b1p, w2p, b2p, w3p, frp, deltas)
    return out[None]


def _dft_matrices(length):
    n = 2 * length
    k = jnp.arange(length, dtype=jnp.int32)[:, None]
    t = jnp.arange(length, dtype=jnp.int32)[None, :]
    step = 1 << (int(math.log2(length)) // 2)
    part = jnp.arange(length // step, dtype=jnp.int32)[:, None]
    ang_hi = ((part * step * t) % n).astype(F32) * (2.0 * math.pi / n)
    ang_lo = ((jnp.arange(step, dtype=jnp.int32)[:, None] * t) % n).astype(F32) * (2.0 * math.pi / n)
    ch, sh = jnp.cos(ang_hi)[:, None, :], jnp.sin(ang_hi)[:, None, :]
    cl, sl = jnp.cos(ang_lo)[None, :, :], jnp.sin(ang_lo)[None, :, :]
    c = (ch * cl - sh * sl).reshape(length, length)
    s = (sh * cl + ch * sl).reshape(length, length)
    alt_t = jnp.where(t % 2 == 0, 1.0, -1.0).astype(F32)
    fwd = jnp.concatenate([c, jnp.where(k == 0, alt_t, -s)], axis=0)
    alt_r = jnp.where(k % 2 == 0, 1.0, -1.0).astype(F32)
    inv_re = jnp.where(t == 0, 1.0 / n, (2.0 / n) * c)
    inv_im = jnp.where(t == 0, alt_r / n, (-2.0 / n) * s)
    inv = jnp.concatenate([inv_re, inv_im], axis=1)
    return fwd.astype(BF16)[None], inv.astype(BF16)[None]


def _tile_interleave(x, axis, hm):
    shp = x.shape
    n = shp[axis] // 2
    x = x.reshape(shp[:axis] + (2, n // hm, hm) + shp[axis + 1:])
    return jnp.swapaxes(x, axis, axis + 1).reshape(shp)


def _twiddle_table(length, hm):
    k = np.arange(length // 2, dtype=np.float64)[:, None] * (np.pi / length)
    tab = np.concatenate([np.cos(k), np.sin(k)], axis=0)
    tab = tab.reshape(2, -1, hm, 1).swapaxes(0, 1).reshape(length, 1)
    return jnp.asarray(np.broadcast_to(tab, (length, LANES)).astype(np.float32))


def _lanes(x, width):
    return jnp.tile(x, (1, width // LANES))


def _parity_banks(e, ot, tw, hm):
    er, ei, otr, oti = e[0:hm], e[hm:], ot[0:hm], ot[hm:]
    c, s = _lanes(tw[0:hm], e.shape[1]), _lanes(tw[hm:], e.shape[1])
    orr = otr * c + oti * s
    oi = oti * c - otr * s
    return (er + orr, ei + oi, er - orr, ei - oi), (c, s)


def _is_bin0(row_blk, hm):
    return (row_blk == 0) & (lax.broadcasted_iota(jnp.int32, (hm, 1), 0) == 0)


def _hy_kbank_body(ef_ref, eb_ref, of_ref, ob_ref, tw_ref, ka_ref, kb_ref, *, hm):
    tw = tw_ref[...]
    (afr, afi, bfr, bfi), _ = _parity_banks(ef_ref[...], of_ref[...], tw, hm)
    (abr, abi, bbr, bbi), _ = _parity_banks(eb_ref[...], ob_ref[...], tw, hm)
    first = _is_bin0(pl.program_id(0), hm)
    kai = jnp.where(first, ef_ref[hm:, :] + eb_ref[hm:, :], afi - abi)
    kbi = jnp.where(first, ob_ref[hm:, :] - of_ref[hm:, :], bfi - bbi)
    ka_ref[...] = jnp.concatenate([afr + abr, kai], axis=0)
    kb_ref[...] = jnp.concatenate([bfr + bbr, kbi], axis=0)


def _hy_kbank(ef, of, tw, *, tm, tc=512):
    _, ln, c2 = ef.shape
    c = c2 // 2
    fspec = pl.BlockSpec((None, tm, tc), lambda i, j: (0, i, j))
    bspec = pl.BlockSpec((None, tm, tc), lambda i, j: (0, i, j + c // tc))
    return pl.pallas_call(
        functools.partial(_hy_kbank_body, hm=tm // 2),
        out_shape=(jax.ShapeDtypeStruct((1, ln, c), F32),) * 2, grid=(ln // tm, c // tc),
        in_specs=[fspec, bspec, fspec, bspec, pl.BlockSpec((tm, LANES), lambda i, j: (i, 0))],
        out_specs=(fspec, fspec), name="hy_kbank",
        compiler_params=_cp(("parallel", "parallel")))(ef, ef, of, of, tw)


def _hy_spectrum_epilogue(acc, row_blk, e_ref, tw_ref, ka_ref, kb_ref, *, hm):
    e = e_ref[...]
    (ar, ai, br, bi), (c, s) = _parity_banks(e, acc, tw_ref[...], hm)
    kar, kai, kbr, kbi = ka_ref[0:hm], ka_ref[hm:], kb_ref[0:hm], kb_ref[hm:]
    yar, yai = ar * kar - ai * kai, ar * kai + ai * kar
    ybr, ybi = br * kbr - bi * kbi, br * kbi + bi * kbr
    qr, qi = yar - ybr, yai - ybi
    first = _is_bin0(row_blk, hm)
    y0, yl = ar * kar, br * kbr
    amr, ami = e[hm:], -acc[hm:]
    ymr, ymi = amr * kai - ami * kbi, amr * kbi + ami * kai
    pr = jnp.where(first, y0 + yl, yar + ybr)
    pi = jnp.where(first, 2.0 * ymr, yai + ybi)
    qpr = jnp.where(first, y0 - yl, qr * c - qi * s)
    qpi = jnp.where(first, -2.0 * ymi, qi * c + qr * s)
    return jnp.concatenate([pr, pi], axis=0), jnp.concatenate([qpr, qpi], axis=0)


def _interleave_matrix():
    r = lax.broadcasted_iota(jnp.int32, (CONV_CH, CONV_CH), 0)
    c = lax.broadcasted_iota(jnp.int32, (CONV_CH, CONV_CH), 1)
    return (c == r // 2 + (CONV_CH // 2) * (r % 2)).astype(BF16)


def _hyena_out_body(ye_ref, yo_ref, x0_ref, zin_ref, db_ref, buf_ref, o_ref, *, ln):
    del buf_ref
    sel = _interleave_matrix()
    half = CONV_CH // 2

    def body(i, carry):
        rows = pl.ds(pl.multiple_of(i * CONV_CH, CONV_CH), CONV_CH)
        hrows = pl.ds(pl.multiple_of(i * half, half), half)
        y = _dot(sel, jnp.concatenate([ye_ref[hrows, :], yo_ref[hrows, :]], axis=0))
        zin = zin_ref[rows, :].astype(F32)
        o_ref[rows, :] = (x0_ref[rows, :].astype(F32) * (y + zin * db_ref[...])).astype(o_ref.dtype)
        return carry

    lax.fori_loop(0, ln // CONV_CH, body, 0, unroll=min(4, ln // CONV_CH))


def _hyena_out(ye, yo, x0, zin, dbias, ybuf, *, row0, out_col, tc=256):
    nb, ln, c = zin.shape
    rb, cb = row0 // ln, out_col // tc
    return pl.pallas_call(
        functools.partial(_hyena_out_body, ln=ln),
        out_shape=jax.ShapeDtypeStruct(ybuf.shape, ybuf.dtype), grid=(nb, c // tc),
        in_specs=[pl.BlockSpec((None, ln // 2, tc), lambda b, j: (b, 0, j)),
                  pl.BlockSpec((None, ln // 2, tc), lambda b, j: (b, 0, j)),
                  pl.BlockSpec((None, ln, tc), lambda b, j: (b, rb, j)),
                  pl.BlockSpec((None, ln, tc), lambda b, j: (b, 0, j)),
                  pl.BlockSpec((1, tc), lambda b, j: (0, j)),
                  pl.BlockSpec(memory_space=pl.ANY)],
        out_specs=pl.BlockSpec((None, ln, tc), lambda b, j: (b, rb, cb + j)),
        input_output_aliases={5: 0}, name="hyena_out",
        compiler_params=_cp(("parallel", "parallel")))(ye, yo, x0, zin, dbias, ybuf)


def _hyena_longconv(zin, ze, zo, x0, hfb, dbias, ybuf, *, row0, tn=1024):
    nb, ln, c = zin.shape
    h = ln // 2
    tm_s = min(512, ln)
    hm = tm_s // 2
    fwd, inv = _dft_matrices(h)
    fwd = _tile_interleave(fwd, 1, hm)
    inv = _tile_interleave(inv * 0.5, 2, hm)
    tw = _twiddle_table(ln, hm)
    tm_f, tk_f = min(1024, ln), min(2048, h)
    ef = _matmul(fwd, hfb[:, :h], tm=tm_f, tn=tn, tk=tk_f, out_dtype=F32, name="hy_filter_dft_even")
    of = _matmul(fwd, hfb[:, h:], tm=tm_f, tn=tn, tk=tk_f, out_dtype=F32, name="hy_filter_dft_odd")
    ka, kb = _hy_kbank(ef, of, tw, tm=tm_s)
    e = _matmul(fwd, ze, tm=tm_f, tn=tn, tk=tk_f, out_dtype=F32, name="hy_dft_even")
    tile = pl.BlockSpec((None, tm_s, tn), lambda j, b, i, k: (b, i, j))
    ktile = pl.BlockSpec((None, tm_s, tn), lambda j, b, i, k: (0, i, j))
    extra = [(e, tile), (tw, pl.BlockSpec((tm_s, LANES), lambda j, b, i, k: (i, 0))), (ka, ktile), (kb, ktile)]
    p, q = _matmul(fwd, zo, tm=tm_s, tn=tn, tk=tk_f, out_dtype=BF16, extra=extra, n_out=2,
                   epilogue=functools.partial(_hy_spectrum_epilogue, hm=hm), name="hy_dft_odd")
    tm_i, tk_i = min(1024, h), min(2048, ln)
    ye = _matmul(inv, p, tm=tm_i, tn=tn, tk=tk_i, out_dtype=BF16, name="hy_idft_even")
    yo = _matmul(inv, q, tm=tm_i, tn=tn, tk=tk_i, out_dtype=BF16, name="hy_idft_odd")
    return _hyena_out(ye, yo, x0, zin, dbias, ybuf, row0=row0, out_col=SSD_WIDTH)


def _mlstm_body(*refs, reverse, final):
    q_ref, k_ref, v_ref, gc_ref, gr_ref = refs[:5]
    if final:
        hf_ref, og_ref, nw_ref, o_ref, ct_ref, n_ref, m_ref = refs[5:]
    else:
        o_ref, ct_ref, n_ref, m_ref = refs[5:]
    nh, dk, dv = ML_HEADS, ML_QK_DIM, ML_V_DIM
    scale = dk ** -0.5

    @pl.when(pl.program_id(1) == 0)
    def _():
        ct_ref[...] = jnp.zeros_like(ct_ref)
        n_ref[...] = jnp.zeros_like(n_ref)
        m_ref[...] = jnp.zeros_like(m_ref)

    gc, gr = gc_ref[...], gr_ref[...]
    li_c, lf_c = gc[:, 0:nh], _log_sigmoid(gc[:, nh:2 * nh])
    li_r, lf_r = gr[0:nh, :], _log_sigmoid(gr[nh:2 * nh, :])
    mask, maskf, mask_tf = _scan_masks(reverse)
    bc_c = _dot(maskf, lf_c, HI)
    bc_r = _dot(lf_r, mask_tf, HI)
    btot_c = jnp.sum(lf_c, axis=0, keepdims=True)
    n_all, m_all = n_ref[...], m_ref[...]
    n_rows, m_rows = [], []

    for h in range(nh):
        qh = q_ref[:, dk * h:dk * (h + 1)]
        kh = k_ref[:, dk * h:dk * (h + 1)]
        vh = v_ref[:, dv * h:dv * (h + 1)]
        m_s = m_all[h:h + 1, 0:1]
        b_c, b_r = bc_c[:, h:h + 1], bc_r[h:h + 1, :]
        i_c, i_r = li_c[:, h:h + 1], li_r[h:h + 1, :]
        btot = btot_c[:, h:h + 1]
        dmat = jnp.where(mask, b_c - b_r + i_r, -jnp.inf)
        inter = b_c + m_s
        m_t = jnp.maximum(inter, jnp.max(dmat, axis=1, keepdims=True))
        s = _dot_nt(qh, kh) * scale * jnp.exp(dmat - m_t)
        dec = jnp.exp(inter - m_t)
        ct = ct_ref[h]
        n_s = n_all[h:h + 1, :]
        num = _dot(s.astype(BF16), vh) + (dec * scale) * _dot(qh, ct.astype(BF16))
        qn = jnp.sum(qh.astype(F32) * n_s, axis=1, keepdims=True) * scale
        den = jnp.sum(s, axis=1, keepdims=True) + dec * qn
        hh = num / jnp.maximum(jnp.abs(den), jnp.exp(-m_t))

        wst_c = btot - b_c + i_c
        wst_r = btot - b_r + i_r
        m_new = jnp.maximum(btot + m_s, jnp.max(wst_r, axis=1, keepdims=True))
        dstate = jnp.exp(btot + m_s - m_new)
        kw = kh.astype(F32) * jnp.exp(wst_c - m_new)
        ct_ref[h] = dstate * ct + _dot(kw.T.astype(BF16), vh)
        n_rows.append(dstate * n_s + jnp.sum(kw, axis=0, keepdims=True))
        m_rows.append(jnp.broadcast_to(m_new, (1, LANES)))

        cols = slice(dv * h, dv * (h + 1))
        if final:
            hs = hh + hf_ref[:, cols]
            hn = hs * lax.rsqrt(jnp.mean(hs * hs, axis=-1, keepdims=True) + EPS) * nw_ref[:, cols]
            o_ref[:, cols] = (hn * _sigmoid(og_ref[:, cols].astype(F32))).astype(o_ref.dtype)
        else:
            o_ref[:, cols] = hh

    n_ref[...] = jnp.concatenate(n_rows, axis=0)
    m_ref[...] = jnp.concatenate(m_rows, axis=0)


def _mlstm(qk, pr, g_c, g_r, *, direction, n_lat, final_args=None, out_width=None):
    nb, t, _ = qk.shape
    q = SCAN_Q
    nch = t // q
    reverse = direction == 1
    cmap = _chunk_map(reverse, nch, n_lat // q)
    d = direction
    in_specs = [
        pl.BlockSpec((None, q, ML_QK_WIDTH), lambda b, k: (b, cmap(k), 0)),
        pl.BlockSpec((None, q, ML_QK_WIDTH), lambda b, k: (b, cmap(k), 1)),
        pl.BlockSpec((None, q, ML_WIDTH), lambda b, k: (b, cmap(k), 2 * ML_QK_WIDTH // ML_WIDTH)),
        pl.BlockSpec((None, None, q, 2 * ML_HEADS), lambda b, k: (b, d, cmap(k), 0)),
        pl.BlockSpec((None, None, 2 * ML_HEADS, q), lambda b, k: (b, d, 0, cmap(k))),
    ]
    args = [qk, qk, pr, g_c, g_r]
    final = final_args is not None
    if final:
        hf, norm_w = final_args
        in_specs += [pl.BlockSpec((None, q, ML_WIDTH), lambda b, k: (b, cmap(k), 0)),
                     pl.BlockSpec((None, q, ML_WIDTH), lambda b, k: (b, cmap(k), 2 * ML_QK_WIDTH // ML_WIDTH + 1)),
                     pl.BlockSpec((1, ML_WIDTH), lambda b, k: (0, 0))]
        args += [hf, pr, norm_w]
        out_shape = jax.ShapeDtypeStruct((nb, t, out_width), BF16)
    else:
        out_shape = jax.ShapeDtypeStruct((nb, t, ML_WIDTH), F32)
    return pl.pallas_call(
        functools.partial(_mlstm_body, reverse=reverse, final=final),
        out_shape=out_shape, grid=(nb, nch), in_specs=in_specs,
        out_specs=pl.BlockSpec((None, q, ML_WIDTH), lambda b, k: (b, cmap(k), 0)),
        scratch_shapes=[pltpu.VMEM((ML_HEADS, ML_QK_DIM, ML_V_DIM), F32), pltpu.VMEM((ML_HEADS, ML_QK_DIM), F32),
                        pltpu.VMEM((ML_HEADS, LANES), F32)],
        name="mlstm_bwd" if reverse else "mlstm_fwd",
        compiler_params=_cp(("parallel", "arbitrary")))(*args)


def _na_window_start(j, rows):
    return jnp.clip(j * NA_QROWS - NA_ROWS // 2, 0, rows - NA_KROWS)


def _na_body(q_ref, k_ref, v_ref, tab_ref, qw_ref, kw_ref, buf_ref, o_ref, kn_ref, *, n_lat, n_ctx):
    del buf_ref
    dh = NA_HEAD_DIM
    scale = dh ** -0.5
    rows = n_lat // GRID_W
    j = pl.program_id(2)
    nq, nk = NA_QROWS * GRID_W, NA_KROWS * GRID_W

    @pl.when(j == 0)
    def _():
        def body(i, carry):
            r = pl.ds(pl.multiple_of(i * LANES, LANES), LANES)
            kk = k_ref[r, :].astype(F32)
            kn = kk * lax.rsqrt(jnp.mean(kk * kk, axis=-1, keepdims=True) + EPS) * kw_ref[...]
            kn_ref[r, :] = kn.astype(kn_ref.dtype)
            return carry

        lax.fori_loop(0, (n_lat + n_ctx) // LANES, body, 0)

    qq = q_ref[...].astype(F32)
    qn = (qq * lax.rsqrt(jnp.mean(qq * qq, axis=-1, keepdims=True) + EPS) * (qw_ref[...] * scale)).astype(BF16)
    start = _na_window_start(j, rows)
    win = pl.ds(pl.multiple_of(start * GRID_W, GRID_W), nk)
    s = _dot_nt(qn, kn_ref[win, :])
    pieces = []
    for qr in range(NA_QROWS):
        d0 = start - (j * NA_QROWS + qr) + NA_ROWS - 1 + NA_QROWS
        pieces.append(s[qr * GRID_W:(qr + 1) * GRID_W] + tab_ref[d0])
    s = jnp.concatenate(pieces, axis=0)
    ctx = pl.ds(n_lat, n_ctx)
    sc = _dot_nt(qn, kn_ref[ctx, :])
    m = jnp.maximum(jnp.max(s, axis=1, keepdims=True), jnp.max(sc, axis=1, keepdims=True))
    p, pc = jnp.exp(s - m), jnp.exp(sc - m)
    denom = jnp.sum(p, axis=1, keepdims=True) + jnp.sum(pc, axis=1, keepdims=True)
    o = _dot(p.astype(BF16), v_ref[win, :]) + _dot(pc.astype(BF16), v_ref[ctx, :])
    o_ref[...] = (o / denom).astype(o_ref.dtype)


def _na_bias_table(rpb, rows):
    nd = 2 * NA_QROWS
    valid = np.zeros((nd, NA_KROWS), bool)
    seen = np.zeros((nd,), bool)
    for j in range(rows // NA_QROWS):
        start = int(np.clip(j * NA_QROWS - NA_ROWS // 2, 0, rows - NA_KROWS))
        for qr in range(NA_QROWS):
            r = j * NA_QROWS + qr
            rs = int(np.clip(r - NA_ROWS // 2, 0, rows - NA_ROWS))
            d = start - r + NA_ROWS - 1 + NA_QROWS
            v = np.array([(rs <= start + i < rs + NA_ROWS) for i in range(NA_KROWS)])
            assert 0 <= d < nd and (not seen[d] or (valid[d] == v).all())
            valid[d], seen[d] = v, True
    col = np.arange(GRID_W)
    cs = np.clip(col - NA_COLS // 2, 0, GRID_W - NA_COLS)
    in_win = (col[None, :] >= cs[:, None]) & (col[None, :] < cs[:, None] + NA_COLS)
    ndr, ndc, w = 2 * NA_ROWS - 1, 2 * NA_COLS - 1, GRID_W
    dr = (np.arange(nd)[:, None] - NA_QROWS) + np.arange(NA_KROWS)[None, :]
    ok = valid[:, None, :, None] & in_win[None, :, None, :] & (dr >= 0)[:, None, :, None] \
        & (dr < ndr)[:, None, :, None]
    ext = jnp.pad(rpb, ((0, 0), (0, 0), (w, w)), mode="edge")
    c0 = w + NA_COLS - 1
    t1 = jnp.stack([ext[:, :, c0 - qc:c0 - qc + w] for qc in range(w)], axis=2)
    t1p = jnp.pad(t1, ((0, 0), (NA_QROWS, nd + NA_KROWS - 1 - NA_QROWS - ndr), (0, 0), (0, 0)))
    wide = jnp.concatenate([t1p[:, m] for m in range(nd + NA_KROWS - 1)], axis=-1)
    vals = jnp.stack([wide[:, :, d * w:(d + NA_KROWS) * w] for d in range(nd)], axis=1)
    return jnp.where(ok.reshape(nd, w, NA_KROWS * w)[None], vals, -jnp.inf).astype(F32)


def _na(pr, tab, q_w, k_w, ybuf, *, n_lat, n_ctx, qcol, out_col):
    nb, t, _ = pr.shape
    dh, nh = NA_HEAD_DIM, NA_HEADS
    nq = NA_QROWS * GRID_W
    qb, ob = qcol // dh, out_col // dh
    nd = tab.shape[1]
    return pl.pallas_call(
        functools.partial(_na_body, n_lat=n_lat, n_ctx=n_ctx),
        out_shape=jax.ShapeDtypeStruct(ybuf.shape, ybuf.dtype), grid=(nb, nh, n_lat // nq),
        in_specs=[pl.BlockSpec((None, nq, dh), lambda b, h, j: (b, j, qb + h)),
                  pl.BlockSpec((None, t, dh), lambda b, h, j: (b, 0, qb + nh + h)),
                  pl.BlockSpec((None, t, dh), lambda b, h, j: (b, 0, qb + 2 * nh + h)),
                  pl.BlockSpec((None, nd, GRID_W, NA_KROWS * GRID_W), lambda b, h, j: (h, 0, 0, 0)),
                  pl.BlockSpec((1, dh), lambda b, h, j: (0, 0)),
                  pl.BlockSpec((1, dh), lambda b, h, j: (0, 0)),
                  pl.BlockSpec(memory_space=pl.ANY)],
        out_specs=pl.BlockSpec((None, nq, dh), lambda b, h, j: (b, j, ob + h)),
        scratch_shapes=[pltpu.VMEM((t, dh), BF16)],
        input_output_aliases={6: 0}, name="na",
        compiler_params=_cp(("parallel", "parallel", "arbitrary")))(pr, pr, pr, tab, q_w, k_w, ybuf)


def _rope_tables(n_lat, n_ctx, width):
    dh = ML_QK_DIM
    nf = dh // 4
    t = jnp.arange(n_lat)
    inv = ROPE_THETA ** (-jnp.arange(nf, dtype=F32) / nf)
    lane = np.arange(dh)
    pos = jnp.where((lane < dh // 2)[None, :], (t // GRID_W)[:, None], (t % GRID_W)[:, None]).astype(F32)
    ang = pos * inv[lane % nf][None, :]
    sign = np.where((lane % (dh // 2)) < nf, -1.0, 1.0).astype(np.float32)
    cos = jnp.concatenate([jnp.cos(ang), jnp.ones((n_ctx, dh), F32)], axis=0)
    sin = jnp.concatenate([jnp.sin(ang) * sign[None, :], jnp.zeros((n_ctx, dh), F32)], axis=0)
    reps = width // dh
    return jnp.tile(cos, (1, reps)), jnp.tile(sin, (1, reps))


def _ada_mod(c, c_ctx, w_all, b_all, layer):
    nb, d = c.shape
    rows = 16
    cv = jnp.concatenate([c, c_ctx[None, :], jnp.zeros((rows - nb - 1, d), F32)], axis=0)
    a = (cv * _sigmoid(cv)).astype(BF16)[None]
    out = _matmul(a, w_all, w_index=layer, tm=rows, tn=1024, tk=d, out_dtype=F32, name="ada_mod")[0]
    out = out[:nb + 1] + b_all[layer][None, :]
    return out.reshape(nb + 1, 6, d).transpose(1, 0, 2)


def _lat_ctx(m, nb):
    ctx = jnp.broadcast_to(m[nb][None, :], (nb, m.shape[1]))
    return jnp.stack([m[:nb], ctx], axis=1)[:, :, None, :]


def _pad_cols(w, n):
    return jnp.pad(w, ((0, 0), (0, n - w.shape[1])))


def _col_row_forms(g, groups):
    nb, t, _ = g.shape
    g5 = g.reshape(nb, t, 2, groups, -1)
    return g5.transpose(0, 2, 3, 1, 4), g5.transpose(0, 2, 3, 4, 1)


ROW_TILES = (1088, 1024, 512, 256)


def _row_tile(rows):
    return next(tm for tm in ROW_TILES if rows % tm == 0)


def _mod_pair(mods, shift, scale, nb):
    return jnp.stack([_lat_ctx(mods[shift], nb), _lat_ctx(mods[scale], nb)], axis=1)


def _out_projection(a, w_all, x, gate, layer, *, rows, n_lat, tk, next_norm, name):
    tm = _row_tile(rows)
    x = _matmul_gated_residual(a, w_all, x, gate, w_index=layer, tm=tm, tn=1024, tk=tk, rows=rows, n_lat=n_lat,
                               name=name)
    if next_norm is None:
        return x, None
    return x, _rmsmod(x, *next_norm, tm=tm, rows=rows, n_lat=n_lat)


def _conv_ffn_block(x, u, mods, w_up_all, conv_w, conv_b, w_down_all, layer, *, rows, n_lat, segs, next_norm):
    nb = x.shape[0]
    up = _matmul(u, w_up_all, w_index=layer, tm=_row_tile(rows), tn=1024, tk=D_MODEL, out_dtype=BF16, name="ffn_up")
    hid = _conv_ffn(up, conv_w, conv_b[None, :], segs=segs, rows=rows)
    return _out_projection(hid, w_down_all, x, _lat_ctx(mods[5], nb), layer, rows=rows, n_lat=n_lat, tk=D_FF // 4,
                           next_norm=next_norm, name="ffn_down")


def _even_layer(x, u, mods, w_in_all, conv_w, conv_b, dt_bias, a_log, d_skip, ssd_norm_w,
                hy_w1, hy_b1, hy_w2, hy_b2, hy_w3, hy_freq, hy_bias, w_out_all, j, *, n_lat, n_ctx, next_norm):
    nb, t, d = x.shape
    segs = ((0, n_lat), (n_lat, n_ctx))
    tm = _row_tile(t)
    n_main = SSD_WIDTH + EV_CONV_CH
    pr = _matmul(u, w_in_all, w_index=j, cols=n_main, tm=tm, tn=1024, tk=d, out_dtype=BF16, name="ev_in")
    w_dt = _pad_cols(w_in_all[j, :, n_main:], LANES).astype(BF16)
    dtp = _matmul(u, w_dt[None], tm=tm, tn=LANES, tk=d, out_dtype=F32, name="ev_in_dt")[:, :, :2 * SSD_HEADS]
    cb = conv_b[None, :]
    xbc = _conv_silu(pr, conv_w, cb, xcol=SSD_WIDTH, wcol=0, width=SSD_XBC, segs=segs)
    x0, zin_l, zin_c, zle, zlo, zce, zco = _conv_hyena(pr, conv_w, cb, xcol=SSD_WIDTH + SSD_XBC, wcol=SSD_XBC,
                                                       segs=segs)

    dt_c, dt_r = _col_row_forms(dtp, SSD_GROUPS)
    db = dt_bias.reshape(2, SSD_GROUPS, 1, SSD_HPG)
    al = a_log.reshape(2, SSD_GROUPS, 1, SSD_HPG)
    scan_args = (xbc, dt_c, dt_r, db, db.transpose(0, 1, 3, 2), al, al.transpose(0, 1, 3, 2))
    yf = _ssd(*scan_args, direction=0, n_lat=n_lat)
    dsk = jnp.repeat(d_skip, SSD_HEAD_DIM)[None, :]
    ybuf = _ssd(*scan_args, direction=1, n_lat=n_lat, final_args=(yf, pr, dsk, ssd_norm_w[None, :]),
                out_width=SSD_WIDTH + HY_WIDTH)

    hyb = hy_bias[None, :]
    filt = functools.partial(_hyena_filters, w1=hy_w1, b1=hy_b1, w2=hy_w2, b2=hy_b2, w3=hy_w3, freq=hy_freq)
    ybuf = _hyena_longconv(zin_l, zle, zlo, x0, filt(n_lat), hyb, ybuf, row0=0)
    ybuf = _hyena_longconv(zin_c, zce, zco, x0, filt(n_ctx), hyb, ybuf, row0=n_lat)
    return _out_projection(ybuf, w_out_all, x, _lat_ctx(mods[2], nb), j, rows=t, n_lat=n_lat, tk=2048,
                           next_norm=next_norm, name="ev_out")


def _odd_layer_latent(x, u, mods, w_in, conv_w, conv_b, gate_b, ml_norm_w, q_norm_w, k_norm_w, rpb,
                      w_out_all, j, *, n_lat, n_ctx, next_norm):
    nb, t, d = x.shape
    segs = ((0, n_lat), (n_lat, n_ctx))
    tm = _row_tile(t)
    o3 = 2 * ML_QK_WIDTH + 2 * ML_WIDTH
    o4 = o3 + 4 * ML_HEADS
    wb = w_in.astype(BF16)
    w_main = jnp.concatenate([wb[:, :o3], wb[:, o4:]], axis=1)
    pr = _matmul(u, w_main[None], tm=tm, tn=1024, tk=d, out_dtype=BF16, name="od_in")
    gates = _matmul(u, _pad_cols(wb[:, o3:o4], LANES)[None], tm=tm, tn=LANES, tk=d, out_dtype=F32,
                    name="od_in_gates")[:, :, :4 * ML_HEADS] + gate_b.reshape(-1)
    cos, sin = _rope_tables(n_lat, n_ctx, 256)
    qk = _conv_rope(pr, conv_w, conv_b[None, :], cos, sin, width=2 * ML_QK_WIDTH, segs=segs)
    g5 = gates.reshape(nb, t, 2, 2 * ML_HEADS)
    g_c, g_r = g5.transpose(0, 2, 1, 3), g5.transpose(0, 2, 3, 1)
    hf = _mlstm(qk, pr, g_c, g_r, direction=0, n_lat=n_lat)
    ybuf = _mlstm(qk, pr, g_c, g_r, direction=1, n_lat=n_lat, final_args=(hf, ml_norm_w[None, :]),
                  out_width=ML_WIDTH + NA_WIDTH)
    tab = _na_bias_table(rpb, n_lat // GRID_W)
    ybuf = _na(pr, tab, q_norm_w[None, :], k_norm_w[None, :], ybuf, n_lat=n_lat, n_ctx=n_ctx, qcol=o3,
               out_col=ML_WIDTH)
    return _out_projection(ybuf, w_out_all, x, _lat_ctx(mods[2], nb), j, rows=n_lat, n_lat=n_lat, tk=2048,
                           next_norm=next_norm, name="od_out")


def kernel(x, c, ctx, c_ctx, ada_w, ada_b, norm_w, ev_w_in, ev_conv_w, ev_conv_b, ssd_dt_bias, ssd_a_log, ssd_d,
           ssd_norm_w, hy_w1, hy_b1, hy_w2, hy_b2, hy_w3, hy_freq, hy_bias, ev_w_out, od_w_in, ml_conv_w, ml_conv_b,
           ml_gate_b, ml_norm_w, na_q_norm_w, na_k_norm_w, na_rpb, od_w_out, ffn_w_up, ffn_conv_w, ffn_conv_b,
           ffn_w_down):
    depth = ada_w.shape[0]
    assert depth == 2, "layer schedule below is written for one even and one odd layer"
    n_lat, n_ctx = x.shape[1], ctx.shape[1]
    t = n_lat + n_ctx
    both = ((0, n_lat), (n_lat, n_ctx))
    xs = jnp.concatenate([x, ctx], axis=1)

    nb = x.shape[0]
    w_down = ffn_w_down.astype(BF16)
    mods0 = _ada_mod(c, c_ctx, ada_w, ada_b, 0)
    mods1 = _ada_mod(c, c_ctx, ada_w, ada_b, 1)
    norm = lambda layer, sub, mods: (norm_w[layer, sub][None, :], _mod_pair(mods, 3 * sub, 3 * sub + 1, nb))

    w0, mod0 = norm(0, 0, mods0)
    u = _rmsmod(xs, w0, mod0, tm=_row_tile(t), rows=t, n_lat=n_lat)
    xs, u = _even_layer(xs, u, mods0, ev_w_in, ev_conv_w[0], ev_conv_b[0], ssd_dt_bias[0], ssd_a_log[0],
                        ssd_d[0], ssd_norm_w[0], hy_w1[0], hy_b1[0], hy_w2[0], hy_b2[0], hy_w3[0], hy_freq[0],
                        hy_bias[0], ev_w_out.astype(BF16), 0, n_lat=n_lat, n_ctx=n_ctx, next_norm=norm(0, 1, mods0))
    xs, u = _conv_ffn_block(xs, u, mods0, ffn_w_up, ffn_conv_w[0], ffn_conv_b[0], w_down, 0,
                            rows=t, n_lat=n_lat, segs=both, next_norm=norm(1, 0, mods1))
    xl, u = _odd_layer_latent(xs, u, mods1, od_w_in[0], ml_conv_w[0], ml_conv_b[0], ml_gate_b[0], ml_norm_w[0],
                              na_q_norm_w[0], na_k_norm_w[0], na_rpb[0], od_w_out.astype(BF16), 0,
                              n_lat=n_lat, n_ctx=n_ctx, next_norm=norm(1, 1, mods1))
    out, _ = _conv_ffn_block(xl, u, mods1, ffn_w_up, ffn_conv_w[1], ffn_conv_b[1], w_down, 1,
                             rows=n_lat, n_lat=n_lat, segs=((0, n_lat),), next_norm=None)
    return out
```

```python
import functools
import math

import numpy as np
import jax
import jax.numpy as jnp
from jax import lax
from jax.experimental import pallas as pl
from jax.experimental.pallas import tpu as pltpu

F32 = jnp.float32
BF16 = jnp.bfloat16
HI = lax.Precision.HIGHEST

D_MODEL = 2048
SEQ = 4096
CTX_LEN = 256
GRID_W = 64
EPS = 1e-6
CONV_W = 3
SSD_WIDTH = D_MODEL
SSD_HEAD_DIM = 64
SSD_HEADS = SSD_WIDTH // SSD_HEAD_DIM
SSD_STATE = 128
SSD_GROUPS = 4
SSD_HPG = SSD_HEADS // SSD_GROUPS
SSD_GW = SSD_WIDTH // SSD_GROUPS
SSD_XBC = SSD_WIDTH + 2 * SSD_GROUPS * SSD_STATE
HY_WIDTH = D_MODEL
HY_EMB = 33
HY_BANDS = (HY_EMB - 1) // 2
HY_ORDER = 64
HY_FAST = 0.3
HY_SLOW = 1.5
HY_TARGET = 1e-2
ML_HEADS = 8
ML_WIDTH = D_MODEL
ML_V_DIM = ML_WIDTH // ML_HEADS
ML_QK_DIM = ML_V_DIM // 2
ML_QK_WIDTH = ML_HEADS * ML_QK_DIM
ROPE_THETA = 10000.0
NA_WIDTH = D_MODEL
NA_HEAD_DIM = 128
NA_HEADS = NA_WIDTH // NA_HEAD_DIM
NA_ROWS = 8
NA_COLS = 16
NA_QROWS = 8
NA_KROWS = 16
D_FF = 256 * ((8 * D_MODEL // 3 + 255) // 256)
EV_CONV_CH = SSD_XBC + 3 * HY_WIDTH
SCAN_Q = 128
LANES = 128
SUBLANES = 8
VMEM_LIMIT = 48 * 1024 * 1024


def _cp(sem, vmem=VMEM_LIMIT):
    return pltpu.CompilerParams(dimension_semantics=sem, vmem_limit_bytes=vmem)


def _sigmoid(x):
    return 1.0 / (1.0 + jnp.exp(-x))


def _silu(x):
    return x * _sigmoid(x)


def _softplus(x):
    return jnp.maximum(x, 0.0) + jnp.log(1.0 + jnp.exp(-jnp.abs(x)))


def _log_sigmoid(x):
    return -_softplus(-x)


def _dot(a, b, precision=None):
    return jnp.dot(a, b, preferred_element_type=F32, precision=precision)


def _dot_nt(a, b):
    return lax.dot_general(a, b, (((1,), (1,)), ((), ())), preferred_element_type=F32)


def _split_bf16(x, terms):
    out = []
    for _ in range(terms):
        part = x.astype(BF16)
        out.append(part)
        x = x - part.astype(F32)
    return out


def _dot_sel(x, sel, terms):
    selb = sel.astype(BF16)
    return sum(_dot(part, selb) for part in _split_bf16(x, terms))


def _sel_dot(sel, x, terms):
    selb = sel.astype(BF16)
    return sum(_dot(selb, part) for part in _split_bf16(x, terms))


def _mm_body(*refs, nk, n_extra, epilogue, has_buf, cache_w, n_out):
    a_ref, w_ref = refs[0], refs[1]
    extra = refs[2:2 + n_extra]
    pos = 2 + n_extra + (1 if has_buf else 0)
    o_refs = refs[pos:pos + n_out]
    pos += n_out - 1
    row_blk = pl.program_id(2)
    if cache_w:
        wb_ref = refs[pos + 1]

        @pl.when((pl.program_id(1) == 0) & (row_blk == 0))
        def _():
            wb_ref[...] = w_ref[...].astype(BF16)

        w_val = wb_ref[...]
    else:
        w_val = w_ref[...].astype(BF16)
    part = _dot(a_ref[...].astype(BF16), w_val)

    def finish(acc):
        vals = acc if epilogue is None else epilogue(acc, row_blk, *extra)
        for o_ref, val in zip(o_refs, vals if n_out > 1 else (vals,)):
            o_ref[...] = val.astype(o_ref.dtype)

    if nk == 1:
        finish(part)
    else:
        acc_ref = refs[pos + 1]
        k = pl.program_id(3)

        @pl.when(k == 0)
        def _():
            acc_ref[...] = part

        @pl.when(k > 0)
        def _():
            acc_ref[...] += part

        @pl.when(k == nk - 1)
        def _():
            finish(acc_ref[...])


def _matmul(a, w, *, tm, tn, tk, out_dtype, rows=None, cols=None, w_index=None, extra=(), epilogue=None,
            out_buf=None, out_row_blk=0, out_col_blk=0, n_out=1, name="mm"):
    ba, ta, kdim = a.shape
    bw, kdim2, n = w.shape
    assert kdim == kdim2
    nb = ba if w_index is not None else max(ba, bw)
    rows = ta if rows is None else rows
    n = n if cols is None else cols
    assert rows % tm == 0 and n % tn == 0 and kdim % tk == 0
    ni, nj, nk = rows // tm, n // tn, kdim // tk
    a_map = (lambda j, b, i, k: (b, i, k)) if ba > 1 else (lambda j, b, i, k: (0, i, k))
    if w_index is not None:
        w_map = lambda j, b, i, k: (w_index, k, j)
    else:
        w_map = (lambda j, b, i, k: (b, k, j)) if bw > 1 else (lambda j, b, i, k: (0, k, j))
    cache_w = w.dtype != BF16 and nk == 1 and (w_index is not None or bw == 1)
    in_specs = [pl.BlockSpec((None, tm, tk), a_map), pl.BlockSpec((None, tk, tn), w_map)]
    args = [a, w]
    for arr, spec in extra:
        args.append(arr)
        in_specs.append(spec)
    aliases = {}
    if out_buf is not None:
        aliases = {len(args): 0}
        args.append(out_buf)
        in_specs.append(pl.BlockSpec(memory_space=pl.ANY))
        out_shape = jax.ShapeDtypeStruct(out_buf.shape, out_buf.dtype)
    else:
        out_shape = jax.ShapeDtypeStruct((nb, rows, n), out_dtype)
    out_spec = pl.BlockSpec((None, tm, tn), lambda j, b, i, k: (b, i + out_row_blk, j + out_col_blk))
    if n_out > 1:
        out_shape, out_spec = (out_shape,) * n_out, (out_spec,) * n_out
    scratch = [pltpu.VMEM((tm, tn), F32)] if nk > 1 else []
    if cache_w:
        scratch = [pltpu.VMEM((tk, tn), BF16)]
    body = functools.partial(_mm_body, nk=nk, n_extra=len(extra), epilogue=epilogue,
                             has_buf=out_buf is not None, cache_w=cache_w, n_out=n_out)
    inner = "arbitrary" if cache_w else "parallel"
    return pl.pallas_call(
        body, out_shape=out_shape, grid=(nj, nb, ni, nk), in_specs=in_specs, out_specs=out_spec,
        scratch_shapes=scratch, input_output_aliases=aliases, name=name,
        compiler_params=_cp(("parallel", inner, inner, "arbitrary")))(*args)


def _row_is_lat(row_blk, tm, n_lat):
    rows = row_blk * tm + lax.broadcasted_iota(jnp.int32, (tm, 1), 0)
    return rows < n_lat


def _gated_residual_epilogue(acc, row_blk, res_ref, gate_ref, *, tm, n_lat):
    gate = jnp.where(_row_is_lat(row_blk, tm, n_lat), gate_ref[0], gate_ref[1])
    return res_ref[...] + gate * acc


def _modulated_rmsnorm(x, is_lat, w_ref, mod_ref):
    y = x * lax.rsqrt(jnp.mean(x * x, axis=-1, keepdims=True) + EPS) * w_ref[...]
    shift = jnp.where(is_lat, mod_ref[0, 0], mod_ref[0, 1])
    scale = jnp.where(is_lat, mod_ref[1, 0], mod_ref[1, 1])
    return y * (1.0 + scale) + shift


def _matmul_gated_residual(a, w, res, gate, *, w_index, tm, tn, tk, rows, n_lat, name):
    extra = [(res, pl.BlockSpec((None, tm, tn), lambda j, b, i, k: (b, i, j))),
             (gate, pl.BlockSpec((None, 2, 1, tn), lambda j, b, i, k: (b, 0, 0, j)))]
    epi = functools.partial(_gated_residual_epilogue, tm=tm, n_lat=n_lat)
    return _matmul(a, w, w_index=w_index, tm=tm, tn=tn, tk=tk, out_dtype=F32, rows=rows, extra=extra,
                   epilogue=epi, name=name)


def _rmsmod_body(x_ref, w_ref, mod_ref, o_ref, *, tm, n_lat):
    is_lat = _row_is_lat(pl.program_id(2), tm, n_lat)
    o_ref[...] = _modulated_rmsnorm(x_ref[...], is_lat, w_ref, mod_ref).astype(o_ref.dtype)


def _rmsmod(x, w, mod, *, tm, rows, n_lat):
    nb, _, d = x.shape
    tm = tm // 4
    assert rows % tm == 0 and tm % 16 == 0
    body = functools.partial(_rmsmod_body, tm=tm, n_lat=n_lat)
    return pl.pallas_call(
        body, out_shape=jax.ShapeDtypeStruct((nb, rows, d), BF16), grid=(1, nb, rows // tm),
        in_specs=[pl.BlockSpec((None, tm, d), lambda j, b, i: (b, i, 0)),
                  pl.BlockSpec((1, d), lambda j, b, i: (0, 0)),
                  pl.BlockSpec((None, 2, 2, 1, d), lambda j, b, i: (b, 0, 0, 0, 0))],
        out_specs=pl.BlockSpec((None, tm, d), lambda j, b, i: (b, i, 0)),
        name="rmsmod", compiler_params=_cp(("parallel", "parallel", "parallel")))(x, w, mod)


CONV_CH = 128
CONV_HALO = 16


def _shift_matrices():
    r = lax.broadcasted_iota(jnp.int32, (CONV_CH, CONV_CH + 2 * CONV_HALO), 0)
    c = lax.broadcasted_iota(jnp.int32, (CONV_CH, CONV_CH + 2 * CONV_HALO), 1)
    return (c == r + CONV_HALO - 1).astype(BF16), (c == r + CONV_HALO + 1).astype(BF16)


def _conv_at(x_ref, w_ref, b_ref, shifts, r0, n, i):
    t0 = pl.multiple_of(i * CONV_CH, CONV_CH)
    cur = x_ref[pl.ds(r0 + t0, CONV_CH), :]
    lo = pl.multiple_of(jnp.maximum(t0 - CONV_HALO, 0), CONV_HALO)
    hi = pl.multiple_of(jnp.minimum(t0 + CONV_CH, n - CONV_HALO), CONV_HALO)
    zero = jnp.zeros((CONV_HALO, cur.shape[1]), cur.dtype)
    tail = jnp.where(i > 0, x_ref[pl.ds(r0 + lo, CONV_HALO), :], zero)
    head = jnp.where(i < n // CONV_CH - 1, x_ref[pl.ds(r0 + hi, CONV_HALO), :], zero)
    win = jnp.concatenate([tail, cur, head], axis=0)
    return (w_ref[0:1, :] * _dot(shifts[0], win) + w_ref[1:2, :] * cur.astype(F32)
            + w_ref[2:3, :] * _dot(shifts[1], win) + b_ref[...])


def _conv_loop(segs, fn):
    for si, (r0, n) in enumerate(segs):
        def body(i, carry, r0=r0, n=n, si=si):
            fn(si, r0, n, i)
            return carry

        lax.fori_loop(0, n // CONV_CH, body, 0, unroll=min(4, n // CONV_CH))


def _conv_silu_body(x_ref, w_ref, b_ref, o_ref, *, segs):
    shifts = _shift_matrices()

    def fn(si, r0, n, i):
        y = _silu(_conv_at(x_ref, w_ref, b_ref, shifts, r0, n, i))
        o_ref[pl.ds(r0 + pl.multiple_of(i * CONV_CH, CONV_CH), CONV_CH), :] = y.astype(o_ref.dtype)

    _conv_loop(segs, fn)


def _conv_rope_body(x_ref, w_ref, b_ref, cos_ref, sin_ref, o_ref, *, segs, tc):
    shifts = _shift_matrices()
    lane = lax.broadcasted_iota(jnp.int32, (CONV_CH, tc), 1)
    first = (lane % (ML_QK_DIM // 2)) < (ML_QK_DIM // 4)
    nf = ML_QK_DIM // 4

    def fn(si, r0, n, i):
        y = _silu(_conv_at(x_ref, w_ref, b_ref, shifts, r0, n, i))
        partner = jnp.where(first, pltpu.roll(y, tc - nf, 1), pltpu.roll(y, nf, 1))
        rows = pl.ds(r0 + pl.multiple_of(i * CONV_CH, CONV_CH), CONV_CH)
        o_ref[rows, :] = (y * cos_ref[rows, :] + partner * sin_ref[rows, :]).astype(o_ref.dtype)

    _conv_loop(segs, fn)


def _conv_ffn_body(a_ref, x_ref, w_ref, b_ref, o_ref, *, segs):
    shifts = _shift_matrices()

    def fn(si, r0, n, i):
        rows = pl.ds(r0 + pl.multiple_of(i * CONV_CH, CONV_CH), CONV_CH)
        g = _silu(_conv_at(x_ref, w_ref, b_ref, shifts, r0, n, i))
        o_ref[rows, :] = (a_ref[rows, :].astype(F32) * g).astype(o_ref.dtype)

    _conv_loop(segs, fn)


def _conv_hyena_body(x0_ref, x1_ref, v_ref, w0_ref, b0_ref, w1_ref, b1_ref, w2_ref, b2_ref,
                     ox0_ref, ozl_ref, ozc_ref, ozle_ref, ozlo_ref, ozce_ref, ozco_ref, *zs_refs, segs):
    shifts = _shift_matrices()
    half = CONV_CH // 2
    outs = ((ozl_ref, ozle_ref, ozlo_ref), (ozc_ref, ozce_ref, ozco_ref))

    def fn(si, r0, n, i):
        t0 = pl.multiple_of(i * CONV_CH, CONV_CH)
        x0 = _conv_at(x0_ref, w0_ref, b0_ref, shifts, r0, n, i)
        ox0_ref[pl.ds(r0 + t0, CONV_CH), :] = x0.astype(ox0_ref.dtype)
        zin = _conv_at(x1_ref, w1_ref, b1_ref, shifts, r0, n, i) * _conv_at(v_ref, w2_ref, b2_ref, shifts, r0, n, i)
        nat_ref, even_ref, odd_ref = outs[si]
        nat_ref[pl.ds(t0, CONV_CH), :] = zin.astype(BF16)
        hrows = pl.ds(pl.multiple_of(i * half, half), half)
        for g, zs_ref in enumerate(zs_refs):
            lanes = slice(g * LANES, (g + 1) * LANES)
            zs_ref[pl.ds(r0 + t0, CONV_CH), :] = zin[:, lanes]
            even_ref[hrows, lanes] = zs_ref[pl.ds(r0 + t0, half, stride=2), :].astype(BF16)
            odd_ref[hrows, lanes] = zs_ref[pl.ds(r0 + t0 + 1, half, stride=2), :].astype(BF16)

    _conv_loop(segs, fn)


def _conv_specs(t_in, tc, xcol_blk, wcol_blk):
    return [pl.BlockSpec((None, t_in, tc), lambda b, j: (b, 0, j + xcol_blk)),
            pl.BlockSpec((CONV_W, tc), lambda b, j: (0, j + wcol_blk)),
            pl.BlockSpec((1, tc), lambda b, j: (0, j + wcol_blk))]


def _conv_silu(x, w, b, *, xcol, wcol, width, segs, tc=256):
    nb, t_in, _ = x.shape
    return pl.pallas_call(
        functools.partial(_conv_silu_body, segs=segs),
        out_shape=jax.ShapeDtypeStruct((nb, t_in, width), BF16), grid=(nb, width // tc),
        in_specs=_conv_specs(t_in, tc, xcol // tc, wcol // tc),
        out_specs=pl.BlockSpec((None, t_in, tc), lambda b, j: (b, 0, j)),
        name="conv_silu", compiler_params=_cp(("parallel", "parallel")))(x, w, b)


def _conv_rope(x, w, b, cos, sin, *, width, segs, tc=256):
    nb, t_in, _ = x.shape
    tab = pl.BlockSpec((t_in, tc), lambda b, j: (0, 0))
    return pl.pallas_call(
        functools.partial(_conv_rope_body, segs=segs, tc=tc),
        out_shape=jax.ShapeDtypeStruct((nb, t_in, width), BF16), grid=(nb, width // tc),
        in_specs=_conv_specs(t_in, tc, 0, 0) + [tab, tab],
        out_specs=pl.BlockSpec((None, t_in, tc), lambda b, j: (b, 0, j)),
        name="conv_rope", compiler_params=_cp(("parallel", "parallel")))(x, w, b, cos, sin)


def _conv_ffn(up, w, b, *, segs, rows, tc=256):
    nb, t_in, two_ff = up.shape
    ff = two_ff // 2
    return pl.pallas_call(
        functools.partial(_conv_ffn_body, segs=segs),
        out_shape=jax.ShapeDtypeStruct((nb, rows, ff), BF16), grid=(nb, ff // tc),
        in_specs=[pl.BlockSpec((None, rows, tc), lambda b, j: (b, 0, j))] + _conv_specs(rows, tc, ff // tc, 0),
        out_specs=pl.BlockSpec((None, rows, tc), lambda b, j: (b, 0, j)),
        name="conv_ffn", compiler_params=_cp(("parallel", "parallel")))(up, up, w, b)


def _conv_hyena(pr, w, b, *, xcol, wcol, segs, tc=256):
    nb, t_in, _ = pr.shape
    (_, n_lat), (_, n_ctx) = segs
    wd = HY_WIDTH
    out_rows = (t_in, n_lat, n_ctx, n_lat // 2, n_lat // 2, n_ctx // 2, n_ctx // 2)
    in_specs = [pl.BlockSpec((None, t_in, tc), functools.partial(lambda b, j, o: (b, 0, j + o), o=(xcol + m * wd) // tc))
                for m in range(3)]
    wargs = []
    for m in range(3):
        off = (wcol + m * wd) // tc
        in_specs.append(pl.BlockSpec((CONV_W, tc), functools.partial(lambda b, j, o: (0, j + o), o=off)))
        in_specs.append(pl.BlockSpec((1, tc), functools.partial(lambda b, j, o: (0, j + o), o=off)))
        wargs += [w, b]
    return pl.pallas_call(
        functools.partial(_conv_hyena_body, segs=segs),
        out_shape=tuple(jax.ShapeDtypeStruct((nb, rows, wd), BF16) for rows in out_rows),
        grid=(nb, wd // tc), in_specs=in_specs,
        out_specs=tuple(pl.BlockSpec((None, rows, tc), lambda b, j: (b, 0, j)) for rows in out_rows),
        scratch_shapes=[pltpu.VMEM((t_in, LANES), F32)] * (tc // LANES),
        name="conv_hyena", compiler_params=_cp(("parallel", "parallel")))(pr, pr, pr, *wargs)


def _scan_masks(reverse):
    q = SCAN_Q
    ii = lax.broadcasted_iota(jnp.int32, (q, q), 0)
    jj = lax.broadcasted_iota(jnp.int32, (q, q), 1)
    mask = (jj >= ii) if reverse else (jj <= ii)
    mask_t = (jj <= ii) if reverse else (jj >= ii)
    return mask, mask.astype(F32), mask_t.astype(F32)


def _chunk_map(reverse, n_chunks, n_lat_chunks):
    if reverse:
        return lambda k: n_chunks - 1 - k
    return lambda k: (k + n_lat_chunks) % n_chunks


SSD_GPS = 4


def _ssd_body(*refs, reverse, final):
    (xs_ref, b_ref, c_ref, dtc_ref, dtr_ref, dbc_ref, dbr_ref, alc_ref, alr_ref) = refs[:9]
    if final:
        yf_ref, z_ref, dsk_ref, nw_ref, o_ref, h_ref = refs[9:]
    else:
        o_ref, h_ref = refs[9:]

    @pl.when(pl.program_id(2) == 0)
    def _():
        h_ref[...] = jnp.zeros_like(h_ref)

    masks = _scan_masks(reverse)
    for gi in range(SSD_GPS):
        cols = slice(gi * SSD_GW, (gi + 1) * SSD_GW)
        ncols = slice(gi * SSD_STATE, (gi + 1) * SSD_STATE)
        xs = xs_ref[:, cols].astype(F32)
        dt_c = _softplus(dtc_ref[gi] + dbc_ref[gi])
        dt_r = _softplus(dtr_ref[gi] + dbr_ref[gi])
        da_c = dt_c * (-jnp.exp(alc_ref[gi]))
        da_r = dt_r * (-jnp.exp(alr_ref[gi]))
        y, h_new = _ssd_group(xs, b_ref[:, ncols], c_ref[:, ncols], dt_c, da_c, da_r, h_ref[gi], masks)
        h_ref[gi] = h_new
        if final:
            ytot = y + yf_ref[:, cols] + xs * dsk_ref[:, cols]
            g = ytot * _silu(z_ref[:, cols].astype(F32))
            o = g * lax.rsqrt(jnp.mean(g * g, axis=-1, keepdims=True) + EPS) * nw_ref[:, cols]
            o_ref[:, cols] = o.astype(o_ref.dtype)
        else:
            o_ref[:, cols] = y


def _ssd_group(xs, bm, cm, dt_c, da_c, da_r, h_prev, masks):
    q, gw, hd = SCAN_Q, SSD_GW, SSD_HEAD_DIM
    mask, maskf, mask_tf = masks
    acum_c = _sel_dot(maskf, da_c, 3)
    acum_r = _dot_sel(da_r, mask_tf, 3)
    atot = jnp.sum(da_c, axis=0, keepdims=True)

    er = lax.broadcasted_iota(jnp.int32, (SSD_HPG, gw), 0)
    ec = lax.broadcasted_iota(jnp.int32, (SSD_HPG, gw), 1)
    expand = (ec // hd == er).astype(F32)
    stack = jnp.concatenate([dt_c, jnp.exp(atot - acum_c), jnp.exp(acum_c),
                             jnp.broadcast_to(jnp.exp(atot), (SUBLANES, SSD_HPG))], axis=0)
    wide = _dot_sel(stack, expand, 2)
    dt_e, ws_e, eac_e, eat_e = wide[0:q], wide[q:2 * q], wide[2 * q:3 * q], wide[3 * q:3 * q + 1]

    xq = xs * dt_e
    cb = _dot_nt(cm, bm)
    y = eac_e * _dot(cm, h_prev.astype(BF16))
    lane = lax.broadcasted_iota(jnp.int32, (q, 2 * hd), 1)
    pieces = []
    for p in range(SSD_HPG // 2):
        ms = []
        for r in (2 * p, 2 * p + 1):
            seg = acum_c[:, r:r + 1] - acum_r[r:r + 1, :]
            ms.append((cb * jnp.where(mask, jnp.exp(seg), 0.0)).astype(BF16))
        xp = xq[:, 2 * hd * p:2 * hd * (p + 1)]
        rhs = jnp.concatenate([jnp.where(lane < hd, xp, 0.0), jnp.where(lane >= hd, xp, 0.0)], axis=0)
        pieces.append(_dot(jnp.concatenate(ms, axis=1), rhs.astype(BF16)))
    y = y + jnp.concatenate(pieces, axis=1)

    bt = bm.astype(F32).T.astype(BF16)
    return y, eat_e * h_prev + _dot(bt, (xq * ws_e).astype(BF16))


def _ssd(xbc, dt_c, dt_r, db_c, db_r, al_c, al_r, *, direction, n_lat, final_args=None, out_width=None):
    nb, t, _ = xbc.shape
    q, gw, hpg = SCAN_Q, SSD_GW, SSD_HPG
    nch = t // q
    reverse = direction == 1
    cmap = _chunk_map(reverse, nch, n_lat // q)
    gps = SSD_GPS
    sw, gww = gps * SSD_STATE, gps * gw
    b0 = SSD_WIDTH // sw
    c0 = b0 + SSD_GROUPS // gps
    d = direction
    in_specs = [
        pl.BlockSpec((None, q, gww), lambda b, g, k: (b, cmap(k), g)),
        pl.BlockSpec((None, q, sw), lambda b, g, k: (b, cmap(k), b0 + g)),
        pl.BlockSpec((None, q, sw), lambda b, g, k: (b, cmap(k), c0 + g)),
        pl.BlockSpec((None, None, gps, q, hpg), lambda b, g, k: (b, d, g, cmap(k), 0)),
        pl.BlockSpec((None, None, gps, hpg, q), lambda b, g, k: (b, d, g, 0, cmap(k))),
        pl.BlockSpec((None, gps, 1, hpg), lambda b, g, k: (d, g, 0, 0)),
        pl.BlockSpec((None, gps, hpg, 1), lambda b, g, k: (d, g, 0, 0)),
        pl.BlockSpec((None, gps, 1, hpg), lambda b, g, k: (d, g, 0, 0)),
        pl.BlockSpec((None, gps, hpg, 1), lambda b, g, k: (d, g, 0, 0)),
    ]
    args = [xbc, xbc, xbc, dt_c, dt_r, db_c, db_r, al_c, al_r]
    final = final_args is not None
    if final:
        yf, pr, dskip, norm_w = final_args
        in_specs += [pl.BlockSpec((None, q, gww), lambda b, g, k: (b, cmap(k), g)),
                     pl.BlockSpec((None, q, gww), lambda b, g, k: (b, cmap(k), g)),
                     pl.BlockSpec((1, gww), lambda b, g, k: (0, g)),
                     pl.BlockSpec((1, gww), lambda b, g, k: (0, g))]
        args += [yf, pr, dskip, norm_w]
        out_shape = jax.ShapeDtypeStruct((nb, t, out_width), BF16)
    else:
        out_shape = jax.ShapeDtypeStruct((nb, t, SSD_WIDTH), F32)
    return pl.pallas_call(
        functools.partial(_ssd_body, reverse=reverse, final=final),
        out_shape=out_shape, grid=(nb, SSD_GROUPS // gps, nch), in_specs=in_specs,
        out_specs=pl.BlockSpec((None, q, gww), lambda b, g, k: (b, cmap(k), g)),
        scratch_shapes=[pltpu.VMEM((gps, SSD_STATE, gw), F32)],
        name="ssd_bwd" if reverse else "ssd_fwd",
        compiler_params=_cp(("parallel", "parallel", "arbitrary")))(*args)


def _hyfilt_body(f_ref, w1_ref, b1_ref, w2_ref, b2_ref, w3_ref, fr_ref, dl_ref, o_ref, *, tl):
    f = f_ref[...]
    h = jnp.sin(fr_ref[0:1, :] * (_dot(f, w1_ref[...], HI) + b1_ref[...]))
    h = jnp.sin(fr_ref[1:2, :] * (_dot(h, w2_ref[...], HI) + b2_ref[...]))
    h = _dot(h, w3_ref[...], HI)
    dec = jnp.exp(-f[:, 0:1] * dl_ref[...])
    row = pl.program_id(0) * tl + lax.broadcasted_iota(jnp.int32, (tl, 1), 0)
    wd = HY_WIDTH
    o_ref[:, 0:wd] = (h[:, 0:wd] * dec).astype(o_ref.dtype)
    o_ref[:, wd:2 * wd] = jnp.where(row == 0, 0.0, h[:, wd:2 * wd] * dec).astype(o_ref.dtype)


def _hyena_filters(length, w1, b1, w2, b2, w3, freq, *, tl=256):
    p = LANES
    t = jnp.linspace(0.0, 1.0, length, dtype=F32)[:, None]
    w = 2.0 * math.pi * jnp.arange(length, dtype=F32)[:, None] / length
    f = jnp.linspace(1e-4, HY_BANDS - 1, HY_BANDS, dtype=F32)[None, :]
    feats = jnp.concatenate([t, jnp.cos(f * w), -jnp.sin(f * w)], axis=-1)
    feats = jnp.concatenate([feats[0::2], feats[1::2]], axis=0)
    feats = jnp.pad(feats, ((0, 0), (0, p - HY_EMB)))
    po = p - HY_ORDER
    w1p = jnp.pad(w1, ((0, p - HY_EMB), (0, po)))
    w2p = jnp.pad(w2, ((0, po), (0, po)))
    w3p = jnp.pad(w3, ((0, po), (0, 0)))
    b1p = jnp.pad(b1[None, :], ((0, 0), (0, po)))
    b2p = jnp.pad(b2[None, :], ((0, 0), (0, po)))
    frp = jnp.pad(freq, ((0, 0), (0, po)))
    deltas = jnp.abs(jnp.linspace(math.log(HY_TARGET) / HY_FAST, math.log(HY_TARGET) / HY_SLOW, HY_WIDTH,
                                  dtype=F32))[None, :]
    full = lambda shape: pl.BlockSpec(shape, lambda i: (0,) * len(shape))
    out = pl.pallas_call(
        functools.partial(_hyfilt_body, tl=tl),
        out_shape=jax.ShapeDtypeStruct((length, 2 * HY_WIDTH), BF16), grid=(length // tl,),
        in_specs=[pl.BlockSpec((tl, p), lambda i: (i, 0)), full((p, p)), full((1, p)), full((p, p)), full((1, p)),
                  full((p, 2 * HY_WIDTH)), full((2, p)), full((1, HY_WIDTH))],
        out_specs=pl.BlockSpec((tl, 2 * HY_WIDTH), lambda i: (i, 0)),
        name="hyena_filters", compiler_params=_cp(("parallel",)))(feats, w1p, b1p, w2p, b2p, w3p, frp, deltas)
    return out[None]


def _dft_matrices(length):
    n = 2 * length
    k = jnp.arange(length, dtype=jnp.int32)[:, None]
    t = jnp.arange(length, dtype=jnp.int32)[None, :]
    step = 1 << (int(math.log2(length)) // 2)
    part = jnp.arange(length // step, dtype=jnp.int32)[:, None]
    ang_hi = ((part * step * t) % n).astype(F32) * (2.0 * math.pi / n)
    ang_lo = ((jnp.arange(step, dtype=jnp.int32)[:, None] * t) % n).astype(F32) * (2.0 * math.pi / n)
    ch, sh = jnp.cos(ang_hi)[:, None, :], jnp.sin(ang_hi)[:, None, :]
    cl, sl = jnp.cos(ang_lo)[None, :, :], jnp.sin(ang_lo)[None, :, :]
    c = (ch * cl - sh * sl).reshape(length, length)
    s = (sh * cl + ch * sl).reshape(length, length)
    alt_t = jnp.where(t % 2 == 0, 1.0, -1.0).astype(F32)
    fwd = jnp.concatenate([c, jnp.where(k == 0, alt_t, -s)], axis=0)
    alt_r = jnp.where(k % 2 == 0, 1.0, -1.0).astype(F32)
    inv_re = jnp.where(t == 0, 1.0 / n, (2.0 / n) * c)
    inv_im = jnp.where(t == 0, alt_r / n, (-2.0 / n) * s)
    inv = jnp.concatenate([inv_re, inv_im], axis=1)
    return fwd.astype(BF16)[None], inv.astype(BF16)[None]


def _tile_interleave(x, axis, hm):
    shp = x.shape
    n = shp[axis] // 2
    x = x.reshape(shp[:axis] + (2, n // hm, hm) + shp[axis + 1:])
    return jnp.swapaxes(x, axis, axis + 1).reshape(shp)


def _twiddle_table(length, hm):
    k = np.arange(length // 2, dtype=np.float64)[:, None] * (np.pi / length)
    tab = np.concatenate([np.cos(k), np.sin(k)], axis=0)
    tab = tab.reshape(2, -1, hm, 1).swapaxes(0, 1).reshape(length, 1)
    return jnp.asarray(np.broadcast_to(tab, (length, LANES)).astype(np.float32))


def _lanes(x, width):
    return jnp.tile(x, (1, width // LANES))


def _parity_banks(e, ot, tw, hm):
    er, ei, otr, oti = e[0:hm], e[hm:], ot[0:hm], ot[hm:]
    c, s = _lanes(tw[0:hm], e.shape[1]), _lanes(tw[hm:], e.shape[1])
    orr = otr * c + oti * s
    oi = oti * c - otr * s
    return (er + orr, ei + oi, er - orr, ei - oi), (c, s)


def _is_bin0(row_blk, hm):
    return (row_blk == 0) & (lax.broadcasted_iota(jnp.int32, (hm, 1), 0) == 0)


def _hy_kbank_body(ef_ref, eb_ref, of_ref, ob_ref, tw_ref, ka_ref, kb_ref, *, hm):
    tw = tw_ref[...]
    (afr, afi, bfr, bfi), _ = _parity_banks(ef_ref[...], of_ref[...], tw, hm)
    (abr, abi, bbr, bbi), _ = _parity_banks(eb_ref[...], ob_ref[...], tw, hm)
    first = _is_bin0(pl.program_id(0), hm)
    kai = jnp.where(first, ef_ref[hm:, :] + eb_ref[hm:, :], afi - abi)
    kbi = jnp.where(first, ob_ref[hm:, :] - of_ref[hm:, :], bfi - bbi)
    ka_ref[...] = jnp.concatenate([afr + abr, kai], axis=0)
    kb_ref[...] = jnp.concatenate([bfr + bbr, kbi], axis=0)


def _hy_kbank(ef, of, tw, *, tm, tc=512):
    _, ln, c2 = ef.shape
    c = c2 // 2
    fspec = pl.BlockSpec((None, tm, tc), lambda i, j: (0, i, j))
    bspec = pl.BlockSpec((None, tm, tc), lambda i, j: (0, i, j + c // tc))
    return pl.pallas_call(
        functools.partial(_hy_kbank_body, hm=tm // 2),
        out_shape=(jax.ShapeDtypeStruct((1, ln, c), F32),) * 2, grid=(ln // tm, c // tc),
        in_specs=[fspec, bspec, fspec, bspec, pl.BlockSpec((tm, LANES), lambda i, j: (i, 0))],
        out_specs=(fspec, fspec), name="hy_kbank",
        compiler_params=_cp(("parallel", "parallel")))(ef, ef, of, of, tw)


def _hy_spectrum_epilogue(acc, row_blk, e_ref, tw_ref, ka_ref, kb_ref, *, hm):
    e = e_ref[...]
    (ar, ai, br, bi), (c, s) = _parity_banks(e, acc, tw_ref[...], hm)
    kar, kai, kbr, kbi = ka_ref[0:hm], ka_ref[hm:], kb_ref[0:hm], kb_ref[hm:]
    yar, yai = ar * kar - ai * kai, ar * kai + ai * kar
    ybr, ybi = br * kbr - bi * kbi, br * kbi + bi * kbr
    qr, qi = yar - ybr, yai - ybi
    first = _is_bin0(row_blk, hm)
    y0, yl = ar * kar, br * kbr
    amr, ami = e[hm:], -acc[hm:]
    ymr, ymi = amr * kai - ami * kbi, amr * kbi + ami * kai
    pr = jnp.where(first, y0 + yl, yar + ybr)
    pi = jnp.where(first, 2.0 * ymr, yai + ybi)
    qpr = jnp.where(first, y0 - yl, qr * c - qi * s)
    qpi = jnp.where(first, -2.0 * ymi, qi * c + qr * s)
    return jnp.concatenate([pr, pi], axis=0), jnp.concatenate([qpr, qpi], axis=0)


def _hyena_out_body(ye_ref, yo_ref, x0_ref, zin_ref, db_ref, buf_ref, o_ref, *ys_refs, ln):
    del buf_ref
    half = CONV_CH // 2

    def body(i, carry):
        t0 = pl.multiple_of(i * CONV_CH, CONV_CH)
        rows = pl.ds(t0, CONV_CH)
        hrows = pl.ds(pl.multiple_of(i * half, half), half)
        for g, ys_ref in enumerate(ys_refs):
            lanes = slice(g * LANES, (g + 1) * LANES)
            ys_ref[pl.ds(t0, half, stride=2), :] = ye_ref[hrows, lanes].astype(F32)
            ys_ref[pl.ds(t0 + 1, half, stride=2), :] = yo_ref[hrows, lanes].astype(F32)
        y = jnp.concatenate([ys_ref[rows, :] for ys_ref in ys_refs], axis=1)
        zin = zin_ref[rows, :].astype(F32)
        o_ref[rows, :] = (x0_ref[rows, :].astype(F32) * (y + zin * db_ref[...])).astype(o_ref.dtype)
        return carry

    lax.fori_loop(0, ln // CONV_CH, body, 0, unroll=min(4, ln // CONV_CH))


def _hyena_out(ye, yo, x0, zin, dbias, ybuf, *, row0, out_col, tc=256):
    nb, ln, c = zin.shape
    rb, cb = row0 // ln, out_col // tc
    return pl.pallas_call(
        functools.partial(_hyena_out_body, ln=ln),
        out_shape=jax.ShapeDtypeStruct(ybuf.shape, ybuf.dtype), grid=(nb, c // tc),
        in_specs=[pl.BlockSpec((None, ln // 2, tc), lambda b, j: (b, 0, j)),
                  pl.BlockSpec((None, ln // 2, tc), lambda b, j: (b, 0, j)),
                  pl.BlockSpec((None, ln, tc), lambda b, j: (b, rb, j)),
                  pl.BlockSpec((None, ln, tc), lambda b, j: (b, 0, j)),
                  pl.BlockSpec((1, tc), lambda b, j: (0, j)),
                  pl.BlockSpec(memory_space=pl.ANY)],
        out_specs=pl.BlockSpec((None, ln, tc), lambda b, j: (b, rb, cb + j)),
        scratch_shapes=[pltpu.VMEM((ln, LANES), F32)] * (tc // LANES),
        input_output_aliases={5: 0}, name="hyena_out",
        compiler_params=_cp(("parallel", "parallel")))(ye, yo, x0, zin, dbias, ybuf)


def _hyena_longconv(zin, ze, zo, x0, hfb, dbias, ybuf, *, row0, tn=1024):
    nb, ln, c = zin.shape
    h = ln // 2
    tm_s = min(512, ln)
    hm = tm_s // 2
    fwd, inv = _dft_matrices(h)
    fwd = _tile_interleave(fwd, 1, hm)
    inv = _tile_interleave(inv * 0.5, 2, hm)
    tw = _twiddle_table(ln, hm)
    tm_f, tk_f = min(1024, ln), min(2048, h)
    ef = _matmul(fwd, hfb[:, :h], tm=tm_f, tn=tn, tk=tk_f, out_dtype=F32, name="hy_filter_dft_even")
    of = _matmul(fwd, hfb[:, h:], tm=tm_f, tn=tn, tk=tk_f, out_dtype=F32, name="hy_filter_dft_odd")
    ka, kb = _hy_kbank(ef, of, tw, tm=tm_s)
    e = _matmul(fwd, ze, tm=tm_f, tn=tn, tk=tk_f, out_dtype=F32, name="hy_dft_even")
    tile = pl.BlockSpec((None, tm_s, tn), lambda j, b, i, k: (b, i, j))
    ktile = pl.BlockSpec((None, tm_s, tn), lambda j, b, i, k: (0, i, j))
    extra = [(e, tile), (tw, pl.BlockSpec((tm_s, LANES), lambda j, b, i, k: (i, 0))), (ka, ktile), (kb, ktile)]
    p, q = _matmul(fwd, zo, tm=tm_s, tn=tn, tk=tk_f, out_dtype=BF16, extra=extra, n_out=2,
                   epilogue=functools.partial(_hy_spectrum_epilogue, hm=hm), name="hy_dft_odd")
    tm_i, tk_i = min(1024, h), min(2048, ln)
    ye = _matmul(inv, p, tm=tm_i, tn=tn, tk=tk_i, out_dtype=BF16, name="hy_idft_even")
    yo = _matmul(inv, q, tm=tm_i, tn=tn, tk=tk_i, out_dtype=BF16, name="hy_idft_odd")
    return _hyena_out(ye, yo, x0, zin, dbias, ybuf, row0=row0, out_col=SSD_WIDTH)


def _mlstm_body(*refs, reverse, final):
    q_ref, k_ref, v_ref, gc_ref, gr_ref = refs[:5]
    if final:
        hf_ref, og_ref, nw_ref, o_ref, ct_ref, n_ref, m_ref = refs[5:]
    else:
        o_ref, ct_ref, n_ref, m_ref = refs[5:]
    nh, dk, dv = ML_HEADS, ML_QK_DIM, ML_V_DIM
    scale = dk ** -0.5

    @pl.when(pl.program_id(1) == 0)
    def _():
        ct_ref[...] = jnp.zeros_like(ct_ref)
        n_ref[...] = jnp.zeros_like(n_ref)
        m_ref[...] = jnp.zeros_like(m_ref)

    gc, gr = gc_ref[...], gr_ref[...]
    li_c, lf_c = gc[:, 0:nh], _log_sigmoid(gc[:, nh:2 * nh])
    li_r, lf_r = gr[0:nh, :], _log_sigmoid(gr[nh:2 * nh, :])
    mask, maskf, mask_tf = _scan_masks(reverse)
    bc_c = _dot(maskf, lf_c, HI)
    bc_r = _dot(lf_r, mask_tf, HI)
    btot_c = jnp.sum(lf_c, axis=0, keepdims=True)
    n_all, m_all = n_ref[...], m_ref[...]
    n_rows, m_rows = [], []

    for h in range(nh):
        qh = q_ref[:, dk * h:dk * (h + 1)]
        kh = k_ref[:, dk * h:dk * (h + 1)]
        vh = v_ref[:, dv * h:dv * (h + 1)]
        m_s = m_all[h:h + 1, 0:1]
        b_c, b_r = bc_c[:, h:h + 1], bc_r[h:h + 1, :]
        i_c, i_r = li_c[:, h:h + 1], li_r[h:h + 1, :]
        btot = btot_c[:, h:h + 1]
        dmat = jnp.where(mask, b_c - b_r + i_r, -jnp.inf)
        inter = b_c + m_s
        m_t = jnp.maximum(inter, jnp.max(dmat, axis=1, keepdims=True))
        s = _dot_nt(qh, kh) * scale * jnp.exp(dmat - m_t)
        dec = jnp.exp(inter - m_t)
        ct = ct_ref[h]
        n_s = n_all[h:h + 1, :]
        num = _dot(s.astype(BF16), vh) + (dec * scale) * _dot(qh, ct.astype(BF16))
        qn = jnp.sum(qh.astype(F32) * n_s, axis=1, keepdims=True) * scale
        den = jnp.sum(s, axis=1, keepdims=True) + dec * qn
        hh = num / jnp.maximum(jnp.abs(den), jnp.exp(-m_t))

        wst_c = btot - b_c + i_c
        wst_r = btot - b_r + i_r
        m_new = jnp.maximum(btot + m_s, jnp.max(wst_r, axis=1, keepdims=True))
        dstate = jnp.exp(btot + m_s - m_new)
        kw = kh.astype(F32) * jnp.exp(wst_c - m_new)
        ct_ref[h] = dstate * ct + _dot(kw.T.astype(BF16), vh)
        n_rows.append(dstate * n_s + jnp.sum(kw, axis=0, keepdims=True))
        m_rows.append(jnp.broadcast_to(m_new, (1, LANES)))

        cols = slice(dv * h, dv * (h + 1))
        if final:
            hs = hh + hf_ref[:, cols]
            hn = hs * lax.rsqrt(jnp.mean(hs * hs, axis=-1, keepdims=True) + EPS) * nw_ref[:, cols]
            o_ref[:, cols] = (hn * _sigmoid(og_ref[:, cols].astype(F32))).astype(o_ref.dtype)
        else:
            o_ref[:, cols] = hh

    n_ref[...] = jnp.concatenate(n_rows, axis=0)
    m_ref[...] = jnp.concatenate(m_rows, axis=0)


def _mlstm(qk, pr, g_c, g_r, *, direction, n_lat, final_args=None, out_width=None):
    nb, t, _ = qk.shape
    q = SCAN_Q
    nch = t // q
    reverse = direction == 1
    cmap = _chunk_map(reverse, nch, n_lat // q)
    d = direction
    in_specs = [
        pl.BlockSpec((None, q, ML_QK_WIDTH), lambda b, k: (b, cmap(k), 0)),
        pl.BlockSpec((None, q, ML_QK_WIDTH), lambda b, k: (b, cmap(k), 1)),
        pl.BlockSpec((None, q, ML_WIDTH), lambda b, k: (b, cmap(k), 2 * ML_QK_WIDTH // ML_WIDTH)),
        pl.BlockSpec((None, None, q, 2 * ML_HEADS), lambda b, k: (b, d, cmap(k), 0)),
        pl.BlockSpec((None, None, 2 * ML_HEADS, q), lambda b, k: (b, d, 0, cmap(k))),
    ]
    args = [qk, qk, pr, g_c, g_r]
    final = final_args is not None
    if final:
        hf, norm_w = final_args
        in_specs += [pl.BlockSpec((None, q, ML_WIDTH), lambda b, k: (b, cmap(k), 0)),
                     pl.BlockSpec((None, q, ML_WIDTH), lambda b, k: (b, cmap(k), 2 * ML_QK_WIDTH // ML_WIDTH + 1)),
                     pl.BlockSpec((1, ML_WIDTH), lambda b, k: (0, 0))]
        args += [hf, pr, norm_w]
        out_shape = jax.ShapeDtypeStruct((nb, t, out_width), BF16)
    else:
        out_shape = jax.ShapeDtypeStruct((nb, t, ML_WIDTH), F32)
    return pl.pallas_call(
        functools.partial(_mlstm_body, reverse=reverse, final=final),
        out_shape=out_shape, grid=(nb, nch), in_specs=in_specs,
        out_specs=pl.BlockSpec((None, q, ML_WIDTH), lambda b, k: (b, cmap(k), 0)),
        scratch_shapes=[pltpu.VMEM((ML_HEADS, ML_QK_DIM, ML_V_DIM), F32), pltpu.VMEM((ML_HEADS, ML_QK_DIM), F32),
                        pltpu.VMEM((ML_HEADS, LANES), F32)],
        name="mlstm_bwd" if reverse else "mlstm_fwd",
        compiler_params=_cp(("parallel", "arbitrary")))(*args)


def _na_window_start(j, rows):
    return jnp.clip(j * NA_QROWS - NA_ROWS // 2, 0, rows - NA_KROWS)


def _na_body(q_ref, k_ref, v_ref, tab_ref, qw_ref, kw_ref, buf_ref, o_ref, kn_ref, *, n_lat, n_ctx):
    del buf_ref
    dh = NA_HEAD_DIM
    scale = dh ** -0.5
    rows = n_lat // GRID_W
    j = pl.program_id(2)
    nq, nk = NA_QROWS * GRID_W, NA_KROWS * GRID_W

    @pl.when(j == 0)
    def _():
        def body(i, carry):
            r = pl.ds(pl.multiple_of(i * LANES, LANES), LANES)
            kk = k_ref[r, :].astype(F32)
            kn = kk * lax.rsqrt(jnp.mean(kk * kk, axis=-1, keepdims=True) + EPS) * kw_ref[...]
            kn_ref[r, :] = kn.astype(kn_ref.dtype)
            return carry

        lax.fori_loop(0, (n_lat + n_ctx) // LANES, body, 0)

    qq = q_ref[...].astype(F32)
    qn = (qq * lax.rsqrt(jnp.mean(qq * qq, axis=-1, keepdims=True) + EPS) * (qw_ref[...] * scale)).astype(BF16)
    start = _na_window_start(j, rows)
    win = pl.ds(pl.multiple_of(start * GRID_W, GRID_W), nk)
    s = _dot_nt(qn, kn_ref[win, :])
    pieces = []
    for qr in range(NA_QROWS):
        d0 = start - (j * NA_QROWS + qr) + NA_ROWS - 1 + NA_QROWS
        pieces.append(s[qr * GRID_W:(qr + 1) * GRID_W] + tab_ref[d0])
    s = jnp.concatenate(pieces, axis=0)
    ctx = pl.ds(n_lat, n_ctx)
    sc = _dot_nt(qn, kn_ref[ctx, :])
    m = jnp.maximum(jnp.max(s, axis=1, keepdims=True), jnp.max(sc, axis=1, keepdims=True))
    p, pc = jnp.exp(s - m), jnp.exp(sc - m)
    denom = jnp.sum(p, axis=1, keepdims=True) + jnp.sum(pc, axis=1, keepdims=True)
    o = _dot(p.astype(BF16), v_ref[win, :]) + _dot(pc.astype(BF16), v_ref[ctx, :])
    o_ref[...] = (o / denom).astype(o_ref.dtype)


def _na_bias_table(rpb, rows):
    nd = 2 * NA_QROWS
    valid = np.zeros((nd, NA_KROWS), bool)
    seen = np.zeros((nd,), bool)
    for j in range(rows // NA_QROWS):
        start = int(np.clip(j * NA_QROWS - NA_ROWS // 2, 0, rows - NA_KROWS))
        for qr in range(NA_QROWS):
            r = j * NA_QROWS + qr
            rs = int(np.clip(r - NA_ROWS // 2, 0, rows - NA_ROWS))
            d = start - r + NA_ROWS - 1 + NA_QROWS
            v = np.array([(rs <= start + i < rs + NA_ROWS) for i in range(NA_KROWS)])
            assert 0 <= d < nd and (not seen[d] or (valid[d] == v).all())
            valid[d], seen[d] = v, True
    col = np.arange(GRID_W)
    cs = np.clip(col - NA_COLS // 2, 0, GRID_W - NA_COLS)
    in_win = (col[None, :] >= cs[:, None]) & (col[None, :] < cs[:, None] + NA_COLS)
    ndr, ndc, w = 2 * NA_ROWS - 1, 2 * NA_COLS - 1, GRID_W
    dr = (np.arange(nd)[:, None] - NA_QROWS) + np.arange(NA_KROWS)[None, :]
    ok = valid[:, None, :, None] & in_win[None, :, None, :] & (dr >= 0)[:, None, :, None] \
        & (dr < ndr)[:, None, :, None]
    ext = jnp.pad(rpb, ((0, 0), (0, 0), (w, w)), mode="edge")
    c0 = w + NA_COLS - 1
    t1 = jnp.stack([ext[:, :, c0 - qc:c0 - qc + w] for qc in range(w)], axis=2)
    t1p = jnp.pad(t1, ((0, 0), (NA_QROWS, nd + NA_KROWS - 1 - NA_QROWS - ndr), (0, 0), (0, 0)))
    wide = jnp.concatenate([t1p[:, m] for m in range(nd + NA_KROWS - 1)], axis=-1)
    vals = jnp.stack([wide[:, :, d * w:(d + NA_KROWS) * w] for d in range(nd)], axis=1)
    return jnp.where(ok.reshape(nd, w, NA_KROWS * w)[None], vals, -jnp.inf).astype(F32)


def _na(pr, tab, q_w, k_w, ybuf, *, n_lat, n_ctx, qcol, out_col):
    nb, t, _ = pr.shape
    dh, nh = NA_HEAD_DIM, NA_HEADS
    nq = NA_QROWS * GRID_W
    qb, ob = qcol // dh, out_col // dh
    nd = tab.shape[1]
    return pl.pallas_call(
        functools.partial(_na_body, n_lat=n_lat, n_ctx=n_ctx),
        out_shape=jax.ShapeDtypeStruct(ybuf.shape, ybuf.dtype), grid=(nb, nh, n_lat // nq),
        in_specs=[pl.BlockSpec((None, nq, dh), lambda b, h, j: (b, j, qb + h)),
                  pl.BlockSpec((None, t, dh), lambda b, h, j: (b, 0, qb + nh + h)),
                  pl.BlockSpec((None, t, dh), lambda b, h, j: (b, 0, qb + 2 * nh + h)),
                  pl.BlockSpec((None, nd, GRID_W, NA_KROWS * GRID_W), lambda b, h, j: (h, 0, 0, 0)),
                  pl.BlockSpec((1, dh), lambda b, h, j: (0, 0)),
                  pl.BlockSpec((1, dh), lambda b, h, j: (0, 0)),
                  pl.BlockSpec(memory_space=pl.ANY)],
        out_specs=pl.BlockSpec((None, nq, dh), lambda b, h, j: (b, j, ob + h)),
        scratch_shapes=[pltpu.VMEM((t, dh), BF16)],
        input_output_aliases={6: 0}, name="na",
        compiler_params=_cp(("parallel", "parallel", "arbitrary")))(pr, pr, pr, tab, q_w, k_w, ybuf)


def _rope_tables(n_lat, n_ctx, width):
    dh = ML_QK_DIM
    nf = dh // 4
    t = jnp.arange(n_lat)
    inv = ROPE_THETA ** (-jnp.arange(nf, dtype=F32) / nf)
    lane = np.arange(dh)
    pos = jnp.where((lane < dh // 2)[None, :], (t // GRID_W)[:, None], (t % GRID_W)[:, None]).astype(F32)
    ang = pos * inv[lane % nf][None, :]
    sign = np.where((lane % (dh // 2)) < nf, -1.0, 1.0).astype(np.float32)
    cos = jnp.concatenate([jnp.cos(ang), jnp.ones((n_ctx, dh), F32)], axis=0)
    sin = jnp.concatenate([jnp.sin(ang) * sign[None, :], jnp.zeros((n_ctx, dh), F32)], axis=0)
    reps = width // dh
    return jnp.tile(cos, (1, reps)), jnp.tile(sin, (1, reps))


def _ada_mod(c, c_ctx, w_all, b_all, layer):
    nb, d = c.shape
    rows = 16
    cv = jnp.concatenate([c, c_ctx[None, :], jnp.zeros((rows - nb - 1, d), F32)], axis=0)
    a = (cv * _sigmoid(cv)).astype(BF16)[None]
    out = _matmul(a, w_all, w_index=layer, tm=rows, tn=1024, tk=d, out_dtype=F32, name="ada_mod")[0]
    out = out[:nb + 1] + b_all[layer][None, :]
    return out.reshape(nb + 1, 6, d).transpose(1, 0, 2)


def _lat_ctx(m, nb):
    ctx = jnp.broadcast_to(m[nb][None, :], (nb, m.shape[1]))
    return jnp.stack([m[:nb], ctx], axis=1)[:, :, None, :]


def _pad_cols(w, n):
    return jnp.pad(w, ((0, 0), (0, n - w.shape[1])))


def _col_row_forms(g, groups):
    nb, t, _ = g.shape
    g5 = g.reshape(nb, t, 2, groups, -1)
    return g5.transpose(0, 2, 3, 1, 4), g5.transpose(0, 2, 3, 4, 1)


ROW_TILES = (1088, 1024, 512, 256)


def _row_tile(rows):
    return next(tm for tm in ROW_TILES if rows % tm == 0)


def _mod_pair(mods, shift, scale, nb):
    return jnp.stack([_lat_ctx(mods[shift], nb), _lat_ctx(mods[scale], nb)], axis=1)


def _out_projection(a, w_all, x, gate, layer, *, rows, n_lat, tk, next_norm, name):
    tm = _row_tile(rows)
    x = _matmul_gated_residual(a, w_all, x, gate, w_index=layer, tm=tm, tn=1024, tk=tk, rows=rows, n_lat=n_lat,
                               name=name)
    if next_norm is None:
        return x, None
    return x, _rmsmod(x, *next_norm, tm=tm, rows=rows, n_lat=n_lat)


def _conv_ffn_block(x, u, mods, w_up_all, conv_w, conv_b, w_down_all, layer, *, rows, n_lat, segs, next_norm):
    nb = x.shape[0]
    up = _matmul(u, w_up_all, w_index=layer, tm=_row_tile(rows), tn=1024, tk=D_MODEL, out_dtype=BF16, name="ffn_up")
    hid = _conv_ffn(up, conv_w, conv_b[None, :], segs=segs, rows=rows)
    return _out_projection(hid, w_down_all, x, _lat_ctx(mods[5], nb), layer, rows=rows, n_lat=n_lat, tk=D_FF // 4,
                           next_norm=next_norm, name="ffn_down")


def _even_layer(x, u, mods, w_in_all, conv_w, conv_b, dt_bias, a_log, d_skip, ssd_norm_w,
                hy_w1, hy_b1, hy_w2, hy_b2, hy_w3, hy_freq, hy_bias, w_out_all, j, *, n_lat, n_ctx, next_norm):
    nb, t, d = x.shape
    segs = ((0, n_lat), (n_lat, n_ctx))
    tm = _row_tile(t)
    n_main = SSD_WIDTH + EV_CONV_CH
    pr = _matmul(u, w_in_all, w_index=j, cols=n_main, tm=tm, tn=1024, tk=d, out_dtype=BF16, name="ev_in")
    w_dt = _pad_cols(w_in_all[j, :, n_main:], LANES).astype(BF16)
    dtp = _matmul(u, w_dt[None], tm=tm, tn=LANES, tk=d, out_dtype=F32, name="ev_in_dt")[:, :, :2 * SSD_HEADS]
    cb = conv_b[None, :]
    xbc = _conv_silu(pr, conv_w, cb, xcol=SSD_WIDTH, wcol=0, width=SSD_XBC, segs=segs)
    x0, zin_l, zin_c, zle, zlo, zce, zco = _conv_hyena(pr, conv_w, cb, xcol=SSD_WIDTH + SSD_XBC, wcol=SSD_XBC,
                                                       segs=segs)

    dt_c, dt_r = _col_row_forms(dtp, SSD_GROUPS)
    db = dt_bias.reshape(2, SSD_GROUPS, 1, SSD_HPG)
    al = a_log.reshape(2, SSD_GROUPS, 1, SSD_HPG)
    scan_args = (xbc, dt_c, dt_r, db, db.transpose(0, 1, 3, 2), al, al.transpose(0, 1, 3, 2))
    yf = _ssd(*scan_args, direction=0, n_lat=n_lat)
    dsk = jnp.repeat(d_skip, SSD_HEAD_DIM)[None, :]
    ybuf = _ssd(*scan_args, direction=1, n_lat=n_lat, final_args=(yf, pr, dsk, ssd_norm_w[None, :]),
                out_width=SSD_WIDTH + HY_WIDTH)

    hyb = hy_bias[None, :]
    filt = functools.partial(_hyena_filters, w1=hy_w1, b1=hy_b1, w2=hy_w2, b2=hy_b2, w3=hy_w3, freq=hy_freq)
    ybuf = _hyena_longconv(zin_l, zle, zlo, x0, filt(n_lat), hyb, ybuf, row0=0)
    ybuf = _hyena_longconv(zin_c, zce, zco, x0, filt(n_ctx), hyb, ybuf, row0=n_lat)
    return _out_projection(ybuf, w_out_all, x, _lat_ctx(mods[2], nb), j, rows=t, n_lat=n_lat, tk=2048,
                           next_norm=next_norm, name="ev_out")


def _odd_layer_latent(x, u, mods, w_in, conv_w, conv_b, gate_b, ml_norm_w, q_norm_w, k_norm_w, rpb,
                      w_out_all, j, *, n_lat, n_ctx, next_norm):
    nb, t, d = x.shape
    segs = ((0, n_lat), (n_lat, n_ctx))
    tm = _row_tile(t)
    o3 = 2 * ML_QK_WIDTH + 2 * ML_WIDTH
    o4 = o3 + 4 * ML_HEADS
    wb = w_in.astype(BF16)
    w_main = jnp.concatenate([wb[:, :o3], wb[:, o4:]], axis=1)
    pr = _matmul(u, w_main[None], tm=tm, tn=1024, tk=d, out_dtype=BF16, name="od_in")
    gates = _matmul(u, _pad_cols(wb[:, o3:o4], LANES)[None], tm=tm, tn=LANES, tk=d, out_dtype=F32,
                    name="od_in_gates")[:, :, :4 * ML_HEADS] + gate_b.reshape(-1)
    cos, sin = _rope_tables(n_lat, n_ctx, 256)
    qk = _conv_rope(pr, conv_w, conv_b[None, :], cos, sin, width=2 * ML_QK_WIDTH, segs=segs)
    g5 = gates.reshape(nb, t, 2, 2 * ML_HEADS)
    g_c, g_r = g5.transpose(0, 2, 1, 3), g5.transpose(0, 2, 3, 1)
    hf = _mlstm(qk, pr, g_c, g_r, direction=0, n_lat=n_lat)
    ybuf = _mlstm(qk, pr, g_c, g_r, direction=1, n_lat=n_lat, final_args=(hf, ml_norm_w[None, :]),
                  out_width=ML_WIDTH + NA_WIDTH)
    tab = _na_bias_table(rpb, n_lat // GRID_W)
    ybuf = _na(pr, tab, q_norm_w[None, :], k_norm_w[None, :], ybuf, n_lat=n_lat, n_ctx=n_ctx, qcol=o3,
               out_col=ML_WIDTH)
    return _out_projection(ybuf, w_out_all, x, _lat_ctx(mods[2], nb), j, rows=n_lat, n_lat=n_lat, tk=2048,
                           next_norm=next_norm, name="od_out")


def kernel(x, c, ctx, c_ctx, ada_w, ada_b, norm_w, ev_w_in, ev_conv_w, ev_conv_b, ssd_dt_bias, ssd_a_log, ssd_d,
           ssd_norm_w, hy_w1, hy_b1, hy_w2, hy_b2, hy_w3, hy_freq, hy_bias, ev_w_out, od_w_in, ml_conv_w, ml_conv_b,
           ml_gate_b, ml_norm_w, na_q_norm_w, na_k_norm_w, na_rpb, od_w_out, ffn_w_up, ffn_conv_w, ffn_conv_b,
           ffn_w_down):
    depth = ada_w.shape[0]
    assert depth == 2, "layer schedule below is written for one even and one odd layer"
    n_lat, n_ctx = x.shape[1], ctx.shape[1]
    t = n_lat + n_ctx
    both = ((0, n_lat), (n_lat, n_ctx))
    xs = jnp.concatenate([x, ctx], axis=1)

    nb = x.shape[0]
    w_down = ffn_w_down.astype(BF16)
    mods0 = _ada_mod(c, c_ctx, ada_w, ada_b, 0)
    mods1 = _ada_mod(c, c_ctx, ada_w, ada_b, 1)
    norm = lambda layer, sub, mods: (norm_w[layer, sub][None, :], _mod_pair(mods, 3 * sub, 3 * sub + 1, nb))

    w0, mod0 = norm(0, 0, mods0)
    u = _rmsmod(xs, w0, mod0, tm=_row_tile(t), rows=t, n_lat=n_lat)
    xs, u = _even_layer(xs, u, mods0, ev_w_in, ev_conv_w[0], ev_conv_b[0], ssd_dt_bias[0], ssd_a_log[0],
                        ssd_d[0], ssd_norm_w[0], hy_w1[0], hy_b1[0], hy_w2[0], hy_b2[0], hy_w3[0], hy_freq[0],
                        hy_bias[0], ev_w_out.astype(BF16), 0, n_lat=n_lat, n_ctx=n_ctx, next_norm=norm(0, 1, mods0))
    xs, u = _conv_ffn_block(xs, u, mods0, ffn_w_up, ffn_conv_w[0], ffn_conv_b[0], w_down, 0,
                            rows=t, n_lat=n_lat, segs=both, next_norm=norm(1, 0, mods1))
    xl, u = _odd_layer_latent(xs, u, mods1, od_w_in[0], ml_conv_w[0], ml_conv_b[0], ml_gate_b[0], ml_norm_w[0],
                              na_q_norm_w[0], na_k_norm_w[0], na_rpb[0], od_w_out.astype(BF16), 0,
                              n_lat=n_lat, n_ctx=n_ctx, next_norm=norm(1, 1, mods1))
    out, _ = _conv_ffn_block(xl, u, mods1, ffn_w_up, ffn_conv_w[1], ffn_conv_b[1], w_down, 1,
                             rows=n_lat, n_lat=n_lat, segs=((0, n_lat),), next_norm=None)
    return out
```

```python
import functools
import math

import numpy as np
import jax
import jax.numpy as jnp
from jax import lax
from jax.experimental import pallas as pl
from jax.experimental.pallas import tpu as pltpu

F32 = jnp.float32
BF16 = jnp.bfloat16
HI = lax.Precision.HIGHEST

D_MODEL = 2048
SEQ = 4096
CTX_LEN = 256
GRID_W = 64
EPS = 1e-6
CONV_W = 3
SSD_WIDTH = D_MODEL
SSD_HEAD_DIM = 64
SSD_HEADS = SSD_WIDTH // SSD_HEAD_DIM
SSD_STATE = 128
SSD_GROUPS = 4
SSD_HPG = SSD_HEADS // SSD_GROUPS
SSD_GW = SSD_WIDTH // SSD_GROUPS
SSD_XBC = SSD_WIDTH + 2 * SSD_GROUPS * SSD_STATE
HY_WIDTH = D_MODEL
HY_EMB = 33
HY_BANDS = (HY_EMB - 1) // 2
HY_ORDER = 64
HY_FAST = 0.3
HY_SLOW = 1.5
HY_TARGET = 1e-2
ML_HEADS = 8
ML_WIDTH = D_MODEL
ML_V_DIM = ML_WIDTH // ML_HEADS
ML_QK_DIM = ML_V_DIM // 2
ML_QK_WIDTH = ML_HEADS * ML_QK_DIM
ROPE_THETA = 10000.0
NA_WIDTH = D_MODEL
NA_HEAD_DIM = 128
NA_HEADS = NA_WIDTH // NA_HEAD_DIM
NA_ROWS = 8
NA_COLS = 16
NA_QROWS = 8
NA_KROWS = 16
NA_PARTS = 2
D_FF = 256 * ((8 * D_MODEL // 3 + 255) // 256)
EV_CONV_CH = SSD_XBC + 3 * HY_WIDTH
SCAN_Q = 128
LANES = 128
SUBLANES = 8
VMEM_LIMIT = 48 * 1024 * 1024


def _cp(sem, vmem=VMEM_LIMIT):
    return pltpu.CompilerParams(dimension_semantics=sem, vmem_limit_bytes=vmem)


def _sigmoid(x):
    return 1.0 / (1.0 + jnp.exp(-x))


def _silu(x):
    return x * _sigmoid(x)


def _softplus(x):
    return jnp.maximum(x, 0.0) + jnp.log(1.0 + jnp.exp(-jnp.abs(x)))


def _log_sigmoid(x):
    return -_softplus(-x)


def _dot(a, b, precision=None):
    return jnp.dot(a, b, preferred_element_type=F32, precision=precision)


def _dot_nt(a, b):
    return lax.dot_general(a, b, (((1,), (1,)), ((), ())), preferred_element_type=F32)


def _split_bf16(x, terms):
    out = []
    for _ in range(terms):
        part = x.astype(BF16)
        out.append(part)
        x = x - part.astype(F32)
    return out


def _dot_sel(x, sel, terms):
    selb = sel.astype(BF16)
    return sum(_dot(part, selb) for part in _split_bf16(x, terms))


def _sel_dot(sel, x, terms):
    selb = sel.astype(BF16)
    return sum(_dot(selb, part) for part in _split_bf16(x, terms))


def _mm_body(*refs, nk, n_extra, epilogue, has_buf, cache_w, n_out):
    a_ref, w_ref = refs[0], refs[1]
    extra = refs[2:2 + n_extra]
    pos = 2 + n_extra + (1 if has_buf else 0)
    o_refs = refs[pos:pos + n_out]
    pos += n_out - 1
    row_blk = pl.program_id(2)
    if cache_w:
        wb_ref = refs[pos + 1]

        @pl.when((pl.program_id(1) == 0) & (row_blk == 0))
        def _():
            wb_ref[...] = w_ref[...].astype(BF16)

        w_val = wb_ref[...]
    else:
        w_val = w_ref[...].astype(BF16)
    part = _dot(a_ref[...].astype(BF16), w_val)

    def finish(acc):
        vals = acc if epilogue is None else epilogue(acc, row_blk, *extra)
        for o_ref, val in zip(o_refs, vals if n_out > 1 else (vals,)):
            o_ref[...] = val.astype(o_ref.dtype)

    if nk == 1:
        finish(part)
    else:
        acc_ref = refs[pos + 1]
        k = pl.program_id(3)

        @pl.when(k == 0)
        def _():
            acc_ref[...] = part

        @pl.when(k > 0)
        def _():
            acc_ref[...] += part

        @pl.when(k == nk - 1)
        def _():
            finish(acc_ref[...])


def _matmul(a, w, *, tm, tn, tk, out_dtype, rows=None, cols=None, w_index=None, extra=(), epilogue=None,
            out_buf=None, out_row_blk=0, out_col_blk=0, n_out=1, name="mm"):
    ba, ta, kdim = a.shape
    bw, kdim2, n = w.shape
    assert kdim == kdim2
    nb = ba if w_index is not None else max(ba, bw)
    rows = ta if rows is None else rows
    n = n if cols is None else cols
    assert rows % tm == 0 and n % tn == 0 and kdim % tk == 0
    ni, nj, nk = rows // tm, n // tn, kdim // tk
    a_map = (lambda j, b, i, k: (b, i, k)) if ba > 1 else (lambda j, b, i, k: (0, i, k))
    if w_index is not None:
        w_map = lambda j, b, i, k: (w_index, k, j)
    else:
        w_map = (lambda j, b, i, k: (b, k, j)) if bw > 1 else (lambda j, b, i, k: (0, k, j))
    cache_w = w.dtype != BF16 and nk == 1 and (w_index is not None or bw == 1)
    in_specs = [pl.BlockSpec((None, tm, tk), a_map), pl.BlockSpec((None, tk, tn), w_map)]
    args = [a, w]
    for arr, spec in extra:
        args.append(arr)
        in_specs.append(spec)
    aliases = {}
    if out_buf is not None:
        aliases = {len(args): 0}
        args.append(out_buf)
        in_specs.append(pl.BlockSpec(memory_space=pl.ANY))
        out_shape = jax.ShapeDtypeStruct(out_buf.shape, out_buf.dtype)
    else:
        out_shape = jax.ShapeDtypeStruct((nb, rows, n), out_dtype)
    out_spec = pl.BlockSpec((None, tm, tn), lambda j, b, i, k: (b, i + out_row_blk, j + out_col_blk))
    if n_out > 1:
        out_shape, out_spec = (out_shape,) * n_out, (out_spec,) * n_out
    scratch = [pltpu.VMEM((tm, tn), F32)] if nk > 1 else []
    if cache_w:
        scratch = [pltpu.VMEM((tk, tn), BF16)]
    body = functools.partial(_mm_body, nk=nk, n_extra=len(extra), epilogue=epilogue,
                             has_buf=out_buf is not None, cache_w=cache_w, n_out=n_out)
    inner = "arbitrary" if cache_w else "parallel"
    return pl.pallas_call(
        body, out_shape=out_shape, grid=(nj, nb, ni, nk), in_specs=in_specs, out_specs=out_spec,
        scratch_shapes=scratch, input_output_aliases=aliases, name=name,
        compiler_params=_cp(("parallel", inner, inner, "arbitrary")))(*args)


def _row_is_lat(row_blk, tm, n_lat):
    rows = row_blk * tm + lax.broadcasted_iota(jnp.int32, (tm, 1), 0)
    return rows < n_lat


def _gated_residual_epilogue(acc, row_blk, res_ref, gate_ref, *, tm, n_lat):
    gate = jnp.where(_row_is_lat(row_blk, tm, n_lat), gate_ref[0], gate_ref[1])
    return res_ref[...] + gate * acc


def _modulated_rmsnorm(x, is_lat, w_ref, mod_ref):
    y = x * lax.rsqrt(jnp.mean(x * x, axis=-1, keepdims=True) + EPS) * w_ref[...]
    shift = jnp.where(is_lat, mod_ref[0, 0], mod_ref[0, 1])
    scale = jnp.where(is_lat, mod_ref[1, 0], mod_ref[1, 1])
    return y * (1.0 + scale) + shift


def _matmul_gated_residual(a, w, res, gate, *, w_index, tm, tn, tk, rows, n_lat, name):
    extra = [(res, pl.BlockSpec((None, tm, tn), lambda j, b, i, k: (b, i, j))),
             (gate, pl.BlockSpec((None, 2, 1, tn), lambda j, b, i, k: (b, 0, 0, j)))]
    epi = functools.partial(_gated_residual_epilogue, tm=tm, n_lat=n_lat)
    return _matmul(a, w, w_index=w_index, tm=tm, tn=tn, tk=tk, out_dtype=F32, rows=rows, extra=extra,
                   epilogue=epi, name=name)


def _rmsmod_body(x_ref, w_ref, mod_ref, o_ref, *, tm, n_lat):
    is_lat = _row_is_lat(pl.program_id(2), tm, n_lat)
    o_ref[...] = _modulated_rmsnorm(x_ref[...], is_lat, w_ref, mod_ref).astype(o_ref.dtype)


def _rmsmod(x, w, mod, *, tm, rows, n_lat):
    nb, _, d = x.shape
    tm = tm // 4
    assert rows % tm == 0 and tm % 16 == 0
    body = functools.partial(_rmsmod_body, tm=tm, n_lat=n_lat)
    return pl.pallas_call(
        body, out_shape=jax.ShapeDtypeStruct((nb, rows, d), BF16), grid=(1, nb, rows // tm),
        in_specs=[pl.BlockSpec((None, tm, d), lambda j, b, i: (b, i, 0)),
                  pl.BlockSpec((1, d), lambda j, b, i: (0, 0)),
                  pl.BlockSpec((None, 2, 2, 1, d), lambda j, b, i: (b, 0, 0, 0, 0))],
        out_specs=pl.BlockSpec((None, tm, d), lambda j, b, i: (b, i, 0)),
        name="rmsmod", compiler_params=_cp(("parallel", "parallel", "parallel")))(x, w, mod)


CONV_CH = 128
CONV_HALO = 16


def _shift_matrices():
    r = lax.broadcasted_iota(jnp.int32, (CONV_CH, CONV_CH + 2 * CONV_HALO), 0)
    c = lax.broadcasted_iota(jnp.int32, (CONV_CH, CONV_CH + 2 * CONV_HALO), 1)
    return (c == r + CONV_HALO - 1).astype(BF16), (c == r + CONV_HALO + 1).astype(BF16)


def _conv_at(x_ref, w_ref, b_ref, shifts, r0, n, i):
    t0 = pl.multiple_of(i * CONV_CH, CONV_CH)
    cur = x_ref[pl.ds(r0 + t0, CONV_CH), :]
    lo = pl.multiple_of(jnp.maximum(t0 - CONV_HALO, 0), CONV_HALO)
    hi = pl.multiple_of(jnp.minimum(t0 + CONV_CH, n - CONV_HALO), CONV_HALO)
    zero = jnp.zeros((CONV_HALO, cur.shape[1]), cur.dtype)
    tail = jnp.where(i > 0, x_ref[pl.ds(r0 + lo, CONV_HALO), :], zero)
    head = jnp.where(i < n // CONV_CH - 1, x_ref[pl.ds(r0 + hi, CONV_HALO), :], zero)
    win = jnp.concatenate([tail, cur, head], axis=0)
    return (w_ref[0:1, :] * _dot(shifts[0], win) + w_ref[1:2, :] * cur.astype(F32)
            + w_ref[2:3, :] * _dot(shifts[1], win) + b_ref[...])


def _conv_loop(segs, fn):
    for si, (r0, n) in enumerate(segs):
        def body(i, carry, r0=r0, n=n, si=si):
            fn(si, r0, n, i)
            return carry

        lax.fori_loop(0, n // CONV_CH, body, 0, unroll=min(4, n // CONV_CH))


def _conv_silu_body(x_ref, w_ref, b_ref, o_ref, *, segs):
    shifts = _shift_matrices()

    def fn(si, r0, n, i):
        y = _silu(_conv_at(x_ref, w_ref, b_ref, shifts, r0, n, i))
        o_ref[pl.ds(r0 + pl.multiple_of(i * CONV_CH, CONV_CH), CONV_CH), :] = y.astype(o_ref.dtype)

    _conv_loop(segs, fn)


def _conv_rope_body(x_ref, w_ref, b_ref, cos_ref, sin_ref, o_ref, *, segs, tc):
    shifts = _shift_matrices()
    lane = lax.broadcasted_iota(jnp.int32, (CONV_CH, tc), 1)
    first = (lane % (ML_QK_DIM // 2)) < (ML_QK_DIM // 4)
    nf = ML_QK_DIM // 4

    def fn(si, r0, n, i):
        y = _silu(_conv_at(x_ref, w_ref, b_ref, shifts, r0, n, i))
        partner = jnp.where(first, pltpu.roll(y, tc - nf, 1), pltpu.roll(y, nf, 1))
        rows = pl.ds(r0 + pl.multiple_of(i * CONV_CH, CONV_CH), CONV_CH)
        o_ref[rows, :] = (y * cos_ref[rows, :] + partner * sin_ref[rows, :]).astype(o_ref.dtype)

    _conv_loop(segs, fn)


def _conv_ffn_body(a_ref, x_ref, w_ref, b_ref, o_ref, *, segs):
    shifts = _shift_matrices()

    def fn(si, r0, n, i):
        rows = pl.ds(r0 + pl.multiple_of(i * CONV_CH, CONV_CH), CONV_CH)
        g = _silu(_conv_at(x_ref, w_ref, b_ref, shifts, r0, n, i))
        o_ref[rows, :] = (a_ref[rows, :].astype(F32) * g).astype(o_ref.dtype)

    _conv_loop(segs, fn)


def _conv_hyena_body(x0_ref, x1_ref, v_ref, w0_ref, b0_ref, w1_ref, b1_ref, w2_ref, b2_ref,
                     ox0_ref, ozl_ref, ozc_ref, ozle_ref, ozlo_ref, ozce_ref, ozco_ref, *zs_refs, segs):
    shifts = _shift_matrices()
    half = CONV_CH // 2
    outs = ((ozl_ref, ozle_ref, ozlo_ref), (ozc_ref, ozce_ref, ozco_ref))

    def fn(si, r0, n, i):
        t0 = pl.multiple_of(i * CONV_CH, CONV_CH)
        x0 = _conv_at(x0_ref, w0_ref, b0_ref, shifts, r0, n, i)
        ox0_ref[pl.ds(r0 + t0, CONV_CH), :] = x0.astype(ox0_ref.dtype)
        zin = _conv_at(x1_ref, w1_ref, b1_ref, shifts, r0, n, i) * _conv_at(v_ref, w2_ref, b2_ref, shifts, r0, n, i)
        nat_ref, even_ref, odd_ref = outs[si]
        nat_ref[pl.ds(t0, CONV_CH), :] = zin.astype(BF16)
        hrows = pl.ds(pl.multiple_of(i * half, half), half)
        for g, zs_ref in enumerate(zs_refs):
            lanes = slice(g * LANES, (g + 1) * LANES)
            zs_ref[pl.ds(r0 + t0, CONV_CH), :] = zin[:, lanes]
            even_ref[hrows, lanes] = zs_ref[pl.ds(r0 + t0, half, stride=2), :].astype(BF16)
            odd_ref[hrows, lanes] = zs_ref[pl.ds(r0 + t0 + 1, half, stride=2), :].astype(BF16)

    _conv_loop(segs, fn)


def _conv_specs(t_in, tc, xcol_blk, wcol_blk):
    return [pl.BlockSpec((None, t_in, tc), lambda b, j: (b, 0, j + xcol_blk)),
            pl.BlockSpec((CONV_W, tc), lambda b, j: (0, j + wcol_blk)),
            pl.BlockSpec((1, tc), lambda b, j: (0, j + wcol_blk))]


def _conv_silu(x, w, b, *, xcol, wcol, width, segs, tc=256):
    nb, t_in, _ = x.shape
    return pl.pallas_call(
        functools.partial(_conv_silu_body, segs=segs),
        out_shape=jax.ShapeDtypeStruct((nb, t_in, width), BF16), grid=(nb, width // tc),
        in_specs=_conv_specs(t_in, tc, xcol // tc, wcol // tc),
        out_specs=pl.BlockSpec((None, t_in, tc), lambda b, j: (b, 0, j)),
        name="conv_silu", compiler_params=_cp(("parallel", "parallel")))(x, w, b)


def _conv_rope(x, w, b, cos, sin, *, width, segs, tc=256):
    nb, t_in, _ = x.shape
    tab = pl.BlockSpec((t_in, tc), lambda b, j: (0, 0))
    return pl.pallas_call(
        functools.partial(_conv_rope_body, segs=segs, tc=tc),
        out_shape=jax.ShapeDtypeStruct((nb, t_in, width), BF16), grid=(nb, width // tc),
        in_specs=_conv_specs(t_in, tc, 0, 0) + [tab, tab],
        out_specs=pl.BlockSpec((None, t_in, tc), lambda b, j: (b, 0, j)),
        name="conv_rope", compiler_params=_cp(("parallel", "parallel")))(x, w, b, cos, sin)


def _conv_ffn(up, w, b, *, segs, rows, tc=256):
    nb, t_in, two_ff = up.shape
    ff = two_ff // 2
    return pl.pallas_call(
        functools.partial(_conv_ffn_body, segs=segs),
        out_shape=jax.ShapeDtypeStruct((nb, rows, ff), BF16), grid=(nb, ff // tc),
        in_specs=[pl.BlockSpec((None, rows, tc), lambda b, j: (b, 0, j))] + _conv_specs(rows, tc, ff // tc, 0),
        out_specs=pl.BlockSpec((None, rows, tc), lambda b, j: (b, 0, j)),
        name="conv_ffn", compiler_params=_cp(("parallel", "parallel")))(up, up, w, b)


def _conv_hyena(pr, w, b, *, xcol, wcol, segs, tc=256):
    nb, t_in, _ = pr.shape
    (_, n_lat), (_, n_ctx) = segs
    wd = HY_WIDTH
    out_rows = (t_in, n_lat, n_ctx, n_lat // 2, n_lat // 2, n_ctx // 2, n_ctx // 2)
    in_specs = [pl.BlockSpec((None, t_in, tc), functools.partial(lambda b, j, o: (b, 0, j + o), o=(xcol + m * wd) // tc))
                for m in range(3)]
    wargs = []
    for m in range(3):
        off = (wcol + m * wd) // tc
        in_specs.append(pl.BlockSpec((CONV_W, tc), functools.partial(lambda b, j, o: (0, j + o), o=off)))
        in_specs.append(pl.BlockSpec((1, tc), functools.partial(lambda b, j, o: (0, j + o), o=off)))
        wargs += [w, b]
    return pl.pallas_call(
        functools.partial(_conv_hyena_body, segs=segs),
        out_shape=tuple(jax.ShapeDtypeStruct((nb, rows, wd), BF16) for rows in out_rows),
        grid=(nb, wd // tc), in_specs=in_specs,
        out_specs=tuple(pl.BlockSpec((None, rows, tc), lambda b, j: (b, 0, j)) for rows in out_rows),
        scratch_shapes=[pltpu.VMEM((t_in, LANES), F32)] * (tc // LANES),
        name="conv_hyena", compiler_params=_cp(("parallel", "parallel")))(pr, pr, pr, *wargs)


def _scan_masks(reverse):
    q = SCAN_Q
    ii = lax.broadcasted_iota(jnp.int32, (q, q), 0)
    jj = lax.broadcasted_iota(jnp.int32, (q, q), 1)
    mask = (jj >= ii) if reverse else (jj <= ii)
    mask_t = (jj <= ii) if reverse else (jj >= ii)
    return mask, mask.astype(F32), mask_t.astype(F32)


def _chunk_map(reverse, n_chunks, n_lat_chunks):
    if reverse:
        return lambda k: n_chunks - 1 - k
    return lambda k: (k + n_lat_chunks) % n_chunks


SSD_GPS = 4


def _ssd_body(*refs, reverse, final):
    (xs_ref, b_ref, c_ref, dtc_ref, dtr_ref, dbc_ref, dbr_ref, alc_ref, alr_ref) = refs[:9]
    if final:
        yf_ref, z_ref, dsk_ref, nw_ref, o_ref, h_ref = refs[9:]
    else:
        o_ref, h_ref = refs[9:]

    @pl.when(pl.program_id(2) == 0)
    def _():
        h_ref[...] = jnp.zeros_like(h_ref)

    q, gw, hd = SCAN_Q, SSD_GW, SSD_HEAD_DIM
    mask, maskf, mask_tf = _scan_masks(reverse)
    groups = range(SSD_GPS)
    pairs = range(SSD_HPG // 2)
    cols = [slice(g * gw, (g + 1) * gw) for g in groups]
    xs = [xs_ref[:, cols[g]].astype(F32) for g in groups]
    bm = [b_ref[:, g * SSD_STATE:(g + 1) * SSD_STATE] for g in groups]
    cm = [c_ref[:, g * SSD_STATE:(g + 1) * SSD_STATE] for g in groups]
    h_prev = [h_ref[g] for g in groups]
    dt_c = [_softplus(dtc_ref[g] + dbc_ref[g]) for g in groups]
    dt_r = [_softplus(dtr_ref[g] + dbr_ref[g]) for g in groups]
    da_c = [dt_c[g] * (-jnp.exp(alc_ref[g])) for g in groups]
    da_r = [dt_r[g] * (-jnp.exp(alr_ref[g])) for g in groups]
    acum_c = [_sel_dot(maskf, da_c[g], 3) for g in groups]
    acum_r = [_dot_sel(da_r[g], mask_tf, 3) for g in groups]
    atot = [jnp.sum(da_c[g], axis=0, keepdims=True) for g in groups]

    er = lax.broadcasted_iota(jnp.int32, (SSD_HPG, gw), 0)
    ec = lax.broadcasted_iota(jnp.int32, (SSD_HPG, gw), 1)
    expand = (ec // hd == er).astype(F32)
    wide = [_dot_sel(jnp.concatenate([dt_c[g], jnp.exp(atot[g] - acum_c[g]), jnp.exp(acum_c[g]),
                                      jnp.broadcast_to(jnp.exp(atot[g]), (SUBLANES, SSD_HPG))], axis=0), expand, 1)
            for g in groups]
    xq = [xs[g] * wide[g][0:q] for g in groups]
    cb = [_dot_nt(cm[g], bm[g]) for g in groups]
    y_off = [wide[g][2 * q:3 * q] * _dot(cm[g], h_prev[g].astype(BF16)) for g in groups]
    lane = lax.broadcasted_iota(jnp.int32, (q, 2 * hd), 1)

    def pair_lhs(g, p):
        ms = [(cb[g] * jnp.where(mask, jnp.exp(acum_c[g][:, r:r + 1] - acum_r[g][r:r + 1, :]), 0.0)).astype(BF16)
              for r in (2 * p, 2 * p + 1)]
        return jnp.concatenate(ms, axis=1)

    def pair_rhs(g, p):
        xp = xq[g][:, 2 * hd * p:2 * hd * (p + 1)]
        return jnp.concatenate([jnp.where(lane < hd, xp, 0.0), jnp.where(lane >= hd, xp, 0.0)], axis=0).astype(BF16)

    y_in = [[_dot(pair_lhs(g, p), pair_rhs(g, p)) for p in pairs] for g in groups]
    y = [y_off[g] + jnp.concatenate(y_in[g], axis=1) for g in groups]
    bt = [bm[g].astype(F32).T.astype(BF16) for g in groups]
    upd = [_dot(bt[g], (xq[g] * wide[g][q:2 * q]).astype(BF16)) for g in groups]
    for g in groups:
        h_ref[g] = wide[g][3 * q:3 * q + 1] * h_prev[g] + upd[g]

    for g in groups:
        if final:
            ytot = y[g] + yf_ref[:, cols[g]] + xs[g] * dsk_ref[:, cols[g]]
            gated = ytot * _silu(z_ref[:, cols[g]].astype(F32))
            o = gated * lax.rsqrt(jnp.mean(gated * gated, axis=-1, keepdims=True) + EPS) * nw_ref[:, cols[g]]
            o_ref[:, cols[g]] = o.astype(o_ref.dtype)
        else:
            o_ref[:, cols[g]] = y[g]


def _ssd(xbc, dt_c, dt_r, db_c, db_r, al_c, al_r, *, direction, n_lat, final_args=None, out_width=None):
    nb, t, _ = xbc.shape
    q, gw, hpg = SCAN_Q, SSD_GW, SSD_HPG
    nch = t // q
    reverse = direction == 1
    cmap = _chunk_map(reverse, nch, n_lat // q)
    gps = SSD_GPS
    sw, gww = gps * SSD_STATE, gps * gw
    b0 = SSD_WIDTH // sw
    c0 = b0 + SSD_GROUPS // gps
    d = direction
    in_specs = [
        pl.BlockSpec((None, q, gww), lambda b, g, k: (b, cmap(k), g)),
        pl.BlockSpec((None, q, sw), lambda b, g, k: (b, cmap(k), b0 + g)),
        pl.BlockSpec((None, q, sw), lambda b, g, k: (b, cmap(k), c0 + g)),
        pl.BlockSpec((None, None, gps, q, hpg), lambda b, g, k: (b, d, g, cmap(k), 0)),
        pl.BlockSpec((None, None, gps, hpg, q), lambda b, g, k: (b, d, g, 0, cmap(k))),
        pl.BlockSpec((None, gps, 1, hpg), lambda b, g, k: (d, g, 0, 0)),
        pl.BlockSpec((None, gps, hpg, 1), lambda b, g, k: (d, g, 0, 0)),
        pl.BlockSpec((None, gps, 1, hpg), lambda b, g, k: (d, g, 0, 0)),
        pl.BlockSpec((None, gps, hpg, 1), lambda b, g, k: (d, g, 0, 0)),
    ]
    args = [xbc, xbc, xbc, dt_c, dt_r, db_c, db_r, al_c, al_r]
    final = final_args is not None
    if final:
        yf, pr, dskip, norm_w = final_args
        in_specs += [pl.BlockSpec((None, q, gww), lambda b, g, k: (b, cmap(k), g)),
                     pl.BlockSpec((None, q, gww), lambda b, g, k: (b, cmap(k), g)),
                     pl.BlockSpec((1, gww), lambda b, g, k: (0, g)),
                     pl.BlockSpec((1, gww), lambda b, g, k: (0, g))]
        args += [yf, pr, dskip, norm_w]
        out_shape = jax.ShapeDtypeStruct((nb, t, out_width), BF16)
    else:
        out_shape = jax.ShapeDtypeStruct((nb, t, SSD_WIDTH), F32)
    return pl.pallas_call(
        functools.partial(_ssd_body, reverse=reverse, final=final),
        out_shape=out_shape, grid=(nb, SSD_GROUPS // gps, nch), in_specs=in_specs,
        out_specs=pl.BlockSpec((None, q, gww), lambda b, g, k: (b, cmap(k), g)),
        scratch_shapes=[pltpu.VMEM((gps, SSD_STATE, gw), F32)],
        name="ssd_bwd" if reverse else "ssd_fwd",
        compiler_params=_cp(("parallel", "parallel", "arbitrary")))(*args)


def _hyfilt_body(f_ref, w1_ref, b1_ref, w2_ref, b2_ref, w3_ref, fr_ref, dl_ref, o_ref, *, tl):
    f = f_ref[...]
    h = jnp.sin(fr_ref[0:1, :] * (_dot(f, w1_ref[...], HI) + b1_ref[...]))
    h = jnp.sin(fr_ref[1:2, :] * (_dot(h, w2_ref[...], HI) + b2_ref[...]))
    h = _dot(h, w3_ref[...], HI)
    dec = jnp.exp(-f[:, 0:1] * dl_ref[...])
    row = pl.program_id(0) * tl + lax.broadcasted_iota(jnp.int32, (tl, 1), 0)
    wd = HY_WIDTH
    o_ref[:, 0:wd] = (h[:, 0:wd] * dec).astype(o_ref.dtype)
    o_ref[:, wd:2 * wd] = jnp.where(row == 0, 0.0, h[:, wd:2 * wd] * dec).astype(o_ref.dtype)


def _hyena_filters(length, w1, b1, w2, b2, w3, freq, *, tl=256):
    p = LANES
    t = jnp.linspace(0.0, 1.0, length, dtype=F32)[:, None]
    w = 2.0 * math.pi * jnp.arange(length, dtype=F32)[:, None] / length
    f = jnp.linspace(1e-4, HY_BANDS - 1, HY_BANDS, dtype=F32)[None, :]
    feats = jnp.concatenate([t, jnp.cos(f * w), -jnp.sin(f * w)], axis=-1)
    feats = jnp.concatenate([feats[0::2], feats[1::2]], axis=0)
    feats = jnp.pad(feats, ((0, 0), (0, p - HY_EMB)))
    po = p - HY_ORDER
    w1p = jnp.pad(w1, ((0, p - HY_EMB), (0, po)))
    w2p = jnp.pad(w2, ((0, po), (0, po)))
    w3p = jnp.pad(w3, ((0, po), (0, 0)))
    b1p = jnp.pad(b1[None, :], ((0, 0), (0, po)))
    b2p = jnp.pad(b2[None, :], ((0, 0), (0, po)))
    frp = jnp.pad(freq, ((0, 0), (0, po)))
    deltas = jnp.abs(jnp.linspace(math.log(HY_TARGET) / HY_FAST, math.log(HY_TARGET) / HY_SLOW, HY_WIDTH,
                                  dtype=F32))[None, :]
    full = lambda shape: pl.BlockSpec(shape, lambda i: (0,) * len(shape))
    out = pl.pallas_call(
        functools.partial(_hyfilt_body, tl=tl),
        out_shape=jax.ShapeDtypeStruct((length, 2 * HY_WIDTH), BF16), grid=(length // tl,),
        in_specs=[pl.BlockSpec((tl, p), lambda i: (i, 0)), full((p, p)), full((1, p)), full((p, p)), full((1, p)),
                  full((p, 2 * HY_WIDTH)), full((2, p)), full((1, HY_WIDTH))],
        out_specs=pl.BlockSpec((tl, 2 * HY_WIDTH), lambda i: (i, 0)),
        name="hyena_filters", compiler_params=_cp(("parallel",)))(feats, w1p, b1p, w2p, b2p, w3p, frp, deltas)
    return out[None]


def _dft_matrices(length):
    n = 2 * length
    k = jnp.arange(length, dtype=jnp.int32)[:, None]
    t = jnp.arange(length, dtype=jnp.int32)[None, :]
    step = 1 << (int(math.log2(length)) // 2)
    part = jnp.arange(length // step, dtype=jnp.int32)[:, None]
    ang_hi = ((part * step * t) % n).astype(F32) * (2.0 * math.pi / n)
    ang_lo = ((jnp.arange(step, dtype=jnp.int32)[:, None] * t) % n).astype(F32) * (2.0 * math.pi / n)
    ch, sh = jnp.cos(ang_hi)[:, None, :], jnp.sin(ang_hi)[:, None, :]
    cl, sl = jnp.cos(ang_lo)[None, :, :], jnp.sin(ang_lo)[None, :, :]
    c = (ch * cl - sh * sl).reshape(length, length)
    s = (sh * cl + ch * sl).reshape(length, length)
    alt_t = jnp.where(t % 2 == 0, 1.0, -1.0).astype(F32)
    fwd = jnp.concatenate([c, jnp.where(k == 0, alt_t, -s)], axis=0)
    alt_r = jnp.where(k % 2 == 0, 1.0, -1.0).astype(F32)
    inv_re = jnp.where(t == 0, 1.0 / n, (2.0 / n) * c)
    inv_im = jnp.where(t == 0, alt_r / n, (-2.0 / n) * s)
    inv = jnp.concatenate([inv_re, inv_im], axis=1)
    return fwd.astype(BF16)[None], inv.astype(BF16)[None]


def _tile_interleave(x, axis, hm):
    shp = x.shape
    n = shp[axis] // 2
    x = x.reshape(shp[:axis] + (2, n // hm, hm) + shp[axis + 1:])
    return jnp.swapaxes(x, axis, axis + 1).reshape(shp)


def _twiddle_table(length, hm):
    k = np.arange(length // 2, dtype=np.float64)[:, None] * (np.pi / length)
    tab = np.concatenate([np.cos(k), np.sin(k)], axis=0)
    tab = tab.reshape(2, -1, hm, 1).swapaxes(0, 1).reshape(length, 1)
    return jnp.asarray(np.broadcast_to(tab, (length, LANES)).astype(np.float32))


def _lanes(x, width):
    return jnp.tile(x, (1, width // LANES))


def _parity_banks(e, ot, tw, hm):
    er, ei, otr, oti = e[0:hm], e[hm:], ot[0:hm], ot[hm:]
    c, s = _lanes(tw[0:hm], e.shape[1]), _lanes(tw[hm:], e.shape[1])
    orr = otr * c + oti * s
    oi = oti * c - otr * s
    return (er + orr, ei + oi, er - orr, ei - oi), (c, s)


def _is_bin0(row_blk, hm):
    return (row_blk == 0) & (lax.broadcasted_iota(jnp.int32, (hm, 1), 0) == 0)


def _hy_kbank_body(ef_ref, eb_ref, of_ref, ob_ref, tw_ref, ka_ref, kb_ref, *, hm):
    tw = tw_ref[...]
    (afr, afi, bfr, bfi), _ = _parity_banks(ef_ref[...], of_ref[...], tw, hm)
    (abr, abi, bbr, bbi), _ = _parity_banks(eb_ref[...], ob_ref[...], tw, hm)
    first = _is_bin0(pl.program_id(0), hm)
    kai = jnp.where(first, ef_ref[hm:, :] + eb_ref[hm:, :], afi - abi)
    kbi = jnp.where(first, ob_ref[hm:, :] - of_ref[hm:, :], bfi - bbi)
    ka_ref[...] = jnp.concatenate([afr + abr, kai], axis=0)
    kb_ref[...] = jnp.concatenate([bfr + bbr, kbi], axis=0)


def _hy_kbank(ef, of, tw, *, tm, tc=512):
    _, ln, c2 = ef.shape
    c = c2 // 2
    fspec = pl.BlockSpec((None, tm, tc), lambda i, j: (0, i, j))
    bspec = pl.BlockSpec((None, tm, tc), lambda i, j: (0, i, j + c // tc))
    return pl.pallas_call(
        functools.partial(_hy_kbank_body, hm=tm // 2),
        out_shape=(jax.ShapeDtypeStruct((1, ln, c), F32),) * 2, grid=(ln // tm, c // tc),
        in_specs=[fspec, bspec, fspec, bspec, pl.BlockSpec((tm, LANES), lambda i, j: (i, 0))],
        out_specs=(fspec, fspec), name="hy_kbank",
        compiler_params=_cp(("parallel", "parallel")))(ef, ef, of, of, tw)


def _hy_spectrum_epilogue(acc, row_blk, e_ref, tw_ref, ka_ref, kb_ref, *, hm):
    e = e_ref[...]
    (ar, ai, br, bi), (c, s) = _parity_banks(e, acc, tw_ref[...], hm)
    kar, kai, kbr, kbi = ka_ref[0:hm], ka_ref[hm:], kb_ref[0:hm], kb_ref[hm:]
    yar, yai = ar * kar - ai * kai, ar * kai + ai * kar
    ybr, ybi = br * kbr - bi * kbi, br * kbi + bi * kbr
    qr, qi = yar - ybr, yai - ybi
    first = _is_bin0(row_blk, hm)
    y0, yl = ar * kar, br * kbr
    amr, ami = e[hm:], -acc[hm:]
    ymr, ymi = amr * kai - ami * kbi, amr * kbi + ami * kai
    pr = jnp.where(first, y0 + yl, yar + ybr)
    pi = jnp.where(first, 2.0 * ymr, yai + ybi)
    qpr = jnp.where(first, y0 - yl, qr * c - qi * s)
    qpi = jnp.where(first, -2.0 * ymi, qi * c + qr * s)
    return jnp.concatenate([pr, pi], axis=0), jnp.concatenate([qpr, qpi], axis=0)


def _hyena_out_body(ye_ref, yo_ref, x0_ref, zin_ref, db_ref, buf_ref, o_ref, *ys_refs, ln):
    del buf_ref
    half = CONV_CH // 2

    def body(i, carry):
        t0 = pl.multiple_of(i * CONV_CH, CONV_CH)
        rows = pl.ds(t0, CONV_CH)
        hrows = pl.ds(pl.multiple_of(i * half, half), half)
        for g, ys_ref in enumerate(ys_refs):
            lanes = slice(g * LANES, (g + 1) * LANES)
            ys_ref[pl.ds(t0, half, stride=2), :] = ye_ref[hrows, lanes].astype(F32)
            ys_ref[pl.ds(t0 + 1, half, stride=2), :] = yo_ref[hrows, lanes].astype(F32)
        y = jnp.concatenate([ys_ref[rows, :] for ys_ref in ys_refs], axis=1)
        zin = zin_ref[rows, :].astype(F32)
        o_ref[rows, :] = (x0_ref[rows, :].astype(F32) * (y + zin * db_ref[...])).astype(o_ref.dtype)
        return carry

    lax.fori_loop(0, ln // CONV_CH, body, 0, unroll=min(4, ln // CONV_CH))


def _hyena_out(ye, yo, x0, zin, dbias, ybuf, *, row0, out_col, tc=256):
    nb, ln, c = zin.shape
    rb, cb = row0 // ln, out_col // tc
    return pl.pallas_call(
        functools.partial(_hyena_out_body, ln=ln),
        out_shape=jax.ShapeDtypeStruct(ybuf.shape, ybuf.dtype), grid=(nb, c // tc),
        in_specs=[pl.BlockSpec((None, ln // 2, tc), lambda b, j: (b, 0, j)),
                  pl.BlockSpec((None, ln // 2, tc), lambda b, j: (b, 0, j)),
                  pl.BlockSpec((None, ln, tc), lambda b, j: (b, rb, j)),
                  pl.BlockSpec((None, ln, tc), lambda b, j: (b, 0, j)),
                  pl.BlockSpec((1, tc), lambda b, j: (0, j)),
                  pl.BlockSpec(memory_space=pl.ANY)],
        out_specs=pl.BlockSpec((None, ln, tc), lambda b, j: (b, rb, cb + j)),
        scratch_shapes=[pltpu.VMEM((ln, LANES), F32)] * (tc // LANES),
        input_output_aliases={5: 0}, name="hyena_out",
        compiler_params=_cp(("parallel", "parallel")))(ye, yo, x0, zin, dbias, ybuf)


def _hyena_longconv(zin, ze, zo, x0, hfb, dbias, ybuf, *, row0, tn=1024):
    nb, ln, c = zin.shape
    h = ln // 2
    tm_s = min(512, ln)
    hm = tm_s // 2
    fwd, inv = _dft_matrices(h)
    fwd = _tile_interleave(fwd, 1, hm)
    inv = _tile_interleave(inv * 0.5, 2, hm)
    tw = _twiddle_table(ln, hm)
    tm_f, tk_f = min(1024, ln), min(2048, h)
    ef = _matmul(fwd, hfb[:, :h], tm=tm_f, tn=tn, tk=tk_f, out_dtype=F32, name="hy_filter_dft_even")
    of = _matmul(fwd, hfb[:, h:], tm=tm_f, tn=tn, tk=tk_f, out_dtype=F32, name="hy_filter_dft_odd")
    ka, kb = _hy_kbank(ef, of, tw, tm=tm_s)
    e = _matmul(fwd, ze, tm=tm_f, tn=tn, tk=tk_f, out_dtype=F32, name="hy_dft_even")
    tile = pl.BlockSpec((None, tm_s, tn), lambda j, b, i, k: (b, i, j))
    ktile = pl.BlockSpec((None, tm_s, tn), lambda j, b, i, k: (0, i, j))
    extra = [(e, tile), (tw, pl.BlockSpec((tm_s, LANES), lambda j, b, i, k: (i, 0))), (ka, ktile), (kb, ktile)]
    p, q = _matmul(fwd, zo, tm=tm_s, tn=tn, tk=tk_f, out_dtype=BF16, extra=extra, n_out=2,
                   epilogue=functools.partial(_hy_spectrum_epilogue, hm=hm), name="hy_dft_odd")
    tm_i, tk_i = min(1024, h), min(2048, ln)
    ye = _matmul(inv, p, tm=tm_i, tn=tn, tk=tk_i, out_dtype=BF16, name="hy_idft_even")
    yo = _matmul(inv, q, tm=tm_i, tn=tn, tk=tk_i, out_dtype=BF16, name="hy_idft_odd")
    return _hyena_out(ye, yo, x0, zin, dbias, ybuf, row0=row0, out_col=SSD_WIDTH)


def _mlstm_body(*refs, reverse, final):
    q_ref, k_ref, v_ref, gc_ref, gr_ref = refs[:5]
    if final:
        hf_ref, og_ref, nw_ref, o_ref, ct_ref, n_ref, m_ref = refs[5:]
    else:
        o_ref, ct_ref, n_ref, m_ref = refs[5:]
    nh, dk, dv = ML_HEADS, ML_QK_DIM, ML_V_DIM
    scale = dk ** -0.5

    @pl.when(pl.program_id(1) == 0)
    def _():
        ct_ref[...] = jnp.zeros_like(ct_ref)
        n_ref[...] = jnp.zeros_like(n_ref)
        m_ref[...] = jnp.zeros_like(m_ref)

    gc, gr = gc_ref[...], gr_ref[...]
    li_c, lf_c = gc[:, 0:nh], _log_sigmoid(gc[:, nh:2 * nh])
    li_r, lf_r = gr[0:nh, :], _log_sigmoid(gr[nh:2 * nh, :])
    mask, maskf, mask_tf = _scan_masks(reverse)
    bc_c = _dot(maskf, lf_c, HI)
    bc_r = _dot(lf_r, mask_tf, HI)
    btot_c = jnp.sum(lf_c, axis=0, keepdims=True)
    n_all, m_all = n_ref[...], m_ref[...]
    heads = range(nh)
    qs = [q_ref[:, dk * h:dk * (h + 1)] for h in heads]
    ks = [k_ref[:, dk * h:dk * (h + 1)] for h in heads]
    vs = [v_ref[:, dv * h:dv * (h + 1)] for h in heads]
    cts = [ct_ref[h] for h in heads]
    m_s = [m_all[h:h + 1, 0:1] for h in heads]
    n_s = [n_all[h:h + 1, :] for h in heads]
    qk = [_dot_nt(qs[h], ks[h]) for h in heads]
    qc = [_dot(qs[h], cts[h].astype(BF16)) for h in heads]
    dmat = [jnp.where(mask, bc_c[:, h:h + 1] - bc_r[h:h + 1, :] + li_r[h:h + 1, :], -jnp.inf) for h in heads]
    inter = [bc_c[:, h:h + 1] + m_s[h] for h in heads]
    m_t = [jnp.maximum(inter[h], jnp.max(dmat[h], axis=1, keepdims=True)) for h in heads]
    s = [qk[h] * scale * jnp.exp(dmat[h] - m_t[h]) for h in heads]
    dec = [jnp.exp(inter[h] - m_t[h]) for h in heads]
    num = [_dot(s[h].astype(BF16), vs[h]) + (dec[h] * scale) * qc[h] for h in heads]
    qn = [jnp.sum(qs[h].astype(F32) * n_s[h], axis=1, keepdims=True) * scale for h in heads]
    den = [jnp.sum(s[h], axis=1, keepdims=True) + dec[h] * qn[h] for h in heads]
    hh = [num[h] / jnp.maximum(jnp.abs(den[h]), jnp.exp(-m_t[h])) for h in heads]

    wst_c = [btot_c[:, h:h + 1] - bc_c[:, h:h + 1] + li_c[:, h:h + 1] for h in heads]
    wst_r = [btot_c[:, h:h + 1] - bc_r[h:h + 1, :] + li_r[h:h + 1, :] for h in heads]
    m_new = [jnp.maximum(btot_c[:, h:h + 1] + m_s[h], jnp.max(wst_r[h], axis=1, keepdims=True)) for h in heads]
    dstate = [jnp.exp(btot_c[:, h:h + 1] + m_s[h] - m_new[h]) for h in heads]
    kw = [ks[h].astype(F32) * jnp.exp(wst_c[h] - m_new[h]) for h in heads]
    upd = [_dot(kw[h].T.astype(BF16), vs[h]) for h in heads]
    for h in heads:
        ct_ref[h] = dstate[h] * cts[h] + upd[h]
    n_rows = [dstate[h] * n_s[h] + jnp.sum(kw[h], axis=0, keepdims=True) for h in heads]
    m_rows = [jnp.broadcast_to(m_new[h], (1, LANES)) for h in heads]

    for h in heads:
        cols = slice(dv * h, dv * (h + 1))
        if final:
            hs = hh[h] + hf_ref[:, cols]
            hn = hs * lax.rsqrt(jnp.mean(hs * hs, axis=-1, keepdims=True) + EPS) * nw_ref[:, cols]
            o_ref[:, cols] = (hn * _sigmoid(og_ref[:, cols].astype(F32))).astype(o_ref.dtype)
        else:
            o_ref[:, cols] = hh[h]

    n_ref[...] = jnp.concatenate(n_rows, axis=0)
    m_ref[...] = jnp.concatenate(m_rows, axis=0)


def _mlstm(qk, pr, g_c, g_r, *, direction, n_lat, final_args=None, out_width=None):
    nb, t, _ = qk.shape
    q = SCAN_Q
    nch = t // q
    reverse = direction == 1
    cmap = _chunk_map(reverse, nch, n_lat // q)
    d = direction
    in_specs = [
        pl.BlockSpec((None, q, ML_QK_WIDTH), lambda b, k: (b, cmap(k), 0)),
        pl.BlockSpec((None, q, ML_QK_WIDTH), lambda b, k: (b, cmap(k), 1)),
        pl.BlockSpec((None, q, ML_WIDTH), lambda b, k: (b, cmap(k), 2 * ML_QK_WIDTH // ML_WIDTH)),
        pl.BlockSpec((None, None, q, 2 * ML_HEADS), lambda b, k: (b, d, cmap(k), 0)),
        pl.BlockSpec((None, None, 2 * ML_HEADS, q), lambda b, k: (b, d, 0, cmap(k))),
    ]
    args = [qk, qk, pr, g_c, g_r]
    final = final_args is not None
    if final:
        hf, norm_w = final_args
        in_specs += [pl.BlockSpec((None, q, ML_WIDTH), lambda b, k: (b, cmap(k), 0)),
                     pl.BlockSpec((None, q, ML_WIDTH), lambda b, k: (b, cmap(k), 2 * ML_QK_WIDTH // ML_WIDTH + 1)),
                     pl.BlockSpec((1, ML_WIDTH), lambda b, k: (0, 0))]
        args += [hf, pr, norm_w]
        out_shape = jax.ShapeDtypeStruct((nb, t, out_width), BF16)
    else:
        out_shape = jax.ShapeDtypeStruct((nb, t, ML_WIDTH), F32)
    return pl.pallas_call(
        functools.partial(_mlstm_body, reverse=reverse, final=final),
        out_shape=out_shape, grid=(nb, nch), in_specs=in_specs,
        out_specs=pl.BlockSpec((None, q, ML_WIDTH), lambda b, k: (b, cmap(k), 0)),
        scratch_shapes=[pltpu.VMEM((ML_HEADS, ML_QK_DIM, ML_V_DIM), F32), pltpu.VMEM((ML_HEADS, ML_QK_DIM), F32),
                        pltpu.VMEM((ML_HEADS, LANES), F32)],
        name="mlstm_bwd" if reverse else "mlstm_fwd",
        compiler_params=_cp(("parallel", "arbitrary")))(*args)


def _na_window_start(j, rows):
    return jnp.clip(j * NA_QROWS - NA_ROWS // 2, 0, rows - NA_KROWS)


def _na_body(q_ref, k_ref, v_ref, tab_ref, qw_ref, kw_ref, buf_ref, o_ref, kn_ref, *, n_lat, n_ctx):
    del buf_ref
    dh = NA_HEAD_DIM
    scale = dh ** -0.5
    rows = n_lat // GRID_W
    j = pl.program_id(2)
    nq, nk = NA_QROWS * GRID_W, NA_KROWS * GRID_W

    @pl.when(j == 0)
    def _():
        def body(i, carry):
            r = pl.ds(pl.multiple_of(i * LANES, LANES), LANES)
            kk = k_ref[r, :].astype(F32)
            kn = kk * lax.rsqrt(jnp.mean(kk * kk, axis=-1, keepdims=True) + EPS) * kw_ref[...]
            kn_ref[r, :] = kn.astype(kn_ref.dtype)
            return carry

        lax.fori_loop(0, (n_lat + n_ctx) // LANES, body, 0)

    start = _na_window_start(j, rows)
    win = pl.ds(pl.multiple_of(start * GRID_W, GRID_W), nk)
    ctx = pl.ds(n_lat, n_ctx)
    parts = range(NA_PARTS)
    rpp = NA_QROWS // NA_PARTS
    pq = rpp * GRID_W
    qq = [q_ref[i * pq:(i + 1) * pq, :].astype(F32) for i in parts]
    qn = [(qq[i] * lax.rsqrt(jnp.mean(qq[i] * qq[i], axis=-1, keepdims=True) + EPS)
           * (qw_ref[...] * scale)).astype(BF16) for i in parts]
    sw = [_dot_nt(qn[i], kn_ref[win, :]) for i in parts]
    sc = [_dot_nt(qn[i], kn_ref[ctx, :]) for i in parts]

    def with_bias(i):
        out = []
        for r in range(rpp):
            d0 = start - (j * NA_QROWS + i * rpp + r) + NA_ROWS - 1 + NA_QROWS
            out.append(sw[i][r * GRID_W:(r + 1) * GRID_W] + tab_ref[d0])
        return jnp.concatenate(out, axis=0)

    sb = [with_bias(i) for i in parts]
    m = [jnp.maximum(jnp.max(sb[i], axis=1, keepdims=True), jnp.max(sc[i], axis=1, keepdims=True)) for i in parts]
    pw = [jnp.exp(sb[i] - m[i]) for i in parts]
    pc = [jnp.exp(sc[i] - m[i]) for i in parts]
    denom = [jnp.sum(pw[i], axis=1, keepdims=True) + jnp.sum(pc[i], axis=1, keepdims=True) for i in parts]
    o = [_dot(pw[i].astype(BF16), v_ref[win, :]) + _dot(pc[i].astype(BF16), v_ref[ctx, :]) for i in parts]
    for i in parts:
        o_ref[i * pq:(i + 1) * pq, :] = (o[i] / denom[i]).astype(o_ref.dtype)


def _na_bias_table(rpb, rows):
    nd = 2 * NA_QROWS
    valid = np.zeros((nd, NA_KROWS), bool)
    seen = np.zeros((nd,), bool)
    for j in range(rows // NA_QROWS):
        start = int(np.clip(j * NA_QROWS - NA_ROWS // 2, 0, rows - NA_KROWS))
        for qr in range(NA_QROWS):
            r = j * NA_QROWS + qr
            rs = int(np.clip(r - NA_ROWS // 2, 0, rows - NA_ROWS))
            d = start - r + NA_ROWS - 1 + NA_QROWS
            v = np.array([(rs <= start + i < rs + NA_ROWS) for i in range(NA_KROWS)])
            assert 0 <= d < nd and (not seen[d] or (valid[d] == v).all())
            valid[d], seen[d] = v, True
    col = np.arange(GRID_W)
    cs = np.clip(col - NA_COLS // 2, 0, GRID_W - NA_COLS)
    in_win = (col[None, :] >= cs[:, None]) & (col[None, :] < cs[:, None] + NA_COLS)
    ndr, ndc, w = 2 * NA_ROWS - 1, 2 * NA_COLS - 1, GRID_W
    dr = (np.arange(nd)[:, None] - NA_QROWS) + np.arange(NA_KROWS)[None, :]
    ok = valid[:, None, :, None] & in_win[None, :, None, :] & (dr >= 0)[:, None, :, None] \
        & (dr < ndr)[:, None, :, None]
    ext = jnp.pad(rpb, ((0, 0), (0, 0), (w, w)), mode="edge")
    c0 = w + NA_COLS - 1
    t1 = jnp.stack([ext[:, :, c0 - qc:c0 - qc + w] for qc in range(w)], axis=2)
    t1p = jnp.pad(t1, ((0, 0), (NA_QROWS, nd + NA_KROWS - 1 - NA_QROWS - ndr), (0, 0), (0, 0)))
    wide = jnp.concatenate([t1p[:, m] for m in range(nd + NA_KROWS - 1)], axis=-1)
    vals = jnp.stack([wide[:, :, d * w:(d + NA_KROWS) * w] for d in range(nd)], axis=1)
    return jnp.where(ok.reshape(nd, w, NA_KROWS * w)[None], vals, -jnp.inf).astype(F32)


def _na(pr, tab, q_w, k_w, ybuf, *, n_lat, n_ctx, qcol, out_col):
    nb, t, _ = pr.shape
    dh, nh = NA_HEAD_DIM, NA_HEADS
    nq = NA_QROWS * GRID_W
    qb, ob = qcol // dh, out_col // dh
    nd = tab.shape[1]
    return pl.pallas_call(
        functools.partial(_na_body, n_lat=n_lat, n_ctx=n_ctx),
        out_shape=jax.ShapeDtypeStruct(ybuf.shape, ybuf.dtype), grid=(nb, nh, n_lat // nq),
        in_specs=[pl.BlockSpec((None, nq, dh), lambda b, h, j: (b, j, qb + h)),
                  pl.BlockSpec((None, t, dh), lambda b, h, j: (b, 0, qb + nh + h)),
                  pl.BlockSpec((None, t, dh), lambda b, h, j: (b, 0, qb + 2 * nh + h)),
                  pl.BlockSpec((None, nd, GRID_W, NA_KROWS * GRID_W), lambda b, h, j: (h, 0, 0, 0)),
                  pl.BlockSpec((1, dh), lambda b, h, j: (0, 0)),
                  pl.BlockSpec((1, dh), lambda b, h, j: (0, 0)),
                  pl.BlockSpec(memory_space=pl.ANY)],
        out_specs=pl.BlockSpec((None, nq, dh), lambda b, h, j: (b, j, ob + h)),
        scratch_shapes=[pltpu.VMEM((t, dh), BF16)],
        input_output_aliases={6: 0}, name="na",
        compiler_params=_cp(("parallel", "parallel", "arbitrary")))(pr, pr, pr, tab, q_w, k_w, ybuf)


def _rope_tables(n_lat, n_ctx, width):
    dh = ML_QK_DIM
    nf = dh // 4
    t = jnp.arange(n_lat)
    inv = ROPE_THETA ** (-jnp.arange(nf, dtype=F32) / nf)
    lane = np.arange(dh)
    pos = jnp.where((lane < dh // 2)[None, :], (t // GRID_W)[:, None], (t % GRID_W)[:, None]).astype(F32)
    ang = pos * inv[lane % nf][None, :]
    sign = np.where((lane % (dh // 2)) < nf, -1.0, 1.0).astype(np.float32)
    cos = jnp.concatenate([jnp.cos(ang), jnp.ones((n_ctx, dh), F32)], axis=0)
    sin = jnp.concatenate([jnp.sin(ang) * sign[None, :], jnp.zeros((n_ctx, dh), F32)], axis=0)
    reps = width // dh
    return jnp.tile(cos, (1, reps)), jnp.tile(sin, (1, reps))


def _ada_mod(c, c_ctx, w_all, b_all, layer):
    nb, d = c.shape
    rows = 16
    cv = jnp.concatenate([c, c_ctx[None, :], jnp.zeros((rows - nb - 1, d), F32)], axis=0)
    a = (cv * _sigmoid(cv)).astype(BF16)[None]
    out = _matmul(a, w_all, w_index=layer, tm=rows, tn=1024, tk=d, out_dtype=F32, name="ada_mod")[0]
    out = out[:nb + 1] + b_all[layer][None, :]
    return out.reshape(nb + 1, 6, d).transpose(1, 0, 2)


def _lat_ctx(m, nb):
    ctx = jnp.broadcast_to(m[nb][None, :], (nb, m.shape[1]))
    return jnp.stack([m[:nb], ctx], axis=1)[:, :, None, :]


def _pad_cols(w, n):
    return jnp.pad(w, ((0, 0), (0, n - w.shape[1])))


def _col_row_forms(g, groups):
    nb, t, _ = g.shape
    g5 = g.reshape(nb, t, 2, groups, -1)
    return g5.transpose(0, 2, 3, 1, 4), g5.transpose(0, 2, 3, 4, 1)


ROW_TILES = (1088, 1024, 512, 256)


def _row_tile(rows):
    return next(tm for tm in ROW_TILES if rows % tm == 0)


def _mod_pair(mods, shift, scale, nb):
    return jnp.stack([_lat_ctx(mods[shift], nb), _lat_ctx(mods[scale], nb)], axis=1)


def _out_projection(a, w_all, x, gate, layer, *, rows, n_lat, tk, next_norm, name):
    tm = _row_tile(rows)
    x = _matmul_gated_residual(a, w_all, x, gate, w_index=layer, tm=tm, tn=1024, tk=tk, rows=rows, n_lat=n_lat,
                               name=name)
    if next_norm is None:
        return x, None
    return x, _rmsmod(x, *next_norm, tm=tm, rows=rows, n_lat=n_lat)


def _conv_ffn_block(x, u, mods, w_up_all, conv_w, conv_b, w_down_all, layer, *, rows, n_lat, segs, next_norm):
    nb = x.shape[0]
    up = _matmul(u, w_up_all, w_index=layer, tm=_row_tile(rows), tn=1024, tk=D_MODEL, out_dtype=BF16, name="ffn_up")
    hid = _conv_ffn(up, conv_w, conv_b[None, :], segs=segs, rows=rows)
    return _out_projection(hid, w_down_all, x, _lat_ctx(mods[5], nb), layer, rows=rows, n_lat=n_lat, tk=D_FF // 4,
                           next_norm=next_norm, name="ffn_down")


def _even_layer(x, u, mods, w_in_all, conv_w, conv_b, dt_bias, a_log, d_skip, ssd_norm_w,
                hy_w1, hy_b1, hy_w2, hy_b2, hy_w3, hy_freq, hy_bias, w_out_all, j, *, n_lat, n_ctx, next_norm):
    nb, t, d = x.shape
    segs = ((0, n_lat), (n_lat, n_ctx))
    tm = _row_tile(t)
    n_main = SSD_WIDTH + EV_CONV_CH
    pr = _matmul(u, w_in_all, w_index=j, cols=n_main, tm=tm, tn=1024, tk=d, out_dtype=BF16, name="ev_in")
    w_dt = _pad_cols(w_in_all[j, :, n_main:], LANES).astype(BF16)
    dtp = _matmul(u, w_dt[None], tm=tm, tn=LANES, tk=d, out_dtype=F32, name="ev_in_dt")[:, :, :2 * SSD_HEADS]
    cb = conv_b[None, :]
    xbc = _conv_silu(pr, conv_w, cb, xcol=SSD_WIDTH, wcol=0, width=SSD_XBC, segs=segs)
    x0, zin_l, zin_c, zle, zlo, zce, zco = _conv_hyena(pr, conv_w, cb, xcol=SSD_WIDTH + SSD_XBC, wcol=SSD_XBC,
                                                       segs=segs)

    dt_c, dt_r = _col_row_forms(dtp, SSD_GROUPS)
    db = dt_bias.reshape(2, SSD_GROUPS, 1, SSD_HPG)
    al = a_log.reshape(2, SSD_GROUPS, 1, SSD_HPG)
    scan_args = (xbc, dt_c, dt_r, db, db.transpose(0, 1, 3, 2), al, al.transpose(0, 1, 3, 2))
    yf = _ssd(*scan_args, direction=0, n_lat=n_lat)
    dsk = jnp.repeat(d_skip, SSD_HEAD_DIM)[None, :]
    ybuf = _ssd(*scan_args, direction=1, n_lat=n_lat, final_args=(yf, pr, dsk, ssd_norm_w[None, :]),
                out_width=SSD_WIDTH + HY_WIDTH)

    hyb = hy_bias[None, :]
    filt = functools.partial(_hyena_filters, w1=hy_w1, b1=hy_b1, w2=hy_w2, b2=hy_b2, w3=hy_w3, freq=hy_freq)
    ybuf = _hyena_longconv(zin_l, zle, zlo, x0, filt(n_lat), hyb, ybuf, row0=0)
    ybuf = _hyena_longconv(zin_c, zce, zco, x0, filt(n_ctx), hyb, ybuf, row0=n_lat)
    return _out_projection(ybuf, w_out_all, x, _lat_ctx(mods[2], nb), j, rows=t, n_lat=n_lat, tk=2048,
                           next_norm=next_norm, name="ev_out")


def _odd_layer_latent(x, u, mods, w_in, conv_w, conv_b, gate_b, ml_norm_w, q_norm_w, k_norm_w, rpb,
                      w_out_all, j, *, n_lat, n_ctx, next_norm):
    nb, t, d = x.shape
    segs = ((0, n_lat), (n_lat, n_ctx))
    tm = _row_tile(t)
    o3 = 2 * ML_QK_WIDTH + 2 * ML_WIDTH
    o4 = o3 + 4 * ML_HEADS
    wb = w_in.astype(BF16)
    w_main = jnp.concatenate([wb[:, :o3], wb[:, o4:]], axis=1)
    pr = _matmul(u, w_main[None], tm=tm, tn=1024, tk=d, out_dtype=BF16, name="od_in")
    gates = _matmul(u, _pad_cols(wb[:, o3:o4], LANES)[None], tm=tm, tn=LANES, tk=d, out_dtype=F32,
                    name="od_in_gates")[:, :, :4 * ML_HEADS] + gate_b.reshape(-1)
    cos, sin = _rope_tables(n_lat, n_ctx, 256)
    qk = _conv_rope(pr, conv_w, conv_b[None, :], cos, sin, width=2 * ML_QK_WIDTH, segs=segs)
    g5 = gates.reshape(nb, t, 2, 2 * ML_HEADS)
    g_c, g_r = g5.transpose(0, 2, 1, 3), g5.transpose(0, 2, 3, 1)
    hf = _mlstm(qk, pr, g_c, g_r, direction=0, n_lat=n_lat)
    ybuf = _mlstm(qk, pr, g_c, g_r, direction=1, n_lat=n_lat, final_args=(hf, ml_norm_w[None, :]),
                  out_width=ML_WIDTH + NA_WIDTH)
    tab = _na_bias_table(rpb, n_lat // GRID_W)
    ybuf = _na(pr, tab, q_norm_w[None, :], k_norm_w[None, :], ybuf, n_lat=n_lat, n_ctx=n_ctx, qcol=o3,
               out_col=ML_WIDTH)
    return _out_projection(ybuf, w_out_all, x, _lat_ctx(mods[2], nb), j, rows=n_lat, n_lat=n_lat, tk=2048,
                           next_norm=next_norm, name="od_out")


def kernel(x, c, ctx, c_ctx, ada_w, ada_b, norm_w, ev_w_in, ev_conv_w, ev_conv_b, ssd_dt_bias, ssd_a_log, ssd_d,
           ssd_norm_w, hy_w1, hy_b1, hy_w2, hy_b2, hy_w3, hy_freq, hy_bias, ev_w_out, od_w_in, ml_conv_w, ml_conv_b,
           ml_gate_b, ml_norm_w, na_q_norm_w, na_k_norm_w, na_rpb, od_w_out, ffn_w_up, ffn_conv_w, ffn_conv_b,
           ffn_w_down):
    depth = ada_w.shape[0]
    assert depth == 2, "layer schedule below is written for one even and one odd layer"
    n_lat, n_ctx = x.shape[1], ctx.shape[1]
    t = n_lat + n_ctx
    both = ((0, n_lat), (n_lat, n_ctx))
    xs = jnp.concatenate([x, ctx], axis=1)

    nb = x.shape[0]
    w_down = ffn_w_down.astype(BF16)
    mods0 = _ada_mod(c, c_ctx, ada_w, ada_b, 0)
    mods1 = _ada_mod(c, c_ctx, ada_w, ada_b, 1)
    norm = lambda layer, sub, mods: (norm_w[layer, sub][None, :], _mod_pair(mods, 3 * sub, 3 * sub + 1, nb))

    w0, mod0 = norm(0, 0, mods0)
    u = _rmsmod(xs, w0, mod0, tm=_row_tile(t), rows=t, n_lat=n_lat)
    xs, u = _even_layer(xs, u, mods0, ev_w_in, ev_conv_w[0], ev_conv_b[0], ssd_dt_bias[0], ssd_a_log[0],
                        ssd_d[0], ssd_norm_w[0], hy_w1[0], hy_b1[0], hy_w2[0], hy_b2[0], hy_w3[0], hy_freq[0],
                        hy_bias[0], ev_w_out.astype(BF16), 0, n_lat=n_lat, n_ctx=n_ctx, next_norm=norm(0, 1, mods0))
    xs, u = _conv_ffn_block(xs, u, mods0, ffn_w_up, ffn_conv_w[0], ffn_conv_b[0], w_down, 0,
                            rows=t, n_lat=n_lat, segs=both, next_norm=norm(1, 0, mods1))
    xl, u = _odd_layer_latent(xs, u, mods1, od_w_in[0], ml_conv_w[0], ml_conv_b[0], ml_gate_b[0], ml_norm_w[0],
                              na_q_norm_w[0], na_k_norm_w[0], na_rpb[0], od_w_out.astype(BF16), 0,
                              n_lat=n_lat, n_ctx=n_ctx, next_norm=norm(1, 1, mods1))
    out, _ = _conv_ffn_block(xl, u, mods1, ffn_w_up, ffn_conv_w[1], ffn_conv_b[1], w_down, 1,
                             rows=n_lat, n_lat=n_lat, segs=((0, n_lat),), next_norm=None)
    return out
```

```python
import functools
import math

import numpy as np
import jax
import jax.numpy as jnp
from jax import lax
from jax.experimental import pallas as pl
from jax.experimental.pallas import tpu as pltpu

F32 = jnp.float32
BF16 = jnp.bfloat16
HI = lax.Precision.HIGHEST

D_MODEL = 2048
SEQ = 4096
CTX_LEN = 256
GRID_W = 64
EPS = 1e-6
CONV_W = 3
SSD_WIDTH = D_MODEL
SSD_HEAD_DIM = 64
SSD_HEADS = SSD_WIDTH // SSD_HEAD_DIM
SSD_STATE = 128
SSD_GROUPS = 4
SSD_HPG = SSD_HEADS // SSD_GROUPS
SSD_GW = SSD_WIDTH // SSD_GROUPS
SSD_XBC = SSD_WIDTH + 2 * SSD_GROUPS * SSD_STATE
HY_WIDTH = D_MODEL
HY_EMB = 33
HY_BANDS = (HY_EMB - 1) // 2
HY_ORDER = 64
HY_FAST = 0.3
HY_SLOW = 1.5
HY_TARGET = 1e-2
ML_HEADS = 8
ML_WIDTH = D_MODEL
ML_V_DIM = ML_WIDTH // ML_HEADS
ML_QK_DIM = ML_V_DIM // 2
ML_QK_WIDTH = ML_HEADS * ML_QK_DIM
ROPE_THETA = 10000.0
NA_WIDTH = D_MODEL
NA_HEAD_DIM = 128
NA_HEADS = NA_WIDTH // NA_HEAD_DIM
NA_ROWS = 8
NA_COLS = 16
NA_QROWS = 8
NA_KROWS = 16
NA_PARTS = 2
D_FF = 256 * ((8 * D_MODEL // 3 + 255) // 256)
EV_CONV_CH = SSD_XBC + 3 * HY_WIDTH
SCAN_Q = 128
LANES = 128
SUBLANES = 8
VMEM_LIMIT = 48 * 1024 * 1024


def _cp(sem, vmem=VMEM_LIMIT):
    return pltpu.CompilerParams(dimension_semantics=sem, vmem_limit_bytes=vmem)


def _sigmoid(x):
    return 1.0 / (1.0 + jnp.exp(-x))


def _silu(x):
    return x * _sigmoid(x)


def _softplus(x):
    return jnp.maximum(x, 0.0) + jnp.log(1.0 + jnp.exp(-jnp.abs(x)))


def _log_sigmoid(x):
    return -_softplus(-x)


def _dot(a, b, precision=None):
    return jnp.dot(a, b, preferred_element_type=F32, precision=precision)


def _dot_nt(a, b):
    return lax.dot_general(a, b, (((1,), (1,)), ((), ())), preferred_element_type=F32)


def _split_bf16(x, terms):
    out = []
    for _ in range(terms):
        part = x.astype(BF16)
        out.append(part)
        x = x - part.astype(F32)
    return out


def _dot_sel(x, sel, terms):
    selb = sel.astype(BF16)
    return sum(_dot(part, selb) for part in _split_bf16(x, terms))


def _sel_dot(sel, x, terms):
    selb = sel.astype(BF16)
    return sum(_dot(selb, part) for part in _split_bf16(x, terms))


def _mm_body(*refs, nks, n_extra, epilogue, cache_w, n_out):
    n_a = len(nks)
    a_refs, w_ref = refs[:n_a], refs[n_a]
    extra = refs[n_a + 1:n_a + 1 + n_extra]
    pos = n_a + 1 + n_extra
    o_refs = refs[pos:pos + n_out]
    scratch = refs[pos + n_out:]
    nk = sum(nks)
    row_blk = pl.program_id(2)
    if cache_w:
        wb_ref = scratch[0]

        @pl.when((pl.program_id(1) == 0) & (row_blk == 0))
        def _():
            wb_ref[...] = w_ref[...].astype(BF16)

        w_val = wb_ref[...]
    else:
        w_val = w_ref[...].astype(BF16)

    def finish(acc):
        vals = acc if epilogue is None else epilogue(acc, row_blk, *extra)
        for o_ref, val in zip(o_refs, vals if n_out > 1 else (vals,)):
            o_ref[...] = val.astype(o_ref.dtype)

    if nk == 1:
        finish(_dot(a_refs[0][...].astype(BF16), w_val))
        return
    acc_ref = scratch[0]
    k = pl.program_id(3)
    lo = 0
    for a_ref, steps in zip(a_refs, nks):
        hi = lo + steps
        if lo == 0:
            @pl.when(k == 0)
            def _(a_ref=a_ref):
                acc_ref[...] = _dot(a_ref[...].astype(BF16), w_val)

        if hi > max(lo, 1):
            @pl.when((k >= max(lo, 1)) & (k < hi))
            def _(a_ref=a_ref):
                acc_ref[...] += _dot(a_ref[...].astype(BF16), w_val)

        lo = hi

    @pl.when(k == nk - 1)
    def _():
        finish(acc_ref[...])


def _matmul(a, w, *, tm, tn, tk, out_dtype, rows=None, cols=None, w_index=None, extra=(), epilogue=None,
            n_out=1, name="mm"):
    a_list = a if isinstance(a, (tuple, list)) else (a,)
    ba, ta = a_list[0].shape[0], min(x.shape[1] for x in a_list)
    bw, kdim, n = w.shape
    assert sum(x.shape[2] for x in a_list) == kdim and all(x.shape[2] % tk == 0 for x in a_list)
    nb = ba if w_index is not None else max(ba, bw)
    rows = ta if rows is None else rows
    n = n if cols is None else cols
    assert rows % tm == 0 and n % tn == 0
    nks = tuple(x.shape[2] // tk for x in a_list)
    ni, nj, nk = rows // tm, n // tn, sum(nks)
    in_specs, lo = [], 0
    for steps in nks:
        def a_map(j, b, i, k, lo=lo, steps=steps):
            return (b if ba > 1 else 0, i, jnp.clip(k - lo, 0, steps - 1))

        in_specs.append(pl.BlockSpec((None, tm, tk), a_map))
        lo += steps
    if w_index is not None:
        w_map = lambda j, b, i, k: (w_index, k, j)
    else:
        w_map = (lambda j, b, i, k: (b, k, j)) if bw > 1 else (lambda j, b, i, k: (0, k, j))
    cache_w = w.dtype != BF16 and nk == 1 and (w_index is not None or bw == 1)
    in_specs.append(pl.BlockSpec((None, tk, tn), w_map))
    args = [*a_list, w]
    for arr, spec in extra:
        args.append(arr)
        in_specs.append(spec)
    out_shape = jax.ShapeDtypeStruct((nb, rows, n), out_dtype)
    out_spec = pl.BlockSpec((None, tm, tn), lambda j, b, i, k: (b, i, j))
    if n_out > 1:
        out_shape, out_spec = (out_shape,) * n_out, (out_spec,) * n_out
    scratch = [pltpu.VMEM((tm, tn), F32)] if nk > 1 else []
    if cache_w:
        scratch = [pltpu.VMEM((tk, tn), BF16)]
    body = functools.partial(_mm_body, nks=nks, n_extra=len(extra), epilogue=epilogue, cache_w=cache_w, n_out=n_out)
    inner = "arbitrary" if cache_w else "parallel"
    return pl.pallas_call(
        body, out_shape=out_shape, grid=(nj, nb, ni, nk), in_specs=in_specs, out_specs=out_spec,
        scratch_shapes=scratch, name=name,
        compiler_params=_cp(("parallel", inner, inner, "arbitrary")))(*args)


def _row_is_lat(row_blk, tm, n_lat):
    rows = row_blk * tm + lax.broadcasted_iota(jnp.int32, (tm, 1), 0)
    return rows < n_lat


def _gated_residual_epilogue(acc, row_blk, res_ref, gate_ref, *, tm, n_lat):
    gate = jnp.where(_row_is_lat(row_blk, tm, n_lat), gate_ref[0], gate_ref[1])
    return res_ref[...] + gate * acc


def _gated_residual_rows_body(*refs, ks, tm, n_lat):
    n_a = len(ks)
    a_refs, w_ref, res_ref, gate_ref, o_ref = refs[:n_a], refs[n_a], refs[n_a + 1], refs[n_a + 2], refs[n_a + 3]
    acc, lo = None, 0
    for a_ref, kk in zip(a_refs, ks):
        part = _dot(a_ref[...], w_ref[lo:lo + kk, :])
        acc = part if acc is None else acc + part
        lo += kk
    o_ref[...] = _gated_residual_epilogue(acc, pl.program_id(1), res_ref, gate_ref, tm=tm, n_lat=n_lat)


def _matmul_gated_residual_rows(a_list, w, res, gate, *, w_index, tm, tn, rows, n_lat, name):
    nb = a_list[0].shape[0]
    _, kdim, n = w.shape
    ks = tuple(x.shape[2] for x in a_list)
    assert sum(ks) == kdim and rows % tm == 0 and n % tn == 0
    in_specs = [pl.BlockSpec((None, tm, kk), lambda b, i, j: (b, i, 0)) for kk in ks]
    in_specs += [pl.BlockSpec((None, kdim, tn), lambda b, i, j: (w_index, 0, j)),
                 pl.BlockSpec((None, tm, tn), lambda b, i, j: (b, i, j)),
                 pl.BlockSpec((None, 2, 1, tn), lambda b, i, j: (b, 0, 0, j))]
    return pl.pallas_call(
        functools.partial(_gated_residual_rows_body, ks=ks, tm=tm, n_lat=n_lat),
        out_shape=jax.ShapeDtypeStruct((nb, rows, n), F32), grid=(nb, rows // tm, n // tn), in_specs=in_specs,
        out_specs=pl.BlockSpec((None, tm, tn), lambda b, i, j: (b, i, j)), name=name,
        compiler_params=_cp(("parallel", "parallel", "parallel")))(*a_list, w, res, gate)


def _modulated_rmsnorm(x, is_lat, w_ref, mod_ref):
    y = x * lax.rsqrt(jnp.mean(x * x, axis=-1, keepdims=True) + EPS) * w_ref[...]
    shift = jnp.where(is_lat, mod_ref[0, 0], mod_ref[0, 1])
    scale = jnp.where(is_lat, mod_ref[1, 0], mod_ref[1, 1])
    return y * (1.0 + scale) + shift


def _rmsmod_body(x_ref, w_ref, mod_ref, o_ref, *, tm, n_lat):
    is_lat = _row_is_lat(pl.program_id(2), tm, n_lat)
    o_ref[...] = _modulated_rmsnorm(x_ref[...], is_lat, w_ref, mod_ref).astype(o_ref.dtype)


def _rmsmod(x, w, mod, *, tm, rows, n_lat):
    nb, _, d = x.shape
    tm = tm // 4
    assert rows % tm == 0 and tm % 16 == 0
    body = functools.partial(_rmsmod_body, tm=tm, n_lat=n_lat)
    return pl.pallas_call(
        body, out_shape=jax.ShapeDtypeStruct((nb, rows, d), BF16), grid=(1, nb, rows // tm),
        in_specs=[pl.BlockSpec((None, tm, d), lambda j, b, i: (b, i, 0)),
                  pl.BlockSpec((1, d), lambda j, b, i: (0, 0)),
                  pl.BlockSpec((None, 2, 2, 1, d), lambda j, b, i: (b, 0, 0, 0, 0))],
        out_specs=pl.BlockSpec((None, tm, d), lambda j, b, i: (b, i, 0)),
        name="rmsmod", compiler_params=_cp(("parallel", "parallel", "parallel")))(x, w, mod)


CONV_CH = 128
CONV_HALO = 16
CONV_STEPS = 4


def _shift_matrices():
    r = lax.broadcasted_iota(jnp.int32, (CONV_CH, CONV_CH + 2 * CONV_HALO), 0)
    c = lax.broadcasted_iota(jnp.int32, (CONV_CH, CONV_CH + 2 * CONV_HALO), 1)
    return (c == r + CONV_HALO - 1).astype(BF16), (c == r + CONV_HALO + 1).astype(BF16)


def _chunk_rows(r0, i):
    return pl.ds(r0 + pl.multiple_of(i * CONV_CH, CONV_CH), CONV_CH)


def _conv_at(x_ref, w_ref, b_ref, shifts, r0, n, steps):
    cur = [x_ref[_chunk_rows(r0, i), :] for i in steps]
    zero = jnp.zeros((CONV_HALO, cur[0].shape[1]), cur[0].dtype)
    lo = [pl.multiple_of(jnp.maximum(i * CONV_CH - CONV_HALO, 0), CONV_HALO) for i in steps]
    hi = [pl.multiple_of(jnp.minimum((i + 1) * CONV_CH, n - CONV_HALO), CONV_HALO) for i in steps]
    tail = [jnp.where(i > 0, x_ref[pl.ds(r0 + l, CONV_HALO), :], zero) for i, l in zip(steps, lo)]
    head = [jnp.where(i < n // CONV_CH - 1, x_ref[pl.ds(r0 + h, CONV_HALO), :], zero) for i, h in zip(steps, hi)]
    win = [jnp.concatenate([t, c, h], axis=0) for t, c, h in zip(tail, cur, head)]
    prev = [_dot(shifts[0], w) for w in win]
    nxt = [_dot(shifts[1], w) for w in win]
    return [w_ref[0:1, :] * p + w_ref[1:2, :] * c.astype(F32) + w_ref[2:3, :] * x + b_ref[...]
            for p, c, x in zip(prev, cur, nxt)]


def _conv_loop(segs, fn):
    for si, (r0, n) in enumerate(segs):
        def body(i, carry, r0=r0, n=n, si=si):
            fn(si, r0, n, [i])
            return carry

        lax.fori_loop(0, n // CONV_CH, body, 0, unroll=min(CONV_STEPS, n // CONV_CH))


def _conv_silu_body(x_ref, w_ref, b_ref, o_ref, *, segs):
    shifts = _shift_matrices()

    def fn(si, r0, n, steps):
        ys = [_silu(y) for y in _conv_at(x_ref, w_ref, b_ref, shifts, r0, n, steps)]
        for i, y in zip(steps, ys):
            o_ref[_chunk_rows(r0, i), :] = y.astype(o_ref.dtype)

    _conv_loop(segs, fn)


def _conv_rope_body(x_ref, w_ref, b_ref, cos_ref, sin_ref, o_ref, *, segs, tc):
    shifts = _shift_matrices()
    lane = lax.broadcasted_iota(jnp.int32, (CONV_CH, tc), 1)
    first = (lane % (ML_QK_DIM // 2)) < (ML_QK_DIM // 4)
    nf = ML_QK_DIM // 4

    def fn(si, r0, n, steps):
        ys = [_silu(y) for y in _conv_at(x_ref, w_ref, b_ref, shifts, r0, n, steps)]
        partner = [jnp.where(first, pltpu.roll(y, tc - nf, 1), pltpu.roll(y, nf, 1)) for y in ys]
        for i, y, p in zip(steps, ys, partner):
            rows = _chunk_rows(r0, i)
            o_ref[rows, :] = (y * cos_ref[rows, :] + p * sin_ref[rows, :]).astype(o_ref.dtype)

    _conv_loop(segs, fn)


def _conv_ffn_body(a_ref, x_ref, w_ref, b_ref, o_ref, *, segs):
    shifts = _shift_matrices()

    def fn(si, r0, n, steps):
        gs = [_silu(y) for y in _conv_at(x_ref, w_ref, b_ref, shifts, r0, n, steps)]
        for i, g in zip(steps, gs):
            rows = _chunk_rows(r0, i)
            o_ref[rows, :] = (a_ref[rows, :].astype(F32) * g).astype(o_ref.dtype)

    _conv_loop(segs, fn)


def _conv_hyena_body(x0_ref, x1_ref, v_ref, w0_ref, b0_ref, w1_ref, b1_ref, w2_ref, b2_ref,
                     ox0_ref, ozl_ref, ozc_ref, ozle_ref, ozlo_ref, ozce_ref, ozco_ref, *zs_refs, segs):
    shifts = _shift_matrices()
    half = CONV_CH // 2
    outs = ((ozl_ref, ozle_ref, ozlo_ref), (ozc_ref, ozce_ref, ozco_ref))

    def fn(si, r0, n, steps):
        x0 = _conv_at(x0_ref, w0_ref, b0_ref, shifts, r0, n, steps)
        x1 = _conv_at(x1_ref, w1_ref, b1_ref, shifts, r0, n, steps)
        v = _conv_at(v_ref, w2_ref, b2_ref, shifts, r0, n, steps)
        nat_ref, even_ref, odd_ref = outs[si]
        for i, x0_i, x1_i, v_i in zip(steps, x0, x1, v):
            ox0_ref[_chunk_rows(r0, i), :] = x0_i.astype(ox0_ref.dtype)
            zin = x1_i * v_i
            nat_ref[_chunk_rows(0, i), :] = zin.astype(BF16)
            t0 = r0 + pl.multiple_of(i * CONV_CH, CONV_CH)
            hrows = pl.ds(pl.multiple_of(i * half, half), half)
            for g, zs_ref in enumerate(zs_refs):
                lanes = slice(g * LANES, (g + 1) * LANES)
                zs_ref[pl.ds(t0, CONV_CH), :] = zin[:, lanes]
                even_ref[hrows, lanes] = zs_ref[pl.ds(t0, half, stride=2), :].astype(BF16)
                odd_ref[hrows, lanes] = zs_ref[pl.ds(t0 + 1, half, stride=2), :].astype(BF16)

    _conv_loop(segs, fn)


def _conv_specs(t_in, tc, xcol_blk, wcol_blk):
    return [pl.BlockSpec((None, t_in, tc), lambda b, j: (b, 0, j + xcol_blk)),
            pl.BlockSpec((CONV_W, tc), lambda b, j: (0, j + wcol_blk)),
            pl.BlockSpec((1, tc), lambda b, j: (0, j + wcol_blk))]


def _conv_silu(x, w, b, *, xcol, wcol, width, segs, tc=256):
    nb, t_in, _ = x.shape
    return pl.pallas_call(
        functools.partial(_conv_silu_body, segs=segs),
        out_shape=jax.ShapeDtypeStruct((nb, t_in, width), BF16), grid=(nb, width // tc),
        in_specs=_conv_specs(t_in, tc, xcol // tc, wcol // tc),
        out_specs=pl.BlockSpec((None, t_in, tc), lambda b, j: (b, 0, j)),
        name="conv_silu", compiler_params=_cp(("parallel", "parallel")))(x, w, b)


def _conv_rope(x, w, b, cos, sin, *, width, segs, tc=256):
    nb, t_in, _ = x.shape
    tab = pl.BlockSpec((t_in, tc), lambda b, j: (0, 0))
    return pl.pallas_call(
        functools.partial(_conv_rope_body, segs=segs, tc=tc),
        out_shape=jax.ShapeDtypeStruct((nb, t_in, width), BF16), grid=(nb, width // tc),
        in_specs=_conv_specs(t_in, tc, 0, 0) + [tab, tab],
        out_specs=pl.BlockSpec((None, t_in, tc), lambda b, j: (b, 0, j)),
        name="conv_rope", compiler_params=_cp(("parallel", "parallel")))(x, w, b, cos, sin)


def _conv_ffn(up, w, b, *, segs, rows, tc=256):
    nb, t_in, two_ff = up.shape
    ff = two_ff // 2
    return pl.pallas_call(
        functools.partial(_conv_ffn_body, segs=segs),
        out_shape=jax.ShapeDtypeStruct((nb, rows, ff), BF16), grid=(nb, ff // tc),
        in_specs=[pl.BlockSpec((None, rows, tc), lambda b, j: (b, 0, j))] + _conv_specs(rows, tc, ff // tc, 0),
        out_specs=pl.BlockSpec((None, rows, tc), lambda b, j: (b, 0, j)),
        name="conv_ffn", compiler_params=_cp(("parallel", "parallel")))(up, up, w, b)


def _conv_hyena(pr, w, b, *, xcol, wcol, segs, tc=256):
    nb, t_in, _ = pr.shape
    (_, n_lat), (_, n_ctx) = segs
    wd = HY_WIDTH
    out_rows = (t_in, n_lat, n_ctx, n_lat // 2, n_lat // 2, n_ctx // 2, n_ctx // 2)
    in_specs = [pl.BlockSpec((None, t_in, tc), functools.partial(lambda b, j, o: (b, 0, j + o), o=(xcol + m * wd) // tc))
                for m in range(3)]
    wargs = []
    for m in range(3):
        off = (wcol + m * wd) // tc
        in_specs.append(pl.BlockSpec((CONV_W, tc), functools.partial(lambda b, j, o: (0, j + o), o=off)))
        in_specs.append(pl.BlockSpec((1, tc), functools.partial(lambda b, j, o: (0, j + o), o=off)))
        wargs += [w, b]
    return pl.pallas_call(
        functools.partial(_conv_hyena_body, segs=segs),
        out_shape=tuple(jax.ShapeDtypeStruct((nb, rows, wd), BF16) for rows in out_rows),
        grid=(nb, wd // tc), in_specs=in_specs,
        out_specs=tuple(pl.BlockSpec((None, rows, tc), lambda b, j: (b, 0, j)) for rows in out_rows),
        scratch_shapes=[pltpu.VMEM((t_in, LANES), F32)] * (tc // LANES),
        name="conv_hyena", compiler_params=_cp(("parallel", "parallel")))(pr, pr, pr, *wargs)


def _scan_masks(reverse):
    q = SCAN_Q
    ii = lax.broadcasted_iota(jnp.int32, (q, q), 0)
    jj = lax.broadcasted_iota(jnp.int32, (q, q), 1)
    mask = (jj >= ii) if reverse else (jj <= ii)
    mask_t = (jj <= ii) if reverse else (jj >= ii)
    return mask, mask.astype(F32), mask_t.astype(F32)


def _chunk_map(reverse, n_chunks, n_lat_chunks):
    if reverse:
        return lambda k: n_chunks - 1 - k
    return lambda k: (k + n_lat_chunks) % n_chunks


SSD_GPS = 4


def _ssd_body(*refs, reverse, final):
    (xs_ref, b_ref, c_ref, dtc_ref, dtr_ref, dbc_ref, dbr_ref, alc_ref, alr_ref) = refs[:9]
    if final:
        yf_ref, z_ref, dsk_ref, nw_ref, o_ref, h_ref = refs[9:]
    else:
        o_ref, h_ref = refs[9:]

    @pl.when(pl.program_id(2) == 0)
    def _():
        h_ref[...] = jnp.zeros_like(h_ref)

    q, gw, hd = SCAN_Q, SSD_GW, SSD_HEAD_DIM
    mask, maskf, mask_tf = _scan_masks(reverse)
    groups = range(SSD_GPS)
    pairs = range(SSD_HPG // 2)
    cols = [slice(g * gw, (g + 1) * gw) for g in groups]
    xs = [xs_ref[:, cols[g]].astype(F32) for g in groups]
    bm = [b_ref[:, g * SSD_STATE:(g + 1) * SSD_STATE] for g in groups]
    cm = [c_ref[:, g * SSD_STATE:(g + 1) * SSD_STATE] for g in groups]
    h_prev = [h_ref[g] for g in groups]
    dt_c = [_softplus(dtc_ref[g] + dbc_ref[g]) for g in groups]
    dt_r = [_softplus(dtr_ref[g] + dbr_ref[g]) for g in groups]
    da_c = [dt_c[g] * (-jnp.exp(alc_ref[g])) for g in groups]
    da_r = [dt_r[g] * (-jnp.exp(alr_ref[g])) for g in groups]
    acum_c = [_sel_dot(maskf, da_c[g], 3) for g in groups]
    acum_r = [_dot_sel(da_r[g], mask_tf, 3) for g in groups]
    atot = [jnp.sum(da_c[g], axis=0, keepdims=True) for g in groups]

    er = lax.broadcasted_iota(jnp.int32, (SSD_HPG, gw), 0)
    ec = lax.broadcasted_iota(jnp.int32, (SSD_HPG, gw), 1)
    expand = (ec // hd == er).astype(F32)
    wide = [_dot_sel(jnp.concatenate([dt_c[g], jnp.exp(atot[g] - acum_c[g]), jnp.exp(acum_c[g]),
                                      jnp.broadcast_to(jnp.exp(atot[g]), (SUBLANES, SSD_HPG))], axis=0), expand, 1)
            for g in groups]
    xq = [xs[g] * wide[g][0:q] for g in groups]
    cb = [_dot_nt(cm[g], bm[g]) for g in groups]
    y_off = [wide[g][2 * q:3 * q] * _dot(cm[g], h_prev[g].astype(BF16)) for g in groups]
    lane = lax.broadcasted_iota(jnp.int32, (q, 2 * hd), 1)

    def pair_lhs(g, p):
        ms = [(cb[g] * jnp.where(mask, jnp.exp(acum_c[g][:, r:r + 1] - acum_r[g][r:r + 1, :]), 0.0)).astype(BF16)
              for r in (2 * p, 2 * p + 1)]
        return jnp.concatenate(ms, axis=1)

    def pair_rhs(g, p):
        xp = xq[g][:, 2 * hd * p:2 * hd * (p + 1)]
        return jnp.concatenate([jnp.where(lane < hd, xp, 0.0), jnp.where(lane >= hd, xp, 0.0)], axis=0).astype(BF16)

    y_in = [[_dot(pair_lhs(g, p), pair_rhs(g, p)) for p in pairs] for g in groups]
    y = [y_off[g] + jnp.concatenate(y_in[g], axis=1) for g in groups]
    bt = [bm[g].astype(F32).T.astype(BF16) for g in groups]
    upd = [_dot(bt[g], (xq[g] * wide[g][q:2 * q]).astype(BF16)) for g in groups]
    for g in groups:
        h_ref[g] = wide[g][3 * q:3 * q + 1] * h_prev[g] + upd[g]

    for g in groups:
        if final:
            ytot = y[g] + yf_ref[:, cols[g]] + xs[g] * dsk_ref[:, cols[g]]
            gated = ytot * _silu(z_ref[:, cols[g]].astype(F32))
            o = gated * lax.rsqrt(jnp.mean(gated * gated, axis=-1, keepdims=True) + EPS) * nw_ref[:, cols[g]]
            o_ref[:, cols[g]] = o.astype(o_ref.dtype)
        else:
            o_ref[:, cols[g]] = y[g]


def _ssd(xbc, dt_c, dt_r, db_c, db_r, al_c, al_r, *, direction, n_lat, final_args=None):
    nb, t, _ = xbc.shape
    q, gw, hpg = SCAN_Q, SSD_GW, SSD_HPG
    nch = t // q
    reverse = direction == 1
    cmap = _chunk_map(reverse, nch, n_lat // q)
    gps = SSD_GPS
    sw, gww = gps * SSD_STATE, gps * gw
    b0 = SSD_WIDTH // sw
    c0 = b0 + SSD_GROUPS // gps
    d = direction
    in_specs = [
        pl.BlockSpec((None, q, gww), lambda b, g, k: (b, cmap(k), g)),
        pl.BlockSpec((None, q, sw), lambda b, g, k: (b, cmap(k), b0 + g)),
        pl.BlockSpec((None, q, sw), lambda b, g, k: (b, cmap(k), c0 + g)),
        pl.BlockSpec((None, None, gps, q, hpg), lambda b, g, k: (b, d, g, cmap(k), 0)),
        pl.BlockSpec((None, None, gps, hpg, q), lambda b, g, k: (b, d, g, 0, cmap(k))),
        pl.BlockSpec((None, gps, 1, hpg), lambda b, g, k: (d, g, 0, 0)),
        pl.BlockSpec((None, gps, hpg, 1), lambda b, g, k: (d, g, 0, 0)),
        pl.BlockSpec((None, gps, 1, hpg), lambda b, g, k: (d, g, 0, 0)),
        pl.BlockSpec((None, gps, hpg, 1), lambda b, g, k: (d, g, 0, 0)),
    ]
    args = [xbc, xbc, xbc, dt_c, dt_r, db_c, db_r, al_c, al_r]
    final = final_args is not None
    if final:
        yf, pr, dskip, norm_w = final_args
        in_specs += [pl.BlockSpec((None, q, gww), lambda b, g, k: (b, cmap(k), g)),
                     pl.BlockSpec((None, q, gww), lambda b, g, k: (b, cmap(k), g)),
                     pl.BlockSpec((1, gww), lambda b, g, k: (0, g)),
                     pl.BlockSpec((1, gww), lambda b, g, k: (0, g))]
        args += [yf, pr, dskip, norm_w]
    out_shape = jax.ShapeDtypeStruct((nb, t, SSD_WIDTH), BF16 if final else F32)
    return pl.pallas_call(
        functools.partial(_ssd_body, reverse=reverse, final=final),
        out_shape=out_shape, grid=(nb, SSD_GROUPS // gps, nch), in_specs=in_specs,
        out_specs=pl.BlockSpec((None, q, gww), lambda b, g, k: (b, cmap(k), g)),
        scratch_shapes=[pltpu.VMEM((gps, SSD_STATE, gw), F32)],
        name="ssd_bwd" if reverse else "ssd_fwd",
        compiler_params=_cp(("parallel", "parallel", "arbitrary")))(*args)


def _hyfilt_body(f_ref, w1_ref, b1_ref, w2_ref, b2_ref, w3_ref, fr_ref, dl_ref, o_ref, *, tl):
    f = f_ref[...]
    h = jnp.sin(fr_ref[0:1, :] * (_dot(f, w1_ref[...], HI) + b1_ref[...]))
    h = jnp.sin(fr_ref[1:2, :] * (_dot(h, w2_ref[...], HI) + b2_ref[...]))
    h = _dot(h, w3_ref[...], HI)
    dec = jnp.exp(-f[:, 0:1] * dl_ref[...])
    row = pl.program_id(0) * tl + lax.broadcasted_iota(jnp.int32, (tl, 1), 0)
    wd = HY_WIDTH
    o_ref[:, 0:wd] = (h[:, 0:wd] * dec).astype(o_ref.dtype)
    o_ref[:, wd:2 * wd] = jnp.where(row == 0, 0.0, h[:, wd:2 * wd] * dec).astype(o_ref.dtype)


def _hyena_filters(length, w1, b1, w2, b2, w3, freq, *, tl=256):
    p = LANES
    t = jnp.linspace(0.0, 1.0, length, dtype=F32)[:, None]
    w = 2.0 * math.pi * jnp.arange(length, dtype=F32)[:, None] / length
    f = jnp.linspace(1e-4, HY_BANDS - 1, HY_BANDS, dtype=F32)[None, :]
    feats = jnp.concatenate([t, jnp.cos(f * w), -jnp.sin(f * w)], axis=-1)
    feats = jnp.concatenate([feats[0::2], feats[1::2]], axis=0)
    feats = jnp.pad(feats, ((0, 0), (0, p - HY_EMB)))
    po = p - HY_ORDER
    w1p = jnp.pad(w1, ((0, p - HY_EMB), (0, po)))
    w2p = jnp.pad(w2, ((0, po), (0, po)))
    w3p = jnp.pad(w3, ((0, po), (0, 0)))
    b1p = jnp.pad(b1[None, :], ((0, 0), (0, po)))
    b2p = jnp.pad(b2[None, :], ((0, 0), (0, po)))
    frp = jnp.pad(freq, ((0, 0), (0, po)))
    deltas = jnp.abs(jnp.linspace(math.log(HY_TARGET) / HY_FAST, math.log(HY_TARGET) / HY_SLOW, HY_WIDTH,
                                  dtype=F32))[None, :]
    full = lambda shape: pl.BlockSpec(shape, lambda i: (0,) * len(shape))
    out = pl.pallas_call(
        functools.partial(_hyfilt_body, tl=tl),
        out_shape=jax.ShapeDtypeStruct((length, 2 * HY_WIDTH), BF16), grid=(length // tl,),
        in_specs=[pl.BlockSpec((tl, p), lambda i: (i, 0)), full((p, p)), full((1, p)), full((p, p)), full((1, p)),
                  full((p, 2 * HY_WIDTH)), full((2, p)), full((1, HY_WIDTH))],
        out_specs=pl.BlockSpec((tl, 2 * HY_WIDTH), lambda i: (i, 0)),
        name="hyena_filters", compiler_params=_cp(("parallel",)))(feats, w1p, b1p, w2p, b2p, w3p, frp, deltas)
    return out[None]


def _dft_matrices(length):
    n = 2 * length
    k = jnp.arange(length, dtype=jnp.int32)[:, None]
    t = jnp.arange(length, dtype=jnp.int32)[None, :]
    step = 1 << (int(math.log2(length)) // 2)
    part = jnp.arange(length // step, dtype=jnp.int32)[:, None]
    ang_hi = ((part * step * t) % n).astype(F32) * (2.0 * math.pi / n)
    ang_lo = ((jnp.arange(step, dtype=jnp.int32)[:, None] * t) % n).astype(F32) * (2.0 * math.pi / n)
    ch, sh = jnp.cos(ang_hi)[:, None, :], jnp.sin(ang_hi)[:, None, :]
    cl, sl = jnp.cos(ang_lo)[None, :, :], jnp.sin(ang_lo)[None, :, :]
    c = (ch * cl - sh * sl).reshape(length, length)
    s = (sh * cl + ch * sl).reshape(length, length)
    alt_t = jnp.where(t % 2 == 0, 1.0, -1.0).astype(F32)
    fwd = jnp.concatenate([c, jnp.where(k == 0, alt_t, -s)], axis=0)
    alt_r = jnp.where(k % 2 == 0, 1.0, -1.0).astype(F32)
    inv_re = jnp.where(t == 0, 1.0 / n, (2.0 / n) * c)
    inv_im = jnp.where(t == 0, alt_r / n, (-2.0 / n) * s)
    inv = jnp.concatenate([inv_re, inv_im], axis=1)
    return fwd.astype(BF16)[None], inv.astype(BF16)[None]


def _tile_interleave(x, axis, hm):
    shp = x.shape
    n = shp[axis] // 2
    x = x.reshape(shp[:axis] + (2, n // hm, hm) + shp[axis + 1:])
    return jnp.swapaxes(x, axis, axis + 1).reshape(shp)


def _twiddle_table(length, hm):
    k = np.arange(length // 2, dtype=np.float64)[:, None] * (np.pi / length)
    tab = np.concatenate([np.cos(k), np.sin(k)], axis=0)
    tab = tab.reshape(2, -1, hm, 1).swapaxes(0, 1).reshape(length, 1)
    return jnp.asarray(np.broadcast_to(tab, (length, LANES)).astype(np.float32))


def _lanes(x, width):
    return jnp.tile(x, (1, width // LANES))


def _parity_banks(e, ot, tw, hm):
    er, ei, otr, oti = e[0:hm], e[hm:], ot[0:hm], ot[hm:]
    c, s = _lanes(tw[0:hm], e.shape[1]), _lanes(tw[hm:], e.shape[1])
    orr = otr * c + oti * s
    oi = oti * c - otr * s
    return (er + orr, ei + oi, er - orr, ei - oi), (c, s)


def _is_bin0(row_blk, hm):
    return (row_blk == 0) & (lax.broadcasted_iota(jnp.int32, (hm, 1), 0) == 0)


def _hy_kbank_body(ef_ref, eb_ref, of_ref, ob_ref, tw_ref, ka_ref, kb_ref, *, hm):
    tw = tw_ref[...]
    (afr, afi, bfr, bfi), _ = _parity_banks(ef_ref[...], of_ref[...], tw, hm)
    (abr, abi, bbr, bbi), _ = _parity_banks(eb_ref[...], ob_ref[...], tw, hm)
    first = _is_bin0(pl.program_id(0), hm)
    kai = jnp.where(first, ef_ref[hm:, :] + eb_ref[hm:, :], afi - abi)
    kbi = jnp.where(first, ob_ref[hm:, :] - of_ref[hm:, :], bfi - bbi)
    ka_ref[...] = jnp.concatenate([afr + abr, kai], axis=0)
    kb_ref[...] = jnp.concatenate([bfr + bbr, kbi], axis=0)


def _hy_kbank(ef, of, tw, *, tm, tc=512):
    _, ln, c2 = ef.shape
    c = c2 // 2
    fspec = pl.BlockSpec((None, tm, tc), lambda i, j: (0, i, j))
    bspec = pl.BlockSpec((None, tm, tc), lambda i, j: (0, i, j + c // tc))
    return pl.pallas_call(
        functools.partial(_hy_kbank_body, hm=tm // 2),
        out_shape=(jax.ShapeDtypeStruct((1, ln, c), F32),) * 2, grid=(ln // tm, c // tc),
        in_specs=[fspec, bspec, fspec, bspec, pl.BlockSpec((tm, LANES), lambda i, j: (i, 0))],
        out_specs=(fspec, fspec), name="hy_kbank",
        compiler_params=_cp(("parallel", "parallel")))(ef, ef, of, of, tw)


def _hy_spectrum_epilogue(acc, row_blk, e_ref, tw_ref, ka_ref, kb_ref, *, hm):
    e = e_ref[...]
    (ar, ai, br, bi), (c, s) = _parity_banks(e, acc, tw_ref[...], hm)
    kar, kai, kbr, kbi = ka_ref[0:hm], ka_ref[hm:], kb_ref[0:hm], kb_ref[hm:]
    yar, yai = ar * kar - ai * kai, ar * kai + ai * kar
    ybr, ybi = br * kbr - bi * kbi, br * kbi + bi * kbr
    qr, qi = yar - ybr, yai - ybi
    first = _is_bin0(row_blk, hm)
    y0, yl = ar * kar, br * kbr
    amr, ami = e[hm:], -acc[hm:]
    ymr, ymi = amr * kai - ami * kbi, amr * kbi + ami * kai
    pr = jnp.where(first, y0 + yl, yar + ybr)
    pi = jnp.where(first, 2.0 * ymr, yai + ybi)
    qpr = jnp.where(first, y0 - yl, qr * c - qi * s)
    qpi = jnp.where(first, -2.0 * ymi, qi * c + qr * s)
    return jnp.concatenate([pr, pi], axis=0), jnp.concatenate([qpr, qpi], axis=0)


def _hyena_out_body(*refs, segs):
    ns = len(segs)
    x0_ref, db_ref, o_ref = refs[3 * ns:3 * ns + 3]
    ys_refs = refs[3 * ns + 3:]
    half = CONV_CH // 2
    for si, (r0, n) in enumerate(segs):
        ye_ref, yo_ref, zin_ref = refs[3 * si:3 * si + 3]

        def body(i, carry, r0=r0, ye_ref=ye_ref, yo_ref=yo_ref, zin_ref=zin_ref):
            t0 = pl.multiple_of(i * CONV_CH, CONV_CH)
            hrows = pl.ds(pl.multiple_of(i * half, half), half)
            for g, ys_ref in enumerate(ys_refs):
                lanes = slice(g * LANES, (g + 1) * LANES)
                ys_ref[pl.ds(r0 + t0, half, stride=2), :] = ye_ref[hrows, lanes].astype(F32)
                ys_ref[pl.ds(r0 + t0 + 1, half, stride=2), :] = yo_ref[hrows, lanes].astype(F32)
            rows = pl.ds(r0 + t0, CONV_CH)
            y = jnp.concatenate([ys_ref[rows, :] for ys_ref in ys_refs], axis=1)
            zin = zin_ref[pl.ds(t0, CONV_CH), :].astype(F32)
            o_ref[rows, :] = (x0_ref[rows, :].astype(F32) * (y + zin * db_ref[...])).astype(o_ref.dtype)
            return carry

        lax.fori_loop(0, n // CONV_CH, body, 0, unroll=min(4, n // CONV_CH))


def _hyena_out(parts, x0, dbias, *, segs, tc=256):
    nb, t, c = x0.shape
    in_specs, args = [], []
    for (ye, yo, zin), (_, n) in zip(parts, segs):
        in_specs += [pl.BlockSpec((None, n // 2, tc), lambda b, j: (b, 0, j)),
                     pl.BlockSpec((None, n // 2, tc), lambda b, j: (b, 0, j)),
                     pl.BlockSpec((None, n, tc), lambda b, j: (b, 0, j))]
        args += [ye, yo, zin]
    in_specs += [pl.BlockSpec((None, t, tc), lambda b, j: (b, 0, j)), pl.BlockSpec((1, tc), lambda b, j: (0, j))]
    return pl.pallas_call(
        functools.partial(_hyena_out_body, segs=segs),
        out_shape=jax.ShapeDtypeStruct((nb, t, c), BF16), grid=(nb, c // tc), in_specs=in_specs,
        out_specs=pl.BlockSpec((None, t, tc), lambda b, j: (b, 0, j)),
        scratch_shapes=[pltpu.VMEM((t, LANES), F32)] * (tc // LANES), name="hyena_out",
        compiler_params=_cp(("parallel", "parallel")))(*args, x0, dbias)


def _hyena_longconv(ze, zo, hfb, *, tn=1024):
    nb, h, c = ze.shape
    ln = 2 * h
    tm_s = min(512, ln)
    hm = tm_s // 2
    fwd, inv = _dft_matrices(h)
    fwd = _tile_interleave(fwd, 1, hm)
    inv = _tile_interleave(inv * 0.5, 2, hm)
    tw = _twiddle_table(ln, hm)
    tm_f, tk_f = min(1024, ln), min(2048, h)
    ef = _matmul(fwd, hfb[:, :h], tm=tm_f, tn=tn, tk=tk_f, out_dtype=F32, name="hy_filter_dft_even")
    of = _matmul(fwd, hfb[:, h:], tm=tm_f, tn=tn, tk=tk_f, out_dtype=F32, name="hy_filter_dft_odd")
    ka, kb = _hy_kbank(ef, of, tw, tm=tm_s)
    e = _matmul(fwd, ze, tm=tm_f, tn=tn, tk=tk_f, out_dtype=F32, name="hy_dft_even")
    tile = pl.BlockSpec((None, tm_s, tn), lambda j, b, i, k: (b, i, j))
    ktile = pl.BlockSpec((None, tm_s, tn), lambda j, b, i, k: (0, i, j))
    extra = [(e, tile), (tw, pl.BlockSpec((tm_s, LANES), lambda j, b, i, k: (i, 0))), (ka, ktile), (kb, ktile)]
    p, q = _matmul(fwd, zo, tm=tm_s, tn=tn, tk=tk_f, out_dtype=BF16, extra=extra, n_out=2,
                   epilogue=functools.partial(_hy_spectrum_epilogue, hm=hm), name="hy_dft_odd")
    tm_i, tk_i = min(1024, h), min(2048, ln)
    ye = _matmul(inv, p, tm=tm_i, tn=tn, tk=tk_i, out_dtype=BF16, name="hy_idft_even")
    yo = _matmul(inv, q, tm=tm_i, tn=tn, tk=tk_i, out_dtype=BF16, name="hy_idft_odd")
    return ye, yo


def _mlstm_body(*refs, reverse, final):
    q_ref, k_ref, v_ref, gc_ref, gr_ref = refs[:5]
    if final:
        hf_ref, og_ref, nw_ref, o_ref, ct_ref, n_ref, m_ref = refs[5:]
    else:
        o_ref, ct_ref, n_ref, m_ref = refs[5:]
    nh, dk, dv = ML_HEADS, ML_QK_DIM, ML_V_DIM
    scale = dk ** -0.5

    @pl.when(pl.program_id(1) == 0)
    def _():
        ct_ref[...] = jnp.zeros_like(ct_ref)
        n_ref[...] = jnp.zeros_like(n_ref)
        m_ref[...] = jnp.zeros_like(m_ref)

    gc, gr = gc_ref[...], gr_ref[...]
    li_c, lf_c = gc[:, 0:nh], _log_sigmoid(gc[:, nh:2 * nh])
    li_r, lf_r = gr[0:nh, :], _log_sigmoid(gr[nh:2 * nh, :])
    mask, maskf, mask_tf = _scan_masks(reverse)
    bc_c = _dot(maskf, lf_c, HI)
    bc_r = _dot(lf_r, mask_tf, HI)
    btot_c = jnp.sum(lf_c, axis=0, keepdims=True)
    n_all, m_all = n_ref[...], m_ref[...]
    heads = range(nh)
    qs = [q_ref[:, dk * h:dk * (h + 1)] for h in heads]
    ks = [k_ref[:, dk * h:dk * (h + 1)] for h in heads]
    vs = [v_ref[:, dv * h:dv * (h + 1)] for h in heads]
    cts = [ct_ref[h] for h in heads]
    m_s = [m_all[h:h + 1, 0:1] for h in heads]
    n_s = [n_all[h:h + 1, :] for h in heads]
    qk = [_dot_nt(qs[h], ks[h]) for h in heads]
    qc = [_dot(qs[h], cts[h].astype(BF16)) for h in heads]
    dmat = [jnp.where(mask, bc_c[:, h:h + 1] - bc_r[h:h + 1, :] + li_r[h:h + 1, :], -jnp.inf) for h in heads]
    inter = [bc_c[:, h:h + 1] + m_s[h] for h in heads]
    m_t = [jnp.maximum(inter[h], jnp.max(dmat[h], axis=1, keepdims=True)) for h in heads]
    s = [qk[h] * scale * jnp.exp(dmat[h] - m_t[h]) for h in heads]
    dec = [jnp.exp(inter[h] - m_t[h]) for h in heads]
    num = [_dot(s[h].astype(BF16), vs[h]) + (dec[h] * scale) * qc[h] for h in heads]
    qn = [jnp.sum(qs[h].astype(F32) * n_s[h], axis=1, keepdims=True) * scale for h in heads]
    den = [jnp.sum(s[h], axis=1, keepdims=True) + dec[h] * qn[h] for h in heads]
    hh = [num[h] / jnp.maximum(jnp.abs(den[h]), jnp.exp(-m_t[h])) for h in heads]

    wst_c = [btot_c[:, h:h + 1] - bc_c[:, h:h + 1] + li_c[:, h:h + 1] for h in heads]
    wst_r = [btot_c[:, h:h + 1] - bc_r[h:h + 1, :] + li_r[h:h + 1, :] for h in heads]
    m_new = [jnp.maximum(btot_c[:, h:h + 1] + m_s[h], jnp.max(wst_r[h], axis=1, keepdims=True)) for h in heads]
    dstate = [jnp.exp(btot_c[:, h:h + 1] + m_s[h] - m_new[h]) for h in heads]
    kw = [ks[h].astype(F32) * jnp.exp(wst_c[h] - m_new[h]) for h in heads]
    upd = [_dot(kw[h].T.astype(BF16), vs[h]) for h in heads]
    for h in heads:
        ct_ref[h] = dstate[h] * cts[h] + upd[h]
    n_rows = [dstate[h] * n_s[h] + jnp.sum(kw[h], axis=0, keepdims=True) for h in heads]
    m_rows = [jnp.broadcast_to(m_new[h], (1, LANES)) for h in heads]

    for h in heads:
        cols = slice(dv * h, dv * (h + 1))
        if final:
            hs = hh[h] + hf_ref[:, cols]
            hn = hs * lax.rsqrt(jnp.mean(hs * hs, axis=-1, keepdims=True) + EPS) * nw_ref[:, cols]
            o_ref[:, cols] = (hn * _sigmoid(og_ref[:, cols].astype(F32))).astype(o_ref.dtype)
        else:
            o_ref[:, cols] = hh[h]

    n_ref[...] = jnp.concatenate(n_rows, axis=0)
    m_ref[...] = jnp.concatenate(m_rows, axis=0)


def _mlstm(qk, pr, g_c, g_r, *, direction, n_lat, final_args=None):
    nb, t, _ = qk.shape
    q = SCAN_Q
    nch = t // q
    reverse = direction == 1
    cmap = _chunk_map(reverse, nch, n_lat // q)
    d = direction
    in_specs = [
        pl.BlockSpec((None, q, ML_QK_WIDTH), lambda b, k: (b, cmap(k), 0)),
        pl.BlockSpec((None, q, ML_QK_WIDTH), lambda b, k: (b, cmap(k), 1)),
        pl.BlockSpec((None, q, ML_WIDTH), lambda b, k: (b, cmap(k), 2 * ML_QK_WIDTH // ML_WIDTH)),
        pl.BlockSpec((None, None, q, 2 * ML_HEADS), lambda b, k: (b, d, cmap(k), 0)),
        pl.BlockSpec((None, None, 2 * ML_HEADS, q), lambda b, k: (b, d, 0, cmap(k))),
    ]
    args = [qk, qk, pr, g_c, g_r]
    final = final_args is not None
    if final:
        hf, norm_w = final_args
        in_specs += [pl.BlockSpec((None, q, ML_WIDTH), lambda b, k: (b, cmap(k), 0)),
                     pl.BlockSpec((None, q, ML_WIDTH), lambda b, k: (b, cmap(k), 2 * ML_QK_WIDTH // ML_WIDTH + 1)),
                     pl.BlockSpec((1, ML_WIDTH), lambda b, k: (0, 0))]
        args += [hf, pr, norm_w]
    out_shape = jax.ShapeDtypeStruct((nb, t, ML_WIDTH), BF16 if final else F32)
    return pl.pallas_call(
        functools.partial(_mlstm_body, reverse=reverse, final=final),
        out_shape=out_shape, grid=(nb, nch), in_specs=in_specs,
        out_specs=pl.BlockSpec((None, q, ML_WIDTH), lambda b, k: (b, cmap(k), 0)),
        scratch_shapes=[pltpu.VMEM((ML_HEADS, ML_QK_DIM, ML_V_DIM), F32), pltpu.VMEM((ML_HEADS, ML_QK_DIM), F32),
                        pltpu.VMEM((ML_HEADS, LANES), F32)],
        name="mlstm_bwd" if reverse else "mlstm_fwd",
        compiler_params=_cp(("parallel", "arbitrary")))(*args)


def _na_window_start(j, rows):
    return jnp.clip(j * NA_QROWS - NA_ROWS // 2, 0, rows - NA_KROWS)


def _na_body(q_ref, k_ref, v_ref, tab_ref, qw_ref, kw_ref, o_ref, kn_ref, *, n_lat, n_ctx):
    dh = NA_HEAD_DIM
    scale = dh ** -0.5
    rows = n_lat // GRID_W
    j = pl.program_id(2)
    nq, nk = NA_QROWS * GRID_W, NA_KROWS * GRID_W

    @pl.when(j == 0)
    def _():
        def body(i, carry):
            r = pl.ds(pl.multiple_of(i * LANES, LANES), LANES)
            kk = k_ref[r, :].astype(F32)
            kn = kk * lax.rsqrt(jnp.mean(kk * kk, axis=-1, keepdims=True) + EPS) * kw_ref[...]
            kn_ref[r, :] = kn.astype(kn_ref.dtype)
            return carry

        lax.fori_loop(0, (n_lat + n_ctx) // LANES, body, 0)

    start = _na_window_start(j, rows)
    win = pl.ds(pl.multiple_of(start * GRID_W, GRID_W), nk)
    ctx = pl.ds(n_lat, n_ctx)
    parts = range(NA_PARTS)
    rpp = NA_QROWS // NA_PARTS
    pq = rpp * GRID_W
    qq = [q_ref[i * pq:(i + 1) * pq, :].astype(F32) for i in parts]
    qn = [(qq[i] * lax.rsqrt(jnp.mean(qq[i] * qq[i], axis=-1, keepdims=True) + EPS)
           * (qw_ref[...] * scale)).astype(BF16) for i in parts]
    sw = [_dot_nt(qn[i], kn_ref[win, :]) for i in parts]
    sc = [_dot_nt(qn[i], kn_ref[ctx, :]) for i in parts]

    def with_bias(i):
        out = []
        for r in range(rpp):
            d0 = start - (j * NA_QROWS + i * rpp + r) + NA_ROWS - 1 + NA_QROWS
            out.append(sw[i][r * GRID_W:(r + 1) * GRID_W] + tab_ref[d0])
        return jnp.concatenate(out, axis=0)

    sb = [with_bias(i) for i in parts]
    m = [jnp.maximum(jnp.max(sb[i], axis=1, keepdims=True), jnp.max(sc[i], axis=1, keepdims=True)) for i in parts]
    pw = [jnp.exp(sb[i] - m[i]) for i in parts]
    pc = [jnp.exp(sc[i] - m[i]) for i in parts]
    denom = [jnp.sum(pw[i], axis=1, keepdims=True) + jnp.sum(pc[i], axis=1, keepdims=True) for i in parts]
    o = [_dot(pw[i].astype(BF16), v_ref[win, :]) + _dot(pc[i].astype(BF16), v_ref[ctx, :]) for i in parts]
    for i in parts:
        o_ref[i * pq:(i + 1) * pq, :] = (o[i] / denom[i]).astype(o_ref.dtype)


def _na_bias_table(rpb, rows):
    nd = 2 * NA_QROWS
    valid = np.zeros((nd, NA_KROWS), bool)
    seen = np.zeros((nd,), bool)
    for j in range(rows // NA_QROWS):
        start = int(np.clip(j * NA_QROWS - NA_ROWS // 2, 0, rows - NA_KROWS))
        for qr in range(NA_QROWS):
            r = j * NA_QROWS + qr
            rs = int(np.clip(r - NA_ROWS // 2, 0, rows - NA_ROWS))
            d = start - r + NA_ROWS - 1 + NA_QROWS
            v = np.array([(rs <= start + i < rs + NA_ROWS) for i in range(NA_KROWS)])
            assert 0 <= d < nd and (not seen[d] or (valid[d] == v).all())
            valid[d], seen[d] = v, True
    col = np.arange(GRID_W)
    cs = np.clip(col - NA_COLS // 2, 0, GRID_W - NA_COLS)
    in_win = (col[None, :] >= cs[:, None]) & (col[None, :] < cs[:, None] + NA_COLS)
    ndr, ndc, w = 2 * NA_ROWS - 1, 2 * NA_COLS - 1, GRID_W
    dr = (np.arange(nd)[:, None] - NA_QROWS) + np.arange(NA_KROWS)[None, :]
    ok = valid[:, None, :, None] & in_win[None, :, None, :] & (dr >= 0)[:, None, :, None] \
        & (dr < ndr)[:, None, :, None]
    ext = jnp.pad(rpb, ((0, 0), (0, 0), (w, w)), mode="edge")
    c0 = w + NA_COLS - 1
    t1 = jnp.stack([ext[:, :, c0 - qc:c0 - qc + w] for qc in range(w)], axis=2)
    t1p = jnp.pad(t1, ((0, 0), (NA_QROWS, nd + NA_KROWS - 1 - NA_QROWS - ndr), (0, 0), (0, 0)))
    wide = jnp.concatenate([t1p[:, m] for m in range(nd + NA_KROWS - 1)], axis=-1)
    vals = jnp.stack([wide[:, :, d * w:(d + NA_KROWS) * w] for d in range(nd)], axis=1)
    return jnp.where(ok.reshape(nd, w, NA_KROWS * w)[None], vals, -jnp.inf).astype(F32)


def _na(pr, tab, q_w, k_w, *, n_lat, n_ctx, qcol):
    nb, t, _ = pr.shape
    dh, nh = NA_HEAD_DIM, NA_HEADS
    nq = NA_QROWS * GRID_W
    qb = qcol // dh
    nd = tab.shape[1]
    return pl.pallas_call(
        functools.partial(_na_body, n_lat=n_lat, n_ctx=n_ctx),
        out_shape=jax.ShapeDtypeStruct((nb, n_lat, NA_WIDTH), BF16), grid=(nb, nh, n_lat // nq),
        in_specs=[pl.BlockSpec((None, nq, dh), lambda b, h, j: (b, j, qb + h)),
                  pl.BlockSpec((None, t, dh), lambda b, h, j: (b, 0, qb + nh + h)),
                  pl.BlockSpec((None, t, dh), lambda b, h, j: (b, 0, qb + 2 * nh + h)),
                  pl.BlockSpec((None, nd, GRID_W, NA_KROWS * GRID_W), lambda b, h, j: (h, 0, 0, 0)),
                  pl.BlockSpec((1, dh), lambda b, h, j: (0, 0)),
                  pl.BlockSpec((1, dh), lambda b, h, j: (0, 0))],
        out_specs=pl.BlockSpec((None, nq, dh), lambda b, h, j: (b, j, h)),
        scratch_shapes=[pltpu.VMEM((t, dh), BF16)], name="na",
        compiler_params=_cp(("parallel", "parallel", "arbitrary")))(pr, pr, pr, tab, q_w, k_w)


def _rope_tables(n_lat, n_ctx, width):
    dh = ML_QK_DIM
    nf = dh // 4
    t = jnp.arange(n_lat)
    inv = ROPE_THETA ** (-jnp.arange(nf, dtype=F32) / nf)
    lane = np.arange(dh)
    pos = jnp.where((lane < dh // 2)[None, :], (t // GRID_W)[:, None], (t % GRID_W)[:, None]).astype(F32)
    ang = pos * inv[lane % nf][None, :]
    sign = np.where((lane % (dh // 2)) < nf, -1.0, 1.0).astype(np.float32)
    cos = jnp.concatenate([jnp.cos(ang), jnp.ones((n_ctx, dh), F32)], axis=0)
    sin = jnp.concatenate([jnp.sin(ang) * sign[None, :], jnp.zeros((n_ctx, dh), F32)], axis=0)
    reps = width // dh
    return jnp.tile(cos, (1, reps)), jnp.tile(sin, (1, reps))


def _ada_mod(c, c_ctx, w_all, b_all, layer):
    nb, d = c.shape
    rows = 16
    cv = jnp.concatenate([c, c_ctx[None, :], jnp.zeros((rows - nb - 1, d), F32)], axis=0)
    a = (cv * _sigmoid(cv)).astype(BF16)[None]
    out = _matmul(a, w_all, w_index=layer, tm=rows, tn=1024, tk=d, out_dtype=F32, name="ada_mod")[0]
    out = out[:nb + 1] + b_all[layer][None, :]
    return out.reshape(nb + 1, 6, d).transpose(1, 0, 2)


def _lat_ctx(m, nb):
    ctx = jnp.broadcast_to(m[nb][None, :], (nb, m.shape[1]))
    return jnp.stack([m[:nb], ctx], axis=1)[:, :, None, :]


def _pad_cols(w, n):
    return jnp.pad(w, ((0, 0), (0, n - w.shape[1])))


def _col_row_forms(g, groups):
    nb, t, _ = g.shape
    g5 = g.reshape(nb, t, 2, groups, -1)
    return g5.transpose(0, 2, 3, 1, 4), g5.transpose(0, 2, 3, 4, 1)


ROW_TILES = (1088, 1024, 512, 256)


def _row_tile(rows):
    return next(tm for tm in ROW_TILES if rows % tm == 0)


def _mod_pair(mods, shift, scale, nb):
    return jnp.stack([_lat_ctx(mods[shift], nb), _lat_ctx(mods[scale], nb)], axis=1)


def _out_projection(a_list, w_all, x, gate, layer, *, rows, n_lat, next_norm, name):
    tm = _row_tile(rows)
    x = _matmul_gated_residual_rows(a_list, w_all, x, gate, w_index=layer, tm=tm, tn=256, rows=rows, n_lat=n_lat,
                                    name=name)
    if next_norm is None:
        return x, None
    return x, _rmsmod(x, *next_norm, tm=tm, rows=rows, n_lat=n_lat)


def _conv_ffn_block(x, u, mods, w_up_all, conv_w, conv_b, w_down_all, layer, *, rows, n_lat, segs, next_norm):
    nb = x.shape[0]
    up = _matmul(u, w_up_all, w_index=layer, tm=_row_tile(rows), tn=1024, tk=D_MODEL, out_dtype=BF16, name="ffn_up")
    hid = _conv_ffn(up, conv_w, conv_b[None, :], segs=segs, rows=rows)
    return _out_projection((hid,), w_down_all, x, _lat_ctx(mods[5], nb), layer, rows=rows, n_lat=n_lat,
                           next_norm=next_norm, name="ffn_down")


def _even_layer(x, u, mods, w_in_all, conv_w, conv_b, dt_bias, a_log, d_skip, ssd_norm_w,
                hy_w1, hy_b1, hy_w2, hy_b2, hy_w3, hy_freq, hy_bias, w_out_all, j, *, n_lat, n_ctx, next_norm):
    nb, t, d = x.shape
    segs = ((0, n_lat), (n_lat, n_ctx))
    tm = _row_tile(t)
    n_main = SSD_WIDTH + EV_CONV_CH
    pr = _matmul(u, w_in_all, w_index=j, cols=n_main, tm=tm, tn=1024, tk=d, out_dtype=BF16, name="ev_in")
    w_dt = _pad_cols(w_in_all[j, :, n_main:], LANES).astype(BF16)
    dtp = _matmul(u, w_dt[None], tm=tm, tn=LANES, tk=d, out_dtype=F32, name="ev_in_dt")[:, :, :2 * SSD_HEADS]
    cb = conv_b[None, :]
    xbc = _conv_silu(pr, conv_w, cb, xcol=SSD_WIDTH, wcol=0, width=SSD_XBC, segs=segs)
    x0, zin_l, zin_c, zle, zlo, zce, zco = _conv_hyena(pr, conv_w, cb, xcol=SSD_WIDTH + SSD_XBC, wcol=SSD_XBC,
                                                       segs=segs)

    dt_c, dt_r = _col_row_forms(dtp, SSD_GROUPS)
    db = dt_bias.reshape(2, SSD_GROUPS, 1, SSD_HPG)
    al = a_log.reshape(2, SSD_GROUPS, 1, SSD_HPG)
    scan_args = (xbc, dt_c, dt_r, db, db.transpose(0, 1, 3, 2), al, al.transpose(0, 1, 3, 2))
    yf = _ssd(*scan_args, direction=0, n_lat=n_lat)
    dsk = jnp.repeat(d_skip, SSD_HEAD_DIM)[None, :]
    y_ssd = _ssd(*scan_args, direction=1, n_lat=n_lat, final_args=(yf, pr, dsk, ssd_norm_w[None, :]))

    hyb = hy_bias[None, :]
    filt = functools.partial(_hyena_filters, w1=hy_w1, b1=hy_b1, w2=hy_w2, b2=hy_b2, w3=hy_w3, freq=hy_freq)
    lat = _hyena_longconv(zle, zlo, filt(n_lat)) + (zin_l,)
    ctx = _hyena_longconv(zce, zco, filt(n_ctx)) + (zin_c,)
    y_hy = _hyena_out((lat, ctx), x0, hyb, segs=segs)
    return _out_projection((y_ssd, y_hy), w_out_all, x, _lat_ctx(mods[2], nb), j, rows=t, n_lat=n_lat,
                           next_norm=next_norm, name="ev_out")


def _odd_layer_latent(x, u, mods, w_in, conv_w, conv_b, gate_b, ml_norm_w, q_norm_w, k_norm_w, rpb,
                      w_out_all, j, *, n_lat, n_ctx, next_norm):
    nb, t, d = x.shape
    segs = ((0, n_lat), (n_lat, n_ctx))
    tm = _row_tile(t)
    o3 = 2 * ML_QK_WIDTH + 2 * ML_WIDTH
    o4 = o3 + 4 * ML_HEADS
    wb = w_in.astype(BF16)
    w_main = jnp.concatenate([wb[:, :o3], wb[:, o4:]], axis=1)
    pr = _matmul(u, w_main[None], tm=tm, tn=1024, tk=d, out_dtype=BF16, name="od_in")
    gates = _matmul(u, _pad_cols(wb[:, o3:o4], LANES)[None], tm=tm, tn=LANES, tk=d, out_dtype=F32,
                    name="od_in_gates")[:, :, :4 * ML_HEADS] + gate_b.reshape(-1)
    cos, sin = _rope_tables(n_lat, n_ctx, 256)
    qk = _conv_rope(pr, conv_w, conv_b[None, :], cos, sin, width=2 * ML_QK_WIDTH, segs=segs)
    g5 = gates.reshape(nb, t, 2, 2 * ML_HEADS)
    g_c, g_r = g5.transpose(0, 2, 1, 3), g5.transpose(0, 2, 3, 1)
    hf = _mlstm(qk, pr, g_c, g_r, direction=0, n_lat=n_lat)
    y_ml = _mlstm(qk, pr, g_c, g_r, direction=1, n_lat=n_lat, final_args=(hf, ml_norm_w[None, :]))
    tab = _na_bias_table(rpb, n_lat // GRID_W)
    y_na = _na(pr, tab, q_norm_w[None, :], k_norm_w[None, :], n_lat=n_lat, n_ctx=n_ctx, qcol=o3)
    return _out_projection((y_ml, y_na), w_out_all, x, _lat_ctx(mods[2], nb), j, rows=n_lat, n_lat=n_lat,
                           next_norm=next_norm, name="od_out")


def kernel(x, c, ctx, c_ctx, ada_w, ada_b, norm_w, ev_w_in, ev_conv_w, ev_conv_b, ssd_dt_bias, ssd_a_log, ssd_d,
           ssd_norm_w, hy_w1, hy_b1, hy_w2, hy_b2, hy_w3, hy_freq, hy_bias, ev_w_out, od_w_in, ml_conv_w, ml_conv_b,
           ml_gate_b, ml_norm_w, na_q_norm_w, na_k_norm_w, na_rpb, od_w_out, ffn_w_up, ffn_conv_w, ffn_conv_b,
           ffn_w_down):
    depth = ada_w.shape[0]
    assert depth == 2, "layer schedule below is written for one even and one odd layer"
    n_lat, n_ctx = x.shape[1], ctx.shape[1]
    t = n_lat + n_ctx
    both = ((0, n_lat), (n_lat, n_ctx))
    xs = jnp.concatenate([x, ctx], axis=1)

    nb = x.shape[0]
    w_down = ffn_w_down.astype(BF16)
    mods0 = _ada_mod(c, c_ctx, ada_w, ada_b, 0)
    mods1 = _ada_mod(c, c_ctx, ada_w, ada_b, 1)
    norm = lambda layer, sub, mods: (norm_w[layer, sub][None, :], _mod_pair(mods, 3 * sub, 3 * sub + 1, nb))

    w0, mod0 = norm(0, 0, mods0)
    u = _rmsmod(xs, w0, mod0, tm=_row_tile(t), rows=t, n_lat=n_lat)
    xs, u = _even_layer(xs, u, mods0, ev_w_in, ev_conv_w[0], ev_conv_b[0], ssd_dt_bias[0], ssd_a_log[0],
                        ssd_d[0], ssd_norm_w[0], hy_w1[0], hy_b1[0], hy_w2[0], hy_b2[0], hy_w3[0], hy_freq[0],
                        hy_bias[0], ev_w_out.astype(BF16), 0, n_lat=n_lat, n_ctx=n_ctx, next_norm=norm(0, 1, mods0))
    xs, u = _conv_ffn_block(xs, u, mods0, ffn_w_up, ffn_conv_w[0], ffn_conv_b[0], w_down, 0,
                            rows=t, n_lat=n_lat, segs=both, next_norm=norm(1, 0, mods1))
    xl, u = _odd_layer_latent(xs, u, mods1, od_w_in[0], ml_conv_w[0], ml_conv_b[0], ml_gate_b[0], ml_norm_w[0],
                              na_q_norm_w[0], na_k_norm_w[0], na_rpb[0], od_w_out.astype(BF16), 0,
                              n_lat=n_lat, n_ctx=n_ctx, next_norm=norm(1, 1, mods1))
    out, _ = _conv_ffn_block(xl, u, mods1, ffn_w_up, ffn_conv_w[1], ffn_conv_b[1], w_down, 1,
                             rows=n_lat, n_lat=n_lat, segs=((0, n_lat),), next_norm=None)
    return out
```

```python
import functools
import math

import numpy as np
import jax
import jax.numpy as jnp
from jax import lax
from jax.experimental import pallas as pl
from jax.experimental.pallas import tpu as pltpu

F32 = jnp.float32
BF16 = jnp.bfloat16
HI = lax.Precision.HIGHEST

D_MODEL = 2048
SEQ = 4096
CTX_LEN = 256
GRID_W = 64
EPS = 1e-6
CONV_W = 3
SSD_WIDTH = D_MODEL
SSD_HEAD_DIM = 64
SSD_HEADS = SSD_WIDTH // SSD_HEAD_DIM
SSD_STATE = 128
SSD_GROUPS = 4
SSD_HPG = SSD_HEADS // SSD_GROUPS
SSD_GW = SSD_WIDTH // SSD_GROUPS
SSD_XBC = SSD_WIDTH + 2 * SSD_GROUPS * SSD_STATE
HY_WIDTH = D_MODEL
HY_EMB = 33
HY_BANDS = (HY_EMB - 1) // 2
HY_ORDER = 64
HY_FAST = 0.3
HY_SLOW = 1.5
HY_TARGET = 1e-2
ML_HEADS = 8
ML_WIDTH = D_MODEL
ML_V_DIM = ML_WIDTH // ML_HEADS
ML_QK_DIM = ML_V_DIM // 2
ML_QK_WIDTH = ML_HEADS * ML_QK_DIM
ROPE_THETA = 10000.0
NA_WIDTH = D_MODEL
NA_HEAD_DIM = 128
NA_HEADS = NA_WIDTH // NA_HEAD_DIM
NA_ROWS = 8
NA_COLS = 16
NA_QROWS = 8
NA_KROWS = 16
NA_PARTS = 2
D_FF = 256 * ((8 * D_MODEL // 3 + 255) // 256)
EV_CONV_CH = SSD_XBC + 3 * HY_WIDTH
SCAN_Q = 128
LANES = 128
SUBLANES = 8
VMEM_LIMIT = 48 * 1024 * 1024


def _cp(sem, vmem=VMEM_LIMIT):
    return pltpu.CompilerParams(dimension_semantics=sem, vmem_limit_bytes=vmem)


def _sigmoid(x):
    return 1.0 / (1.0 + jnp.exp(-x))


def _silu(x):
    return x * _sigmoid(x)


def _softplus(x):
    return jnp.maximum(x, 0.0) + jnp.log(1.0 + jnp.exp(-jnp.abs(x)))


def _log_sigmoid(x):
    return -_softplus(-x)


def _dot(a, b, precision=None):
    return jnp.dot(a, b, preferred_element_type=F32, precision=precision)


def _dot_nt(a, b):
    return lax.dot_general(a, b, (((1,), (1,)), ((), ())), preferred_element_type=F32)


def _split_bf16(x, terms):
    out = []
    for _ in range(terms):
        part = x.astype(BF16)
        out.append(part)
        x = x - part.astype(F32)
    return out


def _dot_sel(x, sel, terms):
    selb = sel.astype(BF16)
    return sum(_dot(part, selb) for part in _split_bf16(x, terms))


def _sel_dot(sel, x, terms):
    selb = sel.astype(BF16)
    return sum(_dot(selb, part) for part in _split_bf16(x, terms))


def _mm_body(*refs, nks, n_extra, epilogue, cache_w, n_out):
    n_a = len(nks)
    a_refs, w_ref = refs[:n_a], refs[n_a]
    extra = refs[n_a + 1:n_a + 1 + n_extra]
    pos = n_a + 1 + n_extra
    o_refs = refs[pos:pos + n_out]
    scratch = refs[pos + n_out:]
    nk = sum(nks)
    row_blk = pl.program_id(2)
    if cache_w:
        wb_ref = scratch[0]

        @pl.when((pl.program_id(1) == 0) & (row_blk == 0))
        def _():
            wb_ref[...] = w_ref[...].astype(BF16)

        w_val = wb_ref[...]
    else:
        w_val = w_ref[...].astype(BF16)

    def finish(acc):
        vals = acc if epilogue is None else epilogue(acc, row_blk, *extra)
        for o_ref, val in zip(o_refs, vals if n_out > 1 else (vals,)):
            o_ref[...] = val.astype(o_ref.dtype)

    if nk == 1:
        finish(_dot(a_refs[0][...].astype(BF16), w_val))
        return
    acc_ref = scratch[0]
    k = pl.program_id(3)
    lo = 0
    for a_ref, steps in zip(a_refs, nks):
        hi = lo + steps
        if lo == 0:
            @pl.when(k == 0)
            def _(a_ref=a_ref):
                acc_ref[...] = _dot(a_ref[...].astype(BF16), w_val)

        if hi > max(lo, 1):
            @pl.when((k >= max(lo, 1)) & (k < hi))
            def _(a_ref=a_ref):
                acc_ref[...] += _dot(a_ref[...].astype(BF16), w_val)

        lo = hi

    @pl.when(k == nk - 1)
    def _():
        finish(acc_ref[...])


def _matmul(a, w, *, tm, tn, tk, out_dtype, rows=None, cols=None, w_index=None, extra=(), epilogue=None,
            n_out=1, name="mm"):
    a_list = a if isinstance(a, (tuple, list)) else (a,)
    ba, ta = a_list[0].shape[0], min(x.shape[1] for x in a_list)
    bw, kdim, n = w.shape
    assert sum(x.shape[2] for x in a_list) == kdim and all(x.shape[2] % tk == 0 for x in a_list)
    nb = ba if w_index is not None else max(ba, bw)
    rows = ta if rows is None else rows
    n = n if cols is None else cols
    assert rows % tm == 0 and n % tn == 0
    nks = tuple(x.shape[2] // tk for x in a_list)
    ni, nj, nk = rows // tm, n // tn, sum(nks)
    in_specs, lo = [], 0
    for steps in nks:
        def a_map(j, b, i, k, lo=lo, steps=steps):
            return (b if ba > 1 else 0, i, jnp.clip(k - lo, 0, steps - 1))

        in_specs.append(pl.BlockSpec((None, tm, tk), a_map))
        lo += steps
    if w_index is not None:
        w_map = lambda j, b, i, k: (w_index, k, j)
    else:
        w_map = (lambda j, b, i, k: (b, k, j)) if bw > 1 else (lambda j, b, i, k: (0, k, j))
    cache_w = w.dtype != BF16 and nk == 1 and (w_index is not None or bw == 1)
    in_specs.append(pl.BlockSpec((None, tk, tn), w_map))
    args = [*a_list, w]
    for arr, spec in extra:
        args.append(arr)
        in_specs.append(spec)
    out_shape = jax.ShapeDtypeStruct((nb, rows, n), out_dtype)
    out_spec = pl.BlockSpec((None, tm, tn), lambda j, b, i, k: (b, i, j))
    if n_out > 1:
        out_shape, out_spec = (out_shape,) * n_out, (out_spec,) * n_out
    scratch = [pltpu.VMEM((tm, tn), F32)] if nk > 1 else []
    if cache_w:
        scratch = [pltpu.VMEM((tk, tn), BF16)]
    body = functools.partial(_mm_body, nks=nks, n_extra=len(extra), epilogue=epilogue, cache_w=cache_w, n_out=n_out)
    inner = "arbitrary" if cache_w else "parallel"
    return pl.pallas_call(
        body, out_shape=out_shape, grid=(nj, nb, ni, nk), in_specs=in_specs, out_specs=out_spec,
        scratch_shapes=scratch, name=name,
        compiler_params=_cp(("parallel", inner, inner, "arbitrary")))(*args)


def _row_is_lat(row_blk, tm, n_lat):
    rows = row_blk * tm + lax.broadcasted_iota(jnp.int32, (tm, 1), 0)
    return rows < n_lat


def _gated_residual_epilogue(acc, row_blk, res_ref, gate_ref, *, tm, n_lat):
    gate = jnp.where(_row_is_lat(row_blk, tm, n_lat), gate_ref[0], gate_ref[1])
    return res_ref[...] + gate * acc


def _gated_residual_rows_body(*refs, ks, tm, n_lat):
    n_a = len(ks)
    a_refs, w_ref, res_ref, gate_ref, o_ref = refs[:n_a], refs[n_a], refs[n_a + 1], refs[n_a + 2], refs[n_a + 3]
    acc, lo = None, 0
    for a_ref, kk in zip(a_refs, ks):
        part = _dot(a_ref[...], w_ref[lo:lo + kk, :])
        acc = part if acc is None else acc + part
        lo += kk
    o_ref[...] = _gated_residual_epilogue(acc, pl.program_id(1), res_ref, gate_ref, tm=tm, n_lat=n_lat)


def _matmul_gated_residual_rows(a_list, w, res, gate, *, w_index, tm, tn, rows, n_lat, name):
    nb = a_list[0].shape[0]
    _, kdim, n = w.shape
    ks = tuple(x.shape[2] for x in a_list)
    assert sum(ks) == kdim and rows % tm == 0 and n % tn == 0
    in_specs = [pl.BlockSpec((None, tm, kk), lambda b, i, j: (b, i, 0)) for kk in ks]
    in_specs += [pl.BlockSpec((None, kdim, tn), lambda b, i, j: (w_index, 0, j)),
                 pl.BlockSpec((None, tm, tn), lambda b, i, j: (b, i, j)),
                 pl.BlockSpec((None, 2, 1, tn), lambda b, i, j: (b, 0, 0, j))]
    return pl.pallas_call(
        functools.partial(_gated_residual_rows_body, ks=ks, tm=tm, n_lat=n_lat),
        out_shape=jax.ShapeDtypeStruct((nb, rows, n), F32), grid=(nb, rows // tm, n // tn), in_specs=in_specs,
        out_specs=pl.BlockSpec((None, tm, tn), lambda b, i, j: (b, i, j)), name=name,
        compiler_params=_cp(("parallel", "parallel", "parallel")))(*a_list, w, res, gate)


def _modulated_rmsnorm(x, is_lat, w_ref, mod_ref):
    y = x * lax.rsqrt(jnp.mean(x * x, axis=-1, keepdims=True) + EPS) * w_ref[...]
    shift = jnp.where(is_lat, mod_ref[0, 0], mod_ref[0, 1])
    scale = jnp.where(is_lat, mod_ref[1, 0], mod_ref[1, 1])
    return y * (1.0 + scale) + shift


def _rmsmod_body(x_ref, w_ref, mod_ref, o_ref, *, tm, n_lat):
    is_lat = _row_is_lat(pl.program_id(2), tm, n_lat)
    o_ref[...] = _modulated_rmsnorm(x_ref[...], is_lat, w_ref, mod_ref).astype(o_ref.dtype)


def _rmsmod(x, w, mod, *, tm, rows, n_lat):
    nb, _, d = x.shape
    tm = tm // 4
    assert rows % tm == 0 and tm % 16 == 0
    body = functools.partial(_rmsmod_body, tm=tm, n_lat=n_lat)
    return pl.pallas_call(
        body, out_shape=jax.ShapeDtypeStruct((nb, rows, d), BF16), grid=(1, nb, rows // tm),
        in_specs=[pl.BlockSpec((None, tm, d), lambda j, b, i: (b, i, 0)),
                  pl.BlockSpec((1, d), lambda j, b, i: (0, 0)),
                  pl.BlockSpec((None, 2, 2, 1, d), lambda j, b, i: (b, 0, 0, 0, 0))],
        out_specs=pl.BlockSpec((None, tm, d), lambda j, b, i: (b, i, 0)),
        name="rmsmod", compiler_params=_cp(("parallel", "parallel", "parallel")))(x, w, mod)


CONV_CH = 128
CONV_HALO = 16
CONV_STEPS = 4


def _shift_matrices():
    r = lax.broadcasted_iota(jnp.int32, (CONV_CH, CONV_CH + 2 * CONV_HALO), 0)
    c = lax.broadcasted_iota(jnp.int32, (CONV_CH, CONV_CH + 2 * CONV_HALO), 1)
    return (c == r + CONV_HALO - 1).astype(BF16), (c == r + CONV_HALO + 1).astype(BF16)


def _chunk_rows(r0, i):
    return pl.ds(r0 + pl.multiple_of(i * CONV_CH, CONV_CH), CONV_CH)


def _conv_at(x_ref, w_ref, b_ref, shifts, r0, n, steps):
    cur = [x_ref[_chunk_rows(r0, i), :] for i in steps]
    zero = jnp.zeros((CONV_HALO, cur[0].shape[1]), cur[0].dtype)
    lo = [pl.multiple_of(jnp.maximum(i * CONV_CH - CONV_HALO, 0), CONV_HALO) for i in steps]
    hi = [pl.multiple_of(jnp.minimum((i + 1) * CONV_CH, n - CONV_HALO), CONV_HALO) for i in steps]
    tail = [jnp.where(i > 0, x_ref[pl.ds(r0 + l, CONV_HALO), :], zero) for i, l in zip(steps, lo)]
    head = [jnp.where(i < n // CONV_CH - 1, x_ref[pl.ds(r0 + h, CONV_HALO), :], zero) for i, h in zip(steps, hi)]
    win = [jnp.concatenate([t, c, h], axis=0) for t, c, h in zip(tail, cur, head)]
    prev = [_dot(shifts[0], w) for w in win]
    nxt = [_dot(shifts[1], w) for w in win]
    return [w_ref[0:1, :] * p + w_ref[1:2, :] * c.astype(F32) + w_ref[2:3, :] * x + b_ref[...]
            for p, c, x in zip(prev, cur, nxt)]


def _conv_loop(segs, fn):
    for si, (r0, n) in enumerate(segs):
        def body(i, carry, r0=r0, n=n, si=si):
            fn(si, r0, n, [i])
            return carry

        lax.fori_loop(0, n // CONV_CH, body, 0, unroll=min(CONV_STEPS, n // CONV_CH))


def _conv_silu_body(x_ref, w_ref, b_ref, o_ref, *, segs):
    shifts = _shift_matrices()

    def fn(si, r0, n, steps):
        ys = [_silu(y) for y in _conv_at(x_ref, w_ref, b_ref, shifts, r0, n, steps)]
        for i, y in zip(steps, ys):
            o_ref[_chunk_rows(r0, i), :] = y.astype(o_ref.dtype)

    _conv_loop(segs, fn)


def _conv_rope_body(x_ref, w_ref, b_ref, cos_ref, sin_ref, o_ref, *, segs, tc):
    shifts = _shift_matrices()
    lane = lax.broadcasted_iota(jnp.int32, (CONV_CH, tc), 1)
    first = (lane % (ML_QK_DIM // 2)) < (ML_QK_DIM // 4)
    nf = ML_QK_DIM // 4

    def fn(si, r0, n, steps):
        ys = [_silu(y) for y in _conv_at(x_ref, w_ref, b_ref, shifts, r0, n, steps)]
        partner = [jnp.where(first, pltpu.roll(y, tc - nf, 1), pltpu.roll(y, nf, 1)) for y in ys]
        for i, y, p in zip(steps, ys, partner):
            rows = _chunk_rows(r0, i)
            o_ref[rows, :] = (y * cos_ref[rows, :] + p * sin_ref[rows, :]).astype(o_ref.dtype)

    _conv_loop(segs, fn)


def _conv_ffn_body(a_ref, x_ref, w_ref, b_ref, o_ref, *, segs):
    shifts = _shift_matrices()

    def fn(si, r0, n, steps):
        gs = [_silu(y) for y in _conv_at(x_ref, w_ref, b_ref, shifts, r0, n, steps)]
        for i, g in zip(steps, gs):
            rows = _chunk_rows(r0, i)
            o_ref[rows, :] = (a_ref[rows, :].astype(F32) * g).astype(o_ref.dtype)

    _conv_loop(segs, fn)


def _conv_hyena_body(x0_ref, x1_ref, v_ref, w0_ref, b0_ref, w1_ref, b1_ref, w2_ref, b2_ref,
                     ox0_ref, ozl_ref, ozc_ref, ozle_ref, ozlo_ref, ozce_ref, ozco_ref, *zs_refs, segs):
    shifts = _shift_matrices()
    half = CONV_CH // 2
    outs = ((ozl_ref, ozle_ref, ozlo_ref), (ozc_ref, ozce_ref, ozco_ref))

    def fn(si, r0, n, steps):
        x0 = _conv_at(x0_ref, w0_ref, b0_ref, shifts, r0, n, steps)
        x1 = _conv_at(x1_ref, w1_ref, b1_ref, shifts, r0, n, steps)
        v = _conv_at(v_ref, w2_ref, b2_ref, shifts, r0, n, steps)
        nat_ref, even_ref, odd_ref = outs[si]
        for i, x0_i, x1_i, v_i in zip(steps, x0, x1, v):
            ox0_ref[_chunk_rows(r0, i), :] = x0_i.astype(ox0_ref.dtype)
            zin = x1_i * v_i
            nat_ref[_chunk_rows(0, i), :] = zin.astype(BF16)
            t0 = r0 + pl.multiple_of(i * CONV_CH, CONV_CH)
            hrows = pl.ds(pl.multiple_of(i * half, half), half)
            for g, zs_ref in enumerate(zs_refs):
                lanes = slice(g * LANES, (g + 1) * LANES)
                zs_ref[pl.ds(t0, CONV_CH), :] = zin[:, lanes]
                even_ref[hrows, lanes] = zs_ref[pl.ds(t0, half, stride=2), :].astype(BF16)
                odd_ref[hrows, lanes] = zs_ref[pl.ds(t0 + 1, half, stride=2), :].astype(BF16)

    _conv_loop(segs, fn)


def _conv_specs(t_in, tc, xcol_blk, wcol_blk):
    return [pl.BlockSpec((None, t_in, tc), lambda b, j: (b, 0, j + xcol_blk)),
            pl.BlockSpec((CONV_W, tc), lambda b, j: (0, j + wcol_blk)),
            pl.BlockSpec((1, tc), lambda b, j: (0, j + wcol_blk))]


def _conv_silu(x, w, b, *, xcol, wcol, width, segs, tc=256):
    nb, t_in, _ = x.shape
    return pl.pallas_call(
        functools.partial(_conv_silu_body, segs=segs),
        out_shape=jax.ShapeDtypeStruct((nb, t_in, width), BF16), grid=(nb, width // tc),
        in_specs=_conv_specs(t_in, tc, xcol // tc, wcol // tc),
        out_specs=pl.BlockSpec((None, t_in, tc), lambda b, j: (b, 0, j)),
        name="conv_silu", compiler_params=_cp(("parallel", "parallel")))(x, w, b)


def _conv_rope(x, w, b, cos, sin, *, width, segs, tc=256):
    nb, t_in, _ = x.shape
    tab = pl.BlockSpec((t_in, tc), lambda b, j: (0, 0))
    return pl.pallas_call(
        functools.partial(_conv_rope_body, segs=segs, tc=tc),
        out_shape=jax.ShapeDtypeStruct((nb, t_in, width), BF16), grid=(nb, width // tc),
        in_specs=_conv_specs(t_in, tc, 0, 0) + [tab, tab],
        out_specs=pl.BlockSpec((None, t_in, tc), lambda b, j: (b, 0, j)),
        name="conv_rope", compiler_params=_cp(("parallel", "parallel")))(x, w, b, cos, sin)


def _conv_ffn(up, w, b, *, segs, rows, tc=256):
    nb, t_in, two_ff = up.shape
    ff = two_ff // 2
    return pl.pallas_call(
        functools.partial(_conv_ffn_body, segs=segs),
        out_shape=jax.ShapeDtypeStruct((nb, rows, ff), BF16), grid=(nb, ff // tc),
        in_specs=[pl.BlockSpec((None, rows, tc), lambda b, j: (b, 0, j))] + _conv_specs(rows, tc, ff // tc, 0),
        out_specs=pl.BlockSpec((None, rows, tc), lambda b, j: (b, 0, j)),
        name="conv_ffn", compiler_params=_cp(("parallel", "parallel")))(up, up, w, b)


def _conv_hyena(pr, w, b, *, xcol, wcol, segs, tc=256):
    nb, t_in, _ = pr.shape
    (_, n_lat), (_, n_ctx) = segs
    wd = HY_WIDTH
    out_rows = (t_in, n_lat, n_ctx, n_lat // 2, n_lat // 2, n_ctx // 2, n_ctx // 2)
    in_specs = [pl.BlockSpec((None, t_in, tc), functools.partial(lambda b, j, o: (b, 0, j + o), o=(xcol + m * wd) // tc))
                for m in range(3)]
    wargs = []
    for m in range(3):
        off = (wcol + m * wd) // tc
        in_specs.append(pl.BlockSpec((CONV_W, tc), functools.partial(lambda b, j, o: (0, j + o), o=off)))
        in_specs.append(pl.BlockSpec((1, tc), functools.partial(lambda b, j, o: (0, j + o), o=off)))
        wargs += [w, b]
    return pl.pallas_call(
        functools.partial(_conv_hyena_body, segs=segs),
        out_shape=tuple(jax.ShapeDtypeStruct((nb, rows, wd), BF16) for rows in out_rows),
        grid=(nb, wd // tc), in_specs=in_specs,
        out_specs=tuple(pl.BlockSpec((None, rows, tc), lambda b, j: (b, 0, j)) for rows in out_rows),
        scratch_shapes=[pltpu.VMEM((t_in, LANES), F32)] * (tc // LANES),
        name="conv_hyena", compiler_params=_cp(("parallel", "parallel")))(pr, pr, pr, *wargs)


def _scan_masks(reverse):
    q = SCAN_Q
    ii = lax.broadcasted_iota(jnp.int32, (q, q), 0)
    jj = lax.broadcasted_iota(jnp.int32, (q, q), 1)
    mask = (jj >= ii) if reverse else (jj <= ii)
    mask_t = (jj <= ii) if reverse else (jj >= ii)
    return mask, mask.astype(F32), mask_t.astype(F32)


def _chunk_map(reverse, n_chunks, n_lat_chunks):
    if reverse:
        return lambda k: n_chunks - 1 - k
    return lambda k: (k + n_lat_chunks) % n_chunks


SSD_GPS = 4


def _ssd_body(*refs, reverse, final):
    (xs_ref, b_ref, c_ref, dtc_ref, dtr_ref, dbc_ref, dbr_ref, alc_ref, alr_ref) = refs[:9]
    if final:
        yf_ref, z_ref, dsk_ref, nw_ref, o_ref, h_ref = refs[9:]
    else:
        o_ref, h_ref = refs[9:]

    @pl.when(pl.program_id(2) == 0)
    def _():
        h_ref[...] = jnp.zeros_like(h_ref)

    q, gw, hd = SCAN_Q, SSD_GW, SSD_HEAD_DIM
    mask, maskf, mask_tf = _scan_masks(reverse)
    groups = range(SSD_GPS)
    pairs = range(SSD_HPG // 2)
    cols = [slice(g * gw, (g + 1) * gw) for g in groups]
    xs = [xs_ref[:, cols[g]].astype(F32) for g in groups]
    bm = [b_ref[:, g * SSD_STATE:(g + 1) * SSD_STATE] for g in groups]
    cm = [c_ref[:, g * SSD_STATE:(g + 1) * SSD_STATE] for g in groups]
    h_prev = [h_ref[g] for g in groups]
    dt_c = [_softplus(dtc_ref[g] + dbc_ref[g]) for g in groups]
    dt_r = [_softplus(dtr_ref[g] + dbr_ref[g]) for g in groups]
    da_c = [dt_c[g] * (-jnp.exp(alc_ref[g])) for g in groups]
    da_r = [dt_r[g] * (-jnp.exp(alr_ref[g])) for g in groups]
    acum_c = [_sel_dot(maskf, da_c[g], 3) for g in groups]
    acum_r = [_dot_sel(da_r[g], mask_tf, 3) for g in groups]
    atot = [jnp.sum(da_c[g], axis=0, keepdims=True) for g in groups]

    er = lax.broadcasted_iota(jnp.int32, (SSD_HPG, gw), 0)
    ec = lax.broadcasted_iota(jnp.int32, (SSD_HPG, gw), 1)
    expand = (ec // hd == er).astype(F32)
    wide = [_dot_sel(jnp.concatenate([dt_c[g], jnp.exp(atot[g] - acum_c[g]), jnp.exp(acum_c[g]),
                                      jnp.broadcast_to(jnp.exp(atot[g]), (SUBLANES, SSD_HPG))], axis=0), expand, 1)
            for g in groups]
    xq = [xs[g] * wide[g][0:q] for g in groups]
    cb = [_dot_nt(cm[g], bm[g]) for g in groups]
    y_off = [wide[g][2 * q:3 * q] * _dot(cm[g], h_prev[g].astype(BF16)) for g in groups]
    lane = lax.broadcasted_iota(jnp.int32, (q, 2 * hd), 1)

    def pair_lhs(g, p):
        ms = [(cb[g] * jnp.where(mask, jnp.exp(acum_c[g][:, r:r + 1] - acum_r[g][r:r + 1, :]), 0.0)).astype(BF16)
              for r in (2 * p, 2 * p + 1)]
        return jnp.concatenate(ms, axis=1)

    def pair_rhs(g, p):
        xp = xq[g][:, 2 * hd * p:2 * hd * (p + 1)]
        return jnp.concatenate([jnp.where(lane < hd, xp, 0.0), jnp.where(lane >= hd, xp, 0.0)], axis=0).astype(BF16)

    y_in = [[_dot(pair_lhs(g, p), pair_rhs(g, p)) for p in pairs] for g in groups]
    y = [y_off[g] + jnp.concatenate(y_in[g], axis=1) for g in groups]
    bt = [bm[g].astype(F32).T.astype(BF16) for g in groups]
    upd = [_dot(bt[g], (xq[g] * wide[g][q:2 * q]).astype(BF16)) for g in groups]
    for g in groups:
        h_ref[g] = wide[g][3 * q:3 * q + 1] * h_prev[g] + upd[g]

    for g in groups:
        if final:
            ytot = y[g] + yf_ref[:, cols[g]] + xs[g] * dsk_ref[:, cols[g]]
            gated = ytot * _silu(z_ref[:, cols[g]].astype(F32))
            o = gated * lax.rsqrt(jnp.mean(gated * gated, axis=-1, keepdims=True) + EPS) * nw_ref[:, cols[g]]
            o_ref[:, cols[g]] = o.astype(o_ref.dtype)
        else:
            o_ref[:, cols[g]] = y[g]


def _ssd(xbc, dt_c, dt_r, db_c, db_r, al_c, al_r, *, direction, n_lat, final_args=None):
    nb, t, _ = xbc.shape
    q, gw, hpg = SCAN_Q, SSD_GW, SSD_HPG
    nch = t // q
    reverse = direction == 1
    cmap = _chunk_map(reverse, nch, n_lat // q)
    gps = SSD_GPS
    sw, gww = gps * SSD_STATE, gps * gw
    b0 = SSD_WIDTH // sw
    c0 = b0 + SSD_GROUPS // gps
    d = direction
    in_specs = [
        pl.BlockSpec((None, q, gww), lambda b, g, k: (b, cmap(k), g)),
        pl.BlockSpec((None, q, sw), lambda b, g, k: (b, cmap(k), b0 + g)),
        pl.BlockSpec((None, q, sw), lambda b, g, k: (b, cmap(k), c0 + g)),
        pl.BlockSpec((None, None, gps, q, hpg), lambda b, g, k: (b, d, g, cmap(k), 0)),
        pl.BlockSpec((None, None, gps, hpg, q), lambda b, g, k: (b, d, g, 0, cmap(k))),
        pl.BlockSpec((None, gps, 1, hpg), lambda b, g, k: (d, g, 0, 0)),
        pl.BlockSpec((None, gps, hpg, 1), lambda b, g, k: (d, g, 0, 0)),
        pl.BlockSpec((None, gps, 1, hpg), lambda b, g, k: (d, g, 0, 0)),
        pl.BlockSpec((None, gps, hpg, 1), lambda b, g, k: (d, g, 0, 0)),
    ]
    args = [xbc, xbc, xbc, dt_c, dt_r, db_c, db_r, al_c, al_r]
    final = final_args is not None
    if final:
        yf, pr, dskip, norm_w = final_args
        in_specs += [pl.BlockSpec((None, q, gww), lambda b, g, k: (b, cmap(k), g)),
                     pl.BlockSpec((None, q, gww), lambda b, g, k: (b, cmap(k), g)),
                     pl.BlockSpec((1, gww), lambda b, g, k: (0, g)),
                     pl.BlockSpec((1, gww), lambda b, g, k: (0, g))]
        args += [yf, pr, dskip, norm_w]
    out_shape = jax.ShapeDtypeStruct((nb, t, SSD_WIDTH), BF16 if final else F32)
    return pl.pallas_call(
        functools.partial(_ssd_body, reverse=reverse, final=final),
        out_shape=out_shape, grid=(nb, SSD_GROUPS // gps, nch), in_specs=in_specs,
        out_specs=pl.BlockSpec((None, q, gww), lambda b, g, k: (b, cmap(k), g)),
        scratch_shapes=[pltpu.VMEM((gps, SSD_STATE, gw), F32)],
        name="ssd_bwd" if reverse else "ssd_fwd",
        compiler_params=_cp(("parallel", "parallel", "arbitrary")))(*args)


def _hyfilt_body(f_ref, w1_ref, b1_ref, w2_ref, b2_ref, w3_ref, fr_ref, dl_ref, o_ref, *, tl):
    f = f_ref[...]
    h = jnp.sin(fr_ref[0:1, :] * (_dot(f, w1_ref[...], HI) + b1_ref[...]))
    h = jnp.sin(fr_ref[1:2, :] * (_dot(h, w2_ref[...], HI) + b2_ref[...]))
    h = _dot(h, w3_ref[...], HI)
    dec = jnp.exp(-f[:, 0:1] * dl_ref[...])
    row = pl.program_id(0) * tl + lax.broadcasted_iota(jnp.int32, (tl, 1), 0)
    wd = HY_WIDTH
    o_ref[:, 0:wd] = (h[:, 0:wd] * dec).astype(o_ref.dtype)
    o_ref[:, wd:2 * wd] = jnp.where(row == 0, 0.0, h[:, wd:2 * wd] * dec).astype(o_ref.dtype)


def _hyena_filters(length, w1, b1, w2, b2, w3, freq, *, tl=256):
    p = LANES
    t = jnp.linspace(0.0, 1.0, length, dtype=F32)[:, None]
    w = 2.0 * math.pi * jnp.arange(length, dtype=F32)[:, None] / length
    f = jnp.linspace(1e-4, HY_BANDS - 1, HY_BANDS, dtype=F32)[None, :]
    feats = jnp.concatenate([t, jnp.cos(f * w), -jnp.sin(f * w)], axis=-1)
    feats = jnp.concatenate([feats[0::2], feats[1::2]], axis=0)
    feats = jnp.pad(feats, ((0, 0), (0, p - HY_EMB)))
    po = p - HY_ORDER
    w1p = jnp.pad(w1, ((0, p - HY_EMB), (0, po)))
    w2p = jnp.pad(w2, ((0, po), (0, po)))
    w3p = jnp.pad(w3, ((0, po), (0, 0)))
    b1p = jnp.pad(b1[None, :], ((0, 0), (0, po)))
    b2p = jnp.pad(b2[None, :], ((0, 0), (0, po)))
    frp = jnp.pad(freq, ((0, 0), (0, po)))
    deltas = jnp.abs(jnp.linspace(math.log(HY_TARGET) / HY_FAST, math.log(HY_TARGET) / HY_SLOW, HY_WIDTH,
                                  dtype=F32))[None, :]
    full = lambda shape: pl.BlockSpec(shape, lambda i: (0,) * len(shape))
    out = pl.pallas_call(
        functools.partial(_hyfilt_body, tl=tl),
        out_shape=jax.ShapeDtypeStruct((length, 2 * HY_WIDTH), BF16), grid=(length // tl,),
        in_specs=[pl.BlockSpec((tl, p), lambda i: (i, 0)), full((p, p)), full((1, p)), full((p, p)), full((1, p)),
                  full((p, 2 * HY_WIDTH)), full((2, p)), full((1, HY_WIDTH))],
        out_specs=pl.BlockSpec((tl, 2 * HY_WIDTH), lambda i: (i, 0)),
        name="hyena_filters", compiler_params=_cp(("parallel",)))(feats, w1p, b1p, w2p, b2p, w3p, frp, deltas)
    return out[None]


def _dft_matrices(length):
    n = 2 * length
    k = jnp.arange(length, dtype=jnp.int32)[:, None]
    t = jnp.arange(length, dtype=jnp.int32)[None, :]
    step = 1 << (int(math.log2(length)) // 2)
    part = jnp.arange(length // step, dtype=jnp.int32)[:, None]
    ang_hi = ((part * step * t) % n).astype(F32) * (2.0 * math.pi / n)
    ang_lo = ((jnp.arange(step, dtype=jnp.int32)[:, None] * t) % n).astype(F32) * (2.0 * math.pi / n)
    ch, sh = jnp.cos(ang_hi)[:, None, :], jnp.sin(ang_hi)[:, None, :]
    cl, sl = jnp.cos(ang_lo)[None, :, :], jnp.sin(ang_lo)[None, :, :]
    c = (ch * cl - sh * sl).reshape(length, length)
    s = (sh * cl + ch * sl).reshape(length, length)
    alt_t = jnp.where(t % 2 == 0, 1.0, -1.0).astype(F32)
    fwd = jnp.concatenate([c, jnp.where(k == 0, alt_t, -s)], axis=0)
    alt_r = jnp.where(k % 2 == 0, 1.0, -1.0).astype(F32)
    inv_re = jnp.where(t == 0, 1.0 / n, (2.0 / n) * c)
    inv_im = jnp.where(t == 0, alt_r / n, (-2.0 / n) * s)
    inv = jnp.concatenate([inv_re, inv_im], axis=1)
    return fwd.astype(BF16)[None], inv.astype(BF16)[None]


def _tile_interleave(x, axis, hm):
    shp = x.shape
    n = shp[axis] // 2
    x = x.reshape(shp[:axis] + (2, n // hm, hm) + shp[axis + 1:])
    return jnp.swapaxes(x, axis, axis + 1).reshape(shp)


def _twiddle_table(length, hm):
    k = np.arange(length // 2, dtype=np.float64)[:, None] * (np.pi / length)
    tab = np.concatenate([np.cos(k), np.sin(k)], axis=0)
    tab = tab.reshape(2, -1, hm, 1).swapaxes(0, 1).reshape(length, 1)
    return jnp.asarray(np.broadcast_to(tab, (length, LANES)).astype(np.float32))


def _lanes(x, width):
    return jnp.tile(x, (1, width // LANES))


def _parity_banks(e, ot, tw, hm):
    er, ei, otr, oti = e[0:hm], e[hm:], ot[0:hm], ot[hm:]
    c, s = _lanes(tw[0:hm], e.shape[1]), _lanes(tw[hm:], e.shape[1])
    orr = otr * c + oti * s
    oi = oti * c - otr * s
    return (er + orr, ei + oi, er - orr, ei - oi), (c, s)


def _is_bin0(row_blk, hm):
    return (row_blk == 0) & (lax.broadcasted_iota(jnp.int32, (hm, 1), 0) == 0)


def _hy_kbank_body(ef_ref, eb_ref, of_ref, ob_ref, tw_ref, ka_ref, kb_ref, *, hm):
    tw = tw_ref[...]
    (afr, afi, bfr, bfi), _ = _parity_banks(ef_ref[...], of_ref[...], tw, hm)
    (abr, abi, bbr, bbi), _ = _parity_banks(eb_ref[...], ob_ref[...], tw, hm)
    first = _is_bin0(pl.program_id(0), hm)
    kai = jnp.where(first, ef_ref[hm:, :] + eb_ref[hm:, :], afi - abi)
    kbi = jnp.where(first, ob_ref[hm:, :] - of_ref[hm:, :], bfi - bbi)
    ka_ref[...] = jnp.concatenate([afr + abr, kai], axis=0)
    kb_ref[...] = jnp.concatenate([bfr + bbr, kbi], axis=0)


def _hy_kbank(ef, of, tw, *, tm, tc=512):
    _, ln, c2 = ef.shape
    c = c2 // 2
    fspec = pl.BlockSpec((None, tm, tc), lambda i, j: (0, i, j))
    bspec = pl.BlockSpec((None, tm, tc), lambda i, j: (0, i, j + c // tc))
    return pl.pallas_call(
        functools.partial(_hy_kbank_body, hm=tm // 2),
        out_shape=(jax.ShapeDtypeStruct((1, ln, c), F32),) * 2, grid=(ln // tm, c // tc),
        in_specs=[fspec, bspec, fspec, bspec, pl.BlockSpec((tm, LANES), lambda i, j: (i, 0))],
        out_specs=(fspec, fspec), name="hy_kbank",
        compiler_params=_cp(("parallel", "parallel")))(ef, ef, of, of, tw)


def _hy_spectrum_epilogue(acc, row_blk, e_ref, tw_ref, ka_ref, kb_ref, *, hm):
    e = e_ref[...]
    (ar, ai, br, bi), (c, s) = _parity_banks(e, acc, tw_ref[...], hm)
    kar, kai, kbr, kbi = ka_ref[0:hm], ka_ref[hm:], kb_ref[0:hm], kb_ref[hm:]
    yar, yai = ar * kar - ai * kai, ar * kai + ai * kar
    ybr, ybi = br * kbr - bi * kbi, br * kbi + bi * kbr
    qr, qi = yar - ybr, yai - ybi
    first = _is_bin0(row_blk, hm)
    y0, yl = ar * kar, br * kbr
    amr, ami = e[hm:], -acc[hm:]
    ymr, ymi = amr * kai - ami * kbi, amr * kbi + ami * kai
    pr = jnp.where(first, y0 + yl, yar + ybr)
    pi = jnp.where(first, 2.0 * ymr, yai + ybi)
    qpr = jnp.where(first, y0 - yl, qr * c - qi * s)
    qpi = jnp.where(first, -2.0 * ymi, qi * c + qr * s)
    return jnp.concatenate([pr, pi], axis=0), jnp.concatenate([qpr, qpi], axis=0)


def _hyena_out_body(*refs, segs):
    ns = len(segs)
    x0_ref, db_ref, o_ref = refs[3 * ns:3 * ns + 3]
    ys_refs = refs[3 * ns + 3:]
    half = CONV_CH // 2
    for si, (r0, n) in enumerate(segs):
        ye_ref, yo_ref, zin_ref = refs[3 * si:3 * si + 3]

        def body(i, carry, r0=r0, ye_ref=ye_ref, yo_ref=yo_ref, zin_ref=zin_ref):
            t0 = pl.multiple_of(i * CONV_CH, CONV_CH)
            hrows = pl.ds(pl.multiple_of(i * half, half), half)
            for g, ys_ref in enumerate(ys_refs):
                lanes = slice(g * LANES, (g + 1) * LANES)
                ys_ref[pl.ds(r0 + t0, half, stride=2), :] = ye_ref[hrows, lanes].astype(F32)
                ys_ref[pl.ds(r0 + t0 + 1, half, stride=2), :] = yo_ref[hrows, lanes].astype(F32)
            rows = pl.ds(r0 + t0, CONV_CH)
            y = jnp.concatenate([ys_ref[rows, :] for ys_ref in ys_refs], axis=1)
            zin = zin_ref[pl.ds(t0, CONV_CH), :].astype(F32)
            o_ref[rows, :] = (x0_ref[rows, :].astype(F32) * (y + zin * db_ref[...])).astype(o_ref.dtype)
            return carry

        lax.fori_loop(0, n // CONV_CH, body, 0, unroll=min(4, n // CONV_CH))


def _hyena_out(parts, x0, dbias, *, segs, tc=256):
    nb, t, c = x0.shape
    in_specs, args = [], []
    for (ye, yo, zin), (_, n) in zip(parts, segs):
        in_specs += [pl.BlockSpec((None, n // 2, tc), lambda b, j: (b, 0, j)),
                     pl.BlockSpec((None, n // 2, tc), lambda b, j: (b, 0, j)),
                     pl.BlockSpec((None, n, tc), lambda b, j: (b, 0, j))]
        args += [ye, yo, zin]
    in_specs += [pl.BlockSpec((None, t, tc), lambda b, j: (b, 0, j)), pl.BlockSpec((1, tc), lambda b, j: (0, j))]
    return pl.pallas_call(
        functools.partial(_hyena_out_body, segs=segs),
        out_shape=jax.ShapeDtypeStruct((nb, t, c), BF16), grid=(nb, c // tc), in_specs=in_specs,
        out_specs=pl.BlockSpec((None, t, tc), lambda b, j: (b, 0, j)),
        scratch_shapes=[pltpu.VMEM((t, LANES), F32)] * (tc // LANES), name="hyena_out",
        compiler_params=_cp(("parallel", "parallel")))(*args, x0, dbias)


def _hyena_longconv(ze, zo, hfb, *, tn=1024):
    nb, h, c = ze.shape
    ln = 2 * h
    tm_s = min(512, ln)
    hm = tm_s // 2
    fwd, inv = _dft_matrices(h)
    fwd = _tile_interleave(fwd, 1, hm)
    inv = _tile_interleave(inv * 0.5, 2, hm)
    tw = _twiddle_table(ln, hm)
    tm_f, tk_f = min(1024, ln), min(2048, h)
    ef = _matmul(fwd, hfb[:, :h], tm=tm_f, tn=tn, tk=tk_f, out_dtype=F32, name="hy_filter_dft_even")
    of = _matmul(fwd, hfb[:, h:], tm=tm_f, tn=tn, tk=tk_f, out_dtype=F32, name="hy_filter_dft_odd")
    ka, kb = _hy_kbank(ef, of, tw, tm=tm_s)
    e = _matmul(fwd, ze, tm=tm_f, tn=tn, tk=tk_f, out_dtype=F32, name="hy_dft_even")
    tile = pl.BlockSpec((None, tm_s, tn), lambda j, b, i, k: (b, i, j))
    ktile = pl.BlockSpec((None, tm_s, tn), lambda j, b, i, k: (0, i, j))
    extra = [(e, tile), (tw, pl.BlockSpec((tm_s, LANES), lambda j, b, i, k: (i, 0))), (ka, ktile), (kb, ktile)]
    p, q = _matmul(fwd, zo, tm=tm_s, tn=tn, tk=tk_f, out_dtype=BF16, extra=extra, n_out=2,
                   epilogue=functools.partial(_hy_spectrum_epilogue, hm=hm), name="hy_dft_odd")
    tm_i, tk_i = min(1024, h), min(2048, ln)
    ye = _matmul(inv, p, tm=tm_i, tn=tn, tk=tk_i, out_dtype=BF16, name="hy_idft_even")
    yo = _matmul(inv, q, tm=tm_i, tn=tn, tk=tk_i, out_dtype=BF16, name="hy_idft_odd")
    return ye, yo


def _mlstm_body(*refs, reverse, final):
    q_ref, k_ref, v_ref, gc_ref, gr_ref = refs[:5]
    if final:
        hf_ref, og_ref, nw_ref, o_ref, ct_ref, n_ref, m_ref = refs[5:]
    else:
        o_ref, ct_ref, n_ref, m_ref = refs[5:]
    nh, dk, dv = ML_HEADS, ML_QK_DIM, ML_V_DIM
    scale = dk ** -0.5

    @pl.when(pl.program_id(1) == 0)
    def _():
        ct_ref[...] = jnp.zeros_like(ct_ref)
        n_ref[...] = jnp.zeros_like(n_ref)
        m_ref[...] = jnp.zeros_like(m_ref)

    gc, gr = gc_ref[...], gr_ref[...]
    li_c, lf_c = gc[:, 0:nh], _log_sigmoid(gc[:, nh:2 * nh])
    li_r, lf_r = gr[0:nh, :], _log_sigmoid(gr[nh:2 * nh, :])
    mask, maskf, mask_tf = _scan_masks(reverse)
    bc_c = _dot(maskf, lf_c, HI)
    bc_r = _dot(lf_r, mask_tf, HI)
    btot_c = jnp.sum(lf_c, axis=0, keepdims=True)
    n_all, m_all = n_ref[...], m_ref[...]
    heads = range(nh)
    qs = [q_ref[:, dk * h:dk * (h + 1)] for h in heads]
    ks = [k_ref[:, dk * h:dk * (h + 1)] for h in heads]
    vs = [v_ref[:, dv * h:dv * (h + 1)] for h in heads]
    cts = [ct_ref[h] for h in heads]
    m_s = [m_all[h:h + 1, 0:1] for h in heads]
    n_s = [n_all[h:h + 1, :] for h in heads]
    qk = [_dot_nt(qs[h], ks[h]) for h in heads]
    qc = [_dot(qs[h], cts[h].astype(BF16)) for h in heads]
    dmat = [jnp.where(mask, bc_c[:, h:h + 1] - bc_r[h:h + 1, :] + li_r[h:h + 1, :], -jnp.inf) for h in heads]
    inter = [bc_c[:, h:h + 1] + m_s[h] for h in heads]
    m_t = [jnp.maximum(inter[h], jnp.max(dmat[h], axis=1, keepdims=True)) for h in heads]
    s = [qk[h] * scale * jnp.exp(dmat[h] - m_t[h]) for h in heads]
    dec = [jnp.exp(inter[h] - m_t[h]) for h in heads]
    num = [_dot(s[h].astype(BF16), vs[h]) + (dec[h] * scale) * qc[h] for h in heads]
    qn = [jnp.sum(qs[h].astype(F32) * n_s[h], axis=1, keepdims=True) * scale for h in heads]
    den = [jnp.sum(s[h], axis=1, keepdims=True) + dec[h] * qn[h] for h in heads]
    hh = [num[h] / jnp.maximum(jnp.abs(den[h]), jnp.exp(-m_t[h])) for h in heads]

    wst_c = [btot_c[:, h:h + 1] - bc_c[:, h:h + 1] + li_c[:, h:h + 1] for h in heads]
    wst_r = [btot_c[:, h:h + 1] - bc_r[h:h + 1, :] + li_r[h:h + 1, :] for h in heads]
    m_new = [jnp.maximum(btot_c[:, h:h + 1] + m_s[h], jnp.max(wst_r[h], axis=1, keepdims=True)) for h in heads]
    dstate = [jnp.exp(btot_c[:, h:h + 1] + m_s[h] - m_new[h]) for h in heads]
    kw = [ks[h].astype(F32) * jnp.exp(wst_c[h] - m_new[h]) for h in heads]
    upd = [_dot(kw[h].T.astype(BF16), vs[h]) for h in heads]
    for h in heads:
        ct_ref[h] = dstate[h] * cts[h] + upd[h]
    n_rows = [dstate[h] * n_s[h] + jnp.sum(kw[h], axis=0, keepdims=True) for h in heads]
    m_rows = [jnp.broadcast_to(m_new[h], (1, LANES)) for h in heads]

    for h in heads:
        cols = slice(dv * h, dv * (h + 1))
        if final:
            hs = hh[h] + hf_ref[:, cols]
            hn = hs * lax.rsqrt(jnp.mean(hs * hs, axis=-1, keepdims=True) + EPS) * nw_ref[:, cols]
            o_ref[:, cols] = (hn * _sigmoid(og_ref[:, cols].astype(F32))).astype(o_ref.dtype)
        else:
            o_ref[:, cols] = hh[h]

    n_ref[...] = jnp.concatenate(n_rows, axis=0)
    m_ref[...] = jnp.concatenate(m_rows, axis=0)


def _mlstm(qk, pr, g_c, g_r, *, direction, n_lat, final_args=None):
    nb, t, _ = qk.shape
    q = SCAN_Q
    nch = t // q
    reverse = direction == 1
    cmap = _chunk_map(reverse, nch, n_lat // q)
    d = direction
    in_specs = [
        pl.BlockSpec((None, q, ML_QK_WIDTH), lambda b, k: (b, cmap(k), 0)),
        pl.BlockSpec((None, q, ML_QK_WIDTH), lambda b, k: (b, cmap(k), 1)),
        pl.BlockSpec((None, q, ML_WIDTH), lambda b, k: (b, cmap(k), 2 * ML_QK_WIDTH // ML_WIDTH)),
        pl.BlockSpec((None, None, q, 2 * ML_HEADS), lambda b, k: (b, d, cmap(k), 0)),
        pl.BlockSpec((None, None, 2 * ML_HEADS, q), lambda b, k: (b, d, 0, cmap(k))),
    ]
    args = [qk, qk, pr, g_c, g_r]
    final = final_args is not None
    if final:
        hf, norm_w = final_args
        in_specs += [pl.BlockSpec((None, q, ML_WIDTH), lambda b, k: (b, cmap(k), 0)),
                     pl.BlockSpec((None, q, ML_WIDTH), lambda b, k: (b, cmap(k), 2 * ML_QK_WIDTH // ML_WIDTH + 1)),
                     pl.BlockSpec((1, ML_WIDTH), lambda b, k: (0, 0))]
        args += [hf, pr, norm_w]
    out_shape = jax.ShapeDtypeStruct((nb, t, ML_WIDTH), BF16 if final else F32)
    return pl.pallas_call(
        functools.partial(_mlstm_body, reverse=reverse, final=final),
        out_shape=out_shape, grid=(nb, nch), in_specs=in_specs,
        out_specs=pl.BlockSpec((None, q, ML_WIDTH), lambda b, k: (b, cmap(k), 0)),
        scratch_shapes=[pltpu.VMEM((ML_HEADS, ML_QK_DIM, ML_V_DIM), F32), pltpu.VMEM((ML_HEADS, ML_QK_DIM), F32),
                        pltpu.VMEM((ML_HEADS, LANES), F32)],
        name="mlstm_bwd" if reverse else "mlstm_fwd",
        compiler_params=_cp(("parallel", "arbitrary")))(*args)


def _na_window_start(j, rows):
    return jnp.clip(j * NA_QROWS - NA_ROWS // 2, 0, rows - NA_KROWS)


def _na_body(q_ref, k_ref, v_ref, tab_ref, qw_ref, kw_ref, o_ref, kn_ref, v1_ref, *, n_lat, n_ctx):
    dh = NA_HEAD_DIM
    scale = dh ** -0.5
    rows = n_lat // GRID_W
    j = pl.program_id(2)
    nq, nk = NA_QROWS * GRID_W, NA_KROWS * GRID_W

    @pl.when(j == 0)
    def _():
        one_hot = (lax.broadcasted_iota(jnp.int32, (LANES, LANES), 1) == 0).astype(v1_ref.dtype)

        def body(i, carry):
            r = pl.ds(pl.multiple_of(i * LANES, LANES), LANES)
            kk = k_ref[r, :].astype(F32)
            kn = kk * lax.rsqrt(jnp.mean(kk * kk, axis=-1, keepdims=True) + EPS) * kw_ref[...]
            kn_ref[r, :] = kn.astype(kn_ref.dtype)
            v1_ref[r, 0:dh] = v_ref[r, :]
            v1_ref[r, dh:dh + LANES] = one_hot
            return carry

        lax.fori_loop(0, (n_lat + n_ctx) // LANES, body, 0)

    start = _na_window_start(j, rows)
    win = pl.ds(pl.multiple_of(start * GRID_W, GRID_W), nk)
    ctx = pl.ds(n_lat, n_ctx)
    parts = range(NA_PARTS)
    rpp = NA_QROWS // NA_PARTS
    pq = rpp * GRID_W
    qq = [q_ref[i * pq:(i + 1) * pq, :].astype(F32) for i in parts]
    qn = [(qq[i] * lax.rsqrt(jnp.mean(qq[i] * qq[i], axis=-1, keepdims=True) + EPS)
           * (qw_ref[...] * scale)).astype(BF16) for i in parts]
    sw = [_dot_nt(qn[i], kn_ref[win, :]) for i in parts]
    sc = [_dot_nt(qn[i], kn_ref[ctx, :]) for i in parts]

    def with_bias(i):
        out = []
        for r in range(rpp):
            d0 = start - (j * NA_QROWS + i * rpp + r) + NA_ROWS - 1 + NA_QROWS
            out.append(sw[i][r * GRID_W:(r + 1) * GRID_W] + tab_ref[d0])
        return jnp.concatenate(out, axis=0)

    sb = [with_bias(i) for i in parts]
    m = [jnp.maximum(jnp.max(sb[i], axis=1, keepdims=True), jnp.max(sc[i], axis=1, keepdims=True)) for i in parts]
    pw = [jnp.exp(sb[i] - m[i]) for i in parts]
    pc = [jnp.exp(sc[i] - m[i]) for i in parts]
    o = [_dot(pw[i].astype(BF16), v1_ref[win, :]) + _dot(pc[i].astype(BF16), v1_ref[ctx, :]) for i in parts]
    for i in parts:
        o_ref[i * pq:(i + 1) * pq, :] = (o[i][:, 0:dh] / o[i][:, dh:dh + 1]).astype(o_ref.dtype)


def _na_bias_table(rpb, rows):
    nd = 2 * NA_QROWS
    valid = np.zeros((nd, NA_KROWS), bool)
    seen = np.zeros((nd,), bool)
    for j in range(rows // NA_QROWS):
        start = int(np.clip(j * NA_QROWS - NA_ROWS // 2, 0, rows - NA_KROWS))
        for qr in range(NA_QROWS):
            r = j * NA_QROWS + qr
            rs = int(np.clip(r - NA_ROWS // 2, 0, rows - NA_ROWS))
            d = start - r + NA_ROWS - 1 + NA_QROWS
            v = np.array([(rs <= start + i < rs + NA_ROWS) for i in range(NA_KROWS)])
            assert 0 <= d < nd and (not seen[d] or (valid[d] == v).all())
            valid[d], seen[d] = v, True
    col = np.arange(GRID_W)
    cs = np.clip(col - NA_COLS // 2, 0, GRID_W - NA_COLS)
    in_win = (col[None, :] >= cs[:, None]) & (col[None, :] < cs[:, None] + NA_COLS)
    ndr, ndc, w = 2 * NA_ROWS - 1, 2 * NA_COLS - 1, GRID_W
    dr = (np.arange(nd)[:, None] - NA_QROWS) + np.arange(NA_KROWS)[None, :]
    ok = valid[:, None, :, None] & in_win[None, :, None, :] & (dr >= 0)[:, None, :, None] \
        & (dr < ndr)[:, None, :, None]
    ext = jnp.pad(rpb, ((0, 0), (0, 0), (w, w)), mode="edge")
    c0 = w + NA_COLS - 1
    t1 = jnp.stack([ext[:, :, c0 - qc:c0 - qc + w] for qc in range(w)], axis=2)
    t1p = jnp.pad(t1, ((0, 0), (NA_QROWS, nd + NA_KROWS - 1 - NA_QROWS - ndr), (0, 0), (0, 0)))
    wide = jnp.concatenate([t1p[:, m] for m in range(nd + NA_KROWS - 1)], axis=-1)
    vals = jnp.stack([wide[:, :, d * w:(d + NA_KROWS) * w] for d in range(nd)], axis=1)
    return jnp.where(ok.reshape(nd, w, NA_KROWS * w)[None], vals, -jnp.inf).astype(F32)


def _na(pr, tab, q_w, k_w, *, n_lat, n_ctx, qcol):
    nb, t, _ = pr.shape
    dh, nh = NA_HEAD_DIM, NA_HEADS
    nq = NA_QROWS * GRID_W
    qb = qcol // dh
    nd = tab.shape[1]
    return pl.pallas_call(
        functools.partial(_na_body, n_lat=n_lat, n_ctx=n_ctx),
        out_shape=jax.ShapeDtypeStruct((nb, n_lat, NA_WIDTH), BF16), grid=(nb, nh, n_lat // nq),
        in_specs=[pl.BlockSpec((None, nq, dh), lambda b, h, j: (b, j, qb + h)),
                  pl.BlockSpec((None, t, dh), lambda b, h, j: (b, 0, qb + nh + h)),
                  pl.BlockSpec((None, t, dh), lambda b, h, j: (b, 0, qb + 2 * nh + h)),
                  pl.BlockSpec((None, nd, GRID_W, NA_KROWS * GRID_W), lambda b, h, j: (h, 0, 0, 0)),
                  pl.BlockSpec((1, dh), lambda b, h, j: (0, 0)),
                  pl.BlockSpec((1, dh), lambda b, h, j: (0, 0))],
        out_specs=pl.BlockSpec((None, nq, dh), lambda b, h, j: (b, j, h)),
        scratch_shapes=[pltpu.VMEM((t, dh), BF16), pltpu.VMEM((t, dh + LANES), BF16)], name="na",
        compiler_params=_cp(("parallel", "parallel", "arbitrary")))(pr, pr, pr, tab, q_w, k_w)


def _rope_tables(n_lat, n_ctx, width):
    dh = ML_QK_DIM
    nf = dh // 4
    t = jnp.arange(n_lat)
    inv = ROPE_THETA ** (-jnp.arange(nf, dtype=F32) / nf)
    lane = np.arange(dh)
    pos = jnp.where((lane < dh // 2)[None, :], (t // GRID_W)[:, None], (t % GRID_W)[:, None]).astype(F32)
    ang = pos * inv[lane % nf][None, :]
    sign = np.where((lane % (dh // 2)) < nf, -1.0, 1.0).astype(np.float32)
    cos = jnp.concatenate([jnp.cos(ang), jnp.ones((n_ctx, dh), F32)], axis=0)
    sin = jnp.concatenate([jnp.sin(ang) * sign[None, :], jnp.zeros((n_ctx, dh), F32)], axis=0)
    reps = width // dh
    return jnp.tile(cos, (1, reps)), jnp.tile(sin, (1, reps))


def _ada_mod(c, c_ctx, w_all, b_all, layer):
    nb, d = c.shape
    rows = 16
    cv = jnp.concatenate([c, c_ctx[None, :], jnp.zeros((rows - nb - 1, d), F32)], axis=0)
    a = (cv * _sigmoid(cv)).astype(BF16)[None]
    out = _matmul(a, w_all, w_index=layer, tm=rows, tn=1024, tk=d, out_dtype=F32, name="ada_mod")[0]
    out = out[:nb + 1] + b_all[layer][None, :]
    return out.reshape(nb + 1, 6, d).transpose(1, 0, 2)


def _lat_ctx(m, nb):
    ctx = jnp.broadcast_to(m[nb][None, :], (nb, m.shape[1]))
    return jnp.stack([m[:nb], ctx], axis=1)[:, :, None, :]


def _pad_cols(w, n):
    return jnp.pad(w, ((0, 0), (0, n - w.shape[1])))


def _col_row_forms(g, groups):
    nb, t, _ = g.shape
    g5 = g.reshape(nb, t, 2, groups, -1)
    return g5.transpose(0, 2, 3, 1, 4), g5.transpose(0, 2, 3, 4, 1)


ROW_TILES = (1088, 1024, 512, 256)


def _row_tile(rows):
    return next(tm for tm in ROW_TILES if rows % tm == 0)


def _mod_pair(mods, shift, scale, nb):
    return jnp.stack([_lat_ctx(mods[shift], nb), _lat_ctx(mods[scale], nb)], axis=1)


def _in_projection(u, w_all, layer, *, rows, name, cols=None):
    tm = _row_tile(rows)
    tm = 2 * tm if rows % (2 * tm) == 0 else tm
    return _matmul(u, w_all, w_index=layer, cols=cols, tm=tm, tn=512, tk=u.shape[2], out_dtype=BF16, rows=rows,
                   name=name)


def _out_projection(a_list, w_all, x, gate, layer, *, rows, n_lat, next_norm, name):
    tm = _row_tile(rows)
    tn = 512 if w_all.shape[1] <= 2 * D_MODEL else 256
    x = _matmul_gated_residual_rows(a_list, w_all, x, gate, w_index=layer, tm=tm, tn=tn, rows=rows, n_lat=n_lat,
                                    name=name)
    if next_norm is None:
        return x, None
    return x, _rmsmod(x, *next_norm, tm=tm, rows=rows, n_lat=n_lat)


def _conv_ffn_block(x, u, mods, w_up_all, conv_w, conv_b, w_down_all, layer, *, rows, n_lat, segs, next_norm):
    nb = x.shape[0]
    up = _in_projection(u, w_up_all, layer, rows=rows, name="ffn_up")
    hid = _conv_ffn(up, conv_w, conv_b[None, :], segs=segs, rows=rows)
    return _out_projection((hid,), w_down_all, x, _lat_ctx(mods[5], nb), layer, rows=rows, n_lat=n_lat,
                           next_norm=next_norm, name="ffn_down")


def _even_layer(x, u, mods, w_in_all, conv_w, conv_b, dt_bias, a_log, d_skip, ssd_norm_w,
                hy_w1, hy_b1, hy_w2, hy_b2, hy_w3, hy_freq, hy_bias, w_out_all, j, *, n_lat, n_ctx, next_norm):
    nb, t, d = x.shape
    segs = ((0, n_lat), (n_lat, n_ctx))
    tm = _row_tile(t)
    n_main = SSD_WIDTH + EV_CONV_CH
    pr = _in_projection(u, w_in_all, j, rows=t, cols=n_main, name="ev_in")
    w_dt = _pad_cols(w_in_all[j, :, n_main:], LANES).astype(BF16)
    dtp = _matmul(u, w_dt[None], tm=tm, tn=LANES, tk=d, out_dtype=F32, name="ev_in_dt")[:, :, :2 * SSD_HEADS]
    cb = conv_b[None, :]
    xbc = _conv_silu(pr, conv_w, cb, xcol=SSD_WIDTH, wcol=0, width=SSD_XBC, segs=segs)
    x0, zin_l, zin_c, zle, zlo, zce, zco = _conv_hyena(pr, conv_w, cb, xcol=SSD_WIDTH + SSD_XBC, wcol=SSD_XBC,
                                                       segs=segs)

    dt_c, dt_r = _col_row_forms(dtp, SSD_GROUPS)
    db = dt_bias.reshape(2, SSD_GROUPS, 1, SSD_HPG)
    al = a_log.reshape(2, SSD_GROUPS, 1, SSD_HPG)
    scan_args = (xbc, dt_c, dt_r, db, db.transpose(0, 1, 3, 2), al, al.transpose(0, 1, 3, 2))
    yf = _ssd(*scan_args, direction=0, n_lat=n_lat)
    dsk = jnp.repeat(d_skip, SSD_HEAD_DIM)[None, :]
    y_ssd = _ssd(*scan_args, direction=1, n_lat=n_lat, final_args=(yf, pr, dsk, ssd_norm_w[None, :]))

    hyb = hy_bias[None, :]
    filt = functools.partial(_hyena_filters, w1=hy_w1, b1=hy_b1, w2=hy_w2, b2=hy_b2, w3=hy_w3, freq=hy_freq)
    lat = _hyena_longconv(zle, zlo, filt(n_lat)) + (zin_l,)
    ctx = _hyena_longconv(zce, zco, filt(n_ctx)) + (zin_c,)
    y_hy = _hyena_out((lat, ctx), x0, hyb, segs=segs)
    return _out_projection((y_ssd, y_hy), w_out_all, x, _lat_ctx(mods[2], nb), j, rows=t, n_lat=n_lat,
                           next_norm=next_norm, name="ev_out")


def _odd_layer_latent(x, u, mods, w_in, conv_w, conv_b, gate_b, ml_norm_w, q_norm_w, k_norm_w, rpb,
                      w_out_all, j, *, n_lat, n_ctx, next_norm):
    nb, t, d = x.shape
    segs = ((0, n_lat), (n_lat, n_ctx))
    tm = _row_tile(t)
    o3 = 2 * ML_QK_WIDTH + 2 * ML_WIDTH
    o4 = o3 + 4 * ML_HEADS
    wb = w_in.astype(BF16)
    w_main = jnp.concatenate([wb[:, :o3], wb[:, o4:]], axis=1)
    pr = _in_projection(u, w_main[None], 0, rows=t, name="od_in")
    gates = _matmul(u, _pad_cols(wb[:, o3:o4], LANES)[None], tm=tm, tn=LANES, tk=d, out_dtype=F32,
                    name="od_in_gates")[:, :, :4 * ML_HEADS] + gate_b.reshape(-1)
    cos, sin = _rope_tables(n_lat, n_ctx, 256)
    qk = _conv_rope(pr, conv_w, conv_b[None, :], cos, sin, width=2 * ML_QK_WIDTH, segs=segs)
    g5 = gates.reshape(nb, t, 2, 2 * ML_HEADS)
    g_c, g_r = g5.transpose(0, 2, 1, 3), g5.transpose(0, 2, 3, 1)
    hf = _mlstm(qk, pr, g_c, g_r, direction=0, n_lat=n_lat)
    y_ml = _mlstm(qk, pr, g_c, g_r, direction=1, n_lat=n_lat, final_args=(hf, ml_norm_w[None, :]))
    tab = _na_bias_table(rpb, n_lat // GRID_W)
    y_na = _na(pr, tab, q_norm_w[None, :], k_norm_w[None, :], n_lat=n_lat, n_ctx=n_ctx, qcol=o3)
    return _out_projection((y_ml, y_na), w_out_all, x, _lat_ctx(mods[2], nb), j, rows=n_lat, n_lat=n_lat,
                           next_norm=next_norm, name="od_out")


def kernel(x, c, ctx, c_ctx, ada_w, ada_b, norm_w, ev_w_in, ev_conv_w, ev_conv_b, ssd_dt_bias, ssd_a_log, ssd_d,
           ssd_norm_w, hy_w1, hy_b1, hy_w2, hy_b2, hy_w3, hy_freq, hy_bias, ev_w_out, od_w_in, ml_conv_w, ml_conv_b,
           ml_gate_b, ml_norm_w, na_q_norm_w, na_k_norm_w, na_rpb, od_w_out, ffn_w_up, ffn_conv_w, ffn_conv_b,
           ffn_w_down):
    depth = ada_w.shape[0]
    assert depth == 2, "layer schedule below is written for one even and one odd layer"
    n_lat, n_ctx = x.shape[1], ctx.shape[1]
    t = n_lat + n_ctx
    both = ((0, n_lat), (n_lat, n_ctx))
    xs = jnp.concatenate([x, ctx], axis=1)

    nb = x.shape[0]
    w_down = ffn_w_down.astype(BF16)
    mods0 = _ada_mod(c, c_ctx, ada_w, ada_b, 0)
    mods1 = _ada_mod(c, c_ctx, ada_w, ada_b, 1)
    norm = lambda layer, sub, mods: (norm_w[layer, sub][None, :], _mod_pair(mods, 3 * sub, 3 * sub + 1, nb))

    w0, mod0 = norm(0, 0, mods0)
    u = _rmsmod(xs, w0, mod0, tm=_row_tile(t), rows=t, n_lat=n_lat)
    xs, u = _even_layer(xs, u, mods0, ev_w_in, ev_conv_w[0], ev_conv_b[0], ssd_dt_bias[0], ssd_a_log[0],
                        ssd_d[0], ssd_norm_w[0], hy_w1[0], hy_b1[0], hy_w2[0], hy_b2[0], hy_w3[0], hy_freq[0],
                        hy_bias[0], ev_w_out.astype(BF16), 0, n_lat=n_lat, n_ctx=n_ctx, next_norm=norm(0, 1, mods0))
    xs, u = _conv_ffn_block(xs, u, mods0, ffn_w_up, ffn_conv_w[0], ffn_conv_b[0], w_down, 0,
                            rows=t, n_lat=n_lat, segs=both, next_norm=norm(1, 0, mods1))
    xl, u = _odd_layer_latent(xs, u, mods1, od_w_in[0], ml_conv_w[0], ml_conv_b[0], ml_gate_b[0], ml_norm_w[0],
                              na_q_norm_w[0], na_k_norm_w[0], na_rpb[0], od_w_out.astype(BF16), 0,
                              n_lat=n_lat, n_ctx=n_ctx, next_norm=norm(1, 1, mods1))
    out, _ = _conv_ffn_block(xl, u, mods1, ffn_w_up, ffn_conv_w[1], ffn_conv_b[1], w_down, 1,
                             rows=n_lat, n_lat=n_lat, segs=((0, n_lat),), next_norm=None)
    return out
```

```python
import functools
import math

import numpy as np
import jax
import jax.numpy as jnp
from jax import lax
from jax.experimental import pallas as pl
from jax.experimental.pallas import tpu as pltpu

F32 = jnp.float32
BF16 = jnp.bfloat16
HI = lax.Precision.HIGHEST

D_MODEL = 2048
SEQ = 4096
CTX_LEN = 256
GRID_W = 64
EPS = 1e-6
CONV_W = 3
SSD_WIDTH = D_MODEL
SSD_HEAD_DIM = 64
SSD_HEADS = SSD_WIDTH // SSD_HEAD_DIM
SSD_STATE = 128
SSD_GROUPS = 4
SSD_HPG = SSD_HEADS // SSD_GROUPS
SSD_GW = SSD_WIDTH // SSD_GROUPS
SSD_XBC = SSD_WIDTH + 2 * SSD_GROUPS * SSD_STATE
HY_WIDTH = D_MODEL
HY_EMB = 33
HY_BANDS = (HY_EMB - 1) // 2
HY_ORDER = 64
HY_FAST = 0.3
HY_SLOW = 1.5
HY_TARGET = 1e-2
ML_HEADS = 8
ML_WIDTH = D_MODEL
ML_V_DIM = ML_WIDTH // ML_HEADS
ML_QK_DIM = ML_V_DIM // 2
ML_QK_WIDTH = ML_HEADS * ML_QK_DIM
ROPE_THETA = 10000.0
NA_WIDTH = D_MODEL
NA_HEAD_DIM = 128
NA_HEADS = NA_WIDTH // NA_HEAD_DIM
NA_ROWS = 8
NA_COLS = 16
NA_QROWS = 8
NA_KROWS = 16
NA_PARTS = 2
D_FF = 256 * ((8 * D_MODEL // 3 + 255) // 256)
EV_CONV_CH = SSD_XBC + 3 * HY_WIDTH
SCAN_Q = 128
LANES = 128
SUBLANES = 8
VMEM_LIMIT = 48 * 1024 * 1024


def _cp(sem, vmem=VMEM_LIMIT):
    return pltpu.CompilerParams(dimension_semantics=sem, vmem_limit_bytes=vmem)


def _sigmoid(x):
    return 0.5 * jnp.tanh(0.5 * x) + 0.5


def _silu(x):
    return x * _sigmoid(x)


def _softplus(x):
    return jnp.maximum(x, 0.0) + jnp.log(1.0 + jnp.exp(-jnp.abs(x)))


def _log_sigmoid(x):
    return -_softplus(-x)


def _dot(a, b, precision=None):
    return jnp.dot(a, b, preferred_element_type=F32, precision=precision)


def _dot_nt(a, b):
    return lax.dot_general(a, b, (((1,), (1,)), ((), ())), preferred_element_type=F32)


def _split_bf16(x, terms):
    out = []
    for _ in range(terms):
        part = x.astype(BF16)
        out.append(part)
        x = x - part.astype(F32)
    return out


def _dot_sel(x, sel, terms):
    selb = sel.astype(BF16)
    return sum(_dot(part, selb) for part in _split_bf16(x, terms))


def _sel_dot(sel, x, terms):
    selb = sel.astype(BF16)
    return sum(_dot(selb, part) for part in _split_bf16(x, terms))


def _mm_body(*refs, nk, n_extra, epilogue, cache_w, n_out, batch_axis):
    a_ref, w_ref = refs[0], refs[1]
    extra = refs[2:2 + n_extra]
    o_refs = refs[2 + n_extra:2 + n_extra + n_out]
    scratch = refs[2 + n_extra + n_out:]
    row_blk = pl.program_id(3 - batch_axis)
    if cache_w:
        wb_ref = scratch[0]

        @pl.when((pl.program_id(batch_axis) == 0) & (row_blk == 0))
        def _():
            wb_ref[...] = w_ref[...].astype(BF16)

        w_val = wb_ref[...]
    else:
        w_val = w_ref[...].astype(BF16)
    part = _dot(a_ref[...].astype(BF16), w_val)

    def finish(acc):
        vals = acc if epilogue is None else epilogue(acc, row_blk, *extra)
        for o_ref, val in zip(o_refs, vals if n_out > 1 else (vals,)):
            o_ref[...] = val.astype(o_ref.dtype)

    if nk == 1:
        finish(part)
        return
    acc_ref = scratch[0]
    k = pl.program_id(3)

    @pl.when(k == 0)
    def _():
        acc_ref[...] = part

    @pl.when(k > 0)
    def _():
        acc_ref[...] += part

    @pl.when(k == nk - 1)
    def _():
        finish(acc_ref[...])


def _matmul(a, w, *, tm, tn, tk, out_dtype, rows=None, cols=None, w_index=None, extra=(), epilogue=None,
            n_out=1, batch_inner=False, name="mm"):
    ba, ta, kdim = a.shape
    bw, kdim2, n = w.shape
    assert kdim == kdim2
    nb = ba if w_index is not None else max(ba, bw)
    rows = ta if rows is None else rows
    n = n if cols is None else cols
    assert rows % tm == 0 and n % tn == 0 and kdim % tk == 0
    ni, nj, nk = rows // tm, n // tn, kdim // tk
    order = (lambda f: (lambda j, i, b, k: f(j, b, i, k))) if batch_inner else (lambda f: f)
    respec = lambda spec: pl.BlockSpec(spec.block_shape, order(spec.index_map))
    a_map = (lambda j, b, i, k: (b, i, k)) if ba > 1 else (lambda j, b, i, k: (0, i, k))
    if w_index is not None:
        w_map = lambda j, b, i, k: (w_index, k, j)
    else:
        w_map = (lambda j, b, i, k: (b, k, j)) if bw > 1 else (lambda j, b, i, k: (0, k, j))
    cache_w = w.dtype != BF16 and nk == 1 and (w_index is not None or bw == 1)
    in_specs = [pl.BlockSpec((None, tm, tk), order(a_map)), pl.BlockSpec((None, tk, tn), order(w_map))]
    args = [a, w]
    for arr, spec in extra:
        args.append(arr)
        in_specs.append(respec(spec))
    out_shape = jax.ShapeDtypeStruct((nb, rows, n), out_dtype)
    out_spec = pl.BlockSpec((None, tm, tn), order(lambda j, b, i, k: (b, i, j)))
    if n_out > 1:
        out_shape, out_spec = (out_shape,) * n_out, (out_spec,) * n_out
    scratch = [pltpu.VMEM((tm, tn), F32)] if nk > 1 else []
    if cache_w:
        scratch = [pltpu.VMEM((tk, tn), BF16)]
    body = functools.partial(_mm_body, nk=nk, n_extra=len(extra), epilogue=epilogue, cache_w=cache_w, n_out=n_out,
                             batch_axis=2 if batch_inner else 1)
    inner = "arbitrary" if cache_w else "parallel"
    grid = (nj, ni, nb, nk) if batch_inner else (nj, nb, ni, nk)
    return pl.pallas_call(
        body, out_shape=out_shape, grid=grid, in_specs=in_specs, out_specs=out_spec,
        scratch_shapes=scratch, name=name,
        compiler_params=_cp(("parallel", inner, inner, "arbitrary")))(*args)


def _row_is_lat(row_blk, tm, n_lat):
    rows = row_blk * tm + lax.broadcasted_iota(jnp.int32, (tm, 1), 0)
    return rows < n_lat


def _gated_residual_epilogue(acc, row_blk, res_ref, gate_ref, *, tm, n_lat):
    gate = jnp.where(_row_is_lat(row_blk, tm, n_lat), gate_ref[0], gate_ref[1])
    return res_ref[...] + gate * acc


def _gated_residual_rows_body(*refs, ks, tm, n_lat):
    n_a = len(ks)
    a_refs, w_ref, res_ref, gate_ref, o_ref = refs[:n_a], refs[n_a], refs[n_a + 1], refs[n_a + 2], refs[n_a + 3]
    acc, lo = None, 0
    for a_ref, kk in zip(a_refs, ks):
        part = _dot(a_ref[...], w_ref[lo:lo + kk, :])
        acc = part if acc is None else acc + part
        lo += kk
    o_ref[...] = _gated_residual_epilogue(acc, pl.program_id(1), res_ref, gate_ref, tm=tm, n_lat=n_lat)


def _matmul_gated_residual_rows(a_list, w, res, gate, *, w_index, tm, tn, rows, n_lat, name):
    nb = a_list[0].shape[0]
    _, kdim, n = w.shape
    ks = tuple(x.shape[2] for x in a_list)
    assert sum(ks) == kdim and rows % tm == 0 and n % tn == 0
    in_specs = [pl.BlockSpec((None, tm, kk), lambda b, i, j: (b, i, 0)) for kk in ks]
    in_specs += [pl.BlockSpec((None, kdim, tn), lambda b, i, j: (w_index, 0, j)),
                 pl.BlockSpec((None, tm, tn), lambda b, i, j: (b, i, j)),
                 pl.BlockSpec((None, 2, 1, tn), lambda b, i, j: (b, 0, 0, j))]
    return pl.pallas_call(
        functools.partial(_gated_residual_rows_body, ks=ks, tm=tm, n_lat=n_lat),
        out_shape=jax.ShapeDtypeStruct((nb, rows, n), F32), grid=(nb, rows // tm, n // tn), in_specs=in_specs,
        out_specs=pl.BlockSpec((None, tm, tn), lambda b, i, j: (b, i, j)), name=name,
        compiler_params=_cp(("parallel", "parallel", "parallel")))(*a_list, w, res, gate)


def _modulated_rmsnorm(x, is_lat, w_ref, mod_ref):
    y = x * lax.rsqrt(jnp.mean(x * x, axis=-1, keepdims=True) + EPS) * w_ref[...]
    shift = jnp.where(is_lat, mod_ref[0, 0], mod_ref[0, 1])
    scale = jnp.where(is_lat, mod_ref[1, 0], mod_ref[1, 1])
    return y * (1.0 + scale) + shift


def _rmsmod_body(x_ref, w_ref, mod_ref, o_ref, *, tm, n_lat):
    is_lat = _row_is_lat(pl.program_id(2), tm, n_lat)
    o_ref[...] = _modulated_rmsnorm(x_ref[...], is_lat, w_ref, mod_ref).astype(o_ref.dtype)


def _rmsmod(x, w, mod, *, tm, rows, n_lat):
    nb, _, d = x.shape
    tm = tm // 4
    assert rows % tm == 0 and tm % 16 == 0
    body = functools.partial(_rmsmod_body, tm=tm, n_lat=n_lat)
    return pl.pallas_call(
        body, out_shape=jax.ShapeDtypeStruct((nb, rows, d), BF16), grid=(1, nb, rows // tm),
        in_specs=[pl.BlockSpec((None, tm, d), lambda j, b, i: (b, i, 0)),
                  pl.BlockSpec((1, d), lambda j, b, i: (0, 0)),
                  pl.BlockSpec((None, 2, 2, 1, d), lambda j, b, i: (b, 0, 0, 0, 0))],
        out_specs=pl.BlockSpec((None, tm, d), lambda j, b, i: (b, i, 0)),
        name="rmsmod", compiler_params=_cp(("parallel", "parallel", "parallel")))(x, w, mod)


CONV_CH = 128
CONV_HALO = 16
CONV_STEPS = 4


def _shift_matrices():
    r = lax.broadcasted_iota(jnp.int32, (CONV_CH, CONV_CH + 2 * CONV_HALO), 0)
    c = lax.broadcasted_iota(jnp.int32, (CONV_CH, CONV_CH + 2 * CONV_HALO), 1)
    return (c == r + CONV_HALO - 1).astype(BF16), (c == r + CONV_HALO + 1).astype(BF16)


def _chunk_rows(r0, i):
    return pl.ds(r0 + pl.multiple_of(i * CONV_CH, CONV_CH), CONV_CH)


def _conv_at(x_ref, w_ref, b_ref, shifts, r0, n, steps):
    cur = [x_ref[_chunk_rows(r0, i), :] for i in steps]
    zero = jnp.zeros((CONV_HALO, cur[0].shape[1]), cur[0].dtype)
    lo = [pl.multiple_of(jnp.maximum(i * CONV_CH - CONV_HALO, 0), CONV_HALO) for i in steps]
    hi = [pl.multiple_of(jnp.minimum((i + 1) * CONV_CH, n - CONV_HALO), CONV_HALO) for i in steps]
    tail = [jnp.where(i > 0, x_ref[pl.ds(r0 + l, CONV_HALO), :], zero) for i, l in zip(steps, lo)]
    head = [jnp.where(i < n // CONV_CH - 1, x_ref[pl.ds(r0 + h, CONV_HALO), :], zero) for i, h in zip(steps, hi)]
    win = [jnp.concatenate([t, c, h], axis=0) for t, c, h in zip(tail, cur, head)]
    prev = [_dot(shifts[0], w) for w in win]
    nxt = [_dot(shifts[1], w) for w in win]
    return [w_ref[0:1, :] * p + w_ref[1:2, :] * c.astype(F32) + w_ref[2:3, :] * x + b_ref[...]
            for p, c, x in zip(prev, cur, nxt)]


def _conv_loop(segs, fn):
    for si, (r0, n) in enumerate(segs):
        def body(i, carry, r0=r0, n=n, si=si):
            fn(si, r0, n, [i])
            return carry

        lax.fori_loop(0, n // CONV_CH, body, 0, unroll=min(CONV_STEPS, n // CONV_CH))


def _conv_silu_body(x_ref, w_ref, b_ref, o_ref, *, segs):
    shifts = _shift_matrices()

    def fn(si, r0, n, steps):
        ys = [_silu(y) for y in _conv_at(x_ref, w_ref, b_ref, shifts, r0, n, steps)]
        for i, y in zip(steps, ys):
            o_ref[_chunk_rows(r0, i), :] = y.astype(o_ref.dtype)

    _conv_loop(segs, fn)


def _conv_rope_body(x_ref, w_ref, b_ref, cos_ref, sin_ref, o_ref, *, segs, tc):
    shifts = _shift_matrices()
    lane = lax.broadcasted_iota(jnp.int32, (CONV_CH, tc), 1)
    first = (lane % (ML_QK_DIM // 2)) < (ML_QK_DIM // 4)
    nf = ML_QK_DIM // 4

    def fn(si, r0, n, steps):
        ys = [_silu(y) for y in _conv_at(x_ref, w_ref, b_ref, shifts, r0, n, steps)]
        partner = [jnp.where(first, pltpu.roll(y, tc - nf, 1), pltpu.roll(y, nf, 1)) for y in ys]
        for i, y, p in zip(steps, ys, partner):
            rows = _chunk_rows(r0, i)
            o_ref[rows, :] = (y * cos_ref[rows, :] + p * sin_ref[rows, :]).astype(o_ref.dtype)

    _conv_loop(segs, fn)


def _conv_ffn_body(a_ref, x_ref, w_ref, b_ref, o_ref, *, segs):
    shifts = _shift_matrices()

    def fn(si, r0, n, steps):
        gs = [_silu(y) for y in _conv_at(x_ref, w_ref, b_ref, shifts, r0, n, steps)]
        for i, g in zip(steps, gs):
            rows = _chunk_rows(r0, i)
            o_ref[rows, :] = (a_ref[rows, :].astype(F32) * g).astype(o_ref.dtype)

    _conv_loop(segs, fn)


def _conv_hyena_body(x0_ref, x1_ref, v_ref, w0_ref, b0_ref, w1_ref, b1_ref, w2_ref, b2_ref,
                     ox0_ref, ozl_ref, ozc_ref, ozle_ref, ozlo_ref, ozce_ref, ozco_ref, *zs_refs, segs):
    shifts = _shift_matrices()
    half = CONV_CH // 2
    outs = ((ozl_ref, ozle_ref, ozlo_ref), (ozc_ref, ozce_ref, ozco_ref))

    def fn(si, r0, n, steps):
        x0 = _conv_at(x0_ref, w0_ref, b0_ref, shifts, r0, n, steps)
        x1 = _conv_at(x1_ref, w1_ref, b1_ref, shifts, r0, n, steps)
        v = _conv_at(v_ref, w2_ref, b2_ref, shifts, r0, n, steps)
        nat_ref, even_ref, odd_ref = outs[si]
        for i, x0_i, x1_i, v_i in zip(steps, x0, x1, v):
            ox0_ref[_chunk_rows(r0, i), :] = x0_i.astype(ox0_ref.dtype)
            zin = x1_i * v_i
            nat_ref[_chunk_rows(0, i), :] = zin.astype(BF16)
            t0 = r0 + pl.multiple_of(i * CONV_CH, CONV_CH)
            hrows = pl.ds(pl.multiple_of(i * half, half), half)
            for g, zs_ref in enumerate(zs_refs):
                lanes = slice(g * LANES, (g + 1) * LANES)
                zs_ref[pl.ds(t0, CONV_CH), :] = zin[:, lanes]
                even_ref[hrows, lanes] = zs_ref[pl.ds(t0, half, stride=2), :].astype(BF16)
                odd_ref[hrows, lanes] = zs_ref[pl.ds(t0 + 1, half, stride=2), :].astype(BF16)

    _conv_loop(segs, fn)


def _conv_specs(t_in, tc, xcol_blk, wcol_blk):
    return [pl.BlockSpec((None, t_in, tc), lambda b, j: (b, 0, j + xcol_blk)),
            pl.BlockSpec((CONV_W, tc), lambda b, j: (0, j + wcol_blk)),
            pl.BlockSpec((1, tc), lambda b, j: (0, j + wcol_blk))]


def _conv_silu(x, w, b, *, xcol, wcol, width, segs, tc=256):
    nb, t_in, _ = x.shape
    return pl.pallas_call(
        functools.partial(_conv_silu_body, segs=segs),
        out_shape=jax.ShapeDtypeStruct((nb, t_in, width), BF16), grid=(nb, width // tc),
        in_specs=_conv_specs(t_in, tc, xcol // tc, wcol // tc),
        out_specs=pl.BlockSpec((None, t_in, tc), lambda b, j: (b, 0, j)),
        name="conv_silu", compiler_params=_cp(("parallel", "parallel")))(x, w, b)


def _conv_rope(x, w, b, cos, sin, *, width, segs, tc=256):
    nb, t_in, _ = x.shape
    tab = pl.BlockSpec((t_in, tc), lambda b, j: (0, 0))
    return pl.pallas_call(
        functools.partial(_conv_rope_body, segs=segs, tc=tc),
        out_shape=jax.ShapeDtypeStruct((nb, t_in, width), BF16), grid=(nb, width // tc),
        in_specs=_conv_specs(t_in, tc, 0, 0) + [tab, tab],
        out_specs=pl.BlockSpec((None, t_in, tc), lambda b, j: (b, 0, j)),
        name="conv_rope", compiler_params=_cp(("parallel", "parallel")))(x, w, b, cos, sin)


def _conv_ffn(up, w, b, *, segs, rows, tc=256):
    nb, t_in, two_ff = up.shape
    ff = two_ff // 2
    return pl.pallas_call(
        functools.partial(_conv_ffn_body, segs=segs),
        out_shape=jax.ShapeDtypeStruct((nb, rows, ff), BF16), grid=(nb, ff // tc),
        in_specs=[pl.BlockSpec((None, rows, tc), lambda b, j: (b, 0, j))] + _conv_specs(rows, tc, ff // tc, 0),
        out_specs=pl.BlockSpec((None, rows, tc), lambda b, j: (b, 0, j)),
        name="conv_ffn", compiler_params=_cp(("parallel", "parallel")))(up, up, w, b)


def _conv_hyena(pr, w, b, *, xcol, wcol, segs, tc=256):
    nb, t_in, _ = pr.shape
    (_, n_lat), (_, n_ctx) = segs
    wd = HY_WIDTH
    out_rows = (t_in, n_lat, n_ctx, n_lat // 2, n_lat // 2, n_ctx // 2, n_ctx // 2)
    in_specs = [pl.BlockSpec((None, t_in, tc), functools.partial(lambda b, j, o: (b, 0, j + o), o=(xcol + m * wd) // tc))
                for m in range(3)]
    wargs = []
    for m in range(3):
        off = (wcol + m * wd) // tc
        in_specs.append(pl.BlockSpec((CONV_W, tc), functools.partial(lambda b, j, o: (0, j + o), o=off)))
        in_specs.append(pl.BlockSpec((1, tc), functools.partial(lambda b, j, o: (0, j + o), o=off)))
        wargs += [w, b]
    return pl.pallas_call(
        functools.partial(_conv_hyena_body, segs=segs),
        out_shape=tuple(jax.ShapeDtypeStruct((nb, rows, wd), BF16) for rows in out_rows),
        grid=(nb, wd // tc), in_specs=in_specs,
        out_specs=tuple(pl.BlockSpec((None, rows, tc), lambda b, j: (b, 0, j)) for rows in out_rows),
        scratch_shapes=[pltpu.VMEM((t_in, LANES), F32)] * (tc // LANES),
        name="conv_hyena", compiler_params=_cp(("parallel", "parallel")))(pr, pr, pr, *wargs)


def _scan_masks(reverse):
    q = SCAN_Q
    ii = lax.broadcasted_iota(jnp.int32, (q, q), 0)
    jj = lax.broadcasted_iota(jnp.int32, (q, q), 1)
    mask = (jj >= ii) if reverse else (jj <= ii)
    mask_t = (jj <= ii) if reverse else (jj >= ii)
    return mask, mask.astype(F32), mask_t.astype(F32)


def _chunk_map(reverse, n_chunks, n_lat_chunks):
    if reverse:
        return lambda k: n_chunks - 1 - k
    return lambda k: (k + n_lat_chunks) % n_chunks


SSD_GPS = 4


def _ssd_body(*refs, reverse, final):
    (xs_ref, b_ref, c_ref, dtc_ref, dtr_ref, dbc_ref, dbr_ref, alc_ref, alr_ref) = refs[:9]
    if final:
        yf_ref, z_ref, dsk_ref, nw_ref, o_ref, h_ref = refs[9:]
    else:
        o_ref, h_ref = refs[9:]

    @pl.when(pl.program_id(2) == 0)
    def _():
        h_ref[...] = jnp.zeros_like(h_ref)

    q, gw, hd = SCAN_Q, SSD_GW, SSD_HEAD_DIM
    mask, maskf, mask_tf = _scan_masks(reverse)
    groups = range(SSD_GPS)
    pairs = range(SSD_HPG // 2)
    cols = [slice(g * gw, (g + 1) * gw) for g in groups]
    xs = [xs_ref[:, cols[g]].astype(F32) for g in groups]
    bm = [b_ref[:, g * SSD_STATE:(g + 1) * SSD_STATE] for g in groups]
    cm = [c_ref[:, g * SSD_STATE:(g + 1) * SSD_STATE] for g in groups]
    h_prev = [h_ref[g] for g in groups]
    dt_c = [_softplus(dtc_ref[g] + dbc_ref[g]) for g in groups]
    dt_r = [_softplus(dtr_ref[g] + dbr_ref[g]) for g in groups]
    da_c = [dt_c[g] * (-jnp.exp(alc_ref[g])) for g in groups]
    da_r = [dt_r[g] * (-jnp.exp(alr_ref[g])) for g in groups]
    acum_c = [_sel_dot(maskf, da_c[g], 3) for g in groups]
    acum_r = [_dot_sel(da_r[g], mask_tf, 3) for g in groups]
    atot = [jnp.sum(da_c[g], axis=0, keepdims=True) for g in groups]

    er = lax.broadcasted_iota(jnp.int32, (SSD_HPG, gw), 0)
    ec = lax.broadcasted_iota(jnp.int32, (SSD_HPG, gw), 1)
    expand = (ec // hd == er).astype(F32)
    wide = [_dot_sel(jnp.concatenate([dt_c[g], jnp.exp(atot[g] - acum_c[g]), jnp.exp(acum_c[g]),
                                      jnp.broadcast_to(jnp.exp(atot[g]), (SUBLANES, SSD_HPG))], axis=0), expand, 1)
            for g in groups]
    xq = [xs[g] * wide[g][0:q] for g in groups]
    cb = [_dot_nt(cm[g], bm[g]) for g in groups]
    y_off = [wide[g][2 * q:3 * q] * _dot(cm[g], h_prev[g].astype(BF16)) for g in groups]
    lane = lax.broadcasted_iota(jnp.int32, (q, 2 * hd), 1)

    def pair_lhs(g, p):
        ms = [(cb[g] * jnp.where(mask, jnp.exp(acum_c[g][:, r:r + 1] - acum_r[g][r:r + 1, :]), 0.0)).astype(BF16)
              for r in (2 * p, 2 * p + 1)]
        return jnp.concatenate(ms, axis=1)

    def pair_rhs(g, p):
        xp = xq[g][:, 2 * hd * p:2 * hd * (p + 1)]
        return jnp.concatenate([jnp.where(lane < hd, xp, 0.0), jnp.where(lane >= hd, xp, 0.0)], axis=0).astype(BF16)

    y_in = [[_dot(pair_lhs(g, p), pair_rhs(g, p)) for p in pairs] for g in groups]
    y = [y_off[g] + jnp.concatenate(y_in[g], axis=1) for g in groups]
    bt = [bm[g].astype(F32).T.astype(BF16) for g in groups]
    upd = [_dot(bt[g], (xq[g] * wide[g][q:2 * q]).astype(BF16)) for g in groups]
    for g in groups:
        h_ref[g] = wide[g][3 * q:3 * q + 1] * h_prev[g] + upd[g]

    for g in groups:
        if final:
            ytot = y[g] + yf_ref[:, cols[g]] + xs[g] * dsk_ref[:, cols[g]]
            gated = ytot * _silu(z_ref[:, cols[g]].astype(F32))
            o = gated * lax.rsqrt(jnp.mean(gated * gated, axis=-1, keepdims=True) + EPS) * nw_ref[:, cols[g]]
            o_ref[:, cols[g]] = o.astype(o_ref.dtype)
        else:
            o_ref[:, cols[g]] = y[g]


def _ssd(xbc, dt_c, dt_r, db_c, db_r, al_c, al_r, *, direction, n_lat, final_args=None):
    nb, t, _ = xbc.shape
    q, gw, hpg = SCAN_Q, SSD_GW, SSD_HPG
    nch = t // q
    reverse = direction == 1
    cmap = _chunk_map(reverse, nch, n_lat // q)
    gps = SSD_GPS
    sw, gww = gps * SSD_STATE, gps * gw
    b0 = SSD_WIDTH // sw
    c0 = b0 + SSD_GROUPS // gps
    d = direction
    in_specs = [
        pl.BlockSpec((None, q, gww), lambda b, g, k: (b, cmap(k), g)),
        pl.BlockSpec((None, q, sw), lambda b, g, k: (b, cmap(k), b0 + g)),
        pl.BlockSpec((None, q, sw), lambda b, g, k: (b, cmap(k), c0 + g)),
        pl.BlockSpec((None, None, gps, q, hpg), lambda b, g, k: (b, d, g, cmap(k), 0)),
        pl.BlockSpec((None, None, gps, hpg, q), lambda b, g, k: (b, d, g, 0, cmap(k))),
        pl.BlockSpec((None, gps, 1, hpg), lambda b, g, k: (d, g, 0, 0)),
        pl.BlockSpec((None, gps, hpg, 1), lambda b, g, k: (d, g, 0, 0)),
        pl.BlockSpec((None, gps, 1, hpg), lambda b, g, k: (d, g, 0, 0)),
        pl.BlockSpec((None, gps, hpg, 1), lambda b, g, k: (d, g, 0, 0)),
    ]
    args = [xbc, xbc, xbc, dt_c, dt_r, db_c, db_r, al_c, al_r]
    final = final_args is not None
    if final:
        yf, pr, dskip, norm_w = final_args
        in_specs += [pl.BlockSpec((None, q, gww), lambda b, g, k: (b, cmap(k), g)),
                     pl.BlockSpec((None, q, gww), lambda b, g, k: (b, cmap(k), g)),
                     pl.BlockSpec((1, gww), lambda b, g, k: (0, g)),
                     pl.BlockSpec((1, gww), lambda b, g, k: (0, g))]
        args += [yf, pr, dskip, norm_w]
    out_shape = jax.ShapeDtypeStruct((nb, t, SSD_WIDTH), BF16 if final else F32)
    return pl.pallas_call(
        functools.partial(_ssd_body, reverse=reverse, final=final),
        out_shape=out_shape, grid=(nb, SSD_GROUPS // gps, nch), in_specs=in_specs,
        out_specs=pl.BlockSpec((None, q, gww), lambda b, g, k: (b, cmap(k), g)),
        scratch_shapes=[pltpu.VMEM((gps, SSD_STATE, gw), F32)],
        name="ssd_bwd" if reverse else "ssd_fwd",
        compiler_params=_cp(("parallel", "parallel", "arbitrary")))(*args)


def _hyfilt_body(f_ref, w1_ref, b1_ref, w2_ref, b2_ref, w3_ref, fr_ref, dl_ref, o_ref, *, tl):
    f = f_ref[...]
    h = jnp.sin(fr_ref[0:1, :] * (_dot(f, w1_ref[...], HI) + b1_ref[...]))
    h = jnp.sin(fr_ref[1:2, :] * (_dot(h, w2_ref[...], HI) + b2_ref[...]))
    h = _dot(h, w3_ref[...], HI)
    dec = jnp.exp(-f[:, 0:1] * dl_ref[...])
    row = pl.program_id(0) * tl + lax.broadcasted_iota(jnp.int32, (tl, 1), 0)
    wd = HY_WIDTH
    o_ref[:, 0:wd] = (h[:, 0:wd] * dec).astype(o_ref.dtype)
    o_ref[:, wd:2 * wd] = jnp.where(row == 0, 0.0, h[:, wd:2 * wd] * dec).astype(o_ref.dtype)


def _hyena_filters(length, w1, b1, w2, b2, w3, freq, *, tl=256):
    p = LANES
    t = jnp.linspace(0.0, 1.0, length, dtype=F32)[:, None]
    w = 2.0 * math.pi * jnp.arange(length, dtype=F32)[:, None] / length
    f = jnp.linspace(1e-4, HY_BANDS - 1, HY_BANDS, dtype=F32)[None, :]
    feats = jnp.concatenate([t, jnp.cos(f * w), -jnp.sin(f * w)], axis=-1)
    feats = jnp.concatenate([feats[0::2], feats[1::2]], axis=0)
    feats = jnp.pad(feats, ((0, 0), (0, p - HY_EMB)))
    po = p - HY_ORDER
    w1p = jnp.pad(w1, ((0, p - HY_EMB), (0, po)))
    w2p = jnp.pad(w2, ((0, po), (0, po)))
    w3p = jnp.pad(w3, ((0, po), (0, 0)))
    b1p = jnp.pad(b1[None, :], ((0, 0), (0, po)))
    b2p = jnp.pad(b2[None, :], ((0, 0), (0, po)))
    frp = jnp.pad(freq, ((0, 0), (0, po)))
    deltas = jnp.abs(jnp.linspace(math.log(HY_TARGET) / HY_FAST, math.log(HY_TARGET) / HY_SLOW, HY_WIDTH,
                                  dtype=F32))[None, :]
    full = lambda shape: pl.BlockSpec(shape, lambda i: (0,) * len(shape))
    out = pl.pallas_call(
        functools.partial(_hyfilt_body, tl=tl),
        out_shape=jax.ShapeDtypeStruct((length, 2 * HY_WIDTH), BF16), grid=(length // tl,),
        in_specs=[pl.BlockSpec((tl, p), lambda i: (i, 0)), full((p, p)), full((1, p)), full((p, p)), full((1, p)),
                  full((p, 2 * HY_WIDTH)), full((2, p)), full((1, HY_WIDTH))],
        out_specs=pl.BlockSpec((tl, 2 * HY_WIDTH), lambda i: (i, 0)),
        name="hyena_filters", compiler_params=_cp(("parallel",)))(feats, w1p, b1p, w2p, b2p, w3p, frp, deltas)
    return out[None]


def _dft_matrices(length):
    n = 2 * length
    k = jnp.arange(length, dtype=jnp.int32)[:, None]
    t = jnp.arange(length, dtype=jnp.int32)[None, :]
    step = 1 << (int(math.log2(length)) // 2)
    part = jnp.arange(length // step, dtype=jnp.int32)[:, None]
    ang_hi = ((part * step * t) % n).astype(F32) * (2.0 * math.pi / n)
    ang_lo = ((jnp.arange(step, dtype=jnp.int32)[:, None] * t) % n).astype(F32) * (2.0 * math.pi / n)
    ch, sh = jnp.cos(ang_hi)[:, None, :], jnp.sin(ang_hi)[:, None, :]
    cl, sl = jnp.cos(ang_lo)[None, :, :], jnp.sin(ang_lo)[None, :, :]
    c = (ch * cl - sh * sl).reshape(length, length)
    s = (sh * cl + ch * sl).reshape(length, length)
    alt_t = jnp.where(t % 2 == 0, 1.0, -1.0).astype(F32)
    fwd = jnp.concatenate([c, jnp.where(k == 0, alt_t, -s)], axis=0)
    alt_r = jnp.where(k % 2 == 0, 1.0, -1.0).astype(F32)
    inv_re = jnp.where(t == 0, 1.0 / n, (2.0 / n) * c)
    inv_im = jnp.where(t == 0, alt_r / n, (-2.0 / n) * s)
    inv = jnp.concatenate([inv_re, inv_im], axis=1)
    return fwd.astype(BF16)[None], inv.astype(BF16)[None]


def _tile_interleave(x, axis, hm):
    shp = x.shape
    n = shp[axis] // 2
    x = x.reshape(shp[:axis] + (2, n // hm, hm) + shp[axis + 1:])
    return jnp.swapaxes(x, axis, axis + 1).reshape(shp)


def _twiddle_table(length, hm):
    k = np.arange(length // 2, dtype=np.float64)[:, None] * (np.pi / length)
    tab = np.concatenate([np.cos(k), np.sin(k)], axis=0)
    tab = tab.reshape(2, -1, hm, 1).swapaxes(0, 1).reshape(length, 1)
    return jnp.asarray(np.broadcast_to(tab, (length, LANES)).astype(np.float32))


def _lanes(x, width):
    return jnp.tile(x, (1, width // LANES))


def _parity_banks(e, ot, tw, hm):
    er, ei, otr, oti = e[0:hm], e[hm:], ot[0:hm], ot[hm:]
    c, s = _lanes(tw[0:hm], e.shape[1]), _lanes(tw[hm:], e.shape[1])
    orr = otr * c + oti * s
    oi = oti * c - otr * s
    return (er + orr, ei + oi, er - orr, ei - oi), (c, s)


def _is_bin0(row_blk, hm):
    return (row_blk == 0) & (lax.broadcasted_iota(jnp.int32, (hm, 1), 0) == 0)


def _hy_kbank_body(ef_ref, eb_ref, of_ref, ob_ref, tw_ref, ka_ref, kb_ref, *, hm):
    tw = tw_ref[...]
    (afr, afi, bfr, bfi), _ = _parity_banks(ef_ref[...], of_ref[...], tw, hm)
    (abr, abi, bbr, bbi), _ = _parity_banks(eb_ref[...], ob_ref[...], tw, hm)
    first = _is_bin0(pl.program_id(0), hm)
    kai = jnp.where(first, ef_ref[hm:, :] + eb_ref[hm:, :], afi - abi)
    kbi = jnp.where(first, ob_ref[hm:, :] - of_ref[hm:, :], bfi - bbi)
    ka_ref[...] = jnp.concatenate([afr + abr, kai], axis=0)
    kb_ref[...] = jnp.concatenate([bfr + bbr, kbi], axis=0)


def _hy_kbank(ef, of, tw, *, tm, tc=512):
    _, ln, c2 = ef.shape
    c = c2 // 2
    fspec = pl.BlockSpec((None, tm, tc), lambda i, j: (0, i, j))
    bspec = pl.BlockSpec((None, tm, tc), lambda i, j: (0, i, j + c // tc))
    return pl.pallas_call(
        functools.partial(_hy_kbank_body, hm=tm // 2),
        out_shape=(jax.ShapeDtypeStruct((1, ln, c), F32),) * 2, grid=(ln // tm, c // tc),
        in_specs=[fspec, bspec, fspec, bspec, pl.BlockSpec((tm, LANES), lambda i, j: (i, 0))],
        out_specs=(fspec, fspec), name="hy_kbank",
        compiler_params=_cp(("parallel", "parallel")))(ef, ef, of, of, tw)


def _hy_spectrum_epilogue(acc, row_blk, e_ref, tw_ref, ka_ref, kb_ref, *, hm):
    e = e_ref[...]
    (ar, ai, br, bi), (c, s) = _parity_banks(e, acc, tw_ref[...], hm)
    kar, kai, kbr, kbi = ka_ref[0:hm], ka_ref[hm:], kb_ref[0:hm], kb_ref[hm:]
    yar, yai = ar * kar - ai * kai, ar * kai + ai * kar
    ybr, ybi = br * kbr - bi * kbi, br * kbi + bi * kbr
    qr, qi = yar - ybr, yai - ybi
    first = _is_bin0(row_blk, hm)
    y0, yl = ar * kar, br * kbr
    amr, ami = e[hm:], -acc[hm:]
    ymr, ymi = amr * kai - ami * kbi, amr * kbi + ami * kai
    pr = jnp.where(first, y0 + yl, yar + ybr)
    pi = jnp.where(first, 2.0 * ymr, yai + ybi)
    qpr = jnp.where(first, y0 - yl, qr * c - qi * s)
    qpi = jnp.where(first, -2.0 * ymi, qi * c + qr * s)
    return jnp.concatenate([pr, pi], axis=0), jnp.concatenate([qpr, qpi], axis=0)


def _hyena_out_body(*refs, segs):
    ns = len(segs)
    x0_ref, db_ref, o_ref = refs[3 * ns:3 * ns + 3]
    ys_refs = refs[3 * ns + 3:]
    half = CONV_CH // 2
    for si, (r0, n) in enumerate(segs):
        ye_ref, yo_ref, zin_ref = refs[3 * si:3 * si + 3]

        def body(i, carry, r0=r0, ye_ref=ye_ref, yo_ref=yo_ref, zin_ref=zin_ref):
            t0 = pl.multiple_of(i * CONV_CH, CONV_CH)
            hrows = pl.ds(pl.multiple_of(i * half, half), half)
            for g, ys_ref in enumerate(ys_refs):
                lanes = slice(g * LANES, (g + 1) * LANES)
                ys_ref[pl.ds(r0 + t0, half, stride=2), :] = ye_ref[hrows, lanes].astype(F32)
                ys_ref[pl.ds(r0 + t0 + 1, half, stride=2), :] = yo_ref[hrows, lanes].astype(F32)
            rows = pl.ds(r0 + t0, CONV_CH)
            y = jnp.concatenate([ys_ref[rows, :] for ys_ref in ys_refs], axis=1)
            zin = zin_ref[pl.ds(t0, CONV_CH), :].astype(F32)
            o_ref[rows, :] = (x0_ref[rows, :].astype(F32) * (y + zin * db_ref[...])).astype(o_ref.dtype)
            return carry

        lax.fori_loop(0, n // CONV_CH, body, 0, unroll=min(4, n // CONV_CH))


def _hyena_out(parts, x0, dbias, *, segs, tc=256):
    nb, t, c = x0.shape
    in_specs, args = [], []
    for (ye, yo, zin), (_, n) in zip(parts, segs):
        in_specs += [pl.BlockSpec((None, n // 2, tc), lambda b, j: (b, 0, j)),
                     pl.BlockSpec((None, n // 2, tc), lambda b, j: (b, 0, j)),
                     pl.BlockSpec((None, n, tc), lambda b, j: (b, 0, j))]
        args += [ye, yo, zin]
    in_specs += [pl.BlockSpec((None, t, tc), lambda b, j: (b, 0, j)), pl.BlockSpec((1, tc), lambda b, j: (0, j))]
    return pl.pallas_call(
        functools.partial(_hyena_out_body, segs=segs),
        out_shape=jax.ShapeDtypeStruct((nb, t, c), BF16), grid=(nb, c // tc), in_specs=in_specs,
        out_specs=pl.BlockSpec((None, t, tc), lambda b, j: (b, 0, j)),
        scratch_shapes=[pltpu.VMEM((t, LANES), F32)] * (tc // LANES), name="hyena_out",
        compiler_params=_cp(("parallel", "parallel")))(*args, x0, dbias)


def _hyena_longconv(ze, zo, hfb, *, tn=1024):
    nb, h, c = ze.shape
    ln = 2 * h
    tm_s = min(512, ln)
    hm = tm_s // 2
    fwd, inv = _dft_matrices(h)
    fwd = _tile_interleave(fwd, 1, hm)
    inv = _tile_interleave(inv * 0.5, 2, hm)
    tw = _twiddle_table(ln, hm)
    tm_f, tk_f = min(1024, ln), min(2048, h)
    ef = _matmul(fwd, hfb[:, :h], tm=tm_f, tn=tn, tk=tk_f, out_dtype=F32, name="hy_filter_dft_even")
    of = _matmul(fwd, hfb[:, h:], tm=tm_f, tn=tn, tk=tk_f, out_dtype=F32, name="hy_filter_dft_odd")
    ka, kb = _hy_kbank(ef, of, tw, tm=tm_s)
    e = _matmul(fwd, ze, tm=tm_f, tn=tn, tk=tk_f, out_dtype=F32, name="hy_dft_even")
    tile = pl.BlockSpec((None, tm_s, tn), lambda j, b, i, k: (b, i, j))
    ktile = pl.BlockSpec((None, tm_s, tn), lambda j, b, i, k: (0, i, j))
    extra = [(e, tile), (tw, pl.BlockSpec((tm_s, LANES), lambda j, b, i, k: (i, 0))), (ka, ktile), (kb, ktile)]
    p, q = _matmul(fwd, zo, tm=tm_s, tn=tn, tk=tk_f, out_dtype=BF16, extra=extra, n_out=2, batch_inner=True,
                   epilogue=functools.partial(_hy_spectrum_epilogue, hm=hm), name="hy_dft_odd")
    tm_i, tk_i = min(1024, h), min(2048, ln)
    ye = _matmul(inv, p, tm=tm_i, tn=tn, tk=tk_i, out_dtype=BF16, name="hy_idft_even")
    yo = _matmul(inv, q, tm=tm_i, tn=tn, tk=tk_i, out_dtype=BF16, name="hy_idft_odd")
    return ye, yo


def _mlstm_body(*refs, reverse, final):
    q_ref, k_ref, v_ref, gc_ref, gr_ref = refs[:5]
    if final:
        hf_ref, og_ref, nw_ref, o_ref, ct_ref, n_ref, m_ref = refs[5:]
    else:
        o_ref, ct_ref, n_ref, m_ref = refs[5:]
    nh, dk, dv = ML_HEADS, ML_QK_DIM, ML_V_DIM
    scale = dk ** -0.5

    @pl.when(pl.program_id(1) == 0)
    def _():
        ct_ref[...] = jnp.zeros_like(ct_ref)
        n_ref[...] = jnp.zeros_like(n_ref)
        m_ref[...] = jnp.zeros_like(m_ref)

    gc, gr = gc_ref[...], gr_ref[...]
    li_c, lf_c = gc[:, 0:nh], _log_sigmoid(gc[:, nh:2 * nh])
    li_r, lf_r = gr[0:nh, :], _log_sigmoid(gr[nh:2 * nh, :])
    mask, maskf, mask_tf = _scan_masks(reverse)
    bc_c = _dot(maskf, lf_c, HI)
    bc_r = _dot(lf_r, mask_tf, HI)
    btot_c = jnp.sum(lf_c, axis=0, keepdims=True)
    n_all, m_all = n_ref[...], m_ref[...]
    heads = range(nh)
    qs = [q_ref[:, dk * h:dk * (h + 1)] for h in heads]
    ks = [k_ref[:, dk * h:dk * (h + 1)] for h in heads]
    vs = [v_ref[:, dv * h:dv * (h + 1)] for h in heads]
    cts = [ct_ref[h] for h in heads]
    m_s = [m_all[h:h + 1, 0:1] for h in heads]
    n_s = [n_all[h:h + 1, :] for h in heads]
    qk = [_dot_nt(qs[h], ks[h]) for h in heads]
    qc = [_dot(qs[h], cts[h].astype(BF16)) for h in heads]
    dmat = [jnp.where(mask, bc_c[:, h:h + 1] - bc_r[h:h + 1, :] + li_r[h:h + 1, :], -jnp.inf) for h in heads]
    inter = [bc_c[:, h:h + 1] + m_s[h] for h in heads]
    m_t = [jnp.maximum(inter[h], jnp.max(dmat[h], axis=1, keepdims=True)) for h in heads]
    s = [qk[h] * scale * jnp.exp(dmat[h] - m_t[h]) for h in heads]
    dec = [jnp.exp(inter[h] - m_t[h]) for h in heads]
    num = [_dot(s[h].astype(BF16), vs[h]) + (dec[h] * scale) * qc[h] for h in heads]
    qn = [jnp.sum(qs[h].astype(F32) * n_s[h], axis=1, keepdims=True) * scale for h in heads]
    den = [jnp.sum(s[h], axis=1, keepdims=True) + dec[h] * qn[h] for h in heads]
    hh = [num[h] / jnp.maximum(jnp.abs(den[h]), jnp.exp(-m_t[h])) for h in heads]

    wst_c = [btot_c[:, h:h + 1] - bc_c[:, h:h + 1] + li_c[:, h:h + 1] for h in heads]
    wst_r = [btot_c[:, h:h + 1] - bc_r[h:h + 1, :] + li_r[h:h + 1, :] for h in heads]
    m_new = [jnp.maximum(btot_c[:, h:h + 1] + m_s[h], jnp.max(wst_r[h], axis=1, keepdims=True)) for h in heads]
    dstate = [jnp.exp(btot_c[:, h:h + 1] + m_s[h] - m_new[h]) for h in heads]
    kw = [ks[h].astype(F32) * jnp.exp(wst_c[h] - m_new[h]) for h in heads]
    upd = [_dot(kw[h].T.astype(BF16), vs[h]) for h in heads]
    for h in heads:
        ct_ref[h] = dstate[h] * cts[h] + upd[h]
    n_rows = [dstate[h] * n_s[h] + jnp.sum(kw[h], axis=0, keepdims=True) for h in heads]
    m_rows = [jnp.broadcast_to(m_new[h], (1, LANES)) for h in heads]

    for h in heads:
        cols = slice(dv * h, dv * (h + 1))
        if final:
            hs = hh[h] + hf_ref[:, cols]
            hn = hs * lax.rsqrt(jnp.mean(hs * hs, axis=-1, keepdims=True) + EPS) * nw_ref[:, cols]
            o_ref[:, cols] = (hn * _sigmoid(og_ref[:, cols].astype(F32))).astype(o_ref.dtype)
        else:
            o_ref[:, cols] = hh[h]

    n_ref[...] = jnp.concatenate(n_rows, axis=0)
    m_ref[...] = jnp.concatenate(m_rows, axis=0)


def _mlstm(qk, pr, g_c, g_r, *, direction, n_lat, final_args=None):
    nb, t, _ = qk.shape
    q = SCAN_Q
    nch = t // q
    reverse = direction == 1
    cmap = _chunk_map(reverse, nch, n_lat // q)
    d = direction
    in_specs = [
        pl.BlockSpec((None, q, ML_QK_WIDTH), lambda b, k: (b, cmap(k), 0)),
        pl.BlockSpec((None, q, ML_QK_WIDTH), lambda b, k: (b, cmap(k), 1)),
        pl.BlockSpec((None, q, ML_WIDTH), lambda b, k: (b, cmap(k), 2 * ML_QK_WIDTH // ML_WIDTH)),
        pl.BlockSpec((None, None, q, 2 * ML_HEADS), lambda b, k: (b, d, cmap(k), 0)),
        pl.BlockSpec((None, None, 2 * ML_HEADS, q), lambda b, k: (b, d, 0, cmap(k))),
    ]
    args = [qk, qk, pr, g_c, g_r]
    final = final_args is not None
    if final:
        hf, norm_w = final_args
        in_specs += [pl.BlockSpec((None, q, ML_WIDTH), lambda b, k: (b, cmap(k), 0)),
                     pl.BlockSpec((None, q, ML_WIDTH), lambda b, k: (b, cmap(k), 2 * ML_QK_WIDTH // ML_WIDTH + 1)),
                     pl.BlockSpec((1, ML_WIDTH), lambda b, k: (0, 0))]
        args += [hf, pr, norm_w]
    out_shape = jax.ShapeDtypeStruct((nb, t, ML_WIDTH), BF16 if final else F32)
    return pl.pallas_call(
        functools.partial(_mlstm_body, reverse=reverse, final=final),
        out_shape=out_shape, grid=(nb, nch), in_specs=in_specs,
        out_specs=pl.BlockSpec((None, q, ML_WIDTH), lambda b, k: (b, cmap(k), 0)),
        scratch_shapes=[pltpu.VMEM((ML_HEADS, ML_QK_DIM, ML_V_DIM), F32), pltpu.VMEM((ML_HEADS, ML_QK_DIM), F32),
                        pltpu.VMEM((ML_HEADS, LANES), F32)],
        name="mlstm_bwd" if reverse else "mlstm_fwd",
        compiler_params=_cp(("parallel", "arbitrary")))(*args)


def _na_window_start(j, rows):
    return jnp.clip(j * NA_QROWS - NA_ROWS // 2, 0, rows - NA_KROWS)


def _na_body(q_ref, k_ref, v_ref, tab_ref, qw_ref, kw_ref, o_ref, kn_ref, v1_ref, *, n_lat, n_ctx):
    dh = NA_HEAD_DIM
    scale = dh ** -0.5
    rows = n_lat // GRID_W
    j = pl.program_id(2)
    nq, nk = NA_QROWS * GRID_W, NA_KROWS * GRID_W

    @pl.when(j == 0)
    def _():
        one_hot = (lax.broadcasted_iota(jnp.int32, (LANES, LANES), 1) == 0).astype(v1_ref.dtype)

        def body(i, carry):
            r = pl.ds(pl.multiple_of(i * LANES, LANES), LANES)
            kk = k_ref[r, :].astype(F32)
            kn = kk * lax.rsqrt(jnp.mean(kk * kk, axis=-1, keepdims=True) + EPS) * kw_ref[...]
            kn_ref[r, :] = kn.astype(kn_ref.dtype)
            v1_ref[r, 0:dh] = v_ref[r, :]
            v1_ref[r, dh:dh + LANES] = one_hot
            return carry

        lax.fori_loop(0, (n_lat + n_ctx) // LANES, body, 0)

    start = _na_window_start(j, rows)
    win = pl.ds(pl.multiple_of(start * GRID_W, GRID_W), nk)
    ctx = pl.ds(n_lat, n_ctx)
    parts = range(NA_PARTS)
    rpp = NA_QROWS // NA_PARTS
    pq = rpp * GRID_W
    qq = [q_ref[i * pq:(i + 1) * pq, :].astype(F32) for i in parts]
    qn = [(qq[i] * lax.rsqrt(jnp.mean(qq[i] * qq[i], axis=-1, keepdims=True) + EPS)
           * (qw_ref[...] * scale)).astype(BF16) for i in parts]
    sw = [_dot_nt(qn[i], kn_ref[win, :]) for i in parts]
    sc = [_dot_nt(qn[i], kn_ref[ctx, :]) for i in parts]

    def with_bias(i):
        out = []
        for r in range(rpp):
            d0 = start - (j * NA_QROWS + i * rpp + r) + NA_ROWS - 1 + NA_QROWS
            out.append(sw[i][r * GRID_W:(r + 1) * GRID_W] + tab_ref[d0])
        return jnp.concatenate(out, axis=0)

    sb = [with_bias(i) for i in parts]
    m = [jnp.maximum(jnp.max(sb[i], axis=1, keepdims=True), jnp.max(sc[i], axis=1, keepdims=True)) for i in parts]
    pw = [jnp.exp(sb[i] - m[i]) for i in parts]
    pc = [jnp.exp(sc[i] - m[i]) for i in parts]
    o = [_dot(pw[i].astype(BF16), v1_ref[win, :]) + _dot(pc[i].astype(BF16), v1_ref[ctx, :]) for i in parts]
    for i in parts:
        o_ref[i * pq:(i + 1) * pq, :] = (o[i][:, 0:dh] / o[i][:, dh:dh + 1]).astype(o_ref.dtype)


def _na_bias_table(rpb, rows):
    nd = 2 * NA_QROWS
    valid = np.zeros((nd, NA_KROWS), bool)
    seen = np.zeros((nd,), bool)
    for j in range(rows // NA_QROWS):
        start = int(np.clip(j * NA_QROWS - NA_ROWS // 2, 0, rows - NA_KROWS))
        for qr in range(NA_QROWS):
            r = j * NA_QROWS + qr
            rs = int(np.clip(r - NA_ROWS // 2, 0, rows - NA_ROWS))
            d = start - r + NA_ROWS - 1 + NA_QROWS
            v = np.array([(rs <= start + i < rs + NA_ROWS) for i in range(NA_KROWS)])
            assert 0 <= d < nd and (not seen[d] or (valid[d] == v).all())
            valid[d], seen[d] = v, True
    col = np.arange(GRID_W)
    cs = np.clip(col - NA_COLS // 2, 0, GRID_W - NA_COLS)
    in_win = (col[None, :] >= cs[:, None]) & (col[None, :] < cs[:, None] + NA_COLS)
    ndr, ndc, w = 2 * NA_ROWS - 1, 2 * NA_COLS - 1, GRID_W
    dr = (np.arange(nd)[:, None] - NA_QROWS) + np.arange(NA_KROWS)[None, :]
    ok = valid[:, None, :, None] & in_win[None, :, None, :] & (dr >= 0)[:, None, :, None] \
        & (dr < ndr)[:, None, :, None]
    ext = jnp.pad(rpb, ((0, 0), (0, 0), (w, w)), mode="edge")
    c0 = w + NA_COLS - 1
    t1 = jnp.stack([ext[:, :, c0 - qc:c0 - qc + w] for qc in range(w)], axis=2)
    t1p = jnp.pad(t1, ((0, 0), (NA_QROWS, nd + NA_KROWS - 1 - NA_QROWS - ndr), (0, 0), (0, 0)))
    wide = jnp.concatenate([t1p[:, m] for m in range(nd + NA_KROWS - 1)], axis=-1)
    wpad = LANES * pl.cdiv(wide.shape[-1], LANES)
    wide = jnp.pad(wide, ((0, 0), (0, 0), (0, wpad - wide.shape[-1])))
    okf = jnp.asarray(ok.reshape(nd, w, NA_KROWS * w).astype(np.float32))
    nk = NA_KROWS * w

    def body(wide_ref, ok_ref, o_ref):
        d = pl.program_id(1)
        shifted = pltpu.roll(wide_ref[...], (wpad - d * w) % wpad, 1)
        o_ref[...] = jnp.where(ok_ref[...] != 0.0, shifted[:, 0:nk], -jnp.inf)

    return pl.pallas_call(
        body, out_shape=jax.ShapeDtypeStruct((rpb.shape[0], nd, w, nk), F32), grid=(rpb.shape[0], nd),
        in_specs=[pl.BlockSpec((None, w, wpad), lambda h, d: (h, 0, 0)),
                  pl.BlockSpec((None, w, nk), lambda h, d: (d, 0, 0))],
        out_specs=pl.BlockSpec((None, None, w, nk), lambda h, d: (h, d, 0, 0)),
        name="na_bias_table", compiler_params=_cp(("parallel", "parallel")))(wide, okf)


def _na(pr, tab, q_w, k_w, *, n_lat, n_ctx, qcol):
    nb, t, _ = pr.shape
    dh, nh = NA_HEAD_DIM, NA_HEADS
    nq = NA_QROWS * GRID_W
    qb = qcol // dh
    nd = tab.shape[1]
    return pl.pallas_call(
        functools.partial(_na_body, n_lat=n_lat, n_ctx=n_ctx),
        out_shape=jax.ShapeDtypeStruct((nb, n_lat, NA_WIDTH), BF16), grid=(nb, nh, n_lat // nq),
        in_specs=[pl.BlockSpec((None, nq, dh), lambda b, h, j: (b, j, qb + h)),
                  pl.BlockSpec((None, t, dh), lambda b, h, j: (b, 0, qb + nh + h)),
                  pl.BlockSpec((None, t, dh), lambda b, h, j: (b, 0, qb + 2 * nh + h)),
                  pl.BlockSpec((None, nd, GRID_W, NA_KROWS * GRID_W), lambda b, h, j: (h, 0, 0, 0)),
                  pl.BlockSpec((1, dh), lambda b, h, j: (0, 0)),
                  pl.BlockSpec((1, dh), lambda b, h, j: (0, 0))],
        out_specs=pl.BlockSpec((None, nq, dh), lambda b, h, j: (b, j, h)),
        scratch_shapes=[pltpu.VMEM((t, dh), BF16), pltpu.VMEM((t, dh + LANES), BF16)], name="na",
        compiler_params=_cp(("parallel", "parallel", "arbitrary")))(pr, pr, pr, tab, q_w, k_w)


def _rope_tables(n_lat, n_ctx, width):
    dh = ML_QK_DIM
    nf = dh // 4
    t = jnp.arange(n_lat)
    inv = ROPE_THETA ** (-jnp.arange(nf, dtype=F32) / nf)
    lane = np.arange(dh)
    pos = jnp.where((lane < dh // 2)[None, :], (t // GRID_W)[:, None], (t % GRID_W)[:, None]).astype(F32)
    ang = pos * inv[lane % nf][None, :]
    sign = np.where((lane % (dh // 2)) < nf, -1.0, 1.0).astype(np.float32)
    cos = jnp.concatenate([jnp.cos(ang), jnp.ones((n_ctx, dh), F32)], axis=0)
    sin = jnp.concatenate([jnp.sin(ang) * sign[None, :], jnp.zeros((n_ctx, dh), F32)], axis=0)
    reps = width // dh
    return jnp.tile(cos, (1, reps)), jnp.tile(sin, (1, reps))


def _ada_mod(c, c_ctx, w_all, b_all, layer):
    nb, d = c.shape
    rows = 16
    cv = jnp.concatenate([c, c_ctx[None, :], jnp.zeros((rows - nb - 1, d), F32)], axis=0)
    a = (cv * _sigmoid(cv)).astype(BF16)[None]
    out = _matmul(a, w_all, w_index=layer, tm=rows, tn=1024, tk=d, out_dtype=F32, name="ada_mod")[0]
    out = out[:nb + 1] + b_all[layer][None, :]
    return out.reshape(nb + 1, 6, d).transpose(1, 0, 2)


def _lat_ctx(m, nb):
    ctx = jnp.broadcast_to(m[nb][None, :], (nb, m.shape[1]))
    return jnp.stack([m[:nb], ctx], axis=1)[:, :, None, :]


def _pad_cols(w, n):
    return jnp.pad(w, ((0, 0), (0, n - w.shape[1])))


def _col_row_forms(g, groups):
    nb, t, _ = g.shape
    g5 = g.reshape(nb, t, 2, groups, -1)
    return g5.transpose(0, 2, 3, 1, 4), g5.transpose(0, 2, 3, 4, 1)


ROW_TILES = (1088, 1024, 512, 256)


def _row_tile(rows):
    return next(tm for tm in ROW_TILES if rows % tm == 0)


def _mod_pair(mods, shift, scale, nb):
    return jnp.stack([_lat_ctx(mods[shift], nb), _lat_ctx(mods[scale], nb)], axis=1)


def _in_projection(u, w_all, layer, *, rows, name, cols=None):
    return _matmul(u, w_all, w_index=layer, cols=cols, tm=_row_tile(rows), tn=1024, tk=u.shape[2], out_dtype=BF16,
                   rows=rows, name=name)


def _out_projection(a_list, w_all, x, gate, layer, *, rows, n_lat, next_norm, name):
    tm = _row_tile(rows)
    tn = 512 if w_all.shape[1] <= 2 * D_MODEL else 256
    x = _matmul_gated_residual_rows(a_list, w_all, x, gate, w_index=layer, tm=tm, tn=tn, rows=rows, n_lat=n_lat,
                                    name=name)
    if next_norm is None:
        return x, None
    return x, _rmsmod(x, *next_norm, tm=tm, rows=rows, n_lat=n_lat)


def _conv_ffn_block(x, u, mods, w_up_all, conv_w, conv_b, w_down_all, layer, *, rows, n_lat, segs, next_norm):
    nb = x.shape[0]
    up = _in_projection(u, w_up_all, layer, rows=rows, name="ffn_up")
    hid = _conv_ffn(up, conv_w, conv_b[None, :], segs=segs, rows=rows)
    return _out_projection((hid,), w_down_all, x, _lat_ctx(mods[5], nb), layer, rows=rows, n_lat=n_lat,
                           next_norm=next_norm, name="ffn_down")


def _even_layer(x, u, mods, w_in_all, conv_w, conv_b, dt_bias, a_log, d_skip, ssd_norm_w,
                hy_w1, hy_b1, hy_w2, hy_b2, hy_w3, hy_freq, hy_bias, w_out_all, j, *, n_lat, n_ctx, next_norm):
    nb, t, d = x.shape
    segs = ((0, n_lat), (n_lat, n_ctx))
    tm = _row_tile(t)
    n_main = SSD_WIDTH + EV_CONV_CH
    pr = _in_projection(u, w_in_all, j, rows=t, cols=n_main, name="ev_in")
    w_dt = _pad_cols(w_in_all[j, :, n_main:], LANES).astype(BF16)
    dtp = _matmul(u, w_dt[None], tm=tm, tn=LANES, tk=d, out_dtype=F32, name="ev_in_dt")[:, :, :2 * SSD_HEADS]
    cb = conv_b[None, :]
    xbc = _conv_silu(pr, conv_w, cb, xcol=SSD_WIDTH, wcol=0, width=SSD_XBC, segs=segs)
    x0, zin_l, zin_c, zle, zlo, zce, zco = _conv_hyena(pr, conv_w, cb, xcol=SSD_WIDTH + SSD_XBC, wcol=SSD_XBC,
                                                       segs=segs)

    dt_c, dt_r = _col_row_forms(dtp, SSD_GROUPS)
    db = dt_bias.reshape(2, SSD_GROUPS, 1, SSD_HPG)
    al = a_log.reshape(2, SSD_GROUPS, 1, SSD_HPG)
    scan_args = (xbc, dt_c, dt_r, db, db.transpose(0, 1, 3, 2), al, al.transpose(0, 1, 3, 2))
    yf = _ssd(*scan_args, direction=0, n_lat=n_lat)
    dsk = jnp.repeat(d_skip, SSD_HEAD_DIM)[None, :]
    y_ssd = _ssd(*scan_args, direction=1, n_lat=n_lat, final_args=(yf, pr, dsk, ssd_norm_w[None, :]))

    hyb = hy_bias[None, :]
    filt = functools.partial(_hyena_filters, w1=hy_w1, b1=hy_b1, w2=hy_w2, b2=hy_b2, w3=hy_w3, freq=hy_freq)
    lat = _hyena_longconv(zle, zlo, filt(n_lat)) + (zin_l,)
    ctx = _hyena_longconv(zce, zco, filt(n_ctx)) + (zin_c,)
    y_hy = _hyena_out((lat, ctx), x0, hyb, segs=segs)
    return _out_projection((y_ssd, y_hy), w_out_all, x, _lat_ctx(mods[2], nb), j, rows=t, n_lat=n_lat,
                           next_norm=next_norm, name="ev_out")


def _odd_layer_latent(x, u, mods, w_in, conv_w, conv_b, gate_b, ml_norm_w, q_norm_w, k_norm_w, rpb,
                      w_out_all, j, *, n_lat, n_ctx, next_norm):
    nb, t, d = x.shape
    segs = ((0, n_lat), (n_lat, n_ctx))
    tm = _row_tile(t)
    o3 = 2 * ML_QK_WIDTH + 2 * ML_WIDTH
    o4 = o3 + 4 * ML_HEADS
    wb = w_in.astype(BF16)
    w_main = jnp.concatenate([wb[:, :o3], wb[:, o4:]], axis=1)
    pr = _in_projection(u, w_main[None], 0, rows=t, name="od_in")
    gates = _matmul(u, _pad_cols(wb[:, o3:o4], LANES)[None], tm=tm, tn=LANES, tk=d, out_dtype=F32,
                    name="od_in_gates")[:, :, :4 * ML_HEADS] + gate_b.reshape(-1)
    cos, sin = _rope_tables(n_lat, n_ctx, 256)
    qk = _conv_rope(pr, conv_w, conv_b[None, :], cos, sin, width=2 * ML_QK_WIDTH, segs=segs)
    g5 = gates.reshape(nb, t, 2, 2 * ML_HEADS)
    g_c, g_r = g5.transpose(0, 2, 1, 3), g5.transpose(0, 2, 3, 1)
    hf = _mlstm(qk, pr, g_c, g_r, direction=0, n_lat=n_lat)
    y_ml = _mlstm(qk, pr, g_c, g_r, direction=1, n_lat=n_lat, final_args=(hf, ml_norm_w[None, :]))
    tab = _na_bias_table(rpb, n_lat // GRID_W)
    y_na = _na(pr, tab, q_norm_w[None, :], k_norm_w[None, :], n_lat=n_lat, n_ctx=n_ctx, qcol=o3)
    return _out_projection((y_ml, y_na), w_out_all, x, _lat_ctx(mods[2], nb), j, rows=n_lat, n_lat=n_lat,
                           next_norm=next_norm, name="od_out")


def kernel(x, c, ctx, c_ctx, ada_w, ada_b, norm_w, ev_w_in, ev_conv_w, ev_conv_b, ssd_dt_bias, ssd_a_log, ssd_d,
           ssd_norm_w, hy_w1, hy_b1, hy_w2, hy_b2, hy_w3, hy_freq, hy_bias, ev_w_out, od_w_in, ml_conv_w, ml_conv_b,
           ml_gate_b, ml_norm_w, na_q_norm_w, na_k_norm_w, na_rpb, od_w_out, ffn_w_up, ffn_conv_w, ffn_conv_b,
           ffn_w_down):
    depth = ada_w.shape[0]
    assert depth == 2, "layer schedule below is written for one even and one odd layer"
    n_lat, n_ctx = x.shape[1], ctx.shape[1]
    t = n_lat + n_ctx
    both = ((0, n_lat), (n_lat, n_ctx))
    xs = jnp.concatenate([x, ctx], axis=1)

    nb = x.shape[0]
    w_down = ffn_w_down.astype(BF16)
    mods0 = _ada_mod(c, c_ctx, ada_w, ada_b, 0)
    mods1 = _ada_mod(c, c_ctx, ada_w, ada_b, 1)
    norm = lambda layer, sub, mods: (norm_w[layer, sub][None, :], _mod_pair(mods, 3 * sub, 3 * sub + 1, nb))

    w0, mod0 = norm(0, 0, mods0)
    u = _rmsmod(xs, w0, mod0, tm=_row_tile(t), rows=t, n_lat=n_lat)
    xs, u = _even_layer(xs, u, mods0, ev_w_in, ev_conv_w[0], ev_conv_b[0], ssd_dt_bias[0], ssd_a_log[0],
                        ssd_d[0], ssd_norm_w[0], hy_w1[0], hy_b1[0], hy_w2[0], hy_b2[0], hy_w3[0], hy_freq[0],
                        hy_bias[0], ev_w_out.astype(BF16), 0, n_lat=n_lat, n_ctx=n_ctx, next_norm=norm(0, 1, mods0))
    xs, u = _conv_ffn_block(xs, u, mods0, ffn_w_up, ffn_conv_w[0], ffn_conv_b[0], w_down, 0,
                            rows=t, n_lat=n_lat, segs=both, next_norm=norm(1, 0, mods1))
    xl, u = _odd_layer_latent(xs, u, mods1, od_w_in[0], ml_conv_w[0], ml_conv_b[0], ml_gate_b[0], ml_norm_w[0],
                              na_q_norm_w[0], na_k_norm_w[0], na_rpb[0], od_w_out.astype(BF16), 0,
                              n_lat=n_lat, n_ctx=n_ctx, next_norm=norm(1, 1, mods1))
    out, _ = _conv_ffn_block(xl, u, mods1, ffn_w_up, ffn_conv_w[1], ffn_conv_b[1], w_down, 1,
                             rows=n_lat, n_lat=n_lat, segs=((0, n_lat),), next_norm=None)
    return out
```

```python
import functools
import math

import numpy as np
import jax
import jax.numpy as jnp
from jax import lax
from jax.experimental import pallas as pl
from jax.experimental.pallas import tpu as pltpu

F32 = jnp.float32
BF16 = jnp.bfloat16
HI = lax.Precision.HIGHEST

D_MODEL = 2048
SEQ = 4096
CTX_LEN = 256
GRID_W = 64
EPS = 1e-6
CONV_W = 3
SSD_WIDTH = D_MODEL
SSD_HEAD_DIM = 64
SSD_HEADS = SSD_WIDTH // SSD_HEAD_DIM
SSD_STATE = 128
SSD_GROUPS = 4
SSD_HPG = SSD_HEADS // SSD_GROUPS
SSD_GW = SSD_WIDTH // SSD_GROUPS
SSD_XBC = SSD_WIDTH + 2 * SSD_GROUPS * SSD_STATE
HY_WIDTH = D_MODEL
HY_EMB = 33
HY_BANDS = (HY_EMB - 1) // 2
HY_ORDER = 64
HY_FAST = 0.3
HY_SLOW = 1.5
HY_TARGET = 1e-2
ML_HEADS = 8
ML_WIDTH = D_MODEL
ML_V_DIM = ML_WIDTH // ML_HEADS
ML_QK_DIM = ML_V_DIM // 2
ML_QK_WIDTH = ML_HEADS * ML_QK_DIM
ROPE_THETA = 10000.0
NA_WIDTH = D_MODEL
NA_HEAD_DIM = 128
NA_HEADS = NA_WIDTH // NA_HEAD_DIM
NA_ROWS = 8
NA_COLS = 16
NA_QROWS = 8
NA_KROWS = 16
NA_PARTS = 2
NA_TAB_ROWS = 4
NA_PART_KROWS = 12
D_FF = 256 * ((8 * D_MODEL // 3 + 255) // 256)
EV_CONV_CH = SSD_XBC + 3 * HY_WIDTH
SCAN_Q = 128
LANES = 128
SUBLANES = 8
VMEM_LIMIT = 48 * 1024 * 1024
VMEM_LIMIT_BIG = 58 * 1024 * 1024


def _cp(sem, vmem=VMEM_LIMIT):
    return pltpu.CompilerParams(dimension_semantics=sem, vmem_limit_bytes=vmem)


def _sigmoid(x):
    return 0.5 * jnp.tanh(0.5 * x) + 0.5


def _silu(x):
    return x * _sigmoid(x)


def _softplus(x):
    return jnp.maximum(x, 0.0) + jnp.log(1.0 + jnp.exp(-jnp.abs(x)))


def _log_sigmoid(x):
    return -_softplus(-x)


def _dot(a, b, precision=None):
    return jnp.dot(a, b, preferred_element_type=F32, precision=precision)


def _dot_nt(a, b):
    return lax.dot_general(a, b, (((1,), (1,)), ((), ())), preferred_element_type=F32)


def _split_bf16(x, terms):
    out = []
    for _ in range(terms):
        part = x.astype(BF16)
        out.append(part)
        x = x - part.astype(F32)
    return out


def _dot_sel(x, sel, terms):
    selb = sel.astype(BF16)
    return sum(_dot(part, selb) for part in _split_bf16(x, terms))


def _sel_dot(sel, x, terms):
    selb = sel.astype(BF16)
    return sum(_dot(selb, part) for part in _split_bf16(x, terms))


def _mm_body(*refs, nk, n_extra, epilogue, cache_w, n_out, batch_axis):
    a_ref, w_ref = refs[0], refs[1]
    extra = refs[2:2 + n_extra]
    o_refs = refs[2 + n_extra:2 + n_extra + n_out]
    scratch = refs[2 + n_extra + n_out:]
    row_blk = pl.program_id(3 - batch_axis)
    if cache_w:
        wb_ref = scratch[0]

        @pl.when((pl.program_id(batch_axis) == 0) & (row_blk == 0))
        def _():
            wb_ref[...] = w_ref[...].astype(BF16)

        w_val = wb_ref[...]
    else:
        w_val = w_ref[...].astype(BF16)
    part = _dot(a_ref[...].astype(BF16), w_val)

    def finish(acc):
        vals = acc if epilogue is None else epilogue(acc, row_blk, *extra)
        for o_ref, val in zip(o_refs, vals if n_out > 1 else (vals,)):
            o_ref[...] = val.astype(o_ref.dtype)

    if nk == 1:
        finish(part)
        return
    acc_ref = scratch[0]
    k = pl.program_id(3)

    @pl.when(k == 0)
    def _():
        acc_ref[...] = part

    @pl.when(k > 0)
    def _():
        acc_ref[...] += part

    @pl.when(k == nk - 1)
    def _():
        finish(acc_ref[...])


def _matmul(a, w, *, tm, tn, tk, out_dtype, rows=None, cols=None, w_index=None, extra=(), epilogue=None,
            n_out=1, batch_inner=False, vmem=VMEM_LIMIT, name="mm"):
    ba, ta, kdim = a.shape
    bw, kdim2, n = w.shape
    assert kdim == kdim2
    nb = ba if w_index is not None else max(ba, bw)
    rows = ta if rows is None else rows
    n = n if cols is None else cols
    assert rows % tm == 0 and n % tn == 0 and kdim % tk == 0
    ni, nj, nk = rows // tm, n // tn, kdim // tk
    order = (lambda f: (lambda j, i, b, k: f(j, b, i, k))) if batch_inner else (lambda f: f)
    respec = lambda spec: pl.BlockSpec(spec.block_shape, order(spec.index_map))
    a_map = (lambda j, b, i, k: (b, i, k)) if ba > 1 else (lambda j, b, i, k: (0, i, k))
    if w_index is not None:
        w_map = lambda j, b, i, k: (w_index, k, j)
    else:
        w_map = (lambda j, b, i, k: (b, k, j)) if bw > 1 else (lambda j, b, i, k: (0, k, j))
    cache_w = w.dtype != BF16 and nk == 1 and (w_index is not None or bw == 1)
    in_specs = [pl.BlockSpec((None, tm, tk), order(a_map)), pl.BlockSpec((None, tk, tn), order(w_map))]
    args = [a, w]
    for arr, spec in extra:
        args.append(arr)
        in_specs.append(respec(spec))
    out_shape = jax.ShapeDtypeStruct((nb, rows, n), out_dtype)
    out_spec = pl.BlockSpec((None, tm, tn), order(lambda j, b, i, k: (b, i, j)))
    if n_out > 1:
        out_shape, out_spec = (out_shape,) * n_out, (out_spec,) * n_out
    scratch = [pltpu.VMEM((tm, tn), F32)] if nk > 1 else []
    if cache_w:
        scratch = [pltpu.VMEM((tk, tn), BF16)]
    body = functools.partial(_mm_body, nk=nk, n_extra=len(extra), epilogue=epilogue, cache_w=cache_w, n_out=n_out,
                             batch_axis=2 if batch_inner else 1)
    inner = "arbitrary" if cache_w else "parallel"
    grid = (nj, ni, nb, nk) if batch_inner else (nj, nb, ni, nk)
    return pl.pallas_call(
        body, out_shape=out_shape, grid=grid, in_specs=in_specs, out_specs=out_spec,
        scratch_shapes=scratch, name=name,
        compiler_params=_cp(("parallel", inner, inner, "arbitrary"), vmem))(*args)


def _row_is_lat(row_blk, tm, n_lat):
    rows = row_blk * tm + lax.broadcasted_iota(jnp.int32, (tm, 1), 0)
    return rows < n_lat


def _gated_residual_epilogue(acc, row_blk, res_ref, gate_ref, *, tm, n_lat):
    gate = jnp.where(_row_is_lat(row_blk, tm, n_lat), gate_ref[0], gate_ref[1])
    return res_ref[...] + gate * acc


def _gated_residual_rows_body(*refs, ks, tm, n_lat):
    n_a = len(ks)
    a_refs, w_ref, res_ref, gate_ref, o_ref = refs[:n_a], refs[n_a], refs[n_a + 1], refs[n_a + 2], refs[n_a + 3]
    acc, lo = None, 0
    for a_ref, kk in zip(a_refs, ks):
        part = _dot(a_ref[...], w_ref[lo:lo + kk, :])
        acc = part if acc is None else acc + part
        lo += kk
    o_ref[...] = _gated_residual_epilogue(acc, pl.program_id(1), res_ref, gate_ref, tm=tm, n_lat=n_lat)


def _matmul_gated_residual_rows(a_list, w, res, gate, *, w_index, tm, tn, rows, n_lat, name):
    nb = a_list[0].shape[0]
    _, kdim, n = w.shape
    ks = tuple(x.shape[2] for x in a_list)
    assert sum(ks) == kdim and rows % tm == 0 and n % tn == 0
    in_specs = [pl.BlockSpec((None, tm, kk), lambda b, i, j: (b, i, 0)) for kk in ks]
    in_specs += [pl.BlockSpec((None, kdim, tn), lambda b, i, j: (w_index, 0, j)),
                 pl.BlockSpec((None, tm, tn), lambda b, i, j: (b, i, j)),
                 pl.BlockSpec((None, 2, 1, tn), lambda b, i, j: (b, 0, 0, j))]
    return pl.pallas_call(
        functools.partial(_gated_residual_rows_body, ks=ks, tm=tm, n_lat=n_lat),
        out_shape=jax.ShapeDtypeStruct((nb, rows, n), F32), grid=(nb, rows // tm, n // tn), in_specs=in_specs,
        out_specs=pl.BlockSpec((None, tm, tn), lambda b, i, j: (b, i, j)), name=name,
        compiler_params=_cp(("parallel", "parallel", "parallel")))(*a_list, w, res, gate)


def _modulated_rmsnorm(x, is_lat, w_ref, mod_ref):
    y = x * lax.rsqrt(jnp.mean(x * x, axis=-1, keepdims=True) + EPS) * w_ref[...]
    shift = jnp.where(is_lat, mod_ref[0, 0], mod_ref[0, 1])
    scale = jnp.where(is_lat, mod_ref[1, 0], mod_ref[1, 1])
    return y * (1.0 + scale) + shift


def _rmsmod_body(x_ref, w_ref, mod_ref, o_ref, *, tm, n_lat):
    is_lat = _row_is_lat(pl.program_id(2), tm, n_lat)
    o_ref[...] = _modulated_rmsnorm(x_ref[...], is_lat, w_ref, mod_ref).astype(o_ref.dtype)


def _rmsmod(x, w, mod, *, tm, rows, n_lat):
    nb, _, d = x.shape
    tm = tm // 4
    assert rows % tm == 0 and tm % 16 == 0
    body = functools.partial(_rmsmod_body, tm=tm, n_lat=n_lat)
    return pl.pallas_call(
        body, out_shape=jax.ShapeDtypeStruct((nb, rows, d), BF16), grid=(1, nb, rows // tm),
        in_specs=[pl.BlockSpec((None, tm, d), lambda j, b, i: (b, i, 0)),
                  pl.BlockSpec((1, d), lambda j, b, i: (0, 0)),
                  pl.BlockSpec((None, 2, 2, 1, d), lambda j, b, i: (b, 0, 0, 0, 0))],
        out_specs=pl.BlockSpec((None, tm, d), lambda j, b, i: (b, i, 0)),
        name="rmsmod", compiler_params=_cp(("parallel", "parallel", "parallel")))(x, w, mod)


CONV_CH = 128
CONV_HALO = 16
CONV_STEPS = 4


def _shift_matrices():
    r = lax.broadcasted_iota(jnp.int32, (CONV_CH, CONV_CH + 2 * CONV_HALO), 0)
    c = lax.broadcasted_iota(jnp.int32, (CONV_CH, CONV_CH + 2 * CONV_HALO), 1)
    return (c == r + CONV_HALO - 1).astype(BF16), (c == r + CONV_HALO + 1).astype(BF16)


def _chunk_rows(r0, i):
    return pl.ds(r0 + pl.multiple_of(i * CONV_CH, CONV_CH), CONV_CH)


def _conv_at(x_ref, w_ref, b_ref, shifts, r0, n, steps):
    cur = [x_ref[_chunk_rows(r0, i), :] for i in steps]
    zero = jnp.zeros((CONV_HALO, cur[0].shape[1]), cur[0].dtype)
    lo = [pl.multiple_of(jnp.maximum(i * CONV_CH - CONV_HALO, 0), CONV_HALO) for i in steps]
    hi = [pl.multiple_of(jnp.minimum((i + 1) * CONV_CH, n - CONV_HALO), CONV_HALO) for i in steps]
    tail = [jnp.where(i > 0, x_ref[pl.ds(r0 + l, CONV_HALO), :], zero) for i, l in zip(steps, lo)]
    head = [jnp.where(i < n // CONV_CH - 1, x_ref[pl.ds(r0 + h, CONV_HALO), :], zero) for i, h in zip(steps, hi)]
    win = [jnp.concatenate([t, c, h], axis=0) for t, c, h in zip(tail, cur, head)]
    prev = [_dot(shifts[0], w) for w in win]
    nxt = [_dot(shifts[1], w) for w in win]
    return [w_ref[0:1, :] * p + w_ref[1:2, :] * c.astype(F32) + w_ref[2:3, :] * x + b_ref[...]
            for p, c, x in zip(prev, cur, nxt)]


def _conv_loop(segs, fn):
    for si, (r0, n) in enumerate(segs):
        def body(i, carry, r0=r0, n=n, si=si):
            fn(si, r0, n, [i])
            return carry

        lax.fori_loop(0, n // CONV_CH, body, 0, unroll=min(CONV_STEPS, n // CONV_CH))


def _conv_silu_body(x_ref, w_ref, b_ref, o_ref, *, segs):
    shifts = _shift_matrices()

    def fn(si, r0, n, steps):
        ys = [_silu(y) for y in _conv_at(x_ref, w_ref, b_ref, shifts, r0, n, steps)]
        for i, y in zip(steps, ys):
            o_ref[_chunk_rows(r0, i), :] = y.astype(o_ref.dtype)

    _conv_loop(segs, fn)


def _conv_rope_body(x_ref, w_ref, b_ref, cos_ref, sin_ref, o_ref, *, segs, tc):
    shifts = _shift_matrices()
    lane = lax.broadcasted_iota(jnp.int32, (CONV_CH, tc), 1)
    first = (lane % (ML_QK_DIM // 2)) < (ML_QK_DIM // 4)
    nf = ML_QK_DIM // 4

    def fn(si, r0, n, steps):
        ys = [_silu(y) for y in _conv_at(x_ref, w_ref, b_ref, shifts, r0, n, steps)]
        partner = [jnp.where(first, pltpu.roll(y, tc - nf, 1), pltpu.roll(y, nf, 1)) for y in ys]
        for i, y, p in zip(steps, ys, partner):
            rows = _chunk_rows(r0, i)
            o_ref[rows, :] = (y * cos_ref[rows, :] + p * sin_ref[rows, :]).astype(o_ref.dtype)

    _conv_loop(segs, fn)


def _conv_ffn_body(a_ref, x_ref, w_ref, b_ref, o_ref, *, segs):
    shifts = _shift_matrices()

    def fn(si, r0, n, steps):
        gs = [_silu(y) for y in _conv_at(x_ref, w_ref, b_ref, shifts, r0, n, steps)]
        for i, g in zip(steps, gs):
            rows = _chunk_rows(r0, i)
            o_ref[rows, :] = (a_ref[rows, :].astype(F32) * g).astype(o_ref.dtype)

    _conv_loop(segs, fn)


def _conv_hyena_body(x0_ref, x1_ref, v_ref, w0_ref, b0_ref, w1_ref, b1_ref, w2_ref, b2_ref,
                     ox0_ref, ozl_ref, ozc_ref, ozle_ref, ozlo_ref, ozce_ref, ozco_ref, *zs_refs, segs):
    shifts = _shift_matrices()
    half = CONV_CH // 2
    outs = ((ozl_ref, ozle_ref, ozlo_ref), (ozc_ref, ozce_ref, ozco_ref))

    def fn(si, r0, n, steps):
        x0 = _conv_at(x0_ref, w0_ref, b0_ref, shifts, r0, n, steps)
        x1 = _conv_at(x1_ref, w1_ref, b1_ref, shifts, r0, n, steps)
        v = _conv_at(v_ref, w2_ref, b2_ref, shifts, r0, n, steps)
        nat_ref, even_ref, odd_ref = outs[si]
        for i, x0_i, x1_i, v_i in zip(steps, x0, x1, v):
            ox0_ref[_chunk_rows(r0, i), :] = x0_i.astype(ox0_ref.dtype)
            zin = x1_i * v_i
            nat_ref[_chunk_rows(0, i), :] = zin.astype(BF16)
            t0 = r0 + pl.multiple_of(i * CONV_CH, CONV_CH)
            hrows = pl.ds(pl.multiple_of(i * half, half), half)
            for g, zs_ref in enumerate(zs_refs):
                lanes = slice(g * LANES, (g + 1) * LANES)
                zs_ref[pl.ds(t0, CONV_CH), :] = zin[:, lanes]
                even_ref[hrows, lanes] = zs_ref[pl.ds(t0, half, stride=2), :].astype(BF16)
                odd_ref[hrows, lanes] = zs_ref[pl.ds(t0 + 1, half, stride=2), :].astype(BF16)

    _conv_loop(segs, fn)


def _conv_specs(t_in, tc, xcol_blk, wcol_blk):
    return [pl.BlockSpec((None, t_in, tc), lambda b, j: (b, 0, j + xcol_blk)),
            pl.BlockSpec((CONV_W, tc), lambda b, j: (0, j + wcol_blk)),
            pl.BlockSpec((1, tc), lambda b, j: (0, j + wcol_blk))]


def _conv_silu(x, w, b, *, xcol, wcol, width, segs, tc=256):
    nb, t_in, _ = x.shape
    return pl.pallas_call(
        functools.partial(_conv_silu_body, segs=segs),
        out_shape=jax.ShapeDtypeStruct((nb, t_in, width), BF16), grid=(nb, width // tc),
        in_specs=_conv_specs(t_in, tc, xcol // tc, wcol // tc),
        out_specs=pl.BlockSpec((None, t_in, tc), lambda b, j: (b, 0, j)),
        name="conv_silu", compiler_params=_cp(("parallel", "parallel")))(x, w, b)


def _conv_rope(x, w, b, cos, sin, *, width, segs, tc=256):
    nb, t_in, _ = x.shape
    tab = pl.BlockSpec((t_in, tc), lambda b, j: (0, 0))
    return pl.pallas_call(
        functools.partial(_conv_rope_body, segs=segs, tc=tc),
        out_shape=jax.ShapeDtypeStruct((nb, t_in, width), BF16), grid=(nb, width // tc),
        in_specs=_conv_specs(t_in, tc, 0, 0) + [tab, tab],
        out_specs=pl.BlockSpec((None, t_in, tc), lambda b, j: (b, 0, j)),
        name="conv_rope", compiler_params=_cp(("parallel", "parallel")))(x, w, b, cos, sin)


def _conv_ffn(up, w, b, *, segs, rows, tc=256):
    nb, t_in, two_ff = up.shape
    ff = two_ff // 2
    return pl.pallas_call(
        functools.partial(_conv_ffn_body, segs=segs),
        out_shape=jax.ShapeDtypeStruct((nb, rows, ff), BF16), grid=(nb, ff // tc),
        in_specs=[pl.BlockSpec((None, rows, tc), lambda b, j: (b, 0, j))] + _conv_specs(rows, tc, ff // tc, 0),
        out_specs=pl.BlockSpec((None, rows, tc), lambda b, j: (b, 0, j)),
        name="conv_ffn", compiler_params=_cp(("parallel", "parallel")))(up, up, w, b)


def _conv_hyena(pr, w, b, *, xcol, wcol, segs, tc=256):
    nb, t_in, _ = pr.shape
    (_, n_lat), (_, n_ctx) = segs
    wd = HY_WIDTH
    out_rows = (t_in, n_lat, n_ctx, n_lat // 2, n_lat // 2, n_ctx // 2, n_ctx // 2)
    in_specs = [pl.BlockSpec((None, t_in, tc), functools.partial(lambda b, j, o: (b, 0, j + o), o=(xcol + m * wd) // tc))
                for m in range(3)]
    wargs = []
    for m in range(3):
        off = (wcol + m * wd) // tc
        in_specs.append(pl.BlockSpec((CONV_W, tc), functools.partial(lambda b, j, o: (0, j + o), o=off)))
        in_specs.append(pl.BlockSpec((1, tc), functools.partial(lambda b, j, o: (0, j + o), o=off)))
        wargs += [w, b]
    return pl.pallas_call(
        functools.partial(_conv_hyena_body, segs=segs),
        out_shape=tuple(jax.ShapeDtypeStruct((nb, rows, wd), BF16) for rows in out_rows),
        grid=(nb, wd // tc), in_specs=in_specs,
        out_specs=tuple(pl.BlockSpec((None, rows, tc), lambda b, j: (b, 0, j)) for rows in out_rows),
        scratch_shapes=[pltpu.VMEM((t_in, LANES), F32)] * (tc // LANES),
        name="conv_hyena", compiler_params=_cp(("parallel", "parallel")))(pr, pr, pr, *wargs)


def _scan_masks(reverse):
    q = SCAN_Q
    ii = lax.broadcasted_iota(jnp.int32, (q, q), 0)
    jj = lax.broadcasted_iota(jnp.int32, (q, q), 1)
    mask = (jj >= ii) if reverse else (jj <= ii)
    mask_t = (jj <= ii) if reverse else (jj >= ii)
    return mask, mask.astype(F32), mask_t.astype(F32)


def _chunk_map(reverse, n_chunks, n_lat_chunks):
    if reverse:
        return lambda k: n_chunks - 1 - k
    return lambda k: (k + n_lat_chunks) % n_chunks


SSD_GPS = 4


def _ssd_body(*refs, reverse, final):
    (xs_ref, b_ref, c_ref, dtc_ref, dtr_ref, dbc_ref, dbr_ref, alc_ref, alr_ref) = refs[:9]
    if final:
        yf_ref, z_ref, dsk_ref, nw_ref, o_ref, h_ref = refs[9:]
    else:
        o_ref, h_ref = refs[9:]

    @pl.when(pl.program_id(2) == 0)
    def _():
        h_ref[...] = jnp.zeros_like(h_ref)

    q, gw, hd = SCAN_Q, SSD_GW, SSD_HEAD_DIM
    mask, maskf, mask_tf = _scan_masks(reverse)
    groups = range(SSD_GPS)
    pairs = range(SSD_HPG // 2)
    cols = [slice(g * gw, (g + 1) * gw) for g in groups]
    xs = [xs_ref[:, cols[g]].astype(F32) for g in groups]
    bm = [b_ref[:, g * SSD_STATE:(g + 1) * SSD_STATE] for g in groups]
    cm = [c_ref[:, g * SSD_STATE:(g + 1) * SSD_STATE] for g in groups]
    h_prev = [h_ref[g] for g in groups]
    dt_c = [_softplus(dtc_ref[g] + dbc_ref[g]) for g in groups]
    dt_r = [_softplus(dtr_ref[g] + dbr_ref[g]) for g in groups]
    da_c = [dt_c[g] * (-jnp.exp(alc_ref[g])) for g in groups]
    da_r = [dt_r[g] * (-jnp.exp(alr_ref[g])) for g in groups]
    acum_c = [_sel_dot(maskf, da_c[g], 3) for g in groups]
    acum_r = [_dot_sel(da_r[g], mask_tf, 3) for g in groups]
    atot = [jnp.sum(da_c[g], axis=0, keepdims=True) for g in groups]

    er = lax.broadcasted_iota(jnp.int32, (SSD_HPG, gw), 0)
    ec = lax.broadcasted_iota(jnp.int32, (SSD_HPG, gw), 1)
    expand = (ec // hd == er).astype(F32)
    wide = [_dot_sel(jnp.concatenate([dt_c[g], jnp.exp(atot[g] - acum_c[g]), jnp.exp(acum_c[g]),
                                      jnp.broadcast_to(jnp.exp(atot[g]), (SUBLANES, SSD_HPG))], axis=0), expand, 1)
            for g in groups]
    xq = [xs[g] * wide[g][0:q] for g in groups]
    cb = [_dot_nt(cm[g], bm[g]) for g in groups]
    y_off = [wide[g][2 * q:3 * q] * _dot(cm[g], h_prev[g].astype(BF16)) for g in groups]
    lane = lax.broadcasted_iota(jnp.int32, (q, 2 * hd), 1)

    def pair_lhs(g, p):
        ms = [(cb[g] * jnp.where(mask, jnp.exp(acum_c[g][:, r:r + 1] - acum_r[g][r:r + 1, :]), 0.0)).astype(BF16)
              for r in (2 * p, 2 * p + 1)]
        return jnp.concatenate(ms, axis=1)

    def pair_rhs(g, p):
        xp = xq[g][:, 2 * hd * p:2 * hd * (p + 1)]
        return jnp.concatenate([jnp.where(lane < hd, xp, 0.0), jnp.where(lane >= hd, xp, 0.0)], axis=0).astype(BF16)

    y_in = [[_dot(pair_lhs(g, p), pair_rhs(g, p)) for p in pairs] for g in groups]
    y = [y_off[g] + jnp.concatenate(y_in[g], axis=1) for g in groups]
    bt = [bm[g].astype(F32).T.astype(BF16) for g in groups]
    upd = [_dot(bt[g], (xq[g] * wide[g][q:2 * q]).astype(BF16)) for g in groups]
    for g in groups:
        h_ref[g] = wide[g][3 * q:3 * q + 1] * h_prev[g] + upd[g]

    for g in groups:
        if final:
            ytot = y[g] + yf_ref[:, cols[g]] + xs[g] * dsk_ref[:, cols[g]]
            gated = ytot * _silu(z_ref[:, cols[g]].astype(F32))
            o = gated * lax.rsqrt(jnp.mean(gated * gated, axis=-1, keepdims=True) + EPS) * nw_ref[:, cols[g]]
            o_ref[:, cols[g]] = o.astype(o_ref.dtype)
        else:
            o_ref[:, cols[g]] = y[g]


def _ssd(xbc, dt_c, dt_r, db_c, db_r, al_c, al_r, *, direction, n_lat, final_args=None):
    nb, t, _ = xbc.shape
    q, gw, hpg = SCAN_Q, SSD_GW, SSD_HPG
    nch = t // q
    reverse = direction == 1
    cmap = _chunk_map(reverse, nch, n_lat // q)
    gps = SSD_GPS
    sw, gww = gps * SSD_STATE, gps * gw
    b0 = SSD_WIDTH // sw
    c0 = b0 + SSD_GROUPS // gps
    d = direction
    in_specs = [
        pl.BlockSpec((None, q, gww), lambda b, g, k: (b, cmap(k), g)),
        pl.BlockSpec((None, q, sw), lambda b, g, k: (b, cmap(k), b0 + g)),
        pl.BlockSpec((None, q, sw), lambda b, g, k: (b, cmap(k), c0 + g)),
        pl.BlockSpec((None, None, gps, q, hpg), lambda b, g, k: (b, d, g, cmap(k), 0)),
        pl.BlockSpec((None, None, gps, hpg, q), lambda b, g, k: (b, d, g, 0, cmap(k))),
        pl.BlockSpec((None, gps, 1, hpg), lambda b, g, k: (d, g, 0, 0)),
        pl.BlockSpec((None, gps, hpg, 1), lambda b, g, k: (d, g, 0, 0)),
        pl.BlockSpec((None, gps, 1, hpg), lambda b, g, k: (d, g, 0, 0)),
        pl.BlockSpec((None, gps, hpg, 1), lambda b, g, k: (d, g, 0, 0)),
    ]
    args = [xbc, xbc, xbc, dt_c, dt_r, db_c, db_r, al_c, al_r]
    final = final_args is not None
    if final:
        yf, pr, dskip, norm_w = final_args
        in_specs += [pl.BlockSpec((None, q, gww), lambda b, g, k: (b, cmap(k), g)),
                     pl.BlockSpec((None, q, gww), lambda b, g, k: (b, cmap(k), g)),
                     pl.BlockSpec((1, gww), lambda b, g, k: (0, g)),
                     pl.BlockSpec((1, gww), lambda b, g, k: (0, g))]
        args += [yf, pr, dskip, norm_w]
    out_shape = jax.ShapeDtypeStruct((nb, t, SSD_WIDTH), BF16 if final else F32)
    return pl.pallas_call(
        functools.partial(_ssd_body, reverse=reverse, final=final),
        out_shape=out_shape, grid=(nb, SSD_GROUPS // gps, nch), in_specs=in_specs,
        out_specs=pl.BlockSpec((None, q, gww), lambda b, g, k: (b, cmap(k), g)),
        scratch_shapes=[pltpu.VMEM((gps, SSD_STATE, gw), F32)],
        name="ssd_bwd" if reverse else "ssd_fwd",
        compiler_params=_cp(("parallel", "parallel", "arbitrary")))(*args)


def _hyfilt_body(f_ref, w1_ref, b1_ref, w2_ref, b2_ref, w3_ref, fr_ref, dl_ref, o_ref, *, tl):
    f = f_ref[...]
    h = jnp.sin(fr_ref[0:1, :] * (_dot(f, w1_ref[...], HI) + b1_ref[...]))
    h = jnp.sin(fr_ref[1:2, :] * (_dot(h, w2_ref[...], HI) + b2_ref[...]))
    h = _dot(h, w3_ref[...], HI)
    dec = jnp.exp(-f[:, 0:1] * dl_ref[...])
    row = pl.program_id(0) * tl + lax.broadcasted_iota(jnp.int32, (tl, 1), 0)
    wd = HY_WIDTH
    o_ref[:, 0:wd] = (h[:, 0:wd] * dec).astype(o_ref.dtype)
    o_ref[:, wd:2 * wd] = jnp.where(row == 0, 0.0, h[:, wd:2 * wd] * dec).astype(o_ref.dtype)


def _hyena_filters(length, w1, b1, w2, b2, w3, freq, *, tl=256):
    p = LANES
    t = jnp.linspace(0.0, 1.0, length, dtype=F32)[:, None]
    w = 2.0 * math.pi * jnp.arange(length, dtype=F32)[:, None] / length
    f = jnp.linspace(1e-4, HY_BANDS - 1, HY_BANDS, dtype=F32)[None, :]
    feats = jnp.concatenate([t, jnp.cos(f * w), -jnp.sin(f * w)], axis=-1)
    feats = jnp.concatenate([feats[0::2], feats[1::2]], axis=0)
    feats = jnp.pad(feats, ((0, 0), (0, p - HY_EMB)))
    po = p - HY_ORDER
    w1p = jnp.pad(w1, ((0, p - HY_EMB), (0, po)))
    w2p = jnp.pad(w2, ((0, po), (0, po)))
    w3p = jnp.pad(w3, ((0, po), (0, 0)))
    b1p = jnp.pad(b1[None, :], ((0, 0), (0, po)))
    b2p = jnp.pad(b2[None, :], ((0, 0), (0, po)))
    frp = jnp.pad(freq, ((0, 0), (0, po)))
    deltas = jnp.abs(jnp.linspace(math.log(HY_TARGET) / HY_FAST, math.log(HY_TARGET) / HY_SLOW, HY_WIDTH,
                                  dtype=F32))[None, :]
    full = lambda shape: pl.BlockSpec(shape, lambda i: (0,) * len(shape))
    out = pl.pallas_call(
        functools.partial(_hyfilt_body, tl=tl),
        out_shape=jax.ShapeDtypeStruct((length, 2 * HY_WIDTH), BF16), grid=(length // tl,),
        in_specs=[pl.BlockSpec((tl, p), lambda i: (i, 0)), full((p, p)), full((1, p)), full((p, p)), full((1, p)),
                  full((p, 2 * HY_WIDTH)), full((2, p)), full((1, HY_WIDTH))],
        out_specs=pl.BlockSpec((tl, 2 * HY_WIDTH), lambda i: (i, 0)),
        name="hyena_filters", compiler_params=_cp(("parallel",)))(feats, w1p, b1p, w2p, b2p, w3p, frp, deltas)
    return out[None]


def _dft_matrices(length):
    n = 2 * length
    k = jnp.arange(length, dtype=jnp.int32)[:, None]
    t = jnp.arange(length, dtype=jnp.int32)[None, :]
    step = 1 << (int(math.log2(length)) // 2)
    part = jnp.arange(length // step, dtype=jnp.int32)[:, None]
    ang_hi = ((part * step * t) % n).astype(F32) * (2.0 * math.pi / n)
    ang_lo = ((jnp.arange(step, dtype=jnp.int32)[:, None] * t) % n).astype(F32) * (2.0 * math.pi / n)
    ch, sh = jnp.cos(ang_hi)[:, None, :], jnp.sin(ang_hi)[:, None, :]
    cl, sl = jnp.cos(ang_lo)[None, :, :], jnp.sin(ang_lo)[None, :, :]
    c = (ch * cl - sh * sl).reshape(length, length)
    s = (sh * cl + ch * sl).reshape(length, length)
    alt_t = jnp.where(t % 2 == 0, 1.0, -1.0).astype(F32)
    fwd = jnp.concatenate([c, jnp.where(k == 0, alt_t, -s)], axis=0)
    alt_r = jnp.where(k % 2 == 0, 1.0, -1.0).astype(F32)
    inv_re = jnp.where(t == 0, 1.0 / n, (2.0 / n) * c)
    inv_im = jnp.where(t == 0, alt_r / n, (-2.0 / n) * s)
    inv = jnp.concatenate([inv_re, inv_im], axis=1)
    return fwd.astype(BF16)[None], inv.astype(BF16)[None]


def _tile_interleave(x, axis, hm):
    shp = x.shape
    n = shp[axis] // 2
    x = x.reshape(shp[:axis] + (2, n // hm, hm) + shp[axis + 1:])
    return jnp.swapaxes(x, axis, axis + 1).reshape(shp)


def _twiddle_table(length, hm):
    k = np.arange(length // 2, dtype=np.float64)[:, None] * (np.pi / length)
    tab = np.concatenate([np.cos(k), np.sin(k)], axis=0)
    tab = tab.reshape(2, -1, hm, 1).swapaxes(0, 1).reshape(length, 1)
    return jnp.asarray(np.broadcast_to(tab, (length, LANES)).astype(np.float32))


def _lanes(x, width):
    return jnp.tile(x, (1, width // LANES))


def _parity_banks(e, ot, tw, hm):
    er, ei, otr, oti = e[0:hm], e[hm:], ot[0:hm], ot[hm:]
    c, s = _lanes(tw[0:hm], e.shape[1]), _lanes(tw[hm:], e.shape[1])
    orr = otr * c + oti * s
    oi = oti * c - otr * s
    return (er + orr, ei + oi, er - orr, ei - oi), (c, s)


def _is_bin0(row_blk, hm):
    return (row_blk == 0) & (lax.broadcasted_iota(jnp.int32, (hm, 1), 0) == 0)


def _hy_kbank_body(ef_ref, eb_ref, of_ref, ob_ref, tw_ref, ka_ref, kb_ref, *, hm):
    tw = tw_ref[...]
    (afr, afi, bfr, bfi), _ = _parity_banks(ef_ref[...], of_ref[...], tw, hm)
    (abr, abi, bbr, bbi), _ = _parity_banks(eb_ref[...], ob_ref[...], tw, hm)
    first = _is_bin0(pl.program_id(0), hm)
    kai = jnp.where(first, ef_ref[hm:, :] + eb_ref[hm:, :], afi - abi)
    kbi = jnp.where(first, ob_ref[hm:, :] - of_ref[hm:, :], bfi - bbi)
    ka_ref[...] = jnp.concatenate([afr + abr, kai], axis=0)
    kb_ref[...] = jnp.concatenate([bfr + bbr, kbi], axis=0)


def _hy_kbank(ef, of, tw, *, tm, tc=512):
    _, ln, c2 = ef.shape
    c = c2 // 2
    fspec = pl.BlockSpec((None, tm, tc), lambda i, j: (0, i, j))
    bspec = pl.BlockSpec((None, tm, tc), lambda i, j: (0, i, j + c // tc))
    return pl.pallas_call(
        functools.partial(_hy_kbank_body, hm=tm // 2),
        out_shape=(jax.ShapeDtypeStruct((1, ln, c), F32),) * 2, grid=(ln // tm, c // tc),
        in_specs=[fspec, bspec, fspec, bspec, pl.BlockSpec((tm, LANES), lambda i, j: (i, 0))],
        out_specs=(fspec, fspec), name="hy_kbank",
        compiler_params=_cp(("parallel", "parallel")))(ef, ef, of, of, tw)


def _hy_spectrum_epilogue(acc, row_blk, e_ref, tw_ref, ka_ref, kb_ref, *, hm):
    e = e_ref[...]
    (ar, ai, br, bi), (c, s) = _parity_banks(e, acc, tw_ref[...], hm)
    kar, kai, kbr, kbi = ka_ref[0:hm], ka_ref[hm:], kb_ref[0:hm], kb_ref[hm:]
    yar, yai = ar * kar - ai * kai, ar * kai + ai * kar
    ybr, ybi = br * kbr - bi * kbi, br * kbi + bi * kbr
    qr, qi = yar - ybr, yai - ybi
    first = _is_bin0(row_blk, hm)
    y0, yl = ar * kar, br * kbr
    amr, ami = e[hm:], -acc[hm:]
    ymr, ymi = amr * kai - ami * kbi, amr * kbi + ami * kai
    pr = jnp.where(first, y0 + yl, yar + ybr)
    pi = jnp.where(first, 2.0 * ymr, yai + ybi)
    qpr = jnp.where(first, y0 - yl, qr * c - qi * s)
    qpi = jnp.where(first, -2.0 * ymi, qi * c + qr * s)
    return jnp.concatenate([pr, pi], axis=0), jnp.concatenate([qpr, qpi], axis=0)


def _hyena_out_body(*refs, segs):
    ns = len(segs)
    x0_ref, db_ref, o_ref = refs[3 * ns:3 * ns + 3]
    ys_refs = refs[3 * ns + 3:]
    half = CONV_CH // 2
    for si, (r0, n) in enumerate(segs):
        ye_ref, yo_ref, zin_ref = refs[3 * si:3 * si + 3]

        def body(i, carry, r0=r0, ye_ref=ye_ref, yo_ref=yo_ref, zin_ref=zin_ref):
            t0 = pl.multiple_of(i * CONV_CH, CONV_CH)
            hrows = pl.ds(pl.multiple_of(i * half, half), half)
            for g, ys_ref in enumerate(ys_refs):
                lanes = slice(g * LANES, (g + 1) * LANES)
                ys_ref[pl.ds(r0 + t0, half, stride=2), :] = ye_ref[hrows, lanes].astype(F32)
                ys_ref[pl.ds(r0 + t0 + 1, half, stride=2), :] = yo_ref[hrows, lanes].astype(F32)
            rows = pl.ds(r0 + t0, CONV_CH)
            y = jnp.concatenate([ys_ref[rows, :] for ys_ref in ys_refs], axis=1)
            zin = zin_ref[pl.ds(t0, CONV_CH), :].astype(F32)
            o_ref[rows, :] = (x0_ref[rows, :].astype(F32) * (y + zin * db_ref[...])).astype(o_ref.dtype)
            return carry

        lax.fori_loop(0, n // CONV_CH, body, 0, unroll=min(4, n // CONV_CH))


def _hyena_out(parts, x0, dbias, *, segs, tc=256):
    nb, t, c = x0.shape
    in_specs, args = [], []
    for (ye, yo, zin), (_, n) in zip(parts, segs):
        in_specs += [pl.BlockSpec((None, n // 2, tc), lambda b, j: (b, 0, j)),
                     pl.BlockSpec((None, n // 2, tc), lambda b, j: (b, 0, j)),
                     pl.BlockSpec((None, n, tc), lambda b, j: (b, 0, j))]
        args += [ye, yo, zin]
    in_specs += [pl.BlockSpec((None, t, tc), lambda b, j: (b, 0, j)), pl.BlockSpec((1, tc), lambda b, j: (0, j))]
    return pl.pallas_call(
        functools.partial(_hyena_out_body, segs=segs),
        out_shape=jax.ShapeDtypeStruct((nb, t, c), BF16), grid=(nb, c // tc), in_specs=in_specs,
        out_specs=pl.BlockSpec((None, t, tc), lambda b, j: (b, 0, j)),
        scratch_shapes=[pltpu.VMEM((t, LANES), F32)] * (tc // LANES), name="hyena_out",
        compiler_params=_cp(("parallel", "parallel")))(*args, x0, dbias)


def _hyena_longconv(ze, zo, hfb, *, tn=1024):
    nb, h, c = ze.shape
    ln = 2 * h
    tm_s = min(512, ln)
    hm = tm_s // 2
    fwd, inv = _dft_matrices(h)
    fwd = _tile_interleave(fwd, 1, hm)
    inv = _tile_interleave(inv * 0.5, 2, hm)
    tw = _twiddle_table(ln, hm)
    tm_f, tk_f = min(1024, ln), min(2048, h)
    ef = _matmul(fwd, hfb[:, :h], tm=tm_f, tn=tn, tk=tk_f, out_dtype=F32, name="hy_filter_dft_even")
    of = _matmul(fwd, hfb[:, h:], tm=tm_f, tn=tn, tk=tk_f, out_dtype=F32, name="hy_filter_dft_odd")
    ka, kb = _hy_kbank(ef, of, tw, tm=tm_s)
    e = _matmul(fwd, ze, tm=tm_f, tn=tn, tk=tk_f, out_dtype=F32, name="hy_dft_even")
    tile = pl.BlockSpec((None, tm_s, tn), lambda j, b, i, k: (b, i, j))
    ktile = pl.BlockSpec((None, tm_s, tn), lambda j, b, i, k: (0, i, j))
    extra = [(e, tile), (tw, pl.BlockSpec((tm_s, LANES), lambda j, b, i, k: (i, 0))), (ka, ktile), (kb, ktile)]
    p, q = _matmul(fwd, zo, tm=tm_s, tn=tn, tk=tk_f, out_dtype=BF16, extra=extra, n_out=2, batch_inner=True,
                   epilogue=functools.partial(_hy_spectrum_epilogue, hm=hm), name="hy_dft_odd")
    tm_i, tk_i = min(1024, h), min(2048, ln)
    ye = _matmul(inv, p, tm=tm_i, tn=tn, tk=tk_i, out_dtype=BF16, name="hy_idft_even")
    yo = _matmul(inv, q, tm=tm_i, tn=tn, tk=tk_i, out_dtype=BF16, name="hy_idft_odd")
    return ye, yo


def _mlstm_body(*refs, reverse, final):
    q_ref, k_ref, v_ref, gc_ref, gr_ref = refs[:5]
    if final:
        hf_ref, og_ref, nw_ref, o_ref, ct_ref, n_ref, m_ref = refs[5:]
    else:
        o_ref, ct_ref, n_ref, m_ref = refs[5:]
    nh, dk, dv = ML_HEADS, ML_QK_DIM, ML_V_DIM
    scale = dk ** -0.5

    @pl.when(pl.program_id(1) == 0)
    def _():
        ct_ref[...] = jnp.zeros_like(ct_ref)
        n_ref[...] = jnp.zeros_like(n_ref)
        m_ref[...] = jnp.zeros_like(m_ref)

    gc, gr = gc_ref[...], gr_ref[...]
    li_c, lf_c = gc[:, 0:nh], _log_sigmoid(gc[:, nh:2 * nh])
    li_r, lf_r = gr[0:nh, :], _log_sigmoid(gr[nh:2 * nh, :])
    mask, maskf, mask_tf = _scan_masks(reverse)
    bc_c = _dot(maskf, lf_c, HI)
    bc_r = _dot(lf_r, mask_tf, HI)
    btot_c = jnp.sum(lf_c, axis=0, keepdims=True)
    n_all, m_all = n_ref[...], m_ref[...]
    heads = range(nh)
    qs = [q_ref[:, dk * h:dk * (h + 1)] for h in heads]
    ks = [k_ref[:, dk * h:dk * (h + 1)] for h in heads]
    vs = [v_ref[:, dv * h:dv * (h + 1)] for h in heads]
    cts = [ct_ref[h] for h in heads]
    m_s = [m_all[h:h + 1, 0:1] for h in heads]
    n_s = [n_all[h:h + 1, :] for h in heads]
    qk = [_dot_nt(qs[h], ks[h]) for h in heads]
    qc = [_dot(qs[h], cts[h].astype(BF16)) for h in heads]
    dmat = [jnp.where(mask, bc_c[:, h:h + 1] - bc_r[h:h + 1, :] + li_r[h:h + 1, :], -jnp.inf) for h in heads]
    inter = [bc_c[:, h:h + 1] + m_s[h] for h in heads]
    m_t = [jnp.maximum(inter[h], jnp.max(dmat[h], axis=1, keepdims=True)) for h in heads]
    s = [qk[h] * scale * jnp.exp(dmat[h] - m_t[h]) for h in heads]
    dec = [jnp.exp(inter[h] - m_t[h]) for h in heads]
    num = [_dot(s[h].astype(BF16), vs[h]) + (dec[h] * scale) * qc[h] for h in heads]
    qn = [jnp.sum(qs[h].astype(F32) * n_s[h], axis=1, keepdims=True) * scale for h in heads]
    den = [jnp.sum(s[h], axis=1, keepdims=True) + dec[h] * qn[h] for h in heads]
    hh = [num[h] / jnp.maximum(jnp.abs(den[h]), jnp.exp(-m_t[h])) for h in heads]

    wst_c = [btot_c[:, h:h + 1] - bc_c[:, h:h + 1] + li_c[:, h:h + 1] for h in heads]
    wst_r = [btot_c[:, h:h + 1] - bc_r[h:h + 1, :] + li_r[h:h + 1, :] for h in heads]
    m_new = [jnp.maximum(btot_c[:, h:h + 1] + m_s[h], jnp.max(wst_r[h], axis=1, keepdims=True)) for h in heads]
    dstate = [jnp.exp(btot_c[:, h:h + 1] + m_s[h] - m_new[h]) for h in heads]
    kw = [ks[h].astype(F32) * jnp.exp(wst_c[h] - m_new[h]) for h in heads]
    upd = [_dot(kw[h].T.astype(BF16), vs[h]) for h in heads]
    for h in heads:
        ct_ref[h] = dstate[h] * cts[h] + upd[h]
    n_rows = [dstate[h] * n_s[h] + jnp.sum(kw[h], axis=0, keepdims=True) for h in heads]
    m_rows = [jnp.broadcast_to(m_new[h], (1, LANES)) for h in heads]

    for h in heads:
        cols = slice(dv * h, dv * (h + 1))
        if final:
            hs = hh[h] + hf_ref[:, cols]
            hn = hs * lax.rsqrt(jnp.mean(hs * hs, axis=-1, keepdims=True) + EPS) * nw_ref[:, cols]
            o_ref[:, cols] = (hn * _sigmoid(og_ref[:, cols].astype(F32))).astype(o_ref.dtype)
        else:
            o_ref[:, cols] = hh[h]

    n_ref[...] = jnp.concatenate(n_rows, axis=0)
    m_ref[...] = jnp.concatenate(m_rows, axis=0)


def _mlstm(qk, pr, g_c, g_r, *, direction, n_lat, final_args=None):
    nb, t, _ = qk.shape
    q = SCAN_Q
    nch = t // q
    reverse = direction == 1
    cmap = _chunk_map(reverse, nch, n_lat // q)
    d = direction
    in_specs = [
        pl.BlockSpec((None, q, ML_QK_WIDTH), lambda b, k: (b, cmap(k), 0)),
        pl.BlockSpec((None, q, ML_QK_WIDTH), lambda b, k: (b, cmap(k), 1)),
        pl.BlockSpec((None, q, ML_WIDTH), lambda b, k: (b, cmap(k), 2 * ML_QK_WIDTH // ML_WIDTH)),
        pl.BlockSpec((None, None, q, 2 * ML_HEADS), lambda b, k: (b, d, cmap(k), 0)),
        pl.BlockSpec((None, None, 2 * ML_HEADS, q), lambda b, k: (b, d, 0, cmap(k))),
    ]
    args = [qk, qk, pr, g_c, g_r]
    final = final_args is not None
    if final:
        hf, norm_w = final_args
        in_specs += [pl.BlockSpec((None, q, ML_WIDTH), lambda b, k: (b, cmap(k), 0)),
                     pl.BlockSpec((None, q, ML_WIDTH), lambda b, k: (b, cmap(k), 2 * ML_QK_WIDTH // ML_WIDTH + 1)),
                     pl.BlockSpec((1, ML_WIDTH), lambda b, k: (0, 0))]
        args += [hf, pr, norm_w]
    out_shape = jax.ShapeDtypeStruct((nb, t, ML_WIDTH), BF16 if final else F32)
    return pl.pallas_call(
        functools.partial(_mlstm_body, reverse=reverse, final=final),
        out_shape=out_shape, grid=(nb, nch), in_specs=in_specs,
        out_specs=pl.BlockSpec((None, q, ML_WIDTH), lambda b, k: (b, cmap(k), 0)),
        scratch_shapes=[pltpu.VMEM((ML_HEADS, ML_QK_DIM, ML_V_DIM), F32), pltpu.VMEM((ML_HEADS, ML_QK_DIM), F32),
                        pltpu.VMEM((ML_HEADS, LANES), F32)],
        name="mlstm_bwd" if reverse else "mlstm_fwd",
        compiler_params=_cp(("parallel", "arbitrary")))(*args)


def _na_window_start(j, rows):
    return jnp.clip(j * NA_QROWS - NA_ROWS // 2, 0, rows - NA_KROWS)


def _na_part_offset(j, part, start, clip=jnp.clip):
    first = j * NA_QROWS + part * (NA_QROWS // NA_PARTS) - NA_ROWS // 2
    return clip(first - start, 0, NA_KROWS - NA_PART_KROWS)


def _na_body(q_ref, k_ref, v_ref, tab_ref, qw_ref, kw_ref, o_ref, kn_ref, v1_ref, *, n_lat, n_ctx):
    dh = NA_HEAD_DIM
    scale = dh ** -0.5
    rows = n_lat // GRID_W
    j = pl.program_id(2)
    nq, nk = NA_QROWS * GRID_W, NA_KROWS * GRID_W

    @pl.when(j == 0)
    def _():
        one_hot = (lax.broadcasted_iota(jnp.int32, (LANES, LANES), 1) == 0).astype(v1_ref.dtype)

        def body(i, carry):
            r = pl.ds(pl.multiple_of(i * LANES, LANES), LANES)
            kk = k_ref[r, :].astype(F32)
            kn = kk * lax.rsqrt(jnp.mean(kk * kk, axis=-1, keepdims=True) + EPS) * kw_ref[...]
            kn_ref[r, :] = kn.astype(kn_ref.dtype)
            v1_ref[r, 0:dh] = v_ref[r, :]
            v1_ref[r, dh:dh + LANES] = one_hot
            return carry

        lax.fori_loop(0, (n_lat + n_ctx) // LANES, body, 0)

    start = _na_window_start(j, rows)
    ctx = pl.ds(n_lat, n_ctx)
    parts = range(NA_PARTS)
    rpp = NA_QROWS // NA_PARTS
    pq = rpp * GRID_W
    off = [_na_part_offset(j, i, start) for i in parts]
    win = [pl.ds(pl.multiple_of((start + off[i]) * GRID_W, GRID_W), NA_PART_KROWS * GRID_W) for i in parts]
    qq = [q_ref[i * pq:(i + 1) * pq, :].astype(F32) for i in parts]
    qn = [(qq[i] * lax.rsqrt(jnp.mean(qq[i] * qq[i], axis=-1, keepdims=True) + EPS)
           * (qw_ref[...] * scale)).astype(BF16) for i in parts]
    sw = [_dot_nt(qn[i], kn_ref[win[i], :]) for i in parts]
    sc = [_dot_nt(qn[i], kn_ref[ctx, :]) for i in parts]

    def with_bias(i):
        out = []
        for r in range(rpp):
            d0 = start - (j * NA_QROWS + i * rpp + r) + NA_ROWS - 1 + NA_QROWS
            blocks = tab_ref[d0, pl.ds(off[i] // NA_TAB_ROWS, NA_PART_KROWS // NA_TAB_ROWS)]
            bias = jnp.concatenate([blocks[b] for b in range(NA_PART_KROWS // NA_TAB_ROWS)], axis=1)
            out.append(sw[i][r * GRID_W:(r + 1) * GRID_W] + bias)
        return jnp.concatenate(out, axis=0)

    sb = [with_bias(i) for i in parts]
    m = [jnp.maximum(jnp.max(sb[i], axis=1, keepdims=True), jnp.max(sc[i], axis=1, keepdims=True)) for i in parts]
    pw = [jnp.exp(sb[i] - m[i]) for i in parts]
    pc = [jnp.exp(sc[i] - m[i]) for i in parts]
    o = [_dot(pw[i].astype(BF16), v1_ref[win[i], :]) + _dot(pc[i].astype(BF16), v1_ref[ctx, :]) for i in parts]
    for i in parts:
        o_ref[i * pq:(i + 1) * pq, :] = (o[i][:, 0:dh] / o[i][:, dh:dh + 1]).astype(o_ref.dtype)


def _na_bias_table(rpb, rows):
    nd = 2 * NA_QROWS
    valid = np.zeros((nd, NA_KROWS), bool)
    seen = np.zeros((nd,), bool)
    for j in range(rows // NA_QROWS):
        start = int(np.clip(j * NA_QROWS - NA_ROWS // 2, 0, rows - NA_KROWS))
        for qr in range(NA_QROWS):
            r = j * NA_QROWS + qr
            rs = int(np.clip(r - NA_ROWS // 2, 0, rows - NA_ROWS))
            d = start - r + NA_ROWS - 1 + NA_QROWS
            v = np.array([(rs <= start + i < rs + NA_ROWS) for i in range(NA_KROWS)])
            assert 0 <= d < nd and (not seen[d] or (valid[d] == v).all())
            valid[d], seen[d] = v, True
            off = int(_na_part_offset(j, qr // (NA_QROWS // NA_PARTS), start, clip=np.clip))
            assert off % NA_TAB_ROWS == 0 and not v[:off].any() and not v[off + NA_PART_KROWS:].any()
    col = np.arange(GRID_W)
    cs = np.clip(col - NA_COLS // 2, 0, GRID_W - NA_COLS)
    in_win = (col[None, :] >= cs[:, None]) & (col[None, :] < cs[:, None] + NA_COLS)
    ndr, ndc, w = 2 * NA_ROWS - 1, 2 * NA_COLS - 1, GRID_W
    dr = (np.arange(nd)[:, None] - NA_QROWS) + np.arange(NA_KROWS)[None, :]
    ok = valid[:, None, :, None] & in_win[None, :, None, :] & (dr >= 0)[:, None, :, None] \
        & (dr < ndr)[:, None, :, None]
    ext = jnp.pad(rpb, ((0, 0), (0, 0), (w, w)), mode="edge")
    c0 = w + NA_COLS - 1
    t1 = jnp.stack([ext[:, :, c0 - qc:c0 - qc + w] for qc in range(w)], axis=2)
    t1p = jnp.pad(t1, ((0, 0), (NA_QROWS, nd + NA_KROWS - 1 - NA_QROWS - ndr), (0, 0), (0, 0)))
    wide = jnp.concatenate([t1p[:, m] for m in range(nd + NA_KROWS - 1)], axis=-1)
    wpad = LANES * pl.cdiv(wide.shape[-1], LANES)
    wide = jnp.pad(wide, ((0, 0), (0, 0), (0, wpad - wide.shape[-1])))
    nblk, bw = NA_KROWS // NA_TAB_ROWS, NA_TAB_ROWS * w
    okf = jnp.asarray(ok.reshape(nd, w, nblk, bw).transpose(0, 2, 1, 3).astype(np.float32))

    def body(wide_ref, ok_ref, o_ref):
        for d in range(nd):
            for b in range(nblk):
                lo = d * w + b * bw
                o_ref[d, b] = jnp.where(ok_ref[d, b] != 0.0, wide_ref[:, lo:lo + bw], -jnp.inf)

    return pl.pallas_call(
        body, out_shape=jax.ShapeDtypeStruct((rpb.shape[0], nd, nblk, w, bw), F32), grid=(rpb.shape[0],),
        in_specs=[pl.BlockSpec((None, w, wpad), lambda h: (h, 0, 0)),
                  pl.BlockSpec((nd, nblk, w, bw), lambda h: (0, 0, 0, 0))],
        out_specs=pl.BlockSpec((None, nd, nblk, w, bw), lambda h: (h, 0, 0, 0, 0)),
        name="na_bias_table", compiler_params=_cp(("parallel",)))(wide, okf)


def _na(pr, tab, q_w, k_w, *, n_lat, n_ctx, qcol):
    nb, t, _ = pr.shape
    dh, nh = NA_HEAD_DIM, NA_HEADS
    nq = NA_QROWS * GRID_W
    qb = qcol // dh
    nd = tab.shape[1]
    return pl.pallas_call(
        functools.partial(_na_body, n_lat=n_lat, n_ctx=n_ctx),
        out_shape=jax.ShapeDtypeStruct((nb, n_lat, NA_WIDTH), BF16), grid=(nb, nh, n_lat // nq),
        in_specs=[pl.BlockSpec((None, nq, dh), lambda b, h, j: (b, j, qb + h)),
                  pl.BlockSpec((None, t, dh), lambda b, h, j: (b, 0, qb + nh + h)),
                  pl.BlockSpec((None, t, dh), lambda b, h, j: (b, 0, qb + 2 * nh + h)),
                  pl.BlockSpec((None,) + tab.shape[1:], lambda b, h, j: (h, 0, 0, 0, 0)),
                  pl.BlockSpec((1, dh), lambda b, h, j: (0, 0)),
                  pl.BlockSpec((1, dh), lambda b, h, j: (0, 0))],
        out_specs=pl.BlockSpec((None, nq, dh), lambda b, h, j: (b, j, h)),
        scratch_shapes=[pltpu.VMEM((t, dh), BF16), pltpu.VMEM((t, dh + LANES), BF16)], name="na",
        compiler_params=_cp(("parallel", "parallel", "arbitrary")))(pr, pr, pr, tab, q_w, k_w)


def _rope_tables(n_lat, n_ctx, width):
    dh = ML_QK_DIM
    nf = dh // 4
    t = jnp.arange(n_lat)
    inv = ROPE_THETA ** (-jnp.arange(nf, dtype=F32) / nf)
    lane = np.arange(dh)
    pos = jnp.where((lane < dh // 2)[None, :], (t // GRID_W)[:, None], (t % GRID_W)[:, None]).astype(F32)
    ang = pos * inv[lane % nf][None, :]
    sign = np.where((lane % (dh // 2)) < nf, -1.0, 1.0).astype(np.float32)
    cos = jnp.concatenate([jnp.cos(ang), jnp.ones((n_ctx, dh), F32)], axis=0)
    sin = jnp.concatenate([jnp.sin(ang) * sign[None, :], jnp.zeros((n_ctx, dh), F32)], axis=0)
    reps = width // dh
    return jnp.tile(cos, (1, reps)), jnp.tile(sin, (1, reps))


def _ada_mod(c, c_ctx, w_all, b_all, layer):
    nb, d = c.shape
    rows = 16
    cv = jnp.concatenate([c, c_ctx[None, :], jnp.zeros((rows - nb - 1, d), F32)], axis=0)
    a = (cv * _sigmoid(cv)).astype(BF16)[None]
    out = _matmul(a, w_all, w_index=layer, tm=rows, tn=1024, tk=d, out_dtype=F32, name="ada_mod")[0]
    out = out[:nb + 1] + b_all[layer][None, :]
    return out.reshape(nb + 1, 6, d).transpose(1, 0, 2)


def _lat_ctx(m, nb):
    ctx = jnp.broadcast_to(m[nb][None, :], (nb, m.shape[1]))
    return jnp.stack([m[:nb], ctx], axis=1)[:, :, None, :]


def _pad_cols(w, n):
    return jnp.pad(w, ((0, 0), (0, n - w.shape[1])))


def _col_row_forms(g, groups):
    nb, t, _ = g.shape
    g5 = g.reshape(nb, t, 2, groups, -1)
    return g5.transpose(0, 2, 3, 1, 4), g5.transpose(0, 2, 3, 4, 1)


ROW_TILES = (1088, 1024, 512, 256)


def _row_tile(rows):
    return next(tm for tm in ROW_TILES if rows % tm == 0)


def _mod_pair(mods, shift, scale, nb):
    return jnp.stack([_lat_ctx(mods[shift], nb), _lat_ctx(mods[scale], nb)], axis=1)


def _in_projection(u, w_all, layer, *, rows, name, cols=None):
    tm = _row_tile(rows)
    tm = 2 * tm if rows % (2 * tm) == 0 else tm
    return _matmul(u, w_all, w_index=layer, cols=cols, tm=tm, tn=1024, tk=u.shape[2], out_dtype=BF16,
                   rows=rows, vmem=VMEM_LIMIT_BIG, name=name)


def _out_projection(a_list, w_all, x, gate, layer, *, rows, n_lat, next_norm, name):
    tm = _row_tile(rows)
    tn = 512 if w_all.shape[1] <= 2 * D_MODEL else 256
    x = _matmul_gated_residual_rows(a_list, w_all, x, gate, w_index=layer, tm=tm, tn=tn, rows=rows, n_lat=n_lat,
                                    name=name)
    if next_norm is None:
        return x, None
    return x, _rmsmod(x, *next_norm, tm=tm, rows=rows, n_lat=n_lat)


def _conv_ffn_block(x, u, mods, w_up_all, conv_w, conv_b, w_down_all, layer, *, rows, n_lat, segs, next_norm):
    nb = x.shape[0]
    up = _in_projection(u, w_up_all, layer, rows=rows, name="ffn_up")
    hid = _conv_ffn(up, conv_w, conv_b[None, :], segs=segs, rows=rows)
    return _out_projection((hid,), w_down_all, x, _lat_ctx(mods[5], nb), layer, rows=rows, n_lat=n_lat,
                           next_norm=next_norm, name="ffn_down")


def _even_layer(x, u, mods, w_in_all, conv_w, conv_b, dt_bias, a_log, d_skip, ssd_norm_w,
                hy_w1, hy_b1, hy_w2, hy_b2, hy_w3, hy_freq, hy_bias, w_out_all, j, *, n_lat, n_ctx, next_norm):
    nb, t, d = x.shape
    segs = ((0, n_lat), (n_lat, n_ctx))
    tm = _row_tile(t)
    n_main = SSD_WIDTH + EV_CONV_CH
    pr = _in_projection(u, w_in_all, j, rows=t, cols=n_main, name="ev_in")
    w_dt = _pad_cols(w_in_all[j, :, n_main:], LANES).astype(BF16)
    dtp = _matmul(u, w_dt[None], tm=tm, tn=LANES, tk=d, out_dtype=F32, name="ev_in_dt")[:, :, :2 * SSD_HEADS]
    cb = conv_b[None, :]
    xbc = _conv_silu(pr, conv_w, cb, xcol=SSD_WIDTH, wcol=0, width=SSD_XBC, segs=segs)
    x0, zin_l, zin_c, zle, zlo, zce, zco = _conv_hyena(pr, conv_w, cb, xcol=SSD_WIDTH + SSD_XBC, wcol=SSD_XBC,
                                                       segs=segs)

    dt_c, dt_r = _col_row_forms(dtp, SSD_GROUPS)
    db = dt_bias.reshape(2, SSD_GROUPS, 1, SSD_HPG)
    al = a_log.reshape(2, SSD_GROUPS, 1, SSD_HPG)
    scan_args = (xbc, dt_c, dt_r, db, db.transpose(0, 1, 3, 2), al, al.transpose(0, 1, 3, 2))
    yf = _ssd(*scan_args, direction=0, n_lat=n_lat)
    dsk = jnp.repeat(d_skip, SSD_HEAD_DIM)[None, :]
    y_ssd = _ssd(*scan_args, direction=1, n_lat=n_lat, final_args=(yf, pr, dsk, ssd_norm_w[None, :]))

    hyb = hy_bias[None, :]
    filt = functools.partial(_hyena_filters, w1=hy_w1, b1=hy_b1, w2=hy_w2, b2=hy_b2, w3=hy_w3, freq=hy_freq)
    lat = _hyena_longconv(zle, zlo, filt(n_lat)) + (zin_l,)
    ctx = _hyena_longconv(zce, zco, filt(n_ctx)) + (zin_c,)
    y_hy = _hyena_out((lat, ctx), x0, hyb, segs=segs)
    return _out_projection((y_ssd, y_hy), w_out_all, x, _lat_ctx(mods[2], nb), j, rows=t, n_lat=n_lat,
                           next_norm=next_norm, name="ev_out")


def _odd_layer_latent(x, u, mods, w_in, conv_w, conv_b, gate_b, ml_norm_w, q_norm_w, k_norm_w, rpb,
                      w_out_all, j, *, n_lat, n_ctx, next_norm):
    nb, t, d = x.shape
    segs = ((0, n_lat), (n_lat, n_ctx))
    tm = _row_tile(t)
    o3 = 2 * ML_QK_WIDTH + 2 * ML_WIDTH
    o4 = o3 + 4 * ML_HEADS
    wb = w_in.astype(BF16)
    w_main = jnp.concatenate([wb[:, :o3], wb[:, o4:]], axis=1)
    pr = _in_projection(u, w_main[None], 0, rows=t, name="od_in")
    gates = _matmul(u, _pad_cols(wb[:, o3:o4], LANES)[None], tm=tm, tn=LANES, tk=d, out_dtype=F32,
                    name="od_in_gates")[:, :, :4 * ML_HEADS] + gate_b.reshape(-1)
    cos, sin = _rope_tables(n_lat, n_ctx, 256)
    qk = _conv_rope(pr, conv_w, conv_b[None, :], cos, sin, width=2 * ML_QK_WIDTH, segs=segs)
    g5 = gates.reshape(nb, t, 2, 2 * ML_HEADS)
    g_c, g_r = g5.transpose(0, 2, 1, 3), g5.transpose(0, 2, 3, 1)
    hf = _mlstm(qk, pr, g_c, g_r, direction=0, n_lat=n_lat)
    y_ml = _mlstm(qk, pr, g_c, g_r, direction=1, n_lat=n_lat, final_args=(hf, ml_norm_w[None, :]))
    tab = _na_bias_table(rpb, n_lat // GRID_W)
    y_na = _na(pr, tab, q_norm_w[None, :], k_norm_w[None, :], n_lat=n_lat, n_ctx=n_ctx, qcol=o3)
    return _out_projection((y_ml, y_na), w_out_all, x, _lat_ctx(mods[2], nb), j, rows=n_lat, n_lat=n_lat,
                           next_norm=next_norm, name="od_out")


def kernel(x, c, ctx, c_ctx, ada_w, ada_b, norm_w, ev_w_in, ev_conv_w, ev_conv_b, ssd_dt_bias, ssd_a_log, ssd_d,
           ssd_norm_w, hy_w1, hy_b1, hy_w2, hy_b2, hy_w3, hy_freq, hy_bias, ev_w_out, od_w_in, ml_conv_w, ml_conv_b,
           ml_gate_b, ml_norm_w, na_q_norm_w, na_k_norm_w, na_rpb, od_w_out, ffn_w_up, ffn_conv_w, ffn_conv_b,
           ffn_w_down):
    depth = ada_w.shape[0]
    assert depth == 2, "layer schedule below is written for one even and one odd layer"
    n_lat, n_ctx = x.shape[1], ctx.shape[1]
    t = n_lat + n_ctx
    both = ((0, n_lat), (n_lat, n_ctx))
    xs = jnp.concatenate([x, ctx], axis=1)

    nb = x.shape[0]
    w_down = ffn_w_down.astype(BF16)
    mods0 = _ada_mod(c, c_ctx, ada_w, ada_b, 0)
    mods1 = _ada_mod(c, c_ctx, ada_w, ada_b, 1)
    norm = lambda layer, sub, mods: (norm_w[layer, sub][None, :], _mod_pair(mods, 3 * sub, 3 * sub + 1, nb))

    w0, mod0 = norm(0, 0, mods0)
    u = _rmsmod(xs, w0, mod0, tm=_row_tile(t), rows=t, n_lat=n_lat)
    xs, u = _even_layer(xs, u, mods0, ev_w_in, ev_conv_w[0], ev_conv_b[0], ssd_dt_bias[0], ssd_a_log[0],
                        ssd_d[0], ssd_norm_w[0], hy_w1[0], hy_b1[0], hy_w2[0], hy_b2[0], hy_w3[0], hy_freq[0],
                        hy_bias[0], ev_w_out.astype(BF16), 0, n_lat=n_lat, n_ctx=n_ctx, next_norm=norm(0, 1, mods0))
    xs, u = _conv_ffn_block(xs, u, mods0, ffn_w_up, ffn_conv_w[0], ffn_conv_b[0], w_down, 0,
                            rows=t, n_lat=n_lat, segs=both, next_norm=norm(1, 0, mods1))
    xl, u = _odd_layer_latent(xs, u, mods1, od_w_in[0], ml_conv_w[0], ml_conv_b[0], ml_gate_b[0], ml_norm_w[0],
                              na_q_norm_w[0], na_k_norm_w[0], na_rpb[0], od_w_out.astype(BF16), 0,
                              n_lat=n_lat, n_ctx=n_ctx, next_norm=norm(1, 1, mods1))
    out, _ = _conv_ffn_block(xl, u, mods1, ffn_w_up, ffn_conv_w[1], ffn_conv_b[1], w_down, 1,
                             rows=n_lat, n_lat=n_lat, segs=((0, n_lat),), next_norm=None)
    return out
```

```python
import functools
import math

import numpy as np
import jax
import jax.numpy as jnp
from jax import lax
from jax.experimental import pallas as pl
from jax.experimental.pallas import tpu as pltpu

F32 = jnp.float32
BF16 = jnp.bfloat16
HI = lax.Precision.HIGHEST

D_MODEL = 2048
SEQ = 4096
CTX_LEN = 256
GRID_W = 64
EPS = 1e-6
CONV_W = 3
SSD_WIDTH = D_MODEL
SSD_HEAD_DIM = 64
SSD_HEADS = SSD_WIDTH // SSD_HEAD_DIM
SSD_STATE = 128
SSD_GROUPS = 4
SSD_HPG = SSD_HEADS // SSD_GROUPS
SSD_GW = SSD_WIDTH // SSD_GROUPS
SSD_XBC = SSD_WIDTH + 2 * SSD_GROUPS * SSD_STATE
HY_WIDTH = D_MODEL
HY_EMB = 33
HY_BANDS = (HY_EMB - 1) // 2
HY_ORDER = 64
HY_FAST = 0.3
HY_SLOW = 1.5
HY_TARGET = 1e-2
ML_HEADS = 8
ML_WIDTH = D_MODEL
ML_V_DIM = ML_WIDTH // ML_HEADS
ML_QK_DIM = ML_V_DIM // 2
ML_QK_WIDTH = ML_HEADS * ML_QK_DIM
ROPE_THETA = 10000.0
NA_WIDTH = D_MODEL
NA_HEAD_DIM = 128
NA_HEADS = NA_WIDTH // NA_HEAD_DIM
NA_ROWS = 8
NA_COLS = 16
NA_QROWS = 8
NA_KROWS = 16
NA_PARTS = 2
NA_PREP_STEPS = 8
NA_TAB_ROWS = 4
NA_PART_KROWS = 12
D_FF = 256 * ((8 * D_MODEL // 3 + 255) // 256)
EV_CONV_CH = SSD_XBC + 3 * HY_WIDTH
SCAN_Q = 128
LANES = 128
SUBLANES = 8
VMEM_LIMIT = 48 * 1024 * 1024
VMEM_LIMIT_BIG = 58 * 1024 * 1024


def _cp(sem, vmem=VMEM_LIMIT):
    return pltpu.CompilerParams(dimension_semantics=sem, vmem_limit_bytes=vmem)


def _sigmoid(x):
    return 0.5 * jnp.tanh(0.5 * x) + 0.5


def _silu(x):
    return x * _sigmoid(x)


def _softplus(x):
    return jnp.maximum(x, 0.0) + jnp.log(1.0 + jnp.exp(-jnp.abs(x)))


def _log_sigmoid(x):
    return -_softplus(-x)


def _dot(a, b, precision=None):
    return jnp.dot(a, b, preferred_element_type=F32, precision=precision)


def _dot_nt(a, b):
    return lax.dot_general(a, b, (((1,), (1,)), ((), ())), preferred_element_type=F32)


def _split_bf16(x, terms):
    out = []
    for _ in range(terms):
        part = x.astype(BF16)
        out.append(part)
        x = x - part.astype(F32)
    return out


def _dot_sel(x, sel, terms):
    selb = sel.astype(BF16)
    return sum(_dot(part, selb) for part in _split_bf16(x, terms))


def _sel_dot(sel, x, terms):
    selb = sel.astype(BF16)
    return sum(_dot(selb, part) for part in _split_bf16(x, terms))


def _mm_body(*refs, nk, n_extra, epilogue, cache_w, n_out, batch_axis):
    a_ref, w_ref = refs[0], refs[1]
    extra = refs[2:2 + n_extra]
    o_refs = refs[2 + n_extra:2 + n_extra + n_out]
    scratch = refs[2 + n_extra + n_out:]
    row_blk = pl.program_id(3 - batch_axis)
    if cache_w:
        wb_ref = scratch[0]

        @pl.when((pl.program_id(batch_axis) == 0) & (row_blk == 0))
        def _():
            wb_ref[...] = w_ref[...].astype(BF16)

        w_val = wb_ref[...]
    else:
        w_val = w_ref[...].astype(BF16)
    part = _dot(a_ref[...].astype(BF16), w_val)

    def finish(acc):
        vals = acc if epilogue is None else epilogue(acc, row_blk, *extra)
        for o_ref, val in zip(o_refs, vals if n_out > 1 else (vals,)):
            o_ref[...] = val.astype(o_ref.dtype)

    if nk == 1:
        finish(part)
        return
    acc_ref = scratch[0]
    k = pl.program_id(3)

    @pl.when(k == 0)
    def _():
        acc_ref[...] = part

    @pl.when(k > 0)
    def _():
        acc_ref[...] += part

    @pl.when(k == nk - 1)
    def _():
        finish(acc_ref[...])


def _matmul(a, w, *, tm, tn, tk, out_dtype, rows=None, cols=None, w_index=None, extra=(), epilogue=None,
            n_out=1, batch_inner=False, vmem=VMEM_LIMIT, name="mm"):
    ba, ta, kdim = a.shape
    bw, kdim2, n = w.shape
    assert kdim == kdim2
    nb = ba if w_index is not None else max(ba, bw)
    rows = ta if rows is None else rows
    n = n if cols is None else cols
    assert rows % tm == 0 and n % tn == 0 and kdim % tk == 0
    ni, nj, nk = rows // tm, n // tn, kdim // tk
    order = (lambda f: (lambda j, i, b, k: f(j, b, i, k))) if batch_inner else (lambda f: f)
    respec = lambda spec: pl.BlockSpec(spec.block_shape, order(spec.index_map))
    a_map = (lambda j, b, i, k: (b, i, k)) if ba > 1 else (lambda j, b, i, k: (0, i, k))
    if w_index is not None:
        w_map = lambda j, b, i, k: (w_index, k, j)
    else:
        w_map = (lambda j, b, i, k: (b, k, j)) if bw > 1 else (lambda j, b, i, k: (0, k, j))
    cache_w = w.dtype != BF16 and nk == 1 and (w_index is not None or bw == 1)
    in_specs = [pl.BlockSpec((None, tm, tk), order(a_map)), pl.BlockSpec((None, tk, tn), order(w_map))]
    args = [a, w]
    for arr, spec in extra:
        args.append(arr)
        in_specs.append(respec(spec))
    out_shape = jax.ShapeDtypeStruct((nb, rows, n), out_dtype)
    out_spec = pl.BlockSpec((None, tm, tn), order(lambda j, b, i, k: (b, i, j)))
    if n_out > 1:
        out_shape, out_spec = (out_shape,) * n_out, (out_spec,) * n_out
    scratch = [pltpu.VMEM((tm, tn), F32)] if nk > 1 else []
    if cache_w:
        scratch = [pltpu.VMEM((tk, tn), BF16)]
    body = functools.partial(_mm_body, nk=nk, n_extra=len(extra), epilogue=epilogue, cache_w=cache_w, n_out=n_out,
                             batch_axis=2 if batch_inner else 1)
    inner = "arbitrary" if cache_w else "parallel"
    grid = (nj, ni, nb, nk) if batch_inner else (nj, nb, ni, nk)
    return pl.pallas_call(
        body, out_shape=out_shape, grid=grid, in_specs=in_specs, out_specs=out_spec,
        scratch_shapes=scratch, name=name,
        compiler_params=_cp(("parallel", inner, inner, "arbitrary"), vmem))(*args)


def _row_is_lat(row_blk, tm, n_lat):
    rows = row_blk * tm + lax.broadcasted_iota(jnp.int32, (tm, 1), 0)
    return rows < n_lat


def _gated_residual_epilogue(acc, row_blk, res_ref, gate_ref, *, tm, n_lat):
    gate = jnp.where(_row_is_lat(row_blk, tm, n_lat), gate_ref[0], gate_ref[1])
    return res_ref[...] + gate * acc


def _gated_residual_rows_body(*refs, ks, tm, n_lat):
    n_a = len(ks)
    a_refs, w_ref, res_ref, gate_ref, o_ref = refs[:n_a], refs[n_a], refs[n_a + 1], refs[n_a + 2], refs[n_a + 3]
    acc, lo = None, 0
    for a_ref, kk in zip(a_refs, ks):
        part = _dot(a_ref[...], w_ref[lo:lo + kk, :])
        acc = part if acc is None else acc + part
        lo += kk
    o_ref[...] = _gated_residual_epilogue(acc, pl.program_id(1), res_ref, gate_ref, tm=tm, n_lat=n_lat)


def _matmul_gated_residual_rows(a_list, w, res, gate, *, w_index, tm, tn, rows, n_lat, name):
    nb = a_list[0].shape[0]
    _, kdim, n = w.shape
    ks = tuple(x.shape[2] for x in a_list)
    assert sum(ks) == kdim and rows % tm == 0 and n % tn == 0
    in_specs = [pl.BlockSpec((None, tm, kk), lambda b, i, j: (b, i, 0)) for kk in ks]
    in_specs += [pl.BlockSpec((None, kdim, tn), lambda b, i, j: (w_index, 0, j)),
                 pl.BlockSpec((None, tm, tn), lambda b, i, j: (b, i, j)),
                 pl.BlockSpec((None, 2, 1, tn), lambda b, i, j: (b, 0, 0, j))]
    return pl.pallas_call(
        functools.partial(_gated_residual_rows_body, ks=ks, tm=tm, n_lat=n_lat),
        out_shape=jax.ShapeDtypeStruct((nb, rows, n), F32), grid=(nb, rows // tm, n // tn), in_specs=in_specs,
        out_specs=pl.BlockSpec((None, tm, tn), lambda b, i, j: (b, i, j)), name=name,
        compiler_params=_cp(("parallel", "parallel", "parallel")))(*a_list, w, res, gate)


def _modulated_rmsnorm(x, is_lat, w_ref, mod_ref):
    y = x * lax.rsqrt(jnp.mean(x * x, axis=-1, keepdims=True) + EPS) * w_ref[...]
    shift = jnp.where(is_lat, mod_ref[0, 0], mod_ref[0, 1])
    scale = jnp.where(is_lat, mod_ref[1, 0], mod_ref[1, 1])
    return y * (1.0 + scale) + shift


def _rmsmod_body(x_ref, w_ref, mod_ref, o_ref, *, tm, n_lat):
    is_lat = _row_is_lat(pl.program_id(2), tm, n_lat)
    o_ref[...] = _modulated_rmsnorm(x_ref[...], is_lat, w_ref, mod_ref).astype(o_ref.dtype)


def _rmsmod(x, w, mod, *, tm, rows, n_lat):
    nb, _, d = x.shape
    tm = tm // 2
    assert rows % tm == 0 and tm % 16 == 0
    body = functools.partial(_rmsmod_body, tm=tm, n_lat=n_lat)
    return pl.pallas_call(
        body, out_shape=jax.ShapeDtypeStruct((nb, rows, d), BF16), grid=(1, nb, rows // tm),
        in_specs=[pl.BlockSpec((None, tm, d), lambda j, b, i: (b, i, 0)),
                  pl.BlockSpec((1, d), lambda j, b, i: (0, 0)),
                  pl.BlockSpec((None, 2, 2, 1, d), lambda j, b, i: (b, 0, 0, 0, 0))],
        out_specs=pl.BlockSpec((None, tm, d), lambda j, b, i: (b, i, 0)),
        name="rmsmod", compiler_params=_cp(("parallel", "parallel", "parallel")))(x, w, mod)


CONV_CH = 128
CONV_HALO = 16
CONV_STEPS = 4


def _shift_matrices():
    r = lax.broadcasted_iota(jnp.int32, (CONV_CH, CONV_CH + 2 * CONV_HALO), 0)
    c = lax.broadcasted_iota(jnp.int32, (CONV_CH, CONV_CH + 2 * CONV_HALO), 1)
    return (c == r + CONV_HALO - 1).astype(BF16), (c == r + CONV_HALO + 1).astype(BF16)


def _chunk_rows(r0, i):
    return pl.ds(r0 + pl.multiple_of(i * CONV_CH, CONV_CH), CONV_CH)


def _conv_at(x_ref, w_ref, b_ref, shifts, r0, n, steps):
    cur = [x_ref[_chunk_rows(r0, i), :] for i in steps]
    zero = jnp.zeros((CONV_HALO, cur[0].shape[1]), cur[0].dtype)
    lo = [pl.multiple_of(jnp.maximum(i * CONV_CH - CONV_HALO, 0), CONV_HALO) for i in steps]
    hi = [pl.multiple_of(jnp.minimum((i + 1) * CONV_CH, n - CONV_HALO), CONV_HALO) for i in steps]
    tail = [jnp.where(i > 0, x_ref[pl.ds(r0 + l, CONV_HALO), :], zero) for i, l in zip(steps, lo)]
    head = [jnp.where(i < n // CONV_CH - 1, x_ref[pl.ds(r0 + h, CONV_HALO), :], zero) for i, h in zip(steps, hi)]
    win = [jnp.concatenate([t, c, h], axis=0) for t, c, h in zip(tail, cur, head)]
    prev = [_dot(shifts[0], w) for w in win]
    nxt = [_dot(shifts[1], w) for w in win]
    return [w_ref[0:1, :] * p + w_ref[1:2, :] * c.astype(F32) + w_ref[2:3, :] * x + b_ref[...]
            for p, c, x in zip(prev, cur, nxt)]


def _conv_loop(segs, fn):
    for si, (r0, n) in enumerate(segs):
        def body(i, carry, r0=r0, n=n, si=si):
            fn(si, r0, n, [i])
            return carry

        lax.fori_loop(0, n // CONV_CH, body, 0, unroll=min(CONV_STEPS, n // CONV_CH))


def _conv_silu_body(x_ref, w_ref, b_ref, o_ref, *, segs):
    shifts = _shift_matrices()

    def fn(si, r0, n, steps):
        ys = [_silu(y) for y in _conv_at(x_ref, w_ref, b_ref, shifts, r0, n, steps)]
        for i, y in zip(steps, ys):
            o_ref[_chunk_rows(r0, i), :] = y.astype(o_ref.dtype)

    _conv_loop(segs, fn)


def _conv_rope_body(x_ref, w_ref, b_ref, cos_ref, sin_ref, o_ref, *, segs, tc):
    shifts = _shift_matrices()
    lane = lax.broadcasted_iota(jnp.int32, (CONV_CH, tc), 1)
    first = (lane % (ML_QK_DIM // 2)) < (ML_QK_DIM // 4)
    nf = ML_QK_DIM // 4

    def fn(si, r0, n, steps):
        ys = [_silu(y) for y in _conv_at(x_ref, w_ref, b_ref, shifts, r0, n, steps)]
        partner = [jnp.where(first, pltpu.roll(y, tc - nf, 1), pltpu.roll(y, nf, 1)) for y in ys]
        for i, y, p in zip(steps, ys, partner):
            rows = _chunk_rows(r0, i)
            o_ref[rows, :] = (y * cos_ref[rows, :] + p * sin_ref[rows, :]).astype(o_ref.dtype)

    _conv_loop(segs, fn)


def _conv_ffn_body(a_ref, x_ref, w_ref, b_ref, o_ref, *, segs):
    shifts = _shift_matrices()

    def fn(si, r0, n, steps):
        gs = [_silu(y) for y in _conv_at(x_ref, w_ref, b_ref, shifts, r0, n, steps)]
        for i, g in zip(steps, gs):
            rows = _chunk_rows(r0, i)
            o_ref[rows, :] = (a_ref[rows, :].astype(F32) * g).astype(o_ref.dtype)

    _conv_loop(segs, fn)


def _conv_hyena_body(x0_ref, x1_ref, v_ref, w0_ref, b0_ref, w1_ref, b1_ref, w2_ref, b2_ref,
                     ox0_ref, ozl_ref, ozc_ref, ozle_ref, ozlo_ref, ozce_ref, ozco_ref, *zs_refs, segs):
    shifts = _shift_matrices()
    half = CONV_CH // 2
    outs = ((ozl_ref, ozle_ref, ozlo_ref), (ozc_ref, ozce_ref, ozco_ref))

    def fn(si, r0, n, steps):
        x0 = _conv_at(x0_ref, w0_ref, b0_ref, shifts, r0, n, steps)
        x1 = _conv_at(x1_ref, w1_ref, b1_ref, shifts, r0, n, steps)
        v = _conv_at(v_ref, w2_ref, b2_ref, shifts, r0, n, steps)
        nat_ref, even_ref, odd_ref = outs[si]
        for i, x0_i, x1_i, v_i in zip(steps, x0, x1, v):
            ox0_ref[_chunk_rows(r0, i), :] = x0_i.astype(ox0_ref.dtype)
            zin = x1_i * v_i
            nat_ref[_chunk_rows(0, i), :] = zin.astype(BF16)
            t0 = r0 + pl.multiple_of(i * CONV_CH, CONV_CH)
            hrows = pl.ds(pl.multiple_of(i * half, half), half)
            for g, zs_ref in enumerate(zs_refs):
                lanes = slice(g * LANES, (g + 1) * LANES)
                zs_ref[pl.ds(t0, CONV_CH), :] = zin[:, lanes]
                even_ref[hrows, lanes] = zs_ref[pl.ds(t0, half, stride=2), :].astype(BF16)
                odd_ref[hrows, lanes] = zs_ref[pl.ds(t0 + 1, half, stride=2), :].astype(BF16)

    _conv_loop(segs, fn)


def _conv_specs(t_in, tc, xcol_blk, wcol_blk):
    return [pl.BlockSpec((None, t_in, tc), lambda b, j: (b, 0, j + xcol_blk)),
            pl.BlockSpec((CONV_W, tc), lambda b, j: (0, j + wcol_blk)),
            pl.BlockSpec((1, tc), lambda b, j: (0, j + wcol_blk))]


def _conv_silu(x, w, b, *, xcol, wcol, width, segs, tc=256):
    nb, t_in, _ = x.shape
    return pl.pallas_call(
        functools.partial(_conv_silu_body, segs=segs),
        out_shape=jax.ShapeDtypeStruct((nb, t_in, width), BF16), grid=(nb, width // tc),
        in_specs=_conv_specs(t_in, tc, xcol // tc, wcol // tc),
        out_specs=pl.BlockSpec((None, t_in, tc), lambda b, j: (b, 0, j)),
        name="conv_silu", compiler_params=_cp(("parallel", "parallel")))(x, w, b)


def _conv_rope(x, w, b, cos, sin, *, width, segs, tc=256):
    nb, t_in, _ = x.shape
    tab = pl.BlockSpec((t_in, tc), lambda b, j: (0, 0))
    return pl.pallas_call(
        functools.partial(_conv_rope_body, segs=segs, tc=tc),
        out_shape=jax.ShapeDtypeStruct((nb, t_in, width), BF16), grid=(nb, width // tc),
        in_specs=_conv_specs(t_in, tc, 0, 0) + [tab, tab],
        out_specs=pl.BlockSpec((None, t_in, tc), lambda b, j: (b, 0, j)),
        name="conv_rope", compiler_params=_cp(("parallel", "parallel")))(x, w, b, cos, sin)


def _conv_ffn(up, w, b, *, segs, rows, tc=256):
    nb, t_in, two_ff = up.shape
    ff = two_ff // 2
    return pl.pallas_call(
        functools.partial(_conv_ffn_body, segs=segs),
        out_shape=jax.ShapeDtypeStruct((nb, rows, ff), BF16), grid=(nb, ff // tc),
        in_specs=[pl.BlockSpec((None, rows, tc), lambda b, j: (b, 0, j))] + _conv_specs(rows, tc, ff // tc, 0),
        out_specs=pl.BlockSpec((None, rows, tc), lambda b, j: (b, 0, j)),
        name="conv_ffn", compiler_params=_cp(("parallel", "parallel")))(up, up, w, b)


def _conv_hyena(pr, w, b, *, xcol, wcol, segs, tc=256):
    nb, t_in, _ = pr.shape
    (_, n_lat), (_, n_ctx) = segs
    wd = HY_WIDTH
    out_rows = (t_in, n_lat, n_ctx, n_lat // 2, n_lat // 2, n_ctx // 2, n_ctx // 2)
    in_specs = [pl.BlockSpec((None, t_in, tc), functools.partial(lambda b, j, o: (b, 0, j + o), o=(xcol + m * wd) // tc))
                for m in range(3)]
    wargs = []
    for m in range(3):
        off = (wcol + m * wd) // tc
        in_specs.append(pl.BlockSpec((CONV_W, tc), functools.partial(lambda b, j, o: (0, j + o), o=off)))
        in_specs.append(pl.BlockSpec((1, tc), functools.partial(lambda b, j, o: (0, j + o), o=off)))
        wargs += [w, b]
    return pl.pallas_call(
        functools.partial(_conv_hyena_body, segs=segs),
        out_shape=tuple(jax.ShapeDtypeStruct((nb, rows, wd), BF16) for rows in out_rows),
        grid=(nb, wd // tc), in_specs=in_specs,
        out_specs=tuple(pl.BlockSpec((None, rows, tc), lambda b, j: (b, 0, j)) for rows in out_rows),
        scratch_shapes=[pltpu.VMEM((t_in, LANES), F32)] * (tc // LANES),
        name="conv_hyena", compiler_params=_cp(("parallel", "parallel")))(pr, pr, pr, *wargs)


def _scan_masks(reverse):
    q = SCAN_Q
    ii = lax.broadcasted_iota(jnp.int32, (q, q), 0)
    jj = lax.broadcasted_iota(jnp.int32, (q, q), 1)
    mask = (jj >= ii) if reverse else (jj <= ii)
    mask_t = (jj <= ii) if reverse else (jj >= ii)
    return mask, mask.astype(F32), mask_t.astype(F32)


def _chunk_map(reverse, n_chunks, n_lat_chunks):
    if reverse:
        return lambda k: n_chunks - 1 - k
    return lambda k: (k + n_lat_chunks) % n_chunks


SSD_GPS = 4


def _ssd_body(*refs, reverse, final):
    (xs_ref, b_ref, c_ref, dtc_ref, dtr_ref, dbc_ref, dbr_ref, alc_ref, alr_ref) = refs[:9]
    if final:
        yf_ref, z_ref, dsk_ref, nw_ref, o_ref, h_ref = refs[9:]
    else:
        o_ref, h_ref = refs[9:]

    @pl.when(pl.program_id(2) == 0)
    def _():
        h_ref[...] = jnp.zeros_like(h_ref)

    q, gw, hd = SCAN_Q, SSD_GW, SSD_HEAD_DIM
    mask, maskf, mask_tf = _scan_masks(reverse)
    groups = range(SSD_GPS)
    pairs = range(SSD_HPG // 2)
    cols = [slice(g * gw, (g + 1) * gw) for g in groups]
    xs = [xs_ref[:, cols[g]].astype(F32) for g in groups]
    bm = [b_ref[:, g * SSD_STATE:(g + 1) * SSD_STATE] for g in groups]
    cm = [c_ref[:, g * SSD_STATE:(g + 1) * SSD_STATE] for g in groups]
    h_prev = [h_ref[g] for g in groups]
    dt_c = [_softplus(dtc_ref[g] + dbc_ref[g]) for g in groups]
    dt_r = [_softplus(dtr_ref[g] + dbr_ref[g]) for g in groups]
    da_c = [dt_c[g] * (-jnp.exp(alc_ref[g])) for g in groups]
    da_r = [dt_r[g] * (-jnp.exp(alr_ref[g])) for g in groups]
    acum_c = [_sel_dot(maskf, da_c[g], 3) for g in groups]
    acum_r = [_dot_sel(da_r[g], mask_tf, 3) for g in groups]
    atot = [jnp.sum(da_c[g], axis=0, keepdims=True) for g in groups]

    er = lax.broadcasted_iota(jnp.int32, (SSD_HPG, gw), 0)
    ec = lax.broadcasted_iota(jnp.int32, (SSD_HPG, gw), 1)
    expand = (ec // hd == er).astype(F32)
    wide = [_dot_sel(jnp.concatenate([dt_c[g], jnp.exp(atot[g] - acum_c[g]), jnp.exp(acum_c[g]),
                                      jnp.broadcast_to(jnp.exp(atot[g]), (SUBLANES, SSD_HPG))], axis=0), expand, 1)
            for g in groups]
    xq = [xs[g] * wide[g][0:q] for g in groups]
    cb = [_dot_nt(cm[g], bm[g]) for g in groups]
    y_off = [wide[g][2 * q:3 * q] * _dot(cm[g], h_prev[g].astype(BF16)) for g in groups]
    lane = lax.broadcasted_iota(jnp.int32, (q, 2 * hd), 1)

    def pair_lhs(g, p):
        ms = [(cb[g] * jnp.where(mask, jnp.exp(acum_c[g][:, r:r + 1] - acum_r[g][r:r + 1, :]), 0.0)).astype(BF16)
              for r in (2 * p, 2 * p + 1)]
        return jnp.concatenate(ms, axis=1)

    def pair_rhs(g, p):
        xp = xq[g][:, 2 * hd * p:2 * hd * (p + 1)]
        return jnp.concatenate([jnp.where(lane < hd, xp, 0.0), jnp.where(lane >= hd, xp, 0.0)], axis=0).astype(BF16)

    y_in = [[_dot(pair_lhs(g, p), pair_rhs(g, p)) for p in pairs] for g in groups]
    y = [y_off[g] + jnp.concatenate(y_in[g], axis=1) for g in groups]
    bt = [bm[g].astype(F32).T.astype(BF16) for g in groups]
    upd = [_dot(bt[g], (xq[g] * wide[g][q:2 * q]).astype(BF16)) for g in groups]
    for g in groups:
        h_ref[g] = wide[g][3 * q:3 * q + 1] * h_prev[g] + upd[g]

    for g in groups:
        if final:
            ytot = y[g] + yf_ref[:, cols[g]] + xs[g] * dsk_ref[:, cols[g]]
            gated = ytot * _silu(z_ref[:, cols[g]].astype(F32))
            o = gated * lax.rsqrt(jnp.mean(gated * gated, axis=-1, keepdims=True) + EPS) * nw_ref[:, cols[g]]
            o_ref[:, cols[g]] = o.astype(o_ref.dtype)
        else:
            o_ref[:, cols[g]] = y[g]


def _ssd(xbc, dt_c, dt_r, db_c, db_r, al_c, al_r, *, direction, n_lat, final_args=None):
    nb, t, _ = xbc.shape
    q, gw, hpg = SCAN_Q, SSD_GW, SSD_HPG
    nch = t // q
    reverse = direction == 1
    cmap = _chunk_map(reverse, nch, n_lat // q)
    gps = SSD_GPS
    sw, gww = gps * SSD_STATE, gps * gw
    b0 = SSD_WIDTH // sw
    c0 = b0 + SSD_GROUPS // gps
    d = direction
    in_specs = [
        pl.BlockSpec((None, q, gww), lambda b, g, k: (b, cmap(k), g)),
        pl.BlockSpec((None, q, sw), lambda b, g, k: (b, cmap(k), b0 + g)),
        pl.BlockSpec((None, q, sw), lambda b, g, k: (b, cmap(k), c0 + g)),
        pl.BlockSpec((None, None, gps, q, hpg), lambda b, g, k: (b, d, g, cmap(k), 0)),
        pl.BlockSpec((None, None, gps, hpg, q), lambda b, g, k: (b, d, g, 0, cmap(k))),
        pl.BlockSpec((None, gps, 1, hpg), lambda b, g, k: (d, g, 0, 0)),
        pl.BlockSpec((None, gps, hpg, 1), lambda b, g, k: (d, g, 0, 0)),
        pl.BlockSpec((None, gps, 1, hpg), lambda b, g, k: (d, g, 0, 0)),
        pl.BlockSpec((None, gps, hpg, 1), lambda b, g, k: (d, g, 0, 0)),
    ]
    args = [xbc, xbc, xbc, dt_c, dt_r, db_c, db_r, al_c, al_r]
    final = final_args is not None
    if final:
        yf, pr, dskip, norm_w = final_args
        in_specs += [pl.BlockSpec((None, q, gww), lambda b, g, k: (b, cmap(k), g)),
                     pl.BlockSpec((None, q, gww), lambda b, g, k: (b, cmap(k), g)),
                     pl.BlockSpec((1, gww), lambda b, g, k: (0, g)),
                     pl.BlockSpec((1, gww), lambda b, g, k: (0, g))]
        args += [yf, pr, dskip, norm_w]
    out_shape = jax.ShapeDtypeStruct((nb, t, SSD_WIDTH), BF16 if final else F32)
    return pl.pallas_call(
        functools.partial(_ssd_body, reverse=reverse, final=final),
        out_shape=out_shape, grid=(nb, SSD_GROUPS // gps, nch), in_specs=in_specs,
        out_specs=pl.BlockSpec((None, q, gww), lambda b, g, k: (b, cmap(k), g)),
        scratch_shapes=[pltpu.VMEM((gps, SSD_STATE, gw), F32)],
        name="ssd_bwd" if reverse else "ssd_fwd",
        compiler_params=_cp(("parallel", "parallel", "arbitrary")))(*args)


def _hyfilt_body(f_ref, w1_ref, b1_ref, w2_ref, b2_ref, w3_ref, fr_ref, dl_ref, o_ref, *, tl):
    f = f_ref[...]
    h = jnp.sin(fr_ref[0:1, :] * (_dot(f, w1_ref[...], HI) + b1_ref[...]))
    h = jnp.sin(fr_ref[1:2, :] * (_dot(h, w2_ref[...], HI) + b2_ref[...]))
    h = _dot(h, w3_ref[...], HI)
    dec = jnp.exp(-f[:, 0:1] * dl_ref[...])
    row = pl.program_id(0) * tl + lax.broadcasted_iota(jnp.int32, (tl, 1), 0)
    wd = HY_WIDTH
    o_ref[:, 0:wd] = (h[:, 0:wd] * dec).astype(o_ref.dtype)
    o_ref[:, wd:2 * wd] = jnp.where(row == 0, 0.0, h[:, wd:2 * wd] * dec).astype(o_ref.dtype)


def _hyena_filters(length, w1, b1, w2, b2, w3, freq, *, tl=256):
    p = LANES
    t = jnp.linspace(0.0, 1.0, length, dtype=F32)[:, None]
    w = 2.0 * math.pi * jnp.arange(length, dtype=F32)[:, None] / length
    f = jnp.linspace(1e-4, HY_BANDS - 1, HY_BANDS, dtype=F32)[None, :]
    feats = jnp.concatenate([t, jnp.cos(f * w), -jnp.sin(f * w)], axis=-1)
    feats = jnp.concatenate([feats[0::2], feats[1::2]], axis=0)
    feats = jnp.pad(feats, ((0, 0), (0, p - HY_EMB)))
    po = p - HY_ORDER
    w1p = jnp.pad(w1, ((0, p - HY_EMB), (0, po)))
    w2p = jnp.pad(w2, ((0, po), (0, po)))
    w3p = jnp.pad(w3, ((0, po), (0, 0)))
    b1p = jnp.pad(b1[None, :], ((0, 0), (0, po)))
    b2p = jnp.pad(b2[None, :], ((0, 0), (0, po)))
    frp = jnp.pad(freq, ((0, 0), (0, po)))
    deltas = jnp.abs(jnp.linspace(math.log(HY_TARGET) / HY_FAST, math.log(HY_TARGET) / HY_SLOW, HY_WIDTH,
                                  dtype=F32))[None, :]
    full = lambda shape: pl.BlockSpec(shape, lambda i: (0,) * len(shape))
    out = pl.pallas_call(
        functools.partial(_hyfilt_body, tl=tl),
        out_shape=jax.ShapeDtypeStruct((length, 2 * HY_WIDTH), BF16), grid=(length // tl,),
        in_specs=[pl.BlockSpec((tl, p), lambda i: (i, 0)), full((p, p)), full((1, p)), full((p, p)), full((1, p)),
                  full((p, 2 * HY_WIDTH)), full((2, p)), full((1, HY_WIDTH))],
        out_specs=pl.BlockSpec((tl, 2 * HY_WIDTH), lambda i: (i, 0)),
        name="hyena_filters", compiler_params=_cp(("parallel",)))(feats, w1p, b1p, w2p, b2p, w3p, frp, deltas)
    return out[None]


def _dft_matrices(length):
    n = 2 * length
    k = jnp.arange(length, dtype=jnp.int32)[:, None]
    t = jnp.arange(length, dtype=jnp.int32)[None, :]
    step = 1 << (int(math.log2(length)) // 2)
    part = jnp.arange(length // step, dtype=jnp.int32)[:, None]
    ang_hi = ((part * step * t) % n).astype(F32) * (2.0 * math.pi / n)
    ang_lo = ((jnp.arange(step, dtype=jnp.int32)[:, None] * t) % n).astype(F32) * (2.0 * math.pi / n)
    ch, sh = jnp.cos(ang_hi)[:, None, :], jnp.sin(ang_hi)[:, None, :]
    cl, sl = jnp.cos(ang_lo)[None, :, :], jnp.sin(ang_lo)[None, :, :]
    c = (ch * cl - sh * sl).reshape(length, length)
    s = (sh * cl + ch * sl).reshape(length, length)
    alt_t = jnp.where(t % 2 == 0, 1.0, -1.0).astype(F32)
    fwd = jnp.concatenate([c, jnp.where(k == 0, alt_t, -s)], axis=0)
    alt_r = jnp.where(k % 2 == 0, 1.0, -1.0).astype(F32)
    inv_re = jnp.where(t == 0, 1.0 / n, (2.0 / n) * c)
    inv_im = jnp.where(t == 0, alt_r / n, (-2.0 / n) * s)
    inv = jnp.concatenate([inv_re, inv_im], axis=1)
    return fwd.astype(BF16)[None], inv.astype(BF16)[None]


def _tile_interleave(x, axis, hm):
    shp = x.shape
    n = shp[axis] // 2
    x = x.reshape(shp[:axis] + (2, n // hm, hm) + shp[axis + 1:])
    return jnp.swapaxes(x, axis, axis + 1).reshape(shp)


def _twiddle_table(length, hm):
    k = np.arange(length // 2, dtype=np.float64)[:, None] * (np.pi / length)
    tab = np.concatenate([np.cos(k), np.sin(k)], axis=0)
    tab = tab.reshape(2, -1, hm, 1).swapaxes(0, 1).reshape(length, 1)
    return jnp.asarray(np.broadcast_to(tab, (length, LANES)).astype(np.float32))


def _lanes(x, width):
    return jnp.tile(x, (1, width // LANES))


def _parity_banks(e, ot, tw, hm):
    er, ei, otr, oti = e[0:hm], e[hm:], ot[0:hm], ot[hm:]
    c, s = _lanes(tw[0:hm], e.shape[1]), _lanes(tw[hm:], e.shape[1])
    orr = otr * c + oti * s
    oi = oti * c - otr * s
    return (er + orr, ei + oi, er - orr, ei - oi), (c, s)


def _is_bin0(row_blk, hm):
    return (row_blk == 0) & (lax.broadcasted_iota(jnp.int32, (hm, 1), 0) == 0)


def _hy_kbank_body(ef_ref, eb_ref, of_ref, ob_ref, tw_ref, ka_ref, kb_ref, *, hm):
    tw = tw_ref[...]
    (afr, afi, bfr, bfi), _ = _parity_banks(ef_ref[...], of_ref[...], tw, hm)
    (abr, abi, bbr, bbi), _ = _parity_banks(eb_ref[...], ob_ref[...], tw, hm)
    first = _is_bin0(pl.program_id(0), hm)
    kai = jnp.where(first, ef_ref[hm:, :] + eb_ref[hm:, :], afi - abi)
    kbi = jnp.where(first, ob_ref[hm:, :] - of_ref[hm:, :], bfi - bbi)
    ka_ref[...] = jnp.concatenate([afr + abr, kai], axis=0)
    kb_ref[...] = jnp.concatenate([bfr + bbr, kbi], axis=0)


def _hy_kbank(ef, of, tw, *, tm, tc=512):
    _, ln, c2 = ef.shape
    c = c2 // 2
    fspec = pl.BlockSpec((None, tm, tc), lambda i, j: (0, i, j))
    bspec = pl.BlockSpec((None, tm, tc), lambda i, j: (0, i, j + c // tc))
    return pl.pallas_call(
        functools.partial(_hy_kbank_body, hm=tm // 2),
        out_shape=(jax.ShapeDtypeStruct((1, ln, c), F32),) * 2, grid=(ln // tm, c // tc),
        in_specs=[fspec, bspec, fspec, bspec, pl.BlockSpec((tm, LANES), lambda i, j: (i, 0))],
        out_specs=(fspec, fspec), name="hy_kbank",
        compiler_params=_cp(("parallel", "parallel")))(ef, ef, of, of, tw)


def _hy_spectrum_epilogue(acc, row_blk, e_ref, tw_ref, ka_ref, kb_ref, *, hm):
    e = e_ref[...]
    (ar, ai, br, bi), (c, s) = _parity_banks(e, acc, tw_ref[...], hm)
    kar, kai, kbr, kbi = ka_ref[0:hm], ka_ref[hm:], kb_ref[0:hm], kb_ref[hm:]
    yar, yai = ar * kar - ai * kai, ar * kai + ai * kar
    ybr, ybi = br * kbr - bi * kbi, br * kbi + bi * kbr
    qr, qi = yar - ybr, yai - ybi
    pr, pi = yar + ybr, yai + ybi
    qpr, qpi = qr * c - qi * s, qi * c + qr * s
    top = slice(0, SUBLANES)
    first = _is_bin0(row_blk, SUBLANES)
    y0, yl = ar[top] * kar[top], br[top] * kbr[top]
    amr, ami = e[hm:hm + SUBLANES], -acc[hm:hm + SUBLANES]
    ymr, ymi = amr * kai[top] - ami * kbi[top], amr * kbi[top] + ami * kai[top]

    def patch(full, special):
        return jnp.concatenate([jnp.where(first, special, full[top]), full[SUBLANES:]], axis=0)

    pr, pi = patch(pr, y0 + yl), patch(pi, 2.0 * ymr)
    qpr, qpi = patch(qpr, y0 - yl), patch(qpi, -2.0 * ymi)
    return jnp.concatenate([pr, pi], axis=0), jnp.concatenate([qpr, qpi], axis=0)


def _hyena_out_body(*refs, segs):
    ns = len(segs)
    x0_ref, db_ref, o_ref = refs[3 * ns:3 * ns + 3]
    ys_refs = refs[3 * ns + 3:]
    half = CONV_CH // 2
    for si, (r0, n) in enumerate(segs):
        ye_ref, yo_ref, zin_ref = refs[3 * si:3 * si + 3]

        def body(i, carry, r0=r0, ye_ref=ye_ref, yo_ref=yo_ref, zin_ref=zin_ref):
            t0 = pl.multiple_of(i * CONV_CH, CONV_CH)
            hrows = pl.ds(pl.multiple_of(i * half, half), half)
            for g, ys_ref in enumerate(ys_refs):
                lanes = slice(g * LANES, (g + 1) * LANES)
                ys_ref[pl.ds(r0 + t0, half, stride=2), :] = ye_ref[hrows, lanes].astype(F32)
                ys_ref[pl.ds(r0 + t0 + 1, half, stride=2), :] = yo_ref[hrows, lanes].astype(F32)
            rows = pl.ds(r0 + t0, CONV_CH)
            y = jnp.concatenate([ys_ref[rows, :] for ys_ref in ys_refs], axis=1)
            zin = zin_ref[pl.ds(t0, CONV_CH), :].astype(F32)
            o_ref[rows, :] = (x0_ref[rows, :].astype(F32) * (y + zin * db_ref[...])).astype(o_ref.dtype)
            return carry

        lax.fori_loop(0, n // CONV_CH, body, 0, unroll=min(4, n // CONV_CH))


def _hyena_out(parts, x0, dbias, *, segs, tc=256):
    nb, t, c = x0.shape
    in_specs, args = [], []
    for (ye, yo, zin), (_, n) in zip(parts, segs):
        in_specs += [pl.BlockSpec((None, n // 2, tc), lambda b, j: (b, 0, j)),
                     pl.BlockSpec((None, n // 2, tc), lambda b, j: (b, 0, j)),
                     pl.BlockSpec((None, n, tc), lambda b, j: (b, 0, j))]
        args += [ye, yo, zin]
    in_specs += [pl.BlockSpec((None, t, tc), lambda b, j: (b, 0, j)), pl.BlockSpec((1, tc), lambda b, j: (0, j))]
    return pl.pallas_call(
        functools.partial(_hyena_out_body, segs=segs),
        out_shape=jax.ShapeDtypeStruct((nb, t, c), BF16), grid=(nb, c // tc), in_specs=in_specs,
        out_specs=pl.BlockSpec((None, t, tc), lambda b, j: (b, 0, j)),
        scratch_shapes=[pltpu.VMEM((t, LANES), F32)] * (tc // LANES), name="hyena_out",
        compiler_params=_cp(("parallel", "parallel")))(*args, x0, dbias)


def _hyena_longconv(ze, zo, hfb, *, tn=1024):
    nb, h, c = ze.shape
    ln = 2 * h
    tm_s = min(512, ln)
    hm = tm_s // 2
    fwd, inv = _dft_matrices(h)
    fwd = _tile_interleave(fwd, 1, hm)
    inv = _tile_interleave(inv * 0.5, 2, hm)
    tw = _twiddle_table(ln, hm)
    tm_f, tk_f = min(1024, ln), min(2048, h)
    ef = _matmul(fwd, hfb[:, :h], tm=tm_f, tn=tn, tk=tk_f, out_dtype=F32, name="hy_filter_dft_even")
    of = _matmul(fwd, hfb[:, h:], tm=tm_f, tn=tn, tk=tk_f, out_dtype=F32, name="hy_filter_dft_odd")
    ka, kb = _hy_kbank(ef, of, tw, tm=tm_s)
    e = _matmul(fwd, ze, tm=tm_f, tn=tn, tk=tk_f, out_dtype=F32, name="hy_dft_even")
    tile = pl.BlockSpec((None, tm_s, tn), lambda j, b, i, k: (b, i, j))
    ktile = pl.BlockSpec((None, tm_s, tn), lambda j, b, i, k: (0, i, j))
    extra = [(e, tile), (tw, pl.BlockSpec((tm_s, LANES), lambda j, b, i, k: (i, 0))), (ka, ktile), (kb, ktile)]
    p, q = _matmul(fwd, zo, tm=tm_s, tn=tn, tk=tk_f, out_dtype=BF16, extra=extra, n_out=2, batch_inner=True,
                   epilogue=functools.partial(_hy_spectrum_epilogue, hm=hm), name="hy_dft_odd")
    tm_i, tk_i = min(1024, h), min(2048, ln)
    ye = _matmul(inv, p, tm=tm_i, tn=tn, tk=tk_i, out_dtype=BF16, name="hy_idft_even")
    yo = _matmul(inv, q, tm=tm_i, tn=tn, tk=tk_i, out_dtype=BF16, name="hy_idft_odd")
    return ye, yo


def _mlstm_body(*refs, reverse, final):
    q_ref, k_ref, v_ref, gc_ref, gr_ref = refs[:5]
    if final:
        hf_ref, og_ref, nw_ref, o_ref, ct_ref, n_ref, m_ref = refs[5:]
    else:
        o_ref, ct_ref, n_ref, m_ref = refs[5:]
    nh, dk, dv = ML_HEADS, ML_QK_DIM, ML_V_DIM
    scale = dk ** -0.5

    @pl.when(pl.program_id(1) == 0)
    def _():
        ct_ref[...] = jnp.zeros_like(ct_ref)
        n_ref[...] = jnp.zeros_like(n_ref)
        m_ref[...] = jnp.zeros_like(m_ref)

    gc, gr = gc_ref[...], gr_ref[...]
    li_c, lf_c = gc[:, 0:nh], _log_sigmoid(gc[:, nh:2 * nh])
    li_r, lf_r = gr[0:nh, :], _log_sigmoid(gr[nh:2 * nh, :])
    mask, maskf, mask_tf = _scan_masks(reverse)
    bc_c = _dot(maskf, lf_c, HI)
    bc_r = _dot(lf_r, mask_tf, HI)
    btot_c = jnp.sum(lf_c, axis=0, keepdims=True)
    n_all, m_all = n_ref[...], m_ref[...]
    heads = range(nh)
    qs = [q_ref[:, dk * h:dk * (h + 1)] for h in heads]
    ks = [k_ref[:, dk * h:dk * (h + 1)] for h in heads]
    vs = [v_ref[:, dv * h:dv * (h + 1)] for h in heads]
    cts = [ct_ref[h] for h in heads]
    m_s = [m_all[h:h + 1, 0:1] for h in heads]
    n_s = [n_all[h:h + 1, :] for h in heads]
    qk = [_dot_nt(qs[h], ks[h]) for h in heads]
    qc = [_dot(qs[h], cts[h].astype(BF16)) for h in heads]
    dmat = [jnp.where(mask, bc_c[:, h:h + 1] - bc_r[h:h + 1, :] + li_r[h:h + 1, :], -jnp.inf) for h in heads]
    inter = [bc_c[:, h:h + 1] + m_s[h] for h in heads]
    m_t = [jnp.maximum(inter[h], jnp.max(dmat[h], axis=1, keepdims=True)) for h in heads]
    s = [qk[h] * scale * jnp.exp(dmat[h] - m_t[h]) for h in heads]
    dec = [jnp.exp(inter[h] - m_t[h]) for h in heads]
    num = [_dot(s[h].astype(BF16), vs[h]) + (dec[h] * scale) * qc[h] for h in heads]
    qn = [jnp.sum(qs[h].astype(F32) * n_s[h], axis=1, keepdims=True) * scale for h in heads]
    den = [jnp.sum(s[h], axis=1, keepdims=True) + dec[h] * qn[h] for h in heads]
    hh = [num[h] / jnp.maximum(jnp.abs(den[h]), jnp.exp(-m_t[h])) for h in heads]

    wst_c = [btot_c[:, h:h + 1] - bc_c[:, h:h + 1] + li_c[:, h:h + 1] for h in heads]
    wst_r = [btot_c[:, h:h + 1] - bc_r[h:h + 1, :] + li_r[h:h + 1, :] for h in heads]
    m_new = [jnp.maximum(btot_c[:, h:h + 1] + m_s[h], jnp.max(wst_r[h], axis=1, keepdims=True)) for h in heads]
    dstate = [jnp.exp(btot_c[:, h:h + 1] + m_s[h] - m_new[h]) for h in heads]
    kw = [ks[h].astype(F32) * jnp.exp(wst_c[h] - m_new[h]) for h in heads]
    upd = [_dot(kw[h].T.astype(BF16), vs[h]) for h in heads]
    for h in heads:
        ct_ref[h] = dstate[h] * cts[h] + upd[h]
    n_rows = [dstate[h] * n_s[h] + jnp.sum(kw[h], axis=0, keepdims=True) for h in heads]
    m_rows = [jnp.broadcast_to(m_new[h], (1, LANES)) for h in heads]

    for h in heads:
        cols = slice(dv * h, dv * (h + 1))
        if final:
            hs = hh[h] + hf_ref[:, cols]
            hn = hs * lax.rsqrt(jnp.mean(hs * hs, axis=-1, keepdims=True) + EPS) * nw_ref[:, cols]
            o_ref[:, cols] = (hn * _sigmoid(og_ref[:, cols].astype(F32))).astype(o_ref.dtype)
        else:
            o_ref[:, cols] = hh[h]

    n_ref[...] = jnp.concatenate(n_rows, axis=0)
    m_ref[...] = jnp.concatenate(m_rows, axis=0)


def _mlstm(qk, pr, g_c, g_r, *, direction, n_lat, final_args=None):
    nb, t, _ = qk.shape
    q = SCAN_Q
    nch = t // q
    reverse = direction == 1
    cmap = _chunk_map(reverse, nch, n_lat // q)
    d = direction
    in_specs = [
        pl.BlockSpec((None, q, ML_QK_WIDTH), lambda b, k: (b, cmap(k), 0)),
        pl.BlockSpec((None, q, ML_QK_WIDTH), lambda b, k: (b, cmap(k), 1)),
        pl.BlockSpec((None, q, ML_WIDTH), lambda b, k: (b, cmap(k), 2 * ML_QK_WIDTH // ML_WIDTH)),
        pl.BlockSpec((None, None, q, 2 * ML_HEADS), lambda b, k: (b, d, cmap(k), 0)),
        pl.BlockSpec((None, None, 2 * ML_HEADS, q), lambda b, k: (b, d, 0, cmap(k))),
    ]
    args = [qk, qk, pr, g_c, g_r]
    final = final_args is not None
    if final:
        hf, norm_w = final_args
        in_specs += [pl.BlockSpec((None, q, ML_WIDTH), lambda b, k: (b, cmap(k), 0)),
                     pl.BlockSpec((None, q, ML_WIDTH), lambda b, k: (b, cmap(k), 2 * ML_QK_WIDTH // ML_WIDTH + 1)),
                     pl.BlockSpec((1, ML_WIDTH), lambda b, k: (0, 0))]
        args += [hf, pr, norm_w]
    out_shape = jax.ShapeDtypeStruct((nb, t, ML_WIDTH), BF16 if final else F32)
    return pl.pallas_call(
        functools.partial(_mlstm_body, reverse=reverse, final=final),
        out_shape=out_shape, grid=(nb, nch), in_specs=in_specs,
        out_specs=pl.BlockSpec((None, q, ML_WIDTH), lambda b, k: (b, cmap(k), 0)),
        scratch_shapes=[pltpu.VMEM((ML_HEADS, ML_QK_DIM, ML_V_DIM), F32), pltpu.VMEM((ML_HEADS, ML_QK_DIM), F32),
                        pltpu.VMEM((ML_HEADS, LANES), F32)],
        name="mlstm_bwd" if reverse else "mlstm_fwd",
        compiler_params=_cp(("parallel", "arbitrary")))(*args)


def _na_window_start(j, rows):
    return jnp.clip(j * NA_QROWS - NA_ROWS // 2, 0, rows - NA_KROWS)


def _na_part_offset(j, part, start, clip=jnp.clip):
    first = j * NA_QROWS + part * (NA_QROWS // NA_PARTS) - NA_ROWS // 2
    return clip(first - start, 0, NA_KROWS - NA_PART_KROWS)


def _na_body(q_ref, k_ref, v_ref, tab_ref, qw_ref, kw_ref, o_ref, kn_ref, v1_ref, *, n_lat, n_ctx):
    dh = NA_HEAD_DIM
    scale = dh ** -0.5
    rows = n_lat // GRID_W
    j = pl.program_id(2)
    nq, nk = NA_QROWS * GRID_W, NA_KROWS * GRID_W

    @pl.when(j == 0)
    def _():
        one_hot = (lax.broadcasted_iota(jnp.int32, (LANES, LANES), 1) == 0).astype(v1_ref.dtype)

        prep = (n_lat + n_ctx) // NA_PREP_STEPS
        one_hot = jnp.broadcast_to(one_hot[0:1], (prep, LANES))

        def body(i, carry):
            r = pl.ds(pl.multiple_of(i * prep, 16), prep)
            kk = k_ref[r, :].astype(F32)
            kn = kk * lax.rsqrt(jnp.mean(kk * kk, axis=-1, keepdims=True) + EPS) * kw_ref[...]
            kn_ref[r, :] = kn.astype(kn_ref.dtype)
            v1_ref[r, 0:dh] = v_ref[r, :]
            v1_ref[r, dh:dh + LANES] = one_hot
            return carry

        lax.fori_loop(0, NA_PREP_STEPS, body, 0)

    start = _na_window_start(j, rows)
    ctx = pl.ds(n_lat, n_ctx)
    parts = range(NA_PARTS)
    rpp = NA_QROWS // NA_PARTS
    pq = rpp * GRID_W
    off = [_na_part_offset(j, i, start) for i in parts]
    win = [pl.ds(pl.multiple_of((start + off[i]) * GRID_W, GRID_W), NA_PART_KROWS * GRID_W) for i in parts]
    qq = [q_ref[i * pq:(i + 1) * pq, :].astype(F32) for i in parts]
    qn = [(qq[i] * lax.rsqrt(jnp.mean(qq[i] * qq[i], axis=-1, keepdims=True) + EPS)
           * (qw_ref[...] * scale)).astype(BF16) for i in parts]
    sw = [_dot_nt(qn[i], kn_ref[win[i], :]) for i in parts]
    sc = [_dot_nt(qn[i], kn_ref[ctx, :]) for i in parts]

    def with_bias(i):
        out = []
        for r in range(rpp):
            d0 = start - (j * NA_QROWS + i * rpp + r) + NA_ROWS - 1 + NA_QROWS
            blocks = tab_ref[d0, pl.ds(off[i] // NA_TAB_ROWS, NA_PART_KROWS // NA_TAB_ROWS)]
            bias = jnp.concatenate([blocks[b] for b in range(NA_PART_KROWS // NA_TAB_ROWS)], axis=1)
            out.append(sw[i][r * GRID_W:(r + 1) * GRID_W] + bias)
        return jnp.concatenate(out, axis=0)

    sb = [with_bias(i) for i in parts]
    m = [jnp.maximum(jnp.max(sb[i], axis=1, keepdims=True), jnp.max(sc[i], axis=1, keepdims=True)) for i in parts]
    pw = [jnp.exp(sb[i] - m[i]) for i in parts]
    pc = [jnp.exp(sc[i] - m[i]) for i in parts]
    o = [_dot(pw[i].astype(BF16), v1_ref[win[i], :]) + _dot(pc[i].astype(BF16), v1_ref[ctx, :]) for i in parts]
    for i in parts:
        o_ref[i * pq:(i + 1) * pq, :] = (o[i][:, 0:dh] / o[i][:, dh:dh + 1]).astype(o_ref.dtype)


def _na_bias_table(rpb, rows):
    nd = 2 * NA_QROWS
    valid = np.zeros((nd, NA_KROWS), bool)
    seen = np.zeros((nd,), bool)
    for j in range(rows // NA_QROWS):
        start = int(np.clip(j * NA_QROWS - NA_ROWS // 2, 0, rows - NA_KROWS))
        for qr in range(NA_QROWS):
            r = j * NA_QROWS + qr
            rs = int(np.clip(r - NA_ROWS // 2, 0, rows - NA_ROWS))
            d = start - r + NA_ROWS - 1 + NA_QROWS
            v = np.array([(rs <= start + i < rs + NA_ROWS) for i in range(NA_KROWS)])
            assert 0 <= d < nd and (not seen[d] or (valid[d] == v).all())
            valid[d], seen[d] = v, True
            off = int(_na_part_offset(j, qr // (NA_QROWS // NA_PARTS), start, clip=np.clip))
            assert off % NA_TAB_ROWS == 0 and not v[:off].any() and not v[off + NA_PART_KROWS:].any()
    col = np.arange(GRID_W)
    cs = np.clip(col - NA_COLS // 2, 0, GRID_W - NA_COLS)
    in_win = (col[None, :] >= cs[:, None]) & (col[None, :] < cs[:, None] + NA_COLS)
    ndr, ndc, w = 2 * NA_ROWS - 1, 2 * NA_COLS - 1, GRID_W
    dr = (np.arange(nd)[:, None] - NA_QROWS) + np.arange(NA_KROWS)[None, :]
    ok = valid[:, None, :, None] & in_win[None, :, None, :] & (dr >= 0)[:, None, :, None] \
        & (dr < ndr)[:, None, :, None]
    ext = jnp.pad(rpb, ((0, 0), (0, 0), (w, w)), mode="edge")
    c0 = w + NA_COLS - 1
    t1 = jnp.stack([ext[:, :, c0 - qc:c0 - qc + w] for qc in range(w)], axis=2)
    t1p = jnp.pad(t1, ((0, 0), (NA_QROWS, nd + NA_KROWS - 1 - NA_QROWS - ndr), (0, 0), (0, 0)))
    wide = jnp.concatenate([t1p[:, m] for m in range(nd + NA_KROWS - 1)], axis=-1)
    wpad = LANES * pl.cdiv(wide.shape[-1], LANES)
    wide = jnp.pad(wide, ((0, 0), (0, 0), (0, wpad - wide.shape[-1])))
    nblk, bw = NA_KROWS // NA_TAB_ROWS, NA_TAB_ROWS * w
    okf = jnp.asarray(ok.reshape(nd, w, nblk, bw).transpose(0, 2, 1, 3).astype(np.float32))

    def body(wide_ref, ok_ref, o_ref):
        for d in range(nd):
            for b in range(nblk):
                lo = d * w + b * bw
                o_ref[d, b] = jnp.where(ok_ref[d, b] != 0.0, wide_ref[:, lo:lo + bw], -jnp.inf)

    return pl.pallas_call(
        body, out_shape=jax.ShapeDtypeStruct((rpb.shape[0], nd, nblk, w, bw), F32), grid=(rpb.shape[0],),
        in_specs=[pl.BlockSpec((None, w, wpad), lambda h: (h, 0, 0)),
                  pl.BlockSpec((nd, nblk, w, bw), lambda h: (0, 0, 0, 0))],
        out_specs=pl.BlockSpec((None, nd, nblk, w, bw), lambda h: (h, 0, 0, 0, 0)),
        name="na_bias_table", compiler_params=_cp(("parallel",)))(wide, okf)


def _na(pr, tab, q_w, k_w, *, n_lat, n_ctx, qcol):
    nb, t, _ = pr.shape
    dh, nh = NA_HEAD_DIM, NA_HEADS
    nq = NA_QROWS * GRID_W
    qb = qcol // dh
    nd = tab.shape[1]
    return pl.pallas_call(
        functools.partial(_na_body, n_lat=n_lat, n_ctx=n_ctx),
        out_shape=jax.ShapeDtypeStruct((nb, n_lat, NA_WIDTH), BF16), grid=(nb, nh, n_lat // nq),
        in_specs=[pl.BlockSpec((None, nq, dh), lambda b, h, j: (b, j, qb + h)),
                  pl.BlockSpec((None, t, dh), lambda b, h, j: (b, 0, qb + nh + h)),
                  pl.BlockSpec((None, t, dh), lambda b, h, j: (b, 0, qb + 2 * nh + h)),
                  pl.BlockSpec((None,) + tab.shape[1:], lambda b, h, j: (h, 0, 0, 0, 0)),
                  pl.BlockSpec((1, dh), lambda b, h, j: (0, 0)),
                  pl.BlockSpec((1, dh), lambda b, h, j: (0, 0))],
        out_specs=pl.BlockSpec((None, nq, dh), lambda b, h, j: (b, j, h)),
        scratch_shapes=[pltpu.VMEM((t, dh), BF16), pltpu.VMEM((t, dh + LANES), BF16)], name="na",
        compiler_params=_cp(("parallel", "parallel", "arbitrary")))(pr, pr, pr, tab, q_w, k_w)


def _rope_tables(n_lat, n_ctx, width):
    dh = ML_QK_DIM
    nf = dh // 4
    t = jnp.arange(n_lat)
    inv = ROPE_THETA ** (-jnp.arange(nf, dtype=F32) / nf)
    lane = np.arange(dh)
    pos = jnp.where((lane < dh // 2)[None, :], (t // GRID_W)[:, None], (t % GRID_W)[:, None]).astype(F32)
    ang = pos * inv[lane % nf][None, :]
    sign = np.where((lane % (dh // 2)) < nf, -1.0, 1.0).astype(np.float32)
    cos = jnp.concatenate([jnp.cos(ang), jnp.ones((n_ctx, dh), F32)], axis=0)
    sin = jnp.concatenate([jnp.sin(ang) * sign[None, :], jnp.zeros((n_ctx, dh), F32)], axis=0)
    reps = width // dh
    return jnp.tile(cos, (1, reps)), jnp.tile(sin, (1, reps))


def _ada_mod(c, c_ctx, w_all, b_all, layer):
    nb, d = c.shape
    rows = 16
    cv = jnp.concatenate([c, c_ctx[None, :], jnp.zeros((rows - nb - 1, d), F32)], axis=0)
    a = (cv * _sigmoid(cv)).astype(BF16)[None]
    out = _matmul(a, w_all, w_index=layer, tm=rows, tn=1024, tk=d, out_dtype=F32, name="ada_mod")[0]
    out = out[:nb + 1] + b_all[layer][None, :]
    return out.reshape(nb + 1, 6, d).transpose(1, 0, 2)


def _lat_ctx(m, nb):
    ctx = jnp.broadcast_to(m[nb][None, :], (nb, m.shape[1]))
    return jnp.stack([m[:nb], ctx], axis=1)[:, :, None, :]


def _pad_cols(w, n):
    return jnp.pad(w, ((0, 0), (0, n - w.shape[1])))


def _col_row_forms(g, groups):
    nb, t, _ = g.shape
    g5 = g.reshape(nb, t, 2, groups, -1)
    return g5.transpose(0, 2, 3, 1, 4), g5.transpose(0, 2, 3, 4, 1)


ROW_TILES = (1088, 1024, 512, 256)


def _row_tile(rows):
    return next(tm for tm in ROW_TILES if rows % tm == 0)


def _mod_pair(mods, shift, scale, nb):
    return jnp.stack([_lat_ctx(mods[shift], nb), _lat_ctx(mods[scale], nb)], axis=1)


def _in_projection(u, w_all, layer, *, rows, name, cols=None):
    tm = _row_tile(rows)
    tm = 2 * tm if rows % (2 * tm) == 0 else tm
    return _matmul(u, w_all, w_index=layer, cols=cols, tm=tm, tn=1024, tk=u.shape[2], out_dtype=BF16,
                   rows=rows, vmem=VMEM_LIMIT_BIG, name=name)


def _out_projection(a_list, w_all, x, gate, layer, *, rows, n_lat, next_norm, name):
    tm = _row_tile(rows)
    tn = 512 if w_all.shape[1] <= 2 * D_MODEL else 256
    x = _matmul_gated_residual_rows(a_list, w_all, x, gate, w_index=layer, tm=tm, tn=tn, rows=rows, n_lat=n_lat,
                                    name=name)
    if next_norm is None:
        return x, None
    return x, _rmsmod(x, *next_norm, tm=tm, rows=rows, n_lat=n_lat)


def _conv_ffn_block(x, u, mods, w_up_all, conv_w, conv_b, w_down_all, layer, *, rows, n_lat, segs, next_norm):
    nb = x.shape[0]
    up = _in_projection(u, w_up_all, layer, rows=rows, name="ffn_up")
    hid = _conv_ffn(up, conv_w, conv_b[None, :], segs=segs, rows=rows)
    return _out_projection((hid,), w_down_all, x, _lat_ctx(mods[5], nb), layer, rows=rows, n_lat=n_lat,
                           next_norm=next_norm, name="ffn_down")


def _even_layer(x, u, mods, w_in_all, conv_w, conv_b, dt_bias, a_log, d_skip, ssd_norm_w,
                hy_w1, hy_b1, hy_w2, hy_b2, hy_w3, hy_freq, hy_bias, w_out_all, j, *, n_lat, n_ctx, next_norm):
    nb, t, d = x.shape
    segs = ((0, n_lat), (n_lat, n_ctx))
    tm = _row_tile(t)
    n_main = SSD_WIDTH + EV_CONV_CH
    pr = _in_projection(u, w_in_all, j, rows=t, cols=n_main, name="ev_in")
    w_dt = _pad_cols(w_in_all[j, :, n_main:], LANES).astype(BF16)
    dtp = _matmul(u, w_dt[None], tm=tm, tn=LANES, tk=d, out_dtype=F32, name="ev_in_dt")[:, :, :2 * SSD_HEADS]
    cb = conv_b[None, :]
    xbc = _conv_silu(pr, conv_w, cb, xcol=SSD_WIDTH, wcol=0, width=SSD_XBC, segs=segs)
    x0, zin_l, zin_c, zle, zlo, zce, zco = _conv_hyena(pr, conv_w, cb, xcol=SSD_WIDTH + SSD_XBC, wcol=SSD_XBC,
                                                       segs=segs)

    dt_c, dt_r = _col_row_forms(dtp, SSD_GROUPS)
    db = dt_bias.reshape(2, SSD_GROUPS, 1, SSD_HPG)
    al = a_log.reshape(2, SSD_GROUPS, 1, SSD_HPG)
    scan_args = (xbc, dt_c, dt_r, db, db.transpose(0, 1, 3, 2), al, al.transpose(0, 1, 3, 2))
    yf = _ssd(*scan_args, direction=0, n_lat=n_lat)
    dsk = jnp.repeat(d_skip, SSD_HEAD_DIM)[None, :]
    y_ssd = _ssd(*scan_args, direction=1, n_lat=n_lat, final_args=(yf, pr, dsk, ssd_norm_w[None, :]))

    hyb = hy_bias[None, :]
    filt = functools.partial(_hyena_filters, w1=hy_w1, b1=hy_b1, w2=hy_w2, b2=hy_b2, w3=hy_w3, freq=hy_freq)
    lat = _hyena_longconv(zle, zlo, filt(n_lat)) + (zin_l,)
    ctx = _hyena_longconv(zce, zco, filt(n_ctx)) + (zin_c,)
    y_hy = _hyena_out((lat, ctx), x0, hyb, segs=segs)
    return _out_projection((y_ssd, y_hy), w_out_all, x, _lat_ctx(mods[2], nb), j, rows=t, n_lat=n_lat,
                           next_norm=next_norm, name="ev_out")


def _odd_layer_latent(x, u, mods, w_in, conv_w, conv_b, gate_b, ml_norm_w, q_norm_w, k_norm_w, rpb,
                      w_out_all, j, *, n_lat, n_ctx, next_norm):
    nb, t, d = x.shape
    segs = ((0, n_lat), (n_lat, n_ctx))
    tm = _row_tile(t)
    o3 = 2 * ML_QK_WIDTH + 2 * ML_WIDTH
    o4 = o3 + 4 * ML_HEADS
    wb = w_in.astype(BF16)
    w_main = jnp.concatenate([wb[:, :o3], wb[:, o4:]], axis=1)
    pr = _in_projection(u, w_main[None], 0, rows=t, name="od_in")
    gates = _matmul(u, _pad_cols(wb[:, o3:o4], LANES)[None], tm=tm, tn=LANES, tk=d, out_dtype=F32,
                    name="od_in_gates")[:, :, :4 * ML_HEADS] + gate_b.reshape(-1)
    cos, sin = _rope_tables(n_lat, n_ctx, 256)
    qk = _conv_rope(pr, conv_w, conv_b[None, :], cos, sin, width=2 * ML_QK_WIDTH, segs=segs)
    g5 = gates.reshape(nb, t, 2, 2 * ML_HEADS)
    g_c, g_r = g5.transpose(0, 2, 1, 3), g5.transpose(0, 2, 3, 1)
    hf = _mlstm(qk, pr, g_c, g_r, direction=0, n_lat=n_lat)
    y_ml = _mlstm(qk, pr, g_c, g_r, direction=1, n_lat=n_lat, final_args=(hf, ml_norm_w[None, :]))
    tab = _na_bias_table(rpb, n_lat // GRID_W)
    y_na = _na(pr, tab, q_norm_w[None, :], k_norm_w[None, :], n_lat=n_lat, n_ctx=n_ctx, qcol=o3)
    return _out_projection((y_ml, y_na), w_out_all, x, _lat_ctx(mods[2], nb), j, rows=n_lat, n_lat=n_lat,
                           next_norm=next_norm, name="od_out")


def kernel(x, c, ctx, c_ctx, ada_w, ada_b, norm_w, ev_w_in, ev_conv_w, ev_conv_b, ssd_dt_bias, ssd_a_log, ssd_d,
           ssd_norm_w, hy_w1, hy_b1, hy_w2, hy_b2, hy_w3, hy_freq, hy_bias, ev_w_out, od_w_in, ml_conv_w, ml_conv_b,
           ml_gate_b, ml_norm_w, na_q_norm_w, na_k_norm_w, na_rpb, od_w_out, ffn_w_up, ffn_conv_w, ffn_conv_b,
           ffn_w_down):
    depth = ada_w.shape[0]
    assert depth == 2, "layer schedule below is written for one even and one odd layer"
    n_lat, n_ctx = x.shape[1], ctx.shape[1]
    t = n_lat + n_ctx
    both = ((0, n_lat), (n_lat, n_ctx))
    xs = jnp.concatenate([x, ctx], axis=1)

    nb = x.shape[0]
    w_down = ffn_w_down.astype(BF16)
    mods0 = _ada_mod(c, c_ctx, ada_w, ada_b, 0)
    mods1 = _ada_mod(c, c_ctx, ada_w, ada_b, 1)
    norm = lambda layer, sub, mods: (norm_w[layer, sub][None, :], _mod_pair(mods, 3 * sub, 3 * sub + 1, nb))

    w0, mod0 = norm(0, 0, mods0)
    u = _rmsmod(xs, w0, mod0, tm=_row_tile(t), rows=t, n_lat=n_lat)
    xs, u = _even_layer(xs, u, mods0, ev_w_in, ev_conv_w[0], ev_conv_b[0], ssd_dt_bias[0], ssd_a_log[0],
                        ssd_d[0], ssd_norm_w[0], hy_w1[0], hy_b1[0], hy_w2[0], hy_b2[0], hy_w3[0], hy_freq[0],
                        hy_bias[0], ev_w_out.astype(BF16), 0, n_lat=n_lat, n_ctx=n_ctx, next_norm=norm(0, 1, mods0))
    xs, u = _conv_ffn_block(xs, u, mods0, ffn_w_up, ffn_conv_w[0], ffn_conv_b[0], w_down, 0,
                            rows=t, n_lat=n_lat, segs=both, next_norm=norm(1, 0, mods1))
    xl, u = _odd_layer_latent(xs, u, mods1, od_w_in[0], ml_conv_w[0], ml_conv_b[0], ml_gate_b[0], ml_norm_w[0],
                              na_q_norm_w[0], na_k_norm_w[0], na_rpb[0], od_w_out.astype(BF16), 0,
                              n_lat=n_lat, n_ctx=n_ctx, next_norm=norm(1, 1, mods1))
    out, _ = _conv_ffn_block(xl, u, mods1, ffn_w_up, ffn_conv_w[1], ffn_conv_b[1], w_down, 1,
                             rows=n_lat, n_lat=n_lat, segs=((0, n_lat),), next_norm=None)
    return out
```

```python
import functools
import math

import numpy as np
import jax
import jax.numpy as jnp
from jax import lax
from jax.experimental import pallas as pl
from jax.experimental.pallas import tpu as pltpu

F32 = jnp.float32
BF16 = jnp.bfloat16
HI = lax.Precision.HIGHEST

D_MODEL = 2048
SEQ = 4096
CTX_LEN = 256
GRID_W = 64
EPS = 1e-6
CONV_W = 3
SSD_WIDTH = D_MODEL
SSD_HEAD_DIM = 64
SSD_HEADS = SSD_WIDTH // SSD_HEAD_DIM
SSD_STATE = 128
SSD_GROUPS = 4
SSD_HPG = SSD_HEADS // SSD_GROUPS
SSD_GW = SSD_WIDTH // SSD_GROUPS
SSD_XBC = SSD_WIDTH + 2 * SSD_GROUPS * SSD_STATE
HY_WIDTH = D_MODEL
HY_EMB = 33
HY_BANDS = (HY_EMB - 1) // 2
HY_ORDER = 64
HY_FAST = 0.3
HY_SLOW = 1.5
HY_TARGET = 1e-2
ML_HEADS = 8
ML_WIDTH = D_MODEL
ML_V_DIM = ML_WIDTH // ML_HEADS
ML_QK_DIM = ML_V_DIM // 2
ML_QK_WIDTH = ML_HEADS * ML_QK_DIM
ROPE_THETA = 10000.0
NA_WIDTH = D_MODEL
NA_HEAD_DIM = 128
NA_HEADS = NA_WIDTH // NA_HEAD_DIM
NA_ROWS = 8
NA_COLS = 16
NA_QROWS = 8
NA_KROWS = 16
NA_PARTS = 2
NA_PREP_STEPS = 8
NA_TAB_ROWS = 4
NA_PART_KROWS = 12
D_FF = 256 * ((8 * D_MODEL // 3 + 255) // 256)
EV_CONV_CH = SSD_XBC + 3 * HY_WIDTH
SCAN_Q = 128
LANES = 128
SUBLANES = 8
VMEM_LIMIT = 48 * 1024 * 1024
VMEM_LIMIT_BIG = 58 * 1024 * 1024


def _cp(sem, vmem=VMEM_LIMIT):
    return pltpu.CompilerParams(dimension_semantics=sem, vmem_limit_bytes=vmem)


def _sigmoid(x):
    return 0.5 * jnp.tanh(0.5 * x) + 0.5


def _silu(x):
    return x * _sigmoid(x)


def _softplus(x):
    return jnp.maximum(x, 0.0) + jnp.log(1.0 + jnp.exp(-jnp.abs(x)))


def _log_sigmoid(x):
    return -_softplus(-x)


def _dot(a, b, precision=None):
    return jnp.dot(a, b, preferred_element_type=F32, precision=precision)


def _dot_nt(a, b):
    return lax.dot_general(a, b, (((1,), (1,)), ((), ())), preferred_element_type=F32)


def _split_bf16(x, terms):
    out = []
    for _ in range(terms):
        part = x.astype(BF16)
        out.append(part)
        x = x - part.astype(F32)
    return out


def _dot_sel(x, sel, terms):
    selb = sel.astype(BF16)
    return sum(_dot(part, selb) for part in _split_bf16(x, terms))


def _sel_dot(sel, x, terms):
    selb = sel.astype(BF16)
    return sum(_dot(selb, part) for part in _split_bf16(x, terms))


def _mm_body(*refs, nk, n_extra, epilogue, cache_w, n_out, batch_axis):
    a_ref, w_ref = refs[0], refs[1]
    extra = refs[2:2 + n_extra]
    o_refs = refs[2 + n_extra:2 + n_extra + n_out]
    scratch = refs[2 + n_extra + n_out:]
    row_blk = pl.program_id(3 - batch_axis)
    if cache_w:
        wb_ref = scratch[0]

        @pl.when((pl.program_id(batch_axis) == 0) & (row_blk == 0))
        def _():
            wb_ref[...] = w_ref[...].astype(BF16)

        w_val = wb_ref[...]
    else:
        w_val = w_ref[...].astype(BF16)
    part = _dot(a_ref[...].astype(BF16), w_val)

    def finish(acc):
        vals = acc if epilogue is None else epilogue(acc, row_blk, *extra)
        for o_ref, val in zip(o_refs, vals if n_out > 1 else (vals,)):
            o_ref[...] = val.astype(o_ref.dtype)

    if nk == 1:
        finish(part)
        return
    acc_ref = scratch[0]
    k = pl.program_id(3)

    @pl.when(k == 0)
    def _():
        acc_ref[...] = part

    @pl.when(k > 0)
    def _():
        acc_ref[...] += part

    @pl.when(k == nk - 1)
    def _():
        finish(acc_ref[...])


def _matmul(a, w, *, tm, tn, tk, out_dtype, rows=None, cols=None, w_index=None, extra=(), epilogue=None,
            n_out=1, batch_inner=False, vmem=VMEM_LIMIT, name="mm"):
    ba, ta, kdim = a.shape
    bw, kdim2, n = w.shape
    assert kdim == kdim2
    nb = ba if w_index is not None else max(ba, bw)
    rows = ta if rows is None else rows
    n = n if cols is None else cols
    assert rows % tm == 0 and n % tn == 0 and kdim % tk == 0
    ni, nj, nk = rows // tm, n // tn, kdim // tk
    order = (lambda f: (lambda j, i, b, k: f(j, b, i, k))) if batch_inner else (lambda f: f)
    respec = lambda spec: pl.BlockSpec(spec.block_shape, order(spec.index_map))
    a_map = (lambda j, b, i, k: (b, i, k)) if ba > 1 else (lambda j, b, i, k: (0, i, k))
    if w_index is not None:
        w_map = lambda j, b, i, k: (w_index, k, j)
    else:
        w_map = (lambda j, b, i, k: (b, k, j)) if bw > 1 else (lambda j, b, i, k: (0, k, j))
    cache_w = w.dtype != BF16 and nk == 1 and (w_index is not None or bw == 1)
    in_specs = [pl.BlockSpec((None, tm, tk), order(a_map)), pl.BlockSpec((None, tk, tn), order(w_map))]
    args = [a, w]
    for arr, spec in extra:
        args.append(arr)
        in_specs.append(respec(spec))
    out_shape = jax.ShapeDtypeStruct((nb, rows, n), out_dtype)
    out_spec = pl.BlockSpec((None, tm, tn), order(lambda j, b, i, k: (b, i, j)))
    if n_out > 1:
        out_shape, out_spec = (out_shape,) * n_out, (out_spec,) * n_out
    scratch = [pltpu.VMEM((tm, tn), F32)] if nk > 1 else []
    if cache_w:
        scratch = [pltpu.VMEM((tk, tn), BF16)]
    body = functools.partial(_mm_body, nk=nk, n_extra=len(extra), epilogue=epilogue, cache_w=cache_w, n_out=n_out,
                             batch_axis=2 if batch_inner else 1)
    inner = "arbitrary" if cache_w else "parallel"
    grid = (nj, ni, nb, nk) if batch_inner else (nj, nb, ni, nk)
    return pl.pallas_call(
        body, out_shape=out_shape, grid=grid, in_specs=in_specs, out_specs=out_spec,
        scratch_shapes=scratch, name=name,
        compiler_params=_cp(("parallel", inner, inner, "arbitrary"), vmem))(*args)


def _row_is_lat(row_blk, tm, n_lat):
    rows = row_blk * tm + lax.broadcasted_iota(jnp.int32, (tm, 1), 0)
    return rows < n_lat


def _gated_residual_epilogue(acc, row_blk, res_ref, gate_ref, *, tm, n_lat):
    gate = jnp.where(_row_is_lat(row_blk, tm, n_lat), gate_ref[0], gate_ref[1])
    return res_ref[...] + gate * acc


def _gated_residual_rows_body(*refs, ks, tm, n_lat):
    n_a = len(ks)
    a_refs, w_ref, res_ref, gate_ref, o_ref = refs[:n_a], refs[n_a], refs[n_a + 1], refs[n_a + 2], refs[n_a + 3]
    acc, lo = None, 0
    for a_ref, kk in zip(a_refs, ks):
        part = _dot(a_ref[...], w_ref[lo:lo + kk, :])
        acc = part if acc is None else acc + part
        lo += kk
    o_ref[...] = _gated_residual_epilogue(acc, pl.program_id(1), res_ref, gate_ref, tm=tm, n_lat=n_lat)


def _matmul_gated_residual_rows(a_list, w, res, gate, *, w_index, tm, tn, rows, n_lat, name):
    nb = a_list[0].shape[0]
    _, kdim, n = w.shape
    ks = tuple(x.shape[2] for x in a_list)
    assert sum(ks) == kdim and rows % tm == 0 and n % tn == 0
    in_specs = [pl.BlockSpec((None, tm, kk), lambda b, i, j: (b, i, 0)) for kk in ks]
    in_specs += [pl.BlockSpec((None, kdim, tn), lambda b, i, j: (w_index, 0, j)),
                 pl.BlockSpec((None, tm, tn), lambda b, i, j: (b, i, j)),
                 pl.BlockSpec((None, 2, 1, tn), lambda b, i, j: (b, 0, 0, j))]
    return pl.pallas_call(
        functools.partial(_gated_residual_rows_body, ks=ks, tm=tm, n_lat=n_lat),
        out_shape=jax.ShapeDtypeStruct((nb, rows, n), F32), grid=(nb, rows // tm, n // tn), in_specs=in_specs,
        out_specs=pl.BlockSpec((None, tm, tn), lambda b, i, j: (b, i, j)), name=name,
        compiler_params=_cp(("parallel", "parallel", "parallel")))(*a_list, w, res, gate)


def _modulated_rmsnorm(x, is_lat, w_ref, mod_ref):
    y = x * lax.rsqrt(jnp.mean(x * x, axis=-1, keepdims=True) + EPS) * w_ref[...]
    shift = jnp.where(is_lat, mod_ref[0, 0], mod_ref[0, 1])
    scale = jnp.where(is_lat, mod_ref[1, 0], mod_ref[1, 1])
    return y * (1.0 + scale) + shift


def _rmsmod_body(x_ref, w_ref, mod_ref, o_ref, *, tm, n_lat):
    is_lat = _row_is_lat(pl.program_id(2), tm, n_lat)
    o_ref[...] = _modulated_rmsnorm(x_ref[...], is_lat, w_ref, mod_ref).astype(o_ref.dtype)


def _rmsmod(x, w, mod, *, tm, rows, n_lat):
    nb, _, d = x.shape
    tm = tm // 2
    assert rows % tm == 0 and tm % 16 == 0
    body = functools.partial(_rmsmod_body, tm=tm, n_lat=n_lat)
    return pl.pallas_call(
        body, out_shape=jax.ShapeDtypeStruct((nb, rows, d), BF16), grid=(1, nb, rows // tm),
        in_specs=[pl.BlockSpec((None, tm, d), lambda j, b, i: (b, i, 0)),
                  pl.BlockSpec((1, d), lambda j, b, i: (0, 0)),
                  pl.BlockSpec((None, 2, 2, 1, d), lambda j, b, i: (b, 0, 0, 0, 0))],
        out_specs=pl.BlockSpec((None, tm, d), lambda j, b, i: (b, i, 0)),
        name="rmsmod", compiler_params=_cp(("parallel", "parallel", "parallel")))(x, w, mod)


CONV_CH = 128
CONV_HALO = 16
CONV_STEPS = 4


def _shift_matrices():
    r = lax.broadcasted_iota(jnp.int32, (CONV_CH, CONV_CH + 2 * CONV_HALO), 0)
    c = lax.broadcasted_iota(jnp.int32, (CONV_CH, CONV_CH + 2 * CONV_HALO), 1)
    return (c == r + CONV_HALO - 1).astype(BF16), (c == r + CONV_HALO + 1).astype(BF16)


def _chunk_rows(r0, i):
    return pl.ds(r0 + pl.multiple_of(i * CONV_CH, CONV_CH), CONV_CH)


def _taps(w_ref, b_ref, scale=1.0):
    return w_ref[...] * scale, b_ref[...] * scale


def _silu_of_half(h):
    return h + h * jnp.tanh(h)


def _conv_at(x_ref, taps, shifts, r0, n, steps):
    w, b = taps
    cur = [x_ref[_chunk_rows(r0, i), :] for i in steps]
    zero = jnp.zeros((CONV_HALO, cur[0].shape[1]), cur[0].dtype)
    lo = [pl.multiple_of(jnp.maximum(i * CONV_CH - CONV_HALO, 0), CONV_HALO) for i in steps]
    hi = [pl.multiple_of(jnp.minimum((i + 1) * CONV_CH, n - CONV_HALO), CONV_HALO) for i in steps]
    tail = [jnp.where(i > 0, x_ref[pl.ds(r0 + l, CONV_HALO), :], zero) for i, l in zip(steps, lo)]
    head = [jnp.where(i < n // CONV_CH - 1, x_ref[pl.ds(r0 + h, CONV_HALO), :], zero) for i, h in zip(steps, hi)]
    win = [jnp.concatenate([t, c, h], axis=0) for t, c, h in zip(tail, cur, head)]
    prev = [_dot(shifts[0], w) for w in win]
    nxt = [_dot(shifts[1], w) for w in win]
    return [w[0:1, :] * p + w[1:2, :] * c.astype(F32) + w[2:3, :] * x + b for p, c, x in zip(prev, cur, nxt)]


def _conv_loop(segs, fn):
    for si, (r0, n) in enumerate(segs):
        def body(i, carry, r0=r0, n=n, si=si):
            fn(si, r0, n, [i])
            return carry

        lax.fori_loop(0, n // CONV_CH, body, 0, unroll=min(CONV_STEPS, n // CONV_CH))


def _conv_silu_body(x_ref, w_ref, b_ref, o_ref, *, segs):
    shifts = _shift_matrices()
    half_taps = _taps(w_ref, b_ref, 0.5)

    def fn(si, r0, n, steps):
        ys = [_silu_of_half(h) for h in _conv_at(x_ref, half_taps, shifts, r0, n, steps)]
        for i, y in zip(steps, ys):
            o_ref[_chunk_rows(r0, i), :] = y.astype(o_ref.dtype)

    _conv_loop(segs, fn)


def _conv_rope_body(x_ref, w_ref, b_ref, cos_ref, sin_ref, o_ref, *, segs, tc):
    shifts = _shift_matrices()
    lane = lax.broadcasted_iota(jnp.int32, (CONV_CH, tc), 1)
    first = (lane % (ML_QK_DIM // 2)) < (ML_QK_DIM // 4)
    nf = ML_QK_DIM // 4
    half_taps = _taps(w_ref, b_ref, 0.5)

    def fn(si, r0, n, steps):
        ys = [_silu_of_half(h) for h in _conv_at(x_ref, half_taps, shifts, r0, n, steps)]
        partner = [jnp.where(first, pltpu.roll(y, tc - nf, 1), pltpu.roll(y, nf, 1)) for y in ys]
        for i, y, p in zip(steps, ys, partner):
            rows = _chunk_rows(r0, i)
            o_ref[rows, :] = (y * cos_ref[rows, :] + p * sin_ref[rows, :]).astype(o_ref.dtype)

    _conv_loop(segs, fn)


def _conv_ffn_body(a_ref, x_ref, w_ref, b_ref, o_ref, *, segs):
    shifts = _shift_matrices()
    half_taps = _taps(w_ref, b_ref, 0.5)

    def fn(si, r0, n, steps):
        gs = [_silu_of_half(h) for h in _conv_at(x_ref, half_taps, shifts, r0, n, steps)]
        for i, g in zip(steps, gs):
            rows = _chunk_rows(r0, i)
            o_ref[rows, :] = (a_ref[rows, :].astype(F32) * g).astype(o_ref.dtype)

    _conv_loop(segs, fn)


def _conv_hyena_body(x0_ref, x1_ref, v_ref, w0_ref, b0_ref, w1_ref, b1_ref, w2_ref, b2_ref,
                     ox0_ref, ozl_ref, ozc_ref, ozle_ref, ozlo_ref, ozce_ref, ozco_ref, *zs_refs, segs):
    shifts = _shift_matrices()
    taps = (_taps(w0_ref, b0_ref), _taps(w1_ref, b1_ref), _taps(w2_ref, b2_ref))
    half = CONV_CH // 2
    outs = ((ozl_ref, ozle_ref, ozlo_ref), (ozc_ref, ozce_ref, ozco_ref))

    def fn(si, r0, n, steps):
        x0 = _conv_at(x0_ref, taps[0], shifts, r0, n, steps)
        x1 = _conv_at(x1_ref, taps[1], shifts, r0, n, steps)
        v = _conv_at(v_ref, taps[2], shifts, r0, n, steps)
        nat_ref, even_ref, odd_ref = outs[si]
        for i, x0_i, x1_i, v_i in zip(steps, x0, x1, v):
            ox0_ref[_chunk_rows(r0, i), :] = x0_i.astype(ox0_ref.dtype)
            zin = x1_i * v_i
            nat_ref[_chunk_rows(0, i), :] = zin.astype(BF16)
            t0 = r0 + pl.multiple_of(i * CONV_CH, CONV_CH)
            hrows = pl.ds(pl.multiple_of(i * half, half), half)
            for g, zs_ref in enumerate(zs_refs):
                lanes = slice(g * LANES, (g + 1) * LANES)
                zs_ref[pl.ds(t0, CONV_CH), :] = zin[:, lanes]
                even_ref[hrows, lanes] = zs_ref[pl.ds(t0, half, stride=2), :].astype(BF16)
                odd_ref[hrows, lanes] = zs_ref[pl.ds(t0 + 1, half, stride=2), :].astype(BF16)

    _conv_loop(segs, fn)


def _conv_specs(t_in, tc, xcol_blk, wcol_blk):
    return [pl.BlockSpec((None, t_in, tc), lambda b, j: (b, 0, j + xcol_blk)),
            pl.BlockSpec((CONV_W, tc), lambda b, j: (0, j + wcol_blk)),
            pl.BlockSpec((1, tc), lambda b, j: (0, j + wcol_blk))]


def _conv_silu(x, w, b, *, xcol, wcol, width, segs, tc=256):
    nb, t_in, _ = x.shape
    return pl.pallas_call(
        functools.partial(_conv_silu_body, segs=segs),
        out_shape=jax.ShapeDtypeStruct((nb, t_in, width), BF16), grid=(nb, width // tc),
        in_specs=_conv_specs(t_in, tc, xcol // tc, wcol // tc),
        out_specs=pl.BlockSpec((None, t_in, tc), lambda b, j: (b, 0, j)),
        name="conv_silu", compiler_params=_cp(("parallel", "parallel")))(x, w, b)


def _conv_rope(x, w, b, cos, sin, *, width, segs, tc=256):
    nb, t_in, _ = x.shape
    tab = pl.BlockSpec((t_in, tc), lambda b, j: (0, 0))
    return pl.pallas_call(
        functools.partial(_conv_rope_body, segs=segs, tc=tc),
        out_shape=jax.ShapeDtypeStruct((nb, t_in, width), BF16), grid=(nb, width // tc),
        in_specs=_conv_specs(t_in, tc, 0, 0) + [tab, tab],
        out_specs=pl.BlockSpec((None, t_in, tc), lambda b, j: (b, 0, j)),
        name="conv_rope", compiler_params=_cp(("parallel", "parallel")))(x, w, b, cos, sin)


def _conv_ffn(up, w, b, *, segs, rows, tc=256):
    nb, t_in, two_ff = up.shape
    ff = two_ff // 2
    return pl.pallas_call(
        functools.partial(_conv_ffn_body, segs=segs),
        out_shape=jax.ShapeDtypeStruct((nb, rows, ff), BF16), grid=(nb, ff // tc),
        in_specs=[pl.BlockSpec((None, rows, tc), lambda b, j: (b, 0, j))] + _conv_specs(rows, tc, ff // tc, 0),
        out_specs=pl.BlockSpec((None, rows, tc), lambda b, j: (b, 0, j)),
        name="conv_ffn", compiler_params=_cp(("parallel", "parallel")))(up, up, w, b)


def _conv_hyena(pr, w, b, *, xcol, wcol, segs, tc=256):
    nb, t_in, _ = pr.shape
    (_, n_lat), (_, n_ctx) = segs
    wd = HY_WIDTH
    out_rows = (t_in, n_lat, n_ctx, n_lat // 2, n_lat // 2, n_ctx // 2, n_ctx // 2)
    in_specs = [pl.BlockSpec((None, t_in, tc), functools.partial(lambda b, j, o: (b, 0, j + o), o=(xcol + m * wd) // tc))
                for m in range(3)]
    wargs = []
    for m in range(3):
        off = (wcol + m * wd) // tc
        in_specs.append(pl.BlockSpec((CONV_W, tc), functools.partial(lambda b, j, o: (0, j + o), o=off)))
        in_specs.append(pl.BlockSpec((1, tc), functools.partial(lambda b, j, o: (0, j + o), o=off)))
        wargs += [w, b]
    return pl.pallas_call(
        functools.partial(_conv_hyena_body, segs=segs),
        out_shape=tuple(jax.ShapeDtypeStruct((nb, rows, wd), BF16) for rows in out_rows),
        grid=(nb, wd // tc), in_specs=in_specs,
        out_specs=tuple(pl.BlockSpec((None, rows, tc), lambda b, j: (b, 0, j)) for rows in out_rows),
        scratch_shapes=[pltpu.VMEM((t_in, LANES), F32)] * (tc // LANES),
        name="conv_hyena", compiler_params=_cp(("parallel", "parallel")))(pr, pr, pr, *wargs)


def _scan_masks(reverse):
    q = SCAN_Q
    ii = lax.broadcasted_iota(jnp.int32, (q, q), 0)
    jj = lax.broadcasted_iota(jnp.int32, (q, q), 1)
    mask = (jj >= ii) if reverse else (jj <= ii)
    mask_t = (jj <= ii) if reverse else (jj >= ii)
    return mask, mask.astype(F32), mask_t.astype(F32)


def _chunk_map(reverse, n_chunks, n_lat_chunks):
    if reverse:
        return lambda k: n_chunks - 1 - k
    return lambda k: (k + n_lat_chunks) % n_chunks


SSD_GPS = 4


def _ssd_body(*refs, reverse, final):
    (xs_ref, b_ref, c_ref, dtc_ref, dtr_ref, dbc_ref, dbr_ref, alc_ref, alr_ref) = refs[:9]
    if final:
        yf_ref, z_ref, dsk_ref, nw_ref, o_ref, h_ref = refs[9:]
    else:
        o_ref, h_ref = refs[9:]

    @pl.when(pl.program_id(2) == 0)
    def _():
        h_ref[...] = jnp.zeros_like(h_ref)

    q, gw, hd = SCAN_Q, SSD_GW, SSD_HEAD_DIM
    mask, maskf, mask_tf = _scan_masks(reverse)
    groups = range(SSD_GPS)
    pairs = range(SSD_HPG // 2)
    cols = [slice(g * gw, (g + 1) * gw) for g in groups]
    xs = [xs_ref[:, cols[g]].astype(F32) for g in groups]
    bm = [b_ref[:, g * SSD_STATE:(g + 1) * SSD_STATE] for g in groups]
    cm = [c_ref[:, g * SSD_STATE:(g + 1) * SSD_STATE] for g in groups]
    h_prev = [h_ref[g] for g in groups]
    dt_c = [_softplus(dtc_ref[g] + dbc_ref[g]) for g in groups]
    dt_r = [_softplus(dtr_ref[g] + dbr_ref[g]) for g in groups]
    da_c = [dt_c[g] * (-jnp.exp(alc_ref[g])) for g in groups]
    da_r = [dt_r[g] * (-jnp.exp(alr_ref[g])) for g in groups]
    acum_c = [_sel_dot(maskf, da_c[g], 3) for g in groups]
    acum_r = [_dot_sel(da_r[g], mask_tf, 3) for g in groups]
    atot = [jnp.sum(da_c[g], axis=0, keepdims=True) for g in groups]

    er = lax.broadcasted_iota(jnp.int32, (SSD_HPG, gw), 0)
    ec = lax.broadcasted_iota(jnp.int32, (SSD_HPG, gw), 1)
    expand = (ec // hd == er).astype(F32)
    wide = [_dot_sel(jnp.concatenate([dt_c[g], jnp.exp(atot[g] - acum_c[g]), jnp.exp(acum_c[g]),
                                      jnp.broadcast_to(jnp.exp(atot[g]), (SUBLANES, SSD_HPG))], axis=0), expand, 1)
            for g in groups]
    xq = [xs[g] * wide[g][0:q] for g in groups]
    cb = [_dot_nt(cm[g], bm[g]) for g in groups]
    y_off = [wide[g][2 * q:3 * q] * _dot(cm[g], h_prev[g].astype(BF16)) for g in groups]
    lane = lax.broadcasted_iota(jnp.int32, (q, 2 * hd), 1)

    def pair_lhs(g, p):
        ms = [(cb[g] * jnp.where(mask, jnp.exp(acum_c[g][:, r:r + 1] - acum_r[g][r:r + 1, :]), 0.0)).astype(BF16)
              for r in (2 * p, 2 * p + 1)]
        return jnp.concatenate(ms, axis=1)

    def pair_rhs(g, p):
        xp = xq[g][:, 2 * hd * p:2 * hd * (p + 1)]
        return jnp.concatenate([jnp.where(lane < hd, xp, 0.0), jnp.where(lane >= hd, xp, 0.0)], axis=0).astype(BF16)

    y_in = [[_dot(pair_lhs(g, p), pair_rhs(g, p)) for p in pairs] for g in groups]
    y = [y_off[g] + jnp.concatenate(y_in[g], axis=1) for g in groups]
    bt = [bm[g].astype(F32).T.astype(BF16) for g in groups]
    upd = [_dot(bt[g], (xq[g] * wide[g][q:2 * q]).astype(BF16)) for g in groups]
    for g in groups:
        h_ref[g] = wide[g][3 * q:3 * q + 1] * h_prev[g] + upd[g]

    for g in groups:
        if final:
            ytot = y[g] + yf_ref[:, cols[g]] + xs[g] * dsk_ref[:, cols[g]]
            gated = ytot * _silu(z_ref[:, cols[g]].astype(F32))
            o = gated * lax.rsqrt(jnp.mean(gated * gated, axis=-1, keepdims=True) + EPS) * nw_ref[:, cols[g]]
            o_ref[:, cols[g]] = o.astype(o_ref.dtype)
        else:
            o_ref[:, cols[g]] = y[g]


def _ssd(xbc, dt_c, dt_r, db_c, db_r, al_c, al_r, *, direction, n_lat, final_args=None):
    nb, t, _ = xbc.shape
    q, gw, hpg = SCAN_Q, SSD_GW, SSD_HPG
    nch = t // q
    reverse = direction == 1
    cmap = _chunk_map(reverse, nch, n_lat // q)
    gps = SSD_GPS
    sw, gww = gps * SSD_STATE, gps * gw
    b0 = SSD_WIDTH // sw
    c0 = b0 + SSD_GROUPS // gps
    d = direction
    in_specs = [
        pl.BlockSpec((None, q, gww), lambda b, g, k: (b, cmap(k), g)),
        pl.BlockSpec((None, q, sw), lambda b, g, k: (b, cmap(k), b0 + g)),
        pl.BlockSpec((None, q, sw), lambda b, g, k: (b, cmap(k), c0 + g)),
        pl.BlockSpec((None, None, gps, q, hpg), lambda b, g, k: (b, d, g, cmap(k), 0)),
        pl.BlockSpec((None, None, gps, hpg, q), lambda b, g, k: (b, d, g, 0, cmap(k))),
        pl.BlockSpec((None, gps, 1, hpg), lambda b, g, k: (d, g, 0, 0)),
        pl.BlockSpec((None, gps, hpg, 1), lambda b, g, k: (d, g, 0, 0)),
        pl.BlockSpec((None, gps, 1, hpg), lambda b, g, k: (d, g, 0, 0)),
        pl.BlockSpec((None, gps, hpg, 1), lambda b, g, k: (d, g, 0, 0)),
    ]
    args = [xbc, xbc, xbc, dt_c, dt_r, db_c, db_r, al_c, al_r]
    final = final_args is not None
    if final:
        yf, pr, dskip, norm_w = final_args
        in_specs += [pl.BlockSpec((None, q, gww), lambda b, g, k: (b, cmap(k), g)),
                     pl.BlockSpec((None, q, gww), lambda b, g, k: (b, cmap(k), g)),
                     pl.BlockSpec((1, gww), lambda b, g, k: (0, g)),
                     pl.BlockSpec((1, gww), lambda b, g, k: (0, g))]
        args += [yf, pr, dskip, norm_w]
    out_shape = jax.ShapeDtypeStruct((nb, t, SSD_WIDTH), BF16 if final else F32)
    return pl.pallas_call(
        functools.partial(_ssd_body, reverse=reverse, final=final),
        out_shape=out_shape, grid=(nb, SSD_GROUPS // gps, nch), in_specs=in_specs,
        out_specs=pl.BlockSpec((None, q, gww), lambda b, g, k: (b, cmap(k), g)),
        scratch_shapes=[pltpu.VMEM((gps, SSD_STATE, gw), F32)],
        name="ssd_bwd" if reverse else "ssd_fwd",
        compiler_params=_cp(("parallel", "parallel", "arbitrary")))(*args)


def _hyfilt_body(f_ref, w1_ref, b1_ref, w2_ref, b2_ref, w3_ref, fr_ref, dl_ref, o_ref, *, tl):
    f = f_ref[...]
    h = jnp.sin(fr_ref[0:1, :] * (_dot(f, w1_ref[...], HI) + b1_ref[...]))
    h = jnp.sin(fr_ref[1:2, :] * (_dot(h, w2_ref[...], HI) + b2_ref[...]))
    h = _dot(h, w3_ref[...], HI)
    dec = jnp.exp(-f[:, 0:1] * dl_ref[...])
    row = pl.program_id(0) * tl + lax.broadcasted_iota(jnp.int32, (tl, 1), 0)
    wd = HY_WIDTH
    o_ref[:, 0:wd] = (h[:, 0:wd] * dec).astype(o_ref.dtype)
    o_ref[:, wd:2 * wd] = jnp.where(row == 0, 0.0, h[:, wd:2 * wd] * dec).astype(o_ref.dtype)


def _hyena_filters(length, w1, b1, w2, b2, w3, freq, *, tl=256):
    p = LANES
    t = jnp.linspace(0.0, 1.0, length, dtype=F32)[:, None]
    w = 2.0 * math.pi * jnp.arange(length, dtype=F32)[:, None] / length
    f = jnp.linspace(1e-4, HY_BANDS - 1, HY_BANDS, dtype=F32)[None, :]
    feats = jnp.concatenate([t, jnp.cos(f * w), -jnp.sin(f * w)], axis=-1)
    feats = jnp.concatenate([feats[0::2], feats[1::2]], axis=0)
    feats = jnp.pad(feats, ((0, 0), (0, p - HY_EMB)))
    po = p - HY_ORDER
    w1p = jnp.pad(w1, ((0, p - HY_EMB), (0, po)))
    w2p = jnp.pad(w2, ((0, po), (0, po)))
    w3p = jnp.pad(w3, ((0, po), (0, 0)))
    b1p = jnp.pad(b1[None, :], ((0, 0), (0, po)))
    b2p = jnp.pad(b2[None, :], ((0, 0), (0, po)))
    frp = jnp.pad(freq, ((0, 0), (0, po)))
    deltas = jnp.abs(jnp.linspace(math.log(HY_TARGET) / HY_FAST, math.log(HY_TARGET) / HY_SLOW, HY_WIDTH,
                                  dtype=F32))[None, :]
    full = lambda shape: pl.BlockSpec(shape, lambda i: (0,) * len(shape))
    out = pl.pallas_call(
        functools.partial(_hyfilt_body, tl=tl),
        out_shape=jax.ShapeDtypeStruct((length, 2 * HY_WIDTH), BF16), grid=(length // tl,),
        in_specs=[pl.BlockSpec((tl, p), lambda i: (i, 0)), full((p, p)), full((1, p)), full((p, p)), full((1, p)),
                  full((p, 2 * HY_WIDTH)), full((2, p)), full((1, HY_WIDTH))],
        out_specs=pl.BlockSpec((tl, 2 * HY_WIDTH), lambda i: (i, 0)),
        name="hyena_filters", compiler_params=_cp(("parallel",)))(feats, w1p, b1p, w2p, b2p, w3p, frp, deltas)
    return out[None]


def _dft_matrices(length):
    n = 2 * length
    k = jnp.arange(length, dtype=jnp.int32)[:, None]
    t = jnp.arange(length, dtype=jnp.int32)[None, :]
    step = 1 << (int(math.log2(length)) // 2)
    part = jnp.arange(length // step, dtype=jnp.int32)[:, None]
    ang_hi = ((part * step * t) % n).astype(F32) * (2.0 * math.pi / n)
    ang_lo = ((jnp.arange(step, dtype=jnp.int32)[:, None] * t) % n).astype(F32) * (2.0 * math.pi / n)
    ch, sh = jnp.cos(ang_hi)[:, None, :], jnp.sin(ang_hi)[:, None, :]
    cl, sl = jnp.cos(ang_lo)[None, :, :], jnp.sin(ang_lo)[None, :, :]
    c = (ch * cl - sh * sl).reshape(length, length)
    s = (sh * cl + ch * sl).reshape(length, length)
    alt_t = jnp.where(t % 2 == 0, 1.0, -1.0).astype(F32)
    fwd = jnp.concatenate([c, jnp.where(k == 0, alt_t, -s)], axis=0)
    alt_r = jnp.where(k % 2 == 0, 1.0, -1.0).astype(F32)
    inv_re = jnp.where(t == 0, 1.0 / n, (2.0 / n) * c)
    inv_im = jnp.where(t == 0, alt_r / n, (-2.0 / n) * s)
    inv = jnp.concatenate([inv_re, inv_im], axis=1)
    return fwd.astype(BF16)[None], inv.astype(BF16)[None]


def _tile_interleave(x, axis, hm):
    shp = x.shape
    n = shp[axis] // 2
    x = x.reshape(shp[:axis] + (2, n // hm, hm) + shp[axis + 1:])
    return jnp.swapaxes(x, axis, axis + 1).reshape(shp)


def _twiddle_table(length, hm):
    k = np.arange(length // 2, dtype=np.float64)[:, None] * (np.pi / length)
    tab = np.concatenate([np.cos(k), np.sin(k)], axis=0)
    tab = tab.reshape(2, -1, hm, 1).swapaxes(0, 1).reshape(length, 1)
    return jnp.asarray(np.broadcast_to(tab, (length, LANES)).astype(np.float32))


def _lanes(x, width):
    return jnp.tile(x, (1, width // LANES))


def _parity_banks(e, ot, tw, hm):
    er, ei, otr, oti = e[0:hm], e[hm:], ot[0:hm], ot[hm:]
    c, s = _lanes(tw[0:hm], e.shape[1]), _lanes(tw[hm:], e.shape[1])
    orr = otr * c + oti * s
    oi = oti * c - otr * s
    return (er + orr, ei + oi, er - orr, ei - oi), (c, s)


def _is_bin0(row_blk, hm):
    return (row_blk == 0) & (lax.broadcasted_iota(jnp.int32, (hm, 1), 0) == 0)


def _hy_kbank_body(ef_ref, eb_ref, of_ref, ob_ref, tw_ref, ka_ref, kb_ref, *, hm):
    tw = tw_ref[...]
    (afr, afi, bfr, bfi), _ = _parity_banks(ef_ref[...], of_ref[...], tw, hm)
    (abr, abi, bbr, bbi), _ = _parity_banks(eb_ref[...], ob_ref[...], tw, hm)
    first = _is_bin0(pl.program_id(0), hm)
    kai = jnp.where(first, ef_ref[hm:, :] + eb_ref[hm:, :], afi - abi)
    kbi = jnp.where(first, ob_ref[hm:, :] - of_ref[hm:, :], bfi - bbi)
    ka_ref[...] = jnp.concatenate([afr + abr, kai], axis=0)
    kb_ref[...] = jnp.concatenate([bfr + bbr, kbi], axis=0)


def _hy_kbank(ef, of, tw, *, tm, tc=512):
    _, ln, c2 = ef.shape
    c = c2 // 2
    fspec = pl.BlockSpec((None, tm, tc), lambda i, j: (0, i, j))
    bspec = pl.BlockSpec((None, tm, tc), lambda i, j: (0, i, j + c // tc))
    return pl.pallas_call(
        functools.partial(_hy_kbank_body, hm=tm // 2),
        out_shape=(jax.ShapeDtypeStruct((1, ln, c), F32),) * 2, grid=(ln // tm, c // tc),
        in_specs=[fspec, bspec, fspec, bspec, pl.BlockSpec((tm, LANES), lambda i, j: (i, 0))],
        out_specs=(fspec, fspec), name="hy_kbank",
        compiler_params=_cp(("parallel", "parallel")))(ef, ef, of, of, tw)


def _hy_spectrum_epilogue(acc, row_blk, e_ref, tw_ref, ka_ref, kb_ref, *, hm):
    e = e_ref[...]
    (ar, ai, br, bi), (c, s) = _parity_banks(e, acc, tw_ref[...], hm)
    kar, kai, kbr, kbi = ka_ref[0:hm], ka_ref[hm:], kb_ref[0:hm], kb_ref[hm:]
    yar, yai = ar * kar - ai * kai, ar * kai + ai * kar
    ybr, ybi = br * kbr - bi * kbi, br * kbi + bi * kbr
    qr, qi = yar - ybr, yai - ybi
    pr, pi = yar + ybr, yai + ybi
    qpr, qpi = qr * c - qi * s, qi * c + qr * s
    top = slice(0, SUBLANES)
    first = _is_bin0(row_blk, SUBLANES)
    y0, yl = ar[top] * kar[top], br[top] * kbr[top]
    amr, ami = e[hm:hm + SUBLANES], -acc[hm:hm + SUBLANES]
    ymr, ymi = amr * kai[top] - ami * kbi[top], amr * kbi[top] + ami * kai[top]

    def patch(full, special):
        return jnp.concatenate([jnp.where(first, special, full[top]), full[SUBLANES:]], axis=0)

    pr, pi = patch(pr, y0 + yl), patch(pi, 2.0 * ymr)
    qpr, qpi = patch(qpr, y0 - yl), patch(qpi, -2.0 * ymi)
    return jnp.concatenate([pr, pi], axis=0), jnp.concatenate([qpr, qpi], axis=0)


def _hyena_out_body(*refs, segs):
    ns = len(segs)
    x0_ref, db_ref, o_ref = refs[3 * ns:3 * ns + 3]
    ys_refs = refs[3 * ns + 3:]
    half = CONV_CH // 2
    for si, (r0, n) in enumerate(segs):
        ye_ref, yo_ref, zin_ref = refs[3 * si:3 * si + 3]

        def body(i, carry, r0=r0, ye_ref=ye_ref, yo_ref=yo_ref, zin_ref=zin_ref):
            t0 = pl.multiple_of(i * CONV_CH, CONV_CH)
            hrows = pl.ds(pl.multiple_of(i * half, half), half)
            for g, ys_ref in enumerate(ys_refs):
                lanes = slice(g * LANES, (g + 1) * LANES)
                ys_ref[pl.ds(r0 + t0, half, stride=2), :] = ye_ref[hrows, lanes].astype(F32)
                ys_ref[pl.ds(r0 + t0 + 1, half, stride=2), :] = yo_ref[hrows, lanes].astype(F32)
            rows = pl.ds(r0 + t0, CONV_CH)
            y = jnp.concatenate([ys_ref[rows, :] for ys_ref in ys_refs], axis=1)
            zin = zin_ref[pl.ds(t0, CONV_CH), :].astype(F32)
            o_ref[rows, :] = (x0_ref[rows, :].astype(F32) * (y + zin * db_ref[...])).astype(o_ref.dtype)
            return carry

        lax.fori_loop(0, n // CONV_CH, body, 0, unroll=min(4, n // CONV_CH))


def _hyena_out(parts, x0, dbias, *, segs, tc=256):
    nb, t, c = x0.shape
    in_specs, args = [], []
    for (ye, yo, zin), (_, n) in zip(parts, segs):
        in_specs += [pl.BlockSpec((None, n // 2, tc), lambda b, j: (b, 0, j)),
                     pl.BlockSpec((None, n // 2, tc), lambda b, j: (b, 0, j)),
                     pl.BlockSpec((None, n, tc), lambda b, j: (b, 0, j))]
        args += [ye, yo, zin]
    in_specs += [pl.BlockSpec((None, t, tc), lambda b, j: (b, 0, j)), pl.BlockSpec((1, tc), lambda b, j: (0, j))]
    return pl.pallas_call(
        functools.partial(_hyena_out_body, segs=segs),
        out_shape=jax.ShapeDtypeStruct((nb, t, c), BF16), grid=(nb, c // tc), in_specs=in_specs,
        out_specs=pl.BlockSpec((None, t, tc), lambda b, j: (b, 0, j)),
        scratch_shapes=[pltpu.VMEM((t, LANES), F32)] * (tc // LANES), name="hyena_out",
        compiler_params=_cp(("parallel", "parallel")))(*args, x0, dbias)


def _hyena_longconv(ze, zo, hfb, *, tn=1024):
    nb, h, c = ze.shape
    ln = 2 * h
    tm_s = min(512, ln)
    hm = tm_s // 2
    fwd, inv = _dft_matrices(h)
    fwd = _tile_interleave(fwd, 1, hm)
    inv = _tile_interleave(inv * 0.5, 2, hm)
    tw = _twiddle_table(ln, hm)
    tm_f, tk_f = min(1024, ln), min(2048, h)
    ef = _matmul(fwd, hfb[:, :h], tm=tm_f, tn=tn, tk=tk_f, out_dtype=F32, name="hy_filter_dft_even")
    of = _matmul(fwd, hfb[:, h:], tm=tm_f, tn=tn, tk=tk_f, out_dtype=F32, name="hy_filter_dft_odd")
    ka, kb = _hy_kbank(ef, of, tw, tm=tm_s)
    e = _matmul(fwd, ze, tm=tm_f, tn=tn, tk=tk_f, out_dtype=F32, name="hy_dft_even")
    tile = pl.BlockSpec((None, tm_s, tn), lambda j, b, i, k: (b, i, j))
    ktile = pl.BlockSpec((None, tm_s, tn), lambda j, b, i, k: (0, i, j))
    extra = [(e, tile), (tw, pl.BlockSpec((tm_s, LANES), lambda j, b, i, k: (i, 0))), (ka, ktile), (kb, ktile)]
    p, q = _matmul(fwd, zo, tm=tm_s, tn=tn, tk=tk_f, out_dtype=BF16, extra=extra, n_out=2, batch_inner=True,
                   epilogue=functools.partial(_hy_spectrum_epilogue, hm=hm), name="hy_dft_odd")
    tm_i, tk_i = min(1024, h), min(2048, ln)
    ye = _matmul(inv, p, tm=tm_i, tn=tn, tk=tk_i, out_dtype=BF16, name="hy_idft_even")
    yo = _matmul(inv, q, tm=tm_i, tn=tn, tk=tk_i, out_dtype=BF16, name="hy_idft_odd")
    return ye, yo


def _mlstm_body(*refs, reverse, final):
    q_ref, k_ref, v_ref, gc_ref, gr_ref = refs[:5]
    if final:
        hf_ref, og_ref, nw_ref, o_ref, ct_ref, n_ref, m_ref = refs[5:]
    else:
        o_ref, ct_ref, n_ref, m_ref = refs[5:]
    nh, dk, dv = ML_HEADS, ML_QK_DIM, ML_V_DIM
    scale = dk ** -0.5

    @pl.when(pl.program_id(1) == 0)
    def _():
        ct_ref[...] = jnp.zeros_like(ct_ref)
        n_ref[...] = jnp.zeros_like(n_ref)
        m_ref[...] = jnp.zeros_like(m_ref)

    gc, gr = gc_ref[...], gr_ref[...]
    li_c, lf_c = gc[:, 0:nh], _log_sigmoid(gc[:, nh:2 * nh])
    li_r, lf_r = gr[0:nh, :], _log_sigmoid(gr[nh:2 * nh, :])
    mask, maskf, mask_tf = _scan_masks(reverse)
    bc_c = _dot(maskf, lf_c, HI)
    bc_r = _dot(lf_r, mask_tf, HI)
    btot_c = jnp.sum(lf_c, axis=0, keepdims=True)
    n_all, m_all = n_ref[...], m_ref[...]
    heads = range(nh)
    qs = [q_ref[:, dk * h:dk * (h + 1)] for h in heads]
    ks = [k_ref[:, dk * h:dk * (h + 1)] for h in heads]
    vs = [v_ref[:, dv * h:dv * (h + 1)] for h in heads]
    cts = [ct_ref[h] for h in heads]
    m_s = [m_all[h:h + 1, 0:1] for h in heads]
    n_s = [n_all[h:h + 1, :] for h in heads]
    qk = [_dot_nt(qs[h], ks[h]) for h in heads]
    qc = [_dot(qs[h], cts[h].astype(BF16)) for h in heads]
    dmat = [jnp.where(mask, bc_c[:, h:h + 1] - bc_r[h:h + 1, :] + li_r[h:h + 1, :], -jnp.inf) for h in heads]
    inter = [bc_c[:, h:h + 1] + m_s[h] for h in heads]
    m_t = [jnp.maximum(inter[h], jnp.max(dmat[h], axis=1, keepdims=True)) for h in heads]
    s = [qk[h] * scale * jnp.exp(dmat[h] - m_t[h]) for h in heads]
    dec = [jnp.exp(inter[h] - m_t[h]) for h in heads]
    num = [_dot(s[h].astype(BF16), vs[h]) + (dec[h] * scale) * qc[h] for h in heads]
    qn = [jnp.sum(qs[h].astype(F32) * n_s[h], axis=1, keepdims=True) * scale for h in heads]
    den = [jnp.sum(s[h], axis=1, keepdims=True) + dec[h] * qn[h] for h in heads]
    hh = [num[h] / jnp.maximum(jnp.abs(den[h]), jnp.exp(-m_t[h])) for h in heads]

    wst_c = [btot_c[:, h:h + 1] - bc_c[:, h:h + 1] + li_c[:, h:h + 1] for h in heads]
    wst_r = [btot_c[:, h:h + 1] - bc_r[h:h + 1, :] + li_r[h:h + 1, :] for h in heads]
    m_new = [jnp.maximum(btot_c[:, h:h + 1] + m_s[h], jnp.max(wst_r[h], axis=1, keepdims=True)) for h in heads]
    dstate = [jnp.exp(btot_c[:, h:h + 1] + m_s[h] - m_new[h]) for h in heads]
    kw = [ks[h].astype(F32) * jnp.exp(wst_c[h] - m_new[h]) for h in heads]
    upd = [_dot(kw[h].T.astype(BF16), vs[h]) for h in heads]
    for h in heads:
        ct_ref[h] = dstate[h] * cts[h] + upd[h]
    n_rows = [dstate[h] * n_s[h] + jnp.sum(kw[h], axis=0, keepdims=True) for h in heads]
    m_rows = [jnp.broadcast_to(m_new[h], (1, LANES)) for h in heads]

    for h in heads:
        cols = slice(dv * h, dv * (h + 1))
        if final:
            hs = hh[h] + hf_ref[:, cols]
            hn = hs * lax.rsqrt(jnp.mean(hs * hs, axis=-1, keepdims=True) + EPS) * nw_ref[:, cols]
            o_ref[:, cols] = (hn * _sigmoid(og_ref[:, cols].astype(F32))).astype(o_ref.dtype)
        else:
            o_ref[:, cols] = hh[h]

    n_ref[...] = jnp.concatenate(n_rows, axis=0)
    m_ref[...] = jnp.concatenate(m_rows, axis=0)


def _mlstm(qk, pr, g_c, g_r, *, direction, n_lat, final_args=None):
    nb, t, _ = qk.shape
    q = SCAN_Q
    nch = t // q
    reverse = direction == 1
    cmap = _chunk_map(reverse, nch, n_lat // q)
    d = direction
    in_specs = [
        pl.BlockSpec((None, q, ML_QK_WIDTH), lambda b, k: (b, cmap(k), 0)),
        pl.BlockSpec((None, q, ML_QK_WIDTH), lambda b, k: (b, cmap(k), 1)),
        pl.BlockSpec((None, q, ML_WIDTH), lambda b, k: (b, cmap(k), 2 * ML_QK_WIDTH // ML_WIDTH)),
        pl.BlockSpec((None, None, q, 2 * ML_HEADS), lambda b, k: (b, d, cmap(k), 0)),
        pl.BlockSpec((None, None, 2 * ML_HEADS, q), lambda b, k: (b, d, 0, cmap(k))),
    ]
    args = [qk, qk, pr, g_c, g_r]
    final = final_args is not None
    if final:
        hf, norm_w = final_args
        in_specs += [pl.BlockSpec((None, q, ML_WIDTH), lambda b, k: (b, cmap(k), 0)),
                     pl.BlockSpec((None, q, ML_WIDTH), lambda b, k: (b, cmap(k), 2 * ML_QK_WIDTH // ML_WIDTH + 1)),
                     pl.BlockSpec((1, ML_WIDTH), lambda b, k: (0, 0))]
        args += [hf, pr, norm_w]
    out_shape = jax.ShapeDtypeStruct((nb, t, ML_WIDTH), BF16 if final else F32)
    return pl.pallas_call(
        functools.partial(_mlstm_body, reverse=reverse, final=final),
        out_shape=out_shape, grid=(nb, nch), in_specs=in_specs,
        out_specs=pl.BlockSpec((None, q, ML_WIDTH), lambda b, k: (b, cmap(k), 0)),
        scratch_shapes=[pltpu.VMEM((ML_HEADS, ML_QK_DIM, ML_V_DIM), F32), pltpu.VMEM((ML_HEADS, ML_QK_DIM), F32),
                        pltpu.VMEM((ML_HEADS, LANES), F32)],
        name="mlstm_bwd" if reverse else "mlstm_fwd",
        compiler_params=_cp(("parallel", "arbitrary")))(*args)


def _na_window_start(j, rows):
    return jnp.clip(j * NA_QROWS - NA_ROWS // 2, 0, rows - NA_KROWS)


def _na_part_offset(j, part, start, clip=jnp.clip):
    first = j * NA_QROWS + part * (NA_QROWS // NA_PARTS) - NA_ROWS // 2
    return clip(first - start, 0, NA_KROWS - NA_PART_KROWS)


def _na_body(q_ref, k_ref, v_ref, tab_ref, qw_ref, kw_ref, o_ref, kn_ref, v1_ref, *, n_lat, n_ctx):
    dh = NA_HEAD_DIM
    scale = dh ** -0.5
    rows = n_lat // GRID_W
    j = pl.program_id(2)
    nq, nk = NA_QROWS * GRID_W, NA_KROWS * GRID_W

    @pl.when(j == 0)
    def _():
        one_hot = (lax.broadcasted_iota(jnp.int32, (LANES, LANES), 1) == 0).astype(v1_ref.dtype)

        prep = (n_lat + n_ctx) // NA_PREP_STEPS
        one_hot = jnp.broadcast_to(one_hot[0:1], (prep, LANES))

        def body(i, carry):
            r = pl.ds(pl.multiple_of(i * prep, 16), prep)
            kk = k_ref[r, :].astype(F32)
            kn = kk * lax.rsqrt(jnp.mean(kk * kk, axis=-1, keepdims=True) + EPS) * kw_ref[...]
            kn_ref[r, :] = kn.astype(kn_ref.dtype)
            v1_ref[r, 0:dh] = v_ref[r, :]
            v1_ref[r, dh:dh + LANES] = one_hot
            return carry

        lax.fori_loop(0, NA_PREP_STEPS, body, 0)

    start = _na_window_start(j, rows)
    ctx = pl.ds(n_lat, n_ctx)
    parts = range(NA_PARTS)
    rpp = NA_QROWS // NA_PARTS
    pq = rpp * GRID_W
    off = [_na_part_offset(j, i, start) for i in parts]
    win = [pl.ds(pl.multiple_of((start + off[i]) * GRID_W, GRID_W), NA_PART_KROWS * GRID_W) for i in parts]
    qq = [q_ref[i * pq:(i + 1) * pq, :].astype(F32) for i in parts]
    qn = [(qq[i] * lax.rsqrt(jnp.mean(qq[i] * qq[i], axis=-1, keepdims=True) + EPS)
           * (qw_ref[...] * scale)).astype(BF16) for i in parts]
    sw = [_dot_nt(qn[i], kn_ref[win[i], :]) for i in parts]
    sc = [_dot_nt(qn[i], kn_ref[ctx, :]) for i in parts]

    def with_bias(i):
        out = []
        for r in range(rpp):
            d0 = start - (j * NA_QROWS + i * rpp + r) + NA_ROWS - 1 + NA_QROWS
            blocks = tab_ref[d0, pl.ds(off[i] // NA_TAB_ROWS, NA_PART_KROWS // NA_TAB_ROWS)]
            bias = jnp.concatenate([blocks[b] for b in range(NA_PART_KROWS // NA_TAB_ROWS)], axis=1)
            out.append(sw[i][r * GRID_W:(r + 1) * GRID_W] + bias)
        return jnp.concatenate(out, axis=0)

    sb = [with_bias(i) for i in parts]
    m = [jnp.maximum(jnp.max(sb[i], axis=1, keepdims=True), jnp.max(sc[i], axis=1, keepdims=True)) for i in parts]
    pw = [jnp.exp(sb[i] - m[i]) for i in parts]
    pc = [jnp.exp(sc[i] - m[i]) for i in parts]
    o = [_dot(pw[i].astype(BF16), v1_ref[win[i], :]) + _dot(pc[i].astype(BF16), v1_ref[ctx, :]) for i in parts]
    for i in parts:
        o_ref[i * pq:(i + 1) * pq, :] = (o[i][:, 0:dh] / o[i][:, dh:dh + 1]).astype(o_ref.dtype)


def _na_bias_table(rpb, rows):
    nd = 2 * NA_QROWS
    valid = np.zeros((nd, NA_KROWS), bool)
    seen = np.zeros((nd,), bool)
    for j in range(rows // NA_QROWS):
        start = int(np.clip(j * NA_QROWS - NA_ROWS // 2, 0, rows - NA_KROWS))
        for qr in range(NA_QROWS):
            r = j * NA_QROWS + qr
            rs = int(np.clip(r - NA_ROWS // 2, 0, rows - NA_ROWS))
            d = start - r + NA_ROWS - 1 + NA_QROWS
            v = np.array([(rs <= start + i < rs + NA_ROWS) for i in range(NA_KROWS)])
            assert 0 <= d < nd and (not seen[d] or (valid[d] == v).all())
            valid[d], seen[d] = v, True
            off = int(_na_part_offset(j, qr // (NA_QROWS // NA_PARTS), start, clip=np.clip))
            assert off % NA_TAB_ROWS == 0 and not v[:off].any() and not v[off + NA_PART_KROWS:].any()
    col = np.arange(GRID_W)
    cs = np.clip(col - NA_COLS // 2, 0, GRID_W - NA_COLS)
    in_win = (col[None, :] >= cs[:, None]) & (col[None, :] < cs[:, None] + NA_COLS)
    ndr, ndc, w = 2 * NA_ROWS - 1, 2 * NA_COLS - 1, GRID_W
    dr = (np.arange(nd)[:, None] - NA_QROWS) + np.arange(NA_KROWS)[None, :]
    ok = valid[:, None, :, None] & in_win[None, :, None, :] & (dr >= 0)[:, None, :, None] \
        & (dr < ndr)[:, None, :, None]
    ext = jnp.pad(rpb, ((0, 0), (0, 0), (w, w)), mode="edge")
    c0 = w + NA_COLS - 1
    t1 = jnp.stack([ext[:, :, c0 - qc:c0 - qc + w] for qc in range(w)], axis=2)
    t1p = jnp.pad(t1, ((0, 0), (NA_QROWS, nd + NA_KROWS - 1 - NA_QROWS - ndr), (0, 0), (0, 0)))
    wide = jnp.concatenate([t1p[:, m] for m in range(nd + NA_KROWS - 1)], axis=-1)
    wpad = LANES * pl.cdiv(wide.shape[-1], LANES)
    wide = jnp.pad(wide, ((0, 0), (0, 0), (0, wpad - wide.shape[-1])))
    nblk, bw = NA_KROWS // NA_TAB_ROWS, NA_TAB_ROWS * w
    okf = jnp.asarray(ok.reshape(nd, w, nblk, bw).transpose(0, 2, 1, 3).astype(np.float32))

    def body(wide_ref, ok_ref, o_ref):
        for d in range(nd):
            for b in range(nblk):
                lo = d * w + b * bw
                o_ref[d, b] = jnp.where(ok_ref[d, b] != 0.0, wide_ref[:, lo:lo + bw], -jnp.inf)

    return pl.pallas_call(
        body, out_shape=jax.ShapeDtypeStruct((rpb.shape[0], nd, nblk, w, bw), F32), grid=(rpb.shape[0],),
        in_specs=[pl.BlockSpec((None, w, wpad), lambda h: (h, 0, 0)),
                  pl.BlockSpec((nd, nblk, w, bw), lambda h: (0, 0, 0, 0))],
        out_specs=pl.BlockSpec((None, nd, nblk, w, bw), lambda h: (h, 0, 0, 0, 0)),
        name="na_bias_table", compiler_params=_cp(("parallel",)))(wide, okf)


def _na(pr, tab, q_w, k_w, *, n_lat, n_ctx, qcol):
    nb, t, _ = pr.shape
    dh, nh = NA_HEAD_DIM, NA_HEADS
    nq = NA_QROWS * GRID_W
    qb = qcol // dh
    nd = tab.shape[1]
    return pl.pallas_call(
        functools.partial(_na_body, n_lat=n_lat, n_ctx=n_ctx),
        out_shape=jax.ShapeDtypeStruct((nb, n_lat, NA_WIDTH), BF16), grid=(nb, nh, n_lat // nq),
        in_specs=[pl.BlockSpec((None, nq, dh), lambda b, h, j: (b, j, qb + h)),
                  pl.BlockSpec((None, t, dh), lambda b, h, j: (b, 0, qb + nh + h)),
                  pl.BlockSpec((None, t, dh), lambda b, h, j: (b, 0, qb + 2 * nh + h)),
                  pl.BlockSpec((None,) + tab.shape[1:], lambda b, h, j: (h, 0, 0, 0, 0)),
                  pl.BlockSpec((1, dh), lambda b, h, j: (0, 0)),
                  pl.BlockSpec((1, dh), lambda b, h, j: (0, 0))],
        out_specs=pl.BlockSpec((None, nq, dh), lambda b, h, j: (b, j, h)),
        scratch_shapes=[pltpu.VMEM((t, dh), BF16), pltpu.VMEM((t, dh + LANES), BF16)], name="na",
        compiler_params=_cp(("parallel", "parallel", "arbitrary")))(pr, pr, pr, tab, q_w, k_w)


def _rope_tables(n_lat, n_ctx, width):
    dh = ML_QK_DIM
    nf = dh // 4
    t = jnp.arange(n_lat)
    inv = ROPE_THETA ** (-jnp.arange(nf, dtype=F32) / nf)
    lane = np.arange(dh)
    pos = jnp.where((lane < dh // 2)[None, :], (t // GRID_W)[:, None], (t % GRID_W)[:, None]).astype(F32)
    ang = pos * inv[lane % nf][None, :]
    sign = np.where((lane % (dh // 2)) < nf, -1.0, 1.0).astype(np.float32)
    cos = jnp.concatenate([jnp.cos(ang), jnp.ones((n_ctx, dh), F32)], axis=0)
    sin = jnp.concatenate([jnp.sin(ang) * sign[None, :], jnp.zeros((n_ctx, dh), F32)], axis=0)
    reps = width // dh
    return jnp.tile(cos, (1, reps)), jnp.tile(sin, (1, reps))


def _ada_mod(c, c_ctx, w_all, b_all, layer):
    nb, d = c.shape
    rows = 16
    cv = jnp.concatenate([c, c_ctx[None, :], jnp.zeros((rows - nb - 1, d), F32)], axis=0)
    a = (cv * _sigmoid(cv)).astype(BF16)[None]
    out = _matmul(a, w_all, w_index=layer, tm=rows, tn=1024, tk=d, out_dtype=F32, name="ada_mod")[0]
    out = out[:nb + 1] + b_all[layer][None, :]
    return out.reshape(nb + 1, 6, d).transpose(1, 0, 2)


def _lat_ctx(m, nb):
    ctx = jnp.broadcast_to(m[nb][None, :], (nb, m.shape[1]))
    return jnp.stack([m[:nb], ctx], axis=1)[:, :, None, :]


def _pad_cols(w, n):
    return jnp.pad(w, ((0, 0), (0, n - w.shape[1])))


def _col_row_forms(g, groups):
    nb, t, _ = g.shape
    g5 = g.reshape(nb, t, 2, groups, -1)
    return g5.transpose(0, 2, 3, 1, 4), g5.transpose(0, 2, 3, 4, 1)


ROW_TILES = (1088, 1024, 512, 256)


def _row_tile(rows):
    return next(tm for tm in ROW_TILES if rows % tm == 0)


def _mod_pair(mods, shift, scale, nb):
    return jnp.stack([_lat_ctx(mods[shift], nb), _lat_ctx(mods[scale], nb)], axis=1)


def _in_projection(u, w_all, layer, *, rows, name, cols=None):
    tm = _row_tile(rows)
    tm = 2 * tm if rows % (2 * tm) == 0 else tm
    return _matmul(u, w_all, w_index=layer, cols=cols, tm=tm, tn=1024, tk=u.shape[2], out_dtype=BF16,
                   rows=rows, vmem=VMEM_LIMIT_BIG, name=name)


def _out_projection(a_list, w_all, x, gate, layer, *, rows, n_lat, next_norm, name):
    tm = _row_tile(rows)
    tn = 512 if w_all.shape[1] <= 2 * D_MODEL else 256
    x = _matmul_gated_residual_rows(a_list, w_all, x, gate, w_index=layer, tm=tm, tn=tn, rows=rows, n_lat=n_lat,
                                    name=name)
    if next_norm is None:
        return x, None
    return x, _rmsmod(x, *next_norm, tm=tm, rows=rows, n_lat=n_lat)


def _conv_ffn_block(x, u, mods, w_up_all, conv_w, conv_b, w_down_all, layer, *, rows, n_lat, segs, next_norm):
    nb = x.shape[0]
    up = _in_projection(u, w_up_all, layer, rows=rows, name="ffn_up")
    hid = _conv_ffn(up, conv_w, conv_b[None, :], segs=segs, rows=rows)
    return _out_projection((hid,), w_down_all, x, _lat_ctx(mods[5], nb), layer, rows=rows, n_lat=n_lat,
                           next_norm=next_norm, name="ffn_down")


def _even_layer(x, u, mods, w_in_all, conv_w, conv_b, dt_bias, a_log, d_skip, ssd_norm_w,
                hy_w1, hy_b1, hy_w2, hy_b2, hy_w3, hy_freq, hy_bias, w_out_all, j, *, n_lat, n_ctx, next_norm):
    nb, t, d = x.shape
    segs = ((0, n_lat), (n_lat, n_ctx))
    tm = _row_tile(t)
    n_main = SSD_WIDTH + EV_CONV_CH
    pr = _in_projection(u, w_in_all, j, rows=t, cols=n_main, name="ev_in")
    w_dt = _pad_cols(w_in_all[j, :, n_main:], LANES).astype(BF16)
    dtp = _matmul(u, w_dt[None], tm=tm, tn=LANES, tk=d, out_dtype=F32, name="ev_in_dt")[:, :, :2 * SSD_HEADS]
    cb = conv_b[None, :]
    xbc = _conv_silu(pr, conv_w, cb, xcol=SSD_WIDTH, wcol=0, width=SSD_XBC, segs=segs)
    x0, zin_l, zin_c, zle, zlo, zce, zco = _conv_hyena(pr, conv_w, cb, xcol=SSD_WIDTH + SSD_XBC, wcol=SSD_XBC,
                                                       segs=segs)

    dt_c, dt_r = _col_row_forms(dtp, SSD_GROUPS)
    db = dt_bias.reshape(2, SSD_GROUPS, 1, SSD_HPG)
    al = a_log.reshape(2, SSD_GROUPS, 1, SSD_HPG)
    scan_args = (xbc, dt_c, dt_r, db, db.transpose(0, 1, 3, 2), al, al.transpose(0, 1, 3, 2))
    yf = _ssd(*scan_args, direction=0, n_lat=n_lat)
    dsk = jnp.repeat(d_skip, SSD_HEAD_DIM)[None, :]
    y_ssd = _ssd(*scan_args, direction=1, n_lat=n_lat, final_args=(yf, pr, dsk, ssd_norm_w[None, :]))

    hyb = hy_bias[None, :]
    filt = functools.partial(_hyena_filters, w1=hy_w1, b1=hy_b1, w2=hy_w2, b2=hy_b2, w3=hy_w3, freq=hy_freq)
    lat = _hyena_longconv(zle, zlo, filt(n_lat)) + (zin_l,)
    ctx = _hyena_longconv(zce, zco, filt(n_ctx)) + (zin_c,)
    y_hy = _hyena_out((lat, ctx), x0, hyb, segs=segs)
    return _out_projection((y_ssd, y_hy), w_out_all, x, _lat_ctx(mods[2], nb), j, rows=t, n_lat=n_lat,
                           next_norm=next_norm, name="ev_out")


def _odd_layer_latent(x, u, mods, w_in, conv_w, conv_b, gate_b, ml_norm_w, q_norm_w, k_norm_w, rpb,
                      w_out_all, j, *, n_lat, n_ctx, next_norm):
    nb, t, d = x.shape
    segs = ((0, n_lat), (n_lat, n_ctx))
    tm = _row_tile(t)
    o3 = 2 * ML_QK_WIDTH + 2 * ML_WIDTH
    o4 = o3 + 4 * ML_HEADS
    wb = w_in.astype(BF16)
    w_main = jnp.concatenate([wb[:, :o3], wb[:, o4:]], axis=1)
    pr = _in_projection(u, w_main[None], 0, rows=t, name="od_in")
    gates = _matmul(u, _pad_cols(wb[:, o3:o4], LANES)[None], tm=tm, tn=LANES, tk=d, out_dtype=F32,
                    name="od_in_gates")[:, :, :4 * ML_HEADS] + gate_b.reshape(-1)
    cos, sin = _rope_tables(n_lat, n_ctx, 256)
    qk = _conv_rope(pr, conv_w, conv_b[None, :], cos, sin, width=2 * ML_QK_WIDTH, segs=segs)
    g5 = gates.reshape(nb, t, 2, 2 * ML_HEADS)
    g_c, g_r = g5.transpose(0, 2, 1, 3), g5.transpose(0, 2, 3, 1)
    hf = _mlstm(qk, pr, g_c, g_r, direction=0, n_lat=n_lat)
    y_ml = _mlstm(qk, pr, g_c, g_r, direction=1, n_lat=n_lat, final_args=(hf, ml_norm_w[None, :]))
    tab = _na_bias_table(rpb, n_lat // GRID_W)
    y_na = _na(pr, tab, q_norm_w[None, :], k_norm_w[None, :], n_lat=n_lat, n_ctx=n_ctx, qcol=o3)
    return _out_projection((y_ml, y_na), w_out_all, x, _lat_ctx(mods[2], nb), j, rows=n_lat, n_lat=n_lat,
                           next_norm=next_norm, name="od_out")


def kernel(x, c, ctx, c_ctx, ada_w, ada_b, norm_w, ev_w_in, ev_conv_w, ev_conv_b, ssd_dt_bias, ssd_a_log, ssd_d,
           ssd_norm_w, hy_w1, hy_b1, hy_w2, hy_b2, hy_w3, hy_freq, hy_bias, ev_w_out, od_w_in, ml_conv_w, ml_conv_b,
           ml_gate_b, ml_norm_w, na_q_norm_w, na_k_norm_w, na_rpb, od_w_out, ffn_w_up, ffn_conv_w, ffn_conv_b,
           ffn_w_down):
    depth = ada_w.shape[0]
    assert depth == 2, "layer schedule below is written for one even and one odd layer"
    n_lat, n_ctx = x.shape[1], ctx.shape[1]
    t = n_lat + n_ctx
    both = ((0, n_lat), (n_lat, n_ctx))
    xs = jnp.concatenate([x, ctx], axis=1)

    nb = x.shape[0]
    w_down = ffn_w_down.astype(BF16)
    mods0 = _ada_mod(c, c_ctx, ada_w, ada_b, 0)
    mods1 = _ada_mod(c, c_ctx, ada_w, ada_b, 1)
    norm = lambda layer, sub, mods: (norm_w[layer, sub][None, :], _mod_pair(mods, 3 * sub, 3 * sub + 1, nb))

    w0, mod0 = norm(0, 0, mods0)
    u = _rmsmod(xs, w0, mod0, tm=_row_tile(t), rows=t, n_lat=n_lat)
    xs, u = _even_layer(xs, u, mods0, ev_w_in, ev_conv_w[0], ev_conv_b[0], ssd_dt_bias[0], ssd_a_log[0],
                        ssd_d[0], ssd_norm_w[0], hy_w1[0], hy_b1[0], hy_w2[0], hy_b2[0], hy_w3[0], hy_freq[0],
                        hy_bias[0], ev_w_out.astype(BF16), 0, n_lat=n_lat, n_ctx=n_ctx, next_norm=norm(0, 1, mods0))
    xs, u = _conv_ffn_block(xs, u, mods0, ffn_w_up, ffn_conv_w[0], ffn_conv_b[0], w_down, 0,
                            rows=t, n_lat=n_lat, segs=both, next_norm=norm(1, 0, mods1))
    xl, u = _odd_layer_latent(xs, u, mods1, od_w_in[0], ml_conv_w[0], ml_conv_b[0], ml_gate_b[0], ml_norm_w[0],
                              na_q_norm_w[0], na_k_norm_w[0], na_rpb[0], od_w_out.astype(BF16), 0,
                              n_lat=n_lat, n_ctx=n_ctx, next_norm=norm(1, 1, mods1))
    out, _ = _conv_ffn_block(xl, u, mods1, ffn_w_up, ffn_conv_w[1], ffn_conv_b[1], w_down, 1,
                             rows=n_lat, n_lat=n_lat, segs=((0, n_lat),), next_norm=None)
    return out
```

```python
import functools
import math

import numpy as np
import jax
import jax.numpy as jnp
from jax import lax
from jax.experimental import pallas as pl
from jax.experimental.pallas import tpu as pltpu

F32 = jnp.float32
BF16 = jnp.bfloat16
HI = lax.Precision.HIGHEST

D_MODEL = 2048
SEQ = 4096
CTX_LEN = 256
GRID_W = 64
EPS = 1e-6
CONV_W = 3
SSD_WIDTH = D_MODEL
SSD_HEAD_DIM = 64
SSD_HEADS = SSD_WIDTH // SSD_HEAD_DIM
SSD_STATE = 128
SSD_GROUPS = 4
SSD_HPG = SSD_HEADS // SSD_GROUPS
SSD_GW = SSD_WIDTH // SSD_GROUPS
SSD_XBC = SSD_WIDTH + 2 * SSD_GROUPS * SSD_STATE
HY_WIDTH = D_MODEL
HY_EMB = 33
HY_BANDS = (HY_EMB - 1) // 2
HY_ORDER = 64
HY_FAST = 0.3
HY_SLOW = 1.5
HY_TARGET = 1e-2
ML_HEADS = 8
ML_WIDTH = D_MODEL
ML_V_DIM = ML_WIDTH // ML_HEADS
ML_QK_DIM = ML_V_DIM // 2
ML_QK_WIDTH = ML_HEADS * ML_QK_DIM
ROPE_THETA = 10000.0
NA_WIDTH = D_MODEL
NA_HEAD_DIM = 128
NA_HEADS = NA_WIDTH // NA_HEAD_DIM
NA_ROWS = 8
NA_COLS = 16
NA_QROWS = 8
NA_KROWS = 16
NA_PARTS = 2
NA_HPS = 2
NA_PREP_STEPS = 8
NA_TAB_ROWS = 4
NA_PART_KROWS = 12
D_FF = 256 * ((8 * D_MODEL // 3 + 255) // 256)
EV_CONV_CH = SSD_XBC + 3 * HY_WIDTH
SCAN_Q = 128
SCAN_CPS = 2
LANES = 128
SUBLANES = 8
VMEM_LIMIT = 48 * 1024 * 1024
VMEM_LIMIT_BIG = 58 * 1024 * 1024


def _cp(sem, vmem=VMEM_LIMIT):
    return pltpu.CompilerParams(dimension_semantics=sem, vmem_limit_bytes=vmem)


def _sigmoid(x):
    return 0.5 * jnp.tanh(0.5 * x) + 0.5


def _silu(x):
    return x * _sigmoid(x)


def _softplus(x):
    return jnp.maximum(x, 0.0) + jnp.log(1.0 + jnp.exp(-jnp.abs(x)))


def _log_sigmoid(x):
    return -_softplus(-x)


def _dot(a, b, precision=None):
    return jnp.dot(a, b, preferred_element_type=F32, precision=precision)


def _dot_nt(a, b):
    return lax.dot_general(a, b, (((1,), (1,)), ((), ())), preferred_element_type=F32)


def _split_bf16(x, terms):
    out = []
    for _ in range(terms):
        part = x.astype(BF16)
        out.append(part)
        x = x - part.astype(F32)
    return out


def _dot_sel(x, sel, terms):
    selb = sel.astype(BF16)
    return sum(_dot(part, selb) for part in _split_bf16(x, terms))


def _sel_dot(sel, x, terms):
    selb = sel.astype(BF16)
    return sum(_dot(selb, part) for part in _split_bf16(x, terms))


def _mm_body(*refs, nk, n_extra, epilogue, cache_w, n_out, batch_axis):
    a_ref, w_ref = refs[0], refs[1]
    extra = refs[2:2 + n_extra]
    o_refs = refs[2 + n_extra:2 + n_extra + n_out]
    scratch = refs[2 + n_extra + n_out:]
    row_blk = pl.program_id(3 - batch_axis)
    if cache_w:
        wb_ref = scratch[0]

        @pl.when((pl.program_id(batch_axis) == 0) & (row_blk == 0))
        def _():
            wb_ref[...] = w_ref[...].astype(BF16)

        w_val = wb_ref[...]
    else:
        w_val = w_ref[...].astype(BF16)
    part = _dot(a_ref[...].astype(BF16), w_val)

    def finish(acc):
        vals = acc if epilogue is None else epilogue(acc, row_blk, *extra)
        for o_ref, val in zip(o_refs, vals if n_out > 1 else (vals,)):
            o_ref[...] = val.astype(o_ref.dtype)

    if nk == 1:
        finish(part)
        return
    acc_ref = scratch[0]
    k = pl.program_id(3)

    @pl.when(k == 0)
    def _():
        acc_ref[...] = part

    @pl.when(k > 0)
    def _():
        acc_ref[...] += part

    @pl.when(k == nk - 1)
    def _():
        finish(acc_ref[...])


def _matmul(a, w, *, tm, tn, tk, out_dtype, rows=None, cols=None, w_index=None, extra=(), epilogue=None,
            n_out=1, batch_inner=False, vmem=VMEM_LIMIT, name="mm"):
    ba, ta, kdim = a.shape
    bw, kdim2, n = w.shape
    assert kdim == kdim2
    nb = ba if w_index is not None else max(ba, bw)
    rows = ta if rows is None else rows
    n = n if cols is None else cols
    assert rows % tm == 0 and n % tn == 0 and kdim % tk == 0
    ni, nj, nk = rows // tm, n // tn, kdim // tk
    order = (lambda f: (lambda j, i, b, k: f(j, b, i, k))) if batch_inner else (lambda f: f)
    respec = lambda spec: pl.BlockSpec(spec.block_shape, order(spec.index_map))
    a_map = (lambda j, b, i, k: (b, i, k)) if ba > 1 else (lambda j, b, i, k: (0, i, k))
    if w_index is not None:
        w_map = lambda j, b, i, k: (w_index, k, j)
    else:
        w_map = (lambda j, b, i, k: (b, k, j)) if bw > 1 else (lambda j, b, i, k: (0, k, j))
    cache_w = w.dtype != BF16 and nk == 1 and (w_index is not None or bw == 1)
    in_specs = [pl.BlockSpec((None, tm, tk), order(a_map)), pl.BlockSpec((None, tk, tn), order(w_map))]
    args = [a, w]
    for arr, spec in extra:
        args.append(arr)
        in_specs.append(respec(spec))
    out_shape = jax.ShapeDtypeStruct((nb, rows, n), out_dtype)
    out_spec = pl.BlockSpec((None, tm, tn), order(lambda j, b, i, k: (b, i, j)))
    if n_out > 1:
        out_shape, out_spec = (out_shape,) * n_out, (out_spec,) * n_out
    scratch = [pltpu.VMEM((tm, tn), F32)] if nk > 1 else []
    if cache_w:
        scratch = [pltpu.VMEM((tk, tn), BF16)]
    body = functools.partial(_mm_body, nk=nk, n_extra=len(extra), epilogue=epilogue, cache_w=cache_w, n_out=n_out,
                             batch_axis=2 if batch_inner else 1)
    inner = "arbitrary" if cache_w else "parallel"
    grid = (nj, ni, nb, nk) if batch_inner else (nj, nb, ni, nk)
    return pl.pallas_call(
        body, out_shape=out_shape, grid=grid, in_specs=in_specs, out_specs=out_spec,
        scratch_shapes=scratch, name=name,
        compiler_params=_cp(("parallel", inner, inner, "arbitrary"), vmem))(*args)


def _row_is_lat(row_blk, tm, n_lat):
    rows = row_blk * tm + lax.broadcasted_iota(jnp.int32, (tm, 1), 0)
    return rows < n_lat


def _gated_residual_epilogue(acc, row_blk, res_ref, gate_ref, *, tm, n_lat):
    gate = jnp.where(_row_is_lat(row_blk, tm, n_lat), gate_ref[0], gate_ref[1])
    return res_ref[...] + gate * acc


def _gated_residual_rows_body(*refs, ks, tm, n_lat):
    n_a = len(ks)
    a_refs, w_ref, res_ref, gate_ref, o_ref = refs[:n_a], refs[n_a], refs[n_a + 1], refs[n_a + 2], refs[n_a + 3]
    acc, lo = None, 0
    for a_ref, kk in zip(a_refs, ks):
        part = _dot(a_ref[...], w_ref[lo:lo + kk, :])
        acc = part if acc is None else acc + part
        lo += kk
    o_ref[...] = _gated_residual_epilogue(acc, pl.program_id(1), res_ref, gate_ref, tm=tm, n_lat=n_lat)


def _matmul_gated_residual_rows(a_list, w, res, gate, *, w_index, tm, tn, rows, n_lat, name):
    nb = a_list[0].shape[0]
    _, kdim, n = w.shape
    ks = tuple(x.shape[2] for x in a_list)
    assert sum(ks) == kdim and rows % tm == 0 and n % tn == 0
    in_specs = [pl.BlockSpec((None, tm, kk), lambda b, i, j: (b, i, 0)) for kk in ks]
    in_specs += [pl.BlockSpec((None, kdim, tn), lambda b, i, j: (w_index, 0, j)),
                 pl.BlockSpec((None, tm, tn), lambda b, i, j: (b, i, j)),
                 pl.BlockSpec((None, 2, 1, tn), lambda b, i, j: (b, 0, 0, j))]
    return pl.pallas_call(
        functools.partial(_gated_residual_rows_body, ks=ks, tm=tm, n_lat=n_lat),
        out_shape=jax.ShapeDtypeStruct((nb, rows, n), F32), grid=(nb, rows // tm, n // tn), in_specs=in_specs,
        out_specs=pl.BlockSpec((None, tm, tn), lambda b, i, j: (b, i, j)), name=name,
        compiler_params=_cp(("parallel", "parallel", "parallel")))(*a_list, w, res, gate)


def _modulated_rmsnorm(x, is_lat, w_ref, mod_ref):
    y = x * lax.rsqrt(jnp.mean(x * x, axis=-1, keepdims=True) + EPS) * w_ref[...]
    shift = jnp.where(is_lat, mod_ref[0, 0], mod_ref[0, 1])
    scale = jnp.where(is_lat, mod_ref[1, 0], mod_ref[1, 1])
    return y * (1.0 + scale) + shift


def _rmsmod_body(x_ref, w_ref, mod_ref, o_ref, *, tm, n_lat):
    is_lat = _row_is_lat(pl.program_id(2), tm, n_lat)
    o_ref[...] = _modulated_rmsnorm(x_ref[...], is_lat, w_ref, mod_ref).astype(o_ref.dtype)


def _rmsmod(x, w, mod, *, tm, rows, n_lat):
    nb, _, d = x.shape
    tm = tm // 2
    assert rows % tm == 0 and tm % 16 == 0
    body = functools.partial(_rmsmod_body, tm=tm, n_lat=n_lat)
    return pl.pallas_call(
        body, out_shape=jax.ShapeDtypeStruct((nb, rows, d), BF16), grid=(1, nb, rows // tm),
        in_specs=[pl.BlockSpec((None, tm, d), lambda j, b, i: (b, i, 0)),
                  pl.BlockSpec((1, d), lambda j, b, i: (0, 0)),
                  pl.BlockSpec((None, 2, 2, 1, d), lambda j, b, i: (b, 0, 0, 0, 0))],
        out_specs=pl.BlockSpec((None, tm, d), lambda j, b, i: (b, i, 0)),
        name="rmsmod", compiler_params=_cp(("parallel", "parallel", "parallel")))(x, w, mod)


CONV_CH = 128
CONV_HALO = 16
CONV_STEPS = 4


def _shift_matrices():
    r = lax.broadcasted_iota(jnp.int32, (CONV_CH, CONV_CH + 2 * CONV_HALO), 0)
    c = lax.broadcasted_iota(jnp.int32, (CONV_CH, CONV_CH + 2 * CONV_HALO), 1)
    return (c == r + CONV_HALO - 1).astype(BF16), (c == r + CONV_HALO + 1).astype(BF16)


def _chunk_rows(r0, i):
    return pl.ds(r0 + pl.multiple_of(i * CONV_CH, CONV_CH), CONV_CH)


def _taps(w_ref, b_ref, scale=1.0):
    return w_ref[...] * scale, b_ref[...] * scale


def _silu_of_half(h):
    return h + h * jnp.tanh(h)


def _conv_at(x_ref, taps, shifts, r0, n, steps):
    w, b = taps
    cur = [x_ref[_chunk_rows(r0, i), :] for i in steps]
    zero = jnp.zeros((CONV_HALO, cur[0].shape[1]), cur[0].dtype)
    lo = [pl.multiple_of(jnp.maximum(i * CONV_CH - CONV_HALO, 0), CONV_HALO) for i in steps]
    hi = [pl.multiple_of(jnp.minimum((i + 1) * CONV_CH, n - CONV_HALO), CONV_HALO) for i in steps]
    tail = [jnp.where(i > 0, x_ref[pl.ds(r0 + l, CONV_HALO), :], zero) for i, l in zip(steps, lo)]
    head = [jnp.where(i < n // CONV_CH - 1, x_ref[pl.ds(r0 + h, CONV_HALO), :], zero) for i, h in zip(steps, hi)]
    win = [jnp.concatenate([t, c, h], axis=0) for t, c, h in zip(tail, cur, head)]
    prev = [_dot(shifts[0], w) for w in win]
    nxt = [_dot(shifts[1], w) for w in win]
    return [w[0:1, :] * p + w[1:2, :] * c.astype(F32) + w[2:3, :] * x + b for p, c, x in zip(prev, cur, nxt)]


def _conv_loop(segs, fn):
    for si, (r0, n) in enumerate(segs):
        def body(i, carry, r0=r0, n=n, si=si):
            fn(si, r0, n, [i])
            return carry

        lax.fori_loop(0, n // CONV_CH, body, 0, unroll=min(CONV_STEPS, n // CONV_CH))


def _conv_silu_body(x_ref, w_ref, b_ref, o_ref, *, segs):
    shifts = _shift_matrices()
    half_taps = _taps(w_ref, b_ref, 0.5)

    def fn(si, r0, n, steps):
        ys = [_silu_of_half(h) for h in _conv_at(x_ref, half_taps, shifts, r0, n, steps)]
        for i, y in zip(steps, ys):
            o_ref[_chunk_rows(r0, i), :] = y.astype(o_ref.dtype)

    _conv_loop(segs, fn)


def _conv_rope_body(x_ref, w_ref, b_ref, cos_ref, sin_ref, o_ref, *, segs, tc):
    shifts = _shift_matrices()
    lane = lax.broadcasted_iota(jnp.int32, (CONV_CH, tc), 1)
    first = (lane % (ML_QK_DIM // 2)) < (ML_QK_DIM // 4)
    nf = ML_QK_DIM // 4
    half_taps = _taps(w_ref, b_ref, 0.5)

    def fn(si, r0, n, steps):
        ys = [_silu_of_half(h) for h in _conv_at(x_ref, half_taps, shifts, r0, n, steps)]
        partner = [jnp.where(first, pltpu.roll(y, tc - nf, 1), pltpu.roll(y, nf, 1)) for y in ys]
        for i, y, p in zip(steps, ys, partner):
            rows = _chunk_rows(r0, i)
            o_ref[rows, :] = (y * cos_ref[rows, :] + p * sin_ref[rows, :]).astype(o_ref.dtype)

    _conv_loop(segs, fn)


def _conv_ffn_body(a_ref, x_ref, w_ref, b_ref, o_ref, *, segs):
    shifts = _shift_matrices()
    half_taps = _taps(w_ref, b_ref, 0.5)

    def fn(si, r0, n, steps):
        gs = [_silu_of_half(h) for h in _conv_at(x_ref, half_taps, shifts, r0, n, steps)]
        for i, g in zip(steps, gs):
            rows = _chunk_rows(r0, i)
            o_ref[rows, :] = (a_ref[rows, :].astype(F32) * g).astype(o_ref.dtype)

    _conv_loop(segs, fn)


def _conv_hyena_body(x0_ref, x1_ref, v_ref, w0_ref, b0_ref, w1_ref, b1_ref, w2_ref, b2_ref,
                     ox0_ref, ozl_ref, ozc_ref, ozle_ref, ozlo_ref, ozce_ref, ozco_ref, *zs_refs, segs):
    shifts = _shift_matrices()
    taps = (_taps(w0_ref, b0_ref), _taps(w1_ref, b1_ref), _taps(w2_ref, b2_ref))
    half = CONV_CH // 2
    outs = ((ozl_ref, ozle_ref, ozlo_ref), (ozc_ref, ozce_ref, ozco_ref))

    def fn(si, r0, n, steps):
        x0 = _conv_at(x0_ref, taps[0], shifts, r0, n, steps)
        x1 = _conv_at(x1_ref, taps[1], shifts, r0, n, steps)
        v = _conv_at(v_ref, taps[2], shifts, r0, n, steps)
        nat_ref, even_ref, odd_ref = outs[si]
        for i, x0_i, x1_i, v_i in zip(steps, x0, x1, v):
            ox0_ref[_chunk_rows(r0, i), :] = x0_i.astype(ox0_ref.dtype)
            zin = x1_i * v_i
            nat_ref[_chunk_rows(0, i), :] = zin.astype(BF16)
            t0 = r0 + pl.multiple_of(i * CONV_CH, CONV_CH)
            hrows = pl.ds(pl.multiple_of(i * half, half), half)
            for g, zs_ref in enumerate(zs_refs):
                lanes = slice(g * LANES, (g + 1) * LANES)
                zs_ref[pl.ds(t0, CONV_CH), :] = zin[:, lanes]
                even_ref[hrows, lanes] = zs_ref[pl.ds(t0, half, stride=2), :].astype(BF16)
                odd_ref[hrows, lanes] = zs_ref[pl.ds(t0 + 1, half, stride=2), :].astype(BF16)

    _conv_loop(segs, fn)


def _conv_specs(t_in, tc, xcol_blk, wcol_blk):
    return [pl.BlockSpec((None, t_in, tc), lambda b, j: (b, 0, j + xcol_blk)),
            pl.BlockSpec((CONV_W, tc), lambda b, j: (0, j + wcol_blk)),
            pl.BlockSpec((1, tc), lambda b, j: (0, j + wcol_blk))]


def _conv_silu(x, w, b, *, xcol, wcol, width, segs, tc=256):
    nb, t_in, _ = x.shape
    return pl.pallas_call(
        functools.partial(_conv_silu_body, segs=segs),
        out_shape=jax.ShapeDtypeStruct((nb, t_in, width), BF16), grid=(nb, width // tc),
        in_specs=_conv_specs(t_in, tc, xcol // tc, wcol // tc),
        out_specs=pl.BlockSpec((None, t_in, tc), lambda b, j: (b, 0, j)),
        name="conv_silu", compiler_params=_cp(("parallel", "parallel")))(x, w, b)


def _conv_rope(x, w, b, cos, sin, *, width, segs, tc=256):
    nb, t_in, _ = x.shape
    tab = pl.BlockSpec((t_in, tc), lambda b, j: (0, 0))
    return pl.pallas_call(
        functools.partial(_conv_rope_body, segs=segs, tc=tc),
        out_shape=jax.ShapeDtypeStruct((nb, t_in, width), BF16), grid=(nb, width // tc),
        in_specs=_conv_specs(t_in, tc, 0, 0) + [tab, tab],
        out_specs=pl.BlockSpec((None, t_in, tc), lambda b, j: (b, 0, j)),
        name="conv_rope", compiler_params=_cp(("parallel", "parallel")))(x, w, b, cos, sin)


def _conv_ffn(up, w, b, *, segs, rows, tc=256):
    nb, t_in, two_ff = up.shape
    ff = two_ff // 2
    return pl.pallas_call(
        functools.partial(_conv_ffn_body, segs=segs),
        out_shape=jax.ShapeDtypeStruct((nb, rows, ff), BF16), grid=(nb, ff // tc),
        in_specs=[pl.BlockSpec((None, rows, tc), lambda b, j: (b, 0, j))] + _conv_specs(rows, tc, ff // tc, 0),
        out_specs=pl.BlockSpec((None, rows, tc), lambda b, j: (b, 0, j)),
        name="conv_ffn", compiler_params=_cp(("parallel", "parallel")))(up, up, w, b)


def _conv_hyena(pr, w, b, *, xcol, wcol, segs, tc=256):
    nb, t_in, _ = pr.shape
    (_, n_lat), (_, n_ctx) = segs
    wd = HY_WIDTH
    out_rows = (t_in, n_lat, n_ctx, n_lat // 2, n_lat // 2, n_ctx // 2, n_ctx // 2)
    in_specs = [pl.BlockSpec((None, t_in, tc), functools.partial(lambda b, j, o: (b, 0, j + o), o=(xcol + m * wd) // tc))
                for m in range(3)]
    wargs = []
    for m in range(3):
        off = (wcol + m * wd) // tc
        in_specs.append(pl.BlockSpec((CONV_W, tc), functools.partial(lambda b, j, o: (0, j + o), o=off)))
        in_specs.append(pl.BlockSpec((1, tc), functools.partial(lambda b, j, o: (0, j + o), o=off)))
        wargs += [w, b]
    return pl.pallas_call(
        functools.partial(_conv_hyena_body, segs=segs),
        out_shape=tuple(jax.ShapeDtypeStruct((nb, rows, wd), BF16) for rows in out_rows),
        grid=(nb, wd // tc), in_specs=in_specs,
        out_specs=tuple(pl.BlockSpec((None, rows, tc), lambda b, j: (b, 0, j)) for rows in out_rows),
        scratch_shapes=[pltpu.VMEM((t_in, LANES), F32)] * (tc // LANES),
        name="conv_hyena", compiler_params=_cp(("parallel", "parallel")))(pr, pr, pr, *wargs)


def _scan_masks(reverse):
    q = SCAN_Q
    ii = lax.broadcasted_iota(jnp.int32, (q, q), 0)
    jj = lax.broadcasted_iota(jnp.int32, (q, q), 1)
    mask = (jj >= ii) if reverse else (jj <= ii)
    mask_t = (jj <= ii) if reverse else (jj >= ii)
    return mask, mask.astype(F32), mask_t.astype(F32)


def _chunk_map(reverse, n_chunks, n_lat_chunks):
    if reverse:
        return lambda k: n_chunks - 1 - k
    return lambda k: (k + n_lat_chunks) % n_chunks


SSD_GPS = 4


def _ssd_body(*refs, reverse, final):
    (xs_ref, b_ref, c_ref, dtc_ref, dtr_ref, dbc_ref, dbr_ref, alc_ref, alr_ref) = refs[:9]
    if final:
        yf_ref, z_ref, dsk_ref, nw_ref, o_ref, h_ref = refs[9:]
    else:
        o_ref, h_ref = refs[9:]

    @pl.when(pl.program_id(2) == 0)
    def _():
        h_ref[...] = jnp.zeros_like(h_ref)

    groups = range(SSD_GPS)
    h = [h_ref[g] for g in groups]
    for c in (reversed(range(SCAN_CPS)) if reverse else range(SCAN_CPS)):
        h = _ssd_chunk(refs, slice(c * SCAN_Q, (c + 1) * SCAN_Q), h, reverse=reverse, final=final)
    for g in groups:
        h_ref[g] = h[g]


def _ssd_chunk(refs, rs, h_prev, *, reverse, final):
    (xs_ref, b_ref, c_ref, dtc_ref, dtr_ref, dbc_ref, dbr_ref, alc_ref, alr_ref) = refs[:9]
    if final:
        yf_ref, z_ref, dsk_ref, nw_ref, o_ref = refs[9:14]
    else:
        o_ref = refs[9]
    q, gw, hd = SCAN_Q, SSD_GW, SSD_HEAD_DIM
    mask, maskf, mask_tf = _scan_masks(reverse)
    groups = range(SSD_GPS)
    pairs = range(SSD_HPG // 2)
    cols = [slice(g * gw, (g + 1) * gw) for g in groups]
    xs = [xs_ref[rs, cols[g]].astype(F32) for g in groups]
    bm = [b_ref[rs, g * SSD_STATE:(g + 1) * SSD_STATE] for g in groups]
    cm = [c_ref[rs, g * SSD_STATE:(g + 1) * SSD_STATE] for g in groups]
    dt_c = [_softplus(dtc_ref[g, rs, :] + dbc_ref[g]) for g in groups]
    dt_r = [_softplus(dtr_ref[g, :, rs] + dbr_ref[g]) for g in groups]
    da_c = [dt_c[g] * (-jnp.exp(alc_ref[g])) for g in groups]
    da_r = [dt_r[g] * (-jnp.exp(alr_ref[g])) for g in groups]
    acum_c = [_sel_dot(maskf, da_c[g], 3) for g in groups]
    acum_r = [_dot_sel(da_r[g], mask_tf, 3) for g in groups]
    atot = [jnp.sum(da_c[g], axis=0, keepdims=True) for g in groups]

    er = lax.broadcasted_iota(jnp.int32, (SSD_HPG, gw), 0)
    ec = lax.broadcasted_iota(jnp.int32, (SSD_HPG, gw), 1)
    expand = (ec // hd == er).astype(F32)
    wide = [_dot_sel(jnp.concatenate([dt_c[g], jnp.exp(atot[g] - acum_c[g]), jnp.exp(acum_c[g]),
                                      jnp.broadcast_to(jnp.exp(atot[g]), (SUBLANES, SSD_HPG))], axis=0), expand, 1)
            for g in groups]
    xq = [xs[g] * wide[g][0:q] for g in groups]
    cb = [_dot_nt(cm[g], bm[g]) for g in groups]
    y_off = [wide[g][2 * q:3 * q] * _dot(cm[g], h_prev[g].astype(BF16)) for g in groups]
    lane = lax.broadcasted_iota(jnp.int32, (q, 2 * hd), 1)

    def pair_lhs(g, p):
        ms = [(cb[g] * jnp.where(mask, jnp.exp(acum_c[g][:, r:r + 1] - acum_r[g][r:r + 1, :]), 0.0)).astype(BF16)
              for r in (2 * p, 2 * p + 1)]
        return jnp.concatenate(ms, axis=1)

    def pair_rhs(g, p):
        xp = xq[g][:, 2 * hd * p:2 * hd * (p + 1)]
        return jnp.concatenate([jnp.where(lane < hd, xp, 0.0), jnp.where(lane >= hd, xp, 0.0)], axis=0).astype(BF16)

    y_in = [[_dot(pair_lhs(g, p), pair_rhs(g, p)) for p in pairs] for g in groups]
    y = [y_off[g] + jnp.concatenate(y_in[g], axis=1) for g in groups]
    bt = [bm[g].astype(F32).T.astype(BF16) for g in groups]
    upd = [_dot(bt[g], (xq[g] * wide[g][q:2 * q]).astype(BF16)) for g in groups]
    for g in groups:
        if final:
            ytot = y[g] + yf_ref[rs, cols[g]] + xs[g] * dsk_ref[:, cols[g]]
            gated = ytot * _silu(z_ref[rs, cols[g]].astype(F32))
            o = gated * lax.rsqrt(jnp.mean(gated * gated, axis=-1, keepdims=True) + EPS) * nw_ref[:, cols[g]]
            o_ref[rs, cols[g]] = o.astype(o_ref.dtype)
        else:
            o_ref[rs, cols[g]] = y[g]
    return [wide[g][3 * q:3 * q + 1] * h_prev[g] + upd[g] for g in groups]


def _ssd(xbc, dt_c, dt_r, db_c, db_r, al_c, al_r, *, direction, n_lat, final_args=None):
    nb, t, _ = xbc.shape
    q, gw, hpg = SCAN_CPS * SCAN_Q, SSD_GW, SSD_HPG
    assert n_lat % q == 0 and t % q == 0
    nch = t // q
    reverse = direction == 1
    cmap = _chunk_map(reverse, nch, n_lat // q)
    gps = SSD_GPS
    sw, gww = gps * SSD_STATE, gps * gw
    b0 = SSD_WIDTH // sw
    c0 = b0 + SSD_GROUPS // gps
    d = direction
    in_specs = [
        pl.BlockSpec((None, q, gww), lambda b, g, k: (b, cmap(k), g)),
        pl.BlockSpec((None, q, sw), lambda b, g, k: (b, cmap(k), b0 + g)),
        pl.BlockSpec((None, q, sw), lambda b, g, k: (b, cmap(k), c0 + g)),
        pl.BlockSpec((None, None, gps, q, hpg), lambda b, g, k: (b, d, g, cmap(k), 0)),
        pl.BlockSpec((None, None, gps, hpg, q), lambda b, g, k: (b, d, g, 0, cmap(k))),
        pl.BlockSpec((None, gps, 1, hpg), lambda b, g, k: (d, g, 0, 0)),
        pl.BlockSpec((None, gps, hpg, 1), lambda b, g, k: (d, g, 0, 0)),
        pl.BlockSpec((None, gps, 1, hpg), lambda b, g, k: (d, g, 0, 0)),
        pl.BlockSpec((None, gps, hpg, 1), lambda b, g, k: (d, g, 0, 0)),
    ]
    args = [xbc, xbc, xbc, dt_c, dt_r, db_c, db_r, al_c, al_r]
    final = final_args is not None
    if final:
        yf, pr, dskip, norm_w = final_args
        in_specs += [pl.BlockSpec((None, q, gww), lambda b, g, k: (b, cmap(k), g)),
                     pl.BlockSpec((None, q, gww), lambda b, g, k: (b, cmap(k), g)),
                     pl.BlockSpec((1, gww), lambda b, g, k: (0, g)),
                     pl.BlockSpec((1, gww), lambda b, g, k: (0, g))]
        args += [yf, pr, dskip, norm_w]
    out_shape = jax.ShapeDtypeStruct((nb, t, SSD_WIDTH), BF16 if final else F32)
    return pl.pallas_call(
        functools.partial(_ssd_body, reverse=reverse, final=final),
        out_shape=out_shape, grid=(nb, SSD_GROUPS // gps, nch), in_specs=in_specs,
        out_specs=pl.BlockSpec((None, q, gww), lambda b, g, k: (b, cmap(k), g)),
        scratch_shapes=[pltpu.VMEM((gps, SSD_STATE, gw), F32)],
        name="ssd_bwd" if reverse else "ssd_fwd",
        compiler_params=_cp(("parallel", "parallel", "arbitrary")))(*args)


def _hyfilt_body(f_ref, w1_ref, b1_ref, w2_ref, b2_ref, w3_ref, fr_ref, dl_ref, o_ref, *, tl):
    f = f_ref[...]
    h = jnp.sin(fr_ref[0:1, :] * (_dot(f, w1_ref[...], HI) + b1_ref[...]))
    h = jnp.sin(fr_ref[1:2, :] * (_dot(h, w2_ref[...], HI) + b2_ref[...]))
    h = _dot(h, w3_ref[...], HI)
    dec = jnp.exp(-f[:, 0:1] * dl_ref[...])
    row = pl.program_id(0) * tl + lax.broadcasted_iota(jnp.int32, (tl, 1), 0)
    wd = HY_WIDTH
    o_ref[:, 0:wd] = (h[:, 0:wd] * dec).astype(o_ref.dtype)
    o_ref[:, wd:2 * wd] = jnp.where(row == 0, 0.0, h[:, wd:2 * wd] * dec).astype(o_ref.dtype)


def _hyena_filters(length, w1, b1, w2, b2, w3, freq, *, tl=256):
    p = LANES
    t = jnp.linspace(0.0, 1.0, length, dtype=F32)[:, None]
    w = 2.0 * math.pi * jnp.arange(length, dtype=F32)[:, None] / length
    f = jnp.linspace(1e-4, HY_BANDS - 1, HY_BANDS, dtype=F32)[None, :]
    feats = jnp.concatenate([t, jnp.cos(f * w), -jnp.sin(f * w)], axis=-1)
    feats = jnp.concatenate([feats[0::2], feats[1::2]], axis=0)
    feats = jnp.pad(feats, ((0, 0), (0, p - HY_EMB)))
    po = p - HY_ORDER
    w1p = jnp.pad(w1, ((0, p - HY_EMB), (0, po)))
    w2p = jnp.pad(w2, ((0, po), (0, po)))
    w3p = jnp.pad(w3, ((0, po), (0, 0)))
    b1p = jnp.pad(b1[None, :], ((0, 0), (0, po)))
    b2p = jnp.pad(b2[None, :], ((0, 0), (0, po)))
    frp = jnp.pad(freq, ((0, 0), (0, po)))
    deltas = jnp.abs(jnp.linspace(math.log(HY_TARGET) / HY_FAST, math.log(HY_TARGET) / HY_SLOW, HY_WIDTH,
                                  dtype=F32))[None, :]
    full = lambda shape: pl.BlockSpec(shape, lambda i: (0,) * len(shape))
    out = pl.pallas_call(
        functools.partial(_hyfilt_body, tl=tl),
        out_shape=jax.ShapeDtypeStruct((length, 2 * HY_WIDTH), BF16), grid=(length // tl,),
        in_specs=[pl.BlockSpec((tl, p), lambda i: (i, 0)), full((p, p)), full((1, p)), full((p, p)), full((1, p)),
                  full((p, 2 * HY_WIDTH)), full((2, p)), full((1, HY_WIDTH))],
        out_specs=pl.BlockSpec((tl, 2 * HY_WIDTH), lambda i: (i, 0)),
        name="hyena_filters", compiler_params=_cp(("parallel",)))(feats, w1p, b1p, w2p, b2p, w3p, frp, deltas)
    return out[None]


def _dft_matrices(length):
    n = 2 * length
    k = jnp.arange(length, dtype=jnp.int32)[:, None]
    t = jnp.arange(length, dtype=jnp.int32)[None, :]
    step = 1 << (int(math.log2(length)) // 2)
    part = jnp.arange(length // step, dtype=jnp.int32)[:, None]
    ang_hi = ((part * step * t) % n).astype(F32) * (2.0 * math.pi / n)
    ang_lo = ((jnp.arange(step, dtype=jnp.int32)[:, None] * t) % n).astype(F32) * (2.0 * math.pi / n)
    ch, sh = jnp.cos(ang_hi)[:, None, :], jnp.sin(ang_hi)[:, None, :]
    cl, sl = jnp.cos(ang_lo)[None, :, :], jnp.sin(ang_lo)[None, :, :]
    c = (ch * cl - sh * sl).reshape(length, length)
    s = (sh * cl + ch * sl).reshape(length, length)
    alt_t = jnp.where(t % 2 == 0, 1.0, -1.0).astype(F32)
    fwd = jnp.concatenate([c, jnp.where(k == 0, alt_t, -s)], axis=0)
    alt_r = jnp.where(k % 2 == 0, 1.0, -1.0).astype(F32)
    inv_re = jnp.where(t == 0, 1.0 / n, (2.0 / n) * c)
    inv_im = jnp.where(t == 0, alt_r / n, (-2.0 / n) * s)
    inv = jnp.concatenate([inv_re, inv_im], axis=1)
    return fwd.astype(BF16)[None], inv.astype(BF16)[None]


def _tile_interleave(x, axis, hm):
    shp = x.shape
    n = shp[axis] // 2
    x = x.reshape(shp[:axis] + (2, n // hm, hm) + shp[axis + 1:])
    return jnp.swapaxes(x, axis, axis + 1).reshape(shp)


def _twiddle_table(length, hm):
    k = np.arange(length // 2, dtype=np.float64)[:, None] * (np.pi / length)
    tab = np.concatenate([np.cos(k), np.sin(k)], axis=0)
    tab = tab.reshape(2, -1, hm, 1).swapaxes(0, 1).reshape(length, 1)
    return jnp.asarray(np.broadcast_to(tab, (length, LANES)).astype(np.float32))


def _lanes(x, width):
    return jnp.tile(x, (1, width // LANES))


def _parity_banks(e, ot, tw, hm):
    er, ei, otr, oti = e[0:hm], e[hm:], ot[0:hm], ot[hm:]
    c, s = _lanes(tw[0:hm], e.shape[1]), _lanes(tw[hm:], e.shape[1])
    orr = otr * c + oti * s
    oi = oti * c - otr * s
    return (er + orr, ei + oi, er - orr, ei - oi), (c, s)


def _is_bin0(row_blk, hm):
    return (row_blk == 0) & (lax.broadcasted_iota(jnp.int32, (hm, 1), 0) == 0)


def _hy_kbank_body(ef_ref, eb_ref, of_ref, ob_ref, tw_ref, ka_ref, kb_ref, *, hm):
    tw = tw_ref[...]
    (afr, afi, bfr, bfi), _ = _parity_banks(ef_ref[...], of_ref[...], tw, hm)
    (abr, abi, bbr, bbi), _ = _parity_banks(eb_ref[...], ob_ref[...], tw, hm)
    first = _is_bin0(pl.program_id(0), hm)
    kai = jnp.where(first, ef_ref[hm:, :] + eb_ref[hm:, :], afi - abi)
    kbi = jnp.where(first, ob_ref[hm:, :] - of_ref[hm:, :], bfi - bbi)
    ka_ref[...] = jnp.concatenate([afr + abr, kai], axis=0)
    kb_ref[...] = jnp.concatenate([bfr + bbr, kbi], axis=0)


def _hy_kbank(ef, of, tw, *, tm, tc=512):
    _, ln, c2 = ef.shape
    c = c2 // 2
    fspec = pl.BlockSpec((None, tm, tc), lambda i, j: (0, i, j))
    bspec = pl.BlockSpec((None, tm, tc), lambda i, j: (0, i, j + c // tc))
    return pl.pallas_call(
        functools.partial(_hy_kbank_body, hm=tm // 2),
        out_shape=(jax.ShapeDtypeStruct((1, ln, c), F32),) * 2, grid=(ln // tm, c // tc),
        in_specs=[fspec, bspec, fspec, bspec, pl.BlockSpec((tm, LANES), lambda i, j: (i, 0))],
        out_specs=(fspec, fspec), name="hy_kbank",
        compiler_params=_cp(("parallel", "parallel")))(ef, ef, of, of, tw)


def _hy_spectrum_epilogue(acc, row_blk, e_ref, tw_ref, ka_ref, kb_ref, *, hm):
    e = e_ref[...]
    (ar, ai, br, bi), (c, s) = _parity_banks(e, acc, tw_ref[...], hm)
    kar, kai, kbr, kbi = ka_ref[0:hm], ka_ref[hm:], kb_ref[0:hm], kb_ref[hm:]
    yar, yai = ar * kar - ai * kai, ar * kai + ai * kar
    ybr, ybi = br * kbr - bi * kbi, br * kbi + bi * kbr
    qr, qi = yar - ybr, yai - ybi
    pr, pi = yar + ybr, yai + ybi
    qpr, qpi = qr * c - qi * s, qi * c + qr * s
    top = slice(0, SUBLANES)
    first = _is_bin0(row_blk, SUBLANES)
    y0, yl = ar[top] * kar[top], br[top] * kbr[top]
    amr, ami = e[hm:hm + SUBLANES], -acc[hm:hm + SUBLANES]
    ymr, ymi = amr * kai[top] - ami * kbi[top], amr * kbi[top] + ami * kai[top]

    def patch(full, special):
        return jnp.concatenate([jnp.where(first, special, full[top]), full[SUBLANES:]], axis=0)

    pr, pi = patch(pr, y0 + yl), patch(pi, 2.0 * ymr)
    qpr, qpi = patch(qpr, y0 - yl), patch(qpi, -2.0 * ymi)
    return jnp.concatenate([pr, pi], axis=0), jnp.concatenate([qpr, qpi], axis=0)


def _hyena_out_body(*refs, segs):
    ns = len(segs)
    x0_ref, db_ref, o_ref = refs[3 * ns:3 * ns + 3]
    ys_refs = refs[3 * ns + 3:]
    half = CONV_CH // 2
    for si, (r0, n) in enumerate(segs):
        ye_ref, yo_ref, zin_ref = refs[3 * si:3 * si + 3]

        def body(i, carry, r0=r0, ye_ref=ye_ref, yo_ref=yo_ref, zin_ref=zin_ref):
            t0 = pl.multiple_of(i * CONV_CH, CONV_CH)
            hrows = pl.ds(pl.multiple_of(i * half, half), half)
            for g, ys_ref in enumerate(ys_refs):
                lanes = slice(g * LANES, (g + 1) * LANES)
                ys_ref[pl.ds(r0 + t0, half, stride=2), :] = ye_ref[hrows, lanes].astype(F32)
                ys_ref[pl.ds(r0 + t0 + 1, half, stride=2), :] = yo_ref[hrows, lanes].astype(F32)
            rows = pl.ds(r0 + t0, CONV_CH)
            y = jnp.concatenate([ys_ref[rows, :] for ys_ref in ys_refs], axis=1)
            zin = zin_ref[pl.ds(t0, CONV_CH), :].astype(F32)
            o_ref[rows, :] = (x0_ref[rows, :].astype(F32) * (y + zin * db_ref[...])).astype(o_ref.dtype)
            return carry

        lax.fori_loop(0, n // CONV_CH, body, 0, unroll=min(4, n // CONV_CH))


def _hyena_out(parts, x0, dbias, *, segs, tc=256):
    nb, t, c = x0.shape
    in_specs, args = [], []
    for (ye, yo, zin), (_, n) in zip(parts, segs):
        in_specs += [pl.BlockSpec((None, n // 2, tc), lambda b, j: (b, 0, j)),
                     pl.BlockSpec((None, n // 2, tc), lambda b, j: (b, 0, j)),
                     pl.BlockSpec((None, n, tc), lambda b, j: (b, 0, j))]
        args += [ye, yo, zin]
    in_specs += [pl.BlockSpec((None, t, tc), lambda b, j: (b, 0, j)), pl.BlockSpec((1, tc), lambda b, j: (0, j))]
    return pl.pallas_call(
        functools.partial(_hyena_out_body, segs=segs),
        out_shape=jax.ShapeDtypeStruct((nb, t, c), BF16), grid=(nb, c // tc), in_specs=in_specs,
        out_specs=pl.BlockSpec((None, t, tc), lambda b, j: (b, 0, j)),
        scratch_shapes=[pltpu.VMEM((t, LANES), F32)] * (tc // LANES), name="hyena_out",
        compiler_params=_cp(("parallel", "parallel")))(*args, x0, dbias)


def _hyena_longconv(ze, zo, hfb, *, tn=1024):
    nb, h, c = ze.shape
    ln = 2 * h
    tm_s = min(512, ln)
    hm = tm_s // 2
    fwd, inv = _dft_matrices(h)
    fwd = _tile_interleave(fwd, 1, hm)
    inv = _tile_interleave(inv * 0.5, 2, hm)
    tw = _twiddle_table(ln, hm)
    tm_f, tk_f = min(1024, ln), min(2048, h)
    ef = _matmul(fwd, hfb[:, :h], tm=tm_f, tn=tn, tk=tk_f, out_dtype=F32, name="hy_filter_dft_even")
    of = _matmul(fwd, hfb[:, h:], tm=tm_f, tn=tn, tk=tk_f, out_dtype=F32, name="hy_filter_dft_odd")
    ka, kb = _hy_kbank(ef, of, tw, tm=tm_s)
    e = _matmul(fwd, ze, tm=tm_f, tn=tn, tk=tk_f, out_dtype=F32, name="hy_dft_even")
    tile = pl.BlockSpec((None, tm_s, tn), lambda j, b, i, k: (b, i, j))
    ktile = pl.BlockSpec((None, tm_s, tn), lambda j, b, i, k: (0, i, j))
    extra = [(e, tile), (tw, pl.BlockSpec((tm_s, LANES), lambda j, b, i, k: (i, 0))), (ka, ktile), (kb, ktile)]
    p, q = _matmul(fwd, zo, tm=tm_s, tn=tn, tk=tk_f, out_dtype=BF16, extra=extra, n_out=2, batch_inner=True,
                   epilogue=functools.partial(_hy_spectrum_epilogue, hm=hm), name="hy_dft_odd")
    tm_i, tk_i = min(1024, h), min(2048, ln)
    ye = _matmul(inv, p, tm=tm_i, tn=tn, tk=tk_i, out_dtype=BF16, name="hy_idft_even")
    yo = _matmul(inv, q, tm=tm_i, tn=tn, tk=tk_i, out_dtype=BF16, name="hy_idft_odd")
    return ye, yo


def _mlstm_body(*refs, reverse, final):
    q_ref, k_ref, v_ref, gc_ref, gr_ref = refs[:5]
    if final:
        hf_ref, og_ref, nw_ref, o_ref, ct_ref, n_ref, m_ref = refs[5:]
    else:
        o_ref, ct_ref, n_ref, m_ref = refs[5:]
    nh, dk, dv = ML_HEADS, ML_QK_DIM, ML_V_DIM
    scale = dk ** -0.5

    @pl.when(pl.program_id(1) == 0)
    def _():
        ct_ref[...] = jnp.zeros_like(ct_ref)
        n_ref[...] = jnp.zeros_like(n_ref)
        m_ref[...] = jnp.zeros_like(m_ref)

    state = ([ct_ref[h] for h in range(nh)], n_ref[...], m_ref[...])
    for c in (reversed(range(SCAN_CPS)) if reverse else range(SCAN_CPS)):
        state = _mlstm_chunk(refs, slice(c * SCAN_Q, (c + 1) * SCAN_Q), state, reverse=reverse, final=final)
    for h in range(nh):
        ct_ref[h] = state[0][h]
    n_ref[...] = state[1]
    m_ref[...] = state[2]


def _mlstm_chunk(refs, rs, state, *, reverse, final):
    q_ref, k_ref, v_ref, gc_ref, gr_ref = refs[:5]
    if final:
        hf_ref, og_ref, nw_ref, o_ref = refs[5:9]
    else:
        o_ref = refs[5]
    nh, dk, dv = ML_HEADS, ML_QK_DIM, ML_V_DIM
    scale = dk ** -0.5
    cts, n_all, m_all = state
    gc, gr = gc_ref[rs, :], gr_ref[:, rs]
    li_c, lf_c = gc[:, 0:nh], _log_sigmoid(gc[:, nh:2 * nh])
    li_r, lf_r = gr[0:nh, :], _log_sigmoid(gr[nh:2 * nh, :])
    mask, maskf, mask_tf = _scan_masks(reverse)
    bc_c = _dot(maskf, lf_c, HI)
    bc_r = _dot(lf_r, mask_tf, HI)
    btot_c = jnp.sum(lf_c, axis=0, keepdims=True)
    heads = range(nh)
    qs = [q_ref[rs, dk * h:dk * (h + 1)] for h in heads]
    ks = [k_ref[rs, dk * h:dk * (h + 1)] for h in heads]
    vs = [v_ref[rs, dv * h:dv * (h + 1)] for h in heads]
    m_s = [m_all[h:h + 1, 0:1] for h in heads]
    n_s = [n_all[h:h + 1, :] for h in heads]
    qk = [_dot_nt(qs[h], ks[h]) for h in heads]
    qc = [_dot(qs[h], cts[h].astype(BF16)) for h in heads]
    dmat = [jnp.where(mask, bc_c[:, h:h + 1] - bc_r[h:h + 1, :] + li_r[h:h + 1, :], -jnp.inf) for h in heads]
    inter = [bc_c[:, h:h + 1] + m_s[h] for h in heads]
    m_t = [jnp.maximum(inter[h], jnp.max(dmat[h], axis=1, keepdims=True)) for h in heads]
    s = [qk[h] * scale * jnp.exp(dmat[h] - m_t[h]) for h in heads]
    dec = [jnp.exp(inter[h] - m_t[h]) for h in heads]
    num = [_dot(s[h].astype(BF16), vs[h]) + (dec[h] * scale) * qc[h] for h in heads]
    qn = [jnp.sum(qs[h].astype(F32) * n_s[h], axis=1, keepdims=True) * scale for h in heads]
    den = [jnp.sum(s[h], axis=1, keepdims=True) + dec[h] * qn[h] for h in heads]
    hh = [num[h] / jnp.maximum(jnp.abs(den[h]), jnp.exp(-m_t[h])) for h in heads]

    wst_c = [btot_c[:, h:h + 1] - bc_c[:, h:h + 1] + li_c[:, h:h + 1] for h in heads]
    wst_r = [btot_c[:, h:h + 1] - bc_r[h:h + 1, :] + li_r[h:h + 1, :] for h in heads]
    m_new = [jnp.maximum(btot_c[:, h:h + 1] + m_s[h], jnp.max(wst_r[h], axis=1, keepdims=True)) for h in heads]
    dstate = [jnp.exp(btot_c[:, h:h + 1] + m_s[h] - m_new[h]) for h in heads]
    kw = [ks[h].astype(F32) * jnp.exp(wst_c[h] - m_new[h]) for h in heads]
    upd = [_dot(kw[h].T.astype(BF16), vs[h]) for h in heads]
    n_rows = [dstate[h] * n_s[h] + jnp.sum(kw[h], axis=0, keepdims=True) for h in heads]
    m_rows = [jnp.broadcast_to(m_new[h], (1, LANES)) for h in heads]

    for h in heads:
        cols = slice(dv * h, dv * (h + 1))
        if final:
            hs = hh[h] + hf_ref[rs, cols]
            hn = hs * lax.rsqrt(jnp.mean(hs * hs, axis=-1, keepdims=True) + EPS) * nw_ref[:, cols]
            o_ref[rs, cols] = (hn * _sigmoid(og_ref[rs, cols].astype(F32))).astype(o_ref.dtype)
        else:
            o_ref[rs, cols] = hh[h]

    return ([dstate[h] * cts[h] + upd[h] for h in heads], jnp.concatenate(n_rows, axis=0),
            jnp.concatenate(m_rows, axis=0))


def _mlstm(qk, pr, g_c, g_r, *, direction, n_lat, final_args=None):
    nb, t, _ = qk.shape
    q = SCAN_CPS * SCAN_Q
    assert n_lat % q == 0 and t % q == 0
    nch = t // q
    reverse = direction == 1
    cmap = _chunk_map(reverse, nch, n_lat // q)
    d = direction
    in_specs = [
        pl.BlockSpec((None, q, ML_QK_WIDTH), lambda b, k: (b, cmap(k), 0)),
        pl.BlockSpec((None, q, ML_QK_WIDTH), lambda b, k: (b, cmap(k), 1)),
        pl.BlockSpec((None, q, ML_WIDTH), lambda b, k: (b, cmap(k), 2 * ML_QK_WIDTH // ML_WIDTH)),
        pl.BlockSpec((None, None, q, 2 * ML_HEADS), lambda b, k: (b, d, cmap(k), 0)),
        pl.BlockSpec((None, None, 2 * ML_HEADS, q), lambda b, k: (b, d, 0, cmap(k))),
    ]
    args = [qk, qk, pr, g_c, g_r]
    final = final_args is not None
    if final:
        hf, norm_w = final_args
        in_specs += [pl.BlockSpec((None, q, ML_WIDTH), lambda b, k: (b, cmap(k), 0)),
                     pl.BlockSpec((None, q, ML_WIDTH), lambda b, k: (b, cmap(k), 2 * ML_QK_WIDTH // ML_WIDTH + 1)),
                     pl.BlockSpec((1, ML_WIDTH), lambda b, k: (0, 0))]
        args += [hf, pr, norm_w]
    out_shape = jax.ShapeDtypeStruct((nb, t, ML_WIDTH), BF16 if final else F32)
    return pl.pallas_call(
        functools.partial(_mlstm_body, reverse=reverse, final=final),
        out_shape=out_shape, grid=(nb, nch), in_specs=in_specs,
        out_specs=pl.BlockSpec((None, q, ML_WIDTH), lambda b, k: (b, cmap(k), 0)),
        scratch_shapes=[pltpu.VMEM((ML_HEADS, ML_QK_DIM, ML_V_DIM), F32), pltpu.VMEM((ML_HEADS, ML_QK_DIM), F32),
                        pltpu.VMEM((ML_HEADS, LANES), F32)],
        name="mlstm_bwd" if reverse else "mlstm_fwd",
        compiler_params=_cp(("parallel", "arbitrary")))(*args)


def _na_window_start(j, rows):
    return jnp.clip(j * NA_QROWS - NA_ROWS // 2, 0, rows - NA_KROWS)


def _na_part_offset(j, part, start, clip=jnp.clip):
    first = j * NA_QROWS + part * (NA_QROWS // NA_PARTS) - NA_ROWS // 2
    return clip(first - start, 0, NA_KROWS - NA_PART_KROWS)


def _na_body(q_ref, k_ref, v_ref, tab_ref, qw_ref, kw_ref, o_ref, kn_ref, v1_ref, *, n_lat, n_ctx):
    dh = NA_HEAD_DIM
    scale = dh ** -0.5
    rows = n_lat // GRID_W
    j = pl.program_id(2)
    heads = range(NA_HPS)
    hcols = [slice(hh * dh, (hh + 1) * dh) for hh in heads]

    @pl.when(j == 0)
    def _():
        prep = (n_lat + n_ctx) // NA_PREP_STEPS
        one_hot = (lax.broadcasted_iota(jnp.int32, (prep, LANES), 1) == 0).astype(v1_ref.dtype)

        def body(i, carry):
            r = pl.ds(pl.multiple_of(i * prep, 16), prep)
            for hh in heads:
                kk = k_ref[r, hcols[hh]].astype(F32)
                kn = kk * lax.rsqrt(jnp.mean(kk * kk, axis=-1, keepdims=True) + EPS) * kw_ref[...]
                kn_ref[r, hcols[hh]] = kn.astype(kn_ref.dtype)
                v1_ref[hh, r, 0:dh] = v_ref[r, hcols[hh]]
                v1_ref[hh, r, dh:dh + LANES] = one_hot
            return carry

        lax.fori_loop(0, NA_PREP_STEPS, body, 0)

    start = _na_window_start(j, rows)
    ctx = pl.ds(n_lat, n_ctx)
    rpp = NA_QROWS // NA_PARTS
    pq = rpp * GRID_W
    chains = [(hh, i) for hh in heads for i in range(NA_PARTS)]
    off = [_na_part_offset(j, i, start) for i in range(NA_PARTS)]
    win = [pl.ds(pl.multiple_of((start + off[i]) * GRID_W, GRID_W), NA_PART_KROWS * GRID_W) for i in range(NA_PARTS)]
    qq = [q_ref[i * pq:(i + 1) * pq, hcols[hh]].astype(F32) for hh, i in chains]
    qn = [(x * lax.rsqrt(jnp.mean(x * x, axis=-1, keepdims=True) + EPS) * (qw_ref[...] * scale)).astype(BF16)
          for x in qq]
    sw = [_dot_nt(qn[c], kn_ref[win[i], hcols[hh]]) for c, (hh, i) in enumerate(chains)]
    sc = [_dot_nt(qn[c], kn_ref[ctx, hcols[hh]]) for c, (hh, i) in enumerate(chains)]

    def with_bias(c, hh, i):
        out = []
        for r in range(rpp):
            d0 = start - (j * NA_QROWS + i * rpp + r) + NA_ROWS - 1 + NA_QROWS
            blocks = tab_ref[hh, d0, pl.ds(off[i] // NA_TAB_ROWS, NA_PART_KROWS // NA_TAB_ROWS)]
            bias = jnp.concatenate([blocks[b] for b in range(NA_PART_KROWS // NA_TAB_ROWS)], axis=1)
            out.append(sw[c][r * GRID_W:(r + 1) * GRID_W] + bias)
        return jnp.concatenate(out, axis=0)

    sb = [with_bias(c, hh, i) for c, (hh, i) in enumerate(chains)]
    m = [jnp.maximum(jnp.max(a, axis=1, keepdims=True), jnp.max(b, axis=1, keepdims=True)) for a, b in zip(sb, sc)]
    pw = [jnp.exp(a - mm) for a, mm in zip(sb, m)]
    pc = [jnp.exp(a - mm) for a, mm in zip(sc, m)]
    o = [_dot(pw[c].astype(BF16), v1_ref[hh, win[i], :]) + _dot(pc[c].astype(BF16), v1_ref[hh, ctx, :])
         for c, (hh, i) in enumerate(chains)]
    for c, (hh, i) in enumerate(chains):
        o_ref[i * pq:(i + 1) * pq, hcols[hh]] = (o[c][:, 0:dh] / o[c][:, dh:dh + 1]).astype(o_ref.dtype)


def _na_bias_table(rpb, rows):
    nd = 2 * NA_QROWS
    valid = np.zeros((nd, NA_KROWS), bool)
    seen = np.zeros((nd,), bool)
    for j in range(rows // NA_QROWS):
        start = int(np.clip(j * NA_QROWS - NA_ROWS // 2, 0, rows - NA_KROWS))
        for qr in range(NA_QROWS):
            r = j * NA_QROWS + qr
            rs = int(np.clip(r - NA_ROWS // 2, 0, rows - NA_ROWS))
            d = start - r + NA_ROWS - 1 + NA_QROWS
            v = np.array([(rs <= start + i < rs + NA_ROWS) for i in range(NA_KROWS)])
            assert 0 <= d < nd and (not seen[d] or (valid[d] == v).all())
            valid[d], seen[d] = v, True
            off = int(_na_part_offset(j, qr // (NA_QROWS // NA_PARTS), start, clip=np.clip))
            assert off % NA_TAB_ROWS == 0 and not v[:off].any() and not v[off + NA_PART_KROWS:].any()
    col = np.arange(GRID_W)
    cs = np.clip(col - NA_COLS // 2, 0, GRID_W - NA_COLS)
    in_win = (col[None, :] >= cs[:, None]) & (col[None, :] < cs[:, None] + NA_COLS)
    ndr, ndc, w = 2 * NA_ROWS - 1, 2 * NA_COLS - 1, GRID_W
    dr = (np.arange(nd)[:, None] - NA_QROWS) + np.arange(NA_KROWS)[None, :]
    ok = valid[:, None, :, None] & in_win[None, :, None, :] & (dr >= 0)[:, None, :, None] \
        & (dr < ndr)[:, None, :, None]
    ext = jnp.pad(rpb, ((0, 0), (0, 0), (w, w)), mode="edge")
    c0 = w + NA_COLS - 1
    t1 = jnp.stack([ext[:, :, c0 - qc:c0 - qc + w] for qc in range(w)], axis=2)
    t1p = jnp.pad(t1, ((0, 0), (NA_QROWS, nd + NA_KROWS - 1 - NA_QROWS - ndr), (0, 0), (0, 0)))
    wide = jnp.concatenate([t1p[:, m] for m in range(nd + NA_KROWS - 1)], axis=-1)
    wpad = LANES * pl.cdiv(wide.shape[-1], LANES)
    wide = jnp.pad(wide, ((0, 0), (0, 0), (0, wpad - wide.shape[-1])))
    nblk, bw = NA_KROWS // NA_TAB_ROWS, NA_TAB_ROWS * w
    okf = jnp.asarray(ok.reshape(nd, w, nblk, bw).transpose(0, 2, 1, 3).astype(np.float32))

    def body(wide_ref, ok_ref, o_ref):
        for d in range(nd):
            for b in range(nblk):
                lo = d * w + b * bw
                o_ref[d, b] = jnp.where(ok_ref[d, b] != 0.0, wide_ref[:, lo:lo + bw], -jnp.inf)

    return pl.pallas_call(
        body, out_shape=jax.ShapeDtypeStruct((rpb.shape[0], nd, nblk, w, bw), F32), grid=(rpb.shape[0],),
        in_specs=[pl.BlockSpec((None, w, wpad), lambda h: (h, 0, 0)),
                  pl.BlockSpec((nd, nblk, w, bw), lambda h: (0, 0, 0, 0))],
        out_specs=pl.BlockSpec((None, nd, nblk, w, bw), lambda h: (h, 0, 0, 0, 0)),
        name="na_bias_table", compiler_params=_cp(("parallel",)))(wide, okf)


def _na(pr, tab, q_w, k_w, *, n_lat, n_ctx, qcol):
    nb, t, _ = pr.shape
    dh, nh, hps = NA_HEAD_DIM, NA_HEADS, NA_HPS
    nq = NA_QROWS * GRID_W
    hw = hps * dh
    assert qcol % hw == 0 and nh % hps == 0
    qb = qcol // hw
    return pl.pallas_call(
        functools.partial(_na_body, n_lat=n_lat, n_ctx=n_ctx),
        out_shape=jax.ShapeDtypeStruct((nb, n_lat, NA_WIDTH), BF16), grid=(nb, nh // hps, n_lat // nq),
        in_specs=[pl.BlockSpec((None, nq, hw), lambda b, h, j: (b, j, qb + h)),
                  pl.BlockSpec((None, t, hw), lambda b, h, j: (b, 0, qb + nh // hps + h)),
                  pl.BlockSpec((None, t, hw), lambda b, h, j: (b, 0, qb + 2 * (nh // hps) + h)),
                  pl.BlockSpec((hps,) + tab.shape[1:], lambda b, h, j: (h, 0, 0, 0, 0)),
                  pl.BlockSpec((1, dh), lambda b, h, j: (0, 0)),
                  pl.BlockSpec((1, dh), lambda b, h, j: (0, 0))],
        out_specs=pl.BlockSpec((None, nq, hw), lambda b, h, j: (b, j, h)),
        scratch_shapes=[pltpu.VMEM((t, hw), BF16), pltpu.VMEM((hps, t, dh + LANES), BF16)], name="na",
        compiler_params=_cp(("parallel", "parallel", "arbitrary")))(pr, pr, pr, tab, q_w, k_w)


def _rope_tables(n_lat, n_ctx, width):
    dh = ML_QK_DIM
    nf = dh // 4
    t = jnp.arange(n_lat)
    inv = ROPE_THETA ** (-jnp.arange(nf, dtype=F32) / nf)
    lane = np.arange(dh)
    pos = jnp.where((lane < dh // 2)[None, :], (t // GRID_W)[:, None], (t % GRID_W)[:, None]).astype(F32)
    ang = pos * inv[lane % nf][None, :]
    sign = np.where((lane % (dh // 2)) < nf, -1.0, 1.0).astype(np.float32)
    cos = jnp.concatenate([jnp.cos(ang), jnp.ones((n_ctx, dh), F32)], axis=0)
    sin = jnp.concatenate([jnp.sin(ang) * sign[None, :], jnp.zeros((n_ctx, dh), F32)], axis=0)
    reps = width // dh
    return jnp.tile(cos, (1, reps)), jnp.tile(sin, (1, reps))


def _ada_mod(c, c_ctx, w_all, b_all, layer):
    nb, d = c.shape
    rows = 16
    cv = jnp.concatenate([c, c_ctx[None, :], jnp.zeros((rows - nb - 1, d), F32)], axis=0)
    a = (cv * _sigmoid(cv)).astype(BF16)[None]
    out = _matmul(a, w_all, w_index=layer, tm=rows, tn=1024, tk=d, out_dtype=F32, name="ada_mod")[0]
    out = out[:nb + 1] + b_all[layer][None, :]
    return out.reshape(nb + 1, 6, d).transpose(1, 0, 2)


def _lat_ctx(m, nb):
    ctx = jnp.broadcast_to(m[nb][None, :], (nb, m.shape[1]))
    return jnp.stack([m[:nb], ctx], axis=1)[:, :, None, :]


def _pad_cols(w, n):
    return jnp.pad(w, ((0, 0), (0, n - w.shape[1])))


def _col_row_forms(g, groups):
    nb, t, _ = g.shape
    g5 = g.reshape(nb, t, 2, groups, -1)
    return g5.transpose(0, 2, 3, 1, 4), g5.transpose(0, 2, 3, 4, 1)


ROW_TILES = (1088, 1024, 512, 256)


def _row_tile(rows):
    return next(tm for tm in ROW_TILES if rows % tm == 0)


def _mod_pair(mods, shift, scale, nb):
    return jnp.stack([_lat_ctx(mods[shift], nb), _lat_ctx(mods[scale], nb)], axis=1)


def _in_projection(u, w_all, layer, *, rows, name, cols=None):
    tm = _row_tile(rows)
    tm = 2 * tm if rows % (2 * tm) == 0 else tm
    return _matmul(u, w_all, w_index=layer, cols=cols, tm=tm, tn=1024, tk=u.shape[2], out_dtype=BF16,
                   rows=rows, vmem=VMEM_LIMIT_BIG, name=name)


def _out_projection(a_list, w_all, x, gate, layer, *, rows, n_lat, next_norm, name):
    tm = _row_tile(rows)
    tn = 512 if w_all.shape[1] <= 2 * D_MODEL else 256
    x = _matmul_gated_residual_rows(a_list, w_all, x, gate, w_index=layer, tm=tm, tn=tn, rows=rows, n_lat=n_lat,
                                    name=name)
    if next_norm is None:
        return x, None
    return x, _rmsmod(x, *next_norm, tm=tm, rows=rows, n_lat=n_lat)


def _conv_ffn_block(x, u, mods, w_up_all, conv_w, conv_b, w_down_all, layer, *, rows, n_lat, segs, next_norm):
    nb = x.shape[0]
    up = _in_projection(u, w_up_all, layer, rows=rows, name="ffn_up")
    hid = _conv_ffn(up, conv_w, conv_b[None, :], segs=segs, rows=rows)
    return _out_projection((hid,), w_down_all, x, _lat_ctx(mods[5], nb), layer, rows=rows, n_lat=n_lat,
                           next_norm=next_norm, name="ffn_down")


def _even_layer(x, u, mods, w_in_all, conv_w, conv_b, dt_bias, a_log, d_skip, ssd_norm_w,
                hy_w1, hy_b1, hy_w2, hy_b2, hy_w3, hy_freq, hy_bias, w_out_all, j, *, n_lat, n_ctx, next_norm):
    nb, t, d = x.shape
    segs = ((0, n_lat), (n_lat, n_ctx))
    tm = _row_tile(t)
    n_main = SSD_WIDTH + EV_CONV_CH
    pr = _in_projection(u, w_in_all, j, rows=t, cols=n_main, name="ev_in")
    w_dt = _pad_cols(w_in_all[j, :, n_main:], LANES).astype(BF16)
    dtp = _matmul(u, w_dt[None], tm=tm, tn=LANES, tk=d, out_dtype=F32, name="ev_in_dt")[:, :, :2 * SSD_HEADS]
    cb = conv_b[None, :]
    xbc = _conv_silu(pr, conv_w, cb, xcol=SSD_WIDTH, wcol=0, width=SSD_XBC, segs=segs)
    x0, zin_l, zin_c, zle, zlo, zce, zco = _conv_hyena(pr, conv_w, cb, xcol=SSD_WIDTH + SSD_XBC, wcol=SSD_XBC,
                                                       segs=segs)

    dt_c, dt_r = _col_row_forms(dtp, SSD_GROUPS)
    db = dt_bias.reshape(2, SSD_GROUPS, 1, SSD_HPG)
    al = a_log.reshape(2, SSD_GROUPS, 1, SSD_HPG)
    scan_args = (xbc, dt_c, dt_r, db, db.transpose(0, 1, 3, 2), al, al.transpose(0, 1, 3, 2))
    yf = _ssd(*scan_args, direction=0, n_lat=n_lat)
    dsk = jnp.repeat(d_skip, SSD_HEAD_DIM)[None, :]
    y_ssd = _ssd(*scan_args, direction=1, n_lat=n_lat, final_args=(yf, pr, dsk, ssd_norm_w[None, :]))

    hyb = hy_bias[None, :]
    filt = functools.partial(_hyena_filters, w1=hy_w1, b1=hy_b1, w2=hy_w2, b2=hy_b2, w3=hy_w3, freq=hy_freq)
    lat = _hyena_longconv(zle, zlo, filt(n_lat)) + (zin_l,)
    ctx = _hyena_longconv(zce, zco, filt(n_ctx)) + (zin_c,)
    y_hy = _hyena_out((lat, ctx), x0, hyb, segs=segs)
    return _out_projection((y_ssd, y_hy), w_out_all, x, _lat_ctx(mods[2], nb), j, rows=t, n_lat=n_lat,
                           next_norm=next_norm, name="ev_out")


def _odd_layer_latent(x, u, mods, w_in, conv_w, conv_b, gate_b, ml_norm_w, q_norm_w, k_norm_w, rpb,
                      w_out_all, j, *, n_lat, n_ctx, next_norm):
    nb, t, d = x.shape
    segs = ((0, n_lat), (n_lat, n_ctx))
    tm = _row_tile(t)
    o3 = 2 * ML_QK_WIDTH + 2 * ML_WIDTH
    o4 = o3 + 4 * ML_HEADS
    wb = w_in.astype(BF16)
    w_main = jnp.concatenate([wb[:, :o3], wb[:, o4:]], axis=1)
    pr = _in_projection(u, w_main[None], 0, rows=t, name="od_in")
    gates = _matmul(u, _pad_cols(wb[:, o3:o4], LANES)[None], tm=tm, tn=LANES, tk=d, out_dtype=F32,
                    name="od_in_gates")[:, :, :4 * ML_HEADS] + gate_b.reshape(-1)
    cos, sin = _rope_tables(n_lat, n_ctx, 256)
    qk = _conv_rope(pr, conv_w, conv_b[None, :], cos, sin, width=2 * ML_QK_WIDTH, segs=segs)
    g5 = gates.reshape(nb, t, 2, 2 * ML_HEADS)
    g_c, g_r = g5.transpose(0, 2, 1, 3), g5.transpose(0, 2, 3, 1)
    hf = _mlstm(qk, pr, g_c, g_r, direction=0, n_lat=n_lat)
    y_ml = _mlstm(qk, pr, g_c, g_r, direction=1, n_lat=n_lat, final_args=(hf, ml_norm_w[None, :]))
    tab = _na_bias_table(rpb, n_lat // GRID_W)
    y_na = _na(pr, tab, q_norm_w[None, :], k_norm_w[None, :], n_lat=n_lat, n_ctx=n_ctx, qcol=o3)
    return _out_projection((y_ml, y_na), w_out_all, x, _lat_ctx(mods[2], nb), j, rows=n_lat, n_lat=n_lat,
                           next_norm=next_norm, name="od_out")


def kernel(x, c, ctx, c_ctx, ada_w, ada_b, norm_w, ev_w_in, ev_conv_w, ev_conv_b, ssd_dt_bias, ssd_a_log, ssd_d,
           ssd_norm_w, hy_w1, hy_b1, hy_w2, hy_b2, hy_w3, hy_freq, hy_bias, ev_w_out, od_w_in, ml_conv_w, ml_conv_b,
           ml_gate_b, ml_norm_w, na_q_norm_w, na_k_norm_w, na_rpb, od_w_out, ffn_w_up, ffn_conv_w, ffn_conv_b,
           ffn_w_down):
    depth = ada_w.shape[0]
    assert depth == 2, "layer schedule below is written for one even and one odd layer"
    n_lat, n_ctx = x.shape[1], ctx.shape[1]
    t = n_lat + n_ctx
    both = ((0, n_lat), (n_lat, n_ctx))
    xs = jnp.concatenate([x, ctx], axis=1)

    nb = x.shape[0]
    w_down = ffn_w_down.astype(BF16)
    mods0 = _ada_mod(c, c_ctx, ada_w, ada_b, 0)
    mods1 = _ada_mod(c, c_ctx, ada_w, ada_b, 1)
    norm = lambda layer, sub, mods: (norm_w[layer, sub][None, :], _mod_pair(mods, 3 * sub, 3 * sub + 1, nb))

    w0, mod0 = norm(0, 0, mods0)
    u = _rmsmod(xs, w0, mod0, tm=_row_tile(t), rows=t, n_lat=n_lat)
    xs, u = _even_layer(xs, u, mods0, ev_w_in, ev_conv_w[0], ev_conv_b[0], ssd_dt_bias[0], ssd_a_log[0],
                        ssd_d[0], ssd_norm_w[0], hy_w1[0], hy_b1[0], hy_w2[0], hy_b2[0], hy_w3[0], hy_freq[0],
                        hy_bias[0], ev_w_out.astype(BF16), 0, n_lat=n_lat, n_ctx=n_ctx, next_norm=norm(0, 1, mods0))
    xs, u = _conv_ffn_block(xs, u, mods0, ffn_w_up, ffn_conv_w[0], ffn_conv_b[0], w_down, 0,
                            rows=t, n_lat=n_lat, segs=both, next_norm=norm(1, 0, mods1))
    xl, u = _odd_layer_latent(xs, u, mods1, od_w_in[0], ml_conv_w[0], ml_conv_b[0], ml_gate_b[0], ml_norm_w[0],
                              na_q_norm_w[0], na_k_norm_w[0], na_rpb[0], od_w_out.astype(BF16), 0,
                              n_lat=n_lat, n_ctx=n_ctx, next_norm=norm(1, 1, mods1))
    out, _ = _conv_ffn_block(xl, u, mods1, ffn_w_up, ffn_conv_w[1], ffn_conv_b[1], w_down, 1,
                             rows=n_lat, n_lat=n_lat, segs=((0, n_lat),), next_norm=None)
    return out
```

```python
import functools
import math

import numpy as np
import jax
import jax.numpy as jnp
from jax import lax
from jax.experimental import pallas as pl
from jax.experimental.pallas import tpu as pltpu

F32 = jnp.float32
BF16 = jnp.bfloat16
HI = lax.Precision.HIGHEST

D_MODEL = 2048
GRID_W = 64
EPS = 1e-6
CONV_W = 3
SSD_WIDTH = D_MODEL
SSD_HEAD_DIM = 64
SSD_HEADS = SSD_WIDTH // SSD_HEAD_DIM
SSD_STATE = 128
SSD_GROUPS = 4
SSD_HPG = SSD_HEADS // SSD_GROUPS
SSD_GW = SSD_WIDTH // SSD_GROUPS
SSD_XBC = SSD_WIDTH + 2 * SSD_GROUPS * SSD_STATE
HY_WIDTH = D_MODEL
HY_EMB = 33
HY_BANDS = (HY_EMB - 1) // 2
HY_ORDER = 64
HY_FAST = 0.3
HY_SLOW = 1.5
HY_TARGET = 1e-2
ML_HEADS = 8
ML_WIDTH = D_MODEL
ML_V_DIM = ML_WIDTH // ML_HEADS
ML_QK_DIM = ML_V_DIM // 2
ML_QK_WIDTH = ML_HEADS * ML_QK_DIM
ROPE_THETA = 10000.0
NA_WIDTH = D_MODEL
NA_HEAD_DIM = 128
NA_HEADS = NA_WIDTH // NA_HEAD_DIM
NA_ROWS = 8
NA_COLS = 16
NA_QROWS = 8
NA_KROWS = 16
NA_PARTS = 2
NA_HPS = 2
NA_PREP_STEPS = 8
NA_TAB_ROWS = 4
NA_PART_KROWS = 12
EV_CONV_CH = SSD_XBC + 3 * HY_WIDTH
SCAN_Q = 128
SCAN_CPS = 2
LANES = 128
SUBLANES = 8
MIB = 1024 * 1024
V7X_VMEM_BYTES = 64 * MIB
VMEM_LIMIT = 48 * MIB
VMEM_LIMIT_BIG = V7X_VMEM_BYTES - 6 * MIB


def _cp(sem, vmem=VMEM_LIMIT):
    return pltpu.CompilerParams(dimension_semantics=sem, vmem_limit_bytes=vmem)


def _sigmoid(x):
    return 0.5 * jnp.tanh(0.5 * x) + 0.5


def _silu(x):
    return x * _sigmoid(x)


def _softplus(x):
    return jnp.maximum(x, 0.0) + jnp.log(1.0 + jnp.exp(-jnp.abs(x)))


def _log_sigmoid(x):
    return -_softplus(-x)


def _dot(a, b, precision=None):
    return jnp.dot(a, b, preferred_element_type=F32, precision=precision)


def _dot_nt(a, b):
    return lax.dot_general(a, b, (((1,), (1,)), ((), ())), preferred_element_type=F32)


def _split_bf16(x, terms):
    out = []
    for _ in range(terms):
        part = x.astype(BF16)
        out.append(part)
        x = x - part.astype(F32)
    return out


def _dot_sel(x, sel, terms):
    selb = sel.astype(BF16)
    return sum(_dot(part, selb) for part in _split_bf16(x, terms))


def _sel_dot(sel, x, terms):
    selb = sel.astype(BF16)
    return sum(_dot(selb, part) for part in _split_bf16(x, terms))


def _mm_body(a_ref, w_ref, o_ref, *scratch, nk, cache_w):
    if cache_w:
        wb_ref = scratch[0]

        @pl.when((pl.program_id(1) == 0) & (pl.program_id(2) == 0))
        def _():
            wb_ref[...] = w_ref[...].astype(BF16)

        w_val = wb_ref[...]
    else:
        w_val = w_ref[...].astype(BF16)
    part = _dot(a_ref[...].astype(BF16), w_val)
    if nk == 1:
        o_ref[...] = part.astype(o_ref.dtype)
        return
    acc_ref = scratch[0]
    k = pl.program_id(3)

    @pl.when(k == 0)
    def _():
        acc_ref[...] = part

    @pl.when(k > 0)
    def _():
        acc_ref[...] += part

    @pl.when(k == nk - 1)
    def _():
        o_ref[...] = acc_ref[...].astype(o_ref.dtype)


def _matmul(a, w, *, tm, tn, tk, out_dtype, rows=None, cols=None, w_index=None, vmem=VMEM_LIMIT, name="mm"):
    ba, ta, kdim = a.shape
    bw, kdim2, n = w.shape
    assert kdim == kdim2
    nb = ba if w_index is not None else max(ba, bw)
    rows = ta if rows is None else rows
    n = n if cols is None else cols
    assert rows % tm == 0 and n % tn == 0 and kdim % tk == 0
    ni, nj, nk = rows // tm, n // tn, kdim // tk
    a_map = (lambda j, b, i, k: (b, i, k)) if ba > 1 else (lambda j, b, i, k: (0, i, k))
    if w_index is not None:
        w_map = lambda j, b, i, k: (w_index, k, j)
    else:
        w_map = (lambda j, b, i, k: (b, k, j)) if bw > 1 else (lambda j, b, i, k: (0, k, j))
    cache_w = w.dtype != BF16 and nk == 1 and (w_index is not None or bw == 1)
    scratch = [pltpu.VMEM((tm, tn), F32)] if nk > 1 else []
    if cache_w:
        scratch = [pltpu.VMEM((tk, tn), BF16)]
    inner = "arbitrary" if cache_w else "parallel"
    return pl.pallas_call(
        functools.partial(_mm_body, nk=nk, cache_w=cache_w),
        out_shape=jax.ShapeDtypeStruct((nb, rows, n), out_dtype), grid=(nj, nb, ni, nk),
        in_specs=[pl.BlockSpec((None, tm, tk), a_map), pl.BlockSpec((None, tk, tn), w_map)],
        out_specs=pl.BlockSpec((None, tm, tn), lambda j, b, i, k: (b, i, j)),
        scratch_shapes=scratch, name=name,
        compiler_params=_cp(("parallel", inner, inner, "arbitrary"), vmem))(a, w)


def _row_is_lat(row_blk, tm, n_lat):
    rows = row_blk * tm + lax.broadcasted_iota(jnp.int32, (tm, 1), 0)
    return rows < n_lat


def _gated_residual_epilogue(acc, row_blk, res_ref, gate_ref, *, tm, n_lat):
    gate = jnp.where(_row_is_lat(row_blk, tm, n_lat), gate_ref[0], gate_ref[1])
    return res_ref[...] + gate * acc


def _gated_residual_rows_body(*refs, ks, tm, n_lat):
    n_a = len(ks)
    a_refs, w_ref, res_ref, gate_ref, o_ref = refs[:n_a], refs[n_a], refs[n_a + 1], refs[n_a + 2], refs[n_a + 3]
    acc, lo = None, 0
    for a_ref, kk in zip(a_refs, ks):
        part = _dot(a_ref[...], w_ref[lo:lo + kk, :])
        acc = part if acc is None else acc + part
        lo += kk
    o_ref[...] = _gated_residual_epilogue(acc, pl.program_id(1), res_ref, gate_ref, tm=tm, n_lat=n_lat)


def _matmul_gated_residual_rows(a_list, w, res, gate, *, w_index, tm, tn, rows, n_lat, name):
    nb = a_list[0].shape[0]
    _, kdim, n = w.shape
    ks = tuple(x.shape[2] for x in a_list)
    assert sum(ks) == kdim and rows % tm == 0 and n % tn == 0
    in_specs = [pl.BlockSpec((None, tm, kk), lambda b, i, j: (b, i, 0)) for kk in ks]
    in_specs += [pl.BlockSpec((None, kdim, tn), lambda b, i, j: (w_index, 0, j)),
                 pl.BlockSpec((None, tm, tn), lambda b, i, j: (b, i, j)),
                 pl.BlockSpec((None, 2, 1, tn), lambda b, i, j: (b, 0, 0, j))]
    return pl.pallas_call(
        functools.partial(_gated_residual_rows_body, ks=ks, tm=tm, n_lat=n_lat),
        out_shape=jax.ShapeDtypeStruct((nb, rows, n), F32), grid=(nb, rows // tm, n // tn), in_specs=in_specs,
        out_specs=pl.BlockSpec((None, tm, tn), lambda b, i, j: (b, i, j)), name=name,
        compiler_params=_cp(("parallel", "parallel", "parallel")))(*a_list, w, res, gate)


def _modulated_rmsnorm(x, is_lat, w_ref, mod_ref):
    y = x * lax.rsqrt(jnp.mean(x * x, axis=-1, keepdims=True) + EPS) * w_ref[...]
    shift = jnp.where(is_lat, mod_ref[0, 0], mod_ref[0, 1])
    scale = jnp.where(is_lat, mod_ref[1, 0], mod_ref[1, 1])
    return y * (1.0 + scale) + shift


def _rmsmod_body(x_ref, w_ref, mod_ref, o_ref, *, tm, n_lat):
    is_lat = _row_is_lat(pl.program_id(2), tm, n_lat)
    o_ref[...] = _modulated_rmsnorm(x_ref[...], is_lat, w_ref, mod_ref).astype(o_ref.dtype)


def _rmsmod(x, w, mod, *, tm, rows, n_lat):
    nb, _, d = x.shape
    tm = tm // 2
    assert rows % tm == 0 and tm % 16 == 0
    body = functools.partial(_rmsmod_body, tm=tm, n_lat=n_lat)
    return pl.pallas_call(
        body, out_shape=jax.ShapeDtypeStruct((nb, rows, d), BF16), grid=(1, nb, rows // tm),
        in_specs=[pl.BlockSpec((None, tm, d), lambda j, b, i: (b, i, 0)),
                  pl.BlockSpec((1, d), lambda j, b, i: (0, 0)),
                  pl.BlockSpec((None, 2, 2, 1, d), lambda j, b, i: (b, 0, 0, 0, 0))],
        out_specs=pl.BlockSpec((None, tm, d), lambda j, b, i: (b, i, 0)),
        name="rmsmod", compiler_params=_cp(("parallel", "parallel", "parallel")))(x, w, mod)


CONV_CH = 128
CONV_HALO = 16
CONV_STEPS = 4


def _shift_matrices():
    r = lax.broadcasted_iota(jnp.int32, (CONV_CH, CONV_CH + 2 * CONV_HALO), 0)
    c = lax.broadcasted_iota(jnp.int32, (CONV_CH, CONV_CH + 2 * CONV_HALO), 1)
    return (c == r + CONV_HALO - 1).astype(BF16), (c == r + CONV_HALO + 1).astype(BF16)


def _chunk_rows(r0, i):
    return pl.ds(r0 + pl.multiple_of(i * CONV_CH, CONV_CH), CONV_CH)


def _taps(w_ref, b_ref, scale=1.0):
    return w_ref[...] * scale, b_ref[...] * scale


def _silu_of_half(h):
    return h + h * jnp.tanh(h)


def _conv_at(x_ref, taps, shifts, r0, n, steps):
    w, b = taps
    cur = [x_ref[_chunk_rows(r0, i), :] for i in steps]
    zero = jnp.zeros((CONV_HALO, cur[0].shape[1]), cur[0].dtype)
    lo = [pl.multiple_of(jnp.maximum(i * CONV_CH - CONV_HALO, 0), CONV_HALO) for i in steps]
    hi = [pl.multiple_of(jnp.minimum((i + 1) * CONV_CH, n - CONV_HALO), CONV_HALO) for i in steps]
    tail = [jnp.where(i > 0, x_ref[pl.ds(r0 + l, CONV_HALO), :], zero) for i, l in zip(steps, lo)]
    head = [jnp.where(i < n // CONV_CH - 1, x_ref[pl.ds(r0 + h, CONV_HALO), :], zero) for i, h in zip(steps, hi)]
    win = [jnp.concatenate([t, c, h], axis=0) for t, c, h in zip(tail, cur, head)]
    prev = [_dot(shifts[0], w) for w in win]
    nxt = [_dot(shifts[1], w) for w in win]
    return [w[0:1, :] * p + w[1:2, :] * c.astype(F32) + w[2:3, :] * x + b for p, c, x in zip(prev, cur, nxt)]


def _conv_loop(segs, fn):
    for si, (r0, n) in enumerate(segs):
        def body(i, carry, r0=r0, n=n, si=si):
            fn(si, r0, n, [i])
            return carry

        lax.fori_loop(0, n // CONV_CH, body, 0, unroll=min(CONV_STEPS, n // CONV_CH))


def _conv_silu_body(x_ref, w_ref, b_ref, o_ref, *, segs):
    shifts = _shift_matrices()
    half_taps = _taps(w_ref, b_ref, 0.5)

    def fn(si, r0, n, steps):
        ys = [_silu_of_half(h) for h in _conv_at(x_ref, half_taps, shifts, r0, n, steps)]
        for i, y in zip(steps, ys):
            o_ref[_chunk_rows(r0, i), :] = y.astype(o_ref.dtype)

    _conv_loop(segs, fn)


def _conv_rope_body(x_ref, w_ref, b_ref, cos_ref, sin_ref, o_ref, *, segs, tc):
    shifts = _shift_matrices()
    lane = lax.broadcasted_iota(jnp.int32, (CONV_CH, tc), 1)
    first = (lane % (ML_QK_DIM // 2)) < (ML_QK_DIM // 4)
    nf = ML_QK_DIM // 4
    half_taps = _taps(w_ref, b_ref, 0.5)

    def fn(si, r0, n, steps):
        ys = [_silu_of_half(h) for h in _conv_at(x_ref, half_taps, shifts, r0, n, steps)]
        partner = [jnp.where(first, pltpu.roll(y, tc - nf, 1), pltpu.roll(y, nf, 1)) for y in ys]
        for i, y, p in zip(steps, ys, partner):
            rows = _chunk_rows(r0, i)
            o_ref[rows, :] = (y * cos_ref[rows, :] + p * sin_ref[rows, :]).astype(o_ref.dtype)

    _conv_loop(segs, fn)


def _conv_ffn_body(a_ref, x_ref, w_ref, b_ref, o_ref, *, segs):
    shifts = _shift_matrices()
    half_taps = _taps(w_ref, b_ref, 0.5)

    def fn(si, r0, n, steps):
        gs = [_silu_of_half(h) for h in _conv_at(x_ref, half_taps, shifts, r0, n, steps)]
        for i, g in zip(steps, gs):
            rows = _chunk_rows(r0, i)
            o_ref[rows, :] = (a_ref[rows, :].astype(F32) * g).astype(o_ref.dtype)

    _conv_loop(segs, fn)


def _conv_hyena_body(x0_ref, x1_ref, v_ref, w0_ref, b0_ref, w1_ref, b1_ref, w2_ref, b2_ref,
                     ox0_ref, ozl_ref, ozc_ref, ozle_ref, ozlo_ref, ozce_ref, ozco_ref, *zs_refs, segs):
    shifts = _shift_matrices()
    taps = (_taps(w0_ref, b0_ref), _taps(w1_ref, b1_ref), _taps(w2_ref, b2_ref))
    half = CONV_CH // 2
    outs = ((ozl_ref, ozle_ref, ozlo_ref), (ozc_ref, ozce_ref, ozco_ref))

    def fn(si, r0, n, steps):
        x0 = _conv_at(x0_ref, taps[0], shifts, r0, n, steps)
        x1 = _conv_at(x1_ref, taps[1], shifts, r0, n, steps)
        v = _conv_at(v_ref, taps[2], shifts, r0, n, steps)
        nat_ref, even_ref, odd_ref = outs[si]
        for i, x0_i, x1_i, v_i in zip(steps, x0, x1, v):
            ox0_ref[_chunk_rows(r0, i), :] = x0_i.astype(ox0_ref.dtype)
            zin = x1_i * v_i
            nat_ref[_chunk_rows(0, i), :] = zin.astype(BF16)
            t0 = r0 + pl.multiple_of(i * CONV_CH, CONV_CH)
            hrows = pl.ds(pl.multiple_of(i * half, half), half)
            for g, zs_ref in enumerate(zs_refs):
                lanes = slice(g * LANES, (g + 1) * LANES)
                zs_ref[pl.ds(t0, CONV_CH), :] = zin[:, lanes]
                even_ref[hrows, lanes] = zs_ref[pl.ds(t0, half, stride=2), :].astype(BF16)
                odd_ref[hrows, lanes] = zs_ref[pl.ds(t0 + 1, half, stride=2), :].astype(BF16)

    _conv_loop(segs, fn)


def _conv_specs(t_in, tc, xcol_blk, wcol_blk):
    return [pl.BlockSpec((None, t_in, tc), lambda b, j: (b, 0, j + xcol_blk)),
            pl.BlockSpec((CONV_W, tc), lambda b, j: (0, j + wcol_blk)),
            pl.BlockSpec((1, tc), lambda b, j: (0, j + wcol_blk))]


def _conv_silu(x, w, b, *, xcol, wcol, width, segs, tc=256):
    nb, t_in, _ = x.shape
    return pl.pallas_call(
        functools.partial(_conv_silu_body, segs=segs),
        out_shape=jax.ShapeDtypeStruct((nb, t_in, width), BF16), grid=(nb, width // tc),
        in_specs=_conv_specs(t_in, tc, xcol // tc, wcol // tc),
        out_specs=pl.BlockSpec((None, t_in, tc), lambda b, j: (b, 0, j)),
        name="conv_silu", compiler_params=_cp(("parallel", "parallel")))(x, w, b)


def _conv_rope(x, w, b, cos, sin, *, width, segs, tc=256):
    nb, t_in, _ = x.shape
    tab = pl.BlockSpec((t_in, tc), lambda b, j: (0, 0))
    return pl.pallas_call(
        functools.partial(_conv_rope_body, segs=segs, tc=tc),
        out_shape=jax.ShapeDtypeStruct((nb, t_in, width), BF16), grid=(nb, width // tc),
        in_specs=_conv_specs(t_in, tc, 0, 0) + [tab, tab],
        out_specs=pl.BlockSpec((None, t_in, tc), lambda b, j: (b, 0, j)),
        name="conv_rope", compiler_params=_cp(("parallel", "parallel")))(x, w, b, cos, sin)


def _conv_ffn(up, w, b, *, segs, rows, tc=256):
    nb, t_in, two_ff = up.shape
    ff = two_ff // 2
    return pl.pallas_call(
        functools.partial(_conv_ffn_body, segs=segs),
        out_shape=jax.ShapeDtypeStruct((nb, rows, ff), BF16), grid=(nb, ff // tc),
        in_specs=[pl.BlockSpec((None, rows, tc), lambda b, j: (b, 0, j))] + _conv_specs(rows, tc, ff // tc, 0),
        out_specs=pl.BlockSpec((None, rows, tc), lambda b, j: (b, 0, j)),
        name="conv_ffn", compiler_params=_cp(("parallel", "parallel")))(up, up, w, b)


def _conv_hyena(pr, w, b, *, xcol, wcol, segs, tc=256):
    nb, t_in, _ = pr.shape
    (_, n_lat), (_, n_ctx) = segs
    wd = HY_WIDTH
    out_rows = (t_in, n_lat, n_ctx, n_lat // 2, n_lat // 2, n_ctx // 2, n_ctx // 2)
    in_specs = [pl.BlockSpec((None, t_in, tc), functools.partial(lambda b, j, o: (b, 0, j + o), o=(xcol + m * wd) // tc))
                for m in range(3)]
    wargs = []
    for m in range(3):
        off = (wcol + m * wd) // tc
        in_specs.append(pl.BlockSpec((CONV_W, tc), functools.partial(lambda b, j, o: (0, j + o), o=off)))
        in_specs.append(pl.BlockSpec((1, tc), functools.partial(lambda b, j, o: (0, j + o), o=off)))
        wargs += [w, b]
    return pl.pallas_call(
        functools.partial(_conv_hyena_body, segs=segs),
        out_shape=tuple(jax.ShapeDtypeStruct((nb, rows, wd), BF16) for rows in out_rows),
        grid=(nb, wd // tc), in_specs=in_specs,
        out_specs=tuple(pl.BlockSpec((None, rows, tc), lambda b, j: (b, 0, j)) for rows in out_rows),
        scratch_shapes=[pltpu.VMEM((t_in, LANES), F32)] * (tc // LANES),
        name="conv_hyena", compiler_params=_cp(("parallel", "parallel")))(pr, pr, pr, *wargs)


def _scan_masks(reverse):
    q = SCAN_Q
    ii = lax.broadcasted_iota(jnp.int32, (q, q), 0)
    jj = lax.broadcasted_iota(jnp.int32, (q, q), 1)
    mask = (jj >= ii) if reverse else (jj <= ii)
    mask_t = (jj <= ii) if reverse else (jj >= ii)
    return mask, mask.astype(F32), mask_t.astype(F32)


def _chunk_map(reverse, n_chunks, n_lat_chunks):
    if reverse:
        return lambda k: n_chunks - 1 - k
    return lambda k: (k + n_lat_chunks) % n_chunks


SSD_GPS = 4


def _ssd_body(*refs, reverse, final):
    (xs_ref, b_ref, c_ref, dtc_ref, dtr_ref, dbc_ref, dbr_ref, alc_ref, alr_ref) = refs[:9]
    if final:
        yf_ref, z_ref, dsk_ref, nw_ref, o_ref, h_ref = refs[9:]
    else:
        o_ref, h_ref = refs[9:]

    @pl.when(pl.program_id(2) == 0)
    def _():
        h_ref[...] = jnp.zeros_like(h_ref)

    groups = range(SSD_GPS)
    h = [h_ref[g] for g in groups]
    for c in (reversed(range(SCAN_CPS)) if reverse else range(SCAN_CPS)):
        h = _ssd_chunk(refs, slice(c * SCAN_Q, (c + 1) * SCAN_Q), h, reverse=reverse, final=final)
    for g in groups:
        h_ref[g] = h[g]


def _ssd_chunk(refs, rs, h_prev, *, reverse, final):
    (xs_ref, b_ref, c_ref, dtc_ref, dtr_ref, dbc_ref, dbr_ref, alc_ref, alr_ref) = refs[:9]
    if final:
        yf_ref, z_ref, dsk_ref, nw_ref, o_ref = refs[9:14]
    else:
        o_ref = refs[9]
    q, gw, hd = SCAN_Q, SSD_GW, SSD_HEAD_DIM
    mask, maskf, mask_tf = _scan_masks(reverse)
    groups = range(SSD_GPS)
    pairs = range(SSD_HPG // 2)
    cols = [slice(g * gw, (g + 1) * gw) for g in groups]
    xs = [xs_ref[rs, cols[g]].astype(F32) for g in groups]
    bm = [b_ref[rs, g * SSD_STATE:(g + 1) * SSD_STATE] for g in groups]
    cm = [c_ref[rs, g * SSD_STATE:(g + 1) * SSD_STATE] for g in groups]
    dt_c = [_softplus(dtc_ref[g, rs, :] + dbc_ref[g]) for g in groups]
    dt_r = [_softplus(dtr_ref[g, :, rs] + dbr_ref[g]) for g in groups]
    da_c = [dt_c[g] * (-jnp.exp(alc_ref[g])) for g in groups]
    da_r = [dt_r[g] * (-jnp.exp(alr_ref[g])) for g in groups]
    acum_c = [_sel_dot(maskf, da_c[g], 3) for g in groups]
    acum_r = [_dot_sel(da_r[g], mask_tf, 3) for g in groups]
    atot = [jnp.sum(da_c[g], axis=0, keepdims=True) for g in groups]

    er = lax.broadcasted_iota(jnp.int32, (SSD_HPG, gw), 0)
    ec = lax.broadcasted_iota(jnp.int32, (SSD_HPG, gw), 1)
    expand = (ec // hd == er).astype(F32)
    wide = [_dot_sel(jnp.concatenate([dt_c[g], jnp.exp(atot[g] - acum_c[g]), jnp.exp(acum_c[g]),
                                      jnp.broadcast_to(jnp.exp(atot[g]), (SUBLANES, SSD_HPG))], axis=0), expand, 1)
            for g in groups]
    xq = [xs[g] * wide[g][0:q] for g in groups]
    cb = [_dot_nt(cm[g], bm[g]) for g in groups]
    y_off = [wide[g][2 * q:3 * q] * _dot(cm[g], h_prev[g].astype(BF16)) for g in groups]
    lane = lax.broadcasted_iota(jnp.int32, (q, 2 * hd), 1)

    def pair_lhs(g, p):
        ms = [(cb[g] * jnp.where(mask, jnp.exp(acum_c[g][:, r:r + 1] - acum_r[g][r:r + 1, :]), 0.0)).astype(BF16)
              for r in (2 * p, 2 * p + 1)]
        return jnp.concatenate(ms, axis=1)

    def pair_rhs(g, p):
        xp = xq[g][:, 2 * hd * p:2 * hd * (p + 1)]
        return jnp.concatenate([jnp.where(lane < hd, xp, 0.0), jnp.where(lane >= hd, xp, 0.0)], axis=0).astype(BF16)

    y_in = [[_dot(pair_lhs(g, p), pair_rhs(g, p)) for p in pairs] for g in groups]
    y = [y_off[g] + jnp.concatenate(y_in[g], axis=1) for g in groups]
    bt = [bm[g].astype(F32).T.astype(BF16) for g in groups]
    upd = [_dot(bt[g], (xq[g] * wide[g][q:2 * q]).astype(BF16)) for g in groups]
    for g in groups:
        if final:
            ytot = y[g] + yf_ref[rs, cols[g]] + xs[g] * dsk_ref[:, cols[g]]
            gated = ytot * _silu(z_ref[rs, cols[g]].astype(F32))
            o = gated * lax.rsqrt(jnp.mean(gated * gated, axis=-1, keepdims=True) + EPS) * nw_ref[:, cols[g]]
            o_ref[rs, cols[g]] = o.astype(o_ref.dtype)
        else:
            o_ref[rs, cols[g]] = y[g]
    return [wide[g][3 * q:3 * q + 1] * h_prev[g] + upd[g] for g in groups]


def _ssd(xbc, dt_c, dt_r, db_c, db_r, al_c, al_r, *, direction, n_lat, final_args=None):
    nb, t, _ = xbc.shape
    q, gw, hpg = SCAN_CPS * SCAN_Q, SSD_GW, SSD_HPG
    assert n_lat % q == 0 and t % q == 0
    nch = t // q
    reverse = direction == 1
    cmap = _chunk_map(reverse, nch, n_lat // q)
    gps = SSD_GPS
    sw, gww = gps * SSD_STATE, gps * gw
    b0 = SSD_WIDTH // sw
    c0 = b0 + SSD_GROUPS // gps
    d = direction
    in_specs = [
        pl.BlockSpec((None, q, gww), lambda b, g, k: (b, cmap(k), g)),
        pl.BlockSpec((None, q, sw), lambda b, g, k: (b, cmap(k), b0 + g)),
        pl.BlockSpec((None, q, sw), lambda b, g, k: (b, cmap(k), c0 + g)),
        pl.BlockSpec((None, None, gps, q, hpg), lambda b, g, k: (b, d, g, cmap(k), 0)),
        pl.BlockSpec((None, None, gps, hpg, q), lambda b, g, k: (b, d, g, 0, cmap(k))),
        pl.BlockSpec((None, gps, 1, hpg), lambda b, g, k: (d, g, 0, 0)),
        pl.BlockSpec((None, gps, hpg, 1), lambda b, g, k: (d, g, 0, 0)),
        pl.BlockSpec((None, gps, 1, hpg), lambda b, g, k: (d, g, 0, 0)),
        pl.BlockSpec((None, gps, hpg, 1), lambda b, g, k: (d, g, 0, 0)),
    ]
    args = [xbc, xbc, xbc, dt_c, dt_r, db_c, db_r, al_c, al_r]
    final = final_args is not None
    if final:
        yf, pr, dskip, norm_w = final_args
        in_specs += [pl.BlockSpec((None, q, gww), lambda b, g, k: (b, cmap(k), g)),
                     pl.BlockSpec((None, q, gww), lambda b, g, k: (b, cmap(k), g)),
                     pl.BlockSpec((1, gww), lambda b, g, k: (0, g)),
                     pl.BlockSpec((1, gww), lambda b, g, k: (0, g))]
        args += [yf, pr, dskip, norm_w]
    out_shape = jax.ShapeDtypeStruct((nb, t, SSD_WIDTH), BF16 if final else F32)
    return pl.pallas_call(
        functools.partial(_ssd_body, reverse=reverse, final=final),
        out_shape=out_shape, grid=(nb, SSD_GROUPS // gps, nch), in_specs=in_specs,
        out_specs=pl.BlockSpec((None, q, gww), lambda b, g, k: (b, cmap(k), g)),
        scratch_shapes=[pltpu.VMEM((gps, SSD_STATE, gw), F32)],
        name="ssd_bwd" if reverse else "ssd_fwd",
        compiler_params=_cp(("parallel", "parallel", "arbitrary")))(*args)


def _hyfilt_body(f_ref, w1_ref, b1_ref, w2_ref, b2_ref, w3_ref, fr_ref, dl_ref, o_ref, *, tl):
    f = f_ref[...]
    h = jnp.sin(fr_ref[0:1, :] * (_dot(f, w1_ref[...], HI) + b1_ref[...]))
    h = jnp.sin(fr_ref[1:2, :] * (_dot(h, w2_ref[...], HI) + b2_ref[...]))
    h = _dot(h, w3_ref[...], HI)
    dec = jnp.exp(-f[:, 0:1] * dl_ref[...])
    row = pl.program_id(0) * tl + lax.broadcasted_iota(jnp.int32, (tl, 1), 0)
    wd = HY_WIDTH
    o_ref[:, 0:wd] = (h[:, 0:wd] * dec).astype(o_ref.dtype)
    o_ref[:, wd:2 * wd] = jnp.where(row == 0, 0.0, h[:, wd:2 * wd] * dec).astype(o_ref.dtype)


def _hyena_filters(length, w1, b1, w2, b2, w3, freq, *, tl=256):
    p = LANES
    t = jnp.linspace(0.0, 1.0, length, dtype=F32)[:, None]
    w = 2.0 * math.pi * jnp.arange(length, dtype=F32)[:, None] / length
    f = jnp.linspace(1e-4, HY_BANDS - 1, HY_BANDS, dtype=F32)[None, :]
    feats = jnp.concatenate([t, jnp.cos(f * w), -jnp.sin(f * w)], axis=-1)
    feats = jnp.concatenate([feats[0::2], feats[1::2]], axis=0)
    feats = jnp.pad(feats, ((0, 0), (0, p - HY_EMB)))
    po = p - HY_ORDER
    w1p = jnp.pad(w1, ((0, p - HY_EMB), (0, po)))
    w2p = jnp.pad(w2, ((0, po), (0, po)))
    w3p = jnp.pad(w3, ((0, po), (0, 0)))
    b1p = jnp.pad(b1[None, :], ((0, 0), (0, po)))
    b2p = jnp.pad(b2[None, :], ((0, 0), (0, po)))
    frp = jnp.pad(freq, ((0, 0), (0, po)))
    deltas = jnp.abs(jnp.linspace(math.log(HY_TARGET) / HY_FAST, math.log(HY_TARGET) / HY_SLOW, HY_WIDTH,
                                  dtype=F32))[None, :]
    full = lambda shape: pl.BlockSpec(shape, lambda i: (0,) * len(shape))
    out = pl.pallas_call(
        functools.partial(_hyfilt_body, tl=tl),
        out_shape=jax.ShapeDtypeStruct((length, 2 * HY_WIDTH), BF16), grid=(length // tl,),
        in_specs=[pl.BlockSpec((tl, p), lambda i: (i, 0)), full((p, p)), full((1, p)), full((p, p)), full((1, p)),
                  full((p, 2 * HY_WIDTH)), full((2, p)), full((1, HY_WIDTH))],
        out_specs=pl.BlockSpec((tl, 2 * HY_WIDTH), lambda i: (i, 0)),
        name="hyena_filters", compiler_params=_cp(("parallel",)))(feats, w1p, b1p, w2p, b2p, w3p, frp, deltas)
    return out[None]


def _dft_matrices(length):
    n = 2 * length
    k = jnp.arange(length, dtype=jnp.int32)[:, None]
    t = jnp.arange(length, dtype=jnp.int32)[None, :]
    step = 1 << (int(math.log2(length)) // 2)
    part = jnp.arange(length // step, dtype=jnp.int32)[:, None]
    ang_hi = ((part * step * t) % n).astype(F32) * (2.0 * math.pi / n)
    ang_lo = ((jnp.arange(step, dtype=jnp.int32)[:, None] * t) % n).astype(F32) * (2.0 * math.pi / n)
    ch, sh = jnp.cos(ang_hi)[:, None, :], jnp.sin(ang_hi)[:, None, :]
    cl, sl = jnp.cos(ang_lo)[None, :, :], jnp.sin(ang_lo)[None, :, :]
    c = (ch * cl - sh * sl).reshape(length, length)
    s = (sh * cl + ch * sl).reshape(length, length)
    alt_t = jnp.where(t % 2 == 0, 1.0, -1.0).astype(F32)
    fwd = jnp.concatenate([c, jnp.where(k == 0, alt_t, -s)], axis=0)
    alt_r = jnp.where(k % 2 == 0, 1.0, -1.0).astype(F32)
    inv_re = jnp.where(t == 0, 1.0 / n, (2.0 / n) * c)
    inv_im = jnp.where(t == 0, alt_r / n, (-2.0 / n) * s)
    inv = jnp.concatenate([inv_re, inv_im], axis=1)
    return fwd.astype(BF16)[None], inv.astype(BF16)[None]


def _tile_interleave(x, axis, hm):
    shp = x.shape
    n = shp[axis] // 2
    x = x.reshape(shp[:axis] + (2, n // hm, hm) + shp[axis + 1:])
    return jnp.swapaxes(x, axis, axis + 1).reshape(shp)


def _twiddle_table(length, hm):
    k = np.arange(length // 2, dtype=np.float64)[:, None] * (np.pi / length)
    tab = np.concatenate([np.cos(k), np.sin(k)], axis=0)
    tab = tab.reshape(2, -1, hm, 1).swapaxes(0, 1).reshape(length, 1)
    return jnp.asarray(np.broadcast_to(tab, (length, LANES)).astype(np.float32))


def _lanes(x, width):
    return jnp.tile(x, (1, width // LANES))


def _parity_banks(e, ot, tw, hm):
    er, ei, otr, oti = e[0:hm], e[hm:], ot[0:hm], ot[hm:]
    c, s = _lanes(tw[0:hm], e.shape[1]), _lanes(tw[hm:], e.shape[1])
    orr = otr * c + oti * s
    oi = oti * c - otr * s
    return (er + orr, ei + oi, er - orr, ei - oi), (c, s)


def _is_bin0(row_blk, hm):
    return (row_blk == 0) & (lax.broadcasted_iota(jnp.int32, (hm, 1), 0) == 0)


def _hy_kbank_body(ef_ref, eb_ref, of_ref, ob_ref, tw_ref, ka_ref, kb_ref, *, hm):
    tw = tw_ref[...]
    (afr, afi, bfr, bfi), _ = _parity_banks(ef_ref[...], of_ref[...], tw, hm)
    (abr, abi, bbr, bbi), _ = _parity_banks(eb_ref[...], ob_ref[...], tw, hm)
    first = _is_bin0(pl.program_id(0), hm)
    kai = jnp.where(first, ef_ref[hm:, :] + eb_ref[hm:, :], afi - abi)
    kbi = jnp.where(first, ob_ref[hm:, :] - of_ref[hm:, :], bfi - bbi)
    ka_ref[...] = jnp.concatenate([afr + abr, kai], axis=0)
    kb_ref[...] = jnp.concatenate([bfr + bbr, kbi], axis=0)


def _hy_kbank(ef, of, tw, *, tm, tc=512):
    _, ln, c2 = ef.shape
    c = c2 // 2
    fspec = pl.BlockSpec((None, tm, tc), lambda i, j: (0, i, j))
    bspec = pl.BlockSpec((None, tm, tc), lambda i, j: (0, i, j + c // tc))
    return pl.pallas_call(
        functools.partial(_hy_kbank_body, hm=tm // 2),
        out_shape=(jax.ShapeDtypeStruct((1, ln, c), F32),) * 2, grid=(ln // tm, c // tc),
        in_specs=[fspec, bspec, fspec, bspec, pl.BlockSpec((tm, LANES), lambda i, j: (i, 0))],
        out_specs=(fspec, fspec), name="hy_kbank",
        compiler_params=_cp(("parallel", "parallel")))(ef, ef, of, of, tw)


def _hy_spectrum_body(f_ref, ze_ref, zo_ref, tw_ref, ka_ref, kb_ref, p_ref, q_ref, *, hm):
    fwd = f_ref[...]
    e, acc = _dot(fwd, ze_ref[...]), _dot(fwd, zo_ref[...])
    row_blk = pl.program_id(2)
    (ar, ai, br, bi), (c, s) = _parity_banks(e, acc, tw_ref[...], hm)
    kar, kai, kbr, kbi = ka_ref[0:hm], ka_ref[hm:], kb_ref[0:hm], kb_ref[hm:]
    yar, yai = ar * kar - ai * kai, ar * kai + ai * kar
    ybr, ybi = br * kbr - bi * kbi, br * kbi + bi * kbr
    qr, qi = yar - ybr, yai - ybi
    pr, pi = yar + ybr, yai + ybi
    qpr, qpi = qr * c - qi * s, qi * c + qr * s
    top = slice(0, SUBLANES)
    first = _is_bin0(row_blk, SUBLANES)
    y0, yl = ar[top] * kar[top], br[top] * kbr[top]
    amr, ami = e[hm:hm + SUBLANES], -acc[hm:hm + SUBLANES]
    ymr, ymi = amr * kai[top] - ami * kbi[top], amr * kbi[top] + ami * kai[top]

    def patch(full, special):
        return jnp.concatenate([jnp.where(first, special, full[top]), full[SUBLANES:]], axis=0)

    pr, pi = patch(pr, y0 + yl), patch(pi, 2.0 * ymr)
    qpr, qpi = patch(qpr, y0 - yl), patch(qpi, -2.0 * ymi)
    p_ref[...] = jnp.concatenate([pr, pi], axis=0).astype(p_ref.dtype)
    q_ref[...] = jnp.concatenate([qpr, qpi], axis=0).astype(q_ref.dtype)


def _hy_spectrum(fwd, ze, zo, tw, ka, kb, *, tm, tn):
    nb, h, c = ze.shape
    ln = fwd.shape[1]
    data = pl.BlockSpec((None, h, tn), lambda j, b, i: (b, 0, j))
    bank = pl.BlockSpec((None, tm, tn), lambda j, b, i: (0, i, j))
    out = pl.BlockSpec((None, tm, tn), lambda j, b, i: (b, i, j))
    return pl.pallas_call(
        functools.partial(_hy_spectrum_body, hm=tm // 2),
        out_shape=(jax.ShapeDtypeStruct((nb, ln, c), BF16),) * 2, grid=(c // tn, nb, ln // tm),
        in_specs=[pl.BlockSpec((None, tm, h), lambda j, b, i: (0, i, 0)), data, data,
                  pl.BlockSpec((tm, LANES), lambda j, b, i: (i, 0)), bank, bank],
        out_specs=(out, out), name="hy_spectrum",
        compiler_params=_cp(("parallel", "parallel", "parallel")))(fwd, ze, zo, tw, ka, kb)


def _hyena_out_body(*refs, segs):
    ns = len(segs)
    x0_ref, db_ref, o_ref = refs[3 * ns:3 * ns + 3]
    ys_refs = refs[3 * ns + 3:]
    half = CONV_CH // 2
    for si, (r0, n) in enumerate(segs):
        ye_ref, yo_ref, zin_ref = refs[3 * si:3 * si + 3]

        def body(i, carry, r0=r0, ye_ref=ye_ref, yo_ref=yo_ref, zin_ref=zin_ref):
            t0 = pl.multiple_of(i * CONV_CH, CONV_CH)
            hrows = pl.ds(pl.multiple_of(i * half, half), half)
            for g, ys_ref in enumerate(ys_refs):
                lanes = slice(g * LANES, (g + 1) * LANES)
                ys_ref[pl.ds(r0 + t0, half, stride=2), :] = ye_ref[hrows, lanes].astype(F32)
                ys_ref[pl.ds(r0 + t0 + 1, half, stride=2), :] = yo_ref[hrows, lanes].astype(F32)
            rows = pl.ds(r0 + t0, CONV_CH)
            y = jnp.concatenate([ys_ref[rows, :] for ys_ref in ys_refs], axis=1)
            zin = zin_ref[pl.ds(t0, CONV_CH), :].astype(F32)
            o_ref[rows, :] = (x0_ref[rows, :].astype(F32) * (y + zin * db_ref[...])).astype(o_ref.dtype)
            return carry

        lax.fori_loop(0, n // CONV_CH, body, 0, unroll=min(4, n // CONV_CH))


def _hyena_out(parts, x0, dbias, *, segs, tc=256):
    nb, t, c = x0.shape
    in_specs, args = [], []
    for (ye, yo, zin), (_, n) in zip(parts, segs):
        in_specs += [pl.BlockSpec((None, n // 2, tc), lambda b, j: (b, 0, j)),
                     pl.BlockSpec((None, n // 2, tc), lambda b, j: (b, 0, j)),
                     pl.BlockSpec((None, n, tc), lambda b, j: (b, 0, j))]
        args += [ye, yo, zin]
    in_specs += [pl.BlockSpec((None, t, tc), lambda b, j: (b, 0, j)), pl.BlockSpec((1, tc), lambda b, j: (0, j))]
    return pl.pallas_call(
        functools.partial(_hyena_out_body, segs=segs),
        out_shape=jax.ShapeDtypeStruct((nb, t, c), BF16), grid=(nb, c // tc), in_specs=in_specs,
        out_specs=pl.BlockSpec((None, t, tc), lambda b, j: (b, 0, j)),
        scratch_shapes=[pltpu.VMEM((t, LANES), F32)] * (tc // LANES), name="hyena_out",
        compiler_params=_cp(("parallel", "parallel")))(*args, x0, dbias)


def _hyena_longconv(ze, zo, hfb, *, tn=1024):
    nb, h, c = ze.shape
    ln = 2 * h
    tm_s = min(512, ln)
    hm = tm_s // 2
    fwd, inv = _dft_matrices(h)
    fwd = _tile_interleave(fwd, 1, hm)
    inv = _tile_interleave(inv * 0.5, 2, hm)
    tw = _twiddle_table(ln, hm)
    tm_f, tk_f = min(1024, ln), min(2048, h)
    ef = _matmul(fwd, hfb[:, :h], tm=tm_f, tn=tn, tk=tk_f, out_dtype=F32, name="hy_filter_dft_even")
    of = _matmul(fwd, hfb[:, h:], tm=tm_f, tn=tn, tk=tk_f, out_dtype=F32, name="hy_filter_dft_odd")
    ka, kb = _hy_kbank(ef, of, tw, tm=tm_s)
    p, q = _hy_spectrum(fwd, ze, zo, tw, ka, kb, tm=tm_s, tn=tn)
    tm_i, tk_i = min(1024, h), min(2048, ln)
    ye = _matmul(inv, p, tm=tm_i, tn=tn, tk=tk_i, out_dtype=BF16, name="hy_idft_even")
    yo = _matmul(inv, q, tm=tm_i, tn=tn, tk=tk_i, out_dtype=BF16, name="hy_idft_odd")
    return ye, yo


def _mlstm_body(*refs, reverse, final):
    q_ref, k_ref, v_ref, gc_ref, gr_ref = refs[:5]
    if final:
        hf_ref, og_ref, nw_ref, o_ref, ct_ref, n_ref, m_ref = refs[5:]
    else:
        o_ref, ct_ref, n_ref, m_ref = refs[5:]
    nh, dk, dv = ML_HEADS, ML_QK_DIM, ML_V_DIM
    scale = dk ** -0.5

    @pl.when(pl.program_id(1) == 0)
    def _():
        ct_ref[...] = jnp.zeros_like(ct_ref)
        n_ref[...] = jnp.zeros_like(n_ref)
        m_ref[...] = jnp.zeros_like(m_ref)

    state = ([ct_ref[h] for h in range(nh)], n_ref[...], m_ref[...])
    for c in (reversed(range(SCAN_CPS)) if reverse else range(SCAN_CPS)):
        state = _mlstm_chunk(refs, slice(c * SCAN_Q, (c + 1) * SCAN_Q), state, reverse=reverse, final=final)
    for h in range(nh):
        ct_ref[h] = state[0][h]
    n_ref[...] = state[1]
    m_ref[...] = state[2]


def _mlstm_chunk(refs, rs, state, *, reverse, final):
    q_ref, k_ref, v_ref, gc_ref, gr_ref = refs[:5]
    if final:
        hf_ref, og_ref, nw_ref, o_ref = refs[5:9]
    else:
        o_ref = refs[5]
    nh, dk, dv = ML_HEADS, ML_QK_DIM, ML_V_DIM
    scale = dk ** -0.5
    cts, n_all, m_all = state
    gc, gr = gc_ref[rs, :], gr_ref[:, rs]
    li_c, lf_c = gc[:, 0:nh], _log_sigmoid(gc[:, nh:2 * nh])
    li_r, lf_r = gr[0:nh, :], _log_sigmoid(gr[nh:2 * nh, :])
    mask, maskf, mask_tf = _scan_masks(reverse)
    bc_c = _dot(maskf, lf_c, HI)
    bc_r = _dot(lf_r, mask_tf, HI)
    btot_c = jnp.sum(lf_c, axis=0, keepdims=True)
    heads = range(nh)
    qs = [q_ref[rs, dk * h:dk * (h + 1)] for h in heads]
    ks = [k_ref[rs, dk * h:dk * (h + 1)] for h in heads]
    vs = [v_ref[rs, dv * h:dv * (h + 1)] for h in heads]
    m_s = [m_all[h:h + 1, 0:1] for h in heads]
    n_s = [n_all[h:h + 1, :] for h in heads]
    qk = [_dot_nt(qs[h], ks[h]) for h in heads]
    qc = [_dot(qs[h], cts[h].astype(BF16)) for h in heads]
    dmat = [jnp.where(mask, bc_c[:, h:h + 1] - bc_r[h:h + 1, :] + li_r[h:h + 1, :], -jnp.inf) for h in heads]
    inter = [bc_c[:, h:h + 1] + m_s[h] for h in heads]
    m_t = [jnp.maximum(inter[h], jnp.max(dmat[h], axis=1, keepdims=True)) for h in heads]
    s = [qk[h] * scale * jnp.exp(dmat[h] - m_t[h]) for h in heads]
    dec = [jnp.exp(inter[h] - m_t[h]) for h in heads]
    num = [_dot(s[h].astype(BF16), vs[h]) + (dec[h] * scale) * qc[h] for h in heads]
    qn = [jnp.sum(qs[h].astype(F32) * n_s[h], axis=1, keepdims=True) * scale for h in heads]
    den = [jnp.sum(s[h], axis=1, keepdims=True) + dec[h] * qn[h] for h in heads]
    hh = [num[h] / jnp.maximum(jnp.abs(den[h]), jnp.exp(-m_t[h])) for h in heads]

    wst_c = [btot_c[:, h:h + 1] - bc_c[:, h:h + 1] + li_c[:, h:h + 1] for h in heads]
    wst_r = [btot_c[:, h:h + 1] - bc_r[h:h + 1, :] + li_r[h:h + 1, :] for h in heads]
    m_new = [jnp.maximum(btot_c[:, h:h + 1] + m_s[h], jnp.max(wst_r[h], axis=1, keepdims=True)) for h in heads]
    dstate = [jnp.exp(btot_c[:, h:h + 1] + m_s[h] - m_new[h]) for h in heads]
    kw = [ks[h].astype(F32) * jnp.exp(wst_c[h] - m_new[h]) for h in heads]
    upd = [_dot(kw[h].T.astype(BF16), vs[h]) for h in heads]
    n_rows = [dstate[h] * n_s[h] + jnp.sum(kw[h], axis=0, keepdims=True) for h in heads]
    m_rows = [jnp.broadcast_to(m_new[h], (1, LANES)) for h in heads]

    for h in heads:
        cols = slice(dv * h, dv * (h + 1))
        if final:
            hs = hh[h] + hf_ref[rs, cols]
            hn = hs * lax.rsqrt(jnp.mean(hs * hs, axis=-1, keepdims=True) + EPS) * nw_ref[:, cols]
            o_ref[rs, cols] = (hn * _sigmoid(og_ref[rs, cols].astype(F32))).astype(o_ref.dtype)
        else:
            o_ref[rs, cols] = hh[h]

    return ([dstate[h] * cts[h] + upd[h] for h in heads], jnp.concatenate(n_rows, axis=0),
            jnp.concatenate(m_rows, axis=0))


def _mlstm(qk, pr, g_c, g_r, *, direction, n_lat, final_args=None):
    nb, t, _ = qk.shape
    q = SCAN_CPS * SCAN_Q
    assert n_lat % q == 0 and t % q == 0
    nch = t // q
    reverse = direction == 1
    cmap = _chunk_map(reverse, nch, n_lat // q)
    d = direction
    in_specs = [
        pl.BlockSpec((None, q, ML_QK_WIDTH), lambda b, k: (b, cmap(k), 0)),
        pl.BlockSpec((None, q, ML_QK_WIDTH), lambda b, k: (b, cmap(k), 1)),
        pl.BlockSpec((None, q, ML_WIDTH), lambda b, k: (b, cmap(k), 2 * ML_QK_WIDTH // ML_WIDTH)),
        pl.BlockSpec((None, None, q, 2 * ML_HEADS), lambda b, k: (b, d, cmap(k), 0)),
        pl.BlockSpec((None, None, 2 * ML_HEADS, q), lambda b, k: (b, d, 0, cmap(k))),
    ]
    args = [qk, qk, pr, g_c, g_r]
    final = final_args is not None
    if final:
        hf, norm_w = final_args
        in_specs += [pl.BlockSpec((None, q, ML_WIDTH), lambda b, k: (b, cmap(k), 0)),
                     pl.BlockSpec((None, q, ML_WIDTH), lambda b, k: (b, cmap(k), 2 * ML_QK_WIDTH // ML_WIDTH + 1)),
                     pl.BlockSpec((1, ML_WIDTH), lambda b, k: (0, 0))]
        args += [hf, pr, norm_w]
    out_shape = jax.ShapeDtypeStruct((nb, t, ML_WIDTH), BF16 if final else F32)
    return pl.pallas_call(
        functools.partial(_mlstm_body, reverse=reverse, final=final),
        out_shape=out_shape, grid=(nb, nch), in_specs=in_specs,
        out_specs=pl.BlockSpec((None, q, ML_WIDTH), lambda b, k: (b, cmap(k), 0)),
        scratch_shapes=[pltpu.VMEM((ML_HEADS, ML_QK_DIM, ML_V_DIM), F32), pltpu.VMEM((ML_HEADS, ML_QK_DIM), F32),
                        pltpu.VMEM((ML_HEADS, LANES), F32)],
        name="mlstm_bwd" if reverse else "mlstm_fwd",
        compiler_params=_cp(("parallel", "arbitrary")))(*args)


def _na_window_start(j, rows):
    return jnp.clip(j * NA_QROWS - NA_ROWS // 2, 0, rows - NA_KROWS)


def _na_part_offset(j, part, start, clip=jnp.clip):
    first = j * NA_QROWS + part * (NA_QROWS // NA_PARTS) - NA_ROWS // 2
    return clip(first - start, 0, NA_KROWS - NA_PART_KROWS)


def _na_body(q_ref, k_ref, v_ref, tab_ref, qw_ref, kw_ref, o_ref, kn_ref, v1_ref, *, n_lat, n_ctx):
    dh = NA_HEAD_DIM
    scale = dh ** -0.5
    rows = n_lat // GRID_W
    j = pl.program_id(2)
    heads = range(NA_HPS)
    hcols = [slice(hh * dh, (hh + 1) * dh) for hh in heads]

    @pl.when(j == 0)
    def _():
        prep = (n_lat + n_ctx) // NA_PREP_STEPS
        one_hot = (lax.broadcasted_iota(jnp.int32, (prep, LANES), 1) == 0).astype(v1_ref.dtype)

        def body(i, carry):
            r = pl.ds(pl.multiple_of(i * prep, 16), prep)
            for hh in heads:
                kk = k_ref[r, hcols[hh]].astype(F32)
                kn = kk * lax.rsqrt(jnp.mean(kk * kk, axis=-1, keepdims=True) + EPS) * kw_ref[...]
                kn_ref[r, hcols[hh]] = kn.astype(kn_ref.dtype)
                v1_ref[hh, r, 0:dh] = v_ref[r, hcols[hh]]
                v1_ref[hh, r, dh:dh + LANES] = one_hot
            return carry

        lax.fori_loop(0, NA_PREP_STEPS, body, 0)

    start = _na_window_start(j, rows)
    ctx = pl.ds(n_lat, n_ctx)
    rpp = NA_QROWS // NA_PARTS
    pq = rpp * GRID_W
    chains = [(hh, i) for hh in heads for i in range(NA_PARTS)]
    off = [_na_part_offset(j, i, start) for i in range(NA_PARTS)]
    win = [pl.ds(pl.multiple_of((start + off[i]) * GRID_W, GRID_W), NA_PART_KROWS * GRID_W) for i in range(NA_PARTS)]
    qq = [q_ref[i * pq:(i + 1) * pq, hcols[hh]].astype(F32) for hh, i in chains]
    qn = [(x * lax.rsqrt(jnp.mean(x * x, axis=-1, keepdims=True) + EPS) * (qw_ref[...] * scale)).astype(BF16)
          for x in qq]
    sw = [_dot_nt(qn[c], kn_ref[win[i], hcols[hh]]) for c, (hh, i) in enumerate(chains)]
    sc = [_dot_nt(qn[c], kn_ref[ctx, hcols[hh]]) for c, (hh, i) in enumerate(chains)]

    def with_bias(c, hh, i):
        out = []
        for r in range(rpp):
            d0 = start - (j * NA_QROWS + i * rpp + r) + NA_ROWS - 1 + NA_QROWS
            blocks = tab_ref[hh, d0, pl.ds(off[i] // NA_TAB_ROWS, NA_PART_KROWS // NA_TAB_ROWS)]
            bias = jnp.concatenate([blocks[b] for b in range(NA_PART_KROWS // NA_TAB_ROWS)], axis=1)
            out.append(sw[c][r * GRID_W:(r + 1) * GRID_W] + bias)
        return jnp.concatenate(out, axis=0)

    sb = [with_bias(c, hh, i) for c, (hh, i) in enumerate(chains)]
    m = [jnp.maximum(jnp.max(a, axis=1, keepdims=True), jnp.max(b, axis=1, keepdims=True)) for a, b in zip(sb, sc)]
    pw = [jnp.exp(a - mm) for a, mm in zip(sb, m)]
    pc = [jnp.exp(a - mm) for a, mm in zip(sc, m)]
    o = [_dot(pw[c].astype(BF16), v1_ref[hh, win[i], :]) + _dot(pc[c].astype(BF16), v1_ref[hh, ctx, :])
         for c, (hh, i) in enumerate(chains)]
    for c, (hh, i) in enumerate(chains):
        o_ref[i * pq:(i + 1) * pq, hcols[hh]] = (o[c][:, 0:dh] / o[c][:, dh:dh + 1]).astype(o_ref.dtype)


def _na_bias_table(rpb, rows):
    nd = 2 * NA_QROWS
    valid = np.zeros((nd, NA_KROWS), bool)
    seen = np.zeros((nd,), bool)
    for j in range(rows // NA_QROWS):
        start = int(np.clip(j * NA_QROWS - NA_ROWS // 2, 0, rows - NA_KROWS))
        for qr in range(NA_QROWS):
            r = j * NA_QROWS + qr
            rs = int(np.clip(r - NA_ROWS // 2, 0, rows - NA_ROWS))
            d = start - r + NA_ROWS - 1 + NA_QROWS
            v = np.array([(rs <= start + i < rs + NA_ROWS) for i in range(NA_KROWS)])
            assert 0 <= d < nd and (not seen[d] or (valid[d] == v).all())
            valid[d], seen[d] = v, True
            off = int(_na_part_offset(j, qr // (NA_QROWS // NA_PARTS), start, clip=np.clip))
            assert off % NA_TAB_ROWS == 0 and not v[:off].any() and not v[off + NA_PART_KROWS:].any()
    col = np.arange(GRID_W)
    cs = np.clip(col - NA_COLS // 2, 0, GRID_W - NA_COLS)
    in_win = (col[None, :] >= cs[:, None]) & (col[None, :] < cs[:, None] + NA_COLS)
    ndr, ndc, w = 2 * NA_ROWS - 1, 2 * NA_COLS - 1, GRID_W
    dr = (np.arange(nd)[:, None] - NA_QROWS) + np.arange(NA_KROWS)[None, :]
    ok = valid[:, None, :, None] & in_win[None, :, None, :] & (dr >= 0)[:, None, :, None] \
        & (dr < ndr)[:, None, :, None]
    ext = jnp.pad(rpb, ((0, 0), (0, 0), (w, w)), mode="edge")
    c0 = w + NA_COLS - 1
    t1 = jnp.stack([ext[:, :, c0 - qc:c0 - qc + w] for qc in range(w)], axis=2)
    t1p = jnp.pad(t1, ((0, 0), (NA_QROWS, nd + NA_KROWS - 1 - NA_QROWS - ndr), (0, 0), (0, 0)))
    wide = jnp.concatenate([t1p[:, m] for m in range(nd + NA_KROWS - 1)], axis=-1)
    wpad = LANES * pl.cdiv(wide.shape[-1], LANES)
    wide = jnp.pad(wide, ((0, 0), (0, 0), (0, wpad - wide.shape[-1])))
    nblk, bw = NA_KROWS // NA_TAB_ROWS, NA_TAB_ROWS * w
    okf = jnp.asarray(ok.reshape(nd, w, nblk, bw).transpose(0, 2, 1, 3).astype(np.float32))

    def body(wide_ref, ok_ref, o_ref):
        for d in range(nd):
            for b in range(nblk):
                lo = d * w + b * bw
                o_ref[d, b] = jnp.where(ok_ref[d, b] != 0.0, wide_ref[:, lo:lo + bw], -jnp.inf)

    return pl.pallas_call(
        body, out_shape=jax.ShapeDtypeStruct((rpb.shape[0], nd, nblk, w, bw), F32), grid=(rpb.shape[0],),
        in_specs=[pl.BlockSpec((None, w, wpad), lambda h: (h, 0, 0)),
                  pl.BlockSpec((nd, nblk, w, bw), lambda h: (0, 0, 0, 0))],
        out_specs=pl.BlockSpec((None, nd, nblk, w, bw), lambda h: (h, 0, 0, 0, 0)),
        name="na_bias_table", compiler_params=_cp(("parallel",)))(wide, okf)


def _na(pr, tab, q_w, k_w, *, n_lat, n_ctx, qcol):
    nb, t, _ = pr.shape
    dh, nh, hps = NA_HEAD_DIM, NA_HEADS, NA_HPS
    nq = NA_QROWS * GRID_W
    hw = hps * dh
    assert qcol % hw == 0 and nh % hps == 0
    qb = qcol // hw
    return pl.pallas_call(
        functools.partial(_na_body, n_lat=n_lat, n_ctx=n_ctx),
        out_shape=jax.ShapeDtypeStruct((nb, n_lat, NA_WIDTH), BF16), grid=(nb, nh // hps, n_lat // nq),
        in_specs=[pl.BlockSpec((None, nq, hw), lambda b, h, j: (b, j, qb + h)),
                  pl.BlockSpec((None, t, hw), lambda b, h, j: (b, 0, qb + nh // hps + h)),
                  pl.BlockSpec((None, t, hw), lambda b, h, j: (b, 0, qb + 2 * (nh // hps) + h)),
                  pl.BlockSpec((hps,) + tab.shape[1:], lambda b, h, j: (h, 0, 0, 0, 0)),
                  pl.BlockSpec((1, dh), lambda b, h, j: (0, 0)),
                  pl.BlockSpec((1, dh), lambda b, h, j: (0, 0))],
        out_specs=pl.BlockSpec((None, nq, hw), lambda b, h, j: (b, j, h)),
        scratch_shapes=[pltpu.VMEM((t, hw), BF16), pltpu.VMEM((hps, t, dh + LANES), BF16)], name="na",
        compiler_params=_cp(("parallel", "parallel", "arbitrary")))(pr, pr, pr, tab, q_w, k_w)


def _rope_tables(n_lat, n_ctx, width):
    dh = ML_QK_DIM
    nf = dh // 4
    t = jnp.arange(n_lat)
    inv = ROPE_THETA ** (-jnp.arange(nf, dtype=F32) / nf)
    lane = np.arange(dh)
    pos = jnp.where((lane < dh // 2)[None, :], (t // GRID_W)[:, None], (t % GRID_W)[:, None]).astype(F32)
    ang = pos * inv[lane % nf][None, :]
    sign = np.where((lane % (dh // 2)) < nf, -1.0, 1.0).astype(np.float32)
    cos = jnp.concatenate([jnp.cos(ang), jnp.ones((n_ctx, dh), F32)], axis=0)
    sin = jnp.concatenate([jnp.sin(ang) * sign[None, :], jnp.zeros((n_ctx, dh), F32)], axis=0)
    reps = width // dh
    return jnp.tile(cos, (1, reps)), jnp.tile(sin, (1, reps))


def _ada_mod(c, c_ctx, w_all, b_all, layer):
    nb, d = c.shape
    rows = 16
    cv = jnp.concatenate([c, c_ctx[None, :], jnp.zeros((rows - nb - 1, d), F32)], axis=0)
    a = (cv * _sigmoid(cv)).astype(BF16)[None]
    out = _matmul(a, w_all, w_index=layer, tm=rows, tn=1024, tk=d, out_dtype=F32, name="ada_mod")[0]
    out = out[:nb + 1] + b_all[layer][None, :]
    return out.reshape(nb + 1, 6, d).transpose(1, 0, 2)


def _lat_ctx(m, nb):
    ctx = jnp.broadcast_to(m[nb][None, :], (nb, m.shape[1]))
    return jnp.stack([m[:nb], ctx], axis=1)[:, :, None, :]


def _pad_cols(w, n):
    return jnp.pad(w, ((0, 0), (0, n - w.shape[1])))


def _col_row_forms(g, groups):
    nb, t, _ = g.shape
    g5 = g.reshape(nb, t, 2, groups, -1)
    return g5.transpose(0, 2, 3, 1, 4), g5.transpose(0, 2, 3, 4, 1)


ROW_TILES = (1088, 1024, 512, 256)


def _row_tile(rows):
    return next(tm for tm in ROW_TILES if rows % tm == 0)


def _mod_pair(mods, shift, scale, nb):
    return jnp.stack([_lat_ctx(mods[shift], nb), _lat_ctx(mods[scale], nb)], axis=1)


def _in_projection(u, w_all, layer, *, rows, name, cols=None):
    tm = _row_tile(rows)
    tm = 2 * tm if rows % (2 * tm) == 0 else tm
    return _matmul(u, w_all, w_index=layer, cols=cols, tm=tm, tn=1024, tk=u.shape[2], out_dtype=BF16,
                   rows=rows, vmem=VMEM_LIMIT_BIG, name=name)


def _out_projection(a_list, w_all, x, gate, layer, *, rows, n_lat, next_norm, name):
    tm = _row_tile(rows)
    tn = 512 if w_all.shape[1] <= 2 * D_MODEL else 256
    x = _matmul_gated_residual_rows(a_list, w_all, x, gate, w_index=layer, tm=tm, tn=tn, rows=rows, n_lat=n_lat,
                                    name=name)
    if next_norm is None:
        return x, None
    return x, _rmsmod(x, *next_norm, tm=tm, rows=rows, n_lat=n_lat)


def _conv_ffn_block(x, u, mods, w_up_all, conv_w, conv_b, w_down_all, layer, *, rows, n_lat, segs, next_norm):
    nb = x.shape[0]
    up = _in_projection(u, w_up_all, layer, rows=rows, name="ffn_up")
    hid = _conv_ffn(up, conv_w, conv_b[None, :], segs=segs, rows=rows)
    return _out_projection((hid,), w_down_all, x, _lat_ctx(mods[5], nb), layer, rows=rows, n_lat=n_lat,
                           next_norm=next_norm, name="ffn_down")


def _even_layer(x, u, mods, w_in_all, conv_w, conv_b, dt_bias, a_log, d_skip, ssd_norm_w,
                hy_w1, hy_b1, hy_w2, hy_b2, hy_w3, hy_freq, hy_bias, w_out_all, j, *, n_lat, n_ctx, next_norm):
    nb, t, d = x.shape
    segs = ((0, n_lat), (n_lat, n_ctx))
    tm = _row_tile(t)
    n_main = SSD_WIDTH + EV_CONV_CH
    pr = _in_projection(u, w_in_all, j, rows=t, cols=n_main, name="ev_in")
    w_dt = _pad_cols(w_in_all[j, :, n_main:], LANES).astype(BF16)
    dtp = _matmul(u, w_dt[None], tm=tm, tn=LANES, tk=d, out_dtype=F32, name="ev_in_dt")[:, :, :2 * SSD_HEADS]
    cb = conv_b[None, :]
    xbc = _conv_silu(pr, conv_w, cb, xcol=SSD_WIDTH, wcol=0, width=SSD_XBC, segs=segs)
    x0, zin_l, zin_c, zle, zlo, zce, zco = _conv_hyena(pr, conv_w, cb, xcol=SSD_WIDTH + SSD_XBC, wcol=SSD_XBC,
                                                       segs=segs)

    dt_c, dt_r = _col_row_forms(dtp, SSD_GROUPS)
    db = dt_bias.reshape(2, SSD_GROUPS, 1, SSD_HPG)
    al = a_log.reshape(2, SSD_GROUPS, 1, SSD_HPG)
    scan_args = (xbc, dt_c, dt_r, db, db.transpose(0, 1, 3, 2), al, al.transpose(0, 1, 3, 2))
    yf = _ssd(*scan_args, direction=0, n_lat=n_lat)
    dsk = jnp.repeat(d_skip, SSD_HEAD_DIM)[None, :]
    y_ssd = _ssd(*scan_args, direction=1, n_lat=n_lat, final_args=(yf, pr, dsk, ssd_norm_w[None, :]))

    hyb = hy_bias[None, :]
    filt = functools.partial(_hyena_filters, w1=hy_w1, b1=hy_b1, w2=hy_w2, b2=hy_b2, w3=hy_w3, freq=hy_freq)
    lat = _hyena_longconv(zle, zlo, filt(n_lat)) + (zin_l,)
    ctx = _hyena_longconv(zce, zco, filt(n_ctx)) + (zin_c,)
    y_hy = _hyena_out((lat, ctx), x0, hyb, segs=segs)
    return _out_projection((y_ssd, y_hy), w_out_all, x, _lat_ctx(mods[2], nb), j, rows=t, n_lat=n_lat,
                           next_norm=next_norm, name="ev_out")


def _odd_layer_latent(x, u, mods, w_in, conv_w, conv_b, gate_b, ml_norm_w, q_norm_w, k_norm_w, rpb,
                      w_out_all, j, *, n_lat, n_ctx, next_norm):
    nb, t, d = x.shape
    segs = ((0, n_lat), (n_lat, n_ctx))
    tm = _row_tile(t)
    o3 = 2 * ML_QK_WIDTH + 2 * ML_WIDTH
    o4 = o3 + 4 * ML_HEADS
    wb = w_in.astype(BF16)
    w_main = jnp.concatenate([wb[:, :o3], wb[:, o4:]], axis=1)
    pr = _in_projection(u, w_main[None], 0, rows=t, name="od_in")
    gates = _matmul(u, _pad_cols(wb[:, o3:o4], LANES)[None], tm=tm, tn=LANES, tk=d, out_dtype=F32,
                    name="od_in_gates")[:, :, :4 * ML_HEADS] + gate_b.reshape(-1)
    cos, sin = _rope_tables(n_lat, n_ctx, 256)
    qk = _conv_rope(pr, conv_w, conv_b[None, :], cos, sin, width=2 * ML_QK_WIDTH, segs=segs)
    g5 = gates.reshape(nb, t, 2, 2 * ML_HEADS)
    g_c, g_r = g5.transpose(0, 2, 1, 3), g5.transpose(0, 2, 3, 1)
    hf = _mlstm(qk, pr, g_c, g_r, direction=0, n_lat=n_lat)
    y_ml = _mlstm(qk, pr, g_c, g_r, direction=1, n_lat=n_lat, final_args=(hf, ml_norm_w[None, :]))
    tab = _na_bias_table(rpb, n_lat // GRID_W)
    y_na = _na(pr, tab, q_norm_w[None, :], k_norm_w[None, :], n_lat=n_lat, n_ctx=n_ctx, qcol=o3)
    return _out_projection((y_ml, y_na), w_out_all, x, _lat_ctx(mods[2], nb), j, rows=n_lat, n_lat=n_lat,
                           next_norm=next_norm, name="od_out")


def kernel(x, c, ctx, c_ctx, ada_w, ada_b, norm_w, ev_w_in, ev_conv_w, ev_conv_b, ssd_dt_bias, ssd_a_log, ssd_d,
           ssd_norm_w, hy_w1, hy_b1, hy_w2, hy_b2, hy_w3, hy_freq, hy_bias, ev_w_out, od_w_in, ml_conv_w, ml_conv_b,
           ml_gate_b, ml_norm_w, na_q_norm_w, na_k_norm_w, na_rpb, od_w_out, ffn_w_up, ffn_conv_w, ffn_conv_b,
           ffn_w_down):
    depth = ada_w.shape[0]
    assert depth == 2, "layer schedule below is written for one even and one odd layer"
    n_lat, n_ctx = x.shape[1], ctx.shape[1]
    t = n_lat + n_ctx
    both = ((0, n_lat), (n_lat, n_ctx))
    xs = jnp.concatenate([x, ctx], axis=1)

    nb = x.shape[0]
    w_down = ffn_w_down.astype(BF16)
    mods0 = _ada_mod(c, c_ctx, ada_w, ada_b, 0)
    mods1 = _ada_mod(c, c_ctx, ada_w, ada_b, 1)
    norm = lambda layer, sub, mods: (norm_w[layer, sub][None, :], _mod_pair(mods, 3 * sub, 3 * sub + 1, nb))

    w0, mod0 = norm(0, 0, mods0)
    u = _rmsmod(xs, w0, mod0, tm=_row_tile(t), rows=t, n_lat=n_lat)
    xs, u = _even_layer(xs, u, mods0, ev_w_in, ev_conv_w[0], ev_conv_b[0], ssd_dt_bias[0], ssd_a_log[0],
                        ssd_d[0], ssd_norm_w[0], hy_w1[0], hy_b1[0], hy_w2[0], hy_b2[0], hy_w3[0], hy_freq[0],
                        hy_bias[0], ev_w_out.astype(BF16), 0, n_lat=n_lat, n_ctx=n_ctx, next_norm=norm(0, 1, mods0))
    xs, u = _conv_ffn_block(xs, u, mods0, ffn_w_up, ffn_conv_w[0], ffn_conv_b[0], w_down, 0,
                            rows=t, n_lat=n_lat, segs=both, next_norm=norm(1, 0, mods1))
    xl, u = _odd_layer_latent(xs, u, mods1, od_w_in[0], ml_conv_w[0], ml_conv_b[0], ml_gate_b[0], ml_norm_w[0],
                              na_q_norm_w[0], na_k_norm_w[0], na_rpb[0], od_w_out.astype(BF16), 0,
                              n_lat=n_lat, n_ctx=n_ctx, next_norm=norm(1, 1, mods1))
    out, _ = _conv_ffn_block(xl, u, mods1, ffn_w_up, ffn_conv_w[1], ffn_conv_b[1], w_down, 1,
                             rows=n_lat, n_lat=n_lat, segs=((0, n_lat),), next_norm=None)
    return out
```

```python
import functools
import math

import numpy as np
import jax
import jax.numpy as jnp
from jax import lax
from jax.experimental import pallas as pl
from jax.experimental.pallas import tpu as pltpu

F32 = jnp.float32
BF16 = jnp.bfloat16
HI = lax.Precision.HIGHEST

D_MODEL = 2048
GRID_W = 64
EPS = 1e-6
CONV_W = 3
SSD_WIDTH = D_MODEL
SSD_HEAD_DIM = 64
SSD_HEADS = SSD_WIDTH // SSD_HEAD_DIM
SSD_STATE = 128
SSD_GROUPS = 4
SSD_HPG = SSD_HEADS // SSD_GROUPS
SSD_GW = SSD_WIDTH // SSD_GROUPS
SSD_XBC = SSD_WIDTH + 2 * SSD_GROUPS * SSD_STATE
HY_WIDTH = D_MODEL
HY_EMB = 33
HY_BANDS = (HY_EMB - 1) // 2
HY_ORDER = 64
HY_FAST = 0.3
HY_SLOW = 1.5
HY_TARGET = 1e-2
ML_HEADS = 8
ML_WIDTH = D_MODEL
ML_V_DIM = ML_WIDTH // ML_HEADS
ML_QK_DIM = ML_V_DIM // 2
ML_QK_WIDTH = ML_HEADS * ML_QK_DIM
ROPE_THETA = 10000.0
NA_WIDTH = D_MODEL
NA_HEAD_DIM = 128
NA_HEADS = NA_WIDTH // NA_HEAD_DIM
NA_ROWS = 8
NA_COLS = 16
NA_QROWS = 8
NA_KROWS = 16
NA_PARTS = 2
NA_HPS = 2
NA_PREP_STEPS = 8
NA_TAB_ROWS = 4
NA_PART_KROWS = 12
EV_CONV_CH = SSD_XBC + 3 * HY_WIDTH
SCAN_Q = 128
SCAN_CPS = 2
LANES = 128
SUBLANES = 8
MIB = 1024 * 1024
V7X_VMEM_BYTES = 64 * MIB
VMEM_LIMIT = 48 * MIB
VMEM_LIMIT_BIG = V7X_VMEM_BYTES - 6 * MIB


def _cp(sem, vmem=VMEM_LIMIT):
    return pltpu.CompilerParams(dimension_semantics=sem, vmem_limit_bytes=vmem)


def _sigmoid(x):
    return 0.5 * jnp.tanh(0.5 * x) + 0.5


def _silu(x):
    return x * _sigmoid(x)


def _softplus(x):
    return jnp.maximum(x, 0.0) + jnp.log(1.0 + jnp.exp(-jnp.abs(x)))


def _log_sigmoid(x):
    return -_softplus(-x)


def _dot(a, b, precision=None):
    return jnp.dot(a, b, preferred_element_type=F32, precision=precision)


def _dot_nt(a, b):
    return lax.dot_general(a, b, (((1,), (1,)), ((), ())), preferred_element_type=F32)


def _split_bf16(x, terms):
    out = []
    for _ in range(terms):
        part = x.astype(BF16)
        out.append(part)
        x = x - part.astype(F32)
    return out


def _dot_sel(x, sel, terms):
    selb = sel.astype(BF16)
    return sum(_dot(part, selb) for part in _split_bf16(x, terms))


def _sel_dot(sel, x, terms):
    selb = sel.astype(BF16)
    return sum(_dot(selb, part) for part in _split_bf16(x, terms))


def _mm_body(a_ref, w_ref, o_ref, *scratch, nk, cache_w):
    if cache_w:
        wb_ref = scratch[0]

        @pl.when((pl.program_id(1) == 0) & (pl.program_id(2) == 0))
        def _():
            wb_ref[...] = w_ref[...].astype(BF16)

        w_val = wb_ref[...]
    else:
        w_val = w_ref[...].astype(BF16)
    part = _dot(a_ref[...].astype(BF16), w_val)
    if nk == 1:
        o_ref[...] = part.astype(o_ref.dtype)
        return
    acc_ref = scratch[0]
    k = pl.program_id(3)

    @pl.when(k == 0)
    def _():
        acc_ref[...] = part

    @pl.when(k > 0)
    def _():
        acc_ref[...] += part

    @pl.when(k == nk - 1)
    def _():
        o_ref[...] = acc_ref[...].astype(o_ref.dtype)


def _matmul(a, w, *, tm, tn, tk, out_dtype, rows=None, cols=None, w_index=None, vmem=VMEM_LIMIT, name="mm"):
    ba, ta, kdim = a.shape
    bw, kdim2, n = w.shape
    assert kdim == kdim2
    nb = ba if w_index is not None else max(ba, bw)
    rows = ta if rows is None else rows
    n = n if cols is None else cols
    assert rows % tm == 0 and n % tn == 0 and kdim % tk == 0
    ni, nj, nk = rows // tm, n // tn, kdim // tk
    a_map = (lambda j, b, i, k: (b, i, k)) if ba > 1 else (lambda j, b, i, k: (0, i, k))
    if w_index is not None:
        w_map = lambda j, b, i, k: (w_index, k, j)
    else:
        w_map = (lambda j, b, i, k: (b, k, j)) if bw > 1 else (lambda j, b, i, k: (0, k, j))
    cache_w = w.dtype != BF16 and nk == 1 and (w_index is not None or bw == 1)
    scratch = [pltpu.VMEM((tm, tn), F32)] if nk > 1 else []
    if cache_w:
        scratch = [pltpu.VMEM((tk, tn), BF16)]
    inner = "arbitrary" if cache_w else "parallel"
    return pl.pallas_call(
        functools.partial(_mm_body, nk=nk, cache_w=cache_w),
        out_shape=jax.ShapeDtypeStruct((nb, rows, n), out_dtype), grid=(nj, nb, ni, nk),
        in_specs=[pl.BlockSpec((None, tm, tk), a_map), pl.BlockSpec((None, tk, tn), w_map)],
        out_specs=pl.BlockSpec((None, tm, tn), lambda j, b, i, k: (b, i, j)),
        scratch_shapes=scratch, name=name,
        compiler_params=_cp(("parallel", inner, inner, "arbitrary"), vmem))(a, w)


def _row_is_lat(row_blk, tm, n_lat):
    rows = row_blk * tm + lax.broadcasted_iota(jnp.int32, (tm, 1), 0)
    return rows < n_lat


def _gated_residual_epilogue(acc, row_blk, res_ref, gate_ref, *, tm, n_lat):
    gate = jnp.where(_row_is_lat(row_blk, tm, n_lat), gate_ref[0], gate_ref[1])
    return res_ref[...] + gate * acc


def _gated_residual_rows_body(*refs, ks, tm, n_lat):
    n_a = len(ks)
    a_refs, w_ref, res_ref, gate_ref, o_ref = refs[:n_a], refs[n_a], refs[n_a + 1], refs[n_a + 2], refs[n_a + 3]
    acc, lo = None, 0
    for a_ref, kk in zip(a_refs, ks):
        part = _dot(a_ref[...], w_ref[lo:lo + kk, :])
        acc = part if acc is None else acc + part
        lo += kk
    o_ref[...] = _gated_residual_epilogue(acc, pl.program_id(1), res_ref, gate_ref, tm=tm, n_lat=n_lat)


def _matmul_gated_residual_rows(a_list, w, res, gate, *, w_index, tm, tn, rows, n_lat, name):
    nb = a_list[0].shape[0]
    _, kdim, n = w.shape
    ks = tuple(x.shape[2] for x in a_list)
    assert sum(ks) == kdim and rows % tm == 0 and n % tn == 0
    in_specs = [pl.BlockSpec((None, tm, kk), lambda b, i, j: (b, i, 0)) for kk in ks]
    in_specs += [pl.BlockSpec((None, kdim, tn), lambda b, i, j: (w_index, 0, j)),
                 pl.BlockSpec((None, tm, tn), lambda b, i, j: (b, i, j)),
                 pl.BlockSpec((None, 2, 1, tn), lambda b, i, j: (b, 0, 0, j))]
    return pl.pallas_call(
        functools.partial(_gated_residual_rows_body, ks=ks, tm=tm, n_lat=n_lat),
        out_shape=jax.ShapeDtypeStruct((nb, rows, n), F32), grid=(nb, rows // tm, n // tn), in_specs=in_specs,
        out_specs=pl.BlockSpec((None, tm, tn), lambda b, i, j: (b, i, j)), name=name,
        compiler_params=_cp(("parallel", "parallel", "parallel")))(*a_list, w, res, gate)


def _modulated_rmsnorm(x, is_lat, w_ref, mod_ref):
    y = x * lax.rsqrt(jnp.mean(x * x, axis=-1, keepdims=True) + EPS) * w_ref[...]
    shift = jnp.where(is_lat, mod_ref[0, 0], mod_ref[0, 1])
    scale = jnp.where(is_lat, mod_ref[1, 0], mod_ref[1, 1])
    return y * (1.0 + scale) + shift


def _rmsmod_body(x_ref, w_ref, mod_ref, o_ref, *, tm, n_lat):
    is_lat = _row_is_lat(pl.program_id(2), tm, n_lat)
    o_ref[...] = _modulated_rmsnorm(x_ref[...], is_lat, w_ref, mod_ref).astype(o_ref.dtype)


def _rmsmod(x, w, mod, *, tm, rows, n_lat):
    nb, _, d = x.shape
    tm = tm // 2
    assert rows % tm == 0 and tm % 16 == 0
    body = functools.partial(_rmsmod_body, tm=tm, n_lat=n_lat)
    return pl.pallas_call(
        body, out_shape=jax.ShapeDtypeStruct((nb, rows, d), BF16), grid=(1, nb, rows // tm),
        in_specs=[pl.BlockSpec((None, tm, d), lambda j, b, i: (b, i, 0)),
                  pl.BlockSpec((1, d), lambda j, b, i: (0, 0)),
                  pl.BlockSpec((None, 2, 2, 1, d), lambda j, b, i: (b, 0, 0, 0, 0))],
        out_specs=pl.BlockSpec((None, tm, d), lambda j, b, i: (b, i, 0)),
        name="rmsmod", compiler_params=_cp(("parallel", "parallel", "parallel")))(x, w, mod)


CONV_CH = 128
CONV_HALO = 16
CONV_STEPS = 4


def _shift_matrices():
    r = lax.broadcasted_iota(jnp.int32, (CONV_CH, CONV_CH + 2 * CONV_HALO), 0)
    c = lax.broadcasted_iota(jnp.int32, (CONV_CH, CONV_CH + 2 * CONV_HALO), 1)
    return (c == r + CONV_HALO - 1).astype(BF16), (c == r + CONV_HALO + 1).astype(BF16)


def _chunk_rows(r0, i):
    return pl.ds(r0 + pl.multiple_of(i * CONV_CH, CONV_CH), CONV_CH)


def _taps(w_ref, b_ref, scale=1.0):
    return w_ref[...] * scale, b_ref[...] * scale


def _silu_of_half(h):
    return h + h * jnp.tanh(h)


def _conv_at(x_ref, taps, shifts, r0, n, steps):
    w, b = taps
    cur = [x_ref[_chunk_rows(r0, i), :] for i in steps]
    zero = jnp.zeros((CONV_HALO, cur[0].shape[1]), cur[0].dtype)
    lo = [pl.multiple_of(jnp.maximum(i * CONV_CH - CONV_HALO, 0), CONV_HALO) for i in steps]
    hi = [pl.multiple_of(jnp.minimum((i + 1) * CONV_CH, n - CONV_HALO), CONV_HALO) for i in steps]
    tail = [jnp.where(i > 0, x_ref[pl.ds(r0 + l, CONV_HALO), :], zero) for i, l in zip(steps, lo)]
    head = [jnp.where(i < n // CONV_CH - 1, x_ref[pl.ds(r0 + h, CONV_HALO), :], zero) for i, h in zip(steps, hi)]
    win = [jnp.concatenate([t, c, h], axis=0) for t, c, h in zip(tail, cur, head)]
    prev = [_dot(shifts[0], w) for w in win]
    nxt = [_dot(shifts[1], w) for w in win]
    return [w[0:1, :] * p + w[1:2, :] * c.astype(F32) + w[2:3, :] * x + b for p, c, x in zip(prev, cur, nxt)]


def _conv_loop(segs, fn):
    for si, (r0, n) in enumerate(segs):
        def body(i, carry, r0=r0, n=n, si=si):
            fn(si, r0, n, [i])
            return carry

        lax.fori_loop(0, n // CONV_CH, body, 0, unroll=min(CONV_STEPS, n // CONV_CH))


def _conv_silu_body(x_ref, w_ref, b_ref, o_ref, *, segs):
    shifts = _shift_matrices()
    half_taps = _taps(w_ref, b_ref, 0.5)

    def fn(si, r0, n, steps):
        ys = [_silu_of_half(h) for h in _conv_at(x_ref, half_taps, shifts, r0, n, steps)]
        for i, y in zip(steps, ys):
            o_ref[_chunk_rows(r0, i), :] = y.astype(o_ref.dtype)

    _conv_loop(segs, fn)


def _conv_rope_body(x_ref, w_ref, b_ref, cos_ref, sin_ref, o_ref, *, segs, tc):
    shifts = _shift_matrices()
    lane = lax.broadcasted_iota(jnp.int32, (CONV_CH, tc), 1)
    first = (lane % (ML_QK_DIM // 2)) < (ML_QK_DIM // 4)
    nf = ML_QK_DIM // 4
    half_taps = _taps(w_ref, b_ref, 0.5)

    def fn(si, r0, n, steps):
        ys = [_silu_of_half(h) for h in _conv_at(x_ref, half_taps, shifts, r0, n, steps)]
        partner = [jnp.where(first, pltpu.roll(y, tc - nf, 1), pltpu.roll(y, nf, 1)) for y in ys]
        for i, y, p in zip(steps, ys, partner):
            rows = _chunk_rows(r0, i)
            o_ref[rows, :] = (y * cos_ref[rows, :] + p * sin_ref[rows, :]).astype(o_ref.dtype)

    _conv_loop(segs, fn)


def _conv_ffn_body(a_ref, x_ref, w_ref, b_ref, o_ref, *, segs):
    shifts = _shift_matrices()
    half_taps = _taps(w_ref, b_ref, 0.5)

    def fn(si, r0, n, steps):
        gs = [_silu_of_half(h) for h in _conv_at(x_ref, half_taps, shifts, r0, n, steps)]
        for i, g in zip(steps, gs):
            rows = _chunk_rows(r0, i)
            o_ref[rows, :] = (a_ref[rows, :].astype(F32) * g).astype(o_ref.dtype)

    _conv_loop(segs, fn)


def _conv_hyena_body(x0_ref, x1_ref, v_ref, w0_ref, b0_ref, w1_ref, b1_ref, w2_ref, b2_ref,
                     ox0_ref, ozl_ref, ozc_ref, ozle_ref, ozlo_ref, ozce_ref, ozco_ref, *zs_refs, segs):
    shifts = _shift_matrices()
    taps = (_taps(w0_ref, b0_ref), _taps(w1_ref, b1_ref), _taps(w2_ref, b2_ref))
    half = CONV_CH // 2
    outs = ((ozl_ref, ozle_ref, ozlo_ref), (ozc_ref, ozce_ref, ozco_ref))

    def fn(si, r0, n, steps):
        x0 = _conv_at(x0_ref, taps[0], shifts, r0, n, steps)
        x1 = _conv_at(x1_ref, taps[1], shifts, r0, n, steps)
        v = _conv_at(v_ref, taps[2], shifts, r0, n, steps)
        nat_ref, even_ref, odd_ref = outs[si]
        for i, x0_i, x1_i, v_i in zip(steps, x0, x1, v):
            ox0_ref[_chunk_rows(r0, i), :] = x0_i.astype(ox0_ref.dtype)
            zin = x1_i * v_i
            nat_ref[_chunk_rows(0, i), :] = zin.astype(BF16)
            t0 = r0 + pl.multiple_of(i * CONV_CH, CONV_CH)
            hrows = pl.ds(pl.multiple_of(i * half, half), half)
            for g, zs_ref in enumerate(zs_refs):
                lanes = slice(g * LANES, (g + 1) * LANES)
                zs_ref[pl.ds(t0, CONV_CH), :] = zin[:, lanes]
                even_ref[hrows, lanes] = zs_ref[pl.ds(t0, half, stride=2), :].astype(BF16)
                odd_ref[hrows, lanes] = zs_ref[pl.ds(t0 + 1, half, stride=2), :].astype(BF16)

    _conv_loop(segs, fn)


def _conv_specs(t_in, tc, xcol_blk, wcol_blk):
    return [pl.BlockSpec((None, t_in, tc), lambda b, j: (b, 0, j + xcol_blk)),
            pl.BlockSpec((CONV_W, tc), lambda b, j: (0, j + wcol_blk)),
            pl.BlockSpec((1, tc), lambda b, j: (0, j + wcol_blk))]


def _conv_silu(x, w, b, *, xcol, wcol, width, segs, tc=256):
    nb, t_in, _ = x.shape
    return pl.pallas_call(
        functools.partial(_conv_silu_body, segs=segs),
        out_shape=jax.ShapeDtypeStruct((nb, t_in, width), BF16), grid=(nb, width // tc),
        in_specs=_conv_specs(t_in, tc, xcol // tc, wcol // tc),
        out_specs=pl.BlockSpec((None, t_in, tc), lambda b, j: (b, 0, j)),
        name="conv_silu", compiler_params=_cp(("parallel", "parallel")))(x, w, b)


def _conv_rope(x, w, b, cos, sin, *, width, segs, tc=256):
    nb, t_in, _ = x.shape
    tab = pl.BlockSpec((t_in, tc), lambda b, j: (0, 0))
    return pl.pallas_call(
        functools.partial(_conv_rope_body, segs=segs, tc=tc),
        out_shape=jax.ShapeDtypeStruct((nb, t_in, width), BF16), grid=(nb, width // tc),
        in_specs=_conv_specs(t_in, tc, 0, 0) + [tab, tab],
        out_specs=pl.BlockSpec((None, t_in, tc), lambda b, j: (b, 0, j)),
        name="conv_rope", compiler_params=_cp(("parallel", "parallel")))(x, w, b, cos, sin)


def _conv_ffn(up, w, b, *, segs, rows, tc=256):
    nb, t_in, two_ff = up.shape
    ff = two_ff // 2
    return pl.pallas_call(
        functools.partial(_conv_ffn_body, segs=segs),
        out_shape=jax.ShapeDtypeStruct((nb, rows, ff), BF16), grid=(nb, ff // tc),
        in_specs=[pl.BlockSpec((None, rows, tc), lambda b, j: (b, 0, j))] + _conv_specs(rows, tc, ff // tc, 0),
        out_specs=pl.BlockSpec((None, rows, tc), lambda b, j: (b, 0, j)),
        name="conv_ffn", compiler_params=_cp(("parallel", "parallel")))(up, up, w, b)


def _conv_hyena(pr, w, b, *, xcol, wcol, segs, tc=256):
    nb, t_in, _ = pr.shape
    (_, n_lat), (_, n_ctx) = segs
    wd = HY_WIDTH
    out_rows = (t_in, n_lat, n_ctx, n_lat // 2, n_lat // 2, n_ctx // 2, n_ctx // 2)
    in_specs = [pl.BlockSpec((None, t_in, tc), functools.partial(lambda b, j, o: (b, 0, j + o), o=(xcol + m * wd) // tc))
                for m in range(3)]
    wargs = []
    for m in range(3):
        off = (wcol + m * wd) // tc
        in_specs.append(pl.BlockSpec((CONV_W, tc), functools.partial(lambda b, j, o: (0, j + o), o=off)))
        in_specs.append(pl.BlockSpec((1, tc), functools.partial(lambda b, j, o: (0, j + o), o=off)))
        wargs += [w, b]
    return pl.pallas_call(
        functools.partial(_conv_hyena_body, segs=segs),
        out_shape=tuple(jax.ShapeDtypeStruct((nb, rows, wd), BF16) for rows in out_rows),
        grid=(nb, wd // tc), in_specs=in_specs,
        out_specs=tuple(pl.BlockSpec((None, rows, tc), lambda b, j: (b, 0, j)) for rows in out_rows),
        scratch_shapes=[pltpu.VMEM((t_in, LANES), F32)] * (tc // LANES),
        name="conv_hyena", compiler_params=_cp(("parallel", "parallel")))(pr, pr, pr, *wargs)


def _scan_masks(reverse):
    q = SCAN_Q
    ii = lax.broadcasted_iota(jnp.int32, (q, q), 0)
    jj = lax.broadcasted_iota(jnp.int32, (q, q), 1)
    mask = (jj >= ii) if reverse else (jj <= ii)
    mask_t = (jj <= ii) if reverse else (jj >= ii)
    return mask, mask.astype(F32), mask_t.astype(F32)


def _chunk_map(reverse, n_chunks, n_lat_chunks):
    if reverse:
        return lambda k: n_chunks - 1 - k
    return lambda k: (k + n_lat_chunks) % n_chunks


SSD_GPS = 4


def _ssd_body(*refs, reverse, final):
    (xs_ref, b_ref, c_ref, dtc_ref, dtr_ref, dbc_ref, dbr_ref, alc_ref, alr_ref) = refs[:9]
    if final:
        yf_ref, z_ref, dsk_ref, nw_ref, o_ref, h_ref = refs[9:]
    else:
        o_ref, h_ref = refs[9:]

    @pl.when(pl.program_id(2) == 0)
    def _():
        h_ref[...] = jnp.zeros_like(h_ref)

    groups = range(SSD_GPS)
    h = [h_ref[g] for g in groups]
    for c in (reversed(range(SCAN_CPS)) if reverse else range(SCAN_CPS)):
        h = _ssd_chunk(refs, slice(c * SCAN_Q, (c + 1) * SCAN_Q), h, reverse=reverse, final=final)
    for g in groups:
        h_ref[g] = h[g]


def _ssd_chunk(refs, rs, h_prev, *, reverse, final):
    (xs_ref, b_ref, c_ref, dtc_ref, dtr_ref, dbc_ref, dbr_ref, alc_ref, alr_ref) = refs[:9]
    if final:
        yf_ref, z_ref, dsk_ref, nw_ref, o_ref = refs[9:14]
    else:
        o_ref = refs[9]
    q, gw, hd = SCAN_Q, SSD_GW, SSD_HEAD_DIM
    mask, maskf, mask_tf = _scan_masks(reverse)
    groups = range(SSD_GPS)
    pairs = range(SSD_HPG // 2)
    cols = [slice(g * gw, (g + 1) * gw) for g in groups]
    xs = [xs_ref[rs, cols[g]].astype(F32) for g in groups]
    bm = [b_ref[rs, g * SSD_STATE:(g + 1) * SSD_STATE] for g in groups]
    cm = [c_ref[rs, g * SSD_STATE:(g + 1) * SSD_STATE] for g in groups]
    dt_c = [_softplus(dtc_ref[g, rs, :] + dbc_ref[g]) for g in groups]
    dt_r = [_softplus(dtr_ref[g, :, rs] + dbr_ref[g]) for g in groups]
    da_c = [dt_c[g] * (-jnp.exp(alc_ref[g])) for g in groups]
    da_r = [dt_r[g] * (-jnp.exp(alr_ref[g])) for g in groups]
    acum_c = [_sel_dot(maskf, da_c[g], 3) for g in groups]
    acum_r = [_dot_sel(da_r[g], mask_tf, 3) for g in groups]
    atot = [jnp.sum(da_c[g], axis=0, keepdims=True) for g in groups]

    er = lax.broadcasted_iota(jnp.int32, (SSD_HPG, gw), 0)
    ec = lax.broadcasted_iota(jnp.int32, (SSD_HPG, gw), 1)
    expand = (ec // hd == er).astype(F32)
    wide = [_dot_sel(jnp.concatenate([dt_c[g], jnp.exp(atot[g] - acum_c[g]), jnp.exp(acum_c[g]),
                                      jnp.broadcast_to(jnp.exp(atot[g]), (SUBLANES, SSD_HPG))], axis=0), expand, 1)
            for g in groups]
    xq = [xs[g] * wide[g][0:q] for g in groups]
    cb = [_dot_nt(cm[g], bm[g]) for g in groups]
    y_off = [wide[g][2 * q:3 * q] * _dot(cm[g], h_prev[g].astype(BF16)) for g in groups]
    lane = lax.broadcasted_iota(jnp.int32, (q, 2 * hd), 1)

    def pair_lhs(g, p):
        ms = [(cb[g] * jnp.where(mask, jnp.exp(acum_c[g][:, r:r + 1] - acum_r[g][r:r + 1, :]), 0.0)).astype(BF16)
              for r in (2 * p, 2 * p + 1)]
        return jnp.concatenate(ms, axis=1)

    def pair_rhs(g, p):
        xp = xq[g][:, 2 * hd * p:2 * hd * (p + 1)]
        return jnp.concatenate([jnp.where(lane < hd, xp, 0.0), jnp.where(lane >= hd, xp, 0.0)], axis=0).astype(BF16)

    y_in = [[_dot(pair_lhs(g, p), pair_rhs(g, p)) for p in pairs] for g in groups]
    y = [y_off[g] + jnp.concatenate(y_in[g], axis=1) for g in groups]
    bt = [bm[g].astype(F32).T.astype(BF16) for g in groups]
    upd = [_dot(bt[g], (xq[g] * wide[g][q:2 * q]).astype(BF16)) for g in groups]
    for g in groups:
        if final:
            ytot = y[g] + yf_ref[rs, cols[g]] + xs[g] * dsk_ref[:, cols[g]]
            gated = ytot * _silu(z_ref[rs, cols[g]].astype(F32))
            o = gated * lax.rsqrt(jnp.mean(gated * gated, axis=-1, keepdims=True) + EPS) * nw_ref[:, cols[g]]
            o_ref[rs, cols[g]] = o.astype(o_ref.dtype)
        else:
            o_ref[rs, cols[g]] = y[g]
    return [wide[g][3 * q:3 * q + 1] * h_prev[g] + upd[g] for g in groups]


def _ssd(xbc, dt_c, dt_r, db_c, db_r, al_c, al_r, *, direction, n_lat, final_args=None):
    nb, t, _ = xbc.shape
    q, gw, hpg = SCAN_CPS * SCAN_Q, SSD_GW, SSD_HPG
    assert n_lat % q == 0 and t % q == 0
    nch = t // q
    reverse = direction == 1
    cmap = _chunk_map(reverse, nch, n_lat // q)
    gps = SSD_GPS
    sw, gww = gps * SSD_STATE, gps * gw
    b0 = SSD_WIDTH // sw
    c0 = b0 + SSD_GROUPS // gps
    d = direction
    in_specs = [
        pl.BlockSpec((None, q, gww), lambda b, g, k: (b, cmap(k), g)),
        pl.BlockSpec((None, q, sw), lambda b, g, k: (b, cmap(k), b0 + g)),
        pl.BlockSpec((None, q, sw), lambda b, g, k: (b, cmap(k), c0 + g)),
        pl.BlockSpec((None, None, gps, q, hpg), lambda b, g, k: (b, d, g, cmap(k), 0)),
        pl.BlockSpec((None, None, gps, hpg, q), lambda b, g, k: (b, d, g, 0, cmap(k))),
        pl.BlockSpec((None, gps, 1, hpg), lambda b, g, k: (d, g, 0, 0)),
        pl.BlockSpec((None, gps, hpg, 1), lambda b, g, k: (d, g, 0, 0)),
        pl.BlockSpec((None, gps, 1, hpg), lambda b, g, k: (d, g, 0, 0)),
        pl.BlockSpec((None, gps, hpg, 1), lambda b, g, k: (d, g, 0, 0)),
    ]
    args = [xbc, xbc, xbc, dt_c, dt_r, db_c, db_r, al_c, al_r]
    final = final_args is not None
    if final:
        yf, pr, dskip, norm_w = final_args
        in_specs += [pl.BlockSpec((None, q, gww), lambda b, g, k: (b, cmap(k), g)),
                     pl.BlockSpec((None, q, gww), lambda b, g, k: (b, cmap(k), g)),
                     pl.BlockSpec((1, gww), lambda b, g, k: (0, g)),
                     pl.BlockSpec((1, gww), lambda b, g, k: (0, g))]
        args += [yf, pr, dskip, norm_w]
    out_shape = jax.ShapeDtypeStruct((nb, t, SSD_WIDTH), BF16 if final else F32)
    return pl.pallas_call(
        functools.partial(_ssd_body, reverse=reverse, final=final),
        out_shape=out_shape, grid=(nb, SSD_GROUPS // gps, nch), in_specs=in_specs,
        out_specs=pl.BlockSpec((None, q, gww), lambda b, g, k: (b, cmap(k), g)),
        scratch_shapes=[pltpu.VMEM((gps, SSD_STATE, gw), F32)],
        name="ssd_bwd" if reverse else "ssd_fwd",
        compiler_params=_cp(("parallel", "parallel", "arbitrary")))(*args)


def _hyfilt_body(f_ref, w1_ref, b1_ref, w2_ref, b2_ref, w3_ref, fr_ref, dl_ref, o_ref, *, tl):
    f = f_ref[...]
    h = jnp.sin(fr_ref[0:1, :] * (_dot(f, w1_ref[...], HI) + b1_ref[...]))
    h = jnp.sin(fr_ref[1:2, :] * (_dot(h, w2_ref[...], HI) + b2_ref[...]))
    h = _dot(h, w3_ref[...], HI)
    dec = jnp.exp(-f[:, 0:1] * dl_ref[...])
    row = pl.program_id(0) * tl + lax.broadcasted_iota(jnp.int32, (tl, 1), 0)
    wd = HY_WIDTH
    o_ref[:, 0:wd] = (h[:, 0:wd] * dec).astype(o_ref.dtype)
    o_ref[:, wd:2 * wd] = jnp.where(row == 0, 0.0, h[:, wd:2 * wd] * dec).astype(o_ref.dtype)


def _hyena_filters(length, w1, b1, w2, b2, w3, freq, *, tl=256):
    p = LANES
    t = jnp.linspace(0.0, 1.0, length, dtype=F32)[:, None]
    w = 2.0 * math.pi * jnp.arange(length, dtype=F32)[:, None] / length
    f = jnp.linspace(1e-4, HY_BANDS - 1, HY_BANDS, dtype=F32)[None, :]
    feats = jnp.concatenate([t, jnp.cos(f * w), -jnp.sin(f * w)], axis=-1)
    feats = jnp.concatenate([feats[0::2], feats[1::2]], axis=0)
    feats = jnp.pad(feats, ((0, 0), (0, p - HY_EMB)))
    po = p - HY_ORDER
    w1p = jnp.pad(w1, ((0, p - HY_EMB), (0, po)))
    w2p = jnp.pad(w2, ((0, po), (0, po)))
    w3p = jnp.pad(w3, ((0, po), (0, 0)))
    b1p = jnp.pad(b1[None, :], ((0, 0), (0, po)))
    b2p = jnp.pad(b2[None, :], ((0, 0), (0, po)))
    frp = jnp.pad(freq, ((0, 0), (0, po)))
    deltas = jnp.abs(jnp.linspace(math.log(HY_TARGET) / HY_FAST, math.log(HY_TARGET) / HY_SLOW, HY_WIDTH,
                                  dtype=F32))[None, :]
    full = lambda shape: pl.BlockSpec(shape, lambda i: (0,) * len(shape))
    out = pl.pallas_call(
        functools.partial(_hyfilt_body, tl=tl),
        out_shape=jax.ShapeDtypeStruct((length, 2 * HY_WIDTH), BF16), grid=(length // tl,),
        in_specs=[pl.BlockSpec((tl, p), lambda i: (i, 0)), full((p, p)), full((1, p)), full((p, p)), full((1, p)),
                  full((p, 2 * HY_WIDTH)), full((2, p)), full((1, HY_WIDTH))],
        out_specs=pl.BlockSpec((tl, 2 * HY_WIDTH), lambda i: (i, 0)),
        name="hyena_filters", compiler_params=_cp(("parallel",)))(feats, w1p, b1p, w2p, b2p, w3p, frp, deltas)
    return out[None]


def _dft_matrices(length):
    n = 2 * length
    k = jnp.arange(length, dtype=jnp.int32)[:, None]
    t = jnp.arange(length, dtype=jnp.int32)[None, :]
    step = 1 << (int(math.log2(length)) // 2)
    part = jnp.arange(length // step, dtype=jnp.int32)[:, None]
    ang_hi = ((part * step * t) % n).astype(F32) * (2.0 * math.pi / n)
    ang_lo = ((jnp.arange(step, dtype=jnp.int32)[:, None] * t) % n).astype(F32) * (2.0 * math.pi / n)
    ch, sh = jnp.cos(ang_hi)[:, None, :], jnp.sin(ang_hi)[:, None, :]
    cl, sl = jnp.cos(ang_lo)[None, :, :], jnp.sin(ang_lo)[None, :, :]
    c = (ch * cl - sh * sl).reshape(length, length)
    s = (sh * cl + ch * sl).reshape(length, length)
    alt_t = jnp.where(t % 2 == 0, 1.0, -1.0).astype(F32)
    fwd = jnp.concatenate([c, jnp.where(k == 0, alt_t, -s)], axis=0)
    alt_r = jnp.where(k % 2 == 0, 1.0, -1.0).astype(F32)
    inv_re = jnp.where(t == 0, 1.0 / n, (2.0 / n) * c)
    inv_im = jnp.where(t == 0, alt_r / n, (-2.0 / n) * s)
    inv = jnp.concatenate([inv_re, inv_im], axis=1)
    return fwd.astype(BF16)[None], inv.astype(BF16)[None]


def _tile_interleave(x, axis, hm):
    shp = x.shape
    n = shp[axis] // 2
    x = x.reshape(shp[:axis] + (2, n // hm, hm) + shp[axis + 1:])
    return jnp.swapaxes(x, axis, axis + 1).reshape(shp)


def _twiddle_table(length, hm):
    k = np.arange(length // 2, dtype=np.float64)[:, None] * (np.pi / length)
    tab = np.concatenate([np.cos(k), np.sin(k)], axis=0)
    tab = tab.reshape(2, -1, hm, 1).swapaxes(0, 1).reshape(length, 1)
    return jnp.asarray(np.broadcast_to(tab, (length, LANES)).astype(np.float32))


def _lanes(x, width):
    return jnp.tile(x, (1, width // LANES))


def _parity_banks(e, ot, tw, hm):
    er, ei, otr, oti = e[0:hm], e[hm:], ot[0:hm], ot[hm:]
    c, s = _lanes(tw[0:hm], e.shape[1]), _lanes(tw[hm:], e.shape[1])
    orr = otr * c + oti * s
    oi = oti * c - otr * s
    return (er + orr, ei + oi, er - orr, ei - oi), (c, s)


def _is_bin0(row_blk, hm):
    return (row_blk == 0) & (lax.broadcasted_iota(jnp.int32, (hm, 1), 0) == 0)


def _hy_kbank_body(f_ref, hef_ref, heb_ref, hof_ref, hob_ref, tw_ref, ka_ref, kb_ref, *, hm):
    fwd, tw = f_ref[...], tw_ref[...]
    ef, eb, of, ob = (_dot(fwd, r[...]) for r in (hef_ref, heb_ref, hof_ref, hob_ref))
    (afr, afi, bfr, bfi), _ = _parity_banks(ef, of, tw, hm)
    (abr, abi, bbr, bbi), _ = _parity_banks(eb, ob, tw, hm)
    first = _is_bin0(pl.program_id(0), hm)
    kai = jnp.where(first, ef[hm:] + eb[hm:], afi - abi)
    kbi = jnp.where(first, ob[hm:] - of[hm:], bfi - bbi)
    ka_ref[...] = jnp.concatenate([afr + abr, kai], axis=0)
    kb_ref[...] = jnp.concatenate([bfr + bbr, kbi], axis=0)


def _hy_kbank(fwd, hfb, tw, *, tm, tc=512):
    _, ln, c2 = hfb.shape
    c, h = c2 // 2, ln // 2
    taps = lambda row_blk, col_off: pl.BlockSpec((None, h, tc), lambda i, j: (0, row_blk, j + col_off))
    bank = pl.BlockSpec((None, tm, tc), lambda i, j: (0, i, j))
    return pl.pallas_call(
        functools.partial(_hy_kbank_body, hm=tm // 2),
        out_shape=(jax.ShapeDtypeStruct((1, ln, c), F32),) * 2, grid=(ln // tm, c // tc),
        in_specs=[pl.BlockSpec((None, tm, h), lambda i, j: (0, i, 0)),
                  taps(0, 0), taps(0, c // tc), taps(1, 0), taps(1, c // tc),
                  pl.BlockSpec((tm, LANES), lambda i, j: (i, 0))],
        out_specs=(bank, bank), name="hy_kbank",
        compiler_params=_cp(("parallel", "parallel")))(fwd, hfb, hfb, hfb, hfb, tw)


def _hy_spectrum_body(f_ref, ze_ref, zo_ref, tw_ref, ka_ref, kb_ref, p_ref, q_ref, *, hm):
    fwd = f_ref[...]
    e, acc = _dot(fwd, ze_ref[...]), _dot(fwd, zo_ref[...])
    row_blk = pl.program_id(2)
    (ar, ai, br, bi), (c, s) = _parity_banks(e, acc, tw_ref[...], hm)
    kar, kai, kbr, kbi = ka_ref[0:hm], ka_ref[hm:], kb_ref[0:hm], kb_ref[hm:]
    yar, yai = ar * kar - ai * kai, ar * kai + ai * kar
    ybr, ybi = br * kbr - bi * kbi, br * kbi + bi * kbr
    qr, qi = yar - ybr, yai - ybi
    pr, pi = yar + ybr, yai + ybi
    qpr, qpi = qr * c - qi * s, qi * c + qr * s
    top = slice(0, SUBLANES)
    first = _is_bin0(row_blk, SUBLANES)
    y0, yl = ar[top] * kar[top], br[top] * kbr[top]
    amr, ami = e[hm:hm + SUBLANES], -acc[hm:hm + SUBLANES]
    ymr, ymi = amr * kai[top] - ami * kbi[top], amr * kbi[top] + ami * kai[top]

    def patch(full, special):
        return jnp.concatenate([jnp.where(first, special, full[top]), full[SUBLANES:]], axis=0)

    pr, pi = patch(pr, y0 + yl), patch(pi, 2.0 * ymr)
    qpr, qpi = patch(qpr, y0 - yl), patch(qpi, -2.0 * ymi)
    p_ref[...] = jnp.concatenate([pr, pi], axis=0).astype(p_ref.dtype)
    q_ref[...] = jnp.concatenate([qpr, qpi], axis=0).astype(q_ref.dtype)


def _hy_spectrum(fwd, ze, zo, tw, ka, kb, *, tm, tn):
    nb, h, c = ze.shape
    ln = fwd.shape[1]
    data = pl.BlockSpec((None, h, tn), lambda j, b, i: (b, 0, j))
    bank = pl.BlockSpec((None, tm, tn), lambda j, b, i: (0, i, j))
    out = pl.BlockSpec((None, tm, tn), lambda j, b, i: (b, i, j))
    return pl.pallas_call(
        functools.partial(_hy_spectrum_body, hm=tm // 2),
        out_shape=(jax.ShapeDtypeStruct((nb, ln, c), BF16),) * 2, grid=(c // tn, nb, ln // tm),
        in_specs=[pl.BlockSpec((None, tm, h), lambda j, b, i: (0, i, 0)), data, data,
                  pl.BlockSpec((tm, LANES), lambda j, b, i: (i, 0)), bank, bank],
        out_specs=(out, out), name="hy_spectrum",
        compiler_params=_cp(("parallel", "parallel", "parallel")))(fwd, ze, zo, tw, ka, kb)


def _hyena_out_body(*refs, segs):
    ns = len(segs)
    x0_ref, db_ref, o_ref = refs[3 * ns:3 * ns + 3]
    ys_refs = refs[3 * ns + 3:]
    half = CONV_CH // 2
    for si, (r0, n) in enumerate(segs):
        ye_ref, yo_ref, zin_ref = refs[3 * si:3 * si + 3]

        def body(i, carry, r0=r0, ye_ref=ye_ref, yo_ref=yo_ref, zin_ref=zin_ref):
            t0 = pl.multiple_of(i * CONV_CH, CONV_CH)
            hrows = pl.ds(pl.multiple_of(i * half, half), half)
            for g, ys_ref in enumerate(ys_refs):
                lanes = slice(g * LANES, (g + 1) * LANES)
                ys_ref[pl.ds(r0 + t0, half, stride=2), :] = ye_ref[hrows, lanes].astype(F32)
                ys_ref[pl.ds(r0 + t0 + 1, half, stride=2), :] = yo_ref[hrows, lanes].astype(F32)
            rows = pl.ds(r0 + t0, CONV_CH)
            y = jnp.concatenate([ys_ref[rows, :] for ys_ref in ys_refs], axis=1)
            zin = zin_ref[pl.ds(t0, CONV_CH), :].astype(F32)
            o_ref[rows, :] = (x0_ref[rows, :].astype(F32) * (y + zin * db_ref[...])).astype(o_ref.dtype)
            return carry

        lax.fori_loop(0, n // CONV_CH, body, 0, unroll=min(4, n // CONV_CH))


def _hyena_out(parts, x0, dbias, *, segs, tc=256):
    nb, t, c = x0.shape
    in_specs, args = [], []
    for (ye, yo, zin), (_, n) in zip(parts, segs):
        in_specs += [pl.BlockSpec((None, n // 2, tc), lambda b, j: (b, 0, j)),
                     pl.BlockSpec((None, n // 2, tc), lambda b, j: (b, 0, j)),
                     pl.BlockSpec((None, n, tc), lambda b, j: (b, 0, j))]
        args += [ye, yo, zin]
    in_specs += [pl.BlockSpec((None, t, tc), lambda b, j: (b, 0, j)), pl.BlockSpec((1, tc), lambda b, j: (0, j))]
    return pl.pallas_call(
        functools.partial(_hyena_out_body, segs=segs),
        out_shape=jax.ShapeDtypeStruct((nb, t, c), BF16), grid=(nb, c // tc), in_specs=in_specs,
        out_specs=pl.BlockSpec((None, t, tc), lambda b, j: (b, 0, j)),
        scratch_shapes=[pltpu.VMEM((t, LANES), F32)] * (tc // LANES), name="hyena_out",
        compiler_params=_cp(("parallel", "parallel")))(*args, x0, dbias)


def _hyena_longconv(ze, zo, hfb, *, tn=1024):
    nb, h, c = ze.shape
    ln = 2 * h
    tm_s = min(512, ln)
    hm = tm_s // 2
    fwd, inv = _dft_matrices(h)
    fwd = _tile_interleave(fwd, 1, hm)
    inv = _tile_interleave(inv * 0.5, 2, hm)
    tw = _twiddle_table(ln, hm)
    ka, kb = _hy_kbank(fwd, hfb, tw, tm=tm_s)
    p, q = _hy_spectrum(fwd, ze, zo, tw, ka, kb, tm=tm_s, tn=tn)
    tm_i, tk_i = min(1024, h), ln
    ye = _matmul(inv, p, tm=tm_i, tn=tn, tk=tk_i, out_dtype=BF16, name="hy_idft_even")
    yo = _matmul(inv, q, tm=tm_i, tn=tn, tk=tk_i, out_dtype=BF16, name="hy_idft_odd")
    return ye, yo


def _mlstm_body(*refs, reverse, final):
    q_ref, k_ref, v_ref, gc_ref, gr_ref = refs[:5]
    if final:
        hf_ref, og_ref, nw_ref, o_ref, ct_ref, n_ref, m_ref = refs[5:]
    else:
        o_ref, ct_ref, n_ref, m_ref = refs[5:]
    nh, dk, dv = ML_HEADS, ML_QK_DIM, ML_V_DIM
    scale = dk ** -0.5

    @pl.when(pl.program_id(1) == 0)
    def _():
        ct_ref[...] = jnp.zeros_like(ct_ref)
        n_ref[...] = jnp.zeros_like(n_ref)
        m_ref[...] = jnp.zeros_like(m_ref)

    state = ([ct_ref[h] for h in range(nh)], n_ref[...], m_ref[...])
    for c in (reversed(range(SCAN_CPS)) if reverse else range(SCAN_CPS)):
        state = _mlstm_chunk(refs, slice(c * SCAN_Q, (c + 1) * SCAN_Q), state, reverse=reverse, final=final)
    for h in range(nh):
        ct_ref[h] = state[0][h]
    n_ref[...] = state[1]
    m_ref[...] = state[2]


def _mlstm_chunk(refs, rs, state, *, reverse, final):
    q_ref, k_ref, v_ref, gc_ref, gr_ref = refs[:5]
    if final:
        hf_ref, og_ref, nw_ref, o_ref = refs[5:9]
    else:
        o_ref = refs[5]
    nh, dk, dv = ML_HEADS, ML_QK_DIM, ML_V_DIM
    scale = dk ** -0.5
    cts, n_all, m_all = state
    gc, gr = gc_ref[rs, :], gr_ref[:, rs]
    li_c, lf_c = gc[:, 0:nh], _log_sigmoid(gc[:, nh:2 * nh])
    li_r, lf_r = gr[0:nh, :], _log_sigmoid(gr[nh:2 * nh, :])
    mask, maskf, mask_tf = _scan_masks(reverse)
    bc_c = _dot(maskf, lf_c, HI)
    bc_r = _dot(lf_r, mask_tf, HI)
    btot_c = jnp.sum(lf_c, axis=0, keepdims=True)
    heads = range(nh)
    qs = [q_ref[rs, dk * h:dk * (h + 1)] for h in heads]
    ks = [k_ref[rs, dk * h:dk * (h + 1)] for h in heads]
    vs = [v_ref[rs, dv * h:dv * (h + 1)] for h in heads]
    m_s = [m_all[h:h + 1, 0:1] for h in heads]
    n_s = [n_all[h:h + 1, :] for h in heads]
    qk = [_dot_nt(qs[h], ks[h]) for h in heads]
    qc = [_dot(qs[h], cts[h].astype(BF16)) for h in heads]
    dmat = [jnp.where(mask, bc_c[:, h:h + 1] - bc_r[h:h + 1, :] + li_r[h:h + 1, :], -jnp.inf) for h in heads]
    inter = [bc_c[:, h:h + 1] + m_s[h] for h in heads]
    m_t = [jnp.maximum(inter[h], jnp.max(dmat[h], axis=1, keepdims=True)) for h in heads]
    s = [qk[h] * scale * jnp.exp(dmat[h] - m_t[h]) for h in heads]
    dec = [jnp.exp(inter[h] - m_t[h]) for h in heads]
    num = [_dot(s[h].astype(BF16), vs[h]) + (dec[h] * scale) * qc[h] for h in heads]
    qn = [jnp.sum(qs[h].astype(F32) * n_s[h], axis=1, keepdims=True) * scale for h in heads]
    den = [jnp.sum(s[h], axis=1, keepdims=True) + dec[h] * qn[h] for h in heads]
    hh = [num[h] / jnp.maximum(jnp.abs(den[h]), jnp.exp(-m_t[h])) for h in heads]

    wst_c = [btot_c[:, h:h + 1] - bc_c[:, h:h + 1] + li_c[:, h:h + 1] for h in heads]
    wst_r = [btot_c[:, h:h + 1] - bc_r[h:h + 1, :] + li_r[h:h + 1, :] for h in heads]
    m_new = [jnp.maximum(btot_c[:, h:h + 1] + m_s[h], jnp.max(wst_r[h], axis=1, keepdims=True)) for h in heads]
    dstate = [jnp.exp(btot_c[:, h:h + 1] + m_s[h] - m_new[h]) for h in heads]
    kw = [ks[h].astype(F32) * jnp.exp(wst_c[h] - m_new[h]) for h in heads]
    upd = [_dot(kw[h].T.astype(BF16), vs[h]) for h in heads]
    n_rows = [dstate[h] * n_s[h] + jnp.sum(kw[h], axis=0, keepdims=True) for h in heads]
    m_rows = [jnp.broadcast_to(m_new[h], (1, LANES)) for h in heads]

    for h in heads:
        cols = slice(dv * h, dv * (h + 1))
        if final:
            hs = hh[h] + hf_ref[rs, cols]
            hn = hs * lax.rsqrt(jnp.mean(hs * hs, axis=-1, keepdims=True) + EPS) * nw_ref[:, cols]
            o_ref[rs, cols] = (hn * _sigmoid(og_ref[rs, cols].astype(F32))).astype(o_ref.dtype)
        else:
            o_ref[rs, cols] = hh[h]

    return ([dstate[h] * cts[h] + upd[h] for h in heads], jnp.concatenate(n_rows, axis=0),
            jnp.concatenate(m_rows, axis=0))


def _mlstm(qk, pr, g_c, g_r, *, direction, n_lat, final_args=None):
    nb, t, _ = qk.shape
    q = SCAN_CPS * SCAN_Q
    assert n_lat % q == 0 and t % q == 0
    nch = t // q
    reverse = direction == 1
    cmap = _chunk_map(reverse, nch, n_lat // q)
    d = direction
    in_specs = [
        pl.BlockSpec((None, q, ML_QK_WIDTH), lambda b, k: (b, cmap(k), 0)),
        pl.BlockSpec((None, q, ML_QK_WIDTH), lambda b, k: (b, cmap(k), 1)),
        pl.BlockSpec((None, q, ML_WIDTH), lambda b, k: (b, cmap(k), 2 * ML_QK_WIDTH // ML_WIDTH)),
        pl.BlockSpec((None, None, q, 2 * ML_HEADS), lambda b, k: (b, d, cmap(k), 0)),
        pl.BlockSpec((None, None, 2 * ML_HEADS, q), lambda b, k: (b, d, 0, cmap(k))),
    ]
    args = [qk, qk, pr, g_c, g_r]
    final = final_args is not None
    if final:
        hf, norm_w = final_args
        in_specs += [pl.BlockSpec((None, q, ML_WIDTH), lambda b, k: (b, cmap(k), 0)),
                     pl.BlockSpec((None, q, ML_WIDTH), lambda b, k: (b, cmap(k), 2 * ML_QK_WIDTH // ML_WIDTH + 1)),
                     pl.BlockSpec((1, ML_WIDTH), lambda b, k: (0, 0))]
        args += [hf, pr, norm_w]
    out_shape = jax.ShapeDtypeStruct((nb, t, ML_WIDTH), BF16 if final else F32)
    return pl.pallas_call(
        functools.partial(_mlstm_body, reverse=reverse, final=final),
        out_shape=out_shape, grid=(nb, nch), in_specs=in_specs,
        out_specs=pl.BlockSpec((None, q, ML_WIDTH), lambda b, k: (b, cmap(k), 0)),
        scratch_shapes=[pltpu.VMEM((ML_HEADS, ML_QK_DIM, ML_V_DIM), F32), pltpu.VMEM((ML_HEADS, ML_QK_DIM), F32),
                        pltpu.VMEM((ML_HEADS, LANES), F32)],
        name="mlstm_bwd" if reverse else "mlstm_fwd",
        compiler_params=_cp(("parallel", "arbitrary")))(*args)


def _na_window_start(j, rows):
    return jnp.clip(j * NA_QROWS - NA_ROWS // 2, 0, rows - NA_KROWS)


def _na_part_offset(j, part, start, clip=jnp.clip):
    first = j * NA_QROWS + part * (NA_QROWS // NA_PARTS) - NA_ROWS // 2
    return clip(first - start, 0, NA_KROWS - NA_PART_KROWS) // NA_TAB_ROWS * NA_TAB_ROWS


def _na_body(q_ref, k_ref, v_ref, tab_ref, qw_ref, kw_ref, o_ref, kn_ref, v1_ref, *, n_lat, n_ctx):
    dh = NA_HEAD_DIM
    scale = dh ** -0.5
    rows = n_lat // GRID_W
    j = pl.program_id(2)
    heads = range(NA_HPS)
    hcols = [slice(hh * dh, (hh + 1) * dh) for hh in heads]

    @pl.when(j == 0)
    def _():
        prep = (n_lat + n_ctx) // NA_PREP_STEPS
        one_hot = (lax.broadcasted_iota(jnp.int32, (prep, LANES), 1) == 0).astype(v1_ref.dtype)

        def body(i, carry):
            r = pl.ds(pl.multiple_of(i * prep, 16), prep)
            for hh in heads:
                kk = k_ref[r, hcols[hh]].astype(F32)
                kn = kk * lax.rsqrt(jnp.mean(kk * kk, axis=-1, keepdims=True) + EPS) * kw_ref[...]
                kn_ref[r, hcols[hh]] = kn.astype(kn_ref.dtype)
                v1_ref[hh, r, 0:dh] = v_ref[r, hcols[hh]]
                v1_ref[hh, r, dh:dh + LANES] = one_hot
            return carry

        lax.fori_loop(0, NA_PREP_STEPS, body, 0)

    start = _na_window_start(j, rows)
    ctx = pl.ds(n_lat, n_ctx)
    rpp = NA_QROWS // NA_PARTS
    pq = rpp * GRID_W
    chains = [(hh, i) for hh in heads for i in range(NA_PARTS)]
    off = [_na_part_offset(j, i, start) for i in range(NA_PARTS)]
    win = [pl.ds(pl.multiple_of((start + off[i]) * GRID_W, GRID_W), NA_PART_KROWS * GRID_W) for i in range(NA_PARTS)]
    qq = [q_ref[i * pq:(i + 1) * pq, hcols[hh]].astype(F32) for hh, i in chains]
    qn = [(x * lax.rsqrt(jnp.mean(x * x, axis=-1, keepdims=True) + EPS) * (qw_ref[...] * scale)).astype(BF16)
          for x in qq]
    sw = [_dot_nt(qn[c], kn_ref[win[i], hcols[hh]]) for c, (hh, i) in enumerate(chains)]
    sc = [_dot_nt(qn[c], kn_ref[ctx, hcols[hh]]) for c, (hh, i) in enumerate(chains)]

    def with_bias(c, hh, i):
        out = []
        for r in range(rpp):
            d0 = start - (j * NA_QROWS + i * rpp + r) + NA_ROWS - 1 + NA_QROWS
            blocks = tab_ref[hh, d0, pl.ds(off[i] // NA_TAB_ROWS, NA_PART_KROWS // NA_TAB_ROWS)]
            bias = jnp.concatenate([blocks[b] for b in range(NA_PART_KROWS // NA_TAB_ROWS)], axis=1)
            out.append(sw[c][r * GRID_W:(r + 1) * GRID_W] + bias)
        return jnp.concatenate(out, axis=0)

    sb = [with_bias(c, hh, i) for c, (hh, i) in enumerate(chains)]
    m = [jnp.maximum(jnp.max(a, axis=1, keepdims=True), jnp.max(b, axis=1, keepdims=True)) for a, b in zip(sb, sc)]
    pw = [jnp.exp(a - mm) for a, mm in zip(sb, m)]
    pc = [jnp.exp(a - mm) for a, mm in zip(sc, m)]
    o = [_dot(pw[c].astype(BF16), v1_ref[hh, win[i], :]) + _dot(pc[c].astype(BF16), v1_ref[hh, ctx, :])
         for c, (hh, i) in enumerate(chains)]
    for c, (hh, i) in enumerate(chains):
        o_ref[i * pq:(i + 1) * pq, hcols[hh]] = (o[c][:, 0:dh] / o[c][:, dh:dh + 1]).astype(o_ref.dtype)


def _na_bias_table(rpb, rows):
    nd = 2 * NA_QROWS
    valid = np.zeros((nd, NA_KROWS), bool)
    seen = np.zeros((nd,), bool)
    for j in range(rows // NA_QROWS):
        start = int(np.clip(j * NA_QROWS - NA_ROWS // 2, 0, rows - NA_KROWS))
        for qr in range(NA_QROWS):
            r = j * NA_QROWS + qr
            rs = int(np.clip(r - NA_ROWS // 2, 0, rows - NA_ROWS))
            d = start - r + NA_ROWS - 1 + NA_QROWS
            v = np.array([(rs <= start + i < rs + NA_ROWS) for i in range(NA_KROWS)])
            assert 0 <= d < nd and (not seen[d] or (valid[d] == v).all())
            valid[d], seen[d] = v, True
            off = int(_na_part_offset(j, qr // (NA_QROWS // NA_PARTS), start, clip=np.clip))
            assert off % NA_TAB_ROWS == 0 and not v[:off].any() and not v[off + NA_PART_KROWS:].any()
    col = np.arange(GRID_W)
    cs = np.clip(col - NA_COLS // 2, 0, GRID_W - NA_COLS)
    in_win = (col[None, :] >= cs[:, None]) & (col[None, :] < cs[:, None] + NA_COLS)
    ndr, ndc, w = 2 * NA_ROWS - 1, 2 * NA_COLS - 1, GRID_W
    dr = (np.arange(nd)[:, None] - NA_QROWS) + np.arange(NA_KROWS)[None, :]
    ok = valid[:, None, :, None] & in_win[None, :, None, :] & (dr >= 0)[:, None, :, None] \
        & (dr < ndr)[:, None, :, None]
    ext = jnp.pad(rpb, ((0, 0), (0, 0), (w, w)), mode="edge")
    c0 = w + NA_COLS - 1
    t1 = jnp.stack([ext[:, :, c0 - qc:c0 - qc + w] for qc in range(w)], axis=2)
    t1p = jnp.pad(t1, ((0, 0), (NA_QROWS, nd + NA_KROWS - 1 - NA_QROWS - ndr), (0, 0), (0, 0)))
    wide = jnp.concatenate([t1p[:, m] for m in range(nd + NA_KROWS - 1)], axis=-1)
    wpad = LANES * pl.cdiv(wide.shape[-1], LANES)
    wide = jnp.pad(wide, ((0, 0), (0, 0), (0, wpad - wide.shape[-1])))
    nblk, bw = NA_KROWS // NA_TAB_ROWS, NA_TAB_ROWS * w
    okf = jnp.asarray(ok.reshape(nd, w, nblk, bw).transpose(0, 2, 1, 3).astype(np.float32))

    def body(wide_ref, ok_ref, o_ref):
        for d in range(nd):
            for b in range(nblk):
                lo = d * w + b * bw
                o_ref[d, b] = jnp.where(ok_ref[d, b] != 0.0, wide_ref[:, lo:lo + bw], -jnp.inf)

    return pl.pallas_call(
        body, out_shape=jax.ShapeDtypeStruct((rpb.shape[0], nd, nblk, w, bw), F32), grid=(rpb.shape[0],),
        in_specs=[pl.BlockSpec((None, w, wpad), lambda h: (h, 0, 0)),
                  pl.BlockSpec((nd, nblk, w, bw), lambda h: (0, 0, 0, 0))],
        out_specs=pl.BlockSpec((None, nd, nblk, w, bw), lambda h: (h, 0, 0, 0, 0)),
        name="na_bias_table", compiler_params=_cp(("parallel",)))(wide, okf)


def _na(pr, tab, q_w, k_w, *, n_lat, n_ctx, qcol):
    nb, t, _ = pr.shape
    dh, nh, hps = NA_HEAD_DIM, NA_HEADS, NA_HPS
    nq = NA_QROWS * GRID_W
    hw = hps * dh
    assert qcol % hw == 0 and nh % hps == 0
    qb = qcol // hw
    return pl.pallas_call(
        functools.partial(_na_body, n_lat=n_lat, n_ctx=n_ctx),
        out_shape=jax.ShapeDtypeStruct((nb, n_lat, NA_WIDTH), BF16), grid=(nb, nh // hps, n_lat // nq),
        in_specs=[pl.BlockSpec((None, nq, hw), lambda b, h, j: (b, j, qb + h)),
                  pl.BlockSpec((None, t, hw), lambda b, h, j: (b, 0, qb + nh // hps + h)),
                  pl.BlockSpec((None, t, hw), lambda b, h, j: (b, 0, qb + 2 * (nh // hps) + h)),
                  pl.BlockSpec((hps,) + tab.shape[1:], lambda b, h, j: (h, 0, 0, 0, 0)),
                  pl.BlockSpec((1, dh), lambda b, h, j: (0, 0)),
                  pl.BlockSpec((1, dh), lambda b, h, j: (0, 0))],
        out_specs=pl.BlockSpec((None, nq, hw), lambda b, h, j: (b, j, h)),
        scratch_shapes=[pltpu.VMEM((t, hw), BF16), pltpu.VMEM((hps, t, dh + LANES), BF16)], name="na",
        compiler_params=_cp(("parallel", "parallel", "arbitrary")))(pr, pr, pr, tab, q_w, k_w)


def _rope_tables(n_lat, n_ctx, width):
    dh = ML_QK_DIM
    nf = dh // 4
    t = jnp.arange(n_lat)
    inv = ROPE_THETA ** (-jnp.arange(nf, dtype=F32) / nf)
    lane = np.arange(dh)
    pos = jnp.where((lane < dh // 2)[None, :], (t // GRID_W)[:, None], (t % GRID_W)[:, None]).astype(F32)
    ang = pos * inv[lane % nf][None, :]
    sign = np.where((lane % (dh // 2)) < nf, -1.0, 1.0).astype(np.float32)
    cos = jnp.concatenate([jnp.cos(ang), jnp.ones((n_ctx, dh), F32)], axis=0)
    sin = jnp.concatenate([jnp.sin(ang) * sign[None, :], jnp.zeros((n_ctx, dh), F32)], axis=0)
    reps = width // dh
    return jnp.tile(cos, (1, reps)), jnp.tile(sin, (1, reps))


def _ada_mod(c, c_ctx, w_all, b_all, layer):
    nb, d = c.shape
    rows = 16
    cv = jnp.concatenate([c, c_ctx[None, :], jnp.zeros((rows - nb - 1, d), F32)], axis=0)
    a = (cv * _sigmoid(cv)).astype(BF16)[None]
    out = _matmul(a, w_all, w_index=layer, tm=rows, tn=1024, tk=d, out_dtype=F32, name="ada_mod")[0]
    out = out[:nb + 1] + b_all[layer][None, :]
    return out.reshape(nb + 1, 6, d).transpose(1, 0, 2)


def _lat_ctx(m, nb):
    ctx = jnp.broadcast_to(m[nb][None, :], (nb, m.shape[1]))
    return jnp.stack([m[:nb], ctx], axis=1)[:, :, None, :]


def _pad_cols(w, n):
    return jnp.pad(w, ((0, 0), (0, n - w.shape[1])))


def _col_row_forms(g, groups):
    nb, t, _ = g.shape
    g5 = g.reshape(nb, t, 2, groups, -1)
    return g5.transpose(0, 2, 3, 1, 4), g5.transpose(0, 2, 3, 4, 1)


ROW_TILES = (1088, 1024, 512, 256)


def _row_tile(rows):
    return next(tm for tm in ROW_TILES if rows % tm == 0)


def _mod_pair(mods, shift, scale, nb):
    return jnp.stack([_lat_ctx(mods[shift], nb), _lat_ctx(mods[scale], nb)], axis=1)


def _in_projection(u, w_all, layer, *, rows, name, cols=None):
    tm = _row_tile(rows)
    tm = 2 * tm if rows % (2 * tm) == 0 else tm
    return _matmul(u, w_all, w_index=layer, cols=cols, tm=tm, tn=1024, tk=u.shape[2], out_dtype=BF16,
                   rows=rows, vmem=VMEM_LIMIT_BIG, name=name)


def _out_projection(a_list, w_all, x, gate, layer, *, rows, n_lat, next_norm, name):
    tm = _row_tile(rows)
    tn = 512 if w_all.shape[1] <= 2 * D_MODEL else 256
    x = _matmul_gated_residual_rows(a_list, w_all, x, gate, w_index=layer, tm=tm, tn=tn, rows=rows, n_lat=n_lat,
                                    name=name)
    if next_norm is None:
        return x, None
    return x, _rmsmod(x, *next_norm, tm=tm, rows=rows, n_lat=n_lat)


def _conv_ffn_block(x, u, mods, w_up_all, conv_w, conv_b, w_down_all, layer, *, rows, n_lat, segs, next_norm):
    nb = x.shape[0]
    up = _in_projection(u, w_up_all, layer, rows=rows, name="ffn_up")
    hid = _conv_ffn(up, conv_w, conv_b[None, :], segs=segs, rows=rows)
    return _out_projection((hid,), w_down_all, x, _lat_ctx(mods[5], nb), layer, rows=rows, n_lat=n_lat,
                           next_norm=next_norm, name="ffn_down")


def _even_layer(x, u, mods, w_in_all, conv_w, conv_b, dt_bias, a_log, d_skip, ssd_norm_w,
                hy_w1, hy_b1, hy_w2, hy_b2, hy_w3, hy_freq, hy_bias, w_out_all, j, *, n_lat, n_ctx, next_norm):
    nb, t, d = x.shape
    segs = ((0, n_lat), (n_lat, n_ctx))
    tm = _row_tile(t)
    n_main = SSD_WIDTH + EV_CONV_CH
    pr = _in_projection(u, w_in_all, j, rows=t, cols=n_main, name="ev_in")
    w_dt = _pad_cols(w_in_all[j, :, n_main:], LANES).astype(BF16)
    dtp = _matmul(u, w_dt[None], tm=tm, tn=LANES, tk=d, out_dtype=F32, name="ev_in_dt")[:, :, :2 * SSD_HEADS]
    cb = conv_b[None, :]
    xbc = _conv_silu(pr, conv_w, cb, xcol=SSD_WIDTH, wcol=0, width=SSD_XBC, segs=segs)
    x0, zin_l, zin_c, zle, zlo, zce, zco = _conv_hyena(pr, conv_w, cb, xcol=SSD_WIDTH + SSD_XBC, wcol=SSD_XBC,
                                                       segs=segs)

    dt_c, dt_r = _col_row_forms(dtp, SSD_GROUPS)
    db = dt_bias.reshape(2, SSD_GROUPS, 1, SSD_HPG)
    al = a_log.reshape(2, SSD_GROUPS, 1, SSD_HPG)
    scan_args = (xbc, dt_c, dt_r, db, db.transpose(0, 1, 3, 2), al, al.transpose(0, 1, 3, 2))
    yf = _ssd(*scan_args, direction=0, n_lat=n_lat)
    dsk = jnp.repeat(d_skip, SSD_HEAD_DIM)[None, :]
    y_ssd = _ssd(*scan_args, direction=1, n_lat=n_lat, final_args=(yf, pr, dsk, ssd_norm_w[None, :]))

    hyb = hy_bias[None, :]
    filt = functools.partial(_hyena_filters, w1=hy_w1, b1=hy_b1, w2=hy_w2, b2=hy_b2, w3=hy_w3, freq=hy_freq)
    lat = _hyena_longconv(zle, zlo, filt(n_lat)) + (zin_l,)
    ctx = _hyena_longconv(zce, zco, filt(n_ctx)) + (zin_c,)
    y_hy = _hyena_out((lat, ctx), x0, hyb, segs=segs)
    return _out_projection((y_ssd, y_hy), w_out_all, x, _lat_ctx(mods[2], nb), j, rows=t, n_lat=n_lat,
                           next_norm=next_norm, name="ev_out")


def _odd_layer_latent(x, u, mods, w_in, conv_w, conv_b, gate_b, ml_norm_w, q_norm_w, k_norm_w, rpb,
                      w_out_all, j, *, n_lat, n_ctx, next_norm):
    nb, t, d = x.shape
    segs = ((0, n_lat), (n_lat, n_ctx))
    tm = _row_tile(t)
    o3 = 2 * ML_QK_WIDTH + 2 * ML_WIDTH
    o4 = o3 + 4 * ML_HEADS
    wb = w_in.astype(BF16)
    w_main = jnp.concatenate([wb[:, :o3], wb[:, o4:]], axis=1)
    pr = _in_projection(u, w_main[None], 0, rows=t, name="od_in")
    gates = _matmul(u, _pad_cols(wb[:, o3:o4], LANES)[None], tm=tm, tn=LANES, tk=d, out_dtype=F32,
                    name="od_in_gates")[:, :, :4 * ML_HEADS] + gate_b.reshape(-1)
    cos, sin = _rope_tables(n_lat, n_ctx, 256)
    qk = _conv_rope(pr, conv_w, conv_b[None, :], cos, sin, width=2 * ML_QK_WIDTH, segs=segs)
    g5 = gates.reshape(nb, t, 2, 2 * ML_HEADS)
    g_c, g_r = g5.transpose(0, 2, 1, 3), g5.transpose(0, 2, 3, 1)
    hf = _mlstm(qk, pr, g_c, g_r, direction=0, n_lat=n_lat)
    y_ml = _mlstm(qk, pr, g_c, g_r, direction=1, n_lat=n_lat, final_args=(hf, ml_norm_w[None, :]))
    tab = _na_bias_table(rpb, n_lat // GRID_W)
    y_na = _na(pr, tab, q_norm_w[None, :], k_norm_w[None, :], n_lat=n_lat, n_ctx=n_ctx, qcol=o3)
    return _out_projection((y_ml, y_na), w_out_all, x, _lat_ctx(mods[2], nb), j, rows=n_lat, n_lat=n_lat,
                           next_norm=next_norm, name="od_out")


def kernel(x, c, ctx, c_ctx, ada_w, ada_b, norm_w, ev_w_in, ev_conv_w, ev_conv_b, ssd_dt_bias, ssd_a_log, ssd_d,
           ssd_norm_w, hy_w1, hy_b1, hy_w2, hy_b2, hy_w3, hy_freq, hy_bias, ev_w_out, od_w_in, ml_conv_w, ml_conv_b,
           ml_gate_b, ml_norm_w, na_q_norm_w, na_k_norm_w, na_rpb, od_w_out, ffn_w_up, ffn_conv_w, ffn_conv_b,
           ffn_w_down):
    depth = ada_w.shape[0]
    assert depth == 2, "layer schedule below is written for one even and one odd layer"
    n_lat, n_ctx = x.shape[1], ctx.shape[1]
    t = n_lat + n_ctx
    both = ((0, n_lat), (n_lat, n_ctx))
    xs = jnp.concatenate([x, ctx], axis=1)

    nb = x.shape[0]
    w_down = ffn_w_down.astype(BF16)
    mods0 = _ada_mod(c, c_ctx, ada_w, ada_b, 0)
    mods1 = _ada_mod(c, c_ctx, ada_w, ada_b, 1)
    norm = lambda layer, sub, mods: (norm_w[layer, sub][None, :], _mod_pair(mods, 3 * sub, 3 * sub + 1, nb))

    w0, mod0 = norm(0, 0, mods0)
    u = _rmsmod(xs, w0, mod0, tm=_row_tile(t), rows=t, n_lat=n_lat)
    xs, u = _even_layer(xs, u, mods0, ev_w_in, ev_conv_w[0], ev_conv_b[0], ssd_dt_bias[0], ssd_a_log[0],
                        ssd_d[0], ssd_norm_w[0], hy_w1[0], hy_b1[0], hy_w2[0], hy_b2[0], hy_w3[0], hy_freq[0],
                        hy_bias[0], ev_w_out.astype(BF16), 0, n_lat=n_lat, n_ctx=n_ctx, next_norm=norm(0, 1, mods0))
    xs, u = _conv_ffn_block(xs, u, mods0, ffn_w_up, ffn_conv_w[0], ffn_conv_b[0], w_down, 0,
                            rows=t, n_lat=n_lat, segs=both, next_norm=norm(1, 0, mods1))
    xl, u = _odd_layer_latent(xs, u, mods1, od_w_in[0], ml_conv_w[0], ml_conv_b[0], ml_gate_b[0], ml_norm_w[0],
                              na_q_norm_w[0], na_k_norm_w[0], na_rpb[0], od_w_out.astype(BF16), 0,
                              n_lat=n_lat, n_ctx=n_ctx, next_norm=norm(1, 1, mods1))
    out, _ = _conv_ffn_block(xl, u, mods1, ffn_w_up, ffn_conv_w[1], ffn_conv_b[1], w_down, 1,
                             rows=n_lat, n_lat=n_lat, segs=((0, n_lat),), next_norm=None)
    return out
```

```python
import functools
import math

import numpy as np
import jax
import jax.numpy as jnp
from jax import lax
from jax.experimental import pallas as pl
from jax.experimental.pallas import tpu as pltpu

F32 = jnp.float32
BF16 = jnp.bfloat16
HI = lax.Precision.HIGHEST

D_MODEL = 2048
GRID_W = 64
EPS = 1e-6
CONV_W = 3
SSD_WIDTH = D_MODEL
SSD_HEAD_DIM = 64
SSD_HEADS = SSD_WIDTH // SSD_HEAD_DIM
SSD_STATE = 128
SSD_GROUPS = 4
SSD_HPG = SSD_HEADS // SSD_GROUPS
SSD_GW = SSD_WIDTH // SSD_GROUPS
SSD_XBC = SSD_WIDTH + 2 * SSD_GROUPS * SSD_STATE
HY_WIDTH = D_MODEL
HY_EMB = 33
HY_BANDS = (HY_EMB - 1) // 2
HY_ORDER = 64
HY_FAST = 0.3
HY_SLOW = 1.5
HY_TARGET = 1e-2
ML_HEADS = 8
ML_WIDTH = D_MODEL
ML_V_DIM = ML_WIDTH // ML_HEADS
ML_QK_DIM = ML_V_DIM // 2
ML_QK_WIDTH = ML_HEADS * ML_QK_DIM
ROPE_THETA = 10000.0
NA_WIDTH = D_MODEL
NA_HEAD_DIM = 128
NA_HEADS = NA_WIDTH // NA_HEAD_DIM
NA_ROWS = 8
NA_COLS = 16
NA_QROWS = 8
NA_KROWS = 16
NA_PARTS = 2
NA_HPS = 2
NA_PREP_STEPS = 8
NA_TAB_ROWS = 4
NA_PART_KROWS = 12
EV_CONV_CH = SSD_XBC + 3 * HY_WIDTH
SCAN_Q = 128
SCAN_CPS = 2
LANES = 128
SUBLANES = 8
MIB = 1024 * 1024
V7X_VMEM_BYTES = 64 * MIB
VMEM_LIMIT = 48 * MIB
VMEM_LIMIT_BIG = V7X_VMEM_BYTES - 6 * MIB


def _cp(sem, vmem=VMEM_LIMIT):
    return pltpu.CompilerParams(dimension_semantics=sem, vmem_limit_bytes=vmem)


def _sigmoid(x):
    return 0.5 * jnp.tanh(0.5 * x) + 0.5


def _silu(x):
    return x * _sigmoid(x)


def _softplus(x):
    return jnp.maximum(x, 0.0) + jnp.log(1.0 + jnp.exp(-jnp.abs(x)))


def _log_sigmoid(x):
    return -_softplus(-x)


def _dot(a, b, precision=None):
    return jnp.dot(a, b, preferred_element_type=F32, precision=precision)


def _dot_nt(a, b):
    return lax.dot_general(a, b, (((1,), (1,)), ((), ())), preferred_element_type=F32)


def _split_bf16(x, terms):
    out = []
    for _ in range(terms):
        part = x.astype(BF16)
        out.append(part)
        x = x - part.astype(F32)
    return out


def _dot_sel(x, sel, terms):
    selb = sel.astype(BF16)
    return sum(_dot(part, selb) for part in _split_bf16(x, terms))


def _sel_dot(sel, x, terms):
    selb = sel.astype(BF16)
    return sum(_dot(selb, part) for part in _split_bf16(x, terms))


def _mm_body(a_ref, w_ref, o_ref, *scratch, nk, cache_w):
    if cache_w:
        wb_ref = scratch[0]

        @pl.when((pl.program_id(1) == 0) & (pl.program_id(2) == 0))
        def _():
            wb_ref[...] = w_ref[...].astype(BF16)

        w_val = wb_ref[...]
    else:
        w_val = w_ref[...].astype(BF16)
    part = _dot(a_ref[...].astype(BF16), w_val)
    if nk == 1:
        o_ref[...] = part.astype(o_ref.dtype)
        return
    acc_ref = scratch[0]
    k = pl.program_id(3)

    @pl.when(k == 0)
    def _():
        acc_ref[...] = part

    @pl.when(k > 0)
    def _():
        acc_ref[...] += part

    @pl.when(k == nk - 1)
    def _():
        o_ref[...] = acc_ref[...].astype(o_ref.dtype)


def _matmul(a, w, *, tm, tn, tk, out_dtype, rows=None, cols=None, w_index=None, vmem=VMEM_LIMIT, name="mm"):
    ba, ta, kdim = a.shape
    bw, kdim2, n = w.shape
    assert kdim == kdim2
    nb = ba if w_index is not None else max(ba, bw)
    rows = ta if rows is None else rows
    n = n if cols is None else cols
    assert rows % tm == 0 and n % tn == 0 and kdim % tk == 0
    ni, nj, nk = rows // tm, n // tn, kdim // tk
    a_map = (lambda j, b, i, k: (b, i, k)) if ba > 1 else (lambda j, b, i, k: (0, i, k))
    if w_index is not None:
        w_map = lambda j, b, i, k: (w_index, k, j)
    else:
        w_map = (lambda j, b, i, k: (b, k, j)) if bw > 1 else (lambda j, b, i, k: (0, k, j))
    cache_w = w.dtype != BF16 and nk == 1 and (w_index is not None or bw == 1)
    scratch = [pltpu.VMEM((tm, tn), F32)] if nk > 1 else []
    if cache_w:
        scratch = [pltpu.VMEM((tk, tn), BF16)]
    inner = "arbitrary" if cache_w else "parallel"
    return pl.pallas_call(
        functools.partial(_mm_body, nk=nk, cache_w=cache_w),
        out_shape=jax.ShapeDtypeStruct((nb, rows, n), out_dtype), grid=(nj, nb, ni, nk),
        in_specs=[pl.BlockSpec((None, tm, tk), a_map), pl.BlockSpec((None, tk, tn), w_map)],
        out_specs=pl.BlockSpec((None, tm, tn), lambda j, b, i, k: (b, i, j)),
        scratch_shapes=scratch, name=name,
        compiler_params=_cp(("parallel", inner, inner, "arbitrary"), vmem))(a, w)


def _row_is_lat(row_blk, tm, n_lat):
    rows = row_blk * tm + lax.broadcasted_iota(jnp.int32, (tm, 1), 0)
    return rows < n_lat


def _gated_residual_epilogue(acc, row_blk, res_ref, gate_ref, *, tm, n_lat):
    gate = jnp.where(_row_is_lat(row_blk, tm, n_lat), gate_ref[0], gate_ref[1])
    return res_ref[...] + gate * acc


def _gated_residual_rows_body(*refs, ks, tm, n_lat, nj, norm):
    n_a = len(ks)
    a_refs, w_ref, res_ref, gate_ref = refs[:n_a], refs[n_a], refs[n_a + 1], refs[n_a + 2]
    rest = refs[n_a + 3:]
    o_ref = rest[2] if norm else rest[0]
    row_blk = pl.program_id(1)
    acc, lo = None, 0
    for a_ref, kk in zip(a_refs, ks):
        part = _dot(a_ref[...], w_ref[lo:lo + kk, :])
        acc = part if acc is None else acc + part
        lo += kk
    x = _gated_residual_epilogue(acc, row_blk, res_ref, gate_ref, tm=tm, n_lat=n_lat)
    o_ref[...] = x
    if norm:
        nw_ref, mod_ref, _, u_ref, xrow_ref = rest
        j = pl.program_id(2)
        xrow_ref[j] = x

        @pl.when(j == nj - 1)
        def _():
            full = jnp.concatenate([xrow_ref[jj] for jj in range(nj)], axis=1)
            is_lat = _row_is_lat(row_blk, tm, n_lat)
            u_ref[...] = _modulated_rmsnorm(full, is_lat, nw_ref, mod_ref).astype(u_ref.dtype)


def _matmul_gated_residual_rows(a_list, w, res, gate, *, w_index, tm, tn, rows, n_lat, name, norm=None):
    nb = a_list[0].shape[0]
    _, kdim, n = w.shape
    ks = tuple(x.shape[2] for x in a_list)
    assert sum(ks) == kdim and rows % tm == 0 and n % tn == 0
    nj = n // tn
    in_specs = [pl.BlockSpec((None, tm, kk), lambda b, i, j: (b, i, 0)) for kk in ks]
    in_specs += [pl.BlockSpec((None, kdim, tn), lambda b, i, j: (w_index, 0, j)),
                 pl.BlockSpec((None, tm, tn), lambda b, i, j: (b, i, j)),
                 pl.BlockSpec((None, 2, 1, tn), lambda b, i, j: (b, 0, 0, j))]
    args = [*a_list, w, res, gate]
    out_shape = jax.ShapeDtypeStruct((nb, rows, n), F32)
    out_specs = pl.BlockSpec((None, tm, tn), lambda b, i, j: (b, i, j))
    scratch = []
    if norm is not None:
        in_specs += [pl.BlockSpec((1, n), lambda b, i, j: (0, 0)),
                     pl.BlockSpec((None, 2, 2, 1, n), lambda b, i, j: (b, 0, 0, 0, 0))]
        args += list(norm)
        out_shape = (out_shape, jax.ShapeDtypeStruct((nb, rows, n), BF16))
        out_specs = (out_specs, pl.BlockSpec((None, tm, n), lambda b, i, j: (b, i, 0)))
        scratch = [pltpu.VMEM((nj, tm, tn), F32)]
    return pl.pallas_call(
        functools.partial(_gated_residual_rows_body, ks=ks, tm=tm, n_lat=n_lat, nj=nj, norm=norm is not None),
        out_shape=out_shape, grid=(nb, rows // tm, nj), in_specs=in_specs, out_specs=out_specs,
        scratch_shapes=scratch, name=name,
        compiler_params=_cp(("parallel", "parallel", "arbitrary" if norm is not None else "parallel")))(*args)


def _modulated_rmsnorm(x, is_lat, w_ref, mod_ref):
    y = x * lax.rsqrt(jnp.mean(x * x, axis=-1, keepdims=True) + EPS) * w_ref[...]
    shift = jnp.where(is_lat, mod_ref[0, 0], mod_ref[0, 1])
    scale = jnp.where(is_lat, mod_ref[1, 0], mod_ref[1, 1])
    return y * (1.0 + scale) + shift


def _rmsmod_body(x_ref, w_ref, mod_ref, o_ref, *, tm, n_lat):
    is_lat = _row_is_lat(pl.program_id(2), tm, n_lat)
    o_ref[...] = _modulated_rmsnorm(x_ref[...], is_lat, w_ref, mod_ref).astype(o_ref.dtype)


def _rmsmod(x, w, mod, *, tm, rows, n_lat):
    nb, _, d = x.shape
    tm = tm // 2
    assert rows % tm == 0 and tm % 16 == 0
    body = functools.partial(_rmsmod_body, tm=tm, n_lat=n_lat)
    return pl.pallas_call(
        body, out_shape=jax.ShapeDtypeStruct((nb, rows, d), BF16), grid=(1, nb, rows // tm),
        in_specs=[pl.BlockSpec((None, tm, d), lambda j, b, i: (b, i, 0)),
                  pl.BlockSpec((1, d), lambda j, b, i: (0, 0)),
                  pl.BlockSpec((None, 2, 2, 1, d), lambda j, b, i: (b, 0, 0, 0, 0))],
        out_specs=pl.BlockSpec((None, tm, d), lambda j, b, i: (b, i, 0)),
        name="rmsmod", compiler_params=_cp(("parallel", "parallel", "parallel")))(x, w, mod)


CONV_CH = 128
CONV_HALO = 16
CONV_STEPS = 4


def _shift_matrices():
    r = lax.broadcasted_iota(jnp.int32, (CONV_CH, CONV_CH + 2 * CONV_HALO), 0)
    c = lax.broadcasted_iota(jnp.int32, (CONV_CH, CONV_CH + 2 * CONV_HALO), 1)
    return (c == r + CONV_HALO - 1).astype(BF16), (c == r + CONV_HALO + 1).astype(BF16)


def _chunk_rows(r0, i):
    return pl.ds(r0 + pl.multiple_of(i * CONV_CH, CONV_CH), CONV_CH)


def _taps(w_ref, b_ref, scale=1.0):
    return w_ref[...] * scale, b_ref[...] * scale


def _silu_of_half(h):
    return h + h * jnp.tanh(h)


def _conv_at(x_ref, taps, shifts, r0, n, steps):
    w, b = taps
    cur = [x_ref[_chunk_rows(r0, i), :] for i in steps]
    zero = jnp.zeros((CONV_HALO, cur[0].shape[1]), cur[0].dtype)
    lo = [pl.multiple_of(jnp.maximum(i * CONV_CH - CONV_HALO, 0), CONV_HALO) for i in steps]
    hi = [pl.multiple_of(jnp.minimum((i + 1) * CONV_CH, n - CONV_HALO), CONV_HALO) for i in steps]
    tail = [jnp.where(i > 0, x_ref[pl.ds(r0 + l, CONV_HALO), :], zero) for i, l in zip(steps, lo)]
    head = [jnp.where(i < n // CONV_CH - 1, x_ref[pl.ds(r0 + h, CONV_HALO), :], zero) for i, h in zip(steps, hi)]
    win = [jnp.concatenate([t, c, h], axis=0) for t, c, h in zip(tail, cur, head)]
    prev = [_dot(shifts[0], w) for w in win]
    nxt = [_dot(shifts[1], w) for w in win]
    return [w[0:1, :] * p + w[1:2, :] * c.astype(F32) + w[2:3, :] * x + b for p, c, x in zip(prev, cur, nxt)]


def _conv_loop(segs, fn):
    for si, (r0, n) in enumerate(segs):
        def body(i, carry, r0=r0, n=n, si=si):
            fn(si, r0, n, [i])
            return carry

        lax.fori_loop(0, n // CONV_CH, body, 0, unroll=min(CONV_STEPS, n // CONV_CH))


def _conv_silu_body(x_ref, w_ref, b_ref, o_ref, *, segs):
    shifts = _shift_matrices()
    half_taps = _taps(w_ref, b_ref, 0.5)

    def fn(si, r0, n, steps):
        ys = [_silu_of_half(h) for h in _conv_at(x_ref, half_taps, shifts, r0, n, steps)]
        for i, y in zip(steps, ys):
            o_ref[_chunk_rows(r0, i), :] = y.astype(o_ref.dtype)

    _conv_loop(segs, fn)


def _conv_rope_body(x_ref, w_ref, b_ref, cos_ref, sin_ref, o_ref, *, segs, tc):
    shifts = _shift_matrices()
    lane = lax.broadcasted_iota(jnp.int32, (CONV_CH, tc), 1)
    first = (lane % (ML_QK_DIM // 2)) < (ML_QK_DIM // 4)
    nf = ML_QK_DIM // 4
    half_taps = _taps(w_ref, b_ref, 0.5)

    def fn(si, r0, n, steps):
        ys = [_silu_of_half(h) for h in _conv_at(x_ref, half_taps, shifts, r0, n, steps)]
        partner = [jnp.where(first, pltpu.roll(y, tc - nf, 1), pltpu.roll(y, nf, 1)) for y in ys]
        for i, y, p in zip(steps, ys, partner):
            rows = _chunk_rows(r0, i)
            o_ref[rows, :] = (y * cos_ref[rows, :] + p * sin_ref[rows, :]).astype(o_ref.dtype)

    _conv_loop(segs, fn)


def _conv_ffn_body(a_ref, x_ref, w_ref, b_ref, o_ref, *, segs):
    shifts = _shift_matrices()
    half_taps = _taps(w_ref, b_ref, 0.5)

    def fn(si, r0, n, steps):
        gs = [_silu_of_half(h) for h in _conv_at(x_ref, half_taps, shifts, r0, n, steps)]
        for i, g in zip(steps, gs):
            rows = _chunk_rows(r0, i)
            o_ref[rows, :] = (a_ref[rows, :].astype(F32) * g).astype(o_ref.dtype)

    _conv_loop(segs, fn)


def _conv_hyena_body(x0_ref, x1_ref, v_ref, w0_ref, b0_ref, w1_ref, b1_ref, w2_ref, b2_ref,
                     ox0_ref, ozl_ref, ozc_ref, ozle_ref, ozlo_ref, ozce_ref, ozco_ref, *zs_refs, segs):
    shifts = _shift_matrices()
    taps = (_taps(w0_ref, b0_ref), _taps(w1_ref, b1_ref), _taps(w2_ref, b2_ref))
    half = CONV_CH // 2
    outs = ((ozl_ref, ozle_ref, ozlo_ref), (ozc_ref, ozce_ref, ozco_ref))

    def fn(si, r0, n, steps):
        x0 = _conv_at(x0_ref, taps[0], shifts, r0, n, steps)
        x1 = _conv_at(x1_ref, taps[1], shifts, r0, n, steps)
        v = _conv_at(v_ref, taps[2], shifts, r0, n, steps)
        nat_ref, even_ref, odd_ref = outs[si]
        for i, x0_i, x1_i, v_i in zip(steps, x0, x1, v):
            ox0_ref[_chunk_rows(r0, i), :] = x0_i.astype(ox0_ref.dtype)
            zin = x1_i * v_i
            nat_ref[_chunk_rows(0, i), :] = zin.astype(BF16)
            t0 = r0 + pl.multiple_of(i * CONV_CH, CONV_CH)
            hrows = pl.ds(pl.multiple_of(i * half, half), half)
            for g, zs_ref in enumerate(zs_refs):
                lanes = slice(g * LANES, (g + 1) * LANES)
                zs_ref[pl.ds(t0, CONV_CH), :] = zin[:, lanes]
                even_ref[hrows, lanes] = zs_ref[pl.ds(t0, half, stride=2), :].astype(BF16)
                odd_ref[hrows, lanes] = zs_ref[pl.ds(t0 + 1, half, stride=2), :].astype(BF16)

    _conv_loop(segs, fn)


def _conv_specs(t_in, tc, xcol_blk, wcol_blk):
    return [pl.BlockSpec((None, t_in, tc), lambda b, j: (b, 0, j + xcol_blk)),
            pl.BlockSpec((CONV_W, tc), lambda b, j: (0, j + wcol_blk)),
            pl.BlockSpec((1, tc), lambda b, j: (0, j + wcol_blk))]


def _conv_silu(x, w, b, *, xcol, wcol, width, segs, tc=256):
    nb, t_in, _ = x.shape
    return pl.pallas_call(
        functools.partial(_conv_silu_body, segs=segs),
        out_shape=jax.ShapeDtypeStruct((nb, t_in, width), BF16), grid=(nb, width // tc),
        in_specs=_conv_specs(t_in, tc, xcol // tc, wcol // tc),
        out_specs=pl.BlockSpec((None, t_in, tc), lambda b, j: (b, 0, j)),
        name="conv_silu", compiler_params=_cp(("parallel", "parallel")))(x, w, b)


def _conv_rope(x, w, b, cos, sin, *, width, segs, tc=256):
    nb, t_in, _ = x.shape
    tab = pl.BlockSpec((t_in, tc), lambda b, j: (0, 0))
    return pl.pallas_call(
        functools.partial(_conv_rope_body, segs=segs, tc=tc),
        out_shape=jax.ShapeDtypeStruct((nb, t_in, width), BF16), grid=(nb, width // tc),
        in_specs=_conv_specs(t_in, tc, 0, 0) + [tab, tab],
        out_specs=pl.BlockSpec((None, t_in, tc), lambda b, j: (b, 0, j)),
        name="conv_rope", compiler_params=_cp(("parallel", "parallel")))(x, w, b, cos, sin)


def _conv_ffn(up, w, b, *, segs, rows, tc=256):
    nb, t_in, two_ff = up.shape
    ff = two_ff // 2
    return pl.pallas_call(
        functools.partial(_conv_ffn_body, segs=segs),
        out_shape=jax.ShapeDtypeStruct((nb, rows, ff), BF16), grid=(nb, ff // tc),
        in_specs=[pl.BlockSpec((None, rows, tc), lambda b, j: (b, 0, j))] + _conv_specs(rows, tc, ff // tc, 0),
        out_specs=pl.BlockSpec((None, rows, tc), lambda b, j: (b, 0, j)),
        name="conv_ffn", compiler_params=_cp(("parallel", "parallel")))(up, up, w, b)


def _conv_hyena(pr, w, b, *, xcol, wcol, segs, tc=256):
    nb, t_in, _ = pr.shape
    (_, n_lat), (_, n_ctx) = segs
    wd = HY_WIDTH
    out_rows = (t_in, n_lat, n_ctx, n_lat // 2, n_lat // 2, n_ctx // 2, n_ctx // 2)
    in_specs = [pl.BlockSpec((None, t_in, tc), functools.partial(lambda b, j, o: (b, 0, j + o), o=(xcol + m * wd) // tc))
                for m in range(3)]
    wargs = []
    for m in range(3):
        off = (wcol + m * wd) // tc
        in_specs.append(pl.BlockSpec((CONV_W, tc), functools.partial(lambda b, j, o: (0, j + o), o=off)))
        in_specs.append(pl.BlockSpec((1, tc), functools.partial(lambda b, j, o: (0, j + o), o=off)))
        wargs += [w, b]
    return pl.pallas_call(
        functools.partial(_conv_hyena_body, segs=segs),
        out_shape=tuple(jax.ShapeDtypeStruct((nb, rows, wd), BF16) for rows in out_rows),
        grid=(nb, wd // tc), in_specs=in_specs,
        out_specs=tuple(pl.BlockSpec((None, rows, tc), lambda b, j: (b, 0, j)) for rows in out_rows),
        scratch_shapes=[pltpu.VMEM((t_in, LANES), F32)] * (tc // LANES),
        name="conv_hyena", compiler_params=_cp(("parallel", "parallel")))(pr, pr, pr, *wargs)


def _scan_masks(reverse):
    q = SCAN_Q
    ii = lax.broadcasted_iota(jnp.int32, (q, q), 0)
    jj = lax.broadcasted_iota(jnp.int32, (q, q), 1)
    mask = (jj >= ii) if reverse else (jj <= ii)
    mask_t = (jj <= ii) if reverse else (jj >= ii)
    return mask, mask.astype(F32), mask_t.astype(F32)


def _chunk_map(reverse, n_chunks, n_lat_chunks):
    if reverse:
        return lambda k: n_chunks - 1 - k
    return lambda k: (k + n_lat_chunks) % n_chunks


SSD_GPS = 4


def _ssd_body(*refs, reverse, final):
    (xs_ref, b_ref, c_ref, dtc_ref, dtr_ref, dbc_ref, dbr_ref, alc_ref, alr_ref) = refs[:9]
    if final:
        yf_ref, z_ref, dsk_ref, nw_ref, o_ref, h_ref = refs[9:]
    else:
        o_ref, h_ref = refs[9:]

    @pl.when(pl.program_id(2) == 0)
    def _():
        h_ref[...] = jnp.zeros_like(h_ref)

    groups = range(SSD_GPS)
    h = [h_ref[g] for g in groups]
    for c in (reversed(range(SCAN_CPS)) if reverse else range(SCAN_CPS)):
        h = _ssd_chunk(refs, slice(c * SCAN_Q, (c + 1) * SCAN_Q), h, reverse=reverse, final=final)
    for g in groups:
        h_ref[g] = h[g]


def _ssd_chunk(refs, rs, h_prev, *, reverse, final):
    (xs_ref, b_ref, c_ref, dtc_ref, dtr_ref, dbc_ref, dbr_ref, alc_ref, alr_ref) = refs[:9]
    if final:
        yf_ref, z_ref, dsk_ref, nw_ref, o_ref = refs[9:14]
    else:
        o_ref = refs[9]
    q, gw, hd = SCAN_Q, SSD_GW, SSD_HEAD_DIM
    mask, maskf, mask_tf = _scan_masks(reverse)
    groups = range(SSD_GPS)
    pairs = range(SSD_HPG // 2)
    cols = [slice(g * gw, (g + 1) * gw) for g in groups]
    xs = [xs_ref[rs, cols[g]].astype(F32) for g in groups]
    bm = [b_ref[rs, g * SSD_STATE:(g + 1) * SSD_STATE] for g in groups]
    cm = [c_ref[rs, g * SSD_STATE:(g + 1) * SSD_STATE] for g in groups]
    dt_c = [_softplus(dtc_ref[g, rs, :] + dbc_ref[g]) for g in groups]
    dt_r = [_softplus(dtr_ref[g, :, rs] + dbr_ref[g]) for g in groups]
    da_c = [dt_c[g] * (-jnp.exp(alc_ref[g])) for g in groups]
    da_r = [dt_r[g] * (-jnp.exp(alr_ref[g])) for g in groups]
    acum_c = [_sel_dot(maskf, da_c[g], 3) for g in groups]
    acum_r = [_dot_sel(da_r[g], mask_tf, 3) for g in groups]
    atot = [jnp.sum(da_c[g], axis=0, keepdims=True) for g in groups]

    er = lax.broadcasted_iota(jnp.int32, (SSD_HPG, gw), 0)
    ec = lax.broadcasted_iota(jnp.int32, (SSD_HPG, gw), 1)
    expand = (ec // hd == er).astype(F32)
    wide = [_dot_sel(jnp.concatenate([dt_c[g], jnp.exp(atot[g] - acum_c[g]), jnp.exp(acum_c[g]),
                                      jnp.broadcast_to(jnp.exp(atot[g]), (SUBLANES, SSD_HPG))], axis=0), expand, 1)
            for g in groups]
    xq = [xs[g] * wide[g][0:q] for g in groups]
    cb = [_dot_nt(cm[g], bm[g]) for g in groups]
    y_off = [wide[g][2 * q:3 * q] * _dot(cm[g], h_prev[g].astype(BF16)) for g in groups]
    lane = lax.broadcasted_iota(jnp.int32, (q, 2 * hd), 1)

    def pair_lhs(g, p):
        ms = [(cb[g] * jnp.where(mask, jnp.exp(acum_c[g][:, r:r + 1] - acum_r[g][r:r + 1, :]), 0.0)).astype(BF16)
              for r in (2 * p, 2 * p + 1)]
        return jnp.concatenate(ms, axis=1)

    def pair_rhs(g, p):
        xp = xq[g][:, 2 * hd * p:2 * hd * (p + 1)]
        return jnp.concatenate([jnp.where(lane < hd, xp, 0.0), jnp.where(lane >= hd, xp, 0.0)], axis=0).astype(BF16)

    y_in = [[_dot(pair_lhs(g, p), pair_rhs(g, p)) for p in pairs] for g in groups]
    y = [y_off[g] + jnp.concatenate(y_in[g], axis=1) for g in groups]
    bt = [bm[g].astype(F32).T.astype(BF16) for g in groups]
    upd = [_dot(bt[g], (xq[g] * wide[g][q:2 * q]).astype(BF16)) for g in groups]
    for g in groups:
        if final:
            ytot = y[g] + yf_ref[rs, cols[g]] + xs[g] * dsk_ref[:, cols[g]]
            gated = ytot * _silu(z_ref[rs, cols[g]].astype(F32))
            o = gated * lax.rsqrt(jnp.mean(gated * gated, axis=-1, keepdims=True) + EPS) * nw_ref[:, cols[g]]
            o_ref[rs, cols[g]] = o.astype(o_ref.dtype)
        else:
            o_ref[rs, cols[g]] = y[g]
    return [wide[g][3 * q:3 * q + 1] * h_prev[g] + upd[g] for g in groups]


def _ssd(xbc, dt_c, dt_r, db_c, db_r, al_c, al_r, *, direction, n_lat, final_args=None):
    nb, t, _ = xbc.shape
    q, gw, hpg = SCAN_CPS * SCAN_Q, SSD_GW, SSD_HPG
    assert n_lat % q == 0 and t % q == 0
    nch = t // q
    reverse = direction == 1
    cmap = _chunk_map(reverse, nch, n_lat // q)
    gps = SSD_GPS
    sw, gww = gps * SSD_STATE, gps * gw
    b0 = SSD_WIDTH // sw
    c0 = b0 + SSD_GROUPS // gps
    d = direction
    in_specs = [
        pl.BlockSpec((None, q, gww), lambda b, g, k: (b, cmap(k), g)),
        pl.BlockSpec((None, q, sw), lambda b, g, k: (b, cmap(k), b0 + g)),
        pl.BlockSpec((None, q, sw), lambda b, g, k: (b, cmap(k), c0 + g)),
        pl.BlockSpec((None, None, gps, q, hpg), lambda b, g, k: (b, d, g, cmap(k), 0)),
        pl.BlockSpec((None, None, gps, hpg, q), lambda b, g, k: (b, d, g, 0, cmap(k))),
        pl.BlockSpec((None, gps, 1, hpg), lambda b, g, k: (d, g, 0, 0)),
        pl.BlockSpec((None, gps, hpg, 1), lambda b, g, k: (d, g, 0, 0)),
        pl.BlockSpec((None, gps, 1, hpg), lambda b, g, k: (d, g, 0, 0)),
        pl.BlockSpec((None, gps, hpg, 1), lambda b, g, k: (d, g, 0, 0)),
    ]
    args = [xbc, xbc, xbc, dt_c, dt_r, db_c, db_r, al_c, al_r]
    final = final_args is not None
    if final:
        yf, pr, dskip, norm_w = final_args
        in_specs += [pl.BlockSpec((None, q, gww), lambda b, g, k: (b, cmap(k), g)),
                     pl.BlockSpec((None, q, gww), lambda b, g, k: (b, cmap(k), g)),
                     pl.BlockSpec((1, gww), lambda b, g, k: (0, g)),
                     pl.BlockSpec((1, gww), lambda b, g, k: (0, g))]
        args += [yf, pr, dskip, norm_w]
    out_shape = jax.ShapeDtypeStruct((nb, t, SSD_WIDTH), BF16 if final else F32)
    return pl.pallas_call(
        functools.partial(_ssd_body, reverse=reverse, final=final),
        out_shape=out_shape, grid=(nb, SSD_GROUPS // gps, nch), in_specs=in_specs,
        out_specs=pl.BlockSpec((None, q, gww), lambda b, g, k: (b, cmap(k), g)),
        scratch_shapes=[pltpu.VMEM((gps, SSD_STATE, gw), F32)],
        name="ssd_bwd" if reverse else "ssd_fwd",
        compiler_params=_cp(("parallel", "parallel", "arbitrary")))(*args)


def _hyfilt_body(f_ref, w1_ref, b1_ref, w2_ref, b2_ref, w3_ref, fr_ref, dl_ref, o_ref, *, tl):
    f = f_ref[...]
    h = jnp.sin(fr_ref[0:1, :] * (_dot(f, w1_ref[...], HI) + b1_ref[...]))
    h = jnp.sin(fr_ref[1:2, :] * (_dot(h, w2_ref[...], HI) + b2_ref[...]))
    h = _dot(h, w3_ref[...], HI)
    dec = jnp.exp(-f[:, 0:1] * dl_ref[...])
    row = pl.program_id(0) * tl + lax.broadcasted_iota(jnp.int32, (tl, 1), 0)
    wd = HY_WIDTH
    o_ref[:, 0:wd] = (h[:, 0:wd] * dec).astype(o_ref.dtype)
    o_ref[:, wd:2 * wd] = jnp.where(row == 0, 0.0, h[:, wd:2 * wd] * dec).astype(o_ref.dtype)


def _hyena_filters(length, w1, b1, w2, b2, w3, freq, *, tl=256):
    p = LANES
    t = jnp.linspace(0.0, 1.0, length, dtype=F32)[:, None]
    w = 2.0 * math.pi * jnp.arange(length, dtype=F32)[:, None] / length
    f = jnp.linspace(1e-4, HY_BANDS - 1, HY_BANDS, dtype=F32)[None, :]
    feats = jnp.concatenate([t, jnp.cos(f * w), -jnp.sin(f * w)], axis=-1)
    feats = jnp.concatenate([feats[0::2], feats[1::2]], axis=0)
    feats = jnp.pad(feats, ((0, 0), (0, p - HY_EMB)))
    po = p - HY_ORDER
    w1p = jnp.pad(w1, ((0, p - HY_EMB), (0, po)))
    w2p = jnp.pad(w2, ((0, po), (0, po)))
    w3p = jnp.pad(w3, ((0, po), (0, 0)))
    b1p = jnp.pad(b1[None, :], ((0, 0), (0, po)))
    b2p = jnp.pad(b2[None, :], ((0, 0), (0, po)))
    frp = jnp.pad(freq, ((0, 0), (0, po)))
    deltas = jnp.abs(jnp.linspace(math.log(HY_TARGET) / HY_FAST, math.log(HY_TARGET) / HY_SLOW, HY_WIDTH,
                                  dtype=F32))[None, :]
    full = lambda shape: pl.BlockSpec(shape, lambda i: (0,) * len(shape))
    out = pl.pallas_call(
        functools.partial(_hyfilt_body, tl=tl),
        out_shape=jax.ShapeDtypeStruct((length, 2 * HY_WIDTH), BF16), grid=(length // tl,),
        in_specs=[pl.BlockSpec((tl, p), lambda i: (i, 0)), full((p, p)), full((1, p)), full((p, p)), full((1, p)),
                  full((p, 2 * HY_WIDTH)), full((2, p)), full((1, HY_WIDTH))],
        out_specs=pl.BlockSpec((tl, 2 * HY_WIDTH), lambda i: (i, 0)),
        name="hyena_filters", compiler_params=_cp(("parallel",)))(feats, w1p, b1p, w2p, b2p, w3p, frp, deltas)
    return out[None]


def _dft_matrices(length):
    n = 2 * length
    k = jnp.arange(length, dtype=jnp.int32)[:, None]
    t = jnp.arange(length, dtype=jnp.int32)[None, :]
    step = 1 << (int(math.log2(length)) // 2)
    part = jnp.arange(length // step, dtype=jnp.int32)[:, None]
    ang_hi = ((part * step * t) % n).astype(F32) * (2.0 * math.pi / n)
    ang_lo = ((jnp.arange(step, dtype=jnp.int32)[:, None] * t) % n).astype(F32) * (2.0 * math.pi / n)
    ch, sh = jnp.cos(ang_hi)[:, None, :], jnp.sin(ang_hi)[:, None, :]
    cl, sl = jnp.cos(ang_lo)[None, :, :], jnp.sin(ang_lo)[None, :, :]
    c = (ch * cl - sh * sl).reshape(length, length)
    s = (sh * cl + ch * sl).reshape(length, length)
    alt_t = jnp.where(t % 2 == 0, 1.0, -1.0).astype(F32)
    fwd = jnp.concatenate([c, jnp.where(k == 0, alt_t, -s)], axis=0)
    alt_r = jnp.where(k % 2 == 0, 1.0, -1.0).astype(F32)
    inv_re = jnp.where(t == 0, 1.0 / n, (2.0 / n) * c)
    inv_im = jnp.where(t == 0, alt_r / n, (-2.0 / n) * s)
    inv = jnp.concatenate([inv_re, inv_im], axis=1)
    return fwd.astype(BF16)[None], inv.astype(BF16)[None]


def _tile_interleave(x, axis, hm):
    shp = x.shape
    n = shp[axis] // 2
    x = x.reshape(shp[:axis] + (2, n // hm, hm) + shp[axis + 1:])
    return jnp.swapaxes(x, axis, axis + 1).reshape(shp)


def _twiddle_table(length, hm):
    k = np.arange(length // 2, dtype=np.float64)[:, None] * (np.pi / length)
    tab = np.concatenate([np.cos(k), np.sin(k)], axis=0)
    tab = tab.reshape(2, -1, hm, 1).swapaxes(0, 1).reshape(length, 1)
    return jnp.asarray(np.broadcast_to(tab, (length, LANES)).astype(np.float32))


def _lanes(x, width):
    return jnp.tile(x, (1, width // LANES))


def _parity_banks(e, ot, tw, hm):
    er, ei, otr, oti = e[0:hm], e[hm:], ot[0:hm], ot[hm:]
    c, s = _lanes(tw[0:hm], e.shape[1]), _lanes(tw[hm:], e.shape[1])
    orr = otr * c + oti * s
    oi = oti * c - otr * s
    return (er + orr, ei + oi, er - orr, ei - oi), (c, s)


def _is_bin0(row_blk, hm):
    return (row_blk == 0) & (lax.broadcasted_iota(jnp.int32, (hm, 1), 0) == 0)


def _hy_kbank_body(f_ref, hef_ref, heb_ref, hof_ref, hob_ref, tw_ref, ka_ref, kb_ref, *, hm):
    fwd, tw = f_ref[...], tw_ref[...]
    ef, eb, of, ob = (_dot(fwd, r[...]) for r in (hef_ref, heb_ref, hof_ref, hob_ref))
    (afr, afi, bfr, bfi), _ = _parity_banks(ef, of, tw, hm)
    (abr, abi, bbr, bbi), _ = _parity_banks(eb, ob, tw, hm)
    first = _is_bin0(pl.program_id(0), hm)
    kai = jnp.where(first, ef[hm:] + eb[hm:], afi - abi)
    kbi = jnp.where(first, ob[hm:] - of[hm:], bfi - bbi)
    ka_ref[...] = jnp.concatenate([afr + abr, kai], axis=0)
    kb_ref[...] = jnp.concatenate([bfr + bbr, kbi], axis=0)


def _hy_kbank(fwd, hfb, tw, *, tm, tc=512):
    _, ln, c2 = hfb.shape
    c, h = c2 // 2, ln // 2
    taps = lambda row_blk, col_off: pl.BlockSpec((None, h, tc), lambda i, j: (0, row_blk, j + col_off))
    bank = pl.BlockSpec((None, tm, tc), lambda i, j: (0, i, j))
    return pl.pallas_call(
        functools.partial(_hy_kbank_body, hm=tm // 2),
        out_shape=(jax.ShapeDtypeStruct((1, ln, c), F32),) * 2, grid=(ln // tm, c // tc),
        in_specs=[pl.BlockSpec((None, tm, h), lambda i, j: (0, i, 0)),
                  taps(0, 0), taps(0, c // tc), taps(1, 0), taps(1, c // tc),
                  pl.BlockSpec((tm, LANES), lambda i, j: (i, 0))],
        out_specs=(bank, bank), name="hy_kbank",
        compiler_params=_cp(("parallel", "parallel")))(fwd, hfb, hfb, hfb, hfb, tw)


def _hy_spectrum_body(f_ref, ze_ref, zo_ref, tw_ref, ka_ref, kb_ref, p_ref, q_ref, *, hm):
    fwd = f_ref[...]
    e, acc = _dot(fwd, ze_ref[...]), _dot(fwd, zo_ref[...])
    row_blk = pl.program_id(2)
    (ar, ai, br, bi), (c, s) = _parity_banks(e, acc, tw_ref[...], hm)
    kar, kai, kbr, kbi = ka_ref[0:hm], ka_ref[hm:], kb_ref[0:hm], kb_ref[hm:]
    yar, yai = ar * kar - ai * kai, ar * kai + ai * kar
    ybr, ybi = br * kbr - bi * kbi, br * kbi + bi * kbr
    qr, qi = yar - ybr, yai - ybi
    pr, pi = yar + ybr, yai + ybi
    qpr, qpi = qr * c - qi * s, qi * c + qr * s
    top = slice(0, SUBLANES)
    first = _is_bin0(row_blk, SUBLANES)
    y0, yl = ar[top] * kar[top], br[top] * kbr[top]
    amr, ami = e[hm:hm + SUBLANES], -acc[hm:hm + SUBLANES]
    ymr, ymi = amr * kai[top] - ami * kbi[top], amr * kbi[top] + ami * kai[top]

    def patch(full, special):
        return jnp.concatenate([jnp.where(first, special, full[top]), full[SUBLANES:]], axis=0)

    pr, pi = patch(pr, y0 + yl), patch(pi, 2.0 * ymr)
    qpr, qpi = patch(qpr, y0 - yl), patch(qpi, -2.0 * ymi)
    p_ref[...] = jnp.concatenate([pr, pi], axis=0).astype(p_ref.dtype)
    q_ref[...] = jnp.concatenate([qpr, qpi], axis=0).astype(q_ref.dtype)


def _hy_spectrum(fwd, ze, zo, tw, ka, kb, *, tm, tn):
    nb, h, c = ze.shape
    ln = fwd.shape[1]
    data = pl.BlockSpec((None, h, tn), lambda j, b, i: (b, 0, j))
    bank = pl.BlockSpec((None, tm, tn), lambda j, b, i: (0, i, j))
    out = pl.BlockSpec((None, tm, tn), lambda j, b, i: (b, i, j))
    return pl.pallas_call(
        functools.partial(_hy_spectrum_body, hm=tm // 2),
        out_shape=(jax.ShapeDtypeStruct((nb, ln, c), BF16),) * 2, grid=(c // tn, nb, ln // tm),
        in_specs=[pl.BlockSpec((None, tm, h), lambda j, b, i: (0, i, 0)), data, data,
                  pl.BlockSpec((tm, LANES), lambda j, b, i: (i, 0)), bank, bank],
        out_specs=(out, out), name="hy_spectrum",
        compiler_params=_cp(("parallel", "parallel", "parallel")))(fwd, ze, zo, tw, ka, kb)


def _hyena_out_body(*refs, segs):
    ns = len(segs)
    x0_ref, db_ref, o_ref = refs[3 * ns:3 * ns + 3]
    ys_refs = refs[3 * ns + 3:]
    half = CONV_CH // 2
    for si, (r0, n) in enumerate(segs):
        ye_ref, yo_ref, zin_ref = refs[3 * si:3 * si + 3]

        def body(i, carry, r0=r0, ye_ref=ye_ref, yo_ref=yo_ref, zin_ref=zin_ref):
            t0 = pl.multiple_of(i * CONV_CH, CONV_CH)
            hrows = pl.ds(pl.multiple_of(i * half, half), half)
            for g, ys_ref in enumerate(ys_refs):
                lanes = slice(g * LANES, (g + 1) * LANES)
                ys_ref[pl.ds(r0 + t0, half, stride=2), :] = ye_ref[hrows, lanes].astype(F32)
                ys_ref[pl.ds(r0 + t0 + 1, half, stride=2), :] = yo_ref[hrows, lanes].astype(F32)
            rows = pl.ds(r0 + t0, CONV_CH)
            y = jnp.concatenate([ys_ref[rows, :] for ys_ref in ys_refs], axis=1)
            zin = zin_ref[pl.ds(t0, CONV_CH), :].astype(F32)
            o_ref[rows, :] = (x0_ref[rows, :].astype(F32) * (y + zin * db_ref[...])).astype(o_ref.dtype)
            return carry

        lax.fori_loop(0, n // CONV_CH, body, 0, unroll=min(4, n // CONV_CH))


def _hyena_out(parts, x0, dbias, *, segs, tc=256):
    nb, t, c = x0.shape
    in_specs, args = [], []
    for (ye, yo, zin), (_, n) in zip(parts, segs):
        in_specs += [pl.BlockSpec((None, n // 2, tc), lambda b, j: (b, 0, j)),
                     pl.BlockSpec((None, n // 2, tc), lambda b, j: (b, 0, j)),
                     pl.BlockSpec((None, n, tc), lambda b, j: (b, 0, j))]
        args += [ye, yo, zin]
    in_specs += [pl.BlockSpec((None, t, tc), lambda b, j: (b, 0, j)), pl.BlockSpec((1, tc), lambda b, j: (0, j))]
    return pl.pallas_call(
        functools.partial(_hyena_out_body, segs=segs),
        out_shape=jax.ShapeDtypeStruct((nb, t, c), BF16), grid=(nb, c // tc), in_specs=in_specs,
        out_specs=pl.BlockSpec((None, t, tc), lambda b, j: (b, 0, j)),
        scratch_shapes=[pltpu.VMEM((t, LANES), F32)] * (tc // LANES), name="hyena_out",
        compiler_params=_cp(("parallel", "parallel")))(*args, x0, dbias)


def _hyena_longconv(ze, zo, hfb, *, tn=1024):
    nb, h, c = ze.shape
    ln = 2 * h
    tm_s = min(512, ln)
    hm = tm_s // 2
    fwd, inv = _dft_matrices(h)
    fwd = _tile_interleave(fwd, 1, hm)
    inv = _tile_interleave(inv * 0.5, 2, hm)
    tw = _twiddle_table(ln, hm)
    ka, kb = _hy_kbank(fwd, hfb, tw, tm=tm_s)
    p, q = _hy_spectrum(fwd, ze, zo, tw, ka, kb, tm=tm_s, tn=tn)
    tm_i, tk_i = min(1024, h), ln
    ye = _matmul(inv, p, tm=tm_i, tn=tn, tk=tk_i, out_dtype=BF16, name="hy_idft_even")
    yo = _matmul(inv, q, tm=tm_i, tn=tn, tk=tk_i, out_dtype=BF16, name="hy_idft_odd")
    return ye, yo


def _mlstm_body(*refs, reverse, final):
    q_ref, k_ref, v_ref, gc_ref, gr_ref = refs[:5]
    if final:
        hf_ref, og_ref, nw_ref, o_ref, ct_ref, n_ref, m_ref = refs[5:]
    else:
        o_ref, ct_ref, n_ref, m_ref = refs[5:]
    nh, dk, dv = ML_HEADS, ML_QK_DIM, ML_V_DIM
    scale = dk ** -0.5

    @pl.when(pl.program_id(1) == 0)
    def _():
        ct_ref[...] = jnp.zeros_like(ct_ref)
        n_ref[...] = jnp.zeros_like(n_ref)
        m_ref[...] = jnp.zeros_like(m_ref)

    state = ([ct_ref[h] for h in range(nh)], n_ref[...], m_ref[...])
    for c in (reversed(range(SCAN_CPS)) if reverse else range(SCAN_CPS)):
        state = _mlstm_chunk(refs, slice(c * SCAN_Q, (c + 1) * SCAN_Q), state, reverse=reverse, final=final)
    for h in range(nh):
        ct_ref[h] = state[0][h]
    n_ref[...] = state[1]
    m_ref[...] = state[2]


def _mlstm_chunk(refs, rs, state, *, reverse, final):
    q_ref, k_ref, v_ref, gc_ref, gr_ref = refs[:5]
    if final:
        hf_ref, og_ref, nw_ref, o_ref = refs[5:9]
    else:
        o_ref = refs[5]
    nh, dk, dv = ML_HEADS, ML_QK_DIM, ML_V_DIM
    scale = dk ** -0.5
    cts, n_all, m_all = state
    gc, gr = gc_ref[rs, :], gr_ref[:, rs]
    li_c, lf_c = gc[:, 0:nh], _log_sigmoid(gc[:, nh:2 * nh])
    li_r, lf_r = gr[0:nh, :], _log_sigmoid(gr[nh:2 * nh, :])
    mask, maskf, mask_tf = _scan_masks(reverse)
    bc_c = _dot(maskf, lf_c, HI)
    bc_r = _dot(lf_r, mask_tf, HI)
    btot_c = jnp.sum(lf_c, axis=0, keepdims=True)
    heads = range(nh)
    qs = [q_ref[rs, dk * h:dk * (h + 1)] for h in heads]
    ks = [k_ref[rs, dk * h:dk * (h + 1)] for h in heads]
    vs = [v_ref[rs, dv * h:dv * (h + 1)] for h in heads]
    m_s = [m_all[h:h + 1, 0:1] for h in heads]
    n_s = [n_all[h:h + 1, :] for h in heads]
    qk = [_dot_nt(qs[h], ks[h]) for h in heads]
    qc = [_dot(qs[h], cts[h].astype(BF16)) for h in heads]
    dmat = [jnp.where(mask, bc_c[:, h:h + 1] - bc_r[h:h + 1, :] + li_r[h:h + 1, :], -jnp.inf) for h in heads]
    inter = [bc_c[:, h:h + 1] + m_s[h] for h in heads]
    m_t = [jnp.maximum(inter[h], jnp.max(dmat[h], axis=1, keepdims=True)) for h in heads]
    s = [qk[h] * scale * jnp.exp(dmat[h] - m_t[h]) for h in heads]
    dec = [jnp.exp(inter[h] - m_t[h]) for h in heads]
    num = [_dot(s[h].astype(BF16), vs[h]) + (dec[h] * scale) * qc[h] for h in heads]
    qn = [jnp.sum(qs[h].astype(F32) * n_s[h], axis=1, keepdims=True) * scale for h in heads]
    den = [jnp.sum(s[h], axis=1, keepdims=True) + dec[h] * qn[h] for h in heads]
    hh = [num[h] / jnp.maximum(jnp.abs(den[h]), jnp.exp(-m_t[h])) for h in heads]

    wst_c = [btot_c[:, h:h + 1] - bc_c[:, h:h + 1] + li_c[:, h:h + 1] for h in heads]
    wst_r = [btot_c[:, h:h + 1] - bc_r[h:h + 1, :] + li_r[h:h + 1, :] for h in heads]
    m_new = [jnp.maximum(btot_c[:, h:h + 1] + m_s[h], jnp.max(wst_r[h], axis=1, keepdims=True)) for h in heads]
    dstate = [jnp.exp(btot_c[:, h:h + 1] + m_s[h] - m_new[h]) for h in heads]
    kw = [ks[h].astype(F32) * jnp.exp(wst_c[h] - m_new[h]) for h in heads]
    upd = [_dot(kw[h].T.astype(BF16), vs[h]) for h in heads]
    n_rows = [dstate[h] * n_s[h] + jnp.sum(kw[h], axis=0, keepdims=True) for h in heads]
    m_rows = [jnp.broadcast_to(m_new[h], (1, LANES)) for h in heads]

    for h in heads:
        cols = slice(dv * h, dv * (h + 1))
        if final:
            hs = hh[h] + hf_ref[rs, cols]
            hn = hs * lax.rsqrt(jnp.mean(hs * hs, axis=-1, keepdims=True) + EPS) * nw_ref[:, cols]
            o_ref[rs, cols] = (hn * _sigmoid(og_ref[rs, cols].astype(F32))).astype(o_ref.dtype)
        else:
            o_ref[rs, cols] = hh[h]

    return ([dstate[h] * cts[h] + upd[h] for h in heads], jnp.concatenate(n_rows, axis=0),
            jnp.concatenate(m_rows, axis=0))


def _mlstm(qk, pr, g_c, g_r, *, direction, n_lat, final_args=None):
    nb, t, _ = qk.shape
    q = SCAN_CPS * SCAN_Q
    assert n_lat % q == 0 and t % q == 0
    nch = t // q
    reverse = direction == 1
    cmap = _chunk_map(reverse, nch, n_lat // q)
    d = direction
    in_specs = [
        pl.BlockSpec((None, q, ML_QK_WIDTH), lambda b, k: (b, cmap(k), 0)),
        pl.BlockSpec((None, q, ML_QK_WIDTH), lambda b, k: (b, cmap(k), 1)),
        pl.BlockSpec((None, q, ML_WIDTH), lambda b, k: (b, cmap(k), 2 * ML_QK_WIDTH // ML_WIDTH)),
        pl.BlockSpec((None, None, q, 2 * ML_HEADS), lambda b, k: (b, d, cmap(k), 0)),
        pl.BlockSpec((None, None, 2 * ML_HEADS, q), lambda b, k: (b, d, 0, cmap(k))),
    ]
    args = [qk, qk, pr, g_c, g_r]
    final = final_args is not None
    if final:
        hf, norm_w = final_args
        in_specs += [pl.BlockSpec((None, q, ML_WIDTH), lambda b, k: (b, cmap(k), 0)),
                     pl.BlockSpec((None, q, ML_WIDTH), lambda b, k: (b, cmap(k), 2 * ML_QK_WIDTH // ML_WIDTH + 1)),
                     pl.BlockSpec((1, ML_WIDTH), lambda b, k: (0, 0))]
        args += [hf, pr, norm_w]
    out_shape = jax.ShapeDtypeStruct((nb, t, ML_WIDTH), BF16 if final else F32)
    return pl.pallas_call(
        functools.partial(_mlstm_body, reverse=reverse, final=final),
        out_shape=out_shape, grid=(nb, nch), in_specs=in_specs,
        out_specs=pl.BlockSpec((None, q, ML_WIDTH), lambda b, k: (b, cmap(k), 0)),
        scratch_shapes=[pltpu.VMEM((ML_HEADS, ML_QK_DIM, ML_V_DIM), F32), pltpu.VMEM((ML_HEADS, ML_QK_DIM), F32),
                        pltpu.VMEM((ML_HEADS, LANES), F32)],
        name="mlstm_bwd" if reverse else "mlstm_fwd",
        compiler_params=_cp(("parallel", "arbitrary")))(*args)


def _na_window_start(j, rows):
    return jnp.clip(j * NA_QROWS - NA_ROWS // 2, 0, rows - NA_KROWS)


def _na_part_offset(j, part, start, clip=jnp.clip):
    first = j * NA_QROWS + part * (NA_QROWS // NA_PARTS) - NA_ROWS // 2
    return clip(first - start, 0, NA_KROWS - NA_PART_KROWS) // NA_TAB_ROWS * NA_TAB_ROWS


def _na_body(q_ref, k_ref, v_ref, tab_ref, qw_ref, kw_ref, o_ref, kn_ref, v1_ref, *, n_lat, n_ctx):
    dh = NA_HEAD_DIM
    scale = dh ** -0.5
    rows = n_lat // GRID_W
    j = pl.program_id(2)
    heads = range(NA_HPS)
    hcols = [slice(hh * dh, (hh + 1) * dh) for hh in heads]

    @pl.when(j == 0)
    def _():
        prep = (n_lat + n_ctx) // NA_PREP_STEPS
        one_hot = (lax.broadcasted_iota(jnp.int32, (prep, LANES), 1) == 0).astype(v1_ref.dtype)

        def body(i, carry):
            r = pl.ds(pl.multiple_of(i * prep, 16), prep)
            for hh in heads:
                kk = k_ref[r, hcols[hh]].astype(F32)
                kn = kk * lax.rsqrt(jnp.mean(kk * kk, axis=-1, keepdims=True) + EPS) * kw_ref[...]
                kn_ref[r, hcols[hh]] = kn.astype(kn_ref.dtype)
                v1_ref[hh, r, 0:dh] = v_ref[r, hcols[hh]]
                v1_ref[hh, r, dh:dh + LANES] = one_hot
            return carry

        lax.fori_loop(0, NA_PREP_STEPS, body, 0)

    start = _na_window_start(j, rows)
    ctx = pl.ds(n_lat, n_ctx)
    rpp = NA_QROWS // NA_PARTS
    pq = rpp * GRID_W
    chains = [(hh, i) for hh in heads for i in range(NA_PARTS)]
    off = [_na_part_offset(j, i, start) for i in range(NA_PARTS)]
    win = [pl.ds(pl.multiple_of((start + off[i]) * GRID_W, GRID_W), NA_PART_KROWS * GRID_W) for i in range(NA_PARTS)]
    qq = [q_ref[i * pq:(i + 1) * pq, hcols[hh]].astype(F32) for hh, i in chains]
    qn = [(x * lax.rsqrt(jnp.mean(x * x, axis=-1, keepdims=True) + EPS) * (qw_ref[...] * scale)).astype(BF16)
          for x in qq]
    sw = [_dot_nt(qn[c], kn_ref[win[i], hcols[hh]]) for c, (hh, i) in enumerate(chains)]
    sc = [_dot_nt(qn[c], kn_ref[ctx, hcols[hh]]) for c, (hh, i) in enumerate(chains)]

    def with_bias(c, hh, i):
        out = []
        for r in range(rpp):
            d0 = start - (j * NA_QROWS + i * rpp + r) + NA_ROWS - 1 + NA_QROWS
            blocks = tab_ref[hh, d0, pl.ds(off[i] // NA_TAB_ROWS, NA_PART_KROWS // NA_TAB_ROWS)]
            bias = jnp.concatenate([blocks[b] for b in range(NA_PART_KROWS // NA_TAB_ROWS)], axis=1)
            out.append(sw[c][r * GRID_W:(r + 1) * GRID_W] + bias)
        return jnp.concatenate(out, axis=0)

    sb = [with_bias(c, hh, i) for c, (hh, i) in enumerate(chains)]
    m = [jnp.maximum(jnp.max(a, axis=1, keepdims=True), jnp.max(b, axis=1, keepdims=True)) for a, b in zip(sb, sc)]
    pw = [jnp.exp(a - mm) for a, mm in zip(sb, m)]
    pc = [jnp.exp(a - mm) for a, mm in zip(sc, m)]
    o = [_dot(pw[c].astype(BF16), v1_ref[hh, win[i], :]) + _dot(pc[c].astype(BF16), v1_ref[hh, ctx, :])
         for c, (hh, i) in enumerate(chains)]
    for c, (hh, i) in enumerate(chains):
        o_ref[i * pq:(i + 1) * pq, hcols[hh]] = (o[c][:, 0:dh] / o[c][:, dh:dh + 1]).astype(o_ref.dtype)


def _na_bias_table(rpb, rows):
    nd = 2 * NA_QROWS
    valid = np.zeros((nd, NA_KROWS), bool)
    seen = np.zeros((nd,), bool)
    for j in range(rows // NA_QROWS):
        start = int(np.clip(j * NA_QROWS - NA_ROWS // 2, 0, rows - NA_KROWS))
        for qr in range(NA_QROWS):
            r = j * NA_QROWS + qr
            rs = int(np.clip(r - NA_ROWS // 2, 0, rows - NA_ROWS))
            d = start - r + NA_ROWS - 1 + NA_QROWS
            v = np.array([(rs <= start + i < rs + NA_ROWS) for i in range(NA_KROWS)])
            assert 0 <= d < nd and (not seen[d] or (valid[d] == v).all())
            valid[d], seen[d] = v, True
            off = int(_na_part_offset(j, qr // (NA_QROWS // NA_PARTS), start, clip=np.clip))
            assert off % NA_TAB_ROWS == 0 and not v[:off].any() and not v[off + NA_PART_KROWS:].any()
    col = np.arange(GRID_W)
    cs = np.clip(col - NA_COLS // 2, 0, GRID_W - NA_COLS)
    in_win = (col[None, :] >= cs[:, None]) & (col[None, :] < cs[:, None] + NA_COLS)
    ndr, ndc, w = 2 * NA_ROWS - 1, 2 * NA_COLS - 1, GRID_W
    dr = (np.arange(nd)[:, None] - NA_QROWS) + np.arange(NA_KROWS)[None, :]
    ok = valid[:, None, :, None] & in_win[None, :, None, :] & (dr >= 0)[:, None, :, None] \
        & (dr < ndr)[:, None, :, None]
    ext = jnp.pad(rpb, ((0, 0), (0, 0), (w, w)), mode="edge")
    c0 = w + NA_COLS - 1
    t1 = jnp.stack([ext[:, :, c0 - qc:c0 - qc + w] for qc in range(w)], axis=2)
    t1p = jnp.pad(t1, ((0, 0), (NA_QROWS, nd + NA_KROWS - 1 - NA_QROWS - ndr), (0, 0), (0, 0)))
    wide = jnp.concatenate([t1p[:, m] for m in range(nd + NA_KROWS - 1)], axis=-1)
    wpad = LANES * pl.cdiv(wide.shape[-1], LANES)
    wide = jnp.pad(wide, ((0, 0), (0, 0), (0, wpad - wide.shape[-1])))
    nblk, bw = NA_KROWS // NA_TAB_ROWS, NA_TAB_ROWS * w
    okf = jnp.asarray(ok.reshape(nd, w, nblk, bw).transpose(0, 2, 1, 3).astype(np.float32))

    def body(wide_ref, ok_ref, o_ref):
        for d in range(nd):
            for b in range(nblk):
                lo = d * w + b * bw
                o_ref[d, b] = jnp.where(ok_ref[d, b] != 0.0, wide_ref[:, lo:lo + bw], -jnp.inf)

    return pl.pallas_call(
        body, out_shape=jax.ShapeDtypeStruct((rpb.shape[0], nd, nblk, w, bw), F32), grid=(rpb.shape[0],),
        in_specs=[pl.BlockSpec((None, w, wpad), lambda h: (h, 0, 0)),
                  pl.BlockSpec((nd, nblk, w, bw), lambda h: (0, 0, 0, 0))],
        out_specs=pl.BlockSpec((None, nd, nblk, w, bw), lambda h: (h, 0, 0, 0, 0)),
        name="na_bias_table", compiler_params=_cp(("parallel",)))(wide, okf)


def _na(pr, tab, q_w, k_w, *, n_lat, n_ctx, qcol):
    nb, t, _ = pr.shape
    dh, nh, hps = NA_HEAD_DIM, NA_HEADS, NA_HPS
    nq = NA_QROWS * GRID_W
    hw = hps * dh
    assert qcol % hw == 0 and nh % hps == 0
    qb = qcol // hw
    return pl.pallas_call(
        functools.partial(_na_body, n_lat=n_lat, n_ctx=n_ctx),
        out_shape=jax.ShapeDtypeStruct((nb, n_lat, NA_WIDTH), BF16), grid=(nb, nh // hps, n_lat // nq),
        in_specs=[pl.BlockSpec((None, nq, hw), lambda b, h, j: (b, j, qb + h)),
                  pl.BlockSpec((None, t, hw), lambda b, h, j: (b, 0, qb + nh // hps + h)),
                  pl.BlockSpec((None, t, hw), lambda b, h, j: (b, 0, qb + 2 * (nh // hps) + h)),
                  pl.BlockSpec((hps,) + tab.shape[1:], lambda b, h, j: (h, 0, 0, 0, 0)),
                  pl.BlockSpec((1, dh), lambda b, h, j: (0, 0)),
                  pl.BlockSpec((1, dh), lambda b, h, j: (0, 0))],
        out_specs=pl.BlockSpec((None, nq, hw), lambda b, h, j: (b, j, h)),
        scratch_shapes=[pltpu.VMEM((t, hw), BF16), pltpu.VMEM((hps, t, dh + LANES), BF16)], name="na",
        compiler_params=_cp(("parallel", "parallel", "arbitrary")))(pr, pr, pr, tab, q_w, k_w)


def _rope_tables(n_lat, n_ctx, width):
    dh = ML_QK_DIM
    nf = dh // 4
    t = jnp.arange(n_lat)
    inv = ROPE_THETA ** (-jnp.arange(nf, dtype=F32) / nf)
    lane = np.arange(dh)
    pos = jnp.where((lane < dh // 2)[None, :], (t // GRID_W)[:, None], (t % GRID_W)[:, None]).astype(F32)
    ang = pos * inv[lane % nf][None, :]
    sign = np.where((lane % (dh // 2)) < nf, -1.0, 1.0).astype(np.float32)
    cos = jnp.concatenate([jnp.cos(ang), jnp.ones((n_ctx, dh), F32)], axis=0)
    sin = jnp.concatenate([jnp.sin(ang) * sign[None, :], jnp.zeros((n_ctx, dh), F32)], axis=0)
    reps = width // dh
    return jnp.tile(cos, (1, reps)), jnp.tile(sin, (1, reps))


def _ada_mod(c, c_ctx, w_all, b_all, layer):
    nb, d = c.shape
    rows = 16
    cv = jnp.concatenate([c, c_ctx[None, :], jnp.zeros((rows - nb - 1, d), F32)], axis=0)
    a = (cv * _sigmoid(cv)).astype(BF16)[None]
    out = _matmul(a, w_all, w_index=layer, tm=rows, tn=1024, tk=d, out_dtype=F32, name="ada_mod")[0]
    out = out[:nb + 1] + b_all[layer][None, :]
    return out.reshape(nb + 1, 6, d).transpose(1, 0, 2)


def _lat_ctx(m, nb):
    ctx = jnp.broadcast_to(m[nb][None, :], (nb, m.shape[1]))
    return jnp.stack([m[:nb], ctx], axis=1)[:, :, None, :]


def _pad_cols(w, n):
    return jnp.pad(w, ((0, 0), (0, n - w.shape[1])))


def _col_row_forms(g, groups):
    nb, t, _ = g.shape
    g5 = g.reshape(nb, t, 2, groups, -1)
    return g5.transpose(0, 2, 3, 1, 4), g5.transpose(0, 2, 3, 4, 1)


ROW_TILES = (1088, 1024, 512, 256)


def _row_tile(rows):
    return next(tm for tm in ROW_TILES if rows % tm == 0)


def _mod_pair(mods, shift, scale, nb):
    return jnp.stack([_lat_ctx(mods[shift], nb), _lat_ctx(mods[scale], nb)], axis=1)


def _in_projection(u, w_all, layer, *, rows, name, cols=None):
    tm = _row_tile(rows)
    tm = 2 * tm if rows % (2 * tm) == 0 else tm
    return _matmul(u, w_all, w_index=layer, cols=cols, tm=tm, tn=1024, tk=u.shape[2], out_dtype=BF16,
                   rows=rows, vmem=VMEM_LIMIT_BIG, name=name)


def _out_projection(a_list, w_all, x, gate, layer, *, rows, n_lat, next_norm, name):
    tm = _row_tile(rows)
    tn = 512 if w_all.shape[1] <= 2 * D_MODEL else 256
    if next_norm is None:
        return _matmul_gated_residual_rows(a_list, w_all, x, gate, w_index=layer, tm=tm, tn=tn, rows=rows,
                                           n_lat=n_lat, name=name), None
    return _matmul_gated_residual_rows(a_list, w_all, x, gate, w_index=layer, tm=tm // 2, tn=tn, rows=rows,
                                       n_lat=n_lat, norm=next_norm, name=name)


def _conv_ffn_block(x, u, mods, w_up_all, conv_w, conv_b, w_down_all, layer, *, rows, n_lat, segs, next_norm):
    nb = x.shape[0]
    up = _in_projection(u, w_up_all, layer, rows=rows, name="ffn_up")
    hid = _conv_ffn(up, conv_w, conv_b[None, :], segs=segs, rows=rows)
    return _out_projection((hid,), w_down_all, x, _lat_ctx(mods[5], nb), layer, rows=rows, n_lat=n_lat,
                           next_norm=next_norm, name="ffn_down")


def _even_layer(x, u, mods, w_in_all, conv_w, conv_b, dt_bias, a_log, d_skip, ssd_norm_w,
                hy_w1, hy_b1, hy_w2, hy_b2, hy_w3, hy_freq, hy_bias, w_out_all, j, *, n_lat, n_ctx, next_norm):
    nb, t, d = x.shape
    segs = ((0, n_lat), (n_lat, n_ctx))
    tm = _row_tile(t)
    n_main = SSD_WIDTH + EV_CONV_CH
    pr = _in_projection(u, w_in_all, j, rows=t, cols=n_main, name="ev_in")
    w_dt = _pad_cols(w_in_all[j, :, n_main:], LANES).astype(BF16)
    dtp = _matmul(u, w_dt[None], tm=tm, tn=LANES, tk=d, out_dtype=F32, name="ev_in_dt")[:, :, :2 * SSD_HEADS]
    cb = conv_b[None, :]
    xbc = _conv_silu(pr, conv_w, cb, xcol=SSD_WIDTH, wcol=0, width=SSD_XBC, segs=segs)
    x0, zin_l, zin_c, zle, zlo, zce, zco = _conv_hyena(pr, conv_w, cb, xcol=SSD_WIDTH + SSD_XBC, wcol=SSD_XBC,
                                                       segs=segs)

    dt_c, dt_r = _col_row_forms(dtp, SSD_GROUPS)
    db = dt_bias.reshape(2, SSD_GROUPS, 1, SSD_HPG)
    al = a_log.reshape(2, SSD_GROUPS, 1, SSD_HPG)
    scan_args = (xbc, dt_c, dt_r, db, db.transpose(0, 1, 3, 2), al, al.transpose(0, 1, 3, 2))
    yf = _ssd(*scan_args, direction=0, n_lat=n_lat)
    dsk = jnp.repeat(d_skip, SSD_HEAD_DIM)[None, :]
    y_ssd = _ssd(*scan_args, direction=1, n_lat=n_lat, final_args=(yf, pr, dsk, ssd_norm_w[None, :]))

    hyb = hy_bias[None, :]
    filt = functools.partial(_hyena_filters, w1=hy_w1, b1=hy_b1, w2=hy_w2, b2=hy_b2, w3=hy_w3, freq=hy_freq)
    lat = _hyena_longconv(zle, zlo, filt(n_lat)) + (zin_l,)
    ctx = _hyena_longconv(zce, zco, filt(n_ctx)) + (zin_c,)
    y_hy = _hyena_out((lat, ctx), x0, hyb, segs=segs)
    return _out_projection((y_ssd, y_hy), w_out_all, x, _lat_ctx(mods[2], nb), j, rows=t, n_lat=n_lat,
                           next_norm=next_norm, name="ev_out")


def _odd_layer_latent(x, u, mods, w_in, conv_w, conv_b, gate_b, ml_norm_w, q_norm_w, k_norm_w, rpb,
                      w_out_all, j, *, n_lat, n_ctx, next_norm):
    nb, t, d = x.shape
    segs = ((0, n_lat), (n_lat, n_ctx))
    tm = _row_tile(t)
    o3 = 2 * ML_QK_WIDTH + 2 * ML_WIDTH
    o4 = o3 + 4 * ML_HEADS
    wb = w_in.astype(BF16)
    w_main = jnp.concatenate([wb[:, :o3], wb[:, o4:]], axis=1)
    pr = _in_projection(u, w_main[None], 0, rows=t, name="od_in")
    gates = _matmul(u, _pad_cols(wb[:, o3:o4], LANES)[None], tm=tm, tn=LANES, tk=d, out_dtype=F32,
                    name="od_in_gates")[:, :, :4 * ML_HEADS] + gate_b.reshape(-1)
    cos, sin = _rope_tables(n_lat, n_ctx, 256)
    qk = _conv_rope(pr, conv_w, conv_b[None, :], cos, sin, width=2 * ML_QK_WIDTH, segs=segs)
    g5 = gates.reshape(nb, t, 2, 2 * ML_HEADS)
    g_c, g_r = g5.transpose(0, 2, 1, 3), g5.transpose(0, 2, 3, 1)
    hf = _mlstm(qk, pr, g_c, g_r, direction=0, n_lat=n_lat)
    y_ml = _mlstm(qk, pr, g_c, g_r, direction=1, n_lat=n_lat, final_args=(hf, ml_norm_w[None, :]))
    tab = _na_bias_table(rpb, n_lat // GRID_W)
    y_na = _na(pr, tab, q_norm_w[None, :], k_norm_w[None, :], n_lat=n_lat, n_ctx=n_ctx, qcol=o3)
    return _out_projection((y_ml, y_na), w_out_all, x, _lat_ctx(mods[2], nb), j, rows=n_lat, n_lat=n_lat,
                           next_norm=next_norm, name="od_out")


def kernel(x, c, ctx, c_ctx, ada_w, ada_b, norm_w, ev_w_in, ev_conv_w, ev_conv_b, ssd_dt_bias, ssd_a_log, ssd_d,
           ssd_norm_w, hy_w1, hy_b1, hy_w2, hy_b2, hy_w3, hy_freq, hy_bias, ev_w_out, od_w_in, ml_conv_w, ml_conv_b,
           ml_gate_b, ml_norm_w, na_q_norm_w, na_k_norm_w, na_rpb, od_w_out, ffn_w_up, ffn_conv_w, ffn_conv_b,
           ffn_w_down):
    depth = ada_w.shape[0]
    assert depth == 2, "layer schedule below is written for one even and one odd layer"
    n_lat, n_ctx = x.shape[1], ctx.shape[1]
    t = n_lat + n_ctx
    both = ((0, n_lat), (n_lat, n_ctx))
    xs = jnp.concatenate([x, ctx], axis=1)

    nb = x.shape[0]
    w_down = ffn_w_down.astype(BF16)
    mods0 = _ada_mod(c, c_ctx, ada_w, ada_b, 0)
    mods1 = _ada_mod(c, c_ctx, ada_w, ada_b, 1)
    norm = lambda layer, sub, mods: (norm_w[layer, sub][None, :], _mod_pair(mods, 3 * sub, 3 * sub + 1, nb))

    w0, mod0 = norm(0, 0, mods0)
    u = _rmsmod(xs, w0, mod0, tm=_row_tile(t), rows=t, n_lat=n_lat)
    xs, u = _even_layer(xs, u, mods0, ev_w_in, ev_conv_w[0], ev_conv_b[0], ssd_dt_bias[0], ssd_a_log[0],
                        ssd_d[0], ssd_norm_w[0], hy_w1[0], hy_b1[0], hy_w2[0], hy_b2[0], hy_w3[0], hy_freq[0],
                        hy_bias[0], ev_w_out.astype(BF16), 0, n_lat=n_lat, n_ctx=n_ctx, next_norm=norm(0, 1, mods0))
    xs, u = _conv_ffn_block(xs, u, mods0, ffn_w_up, ffn_conv_w[0], ffn_conv_b[0], w_down, 0,
                            rows=t, n_lat=n_lat, segs=both, next_norm=norm(1, 0, mods1))
    xl, u = _odd_layer_latent(xs, u, mods1, od_w_in[0], ml_conv_w[0], ml_conv_b[0], ml_gate_b[0], ml_norm_w[0],
                              na_q_norm_w[0], na_k_norm_w[0], na_rpb[0], od_w_out.astype(BF16), 0,
                              n_lat=n_lat, n_ctx=n_ctx, next_norm=norm(1, 1, mods1))
    out, _ = _conv_ffn_block(xl, u, mods1, ffn_w_up, ffn_conv_w[1], ffn_conv_b[1], w_down, 1,
                             rows=n_lat, n_lat=n_lat, segs=((0, n_lat),), next_norm=None)
    return out
```

```python
import functools
import math

import numpy as np
import jax
import jax.numpy as jnp
from jax import lax
from jax.experimental import pallas as pl
from jax.experimental.pallas import tpu as pltpu

F32 = jnp.float32
BF16 = jnp.bfloat16
HI = lax.Precision.HIGHEST

D_MODEL = 2048
GRID_W = 64
EPS = 1e-6
CONV_W = 3
SSD_WIDTH = D_MODEL
SSD_HEAD_DIM = 64
SSD_HEADS = SSD_WIDTH // SSD_HEAD_DIM
SSD_STATE = 128
SSD_GROUPS = 4
SSD_HPG = SSD_HEADS // SSD_GROUPS
SSD_GW = SSD_WIDTH // SSD_GROUPS
SSD_XBC = SSD_WIDTH + 2 * SSD_GROUPS * SSD_STATE
HY_WIDTH = D_MODEL
HY_EMB = 33
HY_BANDS = (HY_EMB - 1) // 2
HY_ORDER = 64
HY_FAST = 0.3
HY_SLOW = 1.5
HY_TARGET = 1e-2
ML_HEADS = 8
ML_WIDTH = D_MODEL
ML_V_DIM = ML_WIDTH // ML_HEADS
ML_QK_DIM = ML_V_DIM // 2
ML_QK_WIDTH = ML_HEADS * ML_QK_DIM
ROPE_THETA = 10000.0
NA_WIDTH = D_MODEL
NA_HEAD_DIM = 128
NA_HEADS = NA_WIDTH // NA_HEAD_DIM
NA_ROWS = 8
NA_COLS = 16
NA_QROWS = 8
NA_KROWS = 16
NA_PARTS = 2
NA_HPS = 2
NA_PREP_STEPS = 8
NA_TAB_ROWS = 4
NA_PART_KROWS = 12
EV_CONV_CH = SSD_XBC + 3 * HY_WIDTH
SCAN_Q = 128
SCAN_CPS = 2
LANES = 128
SUBLANES = 8
MIB = 1024 * 1024
V7X_VMEM_BYTES = 64 * MIB
VMEM_LIMIT = 48 * MIB
VMEM_LIMIT_BIG = V7X_VMEM_BYTES - 6 * MIB


def _cp(sem, vmem=VMEM_LIMIT):
    return pltpu.CompilerParams(dimension_semantics=sem, vmem_limit_bytes=vmem)


def _sigmoid(x):
    return 0.5 * jnp.tanh(0.5 * x) + 0.5


def _silu(x):
    return x * _sigmoid(x)


def _softplus(x):
    return jnp.maximum(x, 0.0) + jnp.log(1.0 + jnp.exp(-jnp.abs(x)))


def _log_sigmoid(x):
    return -_softplus(-x)


def _dot(a, b, precision=None):
    return jnp.dot(a, b, preferred_element_type=F32, precision=precision)


def _dot_nt(a, b):
    return lax.dot_general(a, b, (((1,), (1,)), ((), ())), preferred_element_type=F32)


def _split_bf16(x, terms):
    out = []
    for _ in range(terms):
        part = x.astype(BF16)
        out.append(part)
        x = x - part.astype(F32)
    return out


def _dot_sel(x, sel, terms):
    selb = sel.astype(BF16)
    return sum(_dot(part, selb) for part in _split_bf16(x, terms))


def _sel_dot(sel, x, terms):
    selb = sel.astype(BF16)
    return sum(_dot(selb, part) for part in _split_bf16(x, terms))


def _mm_body(a_ref, w_ref, o_ref, *scratch, nk, cache_w, w_nk):
    if cache_w:
        wb_ref = scratch[0]

        @pl.when((pl.program_id(1) == 0) & (pl.program_id(2) == 0))
        def _():
            wb_ref[...] = (w_ref[...].T if w_nk else w_ref[...]).astype(BF16)

        w_val = wb_ref[...]
    else:
        w_val = w_ref[...].astype(BF16)
    part = _dot(a_ref[...].astype(BF16), w_val)
    if nk == 1:
        o_ref[...] = part.astype(o_ref.dtype)
        return
    acc_ref = scratch[0]
    k = pl.program_id(3)

    @pl.when(k == 0)
    def _():
        acc_ref[...] = part

    @pl.when(k > 0)
    def _():
        acc_ref[...] += part

    @pl.when(k == nk - 1)
    def _():
        o_ref[...] = acc_ref[...].astype(o_ref.dtype)


def _matmul(a, w, *, tm, tn, tk, out_dtype, rows=None, cols=None, w_index=None, w_nk=False, vmem=VMEM_LIMIT,
            name="mm"):
    ba, ta, kdim = a.shape
    bw, kdim2, n = (w.shape[0], w.shape[2], w.shape[1]) if w_nk else w.shape
    assert kdim == kdim2
    nb = ba if w_index is not None else max(ba, bw)
    rows = ta if rows is None else rows
    n = n if cols is None else cols
    assert rows % tm == 0 and n % tn == 0 and kdim % tk == 0
    ni, nj, nk = rows // tm, n // tn, kdim // tk
    a_map = (lambda j, b, i, k: (b, i, k)) if ba > 1 else (lambda j, b, i, k: (0, i, k))
    if w_index is not None:
        w_map = lambda j, b, i, k: (w_index, k, j)
    else:
        w_map = (lambda j, b, i, k: (b, k, j)) if bw > 1 else (lambda j, b, i, k: (0, k, j))
    cache_w = w.dtype != BF16 and nk == 1 and (w_index is not None or bw == 1)
    assert cache_w or not w_nk
    scratch = [pltpu.VMEM((tm, tn), F32)] if nk > 1 else []
    if cache_w:
        scratch = [pltpu.VMEM((tk, tn), BF16)]
    inner = "arbitrary" if cache_w else "parallel"
    if w_nk:
        w_spec = pl.BlockSpec((None, tn, tk), lambda j, b, i, k: (w_map(j, b, i, k)[0], j, k))
    else:
        w_spec = pl.BlockSpec((None, tk, tn), w_map)
    return pl.pallas_call(
        functools.partial(_mm_body, nk=nk, cache_w=cache_w, w_nk=w_nk),
        out_shape=jax.ShapeDtypeStruct((nb, rows, n), out_dtype), grid=(nj, nb, ni, nk),
        in_specs=[pl.BlockSpec((None, tm, tk), a_map), w_spec],
        out_specs=pl.BlockSpec((None, tm, tn), lambda j, b, i, k: (b, i, j)),
        scratch_shapes=scratch, name=name,
        compiler_params=_cp(("parallel", inner, inner, "arbitrary"), vmem))(a, w)


def _row_is_lat(row_blk, tm, n_lat):
    rows = row_blk * tm + lax.broadcasted_iota(jnp.int32, (tm, 1), 0)
    return rows < n_lat


def _gated_residual_epilogue(acc, row_blk, res_ref, gate_ref, *, tm, n_lat):
    gate = jnp.where(_row_is_lat(row_blk, tm, n_lat), gate_ref[0], gate_ref[1])
    return res_ref[...] + gate * acc


def _gated_residual_rows_body(*refs, ks, tm, n_lat):
    n_a = len(ks)
    a_refs, w_ref, res_ref, gate_ref, o_ref = refs[:n_a], refs[n_a], refs[n_a + 1], refs[n_a + 2], refs[n_a + 3]
    acc, lo = None, 0
    for a_ref, kk in zip(a_refs, ks):
        part = _dot(a_ref[...], w_ref[lo:lo + kk, :])
        acc = part if acc is None else acc + part
        lo += kk
    o_ref[...] = _gated_residual_epilogue(acc, pl.program_id(1), res_ref, gate_ref, tm=tm, n_lat=n_lat)


def _matmul_gated_residual_rows(a_list, w, res, gate, *, w_index, tm, tn, rows, n_lat, name):
    nb = a_list[0].shape[0]
    _, kdim, n = w.shape
    ks = tuple(x.shape[2] for x in a_list)
    assert sum(ks) == kdim and rows % tm == 0 and n % tn == 0
    in_specs = [pl.BlockSpec((None, tm, kk), lambda b, i, j: (b, i, 0)) for kk in ks]
    in_specs += [pl.BlockSpec((None, kdim, tn), lambda b, i, j: (w_index, 0, j)),
                 pl.BlockSpec((None, tm, tn), lambda b, i, j: (b, i, j)),
                 pl.BlockSpec((None, 2, 1, tn), lambda b, i, j: (b, 0, 0, j))]
    return pl.pallas_call(
        functools.partial(_gated_residual_rows_body, ks=ks, tm=tm, n_lat=n_lat),
        out_shape=jax.ShapeDtypeStruct((nb, rows, n), F32), grid=(nb, rows // tm, n // tn), in_specs=in_specs,
        out_specs=pl.BlockSpec((None, tm, tn), lambda b, i, j: (b, i, j)), name=name,
        compiler_params=_cp(("parallel", "parallel", "parallel")))(*a_list, w, res, gate)


def _modulated_rmsnorm(x, is_lat, w_ref, mod_ref):
    y = x * lax.rsqrt(jnp.mean(x * x, axis=-1, keepdims=True) + EPS) * w_ref[...]
    shift = jnp.where(is_lat, mod_ref[0, 0], mod_ref[0, 1])
    scale = jnp.where(is_lat, mod_ref[1, 0], mod_ref[1, 1])
    return y * (1.0 + scale) + shift


def _rmsmod_body(x_ref, w_ref, mod_ref, o_ref, *, tm, n_lat):
    is_lat = _row_is_lat(pl.program_id(2), tm, n_lat)
    o_ref[...] = _modulated_rmsnorm(x_ref[...], is_lat, w_ref, mod_ref).astype(o_ref.dtype)


def _rmsmod(x, w, mod, *, tm, rows, n_lat):
    nb, _, d = x.shape
    tm = tm // 2
    assert rows % tm == 0 and tm % 16 == 0
    body = functools.partial(_rmsmod_body, tm=tm, n_lat=n_lat)
    return pl.pallas_call(
        body, out_shape=jax.ShapeDtypeStruct((nb, rows, d), BF16), grid=(1, nb, rows // tm),
        in_specs=[pl.BlockSpec((None, tm, d), lambda j, b, i: (b, i, 0)),
                  pl.BlockSpec((1, d), lambda j, b, i: (0, 0)),
                  pl.BlockSpec((None, 2, 2, 1, d), lambda j, b, i: (b, 0, 0, 0, 0))],
        out_specs=pl.BlockSpec((None, tm, d), lambda j, b, i: (b, i, 0)),
        name="rmsmod", compiler_params=_cp(("parallel", "parallel", "parallel")))(x, w, mod)


CONV_CH = 128
CONV_HALO = 16
CONV_STEPS = 4


def _shift_matrices():
    r = lax.broadcasted_iota(jnp.int32, (CONV_CH, CONV_CH + 2 * CONV_HALO), 0)
    c = lax.broadcasted_iota(jnp.int32, (CONV_CH, CONV_CH + 2 * CONV_HALO), 1)
    return (c == r + CONV_HALO - 1).astype(BF16), (c == r + CONV_HALO + 1).astype(BF16)


def _chunk_rows(r0, i):
    return pl.ds(r0 + pl.multiple_of(i * CONV_CH, CONV_CH), CONV_CH)


def _taps(w_ref, b_ref, scale=1.0):
    return w_ref[...] * scale, b_ref[...] * scale


def _silu_of_half(h):
    return h + h * jnp.tanh(h)


def _conv_at(x_ref, taps, shifts, r0, n, steps):
    w, b = taps
    cur = [x_ref[_chunk_rows(r0, i), :] for i in steps]
    zero = jnp.zeros((CONV_HALO, cur[0].shape[1]), cur[0].dtype)
    lo = [pl.multiple_of(jnp.maximum(i * CONV_CH - CONV_HALO, 0), CONV_HALO) for i in steps]
    hi = [pl.multiple_of(jnp.minimum((i + 1) * CONV_CH, n - CONV_HALO), CONV_HALO) for i in steps]
    tail = [jnp.where(i > 0, x_ref[pl.ds(r0 + l, CONV_HALO), :], zero) for i, l in zip(steps, lo)]
    head = [jnp.where(i < n // CONV_CH - 1, x_ref[pl.ds(r0 + h, CONV_HALO), :], zero) for i, h in zip(steps, hi)]
    win = [jnp.concatenate([t, c, h], axis=0) for t, c, h in zip(tail, cur, head)]
    prev = [_dot(shifts[0], w) for w in win]
    nxt = [_dot(shifts[1], w) for w in win]
    return [w[0:1, :] * p + w[1:2, :] * c.astype(F32) + w[2:3, :] * x + b for p, c, x in zip(prev, cur, nxt)]


def _conv_loop(segs, fn):
    for si, (r0, n) in enumerate(segs):
        def body(i, carry, r0=r0, n=n, si=si):
            fn(si, r0, n, [i])
            return carry

        lax.fori_loop(0, n // CONV_CH, body, 0, unroll=min(CONV_STEPS, n // CONV_CH))


def _conv_silu_body(x_ref, w_ref, b_ref, o_ref, *, segs):
    shifts = _shift_matrices()
    half_taps = _taps(w_ref, b_ref, 0.5)

    def fn(si, r0, n, steps):
        ys = [_silu_of_half(h) for h in _conv_at(x_ref, half_taps, shifts, r0, n, steps)]
        for i, y in zip(steps, ys):
            o_ref[_chunk_rows(r0, i), :] = y.astype(o_ref.dtype)

    _conv_loop(segs, fn)


def _conv_rope_body(x_ref, w_ref, b_ref, cos_ref, sin_ref, o_ref, *, segs, tc):
    shifts = _shift_matrices()
    lane = lax.broadcasted_iota(jnp.int32, (CONV_CH, tc), 1)
    first = (lane % (ML_QK_DIM // 2)) < (ML_QK_DIM // 4)
    nf = ML_QK_DIM // 4
    half_taps = _taps(w_ref, b_ref, 0.5)

    def fn(si, r0, n, steps):
        ys = [_silu_of_half(h) for h in _conv_at(x_ref, half_taps, shifts, r0, n, steps)]
        partner = [jnp.where(first, pltpu.roll(y, tc - nf, 1), pltpu.roll(y, nf, 1)) for y in ys]
        for i, y, p in zip(steps, ys, partner):
            rows = _chunk_rows(r0, i)
            o_ref[rows, :] = (y * cos_ref[rows, :] + p * sin_ref[rows, :]).astype(o_ref.dtype)

    _conv_loop(segs, fn)


def _conv_ffn_body(a_ref, x_ref, w_ref, b_ref, o_ref, *, segs):
    shifts = _shift_matrices()
    half_taps = _taps(w_ref, b_ref, 0.5)

    def fn(si, r0, n, steps):
        gs = [_silu_of_half(h) for h in _conv_at(x_ref, half_taps, shifts, r0, n, steps)]
        for i, g in zip(steps, gs):
            rows = _chunk_rows(r0, i)
            o_ref[rows, :] = (a_ref[rows, :].astype(F32) * g).astype(o_ref.dtype)

    _conv_loop(segs, fn)


def _conv_hyena_body(x0_ref, x1_ref, v_ref, w0_ref, b0_ref, w1_ref, b1_ref, w2_ref, b2_ref,
                     ox0_ref, ozl_ref, ozc_ref, ozle_ref, ozlo_ref, ozce_ref, ozco_ref, *zs_refs, segs):
    shifts = _shift_matrices()
    taps = (_taps(w0_ref, b0_ref), _taps(w1_ref, b1_ref), _taps(w2_ref, b2_ref))
    half = CONV_CH // 2
    outs = ((ozl_ref, ozle_ref, ozlo_ref), (ozc_ref, ozce_ref, ozco_ref))

    def fn(si, r0, n, steps):
        x0 = _conv_at(x0_ref, taps[0], shifts, r0, n, steps)
        x1 = _conv_at(x1_ref, taps[1], shifts, r0, n, steps)
        v = _conv_at(v_ref, taps[2], shifts, r0, n, steps)
        nat_ref, even_ref, odd_ref = outs[si]
        for i, x0_i, x1_i, v_i in zip(steps, x0, x1, v):
            ox0_ref[_chunk_rows(r0, i), :] = x0_i.astype(ox0_ref.dtype)
            zin = x1_i * v_i
            nat_ref[_chunk_rows(0, i), :] = zin.astype(BF16)
            t0 = r0 + pl.multiple_of(i * CONV_CH, CONV_CH)
            hrows = pl.ds(pl.multiple_of(i * half, half), half)
            for g, zs_ref in enumerate(zs_refs):
                lanes = slice(g * LANES, (g + 1) * LANES)
                zs_ref[pl.ds(t0, CONV_CH), :] = zin[:, lanes]
                even_ref[hrows, lanes] = zs_ref[pl.ds(t0, half, stride=2), :].astype(BF16)
                odd_ref[hrows, lanes] = zs_ref[pl.ds(t0 + 1, half, stride=2), :].astype(BF16)

    _conv_loop(segs, fn)


def _conv_specs(t_in, tc, xcol_blk, wcol_blk):
    return [pl.BlockSpec((None, t_in, tc), lambda b, j: (b, 0, j + xcol_blk)),
            pl.BlockSpec((CONV_W, tc), lambda b, j: (0, j + wcol_blk)),
            pl.BlockSpec((1, tc), lambda b, j: (0, j + wcol_blk))]


def _conv_silu(x, w, b, *, xcol, wcol, width, segs, tc=256):
    nb, t_in, _ = x.shape
    return pl.pallas_call(
        functools.partial(_conv_silu_body, segs=segs),
        out_shape=jax.ShapeDtypeStruct((nb, t_in, width), BF16), grid=(nb, width // tc),
        in_specs=_conv_specs(t_in, tc, xcol // tc, wcol // tc),
        out_specs=pl.BlockSpec((None, t_in, tc), lambda b, j: (b, 0, j)),
        name="conv_silu", compiler_params=_cp(("parallel", "parallel")))(x, w, b)


def _conv_rope(x, w, b, cos, sin, *, width, segs, tc=256):
    nb, t_in, _ = x.shape
    tab = pl.BlockSpec((t_in, tc), lambda b, j: (0, 0))
    return pl.pallas_call(
        functools.partial(_conv_rope_body, segs=segs, tc=tc),
        out_shape=jax.ShapeDtypeStruct((nb, t_in, width), BF16), grid=(nb, width // tc),
        in_specs=_conv_specs(t_in, tc, 0, 0) + [tab, tab],
        out_specs=pl.BlockSpec((None, t_in, tc), lambda b, j: (b, 0, j)),
        name="conv_rope", compiler_params=_cp(("parallel", "parallel")))(x, w, b, cos, sin)


def _conv_ffn(up, w, b, *, segs, rows, tc=256):
    nb, t_in, two_ff = up.shape
    ff = two_ff // 2
    return pl.pallas_call(
        functools.partial(_conv_ffn_body, segs=segs),
        out_shape=jax.ShapeDtypeStruct((nb, rows, ff), BF16), grid=(nb, ff // tc),
        in_specs=[pl.BlockSpec((None, rows, tc), lambda b, j: (b, 0, j))] + _conv_specs(rows, tc, ff // tc, 0),
        out_specs=pl.BlockSpec((None, rows, tc), lambda b, j: (b, 0, j)),
        name="conv_ffn", compiler_params=_cp(("parallel", "parallel")))(up, up, w, b)


def _conv_hyena(pr, w, b, *, xcol, wcol, segs, tc=256):
    nb, t_in, _ = pr.shape
    (_, n_lat), (_, n_ctx) = segs
    wd = HY_WIDTH
    out_rows = (t_in, n_lat, n_ctx, n_lat // 2, n_lat // 2, n_ctx // 2, n_ctx // 2)
    in_specs = [pl.BlockSpec((None, t_in, tc), functools.partial(lambda b, j, o: (b, 0, j + o), o=(xcol + m * wd) // tc))
                for m in range(3)]
    wargs = []
    for m in range(3):
        off = (wcol + m * wd) // tc
        in_specs.append(pl.BlockSpec((CONV_W, tc), functools.partial(lambda b, j, o: (0, j + o), o=off)))
        in_specs.append(pl.BlockSpec((1, tc), functools.partial(lambda b, j, o: (0, j + o), o=off)))
        wargs += [w, b]
    return pl.pallas_call(
        functools.partial(_conv_hyena_body, segs=segs),
        out_shape=tuple(jax.ShapeDtypeStruct((nb, rows, wd), BF16) for rows in out_rows),
        grid=(nb, wd // tc), in_specs=in_specs,
        out_specs=tuple(pl.BlockSpec((None, rows, tc), lambda b, j: (b, 0, j)) for rows in out_rows),
        scratch_shapes=[pltpu.VMEM((t_in, LANES), F32)] * (tc // LANES),
        name="conv_hyena", compiler_params=_cp(("parallel", "parallel")))(pr, pr, pr, *wargs)


def _scan_masks(reverse):
    q = SCAN_Q
    ii = lax.broadcasted_iota(jnp.int32, (q, q), 0)
    jj = lax.broadcasted_iota(jnp.int32, (q, q), 1)
    mask = (jj >= ii) if reverse else (jj <= ii)
    mask_t = (jj <= ii) if reverse else (jj >= ii)
    return mask, mask.astype(F32), mask_t.astype(F32)


def _chunk_map(reverse, n_chunks, n_lat_chunks):
    if reverse:
        return lambda k: n_chunks - 1 - k
    return lambda k: (k + n_lat_chunks) % n_chunks


SSD_GPS = 4


def _ssd_body(*refs, reverse, final):
    (xs_ref, b_ref, c_ref, dtc_ref, dtr_ref, dbc_ref, dbr_ref, alc_ref, alr_ref) = refs[:9]
    if final:
        yf_ref, z_ref, dsk_ref, nw_ref, o_ref, h_ref = refs[9:]
    else:
        o_ref, h_ref = refs[9:]

    @pl.when(pl.program_id(2) == 0)
    def _():
        h_ref[...] = jnp.zeros_like(h_ref)

    groups = range(SSD_GPS)
    h = [h_ref[g] for g in groups]
    for c in (reversed(range(SCAN_CPS)) if reverse else range(SCAN_CPS)):
        h = _ssd_chunk(refs, slice(c * SCAN_Q, (c + 1) * SCAN_Q), h, reverse=reverse, final=final)
    for g in groups:
        h_ref[g] = h[g]


def _ssd_chunk(refs, rs, h_prev, *, reverse, final):
    (xs_ref, b_ref, c_ref, dtc_ref, dtr_ref, dbc_ref, dbr_ref, alc_ref, alr_ref) = refs[:9]
    if final:
        yf_ref, z_ref, dsk_ref, nw_ref, o_ref = refs[9:14]
    else:
        o_ref = refs[9]
    q, gw, hd = SCAN_Q, SSD_GW, SSD_HEAD_DIM
    mask, maskf, mask_tf = _scan_masks(reverse)
    groups = range(SSD_GPS)
    pairs = range(SSD_HPG // 2)
    cols = [slice(g * gw, (g + 1) * gw) for g in groups]
    xs = [xs_ref[rs, cols[g]].astype(F32) for g in groups]
    bm = [b_ref[rs, g * SSD_STATE:(g + 1) * SSD_STATE] for g in groups]
    cm = [c_ref[rs, g * SSD_STATE:(g + 1) * SSD_STATE] for g in groups]
    dt_c = [_softplus(dtc_ref[g, rs, :] + dbc_ref[g]) for g in groups]
    dt_r = [_softplus(dtr_ref[g, :, rs] + dbr_ref[g]) for g in groups]
    da_c = [dt_c[g] * (-jnp.exp(alc_ref[g])) for g in groups]
    da_r = [dt_r[g] * (-jnp.exp(alr_ref[g])) for g in groups]
    acum_c_all = _sel_dot(maskf, jnp.concatenate(da_c, axis=1), 3)
    acum_r_all = _dot_sel(jnp.concatenate(da_r, axis=0), mask_tf, 3)
    acum_c = [acum_c_all[:, g * SSD_HPG:(g + 1) * SSD_HPG] for g in groups]
    acum_r = [acum_r_all[g * SSD_HPG:(g + 1) * SSD_HPG, :] for g in groups]
    atot = [jnp.sum(da_c[g], axis=0, keepdims=True) for g in groups]

    er = lax.broadcasted_iota(jnp.int32, (SSD_HPG, gw), 0)
    ec = lax.broadcasted_iota(jnp.int32, (SSD_HPG, gw), 1)
    expand = (ec // hd == er).astype(F32)
    wide = [_dot_sel(jnp.concatenate([dt_c[g], jnp.exp(atot[g] - acum_c[g]), jnp.exp(acum_c[g]),
                                      jnp.broadcast_to(jnp.exp(atot[g]), (SUBLANES, SSD_HPG))], axis=0), expand, 1)
            for g in groups]
    xq = [xs[g] * wide[g][0:q] for g in groups]
    cb = [_dot_nt(cm[g], bm[g]) for g in groups]
    y_off = [wide[g][2 * q:3 * q] * _dot(cm[g], h_prev[g].astype(BF16)) for g in groups]
    lane = lax.broadcasted_iota(jnp.int32, (q, 2 * hd), 1)

    def pair_lhs(g, p):
        ms = [(cb[g] * jnp.where(mask, jnp.exp(acum_c[g][:, r:r + 1] - acum_r[g][r:r + 1, :]), 0.0)).astype(BF16)
              for r in (2 * p, 2 * p + 1)]
        return jnp.concatenate(ms, axis=1)

    def pair_rhs(g, p):
        xp = xq[g][:, 2 * hd * p:2 * hd * (p + 1)]
        return jnp.concatenate([jnp.where(lane < hd, xp, 0.0), jnp.where(lane >= hd, xp, 0.0)], axis=0).astype(BF16)

    y_in = [[_dot(pair_lhs(g, p), pair_rhs(g, p)) for p in pairs] for g in groups]
    y = [y_off[g] + jnp.concatenate(y_in[g], axis=1) for g in groups]
    bt = [bm[g].astype(F32).T.astype(BF16) for g in groups]
    upd = [_dot(bt[g], (xq[g] * wide[g][q:2 * q]).astype(BF16)) for g in groups]
    for g in groups:
        if final:
            ytot = y[g] + yf_ref[rs, cols[g]] + xs[g] * dsk_ref[:, cols[g]]
            gated = ytot * _silu(z_ref[rs, cols[g]].astype(F32))
            o = gated * lax.rsqrt(jnp.mean(gated * gated, axis=-1, keepdims=True) + EPS) * nw_ref[:, cols[g]]
            o_ref[rs, cols[g]] = o.astype(o_ref.dtype)
        else:
            o_ref[rs, cols[g]] = y[g]
    return [wide[g][3 * q:3 * q + 1] * h_prev[g] + upd[g] for g in groups]


def _ssd(xbc, dt_c, dt_r, db_c, db_r, al_c, al_r, *, direction, n_lat, final_args=None):
    nb, t, _ = xbc.shape
    q, gw, hpg = SCAN_CPS * SCAN_Q, SSD_GW, SSD_HPG
    assert n_lat % q == 0 and t % q == 0
    nch = t // q
    reverse = direction == 1
    cmap = _chunk_map(reverse, nch, n_lat // q)
    gps = SSD_GPS
    sw, gww = gps * SSD_STATE, gps * gw
    b0 = SSD_WIDTH // sw
    c0 = b0 + SSD_GROUPS // gps
    d = direction
    in_specs = [
        pl.BlockSpec((None, q, gww), lambda b, g, k: (b, cmap(k), g)),
        pl.BlockSpec((None, q, sw), lambda b, g, k: (b, cmap(k), b0 + g)),
        pl.BlockSpec((None, q, sw), lambda b, g, k: (b, cmap(k), c0 + g)),
        pl.BlockSpec((None, None, gps, q, hpg), lambda b, g, k: (b, d, g, cmap(k), 0)),
        pl.BlockSpec((None, None, gps, hpg, q), lambda b, g, k: (b, d, g, 0, cmap(k))),
        pl.BlockSpec((None, gps, 1, hpg), lambda b, g, k: (d, g, 0, 0)),
        pl.BlockSpec((None, gps, hpg, 1), lambda b, g, k: (d, g, 0, 0)),
        pl.BlockSpec((None, gps, 1, hpg), lambda b, g, k: (d, g, 0, 0)),
        pl.BlockSpec((None, gps, hpg, 1), lambda b, g, k: (d, g, 0, 0)),
    ]
    args = [xbc, xbc, xbc, dt_c, dt_r, db_c, db_r, al_c, al_r]
    final = final_args is not None
    if final:
        yf, pr, dskip, norm_w = final_args
        in_specs += [pl.BlockSpec((None, q, gww), lambda b, g, k: (b, cmap(k), g)),
                     pl.BlockSpec((None, q, gww), lambda b, g, k: (b, cmap(k), g)),
                     pl.BlockSpec((1, gww), lambda b, g, k: (0, g)),
                     pl.BlockSpec((1, gww), lambda b, g, k: (0, g))]
        args += [yf, pr, dskip, norm_w]
    out_shape = jax.ShapeDtypeStruct((nb, t, SSD_WIDTH), BF16 if final else F32)
    return pl.pallas_call(
        functools.partial(_ssd_body, reverse=reverse, final=final),
        out_shape=out_shape, grid=(nb, SSD_GROUPS // gps, nch), in_specs=in_specs,
        out_specs=pl.BlockSpec((None, q, gww), lambda b, g, k: (b, cmap(k), g)),
        scratch_shapes=[pltpu.VMEM((gps, SSD_STATE, gw), F32)],
        name="ssd_bwd" if reverse else "ssd_fwd",
        compiler_params=_cp(("parallel", "parallel", "arbitrary")))(*args)


def _hyfilt_body(f_ref, w1_ref, b1_ref, w2_ref, b2_ref, w3_ref, fr_ref, dl_ref, o_ref, *, tl):
    f = f_ref[...]
    h = jnp.sin(fr_ref[0:1, :] * (_dot(f, w1_ref[...], HI) + b1_ref[...]))
    h = jnp.sin(fr_ref[1:2, :] * (_dot(h, w2_ref[...], HI) + b2_ref[...]))
    h = _dot(h, w3_ref[...], HI)
    dec = jnp.exp(-f[:, 0:1] * dl_ref[...])
    row = pl.program_id(0) * tl + lax.broadcasted_iota(jnp.int32, (tl, 1), 0)
    wd = HY_WIDTH
    o_ref[:, 0:wd] = (h[:, 0:wd] * dec).astype(o_ref.dtype)
    o_ref[:, wd:2 * wd] = jnp.where(row == 0, 0.0, h[:, wd:2 * wd] * dec).astype(o_ref.dtype)


def _hyena_filters(length, w1, b1, w2, b2, w3, freq, *, tl=256):
    p = LANES
    t = jnp.linspace(0.0, 1.0, length, dtype=F32)[:, None]
    w = 2.0 * math.pi * jnp.arange(length, dtype=F32)[:, None] / length
    f = jnp.linspace(1e-4, HY_BANDS - 1, HY_BANDS, dtype=F32)[None, :]
    feats = jnp.concatenate([t, jnp.cos(f * w), -jnp.sin(f * w)], axis=-1)
    feats = jnp.concatenate([feats[0::2], feats[1::2]], axis=0)
    feats = jnp.pad(feats, ((0, 0), (0, p - HY_EMB)))
    po = p - HY_ORDER
    w1p = jnp.pad(w1, ((0, p - HY_EMB), (0, po)))
    w2p = jnp.pad(w2, ((0, po), (0, po)))
    w3p = jnp.pad(w3, ((0, po), (0, 0)))
    b1p = jnp.pad(b1[None, :], ((0, 0), (0, po)))
    b2p = jnp.pad(b2[None, :], ((0, 0), (0, po)))
    frp = jnp.pad(freq, ((0, 0), (0, po)))
    deltas = jnp.abs(jnp.linspace(math.log(HY_TARGET) / HY_FAST, math.log(HY_TARGET) / HY_SLOW, HY_WIDTH,
                                  dtype=F32))[None, :]
    full = lambda shape: pl.BlockSpec(shape, lambda i: (0,) * len(shape))
    out = pl.pallas_call(
        functools.partial(_hyfilt_body, tl=tl),
        out_shape=jax.ShapeDtypeStruct((length, 2 * HY_WIDTH), BF16), grid=(length // tl,),
        in_specs=[pl.BlockSpec((tl, p), lambda i: (i, 0)), full((p, p)), full((1, p)), full((p, p)), full((1, p)),
                  full((p, 2 * HY_WIDTH)), full((2, p)), full((1, HY_WIDTH))],
        out_specs=pl.BlockSpec((tl, 2 * HY_WIDTH), lambda i: (i, 0)),
        name="hyena_filters", compiler_params=_cp(("parallel",)))(feats, w1p, b1p, w2p, b2p, w3p, frp, deltas)
    return out[None]


def _dft_matrices(length):
    n = 2 * length
    k = jnp.arange(length, dtype=jnp.int32)[:, None]
    t = jnp.arange(length, dtype=jnp.int32)[None, :]
    step = 1 << (int(math.log2(length)) // 2)
    part = jnp.arange(length // step, dtype=jnp.int32)[:, None]
    ang_hi = ((part * step * t) % n).astype(F32) * (2.0 * math.pi / n)
    ang_lo = ((jnp.arange(step, dtype=jnp.int32)[:, None] * t) % n).astype(F32) * (2.0 * math.pi / n)
    ch, sh = jnp.cos(ang_hi)[:, None, :], jnp.sin(ang_hi)[:, None, :]
    cl, sl = jnp.cos(ang_lo)[None, :, :], jnp.sin(ang_lo)[None, :, :]
    c = (ch * cl - sh * sl).reshape(length, length)
    s = (sh * cl + ch * sl).reshape(length, length)
    alt_t = jnp.where(t % 2 == 0, 1.0, -1.0).astype(F32)
    fwd = jnp.concatenate([c, jnp.where(k == 0, alt_t, -s)], axis=0)
    alt_r = jnp.where(k % 2 == 0, 1.0, -1.0).astype(F32)
    inv_re = jnp.where(t == 0, 1.0 / n, (2.0 / n) * c)
    inv_im = jnp.where(t == 0, alt_r / n, (-2.0 / n) * s)
    inv = jnp.concatenate([inv_re, inv_im], axis=1)
    return fwd.astype(BF16)[None], inv.astype(BF16)[None]


def _tile_interleave(x, axis, hm):
    shp = x.shape
    n = shp[axis] // 2
    x = x.reshape(shp[:axis] + (2, n // hm, hm) + shp[axis + 1:])
    return jnp.swapaxes(x, axis, axis + 1).reshape(shp)


def _twiddle_table(length, hm):
    k = np.arange(length // 2, dtype=np.float64)[:, None] * (np.pi / length)
    tab = np.concatenate([np.cos(k), np.sin(k)], axis=0)
    tab = tab.reshape(2, -1, hm, 1).swapaxes(0, 1).reshape(length, 1)
    return jnp.asarray(np.broadcast_to(tab, (length, LANES)).astype(np.float32))


def _lanes(x, width):
    return jnp.tile(x, (1, width // LANES))


def _parity_banks(e, ot, tw, hm):
    er, ei, otr, oti = e[0:hm], e[hm:], ot[0:hm], ot[hm:]
    c, s = _lanes(tw[0:hm], e.shape[1]), _lanes(tw[hm:], e.shape[1])
    orr = otr * c + oti * s
    oi = oti * c - otr * s
    return (er + orr, ei + oi, er - orr, ei - oi), (c, s)


def _is_bin0(row_blk, hm):
    return (row_blk == 0) & (lax.broadcasted_iota(jnp.int32, (hm, 1), 0) == 0)


def _hy_kbank_body(f_ref, hef_ref, heb_ref, hof_ref, hob_ref, tw_ref, ka_ref, kb_ref, *, hm):
    fwd, tw = f_ref[...], tw_ref[...]
    ef, eb, of, ob = (_dot(fwd, r[...]) for r in (hef_ref, heb_ref, hof_ref, hob_ref))
    (afr, afi, bfr, bfi), _ = _parity_banks(ef, of, tw, hm)
    (abr, abi, bbr, bbi), _ = _parity_banks(eb, ob, tw, hm)
    first = _is_bin0(pl.program_id(0), hm)
    kai = jnp.where(first, ef[hm:] + eb[hm:], afi - abi)
    kbi = jnp.where(first, ob[hm:] - of[hm:], bfi - bbi)
    ka_ref[...] = jnp.concatenate([afr + abr, kai], axis=0)
    kb_ref[...] = jnp.concatenate([bfr + bbr, kbi], axis=0)


def _hy_kbank(fwd, hfb, tw, *, tm, tc=512):
    _, ln, c2 = hfb.shape
    c, h = c2 // 2, ln // 2
    taps = lambda row_blk, col_off: pl.BlockSpec((None, h, tc), lambda i, j: (0, row_blk, j + col_off))
    bank = pl.BlockSpec((None, tm, tc), lambda i, j: (0, i, j))
    return pl.pallas_call(
        functools.partial(_hy_kbank_body, hm=tm // 2),
        out_shape=(jax.ShapeDtypeStruct((1, ln, c), F32),) * 2, grid=(ln // tm, c // tc),
        in_specs=[pl.BlockSpec((None, tm, h), lambda i, j: (0, i, 0)),
                  taps(0, 0), taps(0, c // tc), taps(1, 0), taps(1, c // tc),
                  pl.BlockSpec((tm, LANES), lambda i, j: (i, 0))],
        out_specs=(bank, bank), name="hy_kbank",
        compiler_params=_cp(("parallel", "parallel")))(fwd, hfb, hfb, hfb, hfb, tw)


def _hy_spectrum_body(f_ref, ze_ref, zo_ref, tw_ref, ka_ref, kb_ref, p_ref, q_ref, *, hm):
    fwd = f_ref[...]
    e, acc = _dot(fwd, ze_ref[...]), _dot(fwd, zo_ref[...])
    row_blk = pl.program_id(2)
    (ar, ai, br, bi), (c, s) = _parity_banks(e, acc, tw_ref[...], hm)
    kar, kai, kbr, kbi = ka_ref[0:hm], ka_ref[hm:], kb_ref[0:hm], kb_ref[hm:]
    yar, yai = ar * kar - ai * kai, ar * kai + ai * kar
    ybr, ybi = br * kbr - bi * kbi, br * kbi + bi * kbr
    qr, qi = yar - ybr, yai - ybi
    pr, pi = yar + ybr, yai + ybi
    qpr, qpi = qr * c - qi * s, qi * c + qr * s
    top = slice(0, SUBLANES)
    first = _is_bin0(row_blk, SUBLANES)
    y0, yl = ar[top] * kar[top], br[top] * kbr[top]
    amr, ami = e[hm:hm + SUBLANES], -acc[hm:hm + SUBLANES]
    ymr, ymi = amr * kai[top] - ami * kbi[top], amr * kbi[top] + ami * kai[top]

    def patch(full, special):
        return jnp.concatenate([jnp.where(first, special, full[top]), full[SUBLANES:]], axis=0)

    pr, pi = patch(pr, y0 + yl), patch(pi, 2.0 * ymr)
    qpr, qpi = patch(qpr, y0 - yl), patch(qpi, -2.0 * ymi)
    p_ref[...] = jnp.concatenate([pr, pi], axis=0).astype(p_ref.dtype)
    q_ref[...] = jnp.concatenate([qpr, qpi], axis=0).astype(q_ref.dtype)


def _hy_spectrum(fwd, ze, zo, tw, ka, kb, *, tm, tn):
    nb, h, c = ze.shape
    ln = fwd.shape[1]
    data = pl.BlockSpec((None, h, tn), lambda j, b, i: (b, 0, j))
    bank = pl.BlockSpec((None, tm, tn), lambda j, b, i: (0, i, j))
    out = pl.BlockSpec((None, tm, tn), lambda j, b, i: (b, i, j))
    return pl.pallas_call(
        functools.partial(_hy_spectrum_body, hm=tm // 2),
        out_shape=(jax.ShapeDtypeStruct((nb, ln, c), BF16),) * 2, grid=(c // tn, nb, ln // tm),
        in_specs=[pl.BlockSpec((None, tm, h), lambda j, b, i: (0, i, 0)), data, data,
                  pl.BlockSpec((tm, LANES), lambda j, b, i: (i, 0)), bank, bank],
        out_specs=(out, out), name="hy_spectrum",
        compiler_params=_cp(("parallel", "parallel", "parallel")))(fwd, ze, zo, tw, ka, kb)


def _hyena_out_body(*refs, segs):
    ns = len(segs)
    x0_ref, db_ref, o_ref = refs[3 * ns:3 * ns + 3]
    ys_refs = refs[3 * ns + 3:]
    half = CONV_CH // 2
    for si, (r0, n) in enumerate(segs):
        ye_ref, yo_ref, zin_ref = refs[3 * si:3 * si + 3]

        def body(i, carry, r0=r0, ye_ref=ye_ref, yo_ref=yo_ref, zin_ref=zin_ref):
            t0 = pl.multiple_of(i * CONV_CH, CONV_CH)
            hrows = pl.ds(pl.multiple_of(i * half, half), half)
            for g, ys_ref in enumerate(ys_refs):
                lanes = slice(g * LANES, (g + 1) * LANES)
                ys_ref[pl.ds(r0 + t0, half, stride=2), :] = ye_ref[hrows, lanes].astype(F32)
                ys_ref[pl.ds(r0 + t0 + 1, half, stride=2), :] = yo_ref[hrows, lanes].astype(F32)
            rows = pl.ds(r0 + t0, CONV_CH)
            y = jnp.concatenate([ys_ref[rows, :] for ys_ref in ys_refs], axis=1)
            zin = zin_ref[pl.ds(t0, CONV_CH), :].astype(F32)
            o_ref[rows, :] = (x0_ref[rows, :].astype(F32) * (y + zin * db_ref[...])).astype(o_ref.dtype)
            return carry

        lax.fori_loop(0, n // CONV_CH, body, 0, unroll=min(4, n // CONV_CH))


def _hyena_out(parts, x0, dbias, *, segs, tc=256):
    nb, t, c = x0.shape
    in_specs, args = [], []
    for (ye, yo, zin), (_, n) in zip(parts, segs):
        in_specs += [pl.BlockSpec((None, n // 2, tc), lambda b, j: (b, 0, j)),
                     pl.BlockSpec((None, n // 2, tc), lambda b, j: (b, 0, j)),
                     pl.BlockSpec((None, n, tc), lambda b, j: (b, 0, j))]
        args += [ye, yo, zin]
    in_specs += [pl.BlockSpec((None, t, tc), lambda b, j: (b, 0, j)), pl.BlockSpec((1, tc), lambda b, j: (0, j))]
    return pl.pallas_call(
        functools.partial(_hyena_out_body, segs=segs),
        out_shape=jax.ShapeDtypeStruct((nb, t, c), BF16), grid=(nb, c // tc), in_specs=in_specs,
        out_specs=pl.BlockSpec((None, t, tc), lambda b, j: (b, 0, j)),
        scratch_shapes=[pltpu.VMEM((t, LANES), F32)] * (tc // LANES), name="hyena_out",
        compiler_params=_cp(("parallel", "parallel")))(*args, x0, dbias)


def _hyena_longconv(ze, zo, hfb, *, tn=1024):
    nb, h, c = ze.shape
    ln = 2 * h
    tm_s = min(512, ln)
    hm = tm_s // 2
    fwd, inv = _dft_matrices(h)
    fwd = _tile_interleave(fwd, 1, hm)
    inv = _tile_interleave(inv * 0.5, 2, hm)
    tw = _twiddle_table(ln, hm)
    ka, kb = _hy_kbank(fwd, hfb, tw, tm=tm_s)
    p, q = _hy_spectrum(fwd, ze, zo, tw, ka, kb, tm=tm_s, tn=tn)
    tm_i, tk_i = min(1024, h), ln
    ye = _matmul(inv, p, tm=tm_i, tn=tn, tk=tk_i, out_dtype=BF16, name="hy_idft_even")
    yo = _matmul(inv, q, tm=tm_i, tn=tn, tk=tk_i, out_dtype=BF16, name="hy_idft_odd")
    return ye, yo


def _mlstm_body(*refs, reverse, final):
    q_ref, k_ref, v_ref, gc_ref, gr_ref = refs[:5]
    if final:
        hf_ref, og_ref, nw_ref, o_ref, ct_ref, n_ref, m_ref = refs[5:]
    else:
        o_ref, ct_ref, n_ref, m_ref = refs[5:]
    nh, dk, dv = ML_HEADS, ML_QK_DIM, ML_V_DIM
    scale = dk ** -0.5

    @pl.when(pl.program_id(1) == 0)
    def _():
        ct_ref[...] = jnp.zeros_like(ct_ref)
        n_ref[...] = jnp.zeros_like(n_ref)
        m_ref[...] = jnp.zeros_like(m_ref)

    state = ([ct_ref[h] for h in range(nh)], n_ref[...], m_ref[...])
    for c in (reversed(range(SCAN_CPS)) if reverse else range(SCAN_CPS)):
        state = _mlstm_chunk(refs, slice(c * SCAN_Q, (c + 1) * SCAN_Q), state, reverse=reverse, final=final)
    for h in range(nh):
        ct_ref[h] = state[0][h]
    n_ref[...] = state[1]
    m_ref[...] = state[2]


def _mlstm_chunk(refs, rs, state, *, reverse, final):
    q_ref, k_ref, v_ref, gc_ref, gr_ref = refs[:5]
    if final:
        hf_ref, og_ref, nw_ref, o_ref = refs[5:9]
    else:
        o_ref = refs[5]
    nh, dk, dv = ML_HEADS, ML_QK_DIM, ML_V_DIM
    scale = dk ** -0.5
    cts, n_all, m_all = state
    gc, gr = gc_ref[rs, :], gr_ref[:, rs]
    li_c, lf_c = gc[:, 0:nh], _log_sigmoid(gc[:, nh:2 * nh])
    li_r, lf_r = gr[0:nh, :], _log_sigmoid(gr[nh:2 * nh, :])
    mask, maskf, mask_tf = _scan_masks(reverse)
    bc_c = _dot(maskf, lf_c, HI)
    bc_r = _dot(lf_r, mask_tf, HI)
    btot_c = jnp.sum(lf_c, axis=0, keepdims=True)
    heads = range(nh)
    qs = [q_ref[rs, dk * h:dk * (h + 1)] for h in heads]
    ks = [k_ref[rs, dk * h:dk * (h + 1)] for h in heads]
    vs = [v_ref[rs, dv * h:dv * (h + 1)] for h in heads]
    m_s = [m_all[h:h + 1, 0:1] for h in heads]
    n_s = [n_all[h:h + 1, :] for h in heads]
    qk = [_dot_nt(qs[h], ks[h]) for h in heads]
    qc = [_dot(qs[h], cts[h].astype(BF16)) for h in heads]
    dmat = [jnp.where(mask, bc_c[:, h:h + 1] - bc_r[h:h + 1, :] + li_r[h:h + 1, :], -jnp.inf) for h in heads]
    inter = [bc_c[:, h:h + 1] + m_s[h] for h in heads]
    m_t = [jnp.maximum(inter[h], jnp.max(dmat[h], axis=1, keepdims=True)) for h in heads]
    s = [qk[h] * scale * jnp.exp(dmat[h] - m_t[h]) for h in heads]
    dec = [jnp.exp(inter[h] - m_t[h]) for h in heads]
    num = [_dot(s[h].astype(BF16), vs[h]) + (dec[h] * scale) * qc[h] for h in heads]
    qn = [jnp.sum(qs[h].astype(F32) * n_s[h], axis=1, keepdims=True) * scale for h in heads]
    den = [jnp.sum(s[h], axis=1, keepdims=True) + dec[h] * qn[h] for h in heads]
    hh = [num[h] / jnp.maximum(jnp.abs(den[h]), jnp.exp(-m_t[h])) for h in heads]

    wst_c = [btot_c[:, h:h + 1] - bc_c[:, h:h + 1] + li_c[:, h:h + 1] for h in heads]
    wst_r = [btot_c[:, h:h + 1] - bc_r[h:h + 1, :] + li_r[h:h + 1, :] for h in heads]
    m_new = [jnp.maximum(btot_c[:, h:h + 1] + m_s[h], jnp.max(wst_r[h], axis=1, keepdims=True)) for h in heads]
    dstate = [jnp.exp(btot_c[:, h:h + 1] + m_s[h] - m_new[h]) for h in heads]
    kw = [ks[h].astype(F32) * jnp.exp(wst_c[h] - m_new[h]) for h in heads]
    upd = [_dot(kw[h].T.astype(BF16), vs[h]) for h in heads]
    n_rows = [dstate[h] * n_s[h] + jnp.sum(kw[h], axis=0, keepdims=True) for h in heads]
    m_rows = [jnp.broadcast_to(m_new[h], (1, LANES)) for h in heads]

    for h in heads:
        cols = slice(dv * h, dv * (h + 1))
        if final:
            hs = hh[h] + hf_ref[rs, cols]
            hn = hs * lax.rsqrt(jnp.mean(hs * hs, axis=-1, keepdims=True) + EPS) * nw_ref[:, cols]
            o_ref[rs, cols] = (hn * _sigmoid(og_ref[rs, cols].astype(F32))).astype(o_ref.dtype)
        else:
            o_ref[rs, cols] = hh[h]

    return ([dstate[h] * cts[h] + upd[h] for h in heads], jnp.concatenate(n_rows, axis=0),
            jnp.concatenate(m_rows, axis=0))


def _mlstm(qk, pr, g_c, g_r, *, direction, n_lat, final_args=None):
    nb, t, _ = qk.shape
    q = SCAN_CPS * SCAN_Q
    assert n_lat % q == 0 and t % q == 0
    nch = t // q
    reverse = direction == 1
    cmap = _chunk_map(reverse, nch, n_lat // q)
    d = direction
    in_specs = [
        pl.BlockSpec((None, q, ML_QK_WIDTH), lambda b, k: (b, cmap(k), 0)),
        pl.BlockSpec((None, q, ML_QK_WIDTH), lambda b, k: (b, cmap(k), 1)),
        pl.BlockSpec((None, q, ML_WIDTH), lambda b, k: (b, cmap(k), 2 * ML_QK_WIDTH // ML_WIDTH)),
        pl.BlockSpec((None, None, q, 2 * ML_HEADS), lambda b, k: (b, d, cmap(k), 0)),
        pl.BlockSpec((None, None, 2 * ML_HEADS, q), lambda b, k: (b, d, 0, cmap(k))),
    ]
    args = [qk, qk, pr, g_c, g_r]
    final = final_args is not None
    if final:
        hf, norm_w = final_args
        in_specs += [pl.BlockSpec((None, q, ML_WIDTH), lambda b, k: (b, cmap(k), 0)),
                     pl.BlockSpec((None, q, ML_WIDTH), lambda b, k: (b, cmap(k), 2 * ML_QK_WIDTH // ML_WIDTH + 1)),
                     pl.BlockSpec((1, ML_WIDTH), lambda b, k: (0, 0))]
        args += [hf, pr, norm_w]
    out_shape = jax.ShapeDtypeStruct((nb, t, ML_WIDTH), BF16 if final else F32)
    return pl.pallas_call(
        functools.partial(_mlstm_body, reverse=reverse, final=final),
        out_shape=out_shape, grid=(nb, nch), in_specs=in_specs,
        out_specs=pl.BlockSpec((None, q, ML_WIDTH), lambda b, k: (b, cmap(k), 0)),
        scratch_shapes=[pltpu.VMEM((ML_HEADS, ML_QK_DIM, ML_V_DIM), F32), pltpu.VMEM((ML_HEADS, ML_QK_DIM), F32),
                        pltpu.VMEM((ML_HEADS, LANES), F32)],
        name="mlstm_bwd" if reverse else "mlstm_fwd",
        compiler_params=_cp(("parallel", "arbitrary")))(*args)


def _na_window_start(j, rows):
    return jnp.clip(j * NA_QROWS - NA_ROWS // 2, 0, rows - NA_KROWS)


def _na_part_offset(j, part, start, clip=jnp.clip):
    first = j * NA_QROWS + part * (NA_QROWS // NA_PARTS) - NA_ROWS // 2
    return clip(first - start, 0, NA_KROWS - NA_PART_KROWS) // NA_TAB_ROWS * NA_TAB_ROWS


def _na_body(q_ref, k_ref, v_ref, tab_ref, qw_ref, kw_ref, o_ref, kn_ref, v1_ref, *, n_lat, n_ctx):
    dh = NA_HEAD_DIM
    scale = dh ** -0.5
    rows = n_lat // GRID_W
    j = pl.program_id(2)
    heads = range(NA_HPS)
    hcols = [slice(hh * dh, (hh + 1) * dh) for hh in heads]

    @pl.when(j == 0)
    def _():
        prep = (n_lat + n_ctx) // NA_PREP_STEPS
        one_hot = (lax.broadcasted_iota(jnp.int32, (prep, LANES), 1) == 0).astype(v1_ref.dtype)

        def body(i, carry):
            r = pl.ds(pl.multiple_of(i * prep, 16), prep)
            for hh in heads:
                kk = k_ref[r, hcols[hh]].astype(F32)
                kn = kk * lax.rsqrt(jnp.mean(kk * kk, axis=-1, keepdims=True) + EPS) * kw_ref[...]
                kn_ref[r, hcols[hh]] = kn.astype(kn_ref.dtype)
                v1_ref[hh, r, 0:dh] = v_ref[r, hcols[hh]]
                v1_ref[hh, r, dh:dh + LANES] = one_hot
            return carry

        lax.fori_loop(0, NA_PREP_STEPS, body, 0)

    start = _na_window_start(j, rows)
    ctx = pl.ds(n_lat, n_ctx)
    rpp = NA_QROWS // NA_PARTS
    pq = rpp * GRID_W
    chains = [(hh, i) for hh in heads for i in range(NA_PARTS)]
    off = [_na_part_offset(j, i, start) for i in range(NA_PARTS)]
    win = [pl.ds(pl.multiple_of((start + off[i]) * GRID_W, GRID_W), NA_PART_KROWS * GRID_W) for i in range(NA_PARTS)]
    qq = [q_ref[i * pq:(i + 1) * pq, hcols[hh]].astype(F32) for hh, i in chains]
    qn = [(x * lax.rsqrt(jnp.mean(x * x, axis=-1, keepdims=True) + EPS) * (qw_ref[...] * scale)).astype(BF16)
          for x in qq]
    sw = [_dot_nt(qn[c], kn_ref[win[i], hcols[hh]]) for c, (hh, i) in enumerate(chains)]
    sc = [_dot_nt(qn[c], kn_ref[ctx, hcols[hh]]) for c, (hh, i) in enumerate(chains)]

    def with_bias(c, hh, i):
        out = []
        for r in range(rpp):
            d0 = start - (j * NA_QROWS + i * rpp + r) + NA_ROWS - 1 + NA_QROWS
            blocks = tab_ref[hh, d0, pl.ds(off[i] // NA_TAB_ROWS, NA_PART_KROWS // NA_TAB_ROWS)]
            bias = jnp.concatenate([blocks[b] for b in range(NA_PART_KROWS // NA_TAB_ROWS)], axis=1)
            out.append(sw[c][r * GRID_W:(r + 1) * GRID_W] + bias)
        return jnp.concatenate(out, axis=0)

    sb = [with_bias(c, hh, i) for c, (hh, i) in enumerate(chains)]
    m = [jnp.maximum(jnp.max(a, axis=1, keepdims=True), jnp.max(b, axis=1, keepdims=True)) for a, b in zip(sb, sc)]
    pw = [jnp.exp(a - mm) for a, mm in zip(sb, m)]
    pc = [jnp.exp(a - mm) for a, mm in zip(sc, m)]
    o = [_dot(pw[c].astype(BF16), v1_ref[hh, win[i], :]) + _dot(pc[c].astype(BF16), v1_ref[hh, ctx, :])
         for c, (hh, i) in enumerate(chains)]
    for c, (hh, i) in enumerate(chains):
        o_ref[i * pq:(i + 1) * pq, hcols[hh]] = (o[c][:, 0:dh] / o[c][:, dh:dh + 1]).astype(o_ref.dtype)


def _na_bias_table(rpb, rows):
    nd = 2 * NA_QROWS
    valid = np.zeros((nd, NA_KROWS), bool)
    seen = np.zeros((nd,), bool)
    for j in range(rows // NA_QROWS):
        start = int(np.clip(j * NA_QROWS - NA_ROWS // 2, 0, rows - NA_KROWS))
        for qr in range(NA_QROWS):
            r = j * NA_QROWS + qr
            rs = int(np.clip(r - NA_ROWS // 2, 0, rows - NA_ROWS))
            d = start - r + NA_ROWS - 1 + NA_QROWS
            v = np.array([(rs <= start + i < rs + NA_ROWS) for i in range(NA_KROWS)])
            assert 0 <= d < nd and (not seen[d] or (valid[d] == v).all())
            valid[d], seen[d] = v, True
            off = int(_na_part_offset(j, qr // (NA_QROWS // NA_PARTS), start, clip=np.clip))
            assert off % NA_TAB_ROWS == 0 and not v[:off].any() and not v[off + NA_PART_KROWS:].any()
    col = np.arange(GRID_W)
    cs = np.clip(col - NA_COLS // 2, 0, GRID_W - NA_COLS)
    in_win = (col[None, :] >= cs[:, None]) & (col[None, :] < cs[:, None] + NA_COLS)
    ndr, ndc, w = 2 * NA_ROWS - 1, 2 * NA_COLS - 1, GRID_W
    dr = (np.arange(nd)[:, None] - NA_QROWS) + np.arange(NA_KROWS)[None, :]
    ok = valid[:, None, :, None] & in_win[None, :, None, :] & (dr >= 0)[:, None, :, None] \
        & (dr < ndr)[:, None, :, None]
    ext = jnp.pad(rpb, ((0, 0), (0, 0), (w, w)), mode="edge")
    c0 = w + NA_COLS - 1
    t1 = jnp.stack([ext[:, :, c0 - qc:c0 - qc + w] for qc in range(w)], axis=2)
    t1p = jnp.pad(t1, ((0, 0), (NA_QROWS, nd + NA_KROWS - 1 - NA_QROWS - ndr), (0, 0), (0, 0)))
    wide = jnp.concatenate([t1p[:, m] for m in range(nd + NA_KROWS - 1)], axis=-1)
    wpad = LANES * pl.cdiv(wide.shape[-1], LANES)
    wide = jnp.pad(wide, ((0, 0), (0, 0), (0, wpad - wide.shape[-1])))
    nblk, bw = NA_KROWS // NA_TAB_ROWS, NA_TAB_ROWS * w
    okf = jnp.asarray(ok.reshape(nd, w, nblk, bw).transpose(0, 2, 1, 3).astype(np.float32))

    def body(wide_ref, ok_ref, o_ref):
        for d in range(nd):
            for b in range(nblk):
                lo = d * w + b * bw
                o_ref[d, b] = jnp.where(ok_ref[d, b] != 0.0, wide_ref[:, lo:lo + bw], -jnp.inf)

    return pl.pallas_call(
        body, out_shape=jax.ShapeDtypeStruct((rpb.shape[0], nd, nblk, w, bw), F32), grid=(rpb.shape[0],),
        in_specs=[pl.BlockSpec((None, w, wpad), lambda h: (h, 0, 0)),
                  pl.BlockSpec((nd, nblk, w, bw), lambda h: (0, 0, 0, 0))],
        out_specs=pl.BlockSpec((None, nd, nblk, w, bw), lambda h: (h, 0, 0, 0, 0)),
        name="na_bias_table", compiler_params=_cp(("parallel",)))(wide, okf)


def _na(pr, tab, q_w, k_w, *, n_lat, n_ctx, qcol):
    nb, t, _ = pr.shape
    dh, nh, hps = NA_HEAD_DIM, NA_HEADS, NA_HPS
    nq = NA_QROWS * GRID_W
    hw = hps * dh
    assert qcol % hw == 0 and nh % hps == 0
    qb = qcol // hw
    return pl.pallas_call(
        functools.partial(_na_body, n_lat=n_lat, n_ctx=n_ctx),
        out_shape=jax.ShapeDtypeStruct((nb, n_lat, NA_WIDTH), BF16), grid=(nb, nh // hps, n_lat // nq),
        in_specs=[pl.BlockSpec((None, nq, hw), lambda b, h, j: (b, j, qb + h)),
                  pl.BlockSpec((None, t, hw), lambda b, h, j: (b, 0, qb + nh // hps + h)),
                  pl.BlockSpec((None, t, hw), lambda b, h, j: (b, 0, qb + 2 * (nh // hps) + h)),
                  pl.BlockSpec((hps,) + tab.shape[1:], lambda b, h, j: (h, 0, 0, 0, 0)),
                  pl.BlockSpec((1, dh), lambda b, h, j: (0, 0)),
                  pl.BlockSpec((1, dh), lambda b, h, j: (0, 0))],
        out_specs=pl.BlockSpec((None, nq, hw), lambda b, h, j: (b, j, h)),
        scratch_shapes=[pltpu.VMEM((t, hw), BF16), pltpu.VMEM((hps, t, dh + LANES), BF16)], name="na",
        compiler_params=_cp(("parallel", "parallel", "arbitrary")))(pr, pr, pr, tab, q_w, k_w)


def _rope_tables(n_lat, n_ctx, width):
    dh = ML_QK_DIM
    nf = dh // 4
    t = jnp.arange(n_lat)
    inv = ROPE_THETA ** (-jnp.arange(nf, dtype=F32) / nf)
    lane = np.arange(dh)
    pos = jnp.where((lane < dh // 2)[None, :], (t // GRID_W)[:, None], (t % GRID_W)[:, None]).astype(F32)
    ang = pos * inv[lane % nf][None, :]
    sign = np.where((lane % (dh // 2)) < nf, -1.0, 1.0).astype(np.float32)
    cos = jnp.concatenate([jnp.cos(ang), jnp.ones((n_ctx, dh), F32)], axis=0)
    sin = jnp.concatenate([jnp.sin(ang) * sign[None, :], jnp.zeros((n_ctx, dh), F32)], axis=0)
    reps = width // dh
    return jnp.tile(cos, (1, reps)), jnp.tile(sin, (1, reps))


def _ada_mod(c, c_ctx, w_all, b_all, layer):
    nb, d = c.shape
    rows = 16
    cv = jnp.concatenate([c, c_ctx[None, :], jnp.zeros((rows - nb - 1, d), F32)], axis=0)
    a = (cv * _sigmoid(cv)).astype(BF16)[None]
    out = _matmul(a, w_all, w_index=layer, tm=rows, tn=1024, tk=d, out_dtype=F32, name="ada_mod")[0]
    out = out[:nb + 1] + b_all[layer][None, :]
    return out.reshape(nb + 1, 6, d).transpose(1, 0, 2)


def _lat_ctx(m, nb):
    ctx = jnp.broadcast_to(m[nb][None, :], (nb, m.shape[1]))
    return jnp.stack([m[:nb], ctx], axis=1)[:, :, None, :]


def _pad_cols(w, n):
    return jnp.pad(w, ((0, 0), (0, n - w.shape[1])))


def _col_row_forms(g, groups):
    nb, t, _ = g.shape
    g5 = g.reshape(nb, t, 2, groups, -1)
    return g5.transpose(0, 2, 3, 1, 4), g5.transpose(0, 2, 3, 4, 1)


ROW_TILES = (1088, 1024, 512, 256)


def _row_tile(rows):
    return next(tm for tm in ROW_TILES if rows % tm == 0)


def _mod_pair(mods, shift, scale, nb):
    return jnp.stack([_lat_ctx(mods[shift], nb), _lat_ctx(mods[scale], nb)], axis=1)


def _in_projection(u, w_all, layer, *, rows, name, cols=None, w_nk=False):
    tm = _row_tile(rows)
    tm = 2 * tm if rows % (2 * tm) == 0 else tm
    return _matmul(u, w_all, w_index=layer, w_nk=w_nk, cols=cols, tm=tm, tn=1024, tk=u.shape[2], out_dtype=BF16,
                   rows=rows, vmem=VMEM_LIMIT_BIG, name=name)


def _out_projection(a_list, w_all, x, gate, layer, *, rows, n_lat, next_norm, name):
    tm = _row_tile(rows)
    tn = 512 if w_all.shape[1] <= 2 * D_MODEL else 256
    x = _matmul_gated_residual_rows(a_list, w_all, x, gate, w_index=layer, tm=tm, tn=tn, rows=rows, n_lat=n_lat,
                                    name=name)
    if next_norm is None:
        return x, None
    return x, _rmsmod(x, *next_norm, tm=tm, rows=rows, n_lat=n_lat)


def _conv_ffn_block(x, u, mods, w_up_all, conv_w, conv_b, w_down_all, layer, *, rows, n_lat, segs, next_norm):
    nb = x.shape[0]
    up = _in_projection(u, w_up_all, layer, rows=rows, name="ffn_up")
    hid = _conv_ffn(up, conv_w, conv_b[None, :], segs=segs, rows=rows)
    return _out_projection((hid,), w_down_all, x, _lat_ctx(mods[5], nb), layer, rows=rows, n_lat=n_lat,
                           next_norm=next_norm, name="ffn_down")


def _even_layer(x, u, mods, w_in_all, conv_w, conv_b, dt_bias, a_log, d_skip, ssd_norm_w,
                hy_w1, hy_b1, hy_w2, hy_b2, hy_w3, hy_freq, hy_bias, w_out_all, j, *, n_lat, n_ctx, next_norm):
    nb, t, d = x.shape
    segs = ((0, n_lat), (n_lat, n_ctx))
    tm = _row_tile(t)
    n_main = SSD_WIDTH + EV_CONV_CH
    pr = _in_projection(u, jnp.swapaxes(w_in_all, 1, 2), j, rows=t, cols=n_main, w_nk=True, name="ev_in")
    w_dt = _pad_cols(w_in_all[j, :, n_main:], LANES).astype(BF16)
    dtp = _matmul(u, w_dt[None], tm=tm, tn=LANES, tk=d, out_dtype=F32, name="ev_in_dt")[:, :, :2 * SSD_HEADS]
    cb = conv_b[None, :]
    xbc = _conv_silu(pr, conv_w, cb, xcol=SSD_WIDTH, wcol=0, width=SSD_XBC, segs=segs)
    x0, zin_l, zin_c, zle, zlo, zce, zco = _conv_hyena(pr, conv_w, cb, xcol=SSD_WIDTH + SSD_XBC, wcol=SSD_XBC,
                                                       segs=segs)

    dt_c, dt_r = _col_row_forms(dtp, SSD_GROUPS)
    db = dt_bias.reshape(2, SSD_GROUPS, 1, SSD_HPG)
    al = a_log.reshape(2, SSD_GROUPS, 1, SSD_HPG)
    scan_args = (xbc, dt_c, dt_r, db, db.transpose(0, 1, 3, 2), al, al.transpose(0, 1, 3, 2))
    yf = _ssd(*scan_args, direction=0, n_lat=n_lat)
    dsk = jnp.repeat(d_skip, SSD_HEAD_DIM)[None, :]
    y_ssd = _ssd(*scan_args, direction=1, n_lat=n_lat, final_args=(yf, pr, dsk, ssd_norm_w[None, :]))

    hyb = hy_bias[None, :]
    filt = functools.partial(_hyena_filters, w1=hy_w1, b1=hy_b1, w2=hy_w2, b2=hy_b2, w3=hy_w3, freq=hy_freq)
    lat = _hyena_longconv(zle, zlo, filt(n_lat)) + (zin_l,)
    ctx = _hyena_longconv(zce, zco, filt(n_ctx)) + (zin_c,)
    y_hy = _hyena_out((lat, ctx), x0, hyb, segs=segs)
    return _out_projection((y_ssd, y_hy), w_out_all, x, _lat_ctx(mods[2], nb), j, rows=t, n_lat=n_lat,
                           next_norm=next_norm, name="ev_out")


def _odd_layer_latent(x, u, mods, w_in_all, conv_w, conv_b, gate_b, ml_norm_w, q_norm_w, k_norm_w, rpb,
                      w_out_all, j, *, n_lat, n_ctx, next_norm):
    nb, t, d = x.shape
    segs = ((0, n_lat), (n_lat, n_ctx))
    tm = _row_tile(t)
    o3 = 2 * ML_QK_WIDTH + 2 * ML_WIDTH
    o4 = o3 + 4 * ML_HEADS
    w_nk = jnp.swapaxes(w_in_all, 1, 2)
    pr = _in_projection(u, w_nk, j, rows=t, cols=o3, w_nk=True, name="od_in_mlstm")
    pr_na = _in_projection(u, w_nk[j:j + 1, o4:, :], 0, rows=t, w_nk=True, name="od_in_na")
    gates = _matmul(u, _pad_cols(w_in_all[j, :, o3:o4], LANES).astype(BF16)[None], tm=tm, tn=LANES, tk=d,
                    out_dtype=F32, name="od_in_gates")[:, :, :4 * ML_HEADS] + gate_b.reshape(-1)
    cos, sin = _rope_tables(n_lat, n_ctx, 256)
    qk = _conv_rope(pr, conv_w, conv_b[None, :], cos, sin, width=2 * ML_QK_WIDTH, segs=segs)
    g5 = gates.reshape(nb, t, 2, 2 * ML_HEADS)
    g_c, g_r = g5.transpose(0, 2, 1, 3), g5.transpose(0, 2, 3, 1)
    hf = _mlstm(qk, pr, g_c, g_r, direction=0, n_lat=n_lat)
    y_ml = _mlstm(qk, pr, g_c, g_r, direction=1, n_lat=n_lat, final_args=(hf, ml_norm_w[None, :]))
    tab = _na_bias_table(rpb, n_lat // GRID_W)
    y_na = _na(pr_na, tab, q_norm_w[None, :], k_norm_w[None, :], n_lat=n_lat, n_ctx=n_ctx, qcol=0)
    return _out_projection((y_ml, y_na), w_out_all, x, _lat_ctx(mods[2], nb), j, rows=n_lat, n_lat=n_lat,
                           next_norm=next_norm, name="od_out")


def kernel(x, c, ctx, c_ctx, ada_w, ada_b, norm_w, ev_w_in, ev_conv_w, ev_conv_b, ssd_dt_bias, ssd_a_log, ssd_d,
           ssd_norm_w, hy_w1, hy_b1, hy_w2, hy_b2, hy_w3, hy_freq, hy_bias, ev_w_out, od_w_in, ml_conv_w, ml_conv_b,
           ml_gate_b, ml_norm_w, na_q_norm_w, na_k_norm_w, na_rpb, od_w_out, ffn_w_up, ffn_conv_w, ffn_conv_b,
           ffn_w_down):
    depth = ada_w.shape[0]
    assert depth == 2, "layer schedule below is written for one even and one odd layer"
    n_lat, n_ctx = x.shape[1], ctx.shape[1]
    t = n_lat + n_ctx
    both = ((0, n_lat), (n_lat, n_ctx))
    xs = jnp.concatenate([x, ctx], axis=1)

    nb = x.shape[0]
    w_down = ffn_w_down.astype(BF16)
    mods0 = _ada_mod(c, c_ctx, ada_w, ada_b, 0)
    mods1 = _ada_mod(c, c_ctx, ada_w, ada_b, 1)
    norm = lambda layer, sub, mods: (norm_w[layer, sub][None, :], _mod_pair(mods, 3 * sub, 3 * sub + 1, nb))

    w0, mod0 = norm(0, 0, mods0)
    u = _rmsmod(xs, w0, mod0, tm=_row_tile(t), rows=t, n_lat=n_lat)
    xs, u = _even_layer(xs, u, mods0, ev_w_in, ev_conv_w[0], ev_conv_b[0], ssd_dt_bias[0], ssd_a_log[0],
                        ssd_d[0], ssd_norm_w[0], hy_w1[0], hy_b1[0], hy_w2[0], hy_b2[0], hy_w3[0], hy_freq[0],
                        hy_bias[0], ev_w_out.astype(BF16), 0, n_lat=n_lat, n_ctx=n_ctx, next_norm=norm(0, 1, mods0))
    xs, u = _conv_ffn_block(xs, u, mods0, ffn_w_up, ffn_conv_w[0], ffn_conv_b[0], w_down, 0,
                            rows=t, n_lat=n_lat, segs=both, next_norm=norm(1, 0, mods1))
    xl, u = _odd_layer_latent(xs, u, mods1, od_w_in, ml_conv_w[0], ml_conv_b[0], ml_gate_b[0], ml_norm_w[0],
                              na_q_norm_w[0], na_k_norm_w[0], na_rpb[0], od_w_out.astype(BF16), 0,
                              n_lat=n_lat, n_ctx=n_ctx, next_norm=norm(1, 1, mods1))
    out, _ = _conv_ffn_block(xl, u, mods1, ffn_w_up, ffn_conv_w[1], ffn_conv_b[1], w_down, 1,
                             rows=n_lat, n_lat=n_lat, segs=((0, n_lat),), next_norm=None)
    return out
```

```python
import functools
import math

import numpy as np
import jax
import jax.numpy as jnp
from jax import lax
from jax.experimental import pallas as pl
from jax.experimental.pallas import tpu as pltpu

F32 = jnp.float32
BF16 = jnp.bfloat16
HI = lax.Precision.HIGHEST

D_MODEL = 2048
GRID_W = 64
EPS = 1e-6
CONV_W = 3
SSD_WIDTH = D_MODEL
SSD_HEAD_DIM = 64
SSD_HEADS = SSD_WIDTH // SSD_HEAD_DIM
SSD_STATE = 128
SSD_GROUPS = 4
SSD_HPG = SSD_HEADS // SSD_GROUPS
SSD_GW = SSD_WIDTH // SSD_GROUPS
SSD_XBC = SSD_WIDTH + 2 * SSD_GROUPS * SSD_STATE
HY_WIDTH = D_MODEL
HY_EMB = 33
HY_BANDS = (HY_EMB - 1) // 2
HY_ORDER = 64
HY_FAST = 0.3
HY_SLOW = 1.5
HY_TARGET = 1e-2
ML_HEADS = 8
ML_WIDTH = D_MODEL
ML_V_DIM = ML_WIDTH // ML_HEADS
ML_QK_DIM = ML_V_DIM // 2
ML_QK_WIDTH = ML_HEADS * ML_QK_DIM
ROPE_THETA = 10000.0
NA_WIDTH = D_MODEL
NA_HEAD_DIM = 128
NA_HEADS = NA_WIDTH // NA_HEAD_DIM
NA_ROWS = 8
NA_COLS = 16
NA_QROWS = 8
NA_KROWS = 16
NA_PARTS = 2
NA_HPS = 2
NA_PREP_STEPS = 8
NA_TAB_ROWS = 4
NA_PART_KROWS = 12
EV_CONV_CH = SSD_XBC + 3 * HY_WIDTH
SCAN_Q = 128
SCAN_CPS = 2
LANES = 128
SUBLANES = 8
MIB = 1024 * 1024
V7X_VMEM_BYTES = 64 * MIB
VMEM_LIMIT = 48 * MIB
VMEM_LIMIT_BIG = V7X_VMEM_BYTES - 6 * MIB


def _cp(sem, vmem=VMEM_LIMIT):
    return pltpu.CompilerParams(dimension_semantics=sem, vmem_limit_bytes=vmem)


def _sigmoid(x):
    return 0.5 * jnp.tanh(0.5 * x) + 0.5


def _silu(x):
    return x * _sigmoid(x)


def _softplus(x):
    return jnp.maximum(x, 0.0) + jnp.log(1.0 + jnp.exp(-jnp.abs(x)))


def _log_sigmoid(x):
    return -_softplus(-x)


def _dot(a, b, precision=None):
    return jnp.dot(a, b, preferred_element_type=F32, precision=precision)


def _dot_nt(a, b):
    return lax.dot_general(a, b, (((1,), (1,)), ((), ())), preferred_element_type=F32)


def _split_bf16(x, terms):
    out = []
    for _ in range(terms):
        part = x.astype(BF16)
        out.append(part)
        x = x - part.astype(F32)
    return out


def _dot_sel(x, sel, terms):
    selb = sel.astype(BF16)
    return sum(_dot(part, selb) for part in _split_bf16(x, terms))


def _sel_dot(sel, x, terms):
    selb = sel.astype(BF16)
    return sum(_dot(selb, part) for part in _split_bf16(x, terms))


def _mm_body(a_ref, w_ref, o_ref, *scratch, nk, cache_w, w_nk):
    if cache_w:
        wb_ref = scratch[0]

        @pl.when((pl.program_id(1) == 0) & (pl.program_id(2) == 0))
        def _():
            wb_ref[...] = (w_ref[...].T if w_nk else w_ref[...]).astype(BF16)

        w_val = wb_ref[...]
    else:
        w_val = w_ref[...].astype(BF16)
    part = _dot(a_ref[...].astype(BF16), w_val)
    if nk == 1:
        o_ref[...] = part.astype(o_ref.dtype)
        return
    acc_ref = scratch[0]
    k = pl.program_id(3)

    @pl.when(k == 0)
    def _():
        acc_ref[...] = part

    @pl.when(k > 0)
    def _():
        acc_ref[...] += part

    @pl.when(k == nk - 1)
    def _():
        o_ref[...] = acc_ref[...].astype(o_ref.dtype)


def _matmul(a, w, *, tm, tn, tk, out_dtype, rows=None, cols=None, w_index=None, w_nk=False, vmem=VMEM_LIMIT,
            name="mm"):
    ba, ta, kdim = a.shape
    bw, kdim2, n = (w.shape[0], w.shape[2], w.shape[1]) if w_nk else w.shape
    assert kdim == kdim2
    nb = ba if w_index is not None else max(ba, bw)
    rows = ta if rows is None else rows
    n = n if cols is None else cols
    assert rows % tm == 0 and n % tn == 0 and kdim % tk == 0
    ni, nj, nk = rows // tm, n // tn, kdim // tk
    a_map = (lambda j, b, i, k: (b, i, k)) if ba > 1 else (lambda j, b, i, k: (0, i, k))
    if w_index is not None:
        w_map = lambda j, b, i, k: (w_index, k, j)
    else:
        w_map = (lambda j, b, i, k: (b, k, j)) if bw > 1 else (lambda j, b, i, k: (0, k, j))
    cache_w = w.dtype != BF16 and nk == 1 and (w_index is not None or bw == 1)
    assert cache_w or not w_nk
    scratch = [pltpu.VMEM((tm, tn), F32)] if nk > 1 else []
    if cache_w:
        scratch = [pltpu.VMEM((tk, tn), BF16)]
    inner = "arbitrary" if cache_w else "parallel"
    if w_nk:
        w_spec = pl.BlockSpec((None, tn, tk), lambda j, b, i, k: (w_map(j, b, i, k)[0], j, k))
    else:
        w_spec = pl.BlockSpec((None, tk, tn), w_map)
    return pl.pallas_call(
        functools.partial(_mm_body, nk=nk, cache_w=cache_w, w_nk=w_nk),
        out_shape=jax.ShapeDtypeStruct((nb, rows, n), out_dtype), grid=(nj, nb, ni, nk),
        in_specs=[pl.BlockSpec((None, tm, tk), a_map), w_spec],
        out_specs=pl.BlockSpec((None, tm, tn), lambda j, b, i, k: (b, i, j)),
        scratch_shapes=scratch, name=name,
        compiler_params=_cp(("parallel", inner, inner, "arbitrary"), vmem))(a, w)


def _row_is_lat(row_blk, tm, n_lat):
    rows = row_blk * tm + lax.broadcasted_iota(jnp.int32, (tm, 1), 0)
    return rows < n_lat


def _gated_residual_epilogue(acc, row_blk, res_ref, gate_ref, *, tm, n_lat):
    gate = jnp.where(_row_is_lat(row_blk, tm, n_lat), gate_ref[0], gate_ref[1])
    return res_ref[...] + gate * acc


def _gated_residual_rows_body(*refs, ks, tm, n_lat):
    n_a = len(ks)
    a_refs, w_ref, res_ref, gate_ref, o_ref = refs[:n_a], refs[n_a], refs[n_a + 1], refs[n_a + 2], refs[n_a + 3]
    acc, lo = None, 0
    for a_ref, kk in zip(a_refs, ks):
        part = _dot(a_ref[...], w_ref[lo:lo + kk, :])
        acc = part if acc is None else acc + part
        lo += kk
    o_ref[...] = _gated_residual_epilogue(acc, pl.program_id(1), res_ref, gate_ref, tm=tm, n_lat=n_lat)


def _matmul_gated_residual_rows(a_list, w, res, gate, *, w_index, tm, tn, rows, n_lat, name):
    nb = a_list[0].shape[0]
    _, kdim, n = w.shape
    ks = tuple(x.shape[2] for x in a_list)
    assert sum(ks) == kdim and rows % tm == 0 and n % tn == 0
    in_specs = [pl.BlockSpec((None, tm, kk), lambda b, i, j: (b, i, 0)) for kk in ks]
    in_specs += [pl.BlockSpec((None, kdim, tn), lambda b, i, j: (w_index, 0, j)),
                 pl.BlockSpec((None, tm, tn), lambda b, i, j: (b, i, j)),
                 pl.BlockSpec((None, 2, 1, tn), lambda b, i, j: (b, 0, 0, j))]
    return pl.pallas_call(
        functools.partial(_gated_residual_rows_body, ks=ks, tm=tm, n_lat=n_lat),
        out_shape=jax.ShapeDtypeStruct((nb, rows, n), F32), grid=(nb, rows // tm, n // tn), in_specs=in_specs,
        out_specs=pl.BlockSpec((None, tm, tn), lambda b, i, j: (b, i, j)), name=name,
        compiler_params=_cp(("parallel", "parallel", "parallel"), VMEM_LIMIT_BIG))(*a_list, w, res, gate)


def _modulated_rmsnorm(x, is_lat, w_ref, mod_ref):
    y = x * lax.rsqrt(jnp.mean(x * x, axis=-1, keepdims=True) + EPS) * w_ref[...]
    shift = jnp.where(is_lat, mod_ref[0, 0], mod_ref[0, 1])
    scale = jnp.where(is_lat, mod_ref[1, 0], mod_ref[1, 1])
    return y * (1.0 + scale) + shift


def _rmsmod_body(x_ref, w_ref, mod_ref, o_ref, *, tm, n_lat):
    is_lat = _row_is_lat(pl.program_id(2), tm, n_lat)
    o_ref[...] = _modulated_rmsnorm(x_ref[...], is_lat, w_ref, mod_ref).astype(o_ref.dtype)


def _rmsmod(x, w, mod, *, tm, rows, n_lat):
    nb, _, d = x.shape
    tm = tm // 2
    assert rows % tm == 0 and tm % 16 == 0
    body = functools.partial(_rmsmod_body, tm=tm, n_lat=n_lat)
    return pl.pallas_call(
        body, out_shape=jax.ShapeDtypeStruct((nb, rows, d), BF16), grid=(1, nb, rows // tm),
        in_specs=[pl.BlockSpec((None, tm, d), lambda j, b, i: (b, i, 0)),
                  pl.BlockSpec((1, d), lambda j, b, i: (0, 0)),
                  pl.BlockSpec((None, 2, 2, 1, d), lambda j, b, i: (b, 0, 0, 0, 0))],
        out_specs=pl.BlockSpec((None, tm, d), lambda j, b, i: (b, i, 0)),
        name="rmsmod", compiler_params=_cp(("parallel", "parallel", "parallel")))(x, w, mod)


CONV_CH = 128
CONV_HALO = 16
CONV_STEPS = 4


def _shift_matrices():
    r = lax.broadcasted_iota(jnp.int32, (CONV_CH, CONV_CH + 2 * CONV_HALO), 0)
    c = lax.broadcasted_iota(jnp.int32, (CONV_CH, CONV_CH + 2 * CONV_HALO), 1)
    return (c == r + CONV_HALO - 1).astype(BF16), (c == r + CONV_HALO + 1).astype(BF16)


def _chunk_rows(r0, i):
    return pl.ds(r0 + pl.multiple_of(i * CONV_CH, CONV_CH), CONV_CH)


def _taps(w_ref, b_ref, scale=1.0):
    return w_ref[...] * scale, b_ref[...] * scale


def _silu_of_half(h):
    return h + h * jnp.tanh(h)


def _conv_at(x_ref, taps, shifts, r0, n, steps):
    w, b = taps
    cur = [x_ref[_chunk_rows(r0, i), :] for i in steps]
    zero = jnp.zeros((CONV_HALO, cur[0].shape[1]), cur[0].dtype)
    lo = [pl.multiple_of(jnp.maximum(i * CONV_CH - CONV_HALO, 0), CONV_HALO) for i in steps]
    hi = [pl.multiple_of(jnp.minimum((i + 1) * CONV_CH, n - CONV_HALO), CONV_HALO) for i in steps]
    tail = [jnp.where(i > 0, x_ref[pl.ds(r0 + l, CONV_HALO), :], zero) for i, l in zip(steps, lo)]
    head = [jnp.where(i < n // CONV_CH - 1, x_ref[pl.ds(r0 + h, CONV_HALO), :], zero) for i, h in zip(steps, hi)]
    win = [jnp.concatenate([t, c, h], axis=0) for t, c, h in zip(tail, cur, head)]
    prev = [_dot(shifts[0], w) for w in win]
    nxt = [_dot(shifts[1], w) for w in win]
    return [w[0:1, :] * p + w[1:2, :] * c.astype(F32) + w[2:3, :] * x + b for p, c, x in zip(prev, cur, nxt)]


def _conv_loop(segs, fn):
    for si, (r0, n) in enumerate(segs):
        def body(i, carry, r0=r0, n=n, si=si):
            fn(si, r0, n, [i])
            return carry

        lax.fori_loop(0, n // CONV_CH, body, 0, unroll=min(CONV_STEPS, n // CONV_CH))


def _conv_silu_body(x_ref, w_ref, b_ref, o_ref, *, segs):
    shifts = _shift_matrices()
    half_taps = _taps(w_ref, b_ref, 0.5)

    def fn(si, r0, n, steps):
        ys = [_silu_of_half(h) for h in _conv_at(x_ref, half_taps, shifts, r0, n, steps)]
        for i, y in zip(steps, ys):
            o_ref[_chunk_rows(r0, i), :] = y.astype(o_ref.dtype)

    _conv_loop(segs, fn)


def _conv_rope_body(x_ref, w_ref, b_ref, cos_ref, sin_ref, o_ref, *, segs, tc):
    shifts = _shift_matrices()
    lane = lax.broadcasted_iota(jnp.int32, (CONV_CH, tc), 1)
    first = (lane % (ML_QK_DIM // 2)) < (ML_QK_DIM // 4)
    nf = ML_QK_DIM // 4
    half_taps = _taps(w_ref, b_ref, 0.5)

    def fn(si, r0, n, steps):
        ys = [_silu_of_half(h) for h in _conv_at(x_ref, half_taps, shifts, r0, n, steps)]
        partner = [jnp.where(first, pltpu.roll(y, tc - nf, 1), pltpu.roll(y, nf, 1)) for y in ys]
        for i, y, p in zip(steps, ys, partner):
            rows = _chunk_rows(r0, i)
            o_ref[rows, :] = (y * cos_ref[rows, :] + p * sin_ref[rows, :]).astype(o_ref.dtype)

    _conv_loop(segs, fn)


def _conv_ffn_body(a_ref, x_ref, w_ref, b_ref, o_ref, *, segs):
    shifts = _shift_matrices()
    half_taps = _taps(w_ref, b_ref, 0.5)

    def fn(si, r0, n, steps):
        gs = [_silu_of_half(h) for h in _conv_at(x_ref, half_taps, shifts, r0, n, steps)]
        for i, g in zip(steps, gs):
            rows = _chunk_rows(r0, i)
            o_ref[rows, :] = (a_ref[rows, :].astype(F32) * g).astype(o_ref.dtype)

    _conv_loop(segs, fn)


def _conv_hyena_body(x0_ref, x1_ref, v_ref, w0_ref, b0_ref, w1_ref, b1_ref, w2_ref, b2_ref,
                     ox0_ref, ozl_ref, ozc_ref, ozle_ref, ozlo_ref, ozce_ref, ozco_ref, *zs_refs, segs):
    shifts = _shift_matrices()
    taps = (_taps(w0_ref, b0_ref), _taps(w1_ref, b1_ref), _taps(w2_ref, b2_ref))
    half = CONV_CH // 2
    outs = ((ozl_ref, ozle_ref, ozlo_ref), (ozc_ref, ozce_ref, ozco_ref))

    def fn(si, r0, n, steps):
        x0 = _conv_at(x0_ref, taps[0], shifts, r0, n, steps)
        x1 = _conv_at(x1_ref, taps[1], shifts, r0, n, steps)
        v = _conv_at(v_ref, taps[2], shifts, r0, n, steps)
        nat_ref, even_ref, odd_ref = outs[si]
        for i, x0_i, x1_i, v_i in zip(steps, x0, x1, v):
            ox0_ref[_chunk_rows(r0, i), :] = x0_i.astype(ox0_ref.dtype)
            zin = x1_i * v_i
            nat_ref[_chunk_rows(0, i), :] = zin.astype(BF16)
            t0 = r0 + pl.multiple_of(i * CONV_CH, CONV_CH)
            hrows = pl.ds(pl.multiple_of(i * half, half), half)
            for g, zs_ref in enumerate(zs_refs):
                lanes = slice(g * LANES, (g + 1) * LANES)
                zs_ref[pl.ds(t0, CONV_CH), :] = zin[:, lanes]
                even_ref[hrows, lanes] = zs_ref[pl.ds(t0, half, stride=2), :].astype(BF16)
                odd_ref[hrows, lanes] = zs_ref[pl.ds(t0 + 1, half, stride=2), :].astype(BF16)

    _conv_loop(segs, fn)


def _conv_specs(t_in, tc, xcol_blk, wcol_blk):
    return [pl.BlockSpec((None, t_in, tc), lambda b, j: (b, 0, j + xcol_blk)),
            pl.BlockSpec((CONV_W, tc), lambda b, j: (0, j + wcol_blk)),
            pl.BlockSpec((1, tc), lambda b, j: (0, j + wcol_blk))]


def _conv_silu(x, w, b, *, xcol, wcol, width, segs, tc=256):
    nb, t_in, _ = x.shape
    return pl.pallas_call(
        functools.partial(_conv_silu_body, segs=segs),
        out_shape=jax.ShapeDtypeStruct((nb, t_in, width), BF16), grid=(nb, width // tc),
        in_specs=_conv_specs(t_in, tc, xcol // tc, wcol // tc),
        out_specs=pl.BlockSpec((None, t_in, tc), lambda b, j: (b, 0, j)),
        name="conv_silu", compiler_params=_cp(("parallel", "parallel")))(x, w, b)


def _conv_rope(x, w, b, cos, sin, *, width, segs, tc=256):
    nb, t_in, _ = x.shape
    tab = pl.BlockSpec((t_in, tc), lambda b, j: (0, 0))
    return pl.pallas_call(
        functools.partial(_conv_rope_body, segs=segs, tc=tc),
        out_shape=jax.ShapeDtypeStruct((nb, t_in, width), BF16), grid=(nb, width // tc),
        in_specs=_conv_specs(t_in, tc, 0, 0) + [tab, tab],
        out_specs=pl.BlockSpec((None, t_in, tc), lambda b, j: (b, 0, j)),
        name="conv_rope", compiler_params=_cp(("parallel", "parallel")))(x, w, b, cos, sin)


def _conv_ffn(up, w, b, *, segs, rows, tc=256):
    nb, t_in, two_ff = up.shape
    ff = two_ff // 2
    return pl.pallas_call(
        functools.partial(_conv_ffn_body, segs=segs),
        out_shape=jax.ShapeDtypeStruct((nb, rows, ff), BF16), grid=(nb, ff // tc),
        in_specs=[pl.BlockSpec((None, rows, tc), lambda b, j: (b, 0, j))] + _conv_specs(rows, tc, ff // tc, 0),
        out_specs=pl.BlockSpec((None, rows, tc), lambda b, j: (b, 0, j)),
        name="conv_ffn", compiler_params=_cp(("parallel", "parallel")))(up, up, w, b)


def _conv_hyena(pr, w, b, *, xcol, wcol, segs, tc=256):
    nb, t_in, _ = pr.shape
    (_, n_lat), (_, n_ctx) = segs
    wd = HY_WIDTH
    out_rows = (t_in, n_lat, n_ctx, n_lat // 2, n_lat // 2, n_ctx // 2, n_ctx // 2)
    in_specs = [pl.BlockSpec((None, t_in, tc), functools.partial(lambda b, j, o: (b, 0, j + o), o=(xcol + m * wd) // tc))
                for m in range(3)]
    wargs = []
    for m in range(3):
        off = (wcol + m * wd) // tc
        in_specs.append(pl.BlockSpec((CONV_W, tc), functools.partial(lambda b, j, o: (0, j + o), o=off)))
        in_specs.append(pl.BlockSpec((1, tc), functools.partial(lambda b, j, o: (0, j + o), o=off)))
        wargs += [w, b]
    return pl.pallas_call(
        functools.partial(_conv_hyena_body, segs=segs),
        out_shape=tuple(jax.ShapeDtypeStruct((nb, rows, wd), BF16) for rows in out_rows),
        grid=(nb, wd // tc), in_specs=in_specs,
        out_specs=tuple(pl.BlockSpec((None, rows, tc), lambda b, j: (b, 0, j)) for rows in out_rows),
        scratch_shapes=[pltpu.VMEM((t_in, LANES), F32)] * (tc // LANES),
        name="conv_hyena", compiler_params=_cp(("parallel", "parallel")))(pr, pr, pr, *wargs)


def _scan_masks(reverse):
    q = SCAN_Q
    ii = lax.broadcasted_iota(jnp.int32, (q, q), 0)
    jj = lax.broadcasted_iota(jnp.int32, (q, q), 1)
    mask = (jj >= ii) if reverse else (jj <= ii)
    mask_t = (jj <= ii) if reverse else (jj >= ii)
    return mask, mask.astype(F32), mask_t.astype(F32)


def _chunk_map(reverse, n_chunks, n_lat_chunks):
    if reverse:
        return lambda k: n_chunks - 1 - k
    return lambda k: (k + n_lat_chunks) % n_chunks


SSD_GPS = 4


def _ssd_body(*refs, reverse, final):
    (xs_ref, b_ref, c_ref, dtc_ref, dtr_ref, dbc_ref, dbr_ref, alc_ref, alr_ref) = refs[:9]
    if final:
        yf_ref, z_ref, dsk_ref, nw_ref, o_ref, h_ref = refs[9:]
    else:
        o_ref, h_ref = refs[9:]

    @pl.when(pl.program_id(2) == 0)
    def _():
        h_ref[...] = jnp.zeros_like(h_ref)

    groups = range(SSD_GPS)
    h = [h_ref[g] for g in groups]
    for c in (reversed(range(SCAN_CPS)) if reverse else range(SCAN_CPS)):
        h = _ssd_chunk(refs, slice(c * SCAN_Q, (c + 1) * SCAN_Q), h, reverse=reverse, final=final)
    for g in groups:
        h_ref[g] = h[g]


def _ssd_chunk(refs, rs, h_prev, *, reverse, final):
    (xs_ref, b_ref, c_ref, dtc_ref, dtr_ref, dbc_ref, dbr_ref, alc_ref, alr_ref) = refs[:9]
    if final:
        yf_ref, z_ref, dsk_ref, nw_ref, o_ref = refs[9:14]
    else:
        o_ref = refs[9]
    q, gw, hd = SCAN_Q, SSD_GW, SSD_HEAD_DIM
    mask, maskf, mask_tf = _scan_masks(reverse)
    groups = range(SSD_GPS)
    pairs = range(SSD_HPG // 2)
    cols = [slice(g * gw, (g + 1) * gw) for g in groups]
    xs = [xs_ref[rs, cols[g]].astype(F32) for g in groups]
    bm = [b_ref[rs, g * SSD_STATE:(g + 1) * SSD_STATE] for g in groups]
    cm = [c_ref[rs, g * SSD_STATE:(g + 1) * SSD_STATE] for g in groups]
    dt_c = [_softplus(dtc_ref[g, rs, :] + dbc_ref[g]) for g in groups]
    dt_r = [_softplus(dtr_ref[g, :, rs] + dbr_ref[g]) for g in groups]
    da_c = [dt_c[g] * (-jnp.exp(alc_ref[g])) for g in groups]
    da_r = [dt_r[g] * (-jnp.exp(alr_ref[g])) for g in groups]
    acum_c_all = _sel_dot(maskf, jnp.concatenate(da_c, axis=1), 3)
    acum_r_all = _dot_sel(jnp.concatenate(da_r, axis=0), mask_tf, 3)
    acum_c = [acum_c_all[:, g * SSD_HPG:(g + 1) * SSD_HPG] for g in groups]
    acum_r = [acum_r_all[g * SSD_HPG:(g + 1) * SSD_HPG, :] for g in groups]
    atot = [jnp.sum(da_c[g], axis=0, keepdims=True) for g in groups]

    er = lax.broadcasted_iota(jnp.int32, (SSD_HPG, gw), 0)
    ec = lax.broadcasted_iota(jnp.int32, (SSD_HPG, gw), 1)
    expand = (ec // hd == er).astype(F32)
    wide = [_dot_sel(jnp.concatenate([dt_c[g], jnp.exp(atot[g] - acum_c[g]), jnp.exp(acum_c[g]),
                                      jnp.broadcast_to(jnp.exp(atot[g]), (SUBLANES, SSD_HPG))], axis=0), expand, 1)
            for g in groups]
    xq = [xs[g] * wide[g][0:q] for g in groups]
    cb = [_dot_nt(cm[g], bm[g]) for g in groups]
    y_off = [wide[g][2 * q:3 * q] * _dot(cm[g], h_prev[g].astype(BF16)) for g in groups]
    lane = lax.broadcasted_iota(jnp.int32, (q, 2 * hd), 1)

    def pair_lhs(g, p):
        ms = [(cb[g] * jnp.where(mask, jnp.exp(acum_c[g][:, r:r + 1] - acum_r[g][r:r + 1, :]), 0.0)).astype(BF16)
              for r in (2 * p, 2 * p + 1)]
        return jnp.concatenate(ms, axis=1)

    def pair_rhs(g, p):
        xp = xq[g][:, 2 * hd * p:2 * hd * (p + 1)]
        return jnp.concatenate([jnp.where(lane < hd, xp, 0.0), jnp.where(lane >= hd, xp, 0.0)], axis=0).astype(BF16)

    y_in = [[_dot(pair_lhs(g, p), pair_rhs(g, p)) for p in pairs] for g in groups]
    y = [y_off[g] + jnp.concatenate(y_in[g], axis=1) for g in groups]
    bt = [bm[g].astype(F32).T.astype(BF16) for g in groups]
    upd = [_dot(bt[g], (xq[g] * wide[g][q:2 * q]).astype(BF16)) for g in groups]
    for g in groups:
        if final:
            ytot = y[g] + yf_ref[rs, cols[g]] + xs[g] * dsk_ref[:, cols[g]]
            gated = ytot * _silu(z_ref[rs, cols[g]].astype(F32))
            o = gated * lax.rsqrt(jnp.mean(gated * gated, axis=-1, keepdims=True) + EPS) * nw_ref[:, cols[g]]
            o_ref[rs, cols[g]] = o.astype(o_ref.dtype)
        else:
            o_ref[rs, cols[g]] = y[g]
    return [wide[g][3 * q:3 * q + 1] * h_prev[g] + upd[g] for g in groups]


def _ssd(xbc, dt_c, dt_r, db_c, db_r, al_c, al_r, *, direction, n_lat, final_args=None):
    nb, t, _ = xbc.shape
    q, gw, hpg = SCAN_CPS * SCAN_Q, SSD_GW, SSD_HPG
    assert n_lat % q == 0 and t % q == 0
    nch = t // q
    reverse = direction == 1
    cmap = _chunk_map(reverse, nch, n_lat // q)
    gps = SSD_GPS
    sw, gww = gps * SSD_STATE, gps * gw
    b0 = SSD_WIDTH // sw
    c0 = b0 + SSD_GROUPS // gps
    d = direction
    in_specs = [
        pl.BlockSpec((None, q, gww), lambda b, g, k: (b, cmap(k), g)),
        pl.BlockSpec((None, q, sw), lambda b, g, k: (b, cmap(k), b0 + g)),
        pl.BlockSpec((None, q, sw), lambda b, g, k: (b, cmap(k), c0 + g)),
        pl.BlockSpec((None, None, gps, q, hpg), lambda b, g, k: (b, d, g, cmap(k), 0)),
        pl.BlockSpec((None, None, gps, hpg, q), lambda b, g, k: (b, d, g, 0, cmap(k))),
        pl.BlockSpec((None, gps, 1, hpg), lambda b, g, k: (d, g, 0, 0)),
        pl.BlockSpec((None, gps, hpg, 1), lambda b, g, k: (d, g, 0, 0)),
        pl.BlockSpec((None, gps, 1, hpg), lambda b, g, k: (d, g, 0, 0)),
        pl.BlockSpec((None, gps, hpg, 1), lambda b, g, k: (d, g, 0, 0)),
    ]
    args = [xbc, xbc, xbc, dt_c, dt_r, db_c, db_r, al_c, al_r]
    final = final_args is not None
    if final:
        yf, pr, dskip, norm_w = final_args
        in_specs += [pl.BlockSpec((None, q, gww), lambda b, g, k: (b, cmap(k), g)),
                     pl.BlockSpec((None, q, gww), lambda b, g, k: (b, cmap(k), g)),
                     pl.BlockSpec((1, gww), lambda b, g, k: (0, g)),
                     pl.BlockSpec((1, gww), lambda b, g, k: (0, g))]
        args += [yf, pr, dskip, norm_w]
    out_shape = jax.ShapeDtypeStruct((nb, t, SSD_WIDTH), BF16 if final else F32)
    return pl.pallas_call(
        functools.partial(_ssd_body, reverse=reverse, final=final),
        out_shape=out_shape, grid=(nb, SSD_GROUPS // gps, nch), in_specs=in_specs,
        out_specs=pl.BlockSpec((None, q, gww), lambda b, g, k: (b, cmap(k), g)),
        scratch_shapes=[pltpu.VMEM((gps, SSD_STATE, gw), F32)],
        name="ssd_bwd" if reverse else "ssd_fwd",
        compiler_params=_cp(("parallel", "parallel", "arbitrary")))(*args)


def _hyfilt_body(f_ref, w1_ref, b1_ref, w2_ref, b2_ref, w3_ref, fr_ref, dl_ref, o_ref, *, tl):
    f = f_ref[...]
    h = jnp.sin(fr_ref[0:1, :] * (_dot(f, w1_ref[...], HI) + b1_ref[...]))
    h = jnp.sin(fr_ref[1:2, :] * (_dot(h, w2_ref[...], HI) + b2_ref[...]))
    h = _dot(h, w3_ref[...], HI)
    dec = jnp.exp(-f[:, 0:1] * dl_ref[...])
    row = pl.program_id(0) * tl + lax.broadcasted_iota(jnp.int32, (tl, 1), 0)
    wd = HY_WIDTH
    o_ref[:, 0:wd] = (h[:, 0:wd] * dec).astype(o_ref.dtype)
    o_ref[:, wd:2 * wd] = jnp.where(row == 0, 0.0, h[:, wd:2 * wd] * dec).astype(o_ref.dtype)


def _hyena_filters(length, w1, b1, w2, b2, w3, freq, *, tl=256):
    p = LANES
    t = jnp.linspace(0.0, 1.0, length, dtype=F32)[:, None]
    w = 2.0 * math.pi * jnp.arange(length, dtype=F32)[:, None] / length
    f = jnp.linspace(1e-4, HY_BANDS - 1, HY_BANDS, dtype=F32)[None, :]
    feats = jnp.concatenate([t, jnp.cos(f * w), -jnp.sin(f * w)], axis=-1)
    feats = jnp.concatenate([feats[0::2], feats[1::2]], axis=0)
    feats = jnp.pad(feats, ((0, 0), (0, p - HY_EMB)))
    po = p - HY_ORDER
    w1p = jnp.pad(w1, ((0, p - HY_EMB), (0, po)))
    w2p = jnp.pad(w2, ((0, po), (0, po)))
    w3p = jnp.pad(w3, ((0, po), (0, 0)))
    b1p = jnp.pad(b1[None, :], ((0, 0), (0, po)))
    b2p = jnp.pad(b2[None, :], ((0, 0), (0, po)))
    frp = jnp.pad(freq, ((0, 0), (0, po)))
    deltas = jnp.abs(jnp.linspace(math.log(HY_TARGET) / HY_FAST, math.log(HY_TARGET) / HY_SLOW, HY_WIDTH,
                                  dtype=F32))[None, :]
    full = lambda shape: pl.BlockSpec(shape, lambda i: (0,) * len(shape))
    out = pl.pallas_call(
        functools.partial(_hyfilt_body, tl=tl),
        out_shape=jax.ShapeDtypeStruct((length, 2 * HY_WIDTH), BF16), grid=(length // tl,),
        in_specs=[pl.BlockSpec((tl, p), lambda i: (i, 0)), full((p, p)), full((1, p)), full((p, p)), full((1, p)),
                  full((p, 2 * HY_WIDTH)), full((2, p)), full((1, HY_WIDTH))],
        out_specs=pl.BlockSpec((tl, 2 * HY_WIDTH), lambda i: (i, 0)),
        name="hyena_filters", compiler_params=_cp(("parallel",)))(feats, w1p, b1p, w2p, b2p, w3p, frp, deltas)
    return out[None]


def _dft_matrices(length):
    n = 2 * length
    k = jnp.arange(length, dtype=jnp.int32)[:, None]
    t = jnp.arange(length, dtype=jnp.int32)[None, :]
    step = 1 << (int(math.log2(length)) // 2)
    part = jnp.arange(length // step, dtype=jnp.int32)[:, None]
    ang_hi = ((part * step * t) % n).astype(F32) * (2.0 * math.pi / n)
    ang_lo = ((jnp.arange(step, dtype=jnp.int32)[:, None] * t) % n).astype(F32) * (2.0 * math.pi / n)
    ch, sh = jnp.cos(ang_hi)[:, None, :], jnp.sin(ang_hi)[:, None, :]
    cl, sl = jnp.cos(ang_lo)[None, :, :], jnp.sin(ang_lo)[None, :, :]
    c = (ch * cl - sh * sl).reshape(length, length)
    s = (sh * cl + ch * sl).reshape(length, length)
    alt_t = jnp.where(t % 2 == 0, 1.0, -1.0).astype(F32)
    fwd = jnp.concatenate([c, jnp.where(k == 0, alt_t, -s)], axis=0)
    alt_r = jnp.where(k % 2 == 0, 1.0, -1.0).astype(F32)
    inv_re = jnp.where(t == 0, 1.0 / n, (2.0 / n) * c)
    inv_im = jnp.where(t == 0, alt_r / n, (-2.0 / n) * s)
    inv = jnp.concatenate([inv_re, inv_im], axis=1)
    return fwd.astype(BF16)[None], inv.astype(BF16)[None]


def _tile_interleave(x, axis, hm):
    shp = x.shape
    n = shp[axis] // 2
    x = x.reshape(shp[:axis] + (2, n // hm, hm) + shp[axis + 1:])
    return jnp.swapaxes(x, axis, axis + 1).reshape(shp)


def _twiddle_table(length, hm):
    k = np.arange(length // 2, dtype=np.float64)[:, None] * (np.pi / length)
    tab = np.concatenate([np.cos(k), np.sin(k)], axis=0)
    tab = tab.reshape(2, -1, hm, 1).swapaxes(0, 1).reshape(length, 1)
    return jnp.asarray(np.broadcast_to(tab, (length, LANES)).astype(np.float32))


def _lanes(x, width):
    return jnp.tile(x, (1, width // LANES))


def _parity_banks(e, ot, tw, hm):
    er, ei, otr, oti = e[0:hm], e[hm:], ot[0:hm], ot[hm:]
    c, s = _lanes(tw[0:hm], e.shape[1]), _lanes(tw[hm:], e.shape[1])
    orr = otr * c + oti * s
    oi = oti * c - otr * s
    return (er + orr, ei + oi, er - orr, ei - oi), (c, s)


def _is_bin0(row_blk, hm):
    return (row_blk == 0) & (lax.broadcasted_iota(jnp.int32, (hm, 1), 0) == 0)


def _hy_kbank_body(f_ref, hef_ref, heb_ref, hof_ref, hob_ref, tw_ref, ka_ref, kb_ref, *, hm):
    fwd, tw = f_ref[...], tw_ref[...]
    ef, eb, of, ob = (_dot(fwd, r[...]) for r in (hef_ref, heb_ref, hof_ref, hob_ref))
    (afr, afi, bfr, bfi), _ = _parity_banks(ef, of, tw, hm)
    (abr, abi, bbr, bbi), _ = _parity_banks(eb, ob, tw, hm)
    first = _is_bin0(pl.program_id(0), hm)
    kai = jnp.where(first, ef[hm:] + eb[hm:], afi - abi)
    kbi = jnp.where(first, ob[hm:] - of[hm:], bfi - bbi)
    ka_ref[...] = jnp.concatenate([afr + abr, kai], axis=0)
    kb_ref[...] = jnp.concatenate([bfr + bbr, kbi], axis=0)


def _hy_kbank(fwd, hfb, tw, *, tm, tc=512):
    _, ln, c2 = hfb.shape
    c, h = c2 // 2, ln // 2
    taps = lambda row_blk, col_off: pl.BlockSpec((None, h, tc), lambda i, j: (0, row_blk, j + col_off))
    bank = pl.BlockSpec((None, tm, tc), lambda i, j: (0, i, j))
    return pl.pallas_call(
        functools.partial(_hy_kbank_body, hm=tm // 2),
        out_shape=(jax.ShapeDtypeStruct((1, ln, c), F32),) * 2, grid=(ln // tm, c // tc),
        in_specs=[pl.BlockSpec((None, tm, h), lambda i, j: (0, i, 0)),
                  taps(0, 0), taps(0, c // tc), taps(1, 0), taps(1, c // tc),
                  pl.BlockSpec((tm, LANES), lambda i, j: (i, 0))],
        out_specs=(bank, bank), name="hy_kbank",
        compiler_params=_cp(("parallel", "parallel")))(fwd, hfb, hfb, hfb, hfb, tw)


def _hy_spectrum_body(f_ref, ze_ref, zo_ref, tw_ref, ka_ref, kb_ref, p_ref, q_ref, *, hm):
    fwd = f_ref[...]
    e, acc = _dot(fwd, ze_ref[...]), _dot(fwd, zo_ref[...])
    row_blk = pl.program_id(2)
    (ar, ai, br, bi), (c, s) = _parity_banks(e, acc, tw_ref[...], hm)
    kar, kai, kbr, kbi = ka_ref[0:hm], ka_ref[hm:], kb_ref[0:hm], kb_ref[hm:]
    yar, yai = ar * kar - ai * kai, ar * kai + ai * kar
    ybr, ybi = br * kbr - bi * kbi, br * kbi + bi * kbr
    qr, qi = yar - ybr, yai - ybi
    pr, pi = yar + ybr, yai + ybi
    qpr, qpi = qr * c - qi * s, qi * c + qr * s
    top = slice(0, SUBLANES)
    first = _is_bin0(row_blk, SUBLANES)
    y0, yl = ar[top] * kar[top], br[top] * kbr[top]
    amr, ami = e[hm:hm + SUBLANES], -acc[hm:hm + SUBLANES]
    ymr, ymi = amr * kai[top] - ami * kbi[top], amr * kbi[top] + ami * kai[top]

    def patch(full, special):
        return jnp.concatenate([jnp.where(first, special, full[top]), full[SUBLANES:]], axis=0)

    pr, pi = patch(pr, y0 + yl), patch(pi, 2.0 * ymr)
    qpr, qpi = patch(qpr, y0 - yl), patch(qpi, -2.0 * ymi)
    p_ref[...] = jnp.concatenate([pr, pi], axis=0).astype(p_ref.dtype)
    q_ref[...] = jnp.concatenate([qpr, qpi], axis=0).astype(q_ref.dtype)


def _hy_spectrum(fwd, ze, zo, tw, ka, kb, *, tm, tn):
    nb, h, c = ze.shape
    ln = fwd.shape[1]
    data = pl.BlockSpec((None, h, tn), lambda j, b, i: (b, 0, j))
    bank = pl.BlockSpec((None, tm, tn), lambda j, b, i: (0, i, j))
    out = pl.BlockSpec((None, tm, tn), lambda j, b, i: (b, i, j))
    return pl.pallas_call(
        functools.partial(_hy_spectrum_body, hm=tm // 2),
        out_shape=(jax.ShapeDtypeStruct((nb, ln, c), BF16),) * 2, grid=(c // tn, nb, ln // tm),
        in_specs=[pl.BlockSpec((None, tm, h), lambda j, b, i: (0, i, 0)), data, data,
                  pl.BlockSpec((tm, LANES), lambda j, b, i: (i, 0)), bank, bank],
        out_specs=(out, out), name="hy_spectrum",
        compiler_params=_cp(("parallel", "parallel", "parallel")))(fwd, ze, zo, tw, ka, kb)


def _hyena_out_body(*refs, segs):
    ns = len(segs)
    x0_ref, db_ref, o_ref = refs[3 * ns:3 * ns + 3]
    ys_refs = refs[3 * ns + 3:]
    half = CONV_CH // 2
    for si, (r0, n) in enumerate(segs):
        ye_ref, yo_ref, zin_ref = refs[3 * si:3 * si + 3]

        def body(i, carry, r0=r0, ye_ref=ye_ref, yo_ref=yo_ref, zin_ref=zin_ref):
            t0 = pl.multiple_of(i * CONV_CH, CONV_CH)
            hrows = pl.ds(pl.multiple_of(i * half, half), half)
            for g, ys_ref in enumerate(ys_refs):
                lanes = slice(g * LANES, (g + 1) * LANES)
                ys_ref[pl.ds(r0 + t0, half, stride=2), :] = ye_ref[hrows, lanes].astype(F32)
                ys_ref[pl.ds(r0 + t0 + 1, half, stride=2), :] = yo_ref[hrows, lanes].astype(F32)
            rows = pl.ds(r0 + t0, CONV_CH)
            y = jnp.concatenate([ys_ref[rows, :] for ys_ref in ys_refs], axis=1)
            zin = zin_ref[pl.ds(t0, CONV_CH), :].astype(F32)
            o_ref[rows, :] = (x0_ref[rows, :].astype(F32) * (y + zin * db_ref[...])).astype(o_ref.dtype)
            return carry

        lax.fori_loop(0, n // CONV_CH, body, 0, unroll=min(4, n // CONV_CH))


def _hyena_out(parts, x0, dbias, *, segs, tc=256):
    nb, t, c = x0.shape
    in_specs, args = [], []
    for (ye, yo, zin), (_, n) in zip(parts, segs):
        in_specs += [pl.BlockSpec((None, n // 2, tc), lambda b, j: (b, 0, j)),
                     pl.BlockSpec((None, n // 2, tc), lambda b, j: (b, 0, j)),
                     pl.BlockSpec((None, n, tc), lambda b, j: (b, 0, j))]
        args += [ye, yo, zin]
    in_specs += [pl.BlockSpec((None, t, tc), lambda b, j: (b, 0, j)), pl.BlockSpec((1, tc), lambda b, j: (0, j))]
    return pl.pallas_call(
        functools.partial(_hyena_out_body, segs=segs),
        out_shape=jax.ShapeDtypeStruct((nb, t, c), BF16), grid=(nb, c // tc), in_specs=in_specs,
        out_specs=pl.BlockSpec((None, t, tc), lambda b, j: (b, 0, j)),
        scratch_shapes=[pltpu.VMEM((t, LANES), F32)] * (tc // LANES), name="hyena_out",
        compiler_params=_cp(("parallel", "parallel")))(*args, x0, dbias)


def _hyena_longconv(ze, zo, hfb, *, tn=1024):
    nb, h, c = ze.shape
    ln = 2 * h
    tm_s = min(512, ln)
    hm = tm_s // 2
    fwd, inv = _dft_matrices(h)
    fwd = _tile_interleave(fwd, 1, hm)
    inv = _tile_interleave(inv * 0.5, 2, hm)
    tw = _twiddle_table(ln, hm)
    ka, kb = _hy_kbank(fwd, hfb, tw, tm=tm_s)
    p, q = _hy_spectrum(fwd, ze, zo, tw, ka, kb, tm=tm_s, tn=tn)
    tm_i, tk_i = min(1024, h), ln
    ye = _matmul(inv, p, tm=tm_i, tn=tn, tk=tk_i, out_dtype=BF16, name="hy_idft_even")
    yo = _matmul(inv, q, tm=tm_i, tn=tn, tk=tk_i, out_dtype=BF16, name="hy_idft_odd")
    return ye, yo


def _mlstm_body(*refs, reverse, final):
    q_ref, k_ref, v_ref, gc_ref, gr_ref = refs[:5]
    if final:
        hf_ref, og_ref, nw_ref, o_ref, ct_ref, n_ref, m_ref = refs[5:]
    else:
        o_ref, ct_ref, n_ref, m_ref = refs[5:]
    nh, dk, dv = ML_HEADS, ML_QK_DIM, ML_V_DIM
    scale = dk ** -0.5

    @pl.when(pl.program_id(1) == 0)
    def _():
        ct_ref[...] = jnp.zeros_like(ct_ref)
        n_ref[...] = jnp.zeros_like(n_ref)
        m_ref[...] = jnp.zeros_like(m_ref)

    state = ([ct_ref[h] for h in range(nh)], n_ref[...], m_ref[...])
    for c in (reversed(range(SCAN_CPS)) if reverse else range(SCAN_CPS)):
        state = _mlstm_chunk(refs, slice(c * SCAN_Q, (c + 1) * SCAN_Q), state, reverse=reverse, final=final)
    for h in range(nh):
        ct_ref[h] = state[0][h]
    n_ref[...] = state[1]
    m_ref[...] = state[2]


def _mlstm_chunk(refs, rs, state, *, reverse, final):
    q_ref, k_ref, v_ref, gc_ref, gr_ref = refs[:5]
    if final:
        hf_ref, og_ref, nw_ref, o_ref = refs[5:9]
    else:
        o_ref = refs[5]
    nh, dk, dv = ML_HEADS, ML_QK_DIM, ML_V_DIM
    scale = dk ** -0.5
    cts, n_all, m_all = state
    gc, gr = gc_ref[rs, :], gr_ref[:, rs]
    li_c, lf_c = gc[:, 0:nh], _log_sigmoid(gc[:, nh:2 * nh])
    li_r, lf_r = gr[0:nh, :], _log_sigmoid(gr[nh:2 * nh, :])
    mask, maskf, mask_tf = _scan_masks(reverse)
    bc_c = _dot(maskf, lf_c, HI)
    bc_r = _dot(lf_r, mask_tf, HI)
    btot_c = jnp.sum(lf_c, axis=0, keepdims=True)
    heads = range(nh)
    qs = [q_ref[rs, dk * h:dk * (h + 1)] for h in heads]
    ks = [k_ref[rs, dk * h:dk * (h + 1)] for h in heads]
    vs = [v_ref[rs, dv * h:dv * (h + 1)] for h in heads]
    m_s = [m_all[h:h + 1, 0:1] for h in heads]
    n_s = [n_all[h:h + 1, :] for h in heads]
    qk = [_dot_nt(qs[h], ks[h]) for h in heads]
    qc = [_dot(qs[h], cts[h].astype(BF16)) for h in heads]
    dmat = [jnp.where(mask, bc_c[:, h:h + 1] - bc_r[h:h + 1, :] + li_r[h:h + 1, :], -jnp.inf) for h in heads]
    inter = [bc_c[:, h:h + 1] + m_s[h] for h in heads]
    m_t = [jnp.maximum(inter[h], jnp.max(dmat[h], axis=1, keepdims=True)) for h in heads]
    s = [qk[h] * scale * jnp.exp(dmat[h] - m_t[h]) for h in heads]
    dec = [jnp.exp(inter[h] - m_t[h]) for h in heads]
    num = [_dot(s[h].astype(BF16), vs[h]) + (dec[h] * scale) * qc[h] for h in heads]
    qn = [jnp.sum(qs[h].astype(F32) * n_s[h], axis=1, keepdims=True) * scale for h in heads]
    den = [jnp.sum(s[h], axis=1, keepdims=True) + dec[h] * qn[h] for h in heads]
    hh = [num[h] / jnp.maximum(jnp.abs(den[h]), jnp.exp(-m_t[h])) for h in heads]

    wst_c = [btot_c[:, h:h + 1] - bc_c[:, h:h + 1] + li_c[:, h:h + 1] for h in heads]
    wst_r = [btot_c[:, h:h + 1] - bc_r[h:h + 1, :] + li_r[h:h + 1, :] for h in heads]
    m_new = [jnp.maximum(btot_c[:, h:h + 1] + m_s[h], jnp.max(wst_r[h], axis=1, keepdims=True)) for h in heads]
    dstate = [jnp.exp(btot_c[:, h:h + 1] + m_s[h] - m_new[h]) for h in heads]
    kw = [ks[h].astype(F32) * jnp.exp(wst_c[h] - m_new[h]) for h in heads]
    upd = [_dot(kw[h].T.astype(BF16), vs[h]) for h in heads]
    n_rows = [dstate[h] * n_s[h] + jnp.sum(kw[h], axis=0, keepdims=True) for h in heads]
    m_rows = [jnp.broadcast_to(m_new[h], (1, LANES)) for h in heads]

    for h in heads:
        cols = slice(dv * h, dv * (h + 1))
        if final:
            hs = hh[h] + hf_ref[rs, cols]
            hn = hs * lax.rsqrt(jnp.mean(hs * hs, axis=-1, keepdims=True) + EPS) * nw_ref[:, cols]
            o_ref[rs, cols] = (hn * _sigmoid(og_ref[rs, cols].astype(F32))).astype(o_ref.dtype)
        else:
            o_ref[rs, cols] = hh[h]

    return ([dstate[h] * cts[h] + upd[h] for h in heads], jnp.concatenate(n_rows, axis=0),
            jnp.concatenate(m_rows, axis=0))


def _mlstm(qk, pr, g_c, g_r, *, direction, n_lat, final_args=None):
    nb, t, _ = qk.shape
    q = SCAN_CPS * SCAN_Q
    assert n_lat % q == 0 and t % q == 0
    nch = t // q
    reverse = direction == 1
    cmap = _chunk_map(reverse, nch, n_lat // q)
    d = direction
    in_specs = [
        pl.BlockSpec((None, q, ML_QK_WIDTH), lambda b, k: (b, cmap(k), 0)),
        pl.BlockSpec((None, q, ML_QK_WIDTH), lambda b, k: (b, cmap(k), 1)),
        pl.BlockSpec((None, q, ML_WIDTH), lambda b, k: (b, cmap(k), 2 * ML_QK_WIDTH // ML_WIDTH)),
        pl.BlockSpec((None, None, q, 2 * ML_HEADS), lambda b, k: (b, d, cmap(k), 0)),
        pl.BlockSpec((None, None, 2 * ML_HEADS, q), lambda b, k: (b, d, 0, cmap(k))),
    ]
    args = [qk, qk, pr, g_c, g_r]
    final = final_args is not None
    if final:
        hf, norm_w = final_args
        in_specs += [pl.BlockSpec((None, q, ML_WIDTH), lambda b, k: (b, cmap(k), 0)),
                     pl.BlockSpec((None, q, ML_WIDTH), lambda b, k: (b, cmap(k), 2 * ML_QK_WIDTH // ML_WIDTH + 1)),
                     pl.BlockSpec((1, ML_WIDTH), lambda b, k: (0, 0))]
        args += [hf, pr, norm_w]
    out_shape = jax.ShapeDtypeStruct((nb, t, ML_WIDTH), BF16 if final else F32)
    return pl.pallas_call(
        functools.partial(_mlstm_body, reverse=reverse, final=final),
        out_shape=out_shape, grid=(nb, nch), in_specs=in_specs,
        out_specs=pl.BlockSpec((None, q, ML_WIDTH), lambda b, k: (b, cmap(k), 0)),
        scratch_shapes=[pltpu.VMEM((ML_HEADS, ML_QK_DIM, ML_V_DIM), F32), pltpu.VMEM((ML_HEADS, ML_QK_DIM), F32),
                        pltpu.VMEM((ML_HEADS, LANES), F32)],
        name="mlstm_bwd" if reverse else "mlstm_fwd",
        compiler_params=_cp(("parallel", "arbitrary")))(*args)


def _na_window_start(j, rows):
    return jnp.clip(j * NA_QROWS - NA_ROWS // 2, 0, rows - NA_KROWS)


def _na_part_offset(j, part, start, clip=jnp.clip):
    first = j * NA_QROWS + part * (NA_QROWS // NA_PARTS) - NA_ROWS // 2
    return clip(first - start, 0, NA_KROWS - NA_PART_KROWS) // NA_TAB_ROWS * NA_TAB_ROWS


def _na_body(q_ref, k_ref, v_ref, tab_ref, qw_ref, kw_ref, o_ref, kn_ref, v1_ref, *, n_lat, n_ctx):
    dh = NA_HEAD_DIM
    scale = dh ** -0.5
    rows = n_lat // GRID_W
    j = pl.program_id(2)
    heads = range(NA_HPS)
    hcols = [slice(hh * dh, (hh + 1) * dh) for hh in heads]

    @pl.when(j == 0)
    def _():
        prep = (n_lat + n_ctx) // NA_PREP_STEPS
        one_hot = (lax.broadcasted_iota(jnp.int32, (prep, LANES), 1) == 0).astype(v1_ref.dtype)

        def body(i, carry):
            r = pl.ds(pl.multiple_of(i * prep, 16), prep)
            for hh in heads:
                kk = k_ref[r, hcols[hh]].astype(F32)
                kn = kk * lax.rsqrt(jnp.mean(kk * kk, axis=-1, keepdims=True) + EPS) * kw_ref[...]
                kn_ref[r, hcols[hh]] = kn.astype(kn_ref.dtype)
                v1_ref[hh, r, 0:dh] = v_ref[r, hcols[hh]]
                v1_ref[hh, r, dh:dh + LANES] = one_hot
            return carry

        lax.fori_loop(0, NA_PREP_STEPS, body, 0)

    start = _na_window_start(j, rows)
    ctx = pl.ds(n_lat, n_ctx)
    rpp = NA_QROWS // NA_PARTS
    pq = rpp * GRID_W
    chains = [(hh, i) for hh in heads for i in range(NA_PARTS)]
    off = [_na_part_offset(j, i, start) for i in range(NA_PARTS)]
    win = [pl.ds(pl.multiple_of((start + off[i]) * GRID_W, GRID_W), NA_PART_KROWS * GRID_W) for i in range(NA_PARTS)]
    qq = [q_ref[i * pq:(i + 1) * pq, hcols[hh]].astype(F32) for hh, i in chains]
    qn = [(x * lax.rsqrt(jnp.mean(x * x, axis=-1, keepdims=True) + EPS) * (qw_ref[...] * scale)).astype(BF16)
          for x in qq]
    sw = [_dot_nt(qn[c], kn_ref[win[i], hcols[hh]]) for c, (hh, i) in enumerate(chains)]
    sc = [_dot_nt(qn[c], kn_ref[ctx, hcols[hh]]) for c, (hh, i) in enumerate(chains)]

    def with_bias(c, hh, i):
        out = []
        for r in range(rpp):
            d0 = start - (j * NA_QROWS + i * rpp + r) + NA_ROWS - 1 + NA_QROWS
            blocks = tab_ref[hh, d0, pl.ds(off[i] // NA_TAB_ROWS, NA_PART_KROWS // NA_TAB_ROWS)]
            bias = jnp.concatenate([blocks[b] for b in range(NA_PART_KROWS // NA_TAB_ROWS)], axis=1)
            out.append(sw[c][r * GRID_W:(r + 1) * GRID_W] + bias)
        return jnp.concatenate(out, axis=0)

    sb = [with_bias(c, hh, i) for c, (hh, i) in enumerate(chains)]
    m = [jnp.maximum(jnp.max(a, axis=1, keepdims=True), jnp.max(b, axis=1, keepdims=True)) for a, b in zip(sb, sc)]
    pw = [jnp.exp(a - mm) for a, mm in zip(sb, m)]
    pc = [jnp.exp(a - mm) for a, mm in zip(sc, m)]
    o = [_dot(pw[c].astype(BF16), v1_ref[hh, win[i], :]) + _dot(pc[c].astype(BF16), v1_ref[hh, ctx, :])
         for c, (hh, i) in enumerate(chains)]
    for c, (hh, i) in enumerate(chains):
        o_ref[i * pq:(i + 1) * pq, hcols[hh]] = (o[c][:, 0:dh] / o[c][:, dh:dh + 1]).astype(o_ref.dtype)


def _na_bias_table(rpb, rows):
    nd = 2 * NA_QROWS
    valid = np.zeros((nd, NA_KROWS), bool)
    seen = np.zeros((nd,), bool)
    for j in range(rows // NA_QROWS):
        start = int(np.clip(j * NA_QROWS - NA_ROWS // 2, 0, rows - NA_KROWS))
        for qr in range(NA_QROWS):
            r = j * NA_QROWS + qr
            rs = int(np.clip(r - NA_ROWS // 2, 0, rows - NA_ROWS))
            d = start - r + NA_ROWS - 1 + NA_QROWS
            v = np.array([(rs <= start + i < rs + NA_ROWS) for i in range(NA_KROWS)])
            assert 0 <= d < nd and (not seen[d] or (valid[d] == v).all())
            valid[d], seen[d] = v, True
            off = int(_na_part_offset(j, qr // (NA_QROWS // NA_PARTS), start, clip=np.clip))
            assert off % NA_TAB_ROWS == 0 and not v[:off].any() and not v[off + NA_PART_KROWS:].any()
    col = np.arange(GRID_W)
    cs = np.clip(col - NA_COLS // 2, 0, GRID_W - NA_COLS)
    in_win = (col[None, :] >= cs[:, None]) & (col[None, :] < cs[:, None] + NA_COLS)
    ndr, ndc, w = 2 * NA_ROWS - 1, 2 * NA_COLS - 1, GRID_W
    dr = (np.arange(nd)[:, None] - NA_QROWS) + np.arange(NA_KROWS)[None, :]
    ok = valid[:, None, :, None] & in_win[None, :, None, :] & (dr >= 0)[:, None, :, None] \
        & (dr < ndr)[:, None, :, None]
    ext = jnp.pad(rpb, ((0, 0), (0, 0), (w, w)), mode="edge")
    c0 = w + NA_COLS - 1
    t1 = jnp.stack([ext[:, :, c0 - qc:c0 - qc + w] for qc in range(w)], axis=2)
    t1p = jnp.pad(t1, ((0, 0), (NA_QROWS, nd + NA_KROWS - 1 - NA_QROWS - ndr), (0, 0), (0, 0)))
    wide = jnp.concatenate([t1p[:, m] for m in range(nd + NA_KROWS - 1)], axis=-1)
    wpad = LANES * pl.cdiv(wide.shape[-1], LANES)
    wide = jnp.pad(wide, ((0, 0), (0, 0), (0, wpad - wide.shape[-1])))
    nblk, bw = NA_KROWS // NA_TAB_ROWS, NA_TAB_ROWS * w
    okf = jnp.asarray(ok.reshape(nd, w, nblk, bw).transpose(0, 2, 1, 3).astype(np.float32))

    def body(wide_ref, ok_ref, o_ref):
        for d in range(nd):
            for b in range(nblk):
                lo = d * w + b * bw
                o_ref[d, b] = jnp.where(ok_ref[d, b] != 0.0, wide_ref[:, lo:lo + bw], -jnp.inf)

    return pl.pallas_call(
        body, out_shape=jax.ShapeDtypeStruct((rpb.shape[0], nd, nblk, w, bw), F32), grid=(rpb.shape[0],),
        in_specs=[pl.BlockSpec((None, w, wpad), lambda h: (h, 0, 0)),
                  pl.BlockSpec((nd, nblk, w, bw), lambda h: (0, 0, 0, 0))],
        out_specs=pl.BlockSpec((None, nd, nblk, w, bw), lambda h: (h, 0, 0, 0, 0)),
        name="na_bias_table", compiler_params=_cp(("parallel",)))(wide, okf)


def _na(pr, tab, q_w, k_w, *, n_lat, n_ctx, qcol):
    nb, t, _ = pr.shape
    dh, nh, hps = NA_HEAD_DIM, NA_HEADS, NA_HPS
    nq = NA_QROWS * GRID_W
    hw = hps * dh
    assert qcol % hw == 0 and nh % hps == 0
    qb = qcol // hw
    return pl.pallas_call(
        functools.partial(_na_body, n_lat=n_lat, n_ctx=n_ctx),
        out_shape=jax.ShapeDtypeStruct((nb, n_lat, NA_WIDTH), BF16), grid=(nb, nh // hps, n_lat // nq),
        in_specs=[pl.BlockSpec((None, nq, hw), lambda b, h, j: (b, j, qb + h)),
                  pl.BlockSpec((None, t, hw), lambda b, h, j: (b, 0, qb + nh // hps + h)),
                  pl.BlockSpec((None, t, hw), lambda b, h, j: (b, 0, qb + 2 * (nh // hps) + h)),
                  pl.BlockSpec((hps,) + tab.shape[1:], lambda b, h, j: (h, 0, 0, 0, 0)),
                  pl.BlockSpec((1, dh), lambda b, h, j: (0, 0)),
                  pl.BlockSpec((1, dh), lambda b, h, j: (0, 0))],
        out_specs=pl.BlockSpec((None, nq, hw), lambda b, h, j: (b, j, h)),
        scratch_shapes=[pltpu.VMEM((t, hw), BF16), pltpu.VMEM((hps, t, dh + LANES), BF16)], name="na",
        compiler_params=_cp(("parallel", "parallel", "arbitrary")))(pr, pr, pr, tab, q_w, k_w)


def _rope_tables(n_lat, n_ctx, width):
    dh = ML_QK_DIM
    nf = dh // 4
    t = jnp.arange(n_lat)
    inv = ROPE_THETA ** (-jnp.arange(nf, dtype=F32) / nf)
    lane = np.arange(dh)
    pos = jnp.where((lane < dh // 2)[None, :], (t // GRID_W)[:, None], (t % GRID_W)[:, None]).astype(F32)
    ang = pos * inv[lane % nf][None, :]
    sign = np.where((lane % (dh // 2)) < nf, -1.0, 1.0).astype(np.float32)
    cos = jnp.concatenate([jnp.cos(ang), jnp.ones((n_ctx, dh), F32)], axis=0)
    sin = jnp.concatenate([jnp.sin(ang) * sign[None, :], jnp.zeros((n_ctx, dh), F32)], axis=0)
    reps = width // dh
    return jnp.tile(cos, (1, reps)), jnp.tile(sin, (1, reps))


def _ada_mod(c, c_ctx, w_all, b_all, layer):
    nb, d = c.shape
    rows = 16
    cv = jnp.concatenate([c, c_ctx[None, :], jnp.zeros((rows - nb - 1, d), F32)], axis=0)
    a = (cv * _sigmoid(cv)).astype(BF16)[None]
    out = _matmul(a, w_all, w_index=layer, tm=rows, tn=1024, tk=d, out_dtype=F32, name="ada_mod")[0]
    out = out[:nb + 1] + b_all[layer][None, :]
    return out.reshape(nb + 1, 6, d).transpose(1, 0, 2)


def _lat_ctx(m, nb):
    ctx = jnp.broadcast_to(m[nb][None, :], (nb, m.shape[1]))
    return jnp.stack([m[:nb], ctx], axis=1)[:, :, None, :]


def _pad_cols(w, n):
    return jnp.pad(w, ((0, 0), (0, n - w.shape[1])))


def _col_row_forms(g, groups):
    nb, t, _ = g.shape
    g5 = g.reshape(nb, t, 2, groups, -1)
    return g5.transpose(0, 2, 3, 1, 4), g5.transpose(0, 2, 3, 4, 1)


ROW_TILES = (1088, 1024, 512, 256)


def _row_tile(rows):
    return next(tm for tm in ROW_TILES if rows % tm == 0)


def _mod_pair(mods, shift, scale, nb):
    return jnp.stack([_lat_ctx(mods[shift], nb), _lat_ctx(mods[scale], nb)], axis=1)


def _in_projection(u, w_all, layer, *, rows, name, cols=None, w_nk=False):
    tm = _row_tile(rows)
    tm = 2 * tm if rows % (2 * tm) == 0 else tm
    return _matmul(u, w_all, w_index=layer, w_nk=w_nk, cols=cols, tm=tm, tn=1024, tk=u.shape[2], out_dtype=BF16,
                   rows=rows, vmem=VMEM_LIMIT_BIG, name=name)


def _out_projection(a_list, w_all, x, gate, layer, *, rows, n_lat, next_norm, name):
    tm = _row_tile(rows)
    tn = 512
    x = _matmul_gated_residual_rows(a_list, w_all, x, gate, w_index=layer, tm=tm, tn=tn, rows=rows, n_lat=n_lat,
                                    name=name)
    if next_norm is None:
        return x, None
    return x, _rmsmod(x, *next_norm, tm=tm, rows=rows, n_lat=n_lat)


def _conv_ffn_block(x, u, mods, w_up_all, conv_w, conv_b, w_down_all, layer, *, rows, n_lat, segs, next_norm):
    nb = x.shape[0]
    up = _in_projection(u, w_up_all, layer, rows=rows, name="ffn_up")
    hid = _conv_ffn(up, conv_w, conv_b[None, :], segs=segs, rows=rows)
    return _out_projection((hid,), w_down_all, x, _lat_ctx(mods[5], nb), layer, rows=rows, n_lat=n_lat,
                           next_norm=next_norm, name="ffn_down")


def _even_layer(x, u, mods, w_in_all, conv_w, conv_b, dt_bias, a_log, d_skip, ssd_norm_w,
                hy_w1, hy_b1, hy_w2, hy_b2, hy_w3, hy_freq, hy_bias, w_out_all, j, *, n_lat, n_ctx, next_norm):
    nb, t, d = x.shape
    segs = ((0, n_lat), (n_lat, n_ctx))
    tm = _row_tile(t)
    n_main = SSD_WIDTH + EV_CONV_CH
    pr = _in_projection(u, jnp.swapaxes(w_in_all, 1, 2), j, rows=t, cols=n_main, w_nk=True, name="ev_in")
    w_dt = _pad_cols(w_in_all[j, :, n_main:], LANES).astype(BF16)
    dtp = _matmul(u, w_dt[None], tm=tm, tn=LANES, tk=d, out_dtype=F32, name="ev_in_dt")[:, :, :2 * SSD_HEADS]
    cb = conv_b[None, :]
    xbc = _conv_silu(pr, conv_w, cb, xcol=SSD_WIDTH, wcol=0, width=SSD_XBC, segs=segs)
    x0, zin_l, zin_c, zle, zlo, zce, zco = _conv_hyena(pr, conv_w, cb, xcol=SSD_WIDTH + SSD_XBC, wcol=SSD_XBC,
                                                       segs=segs)

    dt_c, dt_r = _col_row_forms(dtp, SSD_GROUPS)
    db = dt_bias.reshape(2, SSD_GROUPS, 1, SSD_HPG)
    al = a_log.reshape(2, SSD_GROUPS, 1, SSD_HPG)
    scan_args = (xbc, dt_c, dt_r, db, db.transpose(0, 1, 3, 2), al, al.transpose(0, 1, 3, 2))
    yf = _ssd(*scan_args, direction=0, n_lat=n_lat)
    dsk = jnp.repeat(d_skip, SSD_HEAD_DIM)[None, :]
    y_ssd = _ssd(*scan_args, direction=1, n_lat=n_lat, final_args=(yf, pr, dsk, ssd_norm_w[None, :]))

    hyb = hy_bias[None, :]
    filt = functools.partial(_hyena_filters, w1=hy_w1, b1=hy_b1, w2=hy_w2, b2=hy_b2, w3=hy_w3, freq=hy_freq)
    lat = _hyena_longconv(zle, zlo, filt(n_lat)) + (zin_l,)
    ctx = _hyena_longconv(zce, zco, filt(n_ctx)) + (zin_c,)
    y_hy = _hyena_out((lat, ctx), x0, hyb, segs=segs)
    return _out_projection((y_ssd, y_hy), w_out_all, x, _lat_ctx(mods[2], nb), j, rows=t, n_lat=n_lat,
                           next_norm=next_norm, name="ev_out")


def _odd_layer_latent(x, u, mods, w_in_all, conv_w, conv_b, gate_b, ml_norm_w, q_norm_w, k_norm_w, rpb,
                      w_out_all, j, *, n_lat, n_ctx, next_norm):
    nb, t, d = x.shape
    segs = ((0, n_lat), (n_lat, n_ctx))
    tm = _row_tile(t)
    o3 = 2 * ML_QK_WIDTH + 2 * ML_WIDTH
    o4 = o3 + 4 * ML_HEADS
    w_nk = jnp.swapaxes(w_in_all, 1, 2)
    pr = _in_projection(u, w_nk, j, rows=t, cols=o3, w_nk=True, name="od_in_mlstm")
    pr_na = _in_projection(u, w_nk[j:j + 1, o4:, :], 0, rows=t, w_nk=True, name="od_in_na")
    gates = _matmul(u, _pad_cols(w_in_all[j, :, o3:o4], LANES).astype(BF16)[None], tm=tm, tn=LANES, tk=d,
                    out_dtype=F32, name="od_in_gates")[:, :, :4 * ML_HEADS] + gate_b.reshape(-1)
    cos, sin = _rope_tables(n_lat, n_ctx, 256)
    qk = _conv_rope(pr, conv_w, conv_b[None, :], cos, sin, width=2 * ML_QK_WIDTH, segs=segs)
    g5 = gates.reshape(nb, t, 2, 2 * ML_HEADS)
    g_c, g_r = g5.transpose(0, 2, 1, 3), g5.transpose(0, 2, 3, 1)
    hf = _mlstm(qk, pr, g_c, g_r, direction=0, n_lat=n_lat)
    y_ml = _mlstm(qk, pr, g_c, g_r, direction=1, n_lat=n_lat, final_args=(hf, ml_norm_w[None, :]))
    tab = _na_bias_table(rpb, n_lat // GRID_W)
    y_na = _na(pr_na, tab, q_norm_w[None, :], k_norm_w[None, :], n_lat=n_lat, n_ctx=n_ctx, qcol=0)
    return _out_projection((y_ml, y_na), w_out_all, x, _lat_ctx(mods[2], nb), j, rows=n_lat, n_lat=n_lat,
                           next_norm=next_norm, name="od_out")


def kernel(x, c, ctx, c_ctx, ada_w, ada_b, norm_w, ev_w_in, ev_conv_w, ev_conv_b, ssd_dt_bias, ssd_a_log, ssd_d,
           ssd_norm_w, hy_w1, hy_b1, hy_w2, hy_b2, hy_w3, hy_freq, hy_bias, ev_w_out, od_w_in, ml_conv_w, ml_conv_b,
           ml_gate_b, ml_norm_w, na_q_norm_w, na_k_norm_w, na_rpb, od_w_out, ffn_w_up, ffn_conv_w, ffn_conv_b,
           ffn_w_down):
    depth = ada_w.shape[0]
    assert depth == 2, "layer schedule below is written for one even and one odd layer"
    n_lat, n_ctx = x.shape[1], ctx.shape[1]
    t = n_lat + n_ctx
    both = ((0, n_lat), (n_lat, n_ctx))
    xs = jnp.concatenate([x, ctx], axis=1)

    nb = x.shape[0]
    w_down = ffn_w_down.astype(BF16)
    mods0 = _ada_mod(c, c_ctx, ada_w, ada_b, 0)
    mods1 = _ada_mod(c, c_ctx, ada_w, ada_b, 1)
    norm = lambda layer, sub, mods: (norm_w[layer, sub][None, :], _mod_pair(mods, 3 * sub, 3 * sub + 1, nb))

    w0, mod0 = norm(0, 0, mods0)
    u = _rmsmod(xs, w0, mod0, tm=_row_tile(t), rows=t, n_lat=n_lat)
    xs, u = _even_layer(xs, u, mods0, ev_w_in, ev_conv_w[0], ev_conv_b[0], ssd_dt_bias[0], ssd_a_log[0],
                        ssd_d[0], ssd_norm_w[0], hy_w1[0], hy_b1[0], hy_w2[0], hy_b2[0], hy_w3[0], hy_freq[0],
                        hy_bias[0], ev_w_out.astype(BF16), 0, n_lat=n_lat, n_ctx=n_ctx, next_norm=norm(0, 1, mods0))
    xs, u = _conv_ffn_block(xs, u, mods0, ffn_w_up, ffn_conv_w[0], ffn_conv_b[0], w_down, 0,
                            rows=t, n_lat=n_lat, segs=both, next_norm=norm(1, 0, mods1))
    xl, u = _odd_layer_latent(xs, u, mods1, od_w_in, ml_conv_w[0], ml_conv_b[0], ml_gate_b[0], ml_norm_w[0],
                              na_q_norm_w[0], na_k_norm_w[0], na_rpb[0], od_w_out.astype(BF16), 0,
                              n_lat=n_lat, n_ctx=n_ctx, next_norm=norm(1, 1, mods1))
    out, _ = _conv_ffn_block(xl, u, mods1, ffn_w_up, ffn_conv_w[1], ffn_conv_b[1], w_down, 1,
                             rows=n_lat, n_lat=n_lat, segs=((0, n_lat),), next_norm=None)
    return out
```

```python
import functools
import math

import numpy as np
import jax
import jax.numpy as jnp
from jax import lax
from jax.experimental import pallas as pl
from jax.experimental.pallas import tpu as pltpu

F32 = jnp.float32
BF16 = jnp.bfloat16
HI = lax.Precision.HIGHEST

D_MODEL = 2048
GRID_W = 64
EPS = 1e-6
CONV_W = 3
SSD_WIDTH = D_MODEL
SSD_HEAD_DIM = 64
SSD_HEADS = SSD_WIDTH // SSD_HEAD_DIM
SSD_STATE = 128
SSD_GROUPS = 4
SSD_HPG = SSD_HEADS // SSD_GROUPS
SSD_GW = SSD_WIDTH // SSD_GROUPS
SSD_XBC = SSD_WIDTH + 2 * SSD_GROUPS * SSD_STATE
HY_WIDTH = D_MODEL
HY_EMB = 33
HY_BANDS = (HY_EMB - 1) // 2
HY_ORDER = 64
HY_FAST = 0.3
HY_SLOW = 1.5
HY_TARGET = 1e-2
ML_HEADS = 8
ML_WIDTH = D_MODEL
ML_V_DIM = ML_WIDTH // ML_HEADS
ML_QK_DIM = ML_V_DIM // 2
ML_QK_WIDTH = ML_HEADS * ML_QK_DIM
ROPE_THETA = 10000.0
NA_WIDTH = D_MODEL
NA_HEAD_DIM = 128
NA_HEADS = NA_WIDTH // NA_HEAD_DIM
NA_ROWS = 8
NA_COLS = 16
NA_QROWS = 8
NA_KROWS = 16
NA_PARTS = 2
NA_HPS = 2
NA_PREP_STEPS = 8
NA_TAB_ROWS = 4
NA_PART_KROWS = 12
EV_CONV_CH = SSD_XBC + 3 * HY_WIDTH
SCAN_Q = 128
SCAN_CPS = 2
LANES = 128
SUBLANES = 8
MIB = 1024 * 1024
V7X_VMEM_BYTES = 64 * MIB
VMEM_LIMIT = 48 * MIB
VMEM_LIMIT_BIG = V7X_VMEM_BYTES - 6 * MIB


def _cp(sem, vmem=VMEM_LIMIT):
    return pltpu.CompilerParams(dimension_semantics=sem, vmem_limit_bytes=vmem)


def _sigmoid(x):
    return 0.5 * jnp.tanh(0.5 * x) + 0.5


def _silu(x):
    return x * _sigmoid(x)


def _softplus(x):
    return jnp.maximum(x, 0.0) + jnp.log(1.0 + jnp.exp(-jnp.abs(x)))


def _log_sigmoid(x):
    return -_softplus(-x)


def _dot(a, b, precision=None):
    return jnp.dot(a, b, preferred_element_type=F32, precision=precision)


def _dot_nt(a, b):
    return lax.dot_general(a, b, (((1,), (1,)), ((), ())), preferred_element_type=F32)


def _split_bf16(x, terms):
    out = []
    for _ in range(terms):
        part = x.astype(BF16)
        out.append(part)
        x = x - part.astype(F32)
    return out


def _dot_sel(x, sel, terms):
    selb = sel.astype(BF16)
    return sum(_dot(part, selb) for part in _split_bf16(x, terms))


def _sel_dot(sel, x, terms):
    selb = sel.astype(BF16)
    return sum(_dot(selb, part) for part in _split_bf16(x, terms))


def _mm_body(a_ref, w_ref, o_ref, *scratch, nk, cache_w, w_nk):
    if cache_w:
        wb_ref = scratch[0]

        @pl.when((pl.program_id(1) == 0) & (pl.program_id(2) == 0))
        def _():
            wb_ref[...] = (w_ref[...].T if w_nk else w_ref[...]).astype(BF16)

        w_val = wb_ref[...]
    else:
        w_val = w_ref[...].astype(BF16)
    part = _dot(a_ref[...].astype(BF16), w_val)
    if nk == 1:
        o_ref[...] = part.astype(o_ref.dtype)
        return
    acc_ref = scratch[0]
    k = pl.program_id(3)

    @pl.when(k == 0)
    def _():
        acc_ref[...] = part

    @pl.when(k > 0)
    def _():
        acc_ref[...] += part

    @pl.when(k == nk - 1)
    def _():
        o_ref[...] = acc_ref[...].astype(o_ref.dtype)


def _matmul(a, w, *, tm, tn, tk, out_dtype, rows=None, cols=None, w_index=None, w_nk=False, vmem=VMEM_LIMIT,
            name="mm"):
    ba, ta, kdim = a.shape
    bw, kdim2, n = (w.shape[0], w.shape[2], w.shape[1]) if w_nk else w.shape
    assert kdim == kdim2
    nb = ba if w_index is not None else max(ba, bw)
    rows = ta if rows is None else rows
    n = n if cols is None else cols
    assert rows % tm == 0 and n % tn == 0 and kdim % tk == 0
    ni, nj, nk = rows // tm, n // tn, kdim // tk
    a_map = (lambda j, b, i, k: (b, i, k)) if ba > 1 else (lambda j, b, i, k: (0, i, k))
    if w_index is not None:
        w_map = lambda j, b, i, k: (w_index, k, j)
    else:
        w_map = (lambda j, b, i, k: (b, k, j)) if bw > 1 else (lambda j, b, i, k: (0, k, j))
    cache_w = w.dtype != BF16 and nk == 1 and (w_index is not None or bw == 1)
    assert cache_w or not w_nk
    scratch = [pltpu.VMEM((tm, tn), F32)] if nk > 1 else []
    if cache_w:
        scratch = [pltpu.VMEM((tk, tn), BF16)]
    inner = "arbitrary" if cache_w else "parallel"
    if w_nk:
        w_spec = pl.BlockSpec((None, tn, tk), lambda j, b, i, k: (w_map(j, b, i, k)[0], j, k))
    else:
        w_spec = pl.BlockSpec((None, tk, tn), w_map)
    return pl.pallas_call(
        functools.partial(_mm_body, nk=nk, cache_w=cache_w, w_nk=w_nk),
        out_shape=jax.ShapeDtypeStruct((nb, rows, n), out_dtype), grid=(nj, nb, ni, nk),
        in_specs=[pl.BlockSpec((None, tm, tk), a_map), w_spec],
        out_specs=pl.BlockSpec((None, tm, tn), lambda j, b, i, k: (b, i, j)),
        scratch_shapes=scratch, name=name,
        compiler_params=_cp(("parallel", inner, inner, "arbitrary"), vmem))(a, w)


def _row_is_lat(row_blk, tm, n_lat):
    rows = row_blk * tm + lax.broadcasted_iota(jnp.int32, (tm, 1), 0)
    return rows < n_lat


def _gated_residual_epilogue(acc, row_blk, res_ref, gate_ref, *, tm, n_lat):
    gate = jnp.where(_row_is_lat(row_blk, tm, n_lat), gate_ref[0], gate_ref[1])
    return res_ref[...] + gate * acc


def _gated_residual_rows_body(*refs, ks, tm, n_lat):
    n_a = len(ks)
    a_refs, w_ref, res_ref, gate_ref, o_ref = refs[:n_a], refs[n_a], refs[n_a + 1], refs[n_a + 2], refs[n_a + 3]
    acc, lo = None, 0
    for a_ref, kk in zip(a_refs, ks):
        part = _dot(a_ref[...], w_ref[lo:lo + kk, :])
        acc = part if acc is None else acc + part
        lo += kk
    o_ref[...] = _gated_residual_epilogue(acc, pl.program_id(1), res_ref, gate_ref, tm=tm, n_lat=n_lat)


def _matmul_gated_residual_rows(a_list, w, res, gate, *, w_index, tm, tn, rows, n_lat, name):
    nb = a_list[0].shape[0]
    _, kdim, n = w.shape
    ks = tuple(x.shape[2] for x in a_list)
    assert sum(ks) == kdim and rows % tm == 0 and n % tn == 0
    in_specs = [pl.BlockSpec((None, tm, kk), lambda b, i, j: (b, i, 0)) for kk in ks]
    in_specs += [pl.BlockSpec((None, kdim, tn), lambda b, i, j: (w_index, 0, j)),
                 pl.BlockSpec((None, tm, tn), lambda b, i, j: (b, i, j)),
                 pl.BlockSpec((None, 2, 1, tn), lambda b, i, j: (b, 0, 0, j))]
    return pl.pallas_call(
        functools.partial(_gated_residual_rows_body, ks=ks, tm=tm, n_lat=n_lat),
        out_shape=jax.ShapeDtypeStruct((nb, rows, n), F32), grid=(nb, rows // tm, n // tn), in_specs=in_specs,
        out_specs=pl.BlockSpec((None, tm, tn), lambda b, i, j: (b, i, j)), name=name,
        compiler_params=_cp(("parallel", "parallel", "parallel"), VMEM_LIMIT_BIG))(*a_list, w, res, gate)


def _modulated_rmsnorm(x, is_lat, w_ref, mod_ref):
    y = x * lax.rsqrt(jnp.mean(x * x, axis=-1, keepdims=True) + EPS) * w_ref[...]
    shift = jnp.where(is_lat, mod_ref[0, 0], mod_ref[0, 1])
    scale = jnp.where(is_lat, mod_ref[1, 0], mod_ref[1, 1])
    return y * (1.0 + scale) + shift


def _rmsmod_body(x_ref, w_ref, mod_ref, o_ref, *, tm, n_lat):
    is_lat = _row_is_lat(pl.program_id(2), tm, n_lat)
    o_ref[...] = _modulated_rmsnorm(x_ref[...], is_lat, w_ref, mod_ref).astype(o_ref.dtype)


def _rmsmod(x, w, mod, *, tm, rows, n_lat):
    nb, _, d = x.shape
    tm = tm // 2
    assert rows % tm == 0 and tm % 16 == 0
    body = functools.partial(_rmsmod_body, tm=tm, n_lat=n_lat)
    return pl.pallas_call(
        body, out_shape=jax.ShapeDtypeStruct((nb, rows, d), BF16), grid=(1, nb, rows // tm),
        in_specs=[pl.BlockSpec((None, tm, d), lambda j, b, i: (b, i, 0)),
                  pl.BlockSpec((1, d), lambda j, b, i: (0, 0)),
                  pl.BlockSpec((None, 2, 2, 1, d), lambda j, b, i: (b, 0, 0, 0, 0))],
        out_specs=pl.BlockSpec((None, tm, d), lambda j, b, i: (b, i, 0)),
        name="rmsmod", compiler_params=_cp(("parallel", "parallel", "parallel")))(x, w, mod)


CONV_CH = 128
CONV_HALO = 16
CONV_STEPS = 4


def _shift_matrices():
    r = lax.broadcasted_iota(jnp.int32, (CONV_CH, CONV_CH + 2 * CONV_HALO), 0)
    c = lax.broadcasted_iota(jnp.int32, (CONV_CH, CONV_CH + 2 * CONV_HALO), 1)
    return (c == r + CONV_HALO - 1).astype(BF16), (c == r + CONV_HALO + 1).astype(BF16)


def _chunk_rows(r0, i):
    return pl.ds(r0 + pl.multiple_of(i * CONV_CH, CONV_CH), CONV_CH)


def _taps(w_ref, b_ref, scale=1.0):
    return w_ref[...] * scale, b_ref[...] * scale


def _silu_of_half(h):
    return h + h * jnp.tanh(h)


def _conv_at(x_ref, taps, shifts, r0, n, steps):
    w, b = taps
    cur = [x_ref[_chunk_rows(r0, i), :] for i in steps]
    zero = jnp.zeros((CONV_HALO, cur[0].shape[1]), cur[0].dtype)
    lo = [pl.multiple_of(jnp.maximum(i * CONV_CH - CONV_HALO, 0), CONV_HALO) for i in steps]
    hi = [pl.multiple_of(jnp.minimum((i + 1) * CONV_CH, n - CONV_HALO), CONV_HALO) for i in steps]
    tail = [jnp.where(i > 0, x_ref[pl.ds(r0 + l, CONV_HALO), :], zero) for i, l in zip(steps, lo)]
    head = [jnp.where(i < n // CONV_CH - 1, x_ref[pl.ds(r0 + h, CONV_HALO), :], zero) for i, h in zip(steps, hi)]
    win = [jnp.concatenate([t, c, h], axis=0) for t, c, h in zip(tail, cur, head)]
    prev = [_dot(shifts[0], w) for w in win]
    nxt = [_dot(shifts[1], w) for w in win]
    return [w[0:1, :] * p + w[1:2, :] * c.astype(F32) + w[2:3, :] * x + b for p, c, x in zip(prev, cur, nxt)]


def _conv_loop(segs, fn):
    for si, (r0, n) in enumerate(segs):
        def body(i, carry, r0=r0, n=n, si=si):
            fn(si, r0, n, [i])
            return carry

        lax.fori_loop(0, n // CONV_CH, body, 0, unroll=min(CONV_STEPS, n // CONV_CH))


def _conv_silu_body(x_ref, w_ref, b_ref, o_ref, *, segs):
    shifts = _shift_matrices()
    half_taps = _taps(w_ref, b_ref, 0.5)

    def fn(si, r0, n, steps):
        ys = [_silu_of_half(h) for h in _conv_at(x_ref, half_taps, shifts, r0, n, steps)]
        for i, y in zip(steps, ys):
            o_ref[_chunk_rows(r0, i), :] = y.astype(o_ref.dtype)

    _conv_loop(segs, fn)


def _conv_rope_body(x_ref, w_ref, b_ref, cos_ref, sin_ref, o_ref, *, segs, tc):
    shifts = _shift_matrices()
    lane = lax.broadcasted_iota(jnp.int32, (CONV_CH, tc), 1)
    first = (lane % (ML_QK_DIM // 2)) < (ML_QK_DIM // 4)
    nf = ML_QK_DIM // 4
    half_taps = _taps(w_ref, b_ref, 0.5)

    def fn(si, r0, n, steps):
        ys = [_silu_of_half(h) for h in _conv_at(x_ref, half_taps, shifts, r0, n, steps)]
        partner = [jnp.where(first, pltpu.roll(y, tc - nf, 1), pltpu.roll(y, nf, 1)) for y in ys]
        for i, y, p in zip(steps, ys, partner):
            rows = _chunk_rows(r0, i)
            o_ref[rows, :] = (y * cos_ref[rows, :] + p * sin_ref[rows, :]).astype(o_ref.dtype)

    _conv_loop(segs, fn)


def _conv_ffn_body(a_ref, x_ref, w_ref, b_ref, o_ref, *, segs):
    shifts = _shift_matrices()
    half_taps = _taps(w_ref, b_ref, 0.5)

    def fn(si, r0, n, steps):
        gs = [_silu_of_half(h) for h in _conv_at(x_ref, half_taps, shifts, r0, n, steps)]
        for i, g in zip(steps, gs):
            rows = _chunk_rows(r0, i)
            o_ref[rows, :] = (a_ref[rows, :].astype(F32) * g).astype(o_ref.dtype)

    _conv_loop(segs, fn)


def _conv_hyena_body(x0_ref, x1_ref, v_ref, w0_ref, b0_ref, w1_ref, b1_ref, w2_ref, b2_ref,
                     ox0_ref, ozl_ref, ozc_ref, ozle_ref, ozlo_ref, ozce_ref, ozco_ref, *zs_refs, segs):
    shifts = _shift_matrices()
    taps = (_taps(w0_ref, b0_ref), _taps(w1_ref, b1_ref), _taps(w2_ref, b2_ref))
    half = CONV_CH // 2
    outs = ((ozl_ref, ozle_ref, ozlo_ref), (ozc_ref, ozce_ref, ozco_ref))

    def fn(si, r0, n, steps):
        x0 = _conv_at(x0_ref, taps[0], shifts, r0, n, steps)
        x1 = _conv_at(x1_ref, taps[1], shifts, r0, n, steps)
        v = _conv_at(v_ref, taps[2], shifts, r0, n, steps)
        nat_ref, even_ref, odd_ref = outs[si]
        for i, x0_i, x1_i, v_i in zip(steps, x0, x1, v):
            ox0_ref[_chunk_rows(r0, i), :] = x0_i.astype(ox0_ref.dtype)
            zin = x1_i * v_i
            nat_ref[_chunk_rows(0, i), :] = zin.astype(BF16)
            t0 = r0 + pl.multiple_of(i * CONV_CH, CONV_CH)
            hrows = pl.ds(pl.multiple_of(i * half, half), half)
            for g, zs_ref in enumerate(zs_refs):
                lanes = slice(g * LANES, (g + 1) * LANES)
                zs_ref[pl.ds(t0, CONV_CH), :] = zin[:, lanes]
                even_ref[hrows, lanes] = zs_ref[pl.ds(t0, half, stride=2), :].astype(BF16)
                odd_ref[hrows, lanes] = zs_ref[pl.ds(t0 + 1, half, stride=2), :].astype(BF16)

    _conv_loop(segs, fn)


def _conv_specs(t_in, tc, xcol_blk, wcol_blk):
    return [pl.BlockSpec((None, t_in, tc), lambda b, j: (b, 0, j + xcol_blk)),
            pl.BlockSpec((CONV_W, tc), lambda b, j: (0, j + wcol_blk)),
            pl.BlockSpec((1, tc), lambda b, j: (0, j + wcol_blk))]


def _conv_silu(x, w, b, *, xcol, wcol, width, segs, tc=256):
    nb, t_in, _ = x.shape
    return pl.pallas_call(
        functools.partial(_conv_silu_body, segs=segs),
        out_shape=jax.ShapeDtypeStruct((nb, t_in, width), BF16), grid=(nb, width // tc),
        in_specs=_conv_specs(t_in, tc, xcol // tc, wcol // tc),
        out_specs=pl.BlockSpec((None, t_in, tc), lambda b, j: (b, 0, j)),
        name="conv_silu", compiler_params=_cp(("parallel", "parallel")))(x, w, b)


def _conv_rope(x, w, b, cos, sin, *, width, segs, tc=256):
    nb, t_in, _ = x.shape
    tab = pl.BlockSpec((t_in, tc), lambda b, j: (0, 0))
    return pl.pallas_call(
        functools.partial(_conv_rope_body, segs=segs, tc=tc),
        out_shape=jax.ShapeDtypeStruct((nb, t_in, width), BF16), grid=(nb, width // tc),
        in_specs=_conv_specs(t_in, tc, 0, 0) + [tab, tab],
        out_specs=pl.BlockSpec((None, t_in, tc), lambda b, j: (b, 0, j)),
        name="conv_rope", compiler_params=_cp(("parallel", "parallel")))(x, w, b, cos, sin)


def _conv_ffn(up, w, b, *, segs, rows, tc=256):
    nb, t_in, two_ff = up.shape
    ff = two_ff // 2
    return pl.pallas_call(
        functools.partial(_conv_ffn_body, segs=segs),
        out_shape=jax.ShapeDtypeStruct((nb, rows, ff), BF16), grid=(nb, ff // tc),
        in_specs=[pl.BlockSpec((None, rows, tc), lambda b, j: (b, 0, j))] + _conv_specs(rows, tc, ff // tc, 0),
        out_specs=pl.BlockSpec((None, rows, tc), lambda b, j: (b, 0, j)),
        name="conv_ffn", compiler_params=_cp(("parallel", "parallel")))(up, up, w, b)


def _conv_hyena(pr, w, b, *, xcol, wcol, segs, tc=256):
    nb, t_in, _ = pr.shape
    (_, n_lat), (_, n_ctx) = segs
    wd = HY_WIDTH
    out_rows = (t_in, n_lat, n_ctx, n_lat // 2, n_lat // 2, n_ctx // 2, n_ctx // 2)
    in_specs = [pl.BlockSpec((None, t_in, tc), functools.partial(lambda b, j, o: (b, 0, j + o), o=(xcol + m * wd) // tc))
                for m in range(3)]
    wargs = []
    for m in range(3):
        off = (wcol + m * wd) // tc
        in_specs.append(pl.BlockSpec((CONV_W, tc), functools.partial(lambda b, j, o: (0, j + o), o=off)))
        in_specs.append(pl.BlockSpec((1, tc), functools.partial(lambda b, j, o: (0, j + o), o=off)))
        wargs += [w, b]
    return pl.pallas_call(
        functools.partial(_conv_hyena_body, segs=segs),
        out_shape=tuple(jax.ShapeDtypeStruct((nb, rows, wd), BF16) for rows in out_rows),
        grid=(nb, wd // tc), in_specs=in_specs,
        out_specs=tuple(pl.BlockSpec((None, rows, tc), lambda b, j: (b, 0, j)) for rows in out_rows),
        scratch_shapes=[pltpu.VMEM((t_in, LANES), F32)] * (tc // LANES),
        name="conv_hyena", compiler_params=_cp(("parallel", "parallel")))(pr, pr, pr, *wargs)


def _scan_masks(reverse):
    q = SCAN_Q
    ii = lax.broadcasted_iota(jnp.int32, (q, q), 0)
    jj = lax.broadcasted_iota(jnp.int32, (q, q), 1)
    mask = (jj >= ii) if reverse else (jj <= ii)
    mask_t = (jj <= ii) if reverse else (jj >= ii)
    return mask, mask.astype(F32), mask_t.astype(F32)


def _chunk_map(reverse, n_chunks, n_lat_chunks):
    if reverse:
        return lambda k: n_chunks - 1 - k
    return lambda k: (k + n_lat_chunks) % n_chunks


SSD_GPS = 4


def _ssd_body(*refs, reverse, final):
    (xs_ref, b_ref, c_ref, dtc_ref, dtr_ref, dbc_ref, dbr_ref, alc_ref, alr_ref) = refs[:9]
    if final:
        yf_ref, z_ref, dsk_ref, nw_ref, o_ref, h_ref = refs[9:]
    else:
        o_ref, h_ref = refs[9:]

    @pl.when(pl.program_id(2) == 0)
    def _():
        h_ref[...] = jnp.zeros_like(h_ref)

    groups = range(SSD_GPS)
    h = [h_ref[g] for g in groups]
    for c in (reversed(range(SCAN_CPS)) if reverse else range(SCAN_CPS)):
        h = _ssd_chunk(refs, slice(c * SCAN_Q, (c + 1) * SCAN_Q), h, reverse=reverse, final=final)
    for g in groups:
        h_ref[g] = h[g]


def _ssd_chunk(refs, rs, h_prev, *, reverse, final):
    (xs_ref, b_ref, c_ref, dtc_ref, dtr_ref, dbc_ref, dbr_ref, alc_ref, alr_ref) = refs[:9]
    if final:
        yf_ref, z_ref, dsk_ref, nw_ref, o_ref = refs[9:14]
    else:
        o_ref = refs[9]
    q, gw, hd = SCAN_Q, SSD_GW, SSD_HEAD_DIM
    mask, maskf, mask_tf = _scan_masks(reverse)
    groups = range(SSD_GPS)
    pairs = range(SSD_HPG // 2)
    cols = [slice(g * gw, (g + 1) * gw) for g in groups]
    xs = [xs_ref[rs, cols[g]].astype(F32) for g in groups]
    bm = [b_ref[rs, g * SSD_STATE:(g + 1) * SSD_STATE] for g in groups]
    cm = [c_ref[rs, g * SSD_STATE:(g + 1) * SSD_STATE] for g in groups]
    c0 = SSD_HEADS if reverse else 0
    dt_c = [_softplus(dtc_ref[rs, c0 + g * SSD_HPG:c0 + (g + 1) * SSD_HPG] + dbc_ref[g]) for g in groups]
    dt_r = [_softplus(dtr_ref[g, :, rs] + dbr_ref[g]) for g in groups]
    da_c = [dt_c[g] * (-jnp.exp(alc_ref[g])) for g in groups]
    da_r = [dt_r[g] * (-jnp.exp(alr_ref[g])) for g in groups]
    acum_c_all = _sel_dot(maskf, jnp.concatenate(da_c, axis=1), 3)
    acum_r_all = _dot_sel(jnp.concatenate(da_r, axis=0), mask_tf, 3)
    acum_c = [acum_c_all[:, g * SSD_HPG:(g + 1) * SSD_HPG] for g in groups]
    acum_r = [acum_r_all[g * SSD_HPG:(g + 1) * SSD_HPG, :] for g in groups]
    atot = [jnp.sum(da_c[g], axis=0, keepdims=True) for g in groups]

    er = lax.broadcasted_iota(jnp.int32, (SSD_HPG, gw), 0)
    ec = lax.broadcasted_iota(jnp.int32, (SSD_HPG, gw), 1)
    expand = (ec // hd == er).astype(F32)
    wide = [_dot_sel(jnp.concatenate([dt_c[g], jnp.exp(atot[g] - acum_c[g]), jnp.exp(acum_c[g]),
                                      jnp.broadcast_to(jnp.exp(atot[g]), (SUBLANES, SSD_HPG))], axis=0), expand, 1)
            for g in groups]
    xq = [xs[g] * wide[g][0:q] for g in groups]
    cb = [_dot_nt(cm[g], bm[g]) for g in groups]
    y_off = [wide[g][2 * q:3 * q] * _dot(cm[g], h_prev[g].astype(BF16)) for g in groups]
    lane = lax.broadcasted_iota(jnp.int32, (q, 2 * hd), 1)

    def pair_lhs(g, p):
        ms = [(cb[g] * jnp.where(mask, jnp.exp(acum_c[g][:, r:r + 1] - acum_r[g][r:r + 1, :]), 0.0)).astype(BF16)
              for r in (2 * p, 2 * p + 1)]
        return jnp.concatenate(ms, axis=1)

    def pair_rhs(g, p):
        xp = xq[g][:, 2 * hd * p:2 * hd * (p + 1)]
        return jnp.concatenate([jnp.where(lane < hd, xp, 0.0), jnp.where(lane >= hd, xp, 0.0)], axis=0).astype(BF16)

    y_in = [[_dot(pair_lhs(g, p), pair_rhs(g, p)) for p in pairs] for g in groups]
    y = [y_off[g] + jnp.concatenate(y_in[g], axis=1) for g in groups]
    bt = [bm[g].astype(F32).T.astype(BF16) for g in groups]
    upd = [_dot(bt[g], (xq[g] * wide[g][q:2 * q]).astype(BF16)) for g in groups]
    for g in groups:
        if final:
            ytot = y[g] + yf_ref[rs, cols[g]] + xs[g] * dsk_ref[:, cols[g]]
            gated = ytot * _silu(z_ref[rs, cols[g]].astype(F32))
            o = gated * lax.rsqrt(jnp.mean(gated * gated, axis=-1, keepdims=True) + EPS) * nw_ref[:, cols[g]]
            o_ref[rs, cols[g]] = o.astype(o_ref.dtype)
        else:
            o_ref[rs, cols[g]] = y[g]
    return [wide[g][3 * q:3 * q + 1] * h_prev[g] + upd[g] for g in groups]


def _ssd(xbc, dt_c, dt_r, db_c, db_r, al_c, al_r, *, direction, n_lat, final_args=None):
    nb, t, _ = xbc.shape
    q, gw, hpg = SCAN_CPS * SCAN_Q, SSD_GW, SSD_HPG
    assert n_lat % q == 0 and t % q == 0
    assert SSD_GPS == SSD_GROUPS
    nch = t // q
    reverse = direction == 1
    cmap = _chunk_map(reverse, nch, n_lat // q)
    gps = SSD_GPS
    sw, gww = gps * SSD_STATE, gps * gw
    b0 = SSD_WIDTH // sw
    c0 = b0 + SSD_GROUPS // gps
    d = direction
    in_specs = [
        pl.BlockSpec((None, q, gww), lambda b, g, k: (b, cmap(k), g)),
        pl.BlockSpec((None, q, sw), lambda b, g, k: (b, cmap(k), b0 + g)),
        pl.BlockSpec((None, q, sw), lambda b, g, k: (b, cmap(k), c0 + g)),
        pl.BlockSpec((None, q, 2 * SSD_HEADS), lambda b, g, k: (b, cmap(k), 0)),
        pl.BlockSpec((None, None, gps, hpg, q), lambda b, g, k: (b, d, g, 0, cmap(k))),
        pl.BlockSpec((None, gps, 1, hpg), lambda b, g, k: (d, g, 0, 0)),
        pl.BlockSpec((None, gps, hpg, 1), lambda b, g, k: (d, g, 0, 0)),
        pl.BlockSpec((None, gps, 1, hpg), lambda b, g, k: (d, g, 0, 0)),
        pl.BlockSpec((None, gps, hpg, 1), lambda b, g, k: (d, g, 0, 0)),
    ]
    args = [xbc, xbc, xbc, dt_c, dt_r, db_c, db_r, al_c, al_r]
    final = final_args is not None
    if final:
        yf, pr, dskip, norm_w = final_args
        in_specs += [pl.BlockSpec((None, q, gww), lambda b, g, k: (b, cmap(k), g)),
                     pl.BlockSpec((None, q, gww), lambda b, g, k: (b, cmap(k), g)),
                     pl.BlockSpec((1, gww), lambda b, g, k: (0, g)),
                     pl.BlockSpec((1, gww), lambda b, g, k: (0, g))]
        args += [yf, pr, dskip, norm_w]
    out_shape = jax.ShapeDtypeStruct((nb, t, SSD_WIDTH), BF16 if final else F32)
    return pl.pallas_call(
        functools.partial(_ssd_body, reverse=reverse, final=final),
        out_shape=out_shape, grid=(nb, SSD_GROUPS // gps, nch), in_specs=in_specs,
        out_specs=pl.BlockSpec((None, q, gww), lambda b, g, k: (b, cmap(k), g)),
        scratch_shapes=[pltpu.VMEM((gps, SSD_STATE, gw), F32)],
        name="ssd_bwd" if reverse else "ssd_fwd",
        compiler_params=_cp(("parallel", "parallel", "arbitrary")))(*args)


def _hyfilt_body(f_ref, w1_ref, b1_ref, w2_ref, b2_ref, w3_ref, fr_ref, dl_ref, o_ref, *, tl):
    f = f_ref[...]
    h = jnp.sin(fr_ref[0:1, :] * (_dot(f, w1_ref[...], HI) + b1_ref[...]))
    h = jnp.sin(fr_ref[1:2, :] * (_dot(h, w2_ref[...], HI) + b2_ref[...]))
    h = _dot(h, w3_ref[...], HI)
    dec = jnp.exp(-f[:, 0:1] * dl_ref[...])
    row = pl.program_id(0) * tl + lax.broadcasted_iota(jnp.int32, (tl, 1), 0)
    wd = HY_WIDTH
    o_ref[:, 0:wd] = (h[:, 0:wd] * dec).astype(o_ref.dtype)
    o_ref[:, wd:2 * wd] = jnp.where(row == 0, 0.0, h[:, wd:2 * wd] * dec).astype(o_ref.dtype)


def _hyena_filters(length, w1, b1, w2, b2, w3, freq, *, tl=256):
    p = LANES
    t = jnp.linspace(0.0, 1.0, length, dtype=F32)[:, None]
    w = 2.0 * math.pi * jnp.arange(length, dtype=F32)[:, None] / length
    f = jnp.linspace(1e-4, HY_BANDS - 1, HY_BANDS, dtype=F32)[None, :]
    feats = jnp.concatenate([t, jnp.cos(f * w), -jnp.sin(f * w)], axis=-1)
    feats = jnp.concatenate([feats[0::2], feats[1::2]], axis=0)
    feats = jnp.pad(feats, ((0, 0), (0, p - HY_EMB)))
    po = p - HY_ORDER
    w1p = jnp.pad(w1, ((0, p - HY_EMB), (0, po)))
    w2p = jnp.pad(w2, ((0, po), (0, po)))
    w3p = jnp.pad(w3, ((0, po), (0, 0)))
    b1p = jnp.pad(b1[None, :], ((0, 0), (0, po)))
    b2p = jnp.pad(b2[None, :], ((0, 0), (0, po)))
    frp = jnp.pad(freq, ((0, 0), (0, po)))
    deltas = jnp.abs(jnp.linspace(math.log(HY_TARGET) / HY_FAST, math.log(HY_TARGET) / HY_SLOW, HY_WIDTH,
                                  dtype=F32))[None, :]
    full = lambda shape: pl.BlockSpec(shape, lambda i: (0,) * len(shape))
    out = pl.pallas_call(
        functools.partial(_hyfilt_body, tl=tl),
        out_shape=jax.ShapeDtypeStruct((length, 2 * HY_WIDTH), BF16), grid=(length // tl,),
        in_specs=[pl.BlockSpec((tl, p), lambda i: (i, 0)), full((p, p)), full((1, p)), full((p, p)), full((1, p)),
                  full((p, 2 * HY_WIDTH)), full((2, p)), full((1, HY_WIDTH))],
        out_specs=pl.BlockSpec((tl, 2 * HY_WIDTH), lambda i: (i, 0)),
        name="hyena_filters", compiler_params=_cp(("parallel",)))(feats, w1p, b1p, w2p, b2p, w3p, frp, deltas)
    return out[None]


def _dft_matrices(length):
    n = 2 * length
    k = jnp.arange(length, dtype=jnp.int32)[:, None]
    t = jnp.arange(length, dtype=jnp.int32)[None, :]
    step = 1 << (int(math.log2(length)) // 2)
    part = jnp.arange(length // step, dtype=jnp.int32)[:, None]
    ang_hi = ((part * step * t) % n).astype(F32) * (2.0 * math.pi / n)
    ang_lo = ((jnp.arange(step, dtype=jnp.int32)[:, None] * t) % n).astype(F32) * (2.0 * math.pi / n)
    ch, sh = jnp.cos(ang_hi)[:, None, :], jnp.sin(ang_hi)[:, None, :]
    cl, sl = jnp.cos(ang_lo)[None, :, :], jnp.sin(ang_lo)[None, :, :]
    c = (ch * cl - sh * sl).reshape(length, length)
    s = (sh * cl + ch * sl).reshape(length, length)
    alt_t = jnp.where(t % 2 == 0, 1.0, -1.0).astype(F32)
    fwd = jnp.concatenate([c, jnp.where(k == 0, alt_t, -s)], axis=0)
    alt_r = jnp.where(k % 2 == 0, 1.0, -1.0).astype(F32)
    inv_re = jnp.where(t == 0, 1.0 / n, (2.0 / n) * c)
    inv_im = jnp.where(t == 0, alt_r / n, (-2.0 / n) * s)
    inv = jnp.concatenate([inv_re, inv_im], axis=1)
    return fwd.astype(BF16)[None], inv.astype(BF16)[None]


def _tile_interleave(x, axis, hm):
    shp = x.shape
    n = shp[axis] // 2
    x = x.reshape(shp[:axis] + (2, n // hm, hm) + shp[axis + 1:])
    return jnp.swapaxes(x, axis, axis + 1).reshape(shp)


def _twiddle_table(length, hm):
    k = np.arange(length // 2, dtype=np.float64)[:, None] * (np.pi / length)
    tab = np.concatenate([np.cos(k), np.sin(k)], axis=0)
    tab = tab.reshape(2, -1, hm, 1).swapaxes(0, 1).reshape(length, 1)
    return jnp.asarray(np.broadcast_to(tab, (length, LANES)).astype(np.float32))


def _lanes(x, width):
    return jnp.tile(x, (1, width // LANES))


def _parity_banks(e, ot, tw, hm):
    er, ei, otr, oti = e[0:hm], e[hm:], ot[0:hm], ot[hm:]
    c, s = _lanes(tw[0:hm], e.shape[1]), _lanes(tw[hm:], e.shape[1])
    orr = otr * c + oti * s
    oi = oti * c - otr * s
    return (er + orr, ei + oi, er - orr, ei - oi), (c, s)


def _is_bin0(row_blk, hm):
    return (row_blk == 0) & (lax.broadcasted_iota(jnp.int32, (hm, 1), 0) == 0)


def _hy_kbank_body(f_ref, hef_ref, heb_ref, hof_ref, hob_ref, tw_ref, ka_ref, kb_ref, *, hm):
    fwd, tw = f_ref[...], tw_ref[...]
    ef, eb, of, ob = (_dot(fwd, r[...]) for r in (hef_ref, heb_ref, hof_ref, hob_ref))
    (afr, afi, bfr, bfi), _ = _parity_banks(ef, of, tw, hm)
    (abr, abi, bbr, bbi), _ = _parity_banks(eb, ob, tw, hm)
    first = _is_bin0(pl.program_id(0), hm)
    kai = jnp.where(first, ef[hm:] + eb[hm:], afi - abi)
    kbi = jnp.where(first, ob[hm:] - of[hm:], bfi - bbi)
    ka_ref[...] = jnp.concatenate([afr + abr, kai], axis=0)
    kb_ref[...] = jnp.concatenate([bfr + bbr, kbi], axis=0)


def _hy_kbank(fwd, hfb, tw, *, tm, tc=512):
    _, ln, c2 = hfb.shape
    c, h = c2 // 2, ln // 2
    taps = lambda row_blk, col_off: pl.BlockSpec((None, h, tc), lambda i, j: (0, row_blk, j + col_off))
    bank = pl.BlockSpec((None, tm, tc), lambda i, j: (0, i, j))
    return pl.pallas_call(
        functools.partial(_hy_kbank_body, hm=tm // 2),
        out_shape=(jax.ShapeDtypeStruct((1, ln, c), F32),) * 2, grid=(ln // tm, c // tc),
        in_specs=[pl.BlockSpec((None, tm, h), lambda i, j: (0, i, 0)),
                  taps(0, 0), taps(0, c // tc), taps(1, 0), taps(1, c // tc),
                  pl.BlockSpec((tm, LANES), lambda i, j: (i, 0))],
        out_specs=(bank, bank), name="hy_kbank",
        compiler_params=_cp(("parallel", "parallel")))(fwd, hfb, hfb, hfb, hfb, tw)


def _hy_spectrum_body(f_ref, ze_ref, zo_ref, tw_ref, ka_ref, kb_ref, p_ref, q_ref, *, hm):
    fwd = f_ref[...]
    e, acc = _dot(fwd, ze_ref[...]), _dot(fwd, zo_ref[...])
    row_blk = pl.program_id(2)
    (ar, ai, br, bi), (c, s) = _parity_banks(e, acc, tw_ref[...], hm)
    kar, kai, kbr, kbi = ka_ref[0:hm], ka_ref[hm:], kb_ref[0:hm], kb_ref[hm:]
    yar, yai = ar * kar - ai * kai, ar * kai + ai * kar
    ybr, ybi = br * kbr - bi * kbi, br * kbi + bi * kbr
    qr, qi = yar - ybr, yai - ybi
    pr, pi = yar + ybr, yai + ybi
    qpr, qpi = qr * c - qi * s, qi * c + qr * s
    top = slice(0, SUBLANES)
    first = _is_bin0(row_blk, SUBLANES)
    y0, yl = ar[top] * kar[top], br[top] * kbr[top]
    amr, ami = e[hm:hm + SUBLANES], -acc[hm:hm + SUBLANES]
    ymr, ymi = amr * kai[top] - ami * kbi[top], amr * kbi[top] + ami * kai[top]

    def patch(full, special):
        return jnp.concatenate([jnp.where(first, special, full[top]), full[SUBLANES:]], axis=0)

    pr, pi = patch(pr, y0 + yl), patch(pi, 2.0 * ymr)
    qpr, qpi = patch(qpr, y0 - yl), patch(qpi, -2.0 * ymi)
    p_ref[...] = jnp.concatenate([pr, pi], axis=0).astype(p_ref.dtype)
    q_ref[...] = jnp.concatenate([qpr, qpi], axis=0).astype(q_ref.dtype)


def _hy_spectrum(fwd, ze, zo, tw, ka, kb, *, tm, tn):
    nb, h, c = ze.shape
    ln = fwd.shape[1]
    data = pl.BlockSpec((None, h, tn), lambda j, b, i: (b, 0, j))
    bank = pl.BlockSpec((None, tm, tn), lambda j, b, i: (0, i, j))
    out = pl.BlockSpec((None, tm, tn), lambda j, b, i: (b, i, j))
    return pl.pallas_call(
        functools.partial(_hy_spectrum_body, hm=tm // 2),
        out_shape=(jax.ShapeDtypeStruct((nb, ln, c), BF16),) * 2, grid=(c // tn, nb, ln // tm),
        in_specs=[pl.BlockSpec((None, tm, h), lambda j, b, i: (0, i, 0)), data, data,
                  pl.BlockSpec((tm, LANES), lambda j, b, i: (i, 0)), bank, bank],
        out_specs=(out, out), name="hy_spectrum",
        compiler_params=_cp(("parallel", "parallel", "parallel")))(fwd, ze, zo, tw, ka, kb)


def _hyena_out_body(*refs, segs):
    ns = len(segs)
    x0_ref, db_ref, o_ref = refs[3 * ns:3 * ns + 3]
    ys_refs = refs[3 * ns + 3:]
    half = CONV_CH // 2
    for si, (r0, n) in enumerate(segs):
        ye_ref, yo_ref, zin_ref = refs[3 * si:3 * si + 3]

        def body(i, carry, r0=r0, ye_ref=ye_ref, yo_ref=yo_ref, zin_ref=zin_ref):
            t0 = pl.multiple_of(i * CONV_CH, CONV_CH)
            hrows = pl.ds(pl.multiple_of(i * half, half), half)
            for g, ys_ref in enumerate(ys_refs):
                lanes = slice(g * LANES, (g + 1) * LANES)
                ys_ref[pl.ds(r0 + t0, half, stride=2), :] = ye_ref[hrows, lanes].astype(F32)
                ys_ref[pl.ds(r0 + t0 + 1, half, stride=2), :] = yo_ref[hrows, lanes].astype(F32)
            rows = pl.ds(r0 + t0, CONV_CH)
            y = jnp.concatenate([ys_ref[rows, :] for ys_ref in ys_refs], axis=1)
            zin = zin_ref[pl.ds(t0, CONV_CH), :].astype(F32)
            o_ref[rows, :] = (x0_ref[rows, :].astype(F32) * (y + zin * db_ref[...])).astype(o_ref.dtype)
            return carry

        lax.fori_loop(0, n // CONV_CH, body, 0, unroll=min(4, n // CONV_CH))


def _hyena_out(parts, x0, dbias, *, segs, tc=256):
    nb, t, c = x0.shape
    in_specs, args = [], []
    for (ye, yo, zin), (_, n) in zip(parts, segs):
        in_specs += [pl.BlockSpec((None, n // 2, tc), lambda b, j: (b, 0, j)),
                     pl.BlockSpec((None, n // 2, tc), lambda b, j: (b, 0, j)),
                     pl.BlockSpec((None, n, tc), lambda b, j: (b, 0, j))]
        args += [ye, yo, zin]
    in_specs += [pl.BlockSpec((None, t, tc), lambda b, j: (b, 0, j)), pl.BlockSpec((1, tc), lambda b, j: (0, j))]
    return pl.pallas_call(
        functools.partial(_hyena_out_body, segs=segs),
        out_shape=jax.ShapeDtypeStruct((nb, t, c), BF16), grid=(nb, c // tc), in_specs=in_specs,
        out_specs=pl.BlockSpec((None, t, tc), lambda b, j: (b, 0, j)),
        scratch_shapes=[pltpu.VMEM((t, LANES), F32)] * (tc // LANES), name="hyena_out",
        compiler_params=_cp(("parallel", "parallel")))(*args, x0, dbias)


def _hyena_longconv(ze, zo, hfb, *, tn=1024):
    nb, h, c = ze.shape
    ln = 2 * h
    tm_s = min(512, ln)
    hm = tm_s // 2
    fwd, inv = _dft_matrices(h)
    fwd = _tile_interleave(fwd, 1, hm)
    inv = _tile_interleave(inv * 0.5, 2, hm)
    tw = _twiddle_table(ln, hm)
    ka, kb = _hy_kbank(fwd, hfb, tw, tm=tm_s)
    p, q = _hy_spectrum(fwd, ze, zo, tw, ka, kb, tm=tm_s, tn=tn)
    tm_i, tk_i = min(1024, h), ln
    ye = _matmul(inv, p, tm=tm_i, tn=tn, tk=tk_i, out_dtype=BF16, name="hy_idft_even")
    yo = _matmul(inv, q, tm=tm_i, tn=tn, tk=tk_i, out_dtype=BF16, name="hy_idft_odd")
    return ye, yo


def _mlstm_body(*refs, reverse, final):
    q_ref, k_ref, v_ref, gc_ref, gr_ref = refs[:5]
    if final:
        hf_ref, og_ref, nw_ref, o_ref, ct_ref, n_ref, m_ref = refs[5:]
    else:
        o_ref, ct_ref, n_ref, m_ref = refs[5:]
    nh, dk, dv = ML_HEADS, ML_QK_DIM, ML_V_DIM
    scale = dk ** -0.5

    @pl.when(pl.program_id(1) == 0)
    def _():
        ct_ref[...] = jnp.zeros_like(ct_ref)
        n_ref[...] = jnp.zeros_like(n_ref)
        m_ref[...] = jnp.zeros_like(m_ref)

    state = ([ct_ref[h] for h in range(nh)], n_ref[...], m_ref[...])
    for c in (reversed(range(SCAN_CPS)) if reverse else range(SCAN_CPS)):
        state = _mlstm_chunk(refs, slice(c * SCAN_Q, (c + 1) * SCAN_Q), state, reverse=reverse, final=final)
    for h in range(nh):
        ct_ref[h] = state[0][h]
    n_ref[...] = state[1]
    m_ref[...] = state[2]


def _mlstm_chunk(refs, rs, state, *, reverse, final):
    q_ref, k_ref, v_ref, gc_ref, gr_ref = refs[:5]
    if final:
        hf_ref, og_ref, nw_ref, o_ref = refs[5:9]
    else:
        o_ref = refs[5]
    nh, dk, dv = ML_HEADS, ML_QK_DIM, ML_V_DIM
    scale = dk ** -0.5
    cts, n_all, m_all = state
    c0 = 2 * nh if reverse else 0
    gc, gr = gc_ref[rs, c0:c0 + 2 * nh], gr_ref[:, rs]
    li_c, lf_c = gc[:, 0:nh], _log_sigmoid(gc[:, nh:2 * nh])
    li_r, lf_r = gr[0:nh, :], _log_sigmoid(gr[nh:2 * nh, :])
    mask, maskf, mask_tf = _scan_masks(reverse)
    bc_c = _dot(maskf, lf_c, HI)
    bc_r = _dot(lf_r, mask_tf, HI)
    btot_c = jnp.sum(lf_c, axis=0, keepdims=True)
    heads = range(nh)
    qs = [q_ref[rs, dk * h:dk * (h + 1)] for h in heads]
    ks = [k_ref[rs, dk * h:dk * (h + 1)] for h in heads]
    vs = [v_ref[rs, dv * h:dv * (h + 1)] for h in heads]
    m_s = [m_all[h:h + 1, 0:1] for h in heads]
    n_s = [n_all[h:h + 1, :] for h in heads]
    qk = [_dot_nt(qs[h], ks[h]) for h in heads]
    qc = [_dot(qs[h], cts[h].astype(BF16)) for h in heads]
    dmat = [jnp.where(mask, bc_c[:, h:h + 1] - bc_r[h:h + 1, :] + li_r[h:h + 1, :], -jnp.inf) for h in heads]
    inter = [bc_c[:, h:h + 1] + m_s[h] for h in heads]
    m_t = [jnp.maximum(inter[h], jnp.max(dmat[h], axis=1, keepdims=True)) for h in heads]
    s = [qk[h] * scale * jnp.exp(dmat[h] - m_t[h]) for h in heads]
    dec = [jnp.exp(inter[h] - m_t[h]) for h in heads]
    num = [_dot(s[h].astype(BF16), vs[h]) + (dec[h] * scale) * qc[h] for h in heads]
    qn = [jnp.sum(qs[h].astype(F32) * n_s[h], axis=1, keepdims=True) * scale for h in heads]
    den = [jnp.sum(s[h], axis=1, keepdims=True) + dec[h] * qn[h] for h in heads]
    hh = [num[h] / jnp.maximum(jnp.abs(den[h]), jnp.exp(-m_t[h])) for h in heads]

    wst_c = [btot_c[:, h:h + 1] - bc_c[:, h:h + 1] + li_c[:, h:h + 1] for h in heads]
    wst_r = [btot_c[:, h:h + 1] - bc_r[h:h + 1, :] + li_r[h:h + 1, :] for h in heads]
    m_new = [jnp.maximum(btot_c[:, h:h + 1] + m_s[h], jnp.max(wst_r[h], axis=1, keepdims=True)) for h in heads]
    dstate = [jnp.exp(btot_c[:, h:h + 1] + m_s[h] - m_new[h]) for h in heads]
    kw = [ks[h].astype(F32) * jnp.exp(wst_c[h] - m_new[h]) for h in heads]
    upd = [_dot(kw[h].T.astype(BF16), vs[h]) for h in heads]
    n_rows = [dstate[h] * n_s[h] + jnp.sum(kw[h], axis=0, keepdims=True) for h in heads]
    m_rows = [jnp.broadcast_to(m_new[h], (1, LANES)) for h in heads]

    for h in heads:
        cols = slice(dv * h, dv * (h + 1))
        if final:
            hs = hh[h] + hf_ref[rs, cols]
            hn = hs * lax.rsqrt(jnp.mean(hs * hs, axis=-1, keepdims=True) + EPS) * nw_ref[:, cols]
            o_ref[rs, cols] = (hn * _sigmoid(og_ref[rs, cols].astype(F32))).astype(o_ref.dtype)
        else:
            o_ref[rs, cols] = hh[h]

    return ([dstate[h] * cts[h] + upd[h] for h in heads], jnp.concatenate(n_rows, axis=0),
            jnp.concatenate(m_rows, axis=0))


def _mlstm(qk, pr, g_c, g_r, *, direction, n_lat, final_args=None):
    nb, t, _ = qk.shape
    q = SCAN_CPS * SCAN_Q
    assert n_lat % q == 0 and t % q == 0
    nch = t // q
    reverse = direction == 1
    cmap = _chunk_map(reverse, nch, n_lat // q)
    d = direction
    in_specs = [
        pl.BlockSpec((None, q, ML_QK_WIDTH), lambda b, k: (b, cmap(k), 0)),
        pl.BlockSpec((None, q, ML_QK_WIDTH), lambda b, k: (b, cmap(k), 1)),
        pl.BlockSpec((None, q, ML_WIDTH), lambda b, k: (b, cmap(k), 2 * ML_QK_WIDTH // ML_WIDTH)),
        pl.BlockSpec((None, q, 4 * ML_HEADS), lambda b, k: (b, cmap(k), 0)),
        pl.BlockSpec((None, None, 2 * ML_HEADS, q), lambda b, k: (b, d, 0, cmap(k))),
    ]
    args = [qk, qk, pr, g_c, g_r]
    final = final_args is not None
    if final:
        hf, norm_w = final_args
        in_specs += [pl.BlockSpec((None, q, ML_WIDTH), lambda b, k: (b, cmap(k), 0)),
                     pl.BlockSpec((None, q, ML_WIDTH), lambda b, k: (b, cmap(k), 2 * ML_QK_WIDTH // ML_WIDTH + 1)),
                     pl.BlockSpec((1, ML_WIDTH), lambda b, k: (0, 0))]
        args += [hf, pr, norm_w]
    out_shape = jax.ShapeDtypeStruct((nb, t, ML_WIDTH), BF16 if final else F32)
    return pl.pallas_call(
        functools.partial(_mlstm_body, reverse=reverse, final=final),
        out_shape=out_shape, grid=(nb, nch), in_specs=in_specs,
        out_specs=pl.BlockSpec((None, q, ML_WIDTH), lambda b, k: (b, cmap(k), 0)),
        scratch_shapes=[pltpu.VMEM((ML_HEADS, ML_QK_DIM, ML_V_DIM), F32), pltpu.VMEM((ML_HEADS, ML_QK_DIM), F32),
                        pltpu.VMEM((ML_HEADS, LANES), F32)],
        name="mlstm_bwd" if reverse else "mlstm_fwd",
        compiler_params=_cp(("parallel", "arbitrary")))(*args)


def _na_window_start(j, rows):
    return jnp.clip(j * NA_QROWS - NA_ROWS // 2, 0, rows - NA_KROWS)


def _na_part_offset(j, part, start, clip=jnp.clip):
    first = j * NA_QROWS + part * (NA_QROWS // NA_PARTS) - NA_ROWS // 2
    return clip(first - start, 0, NA_KROWS - NA_PART_KROWS) // NA_TAB_ROWS * NA_TAB_ROWS


def _na_body(q_ref, k_ref, v_ref, tab_ref, qw_ref, kw_ref, o_ref, kn_ref, v1_ref, *, n_lat, n_ctx):
    dh = NA_HEAD_DIM
    scale = dh ** -0.5
    rows = n_lat // GRID_W
    j = pl.program_id(2)
    heads = range(NA_HPS)
    hcols = [slice(hh * dh, (hh + 1) * dh) for hh in heads]

    @pl.when(j == 0)
    def _():
        prep = (n_lat + n_ctx) // NA_PREP_STEPS
        one_hot = (lax.broadcasted_iota(jnp.int32, (prep, LANES), 1) == 0).astype(v1_ref.dtype)

        def body(i, carry):
            r = pl.ds(pl.multiple_of(i * prep, 16), prep)
            for hh in heads:
                kk = k_ref[r, hcols[hh]].astype(F32)
                kn = kk * lax.rsqrt(jnp.mean(kk * kk, axis=-1, keepdims=True) + EPS) * kw_ref[...]
                kn_ref[r, hcols[hh]] = kn.astype(kn_ref.dtype)
                v1_ref[hh, r, 0:dh] = v_ref[r, hcols[hh]]
                v1_ref[hh, r, dh:dh + LANES] = one_hot
            return carry

        lax.fori_loop(0, NA_PREP_STEPS, body, 0)

    start = _na_window_start(j, rows)
    ctx = pl.ds(n_lat, n_ctx)
    rpp = NA_QROWS // NA_PARTS
    pq = rpp * GRID_W
    chains = [(hh, i) for hh in heads for i in range(NA_PARTS)]
    off = [_na_part_offset(j, i, start) for i in range(NA_PARTS)]
    win = [pl.ds(pl.multiple_of((start + off[i]) * GRID_W, GRID_W), NA_PART_KROWS * GRID_W) for i in range(NA_PARTS)]
    qq = [q_ref[i * pq:(i + 1) * pq, hcols[hh]].astype(F32) for hh, i in chains]
    qn = [(x * lax.rsqrt(jnp.mean(x * x, axis=-1, keepdims=True) + EPS) * (qw_ref[...] * scale)).astype(BF16)
          for x in qq]
    sw = [_dot_nt(qn[c], kn_ref[win[i], hcols[hh]]) for c, (hh, i) in enumerate(chains)]
    sc = [_dot_nt(qn[c], kn_ref[ctx, hcols[hh]]) for c, (hh, i) in enumerate(chains)]

    def with_bias(c, hh, i):
        out = []
        for r in range(rpp):
            d0 = start - (j * NA_QROWS + i * rpp + r) + NA_ROWS - 1 + NA_QROWS
            blocks = tab_ref[hh, d0, pl.ds(off[i] // NA_TAB_ROWS, NA_PART_KROWS // NA_TAB_ROWS)]
            bias = jnp.concatenate([blocks[b] for b in range(NA_PART_KROWS // NA_TAB_ROWS)], axis=1)
            out.append(sw[c][r * GRID_W:(r + 1) * GRID_W] + bias)
        return jnp.concatenate(out, axis=0)

    sb = [with_bias(c, hh, i) for c, (hh, i) in enumerate(chains)]
    m = [jnp.maximum(jnp.max(a, axis=1, keepdims=True), jnp.max(b, axis=1, keepdims=True)) for a, b in zip(sb, sc)]
    pw = [jnp.exp(a - mm) for a, mm in zip(sb, m)]
    pc = [jnp.exp(a - mm) for a, mm in zip(sc, m)]
    o = [_dot(pw[c].astype(BF16), v1_ref[hh, win[i], :]) + _dot(pc[c].astype(BF16), v1_ref[hh, ctx, :])
         for c, (hh, i) in enumerate(chains)]
    for c, (hh, i) in enumerate(chains):
        o_ref[i * pq:(i + 1) * pq, hcols[hh]] = (o[c][:, 0:dh] / o[c][:, dh:dh + 1]).astype(o_ref.dtype)


def _na_bias_table(rpb, rows):
    nd = 2 * NA_QROWS
    valid = np.zeros((nd, NA_KROWS), bool)
    seen = np.zeros((nd,), bool)
    for j in range(rows // NA_QROWS):
        start = int(np.clip(j * NA_QROWS - NA_ROWS // 2, 0, rows - NA_KROWS))
        for qr in range(NA_QROWS):
            r = j * NA_QROWS + qr
            rs = int(np.clip(r - NA_ROWS // 2, 0, rows - NA_ROWS))
            d = start - r + NA_ROWS - 1 + NA_QROWS
            v = np.array([(rs <= start + i < rs + NA_ROWS) for i in range(NA_KROWS)])
            assert 0 <= d < nd and (not seen[d] or (valid[d] == v).all())
            valid[d], seen[d] = v, True
            off = int(_na_part_offset(j, qr // (NA_QROWS // NA_PARTS), start, clip=np.clip))
            assert off % NA_TAB_ROWS == 0 and not v[:off].any() and not v[off + NA_PART_KROWS:].any()
    col = np.arange(GRID_W)
    cs = np.clip(col - NA_COLS // 2, 0, GRID_W - NA_COLS)
    in_win = (col[None, :] >= cs[:, None]) & (col[None, :] < cs[:, None] + NA_COLS)
    ndr, ndc, w = 2 * NA_ROWS - 1, 2 * NA_COLS - 1, GRID_W
    dr = (np.arange(nd)[:, None] - NA_QROWS) + np.arange(NA_KROWS)[None, :]
    ok = valid[:, None, :, None] & in_win[None, :, None, :] & (dr >= 0)[:, None, :, None] \
        & (dr < ndr)[:, None, :, None]
    ext = jnp.pad(rpb, ((0, 0), (0, 0), (w, w)), mode="edge")
    c0 = w + NA_COLS - 1
    t1 = jnp.stack([ext[:, :, c0 - qc:c0 - qc + w] for qc in range(w)], axis=2)
    t1p = jnp.pad(t1, ((0, 0), (NA_QROWS, nd + NA_KROWS - 1 - NA_QROWS - ndr), (0, 0), (0, 0)))
    wide = jnp.concatenate([t1p[:, m] for m in range(nd + NA_KROWS - 1)], axis=-1)
    wpad = LANES * pl.cdiv(wide.shape[-1], LANES)
    wide = jnp.pad(wide, ((0, 0), (0, 0), (0, wpad - wide.shape[-1])))
    nblk, bw = NA_KROWS // NA_TAB_ROWS, NA_TAB_ROWS * w
    okf = jnp.asarray(ok.reshape(nd, w, nblk, bw).transpose(0, 2, 1, 3).astype(np.float32))

    def body(wide_ref, ok_ref, o_ref):
        for d in range(nd):
            for b in range(nblk):
                lo = d * w + b * bw
                o_ref[d, b] = jnp.where(ok_ref[d, b] != 0.0, wide_ref[:, lo:lo + bw], -jnp.inf)

    return pl.pallas_call(
        body, out_shape=jax.ShapeDtypeStruct((rpb.shape[0], nd, nblk, w, bw), F32), grid=(rpb.shape[0],),
        in_specs=[pl.BlockSpec((None, w, wpad), lambda h: (h, 0, 0)),
                  pl.BlockSpec((nd, nblk, w, bw), lambda h: (0, 0, 0, 0))],
        out_specs=pl.BlockSpec((None, nd, nblk, w, bw), lambda h: (h, 0, 0, 0, 0)),
        name="na_bias_table", compiler_params=_cp(("parallel",)))(wide, okf)


def _na(pr, tab, q_w, k_w, *, n_lat, n_ctx, qcol):
    nb, t, _ = pr.shape
    dh, nh, hps = NA_HEAD_DIM, NA_HEADS, NA_HPS
    nq = NA_QROWS * GRID_W
    hw = hps * dh
    assert qcol % hw == 0 and nh % hps == 0
    qb = qcol // hw
    return pl.pallas_call(
        functools.partial(_na_body, n_lat=n_lat, n_ctx=n_ctx),
        out_shape=jax.ShapeDtypeStruct((nb, n_lat, NA_WIDTH), BF16), grid=(nb, nh // hps, n_lat // nq),
        in_specs=[pl.BlockSpec((None, nq, hw), lambda b, h, j: (b, j, qb + h)),
                  pl.BlockSpec((None, t, hw), lambda b, h, j: (b, 0, qb + nh // hps + h)),
                  pl.BlockSpec((None, t, hw), lambda b, h, j: (b, 0, qb + 2 * (nh // hps) + h)),
                  pl.BlockSpec((hps,) + tab.shape[1:], lambda b, h, j: (h, 0, 0, 0, 0)),
                  pl.BlockSpec((1, dh), lambda b, h, j: (0, 0)),
                  pl.BlockSpec((1, dh), lambda b, h, j: (0, 0))],
        out_specs=pl.BlockSpec((None, nq, hw), lambda b, h, j: (b, j, h)),
        scratch_shapes=[pltpu.VMEM((t, hw), BF16), pltpu.VMEM((hps, t, dh + LANES), BF16)], name="na",
        compiler_params=_cp(("parallel", "parallel", "arbitrary")))(pr, pr, pr, tab, q_w, k_w)


def _rope_tables(n_lat, n_ctx, width):
    dh = ML_QK_DIM
    nf = dh // 4
    t = jnp.arange(n_lat)
    inv = ROPE_THETA ** (-jnp.arange(nf, dtype=F32) / nf)
    lane = np.arange(dh)
    pos = jnp.where((lane < dh // 2)[None, :], (t // GRID_W)[:, None], (t % GRID_W)[:, None]).astype(F32)
    ang = pos * inv[lane % nf][None, :]
    sign = np.where((lane % (dh // 2)) < nf, -1.0, 1.0).astype(np.float32)
    cos = jnp.concatenate([jnp.cos(ang), jnp.ones((n_ctx, dh), F32)], axis=0)
    sin = jnp.concatenate([jnp.sin(ang) * sign[None, :], jnp.zeros((n_ctx, dh), F32)], axis=0)
    reps = width // dh
    return jnp.tile(cos, (1, reps)), jnp.tile(sin, (1, reps))


def _ada_mod(c, c_ctx, w_all, b_all, layer):
    nb, d = c.shape
    rows = 16
    cv = jnp.concatenate([c, c_ctx[None, :], jnp.zeros((rows - nb - 1, d), F32)], axis=0)
    a = (cv * _sigmoid(cv)).astype(BF16)[None]
    out = _matmul(a, w_all, w_index=layer, tm=rows, tn=1024, tk=d, out_dtype=F32, name="ada_mod")[0]
    out = out[:nb + 1] + b_all[layer][None, :]
    return out.reshape(nb + 1, 6, d).transpose(1, 0, 2)


def _lat_ctx(m, nb):
    ctx = jnp.broadcast_to(m[nb][None, :], (nb, m.shape[1]))
    return jnp.stack([m[:nb], ctx], axis=1)[:, :, None, :]


def _pad_cols(w, n):
    return jnp.pad(w, ((0, 0), (0, n - w.shape[1])))


def _row_form(g, groups):
    nb, t, _ = g.shape
    return g.reshape(nb, t, 2, groups, -1).transpose(0, 2, 3, 4, 1)


ROW_TILES = (1088, 1024, 512, 256)


def _row_tile(rows):
    return next(tm for tm in ROW_TILES if rows % tm == 0)


def _mod_pair(mods, shift, scale, nb):
    return jnp.stack([_lat_ctx(mods[shift], nb), _lat_ctx(mods[scale], nb)], axis=1)


def _in_projection(u, w_all, layer, *, rows, name, cols=None, w_nk=False):
    tm = _row_tile(rows)
    tm = 2 * tm if rows % (2 * tm) == 0 else tm
    return _matmul(u, w_all, w_index=layer, w_nk=w_nk, cols=cols, tm=tm, tn=1024, tk=u.shape[2], out_dtype=BF16,
                   rows=rows, vmem=VMEM_LIMIT_BIG, name=name)


def _out_projection(a_list, w_all, x, gate, layer, *, rows, n_lat, next_norm, name):
    tm = _row_tile(rows)
    tn = 512
    x = _matmul_gated_residual_rows(a_list, w_all, x, gate, w_index=layer, tm=tm, tn=tn, rows=rows, n_lat=n_lat,
                                    name=name)
    if next_norm is None:
        return x, None
    return x, _rmsmod(x, *next_norm, tm=tm, rows=rows, n_lat=n_lat)


def _conv_ffn_block(x, u, mods, w_up_all, conv_w, conv_b, w_down_all, layer, *, rows, n_lat, segs, next_norm):
    nb = x.shape[0]
    up = _in_projection(u, w_up_all, layer, rows=rows, name="ffn_up")
    hid = _conv_ffn(up, conv_w, conv_b[None, :], segs=segs, rows=rows)
    return _out_projection((hid,), w_down_all, x, _lat_ctx(mods[5], nb), layer, rows=rows, n_lat=n_lat,
                           next_norm=next_norm, name="ffn_down")


def _even_layer(x, u, mods, w_in_all, conv_w, conv_b, dt_bias, a_log, d_skip, ssd_norm_w,
                hy_w1, hy_b1, hy_w2, hy_b2, hy_w3, hy_freq, hy_bias, w_out_all, j, *, n_lat, n_ctx, next_norm):
    nb, t, d = x.shape
    segs = ((0, n_lat), (n_lat, n_ctx))
    tm = _row_tile(t)
    n_main = SSD_WIDTH + EV_CONV_CH
    pr = _in_projection(u, jnp.swapaxes(w_in_all, 1, 2), j, rows=t, cols=n_main, w_nk=True, name="ev_in")
    w_dt = _pad_cols(w_in_all[j, :, n_main:], LANES).astype(BF16)
    dtp = _matmul(u, w_dt[None], tm=tm, tn=LANES, tk=d, out_dtype=F32, name="ev_in_dt")[:, :, :2 * SSD_HEADS]
    cb = conv_b[None, :]
    xbc = _conv_silu(pr, conv_w, cb, xcol=SSD_WIDTH, wcol=0, width=SSD_XBC, segs=segs)
    x0, zin_l, zin_c, zle, zlo, zce, zco = _conv_hyena(pr, conv_w, cb, xcol=SSD_WIDTH + SSD_XBC, wcol=SSD_XBC,
                                                       segs=segs)

    db = dt_bias.reshape(2, SSD_GROUPS, 1, SSD_HPG)
    al = a_log.reshape(2, SSD_GROUPS, 1, SSD_HPG)
    scan_args = (xbc, dtp, _row_form(dtp, SSD_GROUPS), db, db.transpose(0, 1, 3, 2), al, al.transpose(0, 1, 3, 2))
    yf = _ssd(*scan_args, direction=0, n_lat=n_lat)
    dsk = jnp.repeat(d_skip, SSD_HEAD_DIM)[None, :]
    y_ssd = _ssd(*scan_args, direction=1, n_lat=n_lat, final_args=(yf, pr, dsk, ssd_norm_w[None, :]))

    hyb = hy_bias[None, :]
    filt = functools.partial(_hyena_filters, w1=hy_w1, b1=hy_b1, w2=hy_w2, b2=hy_b2, w3=hy_w3, freq=hy_freq)
    lat = _hyena_longconv(zle, zlo, filt(n_lat)) + (zin_l,)
    ctx = _hyena_longconv(zce, zco, filt(n_ctx)) + (zin_c,)
    y_hy = _hyena_out((lat, ctx), x0, hyb, segs=segs)
    return _out_projection((y_ssd, y_hy), w_out_all, x, _lat_ctx(mods[2], nb), j, rows=t, n_lat=n_lat,
                           next_norm=next_norm, name="ev_out")


def _odd_layer_latent(x, u, mods, w_in_all, conv_w, conv_b, gate_b, ml_norm_w, q_norm_w, k_norm_w, rpb,
                      w_out_all, j, *, n_lat, n_ctx, next_norm):
    nb, t, d = x.shape
    segs = ((0, n_lat), (n_lat, n_ctx))
    tm = _row_tile(t)
    o3 = 2 * ML_QK_WIDTH + 2 * ML_WIDTH
    o4 = o3 + 4 * ML_HEADS
    w_nk = jnp.swapaxes(w_in_all, 1, 2)
    pr = _in_projection(u, w_nk, j, rows=t, cols=o3, w_nk=True, name="od_in_mlstm")
    pr_na = _in_projection(u, w_nk[j:j + 1, o4:, :], 0, rows=t, w_nk=True, name="od_in_na")
    gates = _matmul(u, _pad_cols(w_in_all[j, :, o3:o4], LANES).astype(BF16)[None], tm=tm, tn=LANES, tk=d,
                    out_dtype=F32, name="od_in_gates")[:, :, :4 * ML_HEADS] + gate_b.reshape(-1)
    cos, sin = _rope_tables(n_lat, n_ctx, 256)
    qk = _conv_rope(pr, conv_w, conv_b[None, :], cos, sin, width=2 * ML_QK_WIDTH, segs=segs)
    g_r = gates.reshape(nb, t, 2, 2 * ML_HEADS).transpose(0, 2, 3, 1)
    hf = _mlstm(qk, pr, gates, g_r, direction=0, n_lat=n_lat)
    y_ml = _mlstm(qk, pr, gates, g_r, direction=1, n_lat=n_lat, final_args=(hf, ml_norm_w[None, :]))
    tab = _na_bias_table(rpb, n_lat // GRID_W)
    y_na = _na(pr_na, tab, q_norm_w[None, :], k_norm_w[None, :], n_lat=n_lat, n_ctx=n_ctx, qcol=0)
    return _out_projection((y_ml, y_na), w_out_all, x, _lat_ctx(mods[2], nb), j, rows=n_lat, n_lat=n_lat,
                           next_norm=next_norm, name="od_out")


def kernel(x, c, ctx, c_ctx, ada_w, ada_b, norm_w, ev_w_in, ev_conv_w, ev_conv_b, ssd_dt_bias, ssd_a_log, ssd_d,
           ssd_norm_w, hy_w1, hy_b1, hy_w2, hy_b2, hy_w3, hy_freq, hy_bias, ev_w_out, od_w_in, ml_conv_w, ml_conv_b,
           ml_gate_b, ml_norm_w, na_q_norm_w, na_k_norm_w, na_rpb, od_w_out, ffn_w_up, ffn_conv_w, ffn_conv_b,
           ffn_w_down):
    depth = ada_w.shape[0]
    assert depth == 2, "layer schedule below is written for one even and one odd layer"
    n_lat, n_ctx = x.shape[1], ctx.shape[1]
    t = n_lat + n_ctx
    both = ((0, n_lat), (n_lat, n_ctx))
    xs = jnp.concatenate([x, ctx], axis=1)

    nb = x.shape[0]
    w_down = ffn_w_down.astype(BF16)
    mods0 = _ada_mod(c, c_ctx, ada_w, ada_b, 0)
    mods1 = _ada_mod(c, c_ctx, ada_w, ada_b, 1)
    norm = lambda layer, sub, mods: (norm_w[layer, sub][None, :], _mod_pair(mods, 3 * sub, 3 * sub + 1, nb))

    w0, mod0 = norm(0, 0, mods0)
    u = _rmsmod(xs, w0, mod0, tm=_row_tile(t), rows=t, n_lat=n_lat)
    xs, u = _even_layer(xs, u, mods0, ev_w_in, ev_conv_w[0], ev_conv_b[0], ssd_dt_bias[0], ssd_a_log[0],
                        ssd_d[0], ssd_norm_w[0], hy_w1[0], hy_b1[0], hy_w2[0], hy_b2[0], hy_w3[0], hy_freq[0],
                        hy_bias[0], ev_w_out.astype(BF16), 0, n_lat=n_lat, n_ctx=n_ctx, next_norm=norm(0, 1, mods0))
    xs, u = _conv_ffn_block(xs, u, mods0, ffn_w_up, ffn_conv_w[0], ffn_conv_b[0], w_down, 0,
                            rows=t, n_lat=n_lat, segs=both, next_norm=norm(1, 0, mods1))
    xl, u = _odd_layer_latent(xs, u, mods1, od_w_in, ml_conv_w[0], ml_conv_b[0], ml_gate_b[0], ml_norm_w[0],
                              na_q_norm_w[0], na_k_norm_w[0], na_rpb[0], od_w_out.astype(BF16), 0,
                              n_lat=n_lat, n_ctx=n_ctx, next_norm=norm(1, 1, mods1))
    out, _ = _conv_ffn_block(xl, u, mods1, ffn_w_up, ffn_conv_w[1], ffn_conv_b[1], w_down, 1,
                             rows=n_lat, n_lat=n_lat, segs=((0, n_lat),), next_norm=None)
    return out
```

```python
import functools
import math

import numpy as np
import jax
import jax.numpy as jnp
from jax import lax
from jax.experimental import pallas as pl
from jax.experimental.pallas import tpu as pltpu

F32 = jnp.float32
BF16 = jnp.bfloat16
HI = lax.Precision.HIGHEST

D_MODEL = 2048
GRID_W = 64
EPS = 1e-6
CONV_W = 3
SSD_WIDTH = D_MODEL
SSD_HEAD_DIM = 64
SSD_HEADS = SSD_WIDTH // SSD_HEAD_DIM
SSD_STATE = 128
SSD_GROUPS = 4
SSD_HPG = SSD_HEADS // SSD_GROUPS
SSD_GW = SSD_WIDTH // SSD_GROUPS
SSD_XBC = SSD_WIDTH + 2 * SSD_GROUPS * SSD_STATE
HY_WIDTH = D_MODEL
HY_EMB = 33
HY_BANDS = (HY_EMB - 1) // 2
HY_ORDER = 64
HY_FAST = 0.3
HY_SLOW = 1.5
HY_TARGET = 1e-2
ML_HEADS = 8
ML_WIDTH = D_MODEL
ML_V_DIM = ML_WIDTH // ML_HEADS
ML_QK_DIM = ML_V_DIM // 2
ML_QK_WIDTH = ML_HEADS * ML_QK_DIM
ROPE_THETA = 10000.0
NA_WIDTH = D_MODEL
NA_HEAD_DIM = 128
NA_HEADS = NA_WIDTH // NA_HEAD_DIM
NA_ROWS = 8
NA_COLS = 16
NA_QROWS = 8
NA_KROWS = 16
NA_PARTS = 2
NA_HPS = 2
NA_PREP_STEPS = 8
NA_TAB_ROWS = 4
NA_PART_KROWS = 12
EV_CONV_CH = SSD_XBC + 3 * HY_WIDTH
SCAN_Q = 128
SCAN_CPS = 2
LANES = 128
SUBLANES = 8
MIB = 1024 * 1024
V7X_VMEM_BYTES = 64 * MIB
VMEM_LIMIT = 48 * MIB
VMEM_LIMIT_BIG = V7X_VMEM_BYTES - 6 * MIB


def _cp(sem, vmem=VMEM_LIMIT):
    return pltpu.CompilerParams(dimension_semantics=sem, vmem_limit_bytes=vmem)


def _sigmoid(x):
    return 0.5 * jnp.tanh(0.5 * x) + 0.5


def _silu(x):
    return x * _sigmoid(x)


def _softplus(x):
    return jnp.maximum(x, 0.0) + jnp.log(1.0 + jnp.exp(-jnp.abs(x)))


def _log_sigmoid(x):
    return -_softplus(-x)


def _dot(a, b, precision=None):
    return jnp.dot(a, b, preferred_element_type=F32, precision=precision)


def _dot_nt(a, b):
    return lax.dot_general(a, b, (((1,), (1,)), ((), ())), preferred_element_type=F32)


def _split_bf16(x, terms):
    out = []
    for _ in range(terms):
        part = x.astype(BF16)
        out.append(part)
        x = x - part.astype(F32)
    return out


def _dot_sel(x, sel, terms):
    selb = sel.astype(BF16)
    return sum(_dot(part, selb) for part in _split_bf16(x, terms))


def _sel_dot(sel, x, terms):
    selb = sel.astype(BF16)
    return sum(_dot(selb, part) for part in _split_bf16(x, terms))


def _mm_body(a_ref, w_ref, o_ref, *scratch, nk, cache_w, w_nk):
    if cache_w:
        wb_ref = scratch[0]

        @pl.when((pl.program_id(1) == 0) & (pl.program_id(2) == 0))
        def _():
            wb_ref[...] = (w_ref[...].T if w_nk else w_ref[...]).astype(BF16)

        w_val = wb_ref[...]
    else:
        w_val = w_ref[...].astype(BF16)
    part = _dot(a_ref[...].astype(BF16), w_val)
    if nk == 1:
        o_ref[...] = part.astype(o_ref.dtype)
        return
    acc_ref = scratch[0]
    k = pl.program_id(3)

    @pl.when(k == 0)
    def _():
        acc_ref[...] = part

    @pl.when(k > 0)
    def _():
        acc_ref[...] += part

    @pl.when(k == nk - 1)
    def _():
        o_ref[...] = acc_ref[...].astype(o_ref.dtype)


def _matmul(a, w, *, tm, tn, tk, out_dtype, rows=None, cols=None, w_index=None, w_nk=False, vmem=VMEM_LIMIT,
            name="mm"):
    ba, ta, kdim = a.shape
    bw, kdim2, n = (w.shape[0], w.shape[2], w.shape[1]) if w_nk else w.shape
    assert kdim == kdim2
    nb = ba if w_index is not None else max(ba, bw)
    rows = ta if rows is None else rows
    n = n if cols is None else cols
    assert rows % tm == 0 and n % tn == 0 and kdim % tk == 0
    ni, nj, nk = rows // tm, n // tn, kdim // tk
    a_map = (lambda j, b, i, k: (b, i, k)) if ba > 1 else (lambda j, b, i, k: (0, i, k))
    if w_index is not None:
        w_map = lambda j, b, i, k: (w_index, k, j)
    else:
        w_map = (lambda j, b, i, k: (b, k, j)) if bw > 1 else (lambda j, b, i, k: (0, k, j))
    cache_w = w.dtype != BF16 and nk == 1 and (w_index is not None or bw == 1)
    assert cache_w or not w_nk
    scratch = [pltpu.VMEM((tm, tn), F32)] if nk > 1 else []
    if cache_w:
        scratch = [pltpu.VMEM((tk, tn), BF16)]
    inner = "arbitrary" if cache_w else "parallel"
    if w_nk:
        w_spec = pl.BlockSpec((None, tn, tk), lambda j, b, i, k: (w_map(j, b, i, k)[0], j, k))
    else:
        w_spec = pl.BlockSpec((None, tk, tn), w_map)
    return pl.pallas_call(
        functools.partial(_mm_body, nk=nk, cache_w=cache_w, w_nk=w_nk),
        out_shape=jax.ShapeDtypeStruct((nb, rows, n), out_dtype), grid=(nj, nb, ni, nk),
        in_specs=[pl.BlockSpec((None, tm, tk), a_map), w_spec],
        out_specs=pl.BlockSpec((None, tm, tn), lambda j, b, i, k: (b, i, j)),
        scratch_shapes=scratch, name=name,
        compiler_params=_cp(("parallel", inner, inner, "arbitrary"), vmem))(a, w)


def _row_is_lat(row_blk, tm, n_lat):
    rows = row_blk * tm + lax.broadcasted_iota(jnp.int32, (tm, 1), 0)
    return rows < n_lat


def _gated_residual_epilogue(acc, row_blk, res_ref, gate_ref, *, tm, n_lat):
    gate = jnp.where(_row_is_lat(row_blk, tm, n_lat), gate_ref[0], gate_ref[1])
    return res_ref[...] + gate * acc


def _gated_residual_rows_body(*refs, ks, tm, n_lat):
    n_a = len(ks)
    a_refs, w_ref, res_ref, gate_ref, o_ref = refs[:n_a], refs[n_a], refs[n_a + 1], refs[n_a + 2], refs[n_a + 3]
    acc, lo = None, 0
    for a_ref, kk in zip(a_refs, ks):
        part = _dot(a_ref[...], w_ref[lo:lo + kk, :])
        acc = part if acc is None else acc + part
        lo += kk
    o_ref[...] = _gated_residual_epilogue(acc, pl.program_id(1), res_ref, gate_ref, tm=tm, n_lat=n_lat)


def _matmul_gated_residual_rows(a_list, w, res, gate, *, w_index, tm, tn, rows, n_lat, name):
    nb = a_list[0].shape[0]
    _, kdim, n = w.shape
    ks = tuple(x.shape[2] for x in a_list)
    assert sum(ks) == kdim and rows % tm == 0 and n % tn == 0
    in_specs = [pl.BlockSpec((None, tm, kk), lambda b, i, j: (b, i, 0)) for kk in ks]
    in_specs += [pl.BlockSpec((None, kdim, tn), lambda b, i, j: (w_index, 0, j)),
                 pl.BlockSpec((None, tm, tn), lambda b, i, j: (b, i, j)),
                 pl.BlockSpec((None, 2, 1, tn), lambda b, i, j: (b, 0, 0, j))]
    return pl.pallas_call(
        functools.partial(_gated_residual_rows_body, ks=ks, tm=tm, n_lat=n_lat),
        out_shape=jax.ShapeDtypeStruct((nb, rows, n), F32), grid=(nb, rows // tm, n // tn), in_specs=in_specs,
        out_specs=pl.BlockSpec((None, tm, tn), lambda b, i, j: (b, i, j)), name=name,
        compiler_params=_cp(("parallel", "parallel", "parallel"), VMEM_LIMIT_BIG))(*a_list, w, res, gate)


def _modulated_rmsnorm(x, is_lat, w_ref, mod_ref):
    y = x * lax.rsqrt(jnp.mean(x * x, axis=-1, keepdims=True) + EPS) * w_ref[...]
    shift = jnp.where(is_lat, mod_ref[0, 0], mod_ref[0, 1])
    scale = jnp.where(is_lat, mod_ref[1, 0], mod_ref[1, 1])
    return y * (1.0 + scale) + shift


def _rmsmod_body(x_ref, w_ref, mod_ref, o_ref, *, tm, n_lat):
    is_lat = _row_is_lat(pl.program_id(2), tm, n_lat)
    o_ref[...] = _modulated_rmsnorm(x_ref[...], is_lat, w_ref, mod_ref).astype(o_ref.dtype)


def _rmsmod(x, w, mod, *, tm, rows, n_lat):
    nb, _, d = x.shape
    tm = tm // 2
    assert rows % tm == 0 and tm % 16 == 0
    body = functools.partial(_rmsmod_body, tm=tm, n_lat=n_lat)
    return pl.pallas_call(
        body, out_shape=jax.ShapeDtypeStruct((nb, rows, d), BF16), grid=(1, nb, rows // tm),
        in_specs=[pl.BlockSpec((None, tm, d), lambda j, b, i: (b, i, 0)),
                  pl.BlockSpec((1, d), lambda j, b, i: (0, 0)),
                  pl.BlockSpec((None, 2, 2, 1, d), lambda j, b, i: (b, 0, 0, 0, 0))],
        out_specs=pl.BlockSpec((None, tm, d), lambda j, b, i: (b, i, 0)),
        name="rmsmod", compiler_params=_cp(("parallel", "parallel", "parallel")))(x, w, mod)


CONV_CH = 128
CONV_HALO = 16
CONV_STEPS = 4


def _shift_matrices():
    r = lax.broadcasted_iota(jnp.int32, (CONV_CH, CONV_CH + 2 * CONV_HALO), 0)
    c = lax.broadcasted_iota(jnp.int32, (CONV_CH, CONV_CH + 2 * CONV_HALO), 1)
    return (c == r + CONV_HALO - 1).astype(BF16), (c == r + CONV_HALO + 1).astype(BF16)


def _chunk_rows(r0, i):
    return pl.ds(r0 + pl.multiple_of(i * CONV_CH, CONV_CH), CONV_CH)


def _taps(w_ref, b_ref, scale=1.0):
    return w_ref[...] * scale, b_ref[...] * scale


def _silu_of_half(h):
    return h + h * jnp.tanh(h)


def _conv_at(x_ref, taps, shifts, r0, n, steps):
    w, b = taps
    cur = [x_ref[_chunk_rows(r0, i), :] for i in steps]
    zero = jnp.zeros((CONV_HALO, cur[0].shape[1]), cur[0].dtype)
    lo = [pl.multiple_of(jnp.maximum(i * CONV_CH - CONV_HALO, 0), CONV_HALO) for i in steps]
    hi = [pl.multiple_of(jnp.minimum((i + 1) * CONV_CH, n - CONV_HALO), CONV_HALO) for i in steps]
    tail = [jnp.where(i > 0, x_ref[pl.ds(r0 + l, CONV_HALO), :], zero) for i, l in zip(steps, lo)]
    head = [jnp.where(i < n // CONV_CH - 1, x_ref[pl.ds(r0 + h, CONV_HALO), :], zero) for i, h in zip(steps, hi)]
    win = [jnp.concatenate([t, c, h], axis=0) for t, c, h in zip(tail, cur, head)]
    prev = [_dot(shifts[0], w) for w in win]
    nxt = [_dot(shifts[1], w) for w in win]
    return [w[0:1, :] * p + w[1:2, :] * c.astype(F32) + w[2:3, :] * x + b for p, c, x in zip(prev, cur, nxt)]


def _conv_loop(segs, fn):
    for si, (r0, n) in enumerate(segs):
        def body(i, carry, r0=r0, n=n, si=si):
            fn(si, r0, n, [i])
            return carry

        lax.fori_loop(0, n // CONV_CH, body, 0, unroll=min(CONV_STEPS, n // CONV_CH))


def _conv_silu_body(x_ref, w_ref, b_ref, o_ref, *, segs):
    shifts = _shift_matrices()
    half_taps = _taps(w_ref, b_ref, 0.5)

    def fn(si, r0, n, steps):
        ys = [_silu_of_half(h) for h in _conv_at(x_ref, half_taps, shifts, r0, n, steps)]
        for i, y in zip(steps, ys):
            o_ref[_chunk_rows(r0, i), :] = y.astype(o_ref.dtype)

    _conv_loop(segs, fn)


def _conv_rope_body(x_ref, w_ref, b_ref, cos_ref, sin_ref, o_ref, *, segs, tc):
    shifts = _shift_matrices()
    lane = lax.broadcasted_iota(jnp.int32, (CONV_CH, tc), 1)
    first = (lane % (ML_QK_DIM // 2)) < (ML_QK_DIM // 4)
    nf = ML_QK_DIM // 4
    half_taps = _taps(w_ref, b_ref, 0.5)

    def fn(si, r0, n, steps):
        ys = [_silu_of_half(h) for h in _conv_at(x_ref, half_taps, shifts, r0, n, steps)]
        partner = [jnp.where(first, pltpu.roll(y, tc - nf, 1), pltpu.roll(y, nf, 1)) for y in ys]
        for i, y, p in zip(steps, ys, partner):
            rows = _chunk_rows(r0, i)
            o_ref[rows, :] = (y * cos_ref[rows, :] + p * sin_ref[rows, :]).astype(o_ref.dtype)

    _conv_loop(segs, fn)


def _conv_ffn_body(a_ref, x_ref, w_ref, b_ref, o_ref, *, segs):
    shifts = _shift_matrices()
    half_taps = _taps(w_ref, b_ref, 0.5)

    def fn(si, r0, n, steps):
        gs = [_silu_of_half(h) for h in _conv_at(x_ref, half_taps, shifts, r0, n, steps)]
        for i, g in zip(steps, gs):
            rows = _chunk_rows(r0, i)
            o_ref[rows, :] = (a_ref[rows, :].astype(F32) * g).astype(o_ref.dtype)

    _conv_loop(segs, fn)


def _conv_hyena_body(x0_ref, x1_ref, v_ref, w0_ref, b0_ref, w1_ref, b1_ref, w2_ref, b2_ref,
                     ox0_ref, ozl_ref, ozc_ref, ozle_ref, ozlo_ref, ozce_ref, ozco_ref, *zs_refs, segs):
    shifts = _shift_matrices()
    taps = (_taps(w0_ref, b0_ref), _taps(w1_ref, b1_ref), _taps(w2_ref, b2_ref))
    half = CONV_CH // 2
    outs = ((ozl_ref, ozle_ref, ozlo_ref), (ozc_ref, ozce_ref, ozco_ref))

    def fn(si, r0, n, steps):
        x0 = _conv_at(x0_ref, taps[0], shifts, r0, n, steps)
        x1 = _conv_at(x1_ref, taps[1], shifts, r0, n, steps)
        v = _conv_at(v_ref, taps[2], shifts, r0, n, steps)
        nat_ref, even_ref, odd_ref = outs[si]
        for i, x0_i, x1_i, v_i in zip(steps, x0, x1, v):
            ox0_ref[_chunk_rows(r0, i), :] = x0_i.astype(ox0_ref.dtype)
            zin = x1_i * v_i
            nat_ref[_chunk_rows(0, i), :] = zin.astype(BF16)
            t0 = r0 + pl.multiple_of(i * CONV_CH, CONV_CH)
            hrows = pl.ds(pl.multiple_of(i * half, half), half)
            for g, zs_ref in enumerate(zs_refs):
                lanes = slice(g * LANES, (g + 1) * LANES)
                zs_ref[pl.ds(t0, CONV_CH), :] = zin[:, lanes]
                even_ref[hrows, lanes] = zs_ref[pl.ds(t0, half, stride=2), :].astype(BF16)
                odd_ref[hrows, lanes] = zs_ref[pl.ds(t0 + 1, half, stride=2), :].astype(BF16)

    _conv_loop(segs, fn)


def _conv_specs(t_in, tc, xcol_blk, wcol_blk):
    return [pl.BlockSpec((None, t_in, tc), lambda b, j: (b, 0, j + xcol_blk)),
            pl.BlockSpec((CONV_W, tc), lambda b, j: (0, j + wcol_blk)),
            pl.BlockSpec((1, tc), lambda b, j: (0, j + wcol_blk))]


def _conv_silu(x, w, b, *, xcol, wcol, width, segs, tc=512):
    nb, t_in, _ = x.shape
    return pl.pallas_call(
        functools.partial(_conv_silu_body, segs=segs),
        out_shape=jax.ShapeDtypeStruct((nb, t_in, width), BF16), grid=(nb, width // tc),
        in_specs=_conv_specs(t_in, tc, xcol // tc, wcol // tc),
        out_specs=pl.BlockSpec((None, t_in, tc), lambda b, j: (b, 0, j)),
        name="conv_silu", compiler_params=_cp(("parallel", "parallel")))(x, w, b)


def _conv_rope(x, w, b, cos, sin, *, width, segs, tc=256):
    nb, t_in, _ = x.shape
    tab = pl.BlockSpec((t_in, tc), lambda b, j: (0, 0))
    return pl.pallas_call(
        functools.partial(_conv_rope_body, segs=segs, tc=tc),
        out_shape=jax.ShapeDtypeStruct((nb, t_in, width), BF16), grid=(nb, width // tc),
        in_specs=_conv_specs(t_in, tc, 0, 0) + [tab, tab],
        out_specs=pl.BlockSpec((None, t_in, tc), lambda b, j: (b, 0, j)),
        name="conv_rope", compiler_params=_cp(("parallel", "parallel")))(x, w, b, cos, sin)


def _conv_ffn(up, w, b, *, segs, rows, tc=512):
    nb, t_in, two_ff = up.shape
    ff = two_ff // 2
    return pl.pallas_call(
        functools.partial(_conv_ffn_body, segs=segs),
        out_shape=jax.ShapeDtypeStruct((nb, rows, ff), BF16), grid=(nb, ff // tc),
        in_specs=[pl.BlockSpec((None, rows, tc), lambda b, j: (b, 0, j))] + _conv_specs(rows, tc, ff // tc, 0),
        out_specs=pl.BlockSpec((None, rows, tc), lambda b, j: (b, 0, j)),
        name="conv_ffn", compiler_params=_cp(("parallel", "parallel")))(up, up, w, b)


def _conv_hyena(pr, w, b, *, xcol, wcol, segs, tc=256):
    nb, t_in, _ = pr.shape
    (_, n_lat), (_, n_ctx) = segs
    wd = HY_WIDTH
    out_rows = (t_in, n_lat, n_ctx, n_lat // 2, n_lat // 2, n_ctx // 2, n_ctx // 2)
    in_specs = [pl.BlockSpec((None, t_in, tc), functools.partial(lambda b, j, o: (b, 0, j + o), o=(xcol + m * wd) // tc))
                for m in range(3)]
    wargs = []
    for m in range(3):
        off = (wcol + m * wd) // tc
        in_specs.append(pl.BlockSpec((CONV_W, tc), functools.partial(lambda b, j, o: (0, j + o), o=off)))
        in_specs.append(pl.BlockSpec((1, tc), functools.partial(lambda b, j, o: (0, j + o), o=off)))
        wargs += [w, b]
    return pl.pallas_call(
        functools.partial(_conv_hyena_body, segs=segs),
        out_shape=tuple(jax.ShapeDtypeStruct((nb, rows, wd), BF16) for rows in out_rows),
        grid=(nb, wd // tc), in_specs=in_specs,
        out_specs=tuple(pl.BlockSpec((None, rows, tc), lambda b, j: (b, 0, j)) for rows in out_rows),
        scratch_shapes=[pltpu.VMEM((t_in, LANES), F32)] * (tc // LANES),
        name="conv_hyena", compiler_params=_cp(("parallel", "parallel")))(pr, pr, pr, *wargs)


def _scan_masks(reverse):
    q = SCAN_Q
    ii = lax.broadcasted_iota(jnp.int32, (q, q), 0)
    jj = lax.broadcasted_iota(jnp.int32, (q, q), 1)
    mask = (jj >= ii) if reverse else (jj <= ii)
    mask_t = (jj <= ii) if reverse else (jj >= ii)
    return mask, mask.astype(F32), mask_t.astype(F32)


def _chunk_map(reverse, n_chunks, n_lat_chunks):
    if reverse:
        return lambda k: n_chunks - 1 - k
    return lambda k: (k + n_lat_chunks) % n_chunks


SSD_GPS = 4


def _ssd_body(*refs, reverse, final):
    (xs_ref, b_ref, c_ref, dtc_ref, dtr_ref, dbc_ref, dbr_ref, alc_ref, alr_ref) = refs[:9]
    if final:
        yf_ref, z_ref, dsk_ref, nw_ref, o_ref, h_ref = refs[9:]
    else:
        o_ref, h_ref = refs[9:]

    @pl.when(pl.program_id(2) == 0)
    def _():
        h_ref[...] = jnp.zeros_like(h_ref)

    groups = range(SSD_GPS)
    h = [h_ref[g] for g in groups]
    for c in (reversed(range(SCAN_CPS)) if reverse else range(SCAN_CPS)):
        h = _ssd_chunk(refs, slice(c * SCAN_Q, (c + 1) * SCAN_Q), h, reverse=reverse, final=final)
    for g in groups:
        h_ref[g] = h[g]


def _ssd_chunk(refs, rs, h_prev, *, reverse, final):
    (xs_ref, b_ref, c_ref, dtc_ref, dtr_ref, dbc_ref, dbr_ref, alc_ref, alr_ref) = refs[:9]
    if final:
        yf_ref, z_ref, dsk_ref, nw_ref, o_ref = refs[9:14]
    else:
        o_ref = refs[9]
    q, gw, hd = SCAN_Q, SSD_GW, SSD_HEAD_DIM
    mask, maskf, mask_tf = _scan_masks(reverse)
    groups = range(SSD_GPS)
    pairs = range(SSD_HPG // 2)
    cols = [slice(g * gw, (g + 1) * gw) for g in groups]
    xs = [xs_ref[rs, cols[g]].astype(F32) for g in groups]
    bm = [b_ref[rs, g * SSD_STATE:(g + 1) * SSD_STATE] for g in groups]
    cm = [c_ref[rs, g * SSD_STATE:(g + 1) * SSD_STATE] for g in groups]
    dt_c = [_softplus(dtc_ref[g, rs, :] + dbc_ref[g]) for g in groups]
    dt_r = [_softplus(dtr_ref[g, :, rs] + dbr_ref[g]) for g in groups]
    da_c = [dt_c[g] * (-jnp.exp(alc_ref[g])) for g in groups]
    da_r = [dt_r[g] * (-jnp.exp(alr_ref[g])) for g in groups]
    acum_c_all = _sel_dot(maskf, jnp.concatenate(da_c, axis=1), 3)
    acum_r_all = _dot_sel(jnp.concatenate(da_r, axis=0), mask_tf, 3)
    acum_c = [acum_c_all[:, g * SSD_HPG:(g + 1) * SSD_HPG] for g in groups]
    acum_r = [acum_r_all[g * SSD_HPG:(g + 1) * SSD_HPG, :] for g in groups]
    atot = [jnp.sum(da_c[g], axis=0, keepdims=True) for g in groups]

    er = lax.broadcasted_iota(jnp.int32, (SSD_HPG, gw), 0)
    ec = lax.broadcasted_iota(jnp.int32, (SSD_HPG, gw), 1)
    expand = (ec // hd == er).astype(F32)
    wide = [_dot_sel(jnp.concatenate([dt_c[g], jnp.exp(atot[g] - acum_c[g]), jnp.exp(acum_c[g]),
                                      jnp.broadcast_to(jnp.exp(atot[g]), (SUBLANES, SSD_HPG))], axis=0), expand, 1)
            for g in groups]
    xq = [xs[g] * wide[g][0:q] for g in groups]
    cb = [_dot_nt(cm[g], bm[g]) for g in groups]
    y_off = [wide[g][2 * q:3 * q] * _dot(cm[g], h_prev[g].astype(BF16)) for g in groups]
    lane = lax.broadcasted_iota(jnp.int32, (q, 2 * hd), 1)

    def pair_lhs(g, p):
        ms = [(cb[g] * jnp.where(mask, jnp.exp(acum_c[g][:, r:r + 1] - acum_r[g][r:r + 1, :]), 0.0)).astype(BF16)
              for r in (2 * p, 2 * p + 1)]
        return jnp.concatenate(ms, axis=1)

    def pair_rhs(g, p):
        xp = xq[g][:, 2 * hd * p:2 * hd * (p + 1)]
        return jnp.concatenate([jnp.where(lane < hd, xp, 0.0), jnp.where(lane >= hd, xp, 0.0)], axis=0).astype(BF16)

    y_in = [[_dot(pair_lhs(g, p), pair_rhs(g, p)) for p in pairs] for g in groups]
    y = [y_off[g] + jnp.concatenate(y_in[g], axis=1) for g in groups]
    bt = [bm[g].astype(F32).T.astype(BF16) for g in groups]
    upd = [_dot(bt[g], (xq[g] * wide[g][q:2 * q]).astype(BF16)) for g in groups]
    for g in groups:
        if final:
            ytot = y[g] + yf_ref[rs, cols[g]] + xs[g] * dsk_ref[:, cols[g]]
            gated = ytot * _silu(z_ref[rs, cols[g]].astype(F32))
            o = gated * lax.rsqrt(jnp.mean(gated * gated, axis=-1, keepdims=True) + EPS) * nw_ref[:, cols[g]]
            o_ref[rs, cols[g]] = o.astype(o_ref.dtype)
        else:
            o_ref[rs, cols[g]] = y[g]
    return [wide[g][3 * q:3 * q + 1] * h_prev[g] + upd[g] for g in groups]


def _ssd(xbc, dt_c, dt_r, db_c, db_r, al_c, al_r, *, direction, n_lat, final_args=None):
    nb, t, _ = xbc.shape
    q, gw, hpg = SCAN_CPS * SCAN_Q, SSD_GW, SSD_HPG
    assert n_lat % q == 0 and t % q == 0
    nch = t // q
    reverse = direction == 1
    cmap = _chunk_map(reverse, nch, n_lat // q)
    gps = SSD_GPS
    sw, gww = gps * SSD_STATE, gps * gw
    b0 = SSD_WIDTH // sw
    c0 = b0 + SSD_GROUPS // gps
    d = direction
    in_specs = [
        pl.BlockSpec((None, q, gww), lambda b, g, k: (b, cmap(k), g)),
        pl.BlockSpec((None, q, sw), lambda b, g, k: (b, cmap(k), b0 + g)),
        pl.BlockSpec((None, q, sw), lambda b, g, k: (b, cmap(k), c0 + g)),
        pl.BlockSpec((None, None, gps, q, hpg), lambda b, g, k: (b, d, g, cmap(k), 0)),
        pl.BlockSpec((None, None, gps, hpg, q), lambda b, g, k: (b, d, g, 0, cmap(k))),
        pl.BlockSpec((None, gps, 1, hpg), lambda b, g, k: (d, g, 0, 0)),
        pl.BlockSpec((None, gps, hpg, 1), lambda b, g, k: (d, g, 0, 0)),
        pl.BlockSpec((None, gps, 1, hpg), lambda b, g, k: (d, g, 0, 0)),
        pl.BlockSpec((None, gps, hpg, 1), lambda b, g, k: (d, g, 0, 0)),
    ]
    args = [xbc, xbc, xbc, dt_c, dt_r, db_c, db_r, al_c, al_r]
    final = final_args is not None
    if final:
        yf, pr, dskip, norm_w = final_args
        in_specs += [pl.BlockSpec((None, q, gww), lambda b, g, k: (b, cmap(k), g)),
                     pl.BlockSpec((None, q, gww), lambda b, g, k: (b, cmap(k), g)),
                     pl.BlockSpec((1, gww), lambda b, g, k: (0, g)),
                     pl.BlockSpec((1, gww), lambda b, g, k: (0, g))]
        args += [yf, pr, dskip, norm_w]
    out_shape = jax.ShapeDtypeStruct((nb, t, SSD_WIDTH), BF16 if final else F32)
    return pl.pallas_call(
        functools.partial(_ssd_body, reverse=reverse, final=final),
        out_shape=out_shape, grid=(nb, SSD_GROUPS // gps, nch), in_specs=in_specs,
        out_specs=pl.BlockSpec((None, q, gww), lambda b, g, k: (b, cmap(k), g)),
        scratch_shapes=[pltpu.VMEM((gps, SSD_STATE, gw), F32)],
        name="ssd_bwd" if reverse else "ssd_fwd",
        compiler_params=_cp(("parallel", "parallel", "arbitrary")))(*args)


def _hyfilt_body(f_ref, w1_ref, b1_ref, w2_ref, b2_ref, w3_ref, fr_ref, dl_ref, o_ref, *, tl):
    f = f_ref[...]
    h = jnp.sin(fr_ref[0:1, :] * (_dot(f, w1_ref[...], HI) + b1_ref[...]))
    h = jnp.sin(fr_ref[1:2, :] * (_dot(h, w2_ref[...], HI) + b2_ref[...]))
    h = _dot(h, w3_ref[...], HI)
    dec = jnp.exp(-f[:, 0:1] * dl_ref[...])
    row = pl.program_id(0) * tl + lax.broadcasted_iota(jnp.int32, (tl, 1), 0)
    wd = HY_WIDTH
    o_ref[:, 0:wd] = (h[:, 0:wd] * dec).astype(o_ref.dtype)
    o_ref[:, wd:2 * wd] = jnp.where(row == 0, 0.0, h[:, wd:2 * wd] * dec).astype(o_ref.dtype)


def _hyena_filters(length, w1, b1, w2, b2, w3, freq, *, tl=256):
    p = LANES
    t = jnp.linspace(0.0, 1.0, length, dtype=F32)[:, None]
    w = 2.0 * math.pi * jnp.arange(length, dtype=F32)[:, None] / length
    f = jnp.linspace(1e-4, HY_BANDS - 1, HY_BANDS, dtype=F32)[None, :]
    feats = jnp.concatenate([t, jnp.cos(f * w), -jnp.sin(f * w)], axis=-1)
    feats = jnp.concatenate([feats[0::2], feats[1::2]], axis=0)
    feats = jnp.pad(feats, ((0, 0), (0, p - HY_EMB)))
    po = p - HY_ORDER
    w1p = jnp.pad(w1, ((0, p - HY_EMB), (0, po)))
    w2p = jnp.pad(w2, ((0, po), (0, po)))
    w3p = jnp.pad(w3, ((0, po), (0, 0)))
    b1p = jnp.pad(b1[None, :], ((0, 0), (0, po)))
    b2p = jnp.pad(b2[None, :], ((0, 0), (0, po)))
    frp = jnp.pad(freq, ((0, 0), (0, po)))
    deltas = jnp.abs(jnp.linspace(math.log(HY_TARGET) / HY_FAST, math.log(HY_TARGET) / HY_SLOW, HY_WIDTH,
                                  dtype=F32))[None, :]
    full = lambda shape: pl.BlockSpec(shape, lambda i: (0,) * len(shape))
    out = pl.pallas_call(
        functools.partial(_hyfilt_body, tl=tl),
        out_shape=jax.ShapeDtypeStruct((length, 2 * HY_WIDTH), BF16), grid=(length // tl,),
        in_specs=[pl.BlockSpec((tl, p), lambda i: (i, 0)), full((p, p)), full((1, p)), full((p, p)), full((1, p)),
                  full((p, 2 * HY_WIDTH)), full((2, p)), full((1, HY_WIDTH))],
        out_specs=pl.BlockSpec((tl, 2 * HY_WIDTH), lambda i: (i, 0)),
        name="hyena_filters", compiler_params=_cp(("parallel",)))(feats, w1p, b1p, w2p, b2p, w3p, frp, deltas)
    return out[None]


def _dft_matrices(length):
    n = 2 * length
    k = jnp.arange(length, dtype=jnp.int32)[:, None]
    t = jnp.arange(length, dtype=jnp.int32)[None, :]
    step = 1 << (int(math.log2(length)) // 2)
    part = jnp.arange(length // step, dtype=jnp.int32)[:, None]
    ang_hi = ((part * step * t) % n).astype(F32) * (2.0 * math.pi / n)
    ang_lo = ((jnp.arange(step, dtype=jnp.int32)[:, None] * t) % n).astype(F32) * (2.0 * math.pi / n)
    ch, sh = jnp.cos(ang_hi)[:, None, :], jnp.sin(ang_hi)[:, None, :]
    cl, sl = jnp.cos(ang_lo)[None, :, :], jnp.sin(ang_lo)[None, :, :]
    c = (ch * cl - sh * sl).reshape(length, length)
    s = (sh * cl + ch * sl).reshape(length, length)
    alt_t = jnp.where(t % 2 == 0, 1.0, -1.0).astype(F32)
    fwd = jnp.concatenate([c, jnp.where(k == 0, alt_t, -s)], axis=0)
    alt_r = jnp.where(k % 2 == 0, 1.0, -1.0).astype(F32)
    inv_re = jnp.where(t == 0, 1.0 / n, (2.0 / n) * c)
    inv_im = jnp.where(t == 0, alt_r / n, (-2.0 / n) * s)
    inv = jnp.concatenate([inv_re, inv_im], axis=1)
    return fwd.astype(BF16)[None], inv.astype(BF16)[None]


def _tile_interleave(x, axis, hm):
    shp = x.shape
    n = shp[axis] // 2
    x = x.reshape(shp[:axis] + (2, n // hm, hm) + shp[axis + 1:])
    return jnp.swapaxes(x, axis, axis + 1).reshape(shp)


def _twiddle_table(length, hm):
    k = np.arange(length // 2, dtype=np.float64)[:, None] * (np.pi / length)
    tab = np.concatenate([np.cos(k), np.sin(k)], axis=0)
    tab = tab.reshape(2, -1, hm, 1).swapaxes(0, 1).reshape(length, 1)
    return jnp.asarray(np.broadcast_to(tab, (length, LANES)).astype(np.float32))


def _lanes(x, width):
    return jnp.tile(x, (1, width // LANES))


def _parity_banks(e, ot, tw, hm):
    er, ei, otr, oti = e[0:hm], e[hm:], ot[0:hm], ot[hm:]
    c, s = _lanes(tw[0:hm], e.shape[1]), _lanes(tw[hm:], e.shape[1])
    orr = otr * c + oti * s
    oi = oti * c - otr * s
    return (er + orr, ei + oi, er - orr, ei - oi), (c, s)


def _is_bin0(row_blk, hm):
    return (row_blk == 0) & (lax.broadcasted_iota(jnp.int32, (hm, 1), 0) == 0)


def _hy_kbank_body(f_ref, hef_ref, heb_ref, hof_ref, hob_ref, tw_ref, ka_ref, kb_ref, *, hm):
    fwd, tw = f_ref[...], tw_ref[...]
    ef, eb, of, ob = (_dot(fwd, r[...]) for r in (hef_ref, heb_ref, hof_ref, hob_ref))
    (afr, afi, bfr, bfi), _ = _parity_banks(ef, of, tw, hm)
    (abr, abi, bbr, bbi), _ = _parity_banks(eb, ob, tw, hm)
    first = _is_bin0(pl.program_id(0), hm)
    kai = jnp.where(first, ef[hm:] + eb[hm:], afi - abi)
    kbi = jnp.where(first, ob[hm:] - of[hm:], bfi - bbi)
    ka_ref[...] = jnp.concatenate([afr + abr, kai], axis=0)
    kb_ref[...] = jnp.concatenate([bfr + bbr, kbi], axis=0)


def _hy_kbank(fwd, hfb, tw, *, tm, tc=512):
    _, ln, c2 = hfb.shape
    c, h = c2 // 2, ln // 2
    taps = lambda row_blk, col_off: pl.BlockSpec((None, h, tc), lambda i, j: (0, row_blk, j + col_off))
    bank = pl.BlockSpec((None, tm, tc), lambda i, j: (0, i, j))
    return pl.pallas_call(
        functools.partial(_hy_kbank_body, hm=tm // 2),
        out_shape=(jax.ShapeDtypeStruct((1, ln, c), F32),) * 2, grid=(ln // tm, c // tc),
        in_specs=[pl.BlockSpec((None, tm, h), lambda i, j: (0, i, 0)),
                  taps(0, 0), taps(0, c // tc), taps(1, 0), taps(1, c // tc),
                  pl.BlockSpec((tm, LANES), lambda i, j: (i, 0))],
        out_specs=(bank, bank), name="hy_kbank",
        compiler_params=_cp(("parallel", "parallel")))(fwd, hfb, hfb, hfb, hfb, tw)


def _hy_spectrum_body(f_ref, ze_ref, zo_ref, tw_ref, ka_ref, kb_ref, p_ref, q_ref, *, hm):
    fwd = f_ref[...]
    e, acc = _dot(fwd, ze_ref[...]), _dot(fwd, zo_ref[...])
    row_blk = pl.program_id(2)
    (ar, ai, br, bi), (c, s) = _parity_banks(e, acc, tw_ref[...], hm)
    kar, kai, kbr, kbi = ka_ref[0:hm], ka_ref[hm:], kb_ref[0:hm], kb_ref[hm:]
    yar, yai = ar * kar - ai * kai, ar * kai + ai * kar
    ybr, ybi = br * kbr - bi * kbi, br * kbi + bi * kbr
    qr, qi = yar - ybr, yai - ybi
    pr, pi = yar + ybr, yai + ybi
    qpr, qpi = qr * c - qi * s, qi * c + qr * s
    top = slice(0, SUBLANES)
    first = _is_bin0(row_blk, SUBLANES)
    y0, yl = ar[top] * kar[top], br[top] * kbr[top]
    amr, ami = e[hm:hm + SUBLANES], -acc[hm:hm + SUBLANES]
    ymr, ymi = amr * kai[top] - ami * kbi[top], amr * kbi[top] + ami * kai[top]

    def patch(full, special):
        return jnp.concatenate([jnp.where(first, special, full[top]), full[SUBLANES:]], axis=0)

    pr, pi = patch(pr, y0 + yl), patch(pi, 2.0 * ymr)
    qpr, qpi = patch(qpr, y0 - yl), patch(qpi, -2.0 * ymi)
    p_ref[...] = jnp.concatenate([pr, pi], axis=0).astype(p_ref.dtype)
    q_ref[...] = jnp.concatenate([qpr, qpi], axis=0).astype(q_ref.dtype)


def _hy_spectrum(fwd, ze, zo, tw, ka, kb, *, tm, tn):
    nb, h, c = ze.shape
    ln = fwd.shape[1]
    data = pl.BlockSpec((None, h, tn), lambda j, b, i: (b, 0, j))
    bank = pl.BlockSpec((None, tm, tn), lambda j, b, i: (0, i, j))
    out = pl.BlockSpec((None, tm, tn), lambda j, b, i: (b, i, j))
    return pl.pallas_call(
        functools.partial(_hy_spectrum_body, hm=tm // 2),
        out_shape=(jax.ShapeDtypeStruct((nb, ln, c), BF16),) * 2, grid=(c // tn, nb, ln // tm),
        in_specs=[pl.BlockSpec((None, tm, h), lambda j, b, i: (0, i, 0)), data, data,
                  pl.BlockSpec((tm, LANES), lambda j, b, i: (i, 0)), bank, bank],
        out_specs=(out, out), name="hy_spectrum",
        compiler_params=_cp(("parallel", "parallel", "parallel")))(fwd, ze, zo, tw, ka, kb)


def _hyena_out_body(*refs, segs):
    ns = len(segs)
    x0_ref, db_ref, o_ref = refs[3 * ns:3 * ns + 3]
    ys_refs = refs[3 * ns + 3:]
    half = CONV_CH // 2
    for si, (r0, n) in enumerate(segs):
        ye_ref, yo_ref, zin_ref = refs[3 * si:3 * si + 3]

        def body(i, carry, r0=r0, ye_ref=ye_ref, yo_ref=yo_ref, zin_ref=zin_ref):
            t0 = pl.multiple_of(i * CONV_CH, CONV_CH)
            hrows = pl.ds(pl.multiple_of(i * half, half), half)
            for g, ys_ref in enumerate(ys_refs):
                lanes = slice(g * LANES, (g + 1) * LANES)
                ys_ref[pl.ds(r0 + t0, half, stride=2), :] = ye_ref[hrows, lanes].astype(F32)
                ys_ref[pl.ds(r0 + t0 + 1, half, stride=2), :] = yo_ref[hrows, lanes].astype(F32)
            rows = pl.ds(r0 + t0, CONV_CH)
            y = jnp.concatenate([ys_ref[rows, :] for ys_ref in ys_refs], axis=1)
            zin = zin_ref[pl.ds(t0, CONV_CH), :].astype(F32)
            o_ref[rows, :] = (x0_ref[rows, :].astype(F32) * (y + zin * db_ref[...])).astype(o_ref.dtype)
            return carry

        lax.fori_loop(0, n // CONV_CH, body, 0, unroll=min(4, n // CONV_CH))


def _hyena_out(parts, x0, dbias, *, segs, tc=256):
    nb, t, c = x0.shape
    in_specs, args = [], []
    for (ye, yo, zin), (_, n) in zip(parts, segs):
        in_specs += [pl.BlockSpec((None, n // 2, tc), lambda b, j: (b, 0, j)),
                     pl.BlockSpec((None, n // 2, tc), lambda b, j: (b, 0, j)),
                     pl.BlockSpec((None, n, tc), lambda b, j: (b, 0, j))]
        args += [ye, yo, zin]
    in_specs += [pl.BlockSpec((None, t, tc), lambda b, j: (b, 0, j)), pl.BlockSpec((1, tc), lambda b, j: (0, j))]
    return pl.pallas_call(
        functools.partial(_hyena_out_body, segs=segs),
        out_shape=jax.ShapeDtypeStruct((nb, t, c), BF16), grid=(nb, c // tc), in_specs=in_specs,
        out_specs=pl.BlockSpec((None, t, tc), lambda b, j: (b, 0, j)),
        scratch_shapes=[pltpu.VMEM((t, LANES), F32)] * (tc // LANES), name="hyena_out",
        compiler_params=_cp(("parallel", "parallel")))(*args, x0, dbias)


def _hyena_longconv(ze, zo, hfb, *, tn=1024):
    nb, h, c = ze.shape
    ln = 2 * h
    tm_s = min(512, ln)
    hm = tm_s // 2
    fwd, inv = _dft_matrices(h)
    fwd = _tile_interleave(fwd, 1, hm)
    inv = _tile_interleave(inv * 0.5, 2, hm)
    tw = _twiddle_table(ln, hm)
    ka, kb = _hy_kbank(fwd, hfb, tw, tm=tm_s)
    p, q = _hy_spectrum(fwd, ze, zo, tw, ka, kb, tm=tm_s, tn=tn)
    tm_i, tk_i = min(1024, h), ln
    ye = _matmul(inv, p, tm=tm_i, tn=tn, tk=tk_i, out_dtype=BF16, name="hy_idft_even")
    yo = _matmul(inv, q, tm=tm_i, tn=tn, tk=tk_i, out_dtype=BF16, name="hy_idft_odd")
    return ye, yo


def _mlstm_body(*refs, reverse, final):
    q_ref, k_ref, v_ref, gc_ref, gr_ref = refs[:5]
    if final:
        hf_ref, og_ref, nw_ref, o_ref, ct_ref, n_ref, m_ref = refs[5:]
    else:
        o_ref, ct_ref, n_ref, m_ref = refs[5:]
    nh, dk, dv = ML_HEADS, ML_QK_DIM, ML_V_DIM
    scale = dk ** -0.5

    @pl.when(pl.program_id(1) == 0)
    def _():
        ct_ref[...] = jnp.zeros_like(ct_ref)
        n_ref[...] = jnp.zeros_like(n_ref)
        m_ref[...] = jnp.zeros_like(m_ref)

    state = ([ct_ref[h] for h in range(nh)], n_ref[...], m_ref[...])
    for c in (reversed(range(SCAN_CPS)) if reverse else range(SCAN_CPS)):
        state = _mlstm_chunk(refs, slice(c * SCAN_Q, (c + 1) * SCAN_Q), state, reverse=reverse, final=final)
    for h in range(nh):
        ct_ref[h] = state[0][h]
    n_ref[...] = state[1]
    m_ref[...] = state[2]


def _mlstm_chunk(refs, rs, state, *, reverse, final):
    q_ref, k_ref, v_ref, gc_ref, gr_ref = refs[:5]
    if final:
        hf_ref, og_ref, nw_ref, o_ref = refs[5:9]
    else:
        o_ref = refs[5]
    nh, dk, dv = ML_HEADS, ML_QK_DIM, ML_V_DIM
    scale = dk ** -0.5
    cts, n_all, m_all = state
    gc, gr = gc_ref[rs, :], gr_ref[:, rs]
    li_c, lf_c = gc[:, 0:nh], _log_sigmoid(gc[:, nh:2 * nh])
    li_r, lf_r = gr[0:nh, :], _log_sigmoid(gr[nh:2 * nh, :])
    mask, maskf, mask_tf = _scan_masks(reverse)
    bc_c = _dot(maskf, lf_c, HI)
    bc_r = _dot(lf_r, mask_tf, HI)
    btot_c = jnp.sum(lf_c, axis=0, keepdims=True)
    heads = range(nh)
    qs = [q_ref[rs, dk * h:dk * (h + 1)] for h in heads]
    ks = [k_ref[rs, dk * h:dk * (h + 1)] for h in heads]
    vs = [v_ref[rs, dv * h:dv * (h + 1)] for h in heads]
    m_s = [m_all[h:h + 1, 0:1] for h in heads]
    n_s = [n_all[h:h + 1, :] for h in heads]
    qk = [_dot_nt(qs[h], ks[h]) for h in heads]
    qc = [_dot(qs[h], cts[h].astype(BF16)) for h in heads]
    dmat = [jnp.where(mask, bc_c[:, h:h + 1] - bc_r[h:h + 1, :] + li_r[h:h + 1, :], -jnp.inf) for h in heads]
    inter = [bc_c[:, h:h + 1] + m_s[h] for h in heads]
    m_t = [jnp.maximum(inter[h], jnp.max(dmat[h], axis=1, keepdims=True)) for h in heads]
    s = [qk[h] * scale * jnp.exp(dmat[h] - m_t[h]) for h in heads]
    dec = [jnp.exp(inter[h] - m_t[h]) for h in heads]
    num = [_dot(s[h].astype(BF16), vs[h]) + (dec[h] * scale) * qc[h] for h in heads]
    qn = [jnp.sum(qs[h].astype(F32) * n_s[h], axis=1, keepdims=True) * scale for h in heads]
    den = [jnp.sum(s[h], axis=1, keepdims=True) + dec[h] * qn[h] for h in heads]
    hh = [num[h] / jnp.maximum(jnp.abs(den[h]), jnp.exp(-m_t[h])) for h in heads]

    wst_c = [btot_c[:, h:h + 1] - bc_c[:, h:h + 1] + li_c[:, h:h + 1] for h in heads]
    wst_r = [btot_c[:, h:h + 1] - bc_r[h:h + 1, :] + li_r[h:h + 1, :] for h in heads]
    m_new = [jnp.maximum(btot_c[:, h:h + 1] + m_s[h], jnp.max(wst_r[h], axis=1, keepdims=True)) for h in heads]
    dstate = [jnp.exp(btot_c[:, h:h + 1] + m_s[h] - m_new[h]) for h in heads]
    kw = [ks[h].astype(F32) * jnp.exp(wst_c[h] - m_new[h]) for h in heads]
    upd = [_dot(kw[h].T.astype(BF16), vs[h]) for h in heads]
    n_rows = [dstate[h] * n_s[h] + jnp.sum(kw[h], axis=0, keepdims=True) for h in heads]
    m_rows = [jnp.broadcast_to(m_new[h], (1, LANES)) for h in heads]

    for h in heads:
        cols = slice(dv * h, dv * (h + 1))
        if final:
            hs = hh[h] + hf_ref[rs, cols]
            hn = hs * lax.rsqrt(jnp.mean(hs * hs, axis=-1, keepdims=True) + EPS) * nw_ref[:, cols]
            o_ref[rs, cols] = (hn * _sigmoid(og_ref[rs, cols].astype(F32))).astype(o_ref.dtype)
        else:
            o_ref[rs, cols] = hh[h]

    return ([dstate[h] * cts[h] + upd[h] for h in heads], jnp.concatenate(n_rows, axis=0),
            jnp.concatenate(m_rows, axis=0))


def _mlstm(qk, pr, g_c, g_r, *, direction, n_lat, final_args=None):
    nb, t, _ = qk.shape
    q = SCAN_CPS * SCAN_Q
    assert n_lat % q == 0 and t % q == 0
    nch = t // q
    reverse = direction == 1
    cmap = _chunk_map(reverse, nch, n_lat // q)
    d = direction
    in_specs = [
        pl.BlockSpec((None, q, ML_QK_WIDTH), lambda b, k: (b, cmap(k), 0)),
        pl.BlockSpec((None, q, ML_QK_WIDTH), lambda b, k: (b, cmap(k), 1)),
        pl.BlockSpec((None, q, ML_WIDTH), lambda b, k: (b, cmap(k), 2 * ML_QK_WIDTH // ML_WIDTH)),
        pl.BlockSpec((None, None, q, 2 * ML_HEADS), lambda b, k: (b, d, cmap(k), 0)),
        pl.BlockSpec((None, None, 2 * ML_HEADS, q), lambda b, k: (b, d, 0, cmap(k))),
    ]
    args = [qk, qk, pr, g_c, g_r]
    final = final_args is not None
    if final:
        hf, norm_w = final_args
        in_specs += [pl.BlockSpec((None, q, ML_WIDTH), lambda b, k: (b, cmap(k), 0)),
                     pl.BlockSpec((None, q, ML_WIDTH), lambda b, k: (b, cmap(k), 2 * ML_QK_WIDTH // ML_WIDTH + 1)),
                     pl.BlockSpec((1, ML_WIDTH), lambda b, k: (0, 0))]
        args += [hf, pr, norm_w]
    out_shape = jax.ShapeDtypeStruct((nb, t, ML_WIDTH), BF16 if final else F32)
    return pl.pallas_call(
        functools.partial(_mlstm_body, reverse=reverse, final=final),
        out_shape=out_shape, grid=(nb, nch), in_specs=in_specs,
        out_specs=pl.BlockSpec((None, q, ML_WIDTH), lambda b, k: (b, cmap(k), 0)),
        scratch_shapes=[pltpu.VMEM((ML_HEADS, ML_QK_DIM, ML_V_DIM), F32), pltpu.VMEM((ML_HEADS, ML_QK_DIM), F32),
                        pltpu.VMEM((ML_HEADS, LANES), F32)],
        name="mlstm_bwd" if reverse else "mlstm_fwd",
        compiler_params=_cp(("parallel", "arbitrary")))(*args)


def _na_window_start(j, rows):
    return jnp.clip(j * NA_QROWS - NA_ROWS // 2, 0, rows - NA_KROWS)


def _na_part_offset(j, part, start, clip=jnp.clip):
    first = j * NA_QROWS + part * (NA_QROWS // NA_PARTS) - NA_ROWS // 2
    return clip(first - start, 0, NA_KROWS - NA_PART_KROWS) // NA_TAB_ROWS * NA_TAB_ROWS


def _na_body(q_ref, k_ref, v_ref, tab_ref, qw_ref, kw_ref, o_ref, kn_ref, v1_ref, *, n_lat, n_ctx):
    dh = NA_HEAD_DIM
    scale = dh ** -0.5
    rows = n_lat // GRID_W
    j = pl.program_id(2)
    heads = range(NA_HPS)
    hcols = [slice(hh * dh, (hh + 1) * dh) for hh in heads]

    @pl.when(j == 0)
    def _():
        prep = (n_lat + n_ctx) // NA_PREP_STEPS
        one_hot = (lax.broadcasted_iota(jnp.int32, (prep, LANES), 1) == 0).astype(v1_ref.dtype)

        def body(i, carry):
            r = pl.ds(pl.multiple_of(i * prep, 16), prep)
            for hh in heads:
                kk = k_ref[r, hcols[hh]].astype(F32)
                kn = kk * lax.rsqrt(jnp.mean(kk * kk, axis=-1, keepdims=True) + EPS) * kw_ref[...]
                kn_ref[r, hcols[hh]] = kn.astype(kn_ref.dtype)
                v1_ref[hh, r, 0:dh] = v_ref[r, hcols[hh]]
                v1_ref[hh, r, dh:dh + LANES] = one_hot
            return carry

        lax.fori_loop(0, NA_PREP_STEPS, body, 0)

    start = _na_window_start(j, rows)
    ctx = pl.ds(n_lat, n_ctx)
    rpp = NA_QROWS // NA_PARTS
    pq = rpp * GRID_W
    chains = [(hh, i) for hh in heads for i in range(NA_PARTS)]
    off = [_na_part_offset(j, i, start) for i in range(NA_PARTS)]
    win = [pl.ds(pl.multiple_of((start + off[i]) * GRID_W, GRID_W), NA_PART_KROWS * GRID_W) for i in range(NA_PARTS)]
    qq = [q_ref[i * pq:(i + 1) * pq, hcols[hh]].astype(F32) for hh, i in chains]
    qn = [(x * lax.rsqrt(jnp.mean(x * x, axis=-1, keepdims=True) + EPS) * (qw_ref[...] * scale)).astype(BF16)
          for x in qq]
    sw = [_dot_nt(qn[c], kn_ref[win[i], hcols[hh]]) for c, (hh, i) in enumerate(chains)]
    sc = [_dot_nt(qn[c], kn_ref[ctx, hcols[hh]]) for c, (hh, i) in enumerate(chains)]

    def with_bias(c, hh, i):
        out = []
        for r in range(rpp):
            d0 = start - (j * NA_QROWS + i * rpp + r) + NA_ROWS - 1 + NA_QROWS
            blocks = tab_ref[hh, d0, pl.ds(off[i] // NA_TAB_ROWS, NA_PART_KROWS // NA_TAB_ROWS)]
            bias = jnp.concatenate([blocks[b] for b in range(NA_PART_KROWS // NA_TAB_ROWS)], axis=1)
            out.append(sw[c][r * GRID_W:(r + 1) * GRID_W] + bias)
        return jnp.concatenate(out, axis=0)

    sb = [with_bias(c, hh, i) for c, (hh, i) in enumerate(chains)]
    m = [jnp.maximum(jnp.max(a, axis=1, keepdims=True), jnp.max(b, axis=1, keepdims=True)) for a, b in zip(sb, sc)]
    pw = [jnp.exp(a - mm) for a, mm in zip(sb, m)]
    pc = [jnp.exp(a - mm) for a, mm in zip(sc, m)]
    o = [_dot(pw[c].astype(BF16), v1_ref[hh, win[i], :]) + _dot(pc[c].astype(BF16), v1_ref[hh, ctx, :])
         for c, (hh, i) in enumerate(chains)]
    for c, (hh, i) in enumerate(chains):
        o_ref[i * pq:(i + 1) * pq, hcols[hh]] = (o[c][:, 0:dh] / o[c][:, dh:dh + 1]).astype(o_ref.dtype)


def _na_bias_table(rpb, rows):
    nd = 2 * NA_QROWS
    valid = np.zeros((nd, NA_KROWS), bool)
    seen = np.zeros((nd,), bool)
    for j in range(rows // NA_QROWS):
        start = int(np.clip(j * NA_QROWS - NA_ROWS // 2, 0, rows - NA_KROWS))
        for qr in range(NA_QROWS):
            r = j * NA_QROWS + qr
            rs = int(np.clip(r - NA_ROWS // 2, 0, rows - NA_ROWS))
            d = start - r + NA_ROWS - 1 + NA_QROWS
            v = np.array([(rs <= start + i < rs + NA_ROWS) for i in range(NA_KROWS)])
            assert 0 <= d < nd and (not seen[d] or (valid[d] == v).all())
            valid[d], seen[d] = v, True
            off = int(_na_part_offset(j, qr // (NA_QROWS // NA_PARTS), start, clip=np.clip))
            assert off % NA_TAB_ROWS == 0 and not v[:off].any() and not v[off + NA_PART_KROWS:].any()
    col = np.arange(GRID_W)
    cs = np.clip(col - NA_COLS // 2, 0, GRID_W - NA_COLS)
    in_win = (col[None, :] >= cs[:, None]) & (col[None, :] < cs[:, None] + NA_COLS)
    ndr, ndc, w = 2 * NA_ROWS - 1, 2 * NA_COLS - 1, GRID_W
    dr = (np.arange(nd)[:, None] - NA_QROWS) + np.arange(NA_KROWS)[None, :]
    ok = valid[:, None, :, None] & in_win[None, :, None, :] & (dr >= 0)[:, None, :, None] \
        & (dr < ndr)[:, None, :, None]
    ext = jnp.pad(rpb, ((0, 0), (0, 0), (w, w)), mode="edge")
    c0 = w + NA_COLS - 1
    t1 = jnp.stack([ext[:, :, c0 - qc:c0 - qc + w] for qc in range(w)], axis=2)
    t1p = jnp.pad(t1, ((0, 0), (NA_QROWS, nd + NA_KROWS - 1 - NA_QROWS - ndr), (0, 0), (0, 0)))
    wide = jnp.concatenate([t1p[:, m] for m in range(nd + NA_KROWS - 1)], axis=-1)
    wpad = LANES * pl.cdiv(wide.shape[-1], LANES)
    wide = jnp.pad(wide, ((0, 0), (0, 0), (0, wpad - wide.shape[-1])))
    nblk, bw = NA_KROWS // NA_TAB_ROWS, NA_TAB_ROWS * w
    okf = jnp.asarray(ok.reshape(nd, w, nblk, bw).transpose(0, 2, 1, 3).astype(np.float32))

    def body(wide_ref, ok_ref, o_ref):
        for d in range(nd):
            for b in range(nblk):
                lo = d * w + b * bw
                o_ref[d, b] = jnp.where(ok_ref[d, b] != 0.0, wide_ref[:, lo:lo + bw], -jnp.inf)

    return pl.pallas_call(
        body, out_shape=jax.ShapeDtypeStruct((rpb.shape[0], nd, nblk, w, bw), F32), grid=(rpb.shape[0],),
        in_specs=[pl.BlockSpec((None, w, wpad), lambda h: (h, 0, 0)),
                  pl.BlockSpec((nd, nblk, w, bw), lambda h: (0, 0, 0, 0))],
        out_specs=pl.BlockSpec((None, nd, nblk, w, bw), lambda h: (h, 0, 0, 0, 0)),
        name="na_bias_table", compiler_params=_cp(("parallel",)))(wide, okf)


def _na(pr, tab, q_w, k_w, *, n_lat, n_ctx, qcol):
    nb, t, _ = pr.shape
    dh, nh, hps = NA_HEAD_DIM, NA_HEADS, NA_HPS
    nq = NA_QROWS * GRID_W
    hw = hps * dh
    assert qcol % hw == 0 and nh % hps == 0
    qb = qcol // hw
    return pl.pallas_call(
        functools.partial(_na_body, n_lat=n_lat, n_ctx=n_ctx),
        out_shape=jax.ShapeDtypeStruct((nb, n_lat, NA_WIDTH), BF16), grid=(nb, nh // hps, n_lat // nq),
        in_specs=[pl.BlockSpec((None, nq, hw), lambda b, h, j: (b, j, qb + h)),
                  pl.BlockSpec((None, t, hw), lambda b, h, j: (b, 0, qb + nh // hps + h)),
                  pl.BlockSpec((None, t, hw), lambda b, h, j: (b, 0, qb + 2 * (nh // hps) + h)),
                  pl.BlockSpec((hps,) + tab.shape[1:], lambda b, h, j: (h, 0, 0, 0, 0)),
                  pl.BlockSpec((1, dh), lambda b, h, j: (0, 0)),
                  pl.BlockSpec((1, dh), lambda b, h, j: (0, 0))],
        out_specs=pl.BlockSpec((None, nq, hw), lambda b, h, j: (b, j, h)),
        scratch_shapes=[pltpu.VMEM((t, hw), BF16), pltpu.VMEM((hps, t, dh + LANES), BF16)], name="na",
        compiler_params=_cp(("parallel", "parallel", "arbitrary")))(pr, pr, pr, tab, q_w, k_w)


def _rope_tables(n_lat, n_ctx, width):
    dh = ML_QK_DIM
    nf = dh // 4
    t = jnp.arange(n_lat)
    inv = ROPE_THETA ** (-jnp.arange(nf, dtype=F32) / nf)
    lane = np.arange(dh)
    pos = jnp.where((lane < dh // 2)[None, :], (t // GRID_W)[:, None], (t % GRID_W)[:, None]).astype(F32)
    ang = pos * inv[lane % nf][None, :]
    sign = np.where((lane % (dh // 2)) < nf, -1.0, 1.0).astype(np.float32)
    cos = jnp.concatenate([jnp.cos(ang), jnp.ones((n_ctx, dh), F32)], axis=0)
    sin = jnp.concatenate([jnp.sin(ang) * sign[None, :], jnp.zeros((n_ctx, dh), F32)], axis=0)
    reps = width // dh
    return jnp.tile(cos, (1, reps)), jnp.tile(sin, (1, reps))


def _ada_mod(c, c_ctx, w_all, b_all, layer):
    nb, d = c.shape
    rows = 16
    cv = jnp.concatenate([c, c_ctx[None, :], jnp.zeros((rows - nb - 1, d), F32)], axis=0)
    a = (cv * _sigmoid(cv)).astype(BF16)[None]
    out = _matmul(a, w_all, w_index=layer, tm=rows, tn=1024, tk=d, out_dtype=F32, name="ada_mod")[0]
    out = out[:nb + 1] + b_all[layer][None, :]
    return out.reshape(nb + 1, 6, d).transpose(1, 0, 2)


def _lat_ctx(m, nb):
    ctx = jnp.broadcast_to(m[nb][None, :], (nb, m.shape[1]))
    return jnp.stack([m[:nb], ctx], axis=1)[:, :, None, :]


def _pad_cols(w, n):
    return jnp.pad(w, ((0, 0), (0, n - w.shape[1])))


def _col_row_forms(g, groups):
    nb, t, _ = g.shape
    g5 = g.reshape(nb, t, 2, groups, -1)
    return g5.transpose(0, 2, 3, 1, 4), g5.transpose(0, 2, 3, 4, 1)


ROW_TILES = (1088, 1024, 512, 256)


def _row_tile(rows):
    return next(tm for tm in ROW_TILES if rows % tm == 0)


def _mod_pair(mods, shift, scale, nb):
    return jnp.stack([_lat_ctx(mods[shift], nb), _lat_ctx(mods[scale], nb)], axis=1)


def _in_projection(u, w_all, layer, *, rows, name, cols=None, w_nk=False):
    tm = _row_tile(rows)
    tm = 2 * tm if rows % (2 * tm) == 0 else tm
    return _matmul(u, w_all, w_index=layer, w_nk=w_nk, cols=cols, tm=tm, tn=1024, tk=u.shape[2], out_dtype=BF16,
                   rows=rows, vmem=VMEM_LIMIT_BIG, name=name)


def _out_projection(a_list, w_all, x, gate, layer, *, rows, n_lat, next_norm, name):
    tm = _row_tile(rows)
    tn = 512
    x = _matmul_gated_residual_rows(a_list, w_all, x, gate, w_index=layer, tm=tm, tn=tn, rows=rows, n_lat=n_lat,
                                    name=name)
    if next_norm is None:
        return x, None
    return x, _rmsmod(x, *next_norm, tm=tm, rows=rows, n_lat=n_lat)


def _conv_ffn_block(x, u, mods, w_up_all, conv_w, conv_b, w_down_all, layer, *, rows, n_lat, segs, next_norm):
    nb = x.shape[0]
    up = _in_projection(u, w_up_all, layer, rows=rows, name="ffn_up")
    hid = _conv_ffn(up, conv_w, conv_b[None, :], segs=segs, rows=rows)
    return _out_projection((hid,), w_down_all, x, _lat_ctx(mods[5], nb), layer, rows=rows, n_lat=n_lat,
                           next_norm=next_norm, name="ffn_down")


def _even_layer(x, u, mods, w_in_all, conv_w, conv_b, dt_bias, a_log, d_skip, ssd_norm_w,
                hy_w1, hy_b1, hy_w2, hy_b2, hy_w3, hy_freq, hy_bias, w_out_all, j, *, n_lat, n_ctx, next_norm):
    nb, t, d = x.shape
    segs = ((0, n_lat), (n_lat, n_ctx))
    tm = _row_tile(t)
    n_main = SSD_WIDTH + EV_CONV_CH
    pr = _in_projection(u, jnp.swapaxes(w_in_all, 1, 2), j, rows=t, cols=n_main, w_nk=True, name="ev_in")
    w_dt = _pad_cols(w_in_all[j, :, n_main:], LANES).astype(BF16)
    dtp = _matmul(u, w_dt[None], tm=tm, tn=LANES, tk=d, out_dtype=F32, name="ev_in_dt")[:, :, :2 * SSD_HEADS]
    cb = conv_b[None, :]
    xbc = _conv_silu(pr, conv_w, cb, xcol=SSD_WIDTH, wcol=0, width=SSD_XBC, segs=segs)
    x0, zin_l, zin_c, zle, zlo, zce, zco = _conv_hyena(pr, conv_w, cb, xcol=SSD_WIDTH + SSD_XBC, wcol=SSD_XBC,
                                                       segs=segs)

    dt_c, dt_r = _col_row_forms(dtp, SSD_GROUPS)
    db = dt_bias.reshape(2, SSD_GROUPS, 1, SSD_HPG)
    al = a_log.reshape(2, SSD_GROUPS, 1, SSD_HPG)
    scan_args = (xbc, dt_c, dt_r, db, db.transpose(0, 1, 3, 2), al, al.transpose(0, 1, 3, 2))
    yf = _ssd(*scan_args, direction=0, n_lat=n_lat)
    dsk = jnp.repeat(d_skip, SSD_HEAD_DIM)[None, :]
    y_ssd = _ssd(*scan_args, direction=1, n_lat=n_lat, final_args=(yf, pr, dsk, ssd_norm_w[None, :]))

    hyb = hy_bias[None, :]
    filt = functools.partial(_hyena_filters, w1=hy_w1, b1=hy_b1, w2=hy_w2, b2=hy_b2, w3=hy_w3, freq=hy_freq)
    lat = _hyena_longconv(zle, zlo, filt(n_lat)) + (zin_l,)
    ctx = _hyena_longconv(zce, zco, filt(n_ctx)) + (zin_c,)
    y_hy = _hyena_out((lat, ctx), x0, hyb, segs=segs)
    return _out_projection((y_ssd, y_hy), w_out_all, x, _lat_ctx(mods[2], nb), j, rows=t, n_lat=n_lat,
                           next_norm=next_norm, name="ev_out")


def _odd_layer_latent(x, u, mods, w_in_all, conv_w, conv_b, gate_b, ml_norm_w, q_norm_w, k_norm_w, rpb,
                      w_out_all, j, *, n_lat, n_ctx, next_norm):
    nb, t, d = x.shape
    segs = ((0, n_lat), (n_lat, n_ctx))
    tm = _row_tile(t)
    o3 = 2 * ML_QK_WIDTH + 2 * ML_WIDTH
    o4 = o3 + 4 * ML_HEADS
    w_nk = jnp.swapaxes(w_in_all, 1, 2)
    pr = _in_projection(u, w_nk, j, rows=t, cols=o3, w_nk=True, name="od_in_mlstm")
    pr_na = _in_projection(u, w_nk[j:j + 1, o4:, :], 0, rows=t, w_nk=True, name="od_in_na")
    gates = _matmul(u, _pad_cols(w_in_all[j, :, o3:o4], LANES).astype(BF16)[None], tm=tm, tn=LANES, tk=d,
                    out_dtype=F32, name="od_in_gates")[:, :, :4 * ML_HEADS] + gate_b.reshape(-1)
    cos, sin = _rope_tables(n_lat, n_ctx, 256)
    qk = _conv_rope(pr, conv_w, conv_b[None, :], cos, sin, width=2 * ML_QK_WIDTH, segs=segs)
    g5 = gates.reshape(nb, t, 2, 2 * ML_HEADS)
    g_c, g_r = g5.transpose(0, 2, 1, 3), g5.transpose(0, 2, 3, 1)
    hf = _mlstm(qk, pr, g_c, g_r, direction=0, n_lat=n_lat)
    y_ml = _mlstm(qk, pr, g_c, g_r, direction=1, n_lat=n_lat, final_args=(hf, ml_norm_w[None, :]))
    tab = _na_bias_table(rpb, n_lat // GRID_W)
    y_na = _na(pr_na, tab, q_norm_w[None, :], k_norm_w[None, :], n_lat=n_lat, n_ctx=n_ctx, qcol=0)
    return _out_projection((y_ml, y_na), w_out_all, x, _lat_ctx(mods[2], nb), j, rows=n_lat, n_lat=n_lat,
                           next_norm=next_norm, name="od_out")


def kernel(x, c, ctx, c_ctx, ada_w, ada_b, norm_w, ev_w_in, ev_conv_w, ev_conv_b, ssd_dt_bias, ssd_a_log, ssd_d,
           ssd_norm_w, hy_w1, hy_b1, hy_w2, hy_b2, hy_w3, hy_freq, hy_bias, ev_w_out, od_w_in, ml_conv_w, ml_conv_b,
           ml_gate_b, ml_norm_w, na_q_norm_w, na_k_norm_w, na_rpb, od_w_out, ffn_w_up, ffn_conv_w, ffn_conv_b,
           ffn_w_down):
    depth = ada_w.shape[0]
    assert depth == 2, "layer schedule below is written for one even and one odd layer"
    n_lat, n_ctx = x.shape[1], ctx.shape[1]
    t = n_lat + n_ctx
    both = ((0, n_lat), (n_lat, n_ctx))
    xs = jnp.concatenate([x, ctx], axis=1)

    nb = x.shape[0]
    w_down = ffn_w_down.astype(BF16)
    mods0 = _ada_mod(c, c_ctx, ada_w, ada_b, 0)
    mods1 = _ada_mod(c, c_ctx, ada_w, ada_b, 1)
    norm = lambda layer, sub, mods: (norm_w[layer, sub][None, :], _mod_pair(mods, 3 * sub, 3 * sub + 1, nb))

    w0, mod0 = norm(0, 0, mods0)
    u = _rmsmod(xs, w0, mod0, tm=_row_tile(t), rows=t, n_lat=n_lat)
    xs, u = _even_layer(xs, u, mods0, ev_w_in, ev_conv_w[0], ev_conv_b[0], ssd_dt_bias[0], ssd_a_log[0],
                        ssd_d[0], ssd_norm_w[0], hy_w1[0], hy_b1[0], hy_w2[0], hy_b2[0], hy_w3[0], hy_freq[0],
                        hy_bias[0], ev_w_out.astype(BF16), 0, n_lat=n_lat, n_ctx=n_ctx, next_norm=norm(0, 1, mods0))
    xs, u = _conv_ffn_block(xs, u, mods0, ffn_w_up, ffn_conv_w[0], ffn_conv_b[0], w_down, 0,
                            rows=t, n_lat=n_lat, segs=both, next_norm=norm(1, 0, mods1))
    xl, u = _odd_layer_latent(xs, u, mods1, od_w_in, ml_conv_w[0], ml_conv_b[0], ml_gate_b[0], ml_norm_w[0],
                              na_q_norm_w[0], na_k_norm_w[0], na_rpb[0], od_w_out.astype(BF16), 0,
                              n_lat=n_lat, n_ctx=n_ctx, next_norm=norm(1, 1, mods1))
    out, _ = _conv_ffn_block(xl, u, mods1, ffn_w_up, ffn_conv_w[1], ffn_conv_b[1], w_down, 1,
                             rows=n_lat, n_lat=n_lat, segs=((0, n_lat),), next_norm=None)
    return out
```

```python
import functools
import math

import numpy as np
import jax
import jax.numpy as jnp
from jax import lax
from jax.experimental import pallas as pl
from jax.experimental.pallas import tpu as pltpu

F32 = jnp.float32
BF16 = jnp.bfloat16
HI = lax.Precision.HIGHEST

D_MODEL = 2048
GRID_W = 64
EPS = 1e-6
CONV_W = 3
SSD_WIDTH = D_MODEL
SSD_HEAD_DIM = 64
SSD_HEADS = SSD_WIDTH // SSD_HEAD_DIM
SSD_STATE = 128
SSD_GROUPS = 4
SSD_HPG = SSD_HEADS // SSD_GROUPS
SSD_GW = SSD_WIDTH // SSD_GROUPS
SSD_XBC = SSD_WIDTH + 2 * SSD_GROUPS * SSD_STATE
HY_WIDTH = D_MODEL
HY_EMB = 33
HY_BANDS = (HY_EMB - 1) // 2
HY_ORDER = 64
HY_FAST = 0.3
HY_SLOW = 1.5
HY_TARGET = 1e-2
ML_HEADS = 8
ML_WIDTH = D_MODEL
ML_V_DIM = ML_WIDTH // ML_HEADS
ML_QK_DIM = ML_V_DIM // 2
ML_QK_WIDTH = ML_HEADS * ML_QK_DIM
ROPE_THETA = 10000.0
NA_WIDTH = D_MODEL
NA_HEAD_DIM = 128
NA_HEADS = NA_WIDTH // NA_HEAD_DIM
NA_ROWS = 8
NA_COLS = 16
NA_QROWS = 8
NA_KROWS = 16
NA_PARTS = 2
NA_HPS = 2
NA_PREP_STEPS = 8
NA_TAB_ROWS = 4
NA_PART_KROWS = 12
EV_CONV_CH = SSD_XBC + 3 * HY_WIDTH
SCAN_Q = 128
SCAN_CPS = 2
LANES = 128
SUBLANES = 8
MIB = 1024 * 1024
V7X_VMEM_BYTES = 64 * MIB
VMEM_LIMIT = 48 * MIB
VMEM_LIMIT_BIG = V7X_VMEM_BYTES - 6 * MIB


def _cp(sem, vmem=VMEM_LIMIT):
    return pltpu.CompilerParams(dimension_semantics=sem, vmem_limit_bytes=vmem)


def _sigmoid(x):
    return 0.5 * jnp.tanh(0.5 * x) + 0.5


def _silu(x):
    return x * _sigmoid(x)


def _softplus(x):
    return jnp.maximum(x, 0.0) + jnp.log(1.0 + jnp.exp(-jnp.abs(x)))


def _log_sigmoid(x):
    return -_softplus(-x)


def _dot(a, b, precision=None):
    return jnp.dot(a, b, preferred_element_type=F32, precision=precision)


def _dot_nt(a, b):
    return lax.dot_general(a, b, (((1,), (1,)), ((), ())), preferred_element_type=F32)


def _split_bf16(x, terms):
    out = []
    for _ in range(terms):
        part = x.astype(BF16)
        out.append(part)
        x = x - part.astype(F32)
    return out


def _dot_sel(x, sel, terms):
    selb = sel.astype(BF16)
    return sum(_dot(part, selb) for part in _split_bf16(x, terms))


def _sel_dot(sel, x, terms):
    selb = sel.astype(BF16)
    return sum(_dot(selb, part) for part in _split_bf16(x, terms))


def _mm_body(a_ref, w_ref, o_ref, *scratch, nk, cache_w, w_nk):
    if cache_w:
        wb_ref = scratch[0]

        @pl.when((pl.program_id(1) == 0) & (pl.program_id(2) == 0))
        def _():
            wb_ref[...] = (w_ref[...].T if w_nk else w_ref[...]).astype(BF16)

        w_val = wb_ref[...]
    else:
        w_val = w_ref[...].astype(BF16)
    part = _dot(a_ref[...].astype(BF16), w_val)
    if nk == 1:
        o_ref[...] = part.astype(o_ref.dtype)
        return
    acc_ref = scratch[0]
    k = pl.program_id(3)

    @pl.when(k == 0)
    def _():
        acc_ref[...] = part

    @pl.when(k > 0)
    def _():
        acc_ref[...] += part

    @pl.when(k == nk - 1)
    def _():
        o_ref[...] = acc_ref[...].astype(o_ref.dtype)


def _matmul(a, w, *, tm, tn, tk, out_dtype, rows=None, cols=None, w_index=None, w_nk=False, vmem=VMEM_LIMIT,
            name="mm"):
    ba, ta, kdim = a.shape
    bw, kdim2, n = (w.shape[0], w.shape[2], w.shape[1]) if w_nk else w.shape
    assert kdim == kdim2
    nb = ba if w_index is not None else max(ba, bw)
    rows = ta if rows is None else rows
    n = n if cols is None else cols
    assert rows % tm == 0 and n % tn == 0 and kdim % tk == 0
    ni, nj, nk = rows // tm, n // tn, kdim // tk
    a_map = (lambda j, b, i, k: (b, i, k)) if ba > 1 else (lambda j, b, i, k: (0, i, k))
    if w_index is not None:
        w_map = lambda j, b, i, k: (w_index, k, j)
    else:
        w_map = (lambda j, b, i, k: (b, k, j)) if bw > 1 else (lambda j, b, i, k: (0, k, j))
    cache_w = w.dtype != BF16 and nk == 1 and (w_index is not None or bw == 1)
    assert cache_w or not w_nk
    scratch = [pltpu.VMEM((tm, tn), F32)] if nk > 1 else []
    if cache_w:
        scratch = [pltpu.VMEM((tk, tn), BF16)]
    inner = "arbitrary" if cache_w else "parallel"
    if w_nk:
        w_spec = pl.BlockSpec((None, tn, tk), lambda j, b, i, k: (w_map(j, b, i, k)[0], j, k))
    else:
        w_spec = pl.BlockSpec((None, tk, tn), w_map)
    return pl.pallas_call(
        functools.partial(_mm_body, nk=nk, cache_w=cache_w, w_nk=w_nk),
        out_shape=jax.ShapeDtypeStruct((nb, rows, n), out_dtype), grid=(nj, nb, ni, nk),
        in_specs=[pl.BlockSpec((None, tm, tk), a_map), w_spec],
        out_specs=pl.BlockSpec((None, tm, tn), lambda j, b, i, k: (b, i, j)),
        scratch_shapes=scratch, name=name,
        compiler_params=_cp(("parallel", inner, inner, "arbitrary"), vmem))(a, w)


def _row_is_lat(row_blk, tm, n_lat):
    rows = row_blk * tm + lax.broadcasted_iota(jnp.int32, (tm, 1), 0)
    return rows < n_lat


def _gated_residual_epilogue(acc, row_blk, res_ref, gate_ref, *, tm, n_lat):
    gate = jnp.where(_row_is_lat(row_blk, tm, n_lat), gate_ref[0], gate_ref[1])
    return res_ref[...] + gate * acc


def _gated_residual_rows_body(*refs, ks, tm, n_lat):
    n_a = len(ks)
    a_refs, w_ref, res_ref, gate_ref, o_ref = refs[:n_a], refs[n_a], refs[n_a + 1], refs[n_a + 2], refs[n_a + 3]
    acc, lo = None, 0
    for a_ref, kk in zip(a_refs, ks):
        part = _dot(a_ref[...], w_ref[lo:lo + kk, :])
        acc = part if acc is None else acc + part
        lo += kk
    o_ref[...] = _gated_residual_epilogue(acc, pl.program_id(1), res_ref, gate_ref, tm=tm, n_lat=n_lat)


def _matmul_gated_residual_rows(a_list, w, res, gate, *, w_index, tm, tn, rows, n_lat, name):
    nb = a_list[0].shape[0]
    _, kdim, n = w.shape
    ks = tuple(x.shape[2] for x in a_list)
    assert sum(ks) == kdim and rows % tm == 0 and n % tn == 0
    in_specs = [pl.BlockSpec((None, tm, kk), lambda b, i, j: (b, i, 0)) for kk in ks]
    in_specs += [pl.BlockSpec((None, kdim, tn), lambda b, i, j: (w_index, 0, j)),
                 pl.BlockSpec((None, tm, tn), lambda b, i, j: (b, i, j)),
                 pl.BlockSpec((None, 2, 1, tn), lambda b, i, j: (b, 0, 0, j))]
    return pl.pallas_call(
        functools.partial(_gated_residual_rows_body, ks=ks, tm=tm, n_lat=n_lat),
        out_shape=jax.ShapeDtypeStruct((nb, rows, n), F32), grid=(nb, rows // tm, n // tn), in_specs=in_specs,
        out_specs=pl.BlockSpec((None, tm, tn), lambda b, i, j: (b, i, j)), name=name,
        compiler_params=_cp(("parallel", "parallel", "parallel"), VMEM_LIMIT_BIG))(*a_list, w, res, gate)


def _modulated_rmsnorm(x, is_lat, w_ref, mod_ref):
    y = x * lax.rsqrt(jnp.mean(x * x, axis=-1, keepdims=True) + EPS) * w_ref[...]
    shift = jnp.where(is_lat, mod_ref[0, 0], mod_ref[0, 1])
    scale = jnp.where(is_lat, mod_ref[1, 0], mod_ref[1, 1])
    return y * (1.0 + scale) + shift


def _rmsmod_body(x_ref, w_ref, mod_ref, o_ref, *, tm, n_lat):
    is_lat = _row_is_lat(pl.program_id(2), tm, n_lat)
    o_ref[...] = _modulated_rmsnorm(x_ref[...], is_lat, w_ref, mod_ref).astype(o_ref.dtype)


def _rmsmod(x, w, mod, *, tm, rows, n_lat):
    nb, _, d = x.shape
    tm = tm // 2
    assert rows % tm == 0 and tm % 16 == 0
    body = functools.partial(_rmsmod_body, tm=tm, n_lat=n_lat)
    return pl.pallas_call(
        body, out_shape=jax.ShapeDtypeStruct((nb, rows, d), BF16), grid=(1, nb, rows // tm),
        in_specs=[pl.BlockSpec((None, tm, d), lambda j, b, i: (b, i, 0)),
                  pl.BlockSpec((1, d), lambda j, b, i: (0, 0)),
                  pl.BlockSpec((None, 2, 2, 1, d), lambda j, b, i: (b, 0, 0, 0, 0))],
        out_specs=pl.BlockSpec((None, tm, d), lambda j, b, i: (b, i, 0)),
        name="rmsmod", compiler_params=_cp(("parallel", "parallel", "parallel")))(x, w, mod)


CONV_CH = 128
CONV_HALO = 16
CONV_STEPS = 4


def _shift_matrices():
    r = lax.broadcasted_iota(jnp.int32, (CONV_CH, CONV_CH + 2 * CONV_HALO), 0)
    c = lax.broadcasted_iota(jnp.int32, (CONV_CH, CONV_CH + 2 * CONV_HALO), 1)
    return (c == r + CONV_HALO - 1).astype(BF16), (c == r + CONV_HALO + 1).astype(BF16)


def _chunk_rows(r0, i):
    return pl.ds(r0 + pl.multiple_of(i * CONV_CH, CONV_CH), CONV_CH)


def _taps(w_ref, b_ref, scale=1.0):
    return w_ref[...] * scale, b_ref[...] * scale


def _silu_of_half(h):
    return h + h * jnp.tanh(h)


def _conv_at(x_ref, taps, shifts, r0, n, steps):
    w, b = taps
    cur = [x_ref[_chunk_rows(r0, i), :] for i in steps]
    zero = jnp.zeros((CONV_HALO, cur[0].shape[1]), cur[0].dtype)
    lo = [pl.multiple_of(jnp.maximum(i * CONV_CH - CONV_HALO, 0), CONV_HALO) for i in steps]
    hi = [pl.multiple_of(jnp.minimum((i + 1) * CONV_CH, n - CONV_HALO), CONV_HALO) for i in steps]
    tail = [jnp.where(i > 0, x_ref[pl.ds(r0 + l, CONV_HALO), :], zero) for i, l in zip(steps, lo)]
    head = [jnp.where(i < n // CONV_CH - 1, x_ref[pl.ds(r0 + h, CONV_HALO), :], zero) for i, h in zip(steps, hi)]
    win = [jnp.concatenate([t, c, h], axis=0) for t, c, h in zip(tail, cur, head)]
    prev = [_dot(shifts[0], w) for w in win]
    nxt = [_dot(shifts[1], w) for w in win]
    return [w[0:1, :] * p + w[1:2, :] * c.astype(F32) + w[2:3, :] * x + b for p, c, x in zip(prev, cur, nxt)]


def _conv_loop(segs, fn):
    for si, (r0, n) in enumerate(segs):
        def body(i, carry, r0=r0, n=n, si=si):
            fn(si, r0, n, [i])
            return carry

        lax.fori_loop(0, n // CONV_CH, body, 0, unroll=min(CONV_STEPS, n // CONV_CH))


def _conv_silu_body(x_ref, w_ref, b_ref, o_ref, *, segs):
    shifts = _shift_matrices()
    half_taps = _taps(w_ref, b_ref, 0.5)

    def fn(si, r0, n, steps):
        ys = [_silu_of_half(h) for h in _conv_at(x_ref, half_taps, shifts, r0, n, steps)]
        for i, y in zip(steps, ys):
            o_ref[_chunk_rows(r0, i), :] = y.astype(o_ref.dtype)

    _conv_loop(segs, fn)


def _conv_rope_body(x_ref, w_ref, b_ref, cos_ref, sin_ref, o_ref, *, segs, tc):
    shifts = _shift_matrices()
    lane = lax.broadcasted_iota(jnp.int32, (CONV_CH, tc), 1)
    first = (lane % (ML_QK_DIM // 2)) < (ML_QK_DIM // 4)
    nf = ML_QK_DIM // 4
    half_taps = _taps(w_ref, b_ref, 0.5)

    def fn(si, r0, n, steps):
        ys = [_silu_of_half(h) for h in _conv_at(x_ref, half_taps, shifts, r0, n, steps)]
        partner = [jnp.where(first, pltpu.roll(y, tc - nf, 1), pltpu.roll(y, nf, 1)) for y in ys]
        for i, y, p in zip(steps, ys, partner):
            rows = _chunk_rows(r0, i)
            o_ref[rows, :] = (y * cos_ref[rows, :] + p * sin_ref[rows, :]).astype(o_ref.dtype)

    _conv_loop(segs, fn)


def _conv_ffn_body(a_ref, x_ref, w_ref, b_ref, o_ref, *, segs):
    shifts = _shift_matrices()
    half_taps = _taps(w_ref, b_ref, 0.5)

    def fn(si, r0, n, steps):
        gs = [_silu_of_half(h) for h in _conv_at(x_ref, half_taps, shifts, r0, n, steps)]
        for i, g in zip(steps, gs):
            rows = _chunk_rows(r0, i)
            o_ref[rows, :] = (a_ref[rows, :].astype(F32) * g).astype(o_ref.dtype)

    _conv_loop(segs, fn)


def _conv_hyena_body(x0_ref, x1_ref, v_ref, w0_ref, b0_ref, w1_ref, b1_ref, w2_ref, b2_ref,
                     ox0_ref, ozl_ref, ozc_ref, ozle_ref, ozlo_ref, ozce_ref, ozco_ref, *zs_refs, segs):
    shifts = _shift_matrices()
    taps = (_taps(w0_ref, b0_ref), _taps(w1_ref, b1_ref), _taps(w2_ref, b2_ref))
    half = CONV_CH // 2
    outs = ((ozl_ref, ozle_ref, ozlo_ref), (ozc_ref, ozce_ref, ozco_ref))

    def fn(si, r0, n, steps):
        x0 = _conv_at(x0_ref, taps[0], shifts, r0, n, steps)
        x1 = _conv_at(x1_ref, taps[1], shifts, r0, n, steps)
        v = _conv_at(v_ref, taps[2], shifts, r0, n, steps)
        nat_ref, even_ref, odd_ref = outs[si]
        for i, x0_i, x1_i, v_i in zip(steps, x0, x1, v):
            ox0_ref[_chunk_rows(r0, i), :] = x0_i.astype(ox0_ref.dtype)
            zin = x1_i * v_i
            nat_ref[_chunk_rows(0, i), :] = zin.astype(BF16)
            t0 = r0 + pl.multiple_of(i * CONV_CH, CONV_CH)
            hrows = pl.ds(pl.multiple_of(i * half, half), half)
            for g, zs_ref in enumerate(zs_refs):
                lanes = slice(g * LANES, (g + 1) * LANES)
                zs_ref[pl.ds(t0, CONV_CH), :] = zin[:, lanes]
                even_ref[hrows, lanes] = zs_ref[pl.ds(t0, half, stride=2), :].astype(BF16)
                odd_ref[hrows, lanes] = zs_ref[pl.ds(t0 + 1, half, stride=2), :].astype(BF16)

    _conv_loop(segs, fn)


def _conv_specs(t_in, tc, xcol_blk, wcol_blk):
    return [pl.BlockSpec((None, t_in, tc), lambda b, j: (b, 0, j + xcol_blk)),
            pl.BlockSpec((CONV_W, tc), lambda b, j: (0, j + wcol_blk)),
            pl.BlockSpec((1, tc), lambda b, j: (0, j + wcol_blk))]


def _conv_silu(x, w, b, *, xcol, wcol, width, segs, tc=512):
    nb, t_in, _ = x.shape
    return pl.pallas_call(
        functools.partial(_conv_silu_body, segs=segs),
        out_shape=jax.ShapeDtypeStruct((nb, t_in, width), BF16), grid=(nb, width // tc),
        in_specs=_conv_specs(t_in, tc, xcol // tc, wcol // tc),
        out_specs=pl.BlockSpec((None, t_in, tc), lambda b, j: (b, 0, j)),
        name="conv_silu", compiler_params=_cp(("parallel", "parallel")))(x, w, b)


ROPE_TC = 512


def _conv_rope(x, w, b, cos, sin, *, width, segs, tc=ROPE_TC):
    nb, t_in, _ = x.shape
    assert cos.shape == (t_in, tc) and sin.shape == (t_in, tc)
    tab = pl.BlockSpec((t_in, tc), lambda b, j: (0, 0), pipeline_mode=pl.Buffered(1))
    return pl.pallas_call(
        functools.partial(_conv_rope_body, segs=segs, tc=tc),
        out_shape=jax.ShapeDtypeStruct((nb, t_in, width), BF16), grid=(nb, width // tc),
        in_specs=_conv_specs(t_in, tc, 0, 0) + [tab, tab],
        out_specs=pl.BlockSpec((None, t_in, tc), lambda b, j: (b, 0, j)),
        name="conv_rope", compiler_params=_cp(("parallel", "parallel")))(x, w, b, cos, sin)


def _conv_ffn(up, w, b, *, segs, rows, tc=512):
    nb, t_in, two_ff = up.shape
    ff = two_ff // 2
    return pl.pallas_call(
        functools.partial(_conv_ffn_body, segs=segs),
        out_shape=jax.ShapeDtypeStruct((nb, rows, ff), BF16), grid=(nb, ff // tc),
        in_specs=[pl.BlockSpec((None, rows, tc), lambda b, j: (b, 0, j))] + _conv_specs(rows, tc, ff // tc, 0),
        out_specs=pl.BlockSpec((None, rows, tc), lambda b, j: (b, 0, j)),
        name="conv_ffn", compiler_params=_cp(("parallel", "parallel")))(up, up, w, b)


def _conv_hyena(pr, w, b, *, xcol, wcol, segs, tc=256):
    nb, t_in, _ = pr.shape
    (_, n_lat), (_, n_ctx) = segs
    wd = HY_WIDTH
    out_rows = (t_in, n_lat, n_ctx, n_lat // 2, n_lat // 2, n_ctx // 2, n_ctx // 2)
    in_specs = [pl.BlockSpec((None, t_in, tc), functools.partial(lambda b, j, o: (b, 0, j + o), o=(xcol + m * wd) // tc))
                for m in range(3)]
    wargs = []
    for m in range(3):
        off = (wcol + m * wd) // tc
        in_specs.append(pl.BlockSpec((CONV_W, tc), functools.partial(lambda b, j, o: (0, j + o), o=off)))
        in_specs.append(pl.BlockSpec((1, tc), functools.partial(lambda b, j, o: (0, j + o), o=off)))
        wargs += [w, b]
    return pl.pallas_call(
        functools.partial(_conv_hyena_body, segs=segs),
        out_shape=tuple(jax.ShapeDtypeStruct((nb, rows, wd), BF16) for rows in out_rows),
        grid=(nb, wd // tc), in_specs=in_specs,
        out_specs=tuple(pl.BlockSpec((None, rows, tc), lambda b, j: (b, 0, j)) for rows in out_rows),
        scratch_shapes=[pltpu.VMEM((t_in, LANES), F32)] * (tc // LANES),
        name="conv_hyena", compiler_params=_cp(("parallel", "parallel")))(pr, pr, pr, *wargs)


def _scan_masks(reverse):
    q = SCAN_Q
    ii = lax.broadcasted_iota(jnp.int32, (q, q), 0)
    jj = lax.broadcasted_iota(jnp.int32, (q, q), 1)
    mask = (jj >= ii) if reverse else (jj <= ii)
    mask_t = (jj <= ii) if reverse else (jj >= ii)
    return mask, mask.astype(F32), mask_t.astype(F32)


def _chunk_map(reverse, n_chunks, n_lat_chunks):
    if reverse:
        return lambda k: n_chunks - 1 - k
    return lambda k: (k + n_lat_chunks) % n_chunks


SSD_GPS = 4


def _ssd_body(*refs, reverse, final):
    (xs_ref, b_ref, c_ref, dtc_ref, dtr_ref, dbc_ref, dbr_ref, alc_ref, alr_ref) = refs[:9]
    if final:
        yf_ref, z_ref, dsk_ref, nw_ref, o_ref, h_ref = refs[9:]
    else:
        o_ref, h_ref = refs[9:]

    @pl.when(pl.program_id(2) == 0)
    def _():
        h_ref[...] = jnp.zeros_like(h_ref)

    groups = range(SSD_GPS)
    h = [h_ref[g] for g in groups]
    for c in (reversed(range(SCAN_CPS)) if reverse else range(SCAN_CPS)):
        h = _ssd_chunk(refs, slice(c * SCAN_Q, (c + 1) * SCAN_Q), h, reverse=reverse, final=final)
    for g in groups:
        h_ref[g] = h[g]


def _ssd_chunk(refs, rs, h_prev, *, reverse, final):
    (xs_ref, b_ref, c_ref, dtc_ref, dtr_ref, dbc_ref, dbr_ref, alc_ref, alr_ref) = refs[:9]
    if final:
        yf_ref, z_ref, dsk_ref, nw_ref, o_ref = refs[9:14]
    else:
        o_ref = refs[9]
    q, gw, hd = SCAN_Q, SSD_GW, SSD_HEAD_DIM
    mask, maskf, mask_tf = _scan_masks(reverse)
    groups = range(SSD_GPS)
    pairs = range(SSD_HPG // 2)
    cols = [slice(g * gw, (g + 1) * gw) for g in groups]
    xs = [xs_ref[rs, cols[g]].astype(F32) for g in groups]
    bm = [b_ref[rs, g * SSD_STATE:(g + 1) * SSD_STATE] for g in groups]
    cm = [c_ref[rs, g * SSD_STATE:(g + 1) * SSD_STATE] for g in groups]
    dt_c = [_softplus(dtc_ref[g, rs, :] + dbc_ref[g]) for g in groups]
    dt_r = [_softplus(dtr_ref[g, :, rs] + dbr_ref[g]) for g in groups]
    da_c = [dt_c[g] * (-jnp.exp(alc_ref[g])) for g in groups]
    da_r = [dt_r[g] * (-jnp.exp(alr_ref[g])) for g in groups]
    acum_c_all = _sel_dot(maskf, jnp.concatenate(da_c, axis=1), 3)
    acum_r_all = _dot_sel(jnp.concatenate(da_r, axis=0), mask_tf, 3)
    acum_c = [acum_c_all[:, g * SSD_HPG:(g + 1) * SSD_HPG] for g in groups]
    acum_r = [acum_r_all[g * SSD_HPG:(g + 1) * SSD_HPG, :] for g in groups]
    atot = [jnp.sum(da_c[g], axis=0, keepdims=True) for g in groups]

    er = lax.broadcasted_iota(jnp.int32, (SSD_HPG, gw), 0)
    ec = lax.broadcasted_iota(jnp.int32, (SSD_HPG, gw), 1)
    expand = (ec // hd == er).astype(F32)
    wide = [_dot_sel(jnp.concatenate([dt_c[g], jnp.exp(atot[g] - acum_c[g]), jnp.exp(acum_c[g]),
                                      jnp.broadcast_to(jnp.exp(atot[g]), (SUBLANES, SSD_HPG))], axis=0), expand, 1)
            for g in groups]
    xq = [xs[g] * wide[g][0:q] for g in groups]
    cb = [_dot_nt(cm[g], bm[g]) for g in groups]
    y_off = [wide[g][2 * q:3 * q] * _dot(cm[g], h_prev[g].astype(BF16)) for g in groups]
    lane = lax.broadcasted_iota(jnp.int32, (q, 2 * hd), 1)

    def pair_lhs(g, p):
        ms = [(cb[g] * jnp.where(mask, jnp.exp(acum_c[g][:, r:r + 1] - acum_r[g][r:r + 1, :]), 0.0)).astype(BF16)
              for r in (2 * p, 2 * p + 1)]
        return jnp.concatenate(ms, axis=1)

    def pair_rhs(g, p):
        xp = xq[g][:, 2 * hd * p:2 * hd * (p + 1)]
        return jnp.concatenate([jnp.where(lane < hd, xp, 0.0), jnp.where(lane >= hd, xp, 0.0)], axis=0).astype(BF16)

    y_in = [[_dot(pair_lhs(g, p), pair_rhs(g, p)) for p in pairs] for g in groups]
    y = [y_off[g] + jnp.concatenate(y_in[g], axis=1) for g in groups]
    bt = [bm[g].astype(F32).T.astype(BF16) for g in groups]
    upd = [_dot(bt[g], (xq[g] * wide[g][q:2 * q]).astype(BF16)) for g in groups]
    for g in groups:
        if final:
            ytot = y[g] + yf_ref[rs, cols[g]] + xs[g] * dsk_ref[:, cols[g]]
            gated = ytot * _silu(z_ref[rs, cols[g]].astype(F32))
            o = gated * lax.rsqrt(jnp.mean(gated * gated, axis=-1, keepdims=True) + EPS) * nw_ref[:, cols[g]]
            o_ref[rs, cols[g]] = o.astype(o_ref.dtype)
        else:
            o_ref[rs, cols[g]] = y[g]
    return [wide[g][3 * q:3 * q + 1] * h_prev[g] + upd[g] for g in groups]


def _ssd(xbc, dt_c, dt_r, db_c, db_r, al_c, al_r, *, direction, n_lat, final_args=None):
    nb, t, _ = xbc.shape
    q, gw, hpg = SCAN_CPS * SCAN_Q, SSD_GW, SSD_HPG
    assert n_lat % q == 0 and t % q == 0
    nch = t // q
    reverse = direction == 1
    cmap = _chunk_map(reverse, nch, n_lat // q)
    gps = SSD_GPS
    sw, gww = gps * SSD_STATE, gps * gw
    b0 = SSD_WIDTH // sw
    c0 = b0 + SSD_GROUPS // gps
    d = direction
    in_specs = [
        pl.BlockSpec((None, q, gww), lambda b, g, k: (b, cmap(k), g)),
        pl.BlockSpec((None, q, sw), lambda b, g, k: (b, cmap(k), b0 + g)),
        pl.BlockSpec((None, q, sw), lambda b, g, k: (b, cmap(k), c0 + g)),
        pl.BlockSpec((None, None, gps, q, hpg), lambda b, g, k: (b, d, g, cmap(k), 0)),
        pl.BlockSpec((None, None, gps, hpg, q), lambda b, g, k: (b, d, g, 0, cmap(k))),
        pl.BlockSpec((None, gps, 1, hpg), lambda b, g, k: (d, g, 0, 0)),
        pl.BlockSpec((None, gps, hpg, 1), lambda b, g, k: (d, g, 0, 0)),
        pl.BlockSpec((None, gps, 1, hpg), lambda b, g, k: (d, g, 0, 0)),
        pl.BlockSpec((None, gps, hpg, 1), lambda b, g, k: (d, g, 0, 0)),
    ]
    args = [xbc, xbc, xbc, dt_c, dt_r, db_c, db_r, al_c, al_r]
    final = final_args is not None
    if final:
        yf, pr, dskip, norm_w = final_args
        in_specs += [pl.BlockSpec((None, q, gww), lambda b, g, k: (b, cmap(k), g)),
                     pl.BlockSpec((None, q, gww), lambda b, g, k: (b, cmap(k), g)),
                     pl.BlockSpec((1, gww), lambda b, g, k: (0, g)),
                     pl.BlockSpec((1, gww), lambda b, g, k: (0, g))]
        args += [yf, pr, dskip, norm_w]
    out_shape = jax.ShapeDtypeStruct((nb, t, SSD_WIDTH), BF16 if final else F32)
    return pl.pallas_call(
        functools.partial(_ssd_body, reverse=reverse, final=final),
        out_shape=out_shape, grid=(nb, SSD_GROUPS // gps, nch), in_specs=in_specs,
        out_specs=pl.BlockSpec((None, q, gww), lambda b, g, k: (b, cmap(k), g)),
        scratch_shapes=[pltpu.VMEM((gps, SSD_STATE, gw), F32)],
        name="ssd_bwd" if reverse else "ssd_fwd",
        compiler_params=_cp(("parallel", "parallel", "arbitrary")))(*args)


def _hyfilt_body(f_ref, w1_ref, b1_ref, w2_ref, b2_ref, w3_ref, fr_ref, dl_ref, o_ref, *, tl):
    f = f_ref[...]
    h = jnp.sin(fr_ref[0:1, :] * (_dot(f, w1_ref[...], HI) + b1_ref[...]))
    h = jnp.sin(fr_ref[1:2, :] * (_dot(h, w2_ref[...], HI) + b2_ref[...]))
    h = _dot(h, w3_ref[...], HI)
    dec = jnp.exp(-f[:, 0:1] * dl_ref[...])
    row = pl.program_id(0) * tl + lax.broadcasted_iota(jnp.int32, (tl, 1), 0)
    wd = HY_WIDTH
    o_ref[:, 0:wd] = (h[:, 0:wd] * dec).astype(o_ref.dtype)
    o_ref[:, wd:2 * wd] = jnp.where(row == 0, 0.0, h[:, wd:2 * wd] * dec).astype(o_ref.dtype)


def _hyena_filters(length, w1, b1, w2, b2, w3, freq, *, tl=256):
    p = LANES
    t = jnp.linspace(0.0, 1.0, length, dtype=F32)[:, None]
    w = 2.0 * math.pi * jnp.arange(length, dtype=F32)[:, None] / length
    f = jnp.linspace(1e-4, HY_BANDS - 1, HY_BANDS, dtype=F32)[None, :]
    feats = jnp.concatenate([t, jnp.cos(f * w), -jnp.sin(f * w)], axis=-1)
    feats = jnp.concatenate([feats[0::2], feats[1::2]], axis=0)
    feats = jnp.pad(feats, ((0, 0), (0, p - HY_EMB)))
    po = p - HY_ORDER
    w1p = jnp.pad(w1, ((0, p - HY_EMB), (0, po)))
    w2p = jnp.pad(w2, ((0, po), (0, po)))
    w3p = jnp.pad(w3, ((0, po), (0, 0)))
    b1p = jnp.pad(b1[None, :], ((0, 0), (0, po)))
    b2p = jnp.pad(b2[None, :], ((0, 0), (0, po)))
    frp = jnp.pad(freq, ((0, 0), (0, po)))
    deltas = jnp.abs(jnp.linspace(math.log(HY_TARGET) / HY_FAST, math.log(HY_TARGET) / HY_SLOW, HY_WIDTH,
                                  dtype=F32))[None, :]
    full = lambda shape: pl.BlockSpec(shape, lambda i: (0,) * len(shape))
    out = pl.pallas_call(
        functools.partial(_hyfilt_body, tl=tl),
        out_shape=jax.ShapeDtypeStruct((length, 2 * HY_WIDTH), BF16), grid=(length // tl,),
        in_specs=[pl.BlockSpec((tl, p), lambda i: (i, 0)), full((p, p)), full((1, p)), full((p, p)), full((1, p)),
                  full((p, 2 * HY_WIDTH)), full((2, p)), full((1, HY_WIDTH))],
        out_specs=pl.BlockSpec((tl, 2 * HY_WIDTH), lambda i: (i, 0)),
        name="hyena_filters", compiler_params=_cp(("parallel",)))(feats, w1p, b1p, w2p, b2p, w3p, frp, deltas)
    return out[None]


def _dft_matrices(length):
    n = 2 * length
    k = jnp.arange(length, dtype=jnp.int32)[:, None]
    t = jnp.arange(length, dtype=jnp.int32)[None, :]
    step = 1 << (int(math.log2(length)) // 2)
    part = jnp.arange(length // step, dtype=jnp.int32)[:, None]
    ang_hi = ((part * step * t) % n).astype(F32) * (2.0 * math.pi / n)
    ang_lo = ((jnp.arange(step, dtype=jnp.int32)[:, None] * t) % n).astype(F32) * (2.0 * math.pi / n)
    ch, sh = jnp.cos(ang_hi)[:, None, :], jnp.sin(ang_hi)[:, None, :]
    cl, sl = jnp.cos(ang_lo)[None, :, :], jnp.sin(ang_lo)[None, :, :]
    c = (ch * cl - sh * sl).reshape(length, length)
    s = (sh * cl + ch * sl).reshape(length, length)
    alt_t = jnp.where(t % 2 == 0, 1.0, -1.0).astype(F32)
    fwd = jnp.concatenate([c, jnp.where(k == 0, alt_t, -s)], axis=0)
    alt_r = jnp.where(k % 2 == 0, 1.0, -1.0).astype(F32)
    inv_re = jnp.where(t == 0, 1.0 / n, (2.0 / n) * c)
    inv_im = jnp.where(t == 0, alt_r / n, (-2.0 / n) * s)
    inv = jnp.concatenate([inv_re, inv_im], axis=1)
    return fwd.astype(BF16)[None], inv.astype(BF16)[None]


def _tile_interleave(x, axis, hm):
    shp = x.shape
    n = shp[axis] // 2
    x = x.reshape(shp[:axis] + (2, n // hm, hm) + shp[axis + 1:])
    return jnp.swapaxes(x, axis, axis + 1).reshape(shp)


def _twiddle_table(length, hm):
    k = np.arange(length // 2, dtype=np.float64)[:, None] * (np.pi / length)
    tab = np.concatenate([np.cos(k), np.sin(k)], axis=0)
    tab = tab.reshape(2, -1, hm, 1).swapaxes(0, 1).reshape(length, 1)
    return jnp.asarray(np.broadcast_to(tab, (length, LANES)).astype(np.float32))


def _lanes(x, width):
    return jnp.tile(x, (1, width // LANES))


def _parity_banks(e, ot, tw, hm):
    er, ei, otr, oti = e[0:hm], e[hm:], ot[0:hm], ot[hm:]
    c, s = _lanes(tw[0:hm], e.shape[1]), _lanes(tw[hm:], e.shape[1])
    orr = otr * c + oti * s
    oi = oti * c - otr * s
    return (er + orr, ei + oi, er - orr, ei - oi), (c, s)


def _is_bin0(row_blk, hm):
    return (row_blk == 0) & (lax.broadcasted_iota(jnp.int32, (hm, 1), 0) == 0)


def _hy_kbank_body(f_ref, hef_ref, heb_ref, hof_ref, hob_ref, tw_ref, ka_ref, kb_ref, *, hm):
    fwd, tw = f_ref[...], tw_ref[...]
    ef, eb, of, ob = (_dot(fwd, r[...]) for r in (hef_ref, heb_ref, hof_ref, hob_ref))
    (afr, afi, bfr, bfi), _ = _parity_banks(ef, of, tw, hm)
    (abr, abi, bbr, bbi), _ = _parity_banks(eb, ob, tw, hm)
    first = _is_bin0(pl.program_id(0), hm)
    kai = jnp.where(first, ef[hm:] + eb[hm:], afi - abi)
    kbi = jnp.where(first, ob[hm:] - of[hm:], bfi - bbi)
    ka_ref[...] = jnp.concatenate([afr + abr, kai], axis=0)
    kb_ref[...] = jnp.concatenate([bfr + bbr, kbi], axis=0)


def _hy_kbank(fwd, hfb, tw, *, tm, tc=512):
    _, ln, c2 = hfb.shape
    c, h = c2 // 2, ln // 2
    taps = lambda row_blk, col_off: pl.BlockSpec((None, h, tc), lambda i, j: (0, row_blk, j + col_off))
    bank = pl.BlockSpec((None, tm, tc), lambda i, j: (0, i, j))
    return pl.pallas_call(
        functools.partial(_hy_kbank_body, hm=tm // 2),
        out_shape=(jax.ShapeDtypeStruct((1, ln, c), F32),) * 2, grid=(ln // tm, c // tc),
        in_specs=[pl.BlockSpec((None, tm, h), lambda i, j: (0, i, 0)),
                  taps(0, 0), taps(0, c // tc), taps(1, 0), taps(1, c // tc),
                  pl.BlockSpec((tm, LANES), lambda i, j: (i, 0))],
        out_specs=(bank, bank), name="hy_kbank",
        compiler_params=_cp(("parallel", "parallel")))(fwd, hfb, hfb, hfb, hfb, tw)


def _hy_spectrum_body(f_ref, ze_ref, zo_ref, tw_ref, ka_ref, kb_ref, p_ref, q_ref, *, hm):
    fwd = f_ref[...]
    e, acc = _dot(fwd, ze_ref[...]), _dot(fwd, zo_ref[...])
    row_blk = pl.program_id(2)
    (ar, ai, br, bi), (c, s) = _parity_banks(e, acc, tw_ref[...], hm)
    kar, kai, kbr, kbi = ka_ref[0:hm], ka_ref[hm:], kb_ref[0:hm], kb_ref[hm:]
    yar, yai = ar * kar - ai * kai, ar * kai + ai * kar
    ybr, ybi = br * kbr - bi * kbi, br * kbi + bi * kbr
    qr, qi = yar - ybr, yai - ybi
    pr, pi = yar + ybr, yai + ybi
    qpr, qpi = qr * c - qi * s, qi * c + qr * s
    top = slice(0, SUBLANES)
    first = _is_bin0(row_blk, SUBLANES)
    y0, yl = ar[top] * kar[top], br[top] * kbr[top]
    amr, ami = e[hm:hm + SUBLANES], -acc[hm:hm + SUBLANES]
    ymr, ymi = amr * kai[top] - ami * kbi[top], amr * kbi[top] + ami * kai[top]

    def patch(full, special):
        return jnp.concatenate([jnp.where(first, special, full[top]), full[SUBLANES:]], axis=0)

    pr, pi = patch(pr, y0 + yl), patch(pi, 2.0 * ymr)
    qpr, qpi = patch(qpr, y0 - yl), patch(qpi, -2.0 * ymi)
    p_ref[...] = jnp.concatenate([pr, pi], axis=0).astype(p_ref.dtype)
    q_ref[...] = jnp.concatenate([qpr, qpi], axis=0).astype(q_ref.dtype)


def _hy_spectrum(fwd, ze, zo, tw, ka, kb, *, tm, tn):
    nb, h, c = ze.shape
    ln = fwd.shape[1]
    data = pl.BlockSpec((None, h, tn), lambda j, b, i: (b, 0, j))
    bank = pl.BlockSpec((None, tm, tn), lambda j, b, i: (0, i, j))
    out = pl.BlockSpec((None, tm, tn), lambda j, b, i: (b, i, j))
    return pl.pallas_call(
        functools.partial(_hy_spectrum_body, hm=tm // 2),
        out_shape=(jax.ShapeDtypeStruct((nb, ln, c), BF16),) * 2, grid=(c // tn, nb, ln // tm),
        in_specs=[pl.BlockSpec((None, tm, h), lambda j, b, i: (0, i, 0)), data, data,
                  pl.BlockSpec((tm, LANES), lambda j, b, i: (i, 0)), bank, bank],
        out_specs=(out, out), name="hy_spectrum",
        compiler_params=_cp(("parallel", "parallel", "parallel")))(fwd, ze, zo, tw, ka, kb)


def _hyena_out_body(*refs, segs):
    ns = len(segs)
    x0_ref, db_ref, o_ref = refs[3 * ns:3 * ns + 3]
    ys_refs = refs[3 * ns + 3:]
    half = CONV_CH // 2
    for si, (r0, n) in enumerate(segs):
        ye_ref, yo_ref, zin_ref = refs[3 * si:3 * si + 3]

        def body(i, carry, r0=r0, ye_ref=ye_ref, yo_ref=yo_ref, zin_ref=zin_ref):
            t0 = pl.multiple_of(i * CONV_CH, CONV_CH)
            hrows = pl.ds(pl.multiple_of(i * half, half), half)
            for g, ys_ref in enumerate(ys_refs):
                lanes = slice(g * LANES, (g + 1) * LANES)
                ys_ref[pl.ds(r0 + t0, half, stride=2), :] = ye_ref[hrows, lanes].astype(F32)
                ys_ref[pl.ds(r0 + t0 + 1, half, stride=2), :] = yo_ref[hrows, lanes].astype(F32)
            rows = pl.ds(r0 + t0, CONV_CH)
            y = jnp.concatenate([ys_ref[rows, :] for ys_ref in ys_refs], axis=1)
            zin = zin_ref[pl.ds(t0, CONV_CH), :].astype(F32)
            o_ref[rows, :] = (x0_ref[rows, :].astype(F32) * (y + zin * db_ref[...])).astype(o_ref.dtype)
            return carry

        lax.fori_loop(0, n // CONV_CH, body, 0, unroll=min(4, n // CONV_CH))


def _hyena_out(parts, x0, dbias, *, segs, tc=256):
    nb, t, c = x0.shape
    in_specs, args = [], []
    for (ye, yo, zin), (_, n) in zip(parts, segs):
        in_specs += [pl.BlockSpec((None, n // 2, tc), lambda b, j: (b, 0, j)),
                     pl.BlockSpec((None, n // 2, tc), lambda b, j: (b, 0, j)),
                     pl.BlockSpec((None, n, tc), lambda b, j: (b, 0, j))]
        args += [ye, yo, zin]
    in_specs += [pl.BlockSpec((None, t, tc), lambda b, j: (b, 0, j)), pl.BlockSpec((1, tc), lambda b, j: (0, j))]
    return pl.pallas_call(
        functools.partial(_hyena_out_body, segs=segs),
        out_shape=jax.ShapeDtypeStruct((nb, t, c), BF16), grid=(nb, c // tc), in_specs=in_specs,
        out_specs=pl.BlockSpec((None, t, tc), lambda b, j: (b, 0, j)),
        scratch_shapes=[pltpu.VMEM((t, LANES), F32)] * (tc // LANES), name="hyena_out",
        compiler_params=_cp(("parallel", "parallel")))(*args, x0, dbias)


def _hyena_longconv(ze, zo, hfb, *, tn=1024):
    nb, h, c = ze.shape
    ln = 2 * h
    tm_s = min(512, ln)
    hm = tm_s // 2
    fwd, inv = _dft_matrices(h)
    fwd = _tile_interleave(fwd, 1, hm)
    inv = _tile_interleave(inv * 0.5, 2, hm)
    tw = _twiddle_table(ln, hm)
    ka, kb = _hy_kbank(fwd, hfb, tw, tm=tm_s)
    p, q = _hy_spectrum(fwd, ze, zo, tw, ka, kb, tm=tm_s, tn=tn)
    tm_i, tk_i = min(1024, h), ln
    ye = _matmul(inv, p, tm=tm_i, tn=tn, tk=tk_i, out_dtype=BF16, name="hy_idft_even")
    yo = _matmul(inv, q, tm=tm_i, tn=tn, tk=tk_i, out_dtype=BF16, name="hy_idft_odd")
    return ye, yo


def _mlstm_body(*refs, reverse, final):
    q_ref, k_ref, v_ref, gc_ref, gr_ref = refs[:5]
    if final:
        hf_ref, og_ref, nw_ref, o_ref, ct_ref, n_ref, m_ref = refs[5:]
    else:
        o_ref, ct_ref, n_ref, m_ref = refs[5:]
    nh, dk, dv = ML_HEADS, ML_QK_DIM, ML_V_DIM
    scale = dk ** -0.5

    @pl.when(pl.program_id(1) == 0)
    def _():
        ct_ref[...] = jnp.zeros_like(ct_ref)
        n_ref[...] = jnp.zeros_like(n_ref)
        m_ref[...] = jnp.zeros_like(m_ref)

    state = ([ct_ref[h] for h in range(nh)], n_ref[...], m_ref[...])
    for c in (reversed(range(SCAN_CPS)) if reverse else range(SCAN_CPS)):
        state = _mlstm_chunk(refs, slice(c * SCAN_Q, (c + 1) * SCAN_Q), state, reverse=reverse, final=final)
    for h in range(nh):
        ct_ref[h] = state[0][h]
    n_ref[...] = state[1]
    m_ref[...] = state[2]


def _mlstm_chunk(refs, rs, state, *, reverse, final):
    q_ref, k_ref, v_ref, gc_ref, gr_ref = refs[:5]
    if final:
        hf_ref, og_ref, nw_ref, o_ref = refs[5:9]
    else:
        o_ref = refs[5]
    nh, dk, dv = ML_HEADS, ML_QK_DIM, ML_V_DIM
    scale = dk ** -0.5
    cts, n_all, m_all = state
    gc, gr = gc_ref[rs, :], gr_ref[:, rs]
    li_c, lf_c = gc[:, 0:nh], _log_sigmoid(gc[:, nh:2 * nh])
    li_r, lf_r = gr[0:nh, :], _log_sigmoid(gr[nh:2 * nh, :])
    mask, maskf, mask_tf = _scan_masks(reverse)
    bc_c = _dot(maskf, lf_c, HI)
    bc_r = _dot(lf_r, mask_tf, HI)
    btot_c = jnp.sum(lf_c, axis=0, keepdims=True)
    heads = range(nh)
    qs = [q_ref[rs, dk * h:dk * (h + 1)] for h in heads]
    ks = [k_ref[rs, dk * h:dk * (h + 1)] for h in heads]
    vs = [v_ref[rs, dv * h:dv * (h + 1)] for h in heads]
    m_s = [m_all[h:h + 1, 0:1] for h in heads]
    n_s = [n_all[h:h + 1, :] for h in heads]
    qk = [_dot_nt(qs[h], ks[h]) for h in heads]
    qc = [_dot(qs[h], cts[h].astype(BF16)) for h in heads]
    dmat = [jnp.where(mask, bc_c[:, h:h + 1] - bc_r[h:h + 1, :] + li_r[h:h + 1, :], -jnp.inf) for h in heads]
    inter = [bc_c[:, h:h + 1] + m_s[h] for h in heads]
    m_t = [jnp.maximum(inter[h], jnp.max(dmat[h], axis=1, keepdims=True)) for h in heads]
    s = [qk[h] * scale * jnp.exp(dmat[h] - m_t[h]) for h in heads]
    dec = [jnp.exp(inter[h] - m_t[h]) for h in heads]
    num = [_dot(s[h].astype(BF16), vs[h]) + (dec[h] * scale) * qc[h] for h in heads]
    qn = [jnp.sum(qs[h].astype(F32) * n_s[h], axis=1, keepdims=True) * scale for h in heads]
    den = [jnp.sum(s[h], axis=1, keepdims=True) + dec[h] * qn[h] for h in heads]
    hh = [num[h] / jnp.maximum(jnp.abs(den[h]), jnp.exp(-m_t[h])) for h in heads]

    wst_c = [btot_c[:, h:h + 1] - bc_c[:, h:h + 1] + li_c[:, h:h + 1] for h in heads]
    wst_r = [btot_c[:, h:h + 1] - bc_r[h:h + 1, :] + li_r[h:h + 1, :] for h in heads]
    m_new = [jnp.maximum(btot_c[:, h:h + 1] + m_s[h], jnp.max(wst_r[h], axis=1, keepdims=True)) for h in heads]
    dstate = [jnp.exp(btot_c[:, h:h + 1] + m_s[h] - m_new[h]) for h in heads]
    kw = [ks[h].astype(F32) * jnp.exp(wst_c[h] - m_new[h]) for h in heads]
    upd = [_dot(kw[h].T.astype(BF16), vs[h]) for h in heads]
    n_rows = [dstate[h] * n_s[h] + jnp.sum(kw[h], axis=0, keepdims=True) for h in heads]
    m_rows = [jnp.broadcast_to(m_new[h], (1, LANES)) for h in heads]

    for h in heads:
        cols = slice(dv * h, dv * (h + 1))
        if final:
            hs = hh[h] + hf_ref[rs, cols]
            hn = hs * lax.rsqrt(jnp.mean(hs * hs, axis=-1, keepdims=True) + EPS) * nw_ref[:, cols]
            o_ref[rs, cols] = (hn * _sigmoid(og_ref[rs, cols].astype(F32))).astype(o_ref.dtype)
        else:
            o_ref[rs, cols] = hh[h]

    return ([dstate[h] * cts[h] + upd[h] for h in heads], jnp.concatenate(n_rows, axis=0),
            jnp.concatenate(m_rows, axis=0))


def _mlstm(qk, pr, g_c, g_r, *, direction, n_lat, final_args=None):
    nb, t, _ = qk.shape
    q = SCAN_CPS * SCAN_Q
    assert n_lat % q == 0 and t % q == 0
    nch = t // q
    reverse = direction == 1
    cmap = _chunk_map(reverse, nch, n_lat // q)
    d = direction
    in_specs = [
        pl.BlockSpec((None, q, ML_QK_WIDTH), lambda b, k: (b, cmap(k), 0)),
        pl.BlockSpec((None, q, ML_QK_WIDTH), lambda b, k: (b, cmap(k), 1)),
        pl.BlockSpec((None, q, ML_WIDTH), lambda b, k: (b, cmap(k), 2 * ML_QK_WIDTH // ML_WIDTH)),
        pl.BlockSpec((None, None, q, 2 * ML_HEADS), lambda b, k: (b, d, cmap(k), 0)),
        pl.BlockSpec((None, None, 2 * ML_HEADS, q), lambda b, k: (b, d, 0, cmap(k))),
    ]
    args = [qk, qk, pr, g_c, g_r]
    final = final_args is not None
    if final:
        hf, norm_w = final_args
        in_specs += [pl.BlockSpec((None, q, ML_WIDTH), lambda b, k: (b, cmap(k), 0)),
                     pl.BlockSpec((None, q, ML_WIDTH), lambda b, k: (b, cmap(k), 2 * ML_QK_WIDTH // ML_WIDTH + 1)),
                     pl.BlockSpec((1, ML_WIDTH), lambda b, k: (0, 0))]
        args += [hf, pr, norm_w]
    out_shape = jax.ShapeDtypeStruct((nb, t, ML_WIDTH), BF16 if final else F32)
    return pl.pallas_call(
        functools.partial(_mlstm_body, reverse=reverse, final=final),
        out_shape=out_shape, grid=(nb, nch), in_specs=in_specs,
        out_specs=pl.BlockSpec((None, q, ML_WIDTH), lambda b, k: (b, cmap(k), 0)),
        scratch_shapes=[pltpu.VMEM((ML_HEADS, ML_QK_DIM, ML_V_DIM), F32), pltpu.VMEM((ML_HEADS, ML_QK_DIM), F32),
                        pltpu.VMEM((ML_HEADS, LANES), F32)],
        name="mlstm_bwd" if reverse else "mlstm_fwd",
        compiler_params=_cp(("parallel", "arbitrary")))(*args)


def _na_window_start(j, rows):
    return jnp.clip(j * NA_QROWS - NA_ROWS // 2, 0, rows - NA_KROWS)


def _na_part_offset(j, part, start, clip=jnp.clip):
    first = j * NA_QROWS + part * (NA_QROWS // NA_PARTS) - NA_ROWS // 2
    return clip(first - start, 0, NA_KROWS - NA_PART_KROWS) // NA_TAB_ROWS * NA_TAB_ROWS


def _na_body(q_ref, k_ref, v_ref, tab_ref, qw_ref, kw_ref, o_ref, kn_ref, v1_ref, *, n_lat, n_ctx):
    dh = NA_HEAD_DIM
    scale = dh ** -0.5
    rows = n_lat // GRID_W
    j = pl.program_id(2)
    heads = range(NA_HPS)
    hcols = [slice(hh * dh, (hh + 1) * dh) for hh in heads]

    @pl.when(j == 0)
    def _():
        prep = (n_lat + n_ctx) // NA_PREP_STEPS
        one_hot = (lax.broadcasted_iota(jnp.int32, (prep, LANES), 1) == 0).astype(v1_ref.dtype)

        def body(i, carry):
            r = pl.ds(pl.multiple_of(i * prep, 16), prep)
            for hh in heads:
                kk = k_ref[r, hcols[hh]].astype(F32)
                kn = kk * lax.rsqrt(jnp.mean(kk * kk, axis=-1, keepdims=True) + EPS) * kw_ref[...]
                kn_ref[r, hcols[hh]] = kn.astype(kn_ref.dtype)
                v1_ref[hh, r, 0:dh] = v_ref[r, hcols[hh]]
                v1_ref[hh, r, dh:dh + LANES] = one_hot
            return carry

        lax.fori_loop(0, NA_PREP_STEPS, body, 0)

    start = _na_window_start(j, rows)
    ctx = pl.ds(n_lat, n_ctx)
    rpp = NA_QROWS // NA_PARTS
    pq = rpp * GRID_W
    chains = [(hh, i) for hh in heads for i in range(NA_PARTS)]
    off = [_na_part_offset(j, i, start) for i in range(NA_PARTS)]
    win = [pl.ds(pl.multiple_of((start + off[i]) * GRID_W, GRID_W), NA_PART_KROWS * GRID_W) for i in range(NA_PARTS)]
    qq = [q_ref[i * pq:(i + 1) * pq, hcols[hh]].astype(F32) for hh, i in chains]
    qn = [(x * lax.rsqrt(jnp.mean(x * x, axis=-1, keepdims=True) + EPS) * (qw_ref[...] * scale)).astype(BF16)
          for x in qq]
    sw = [_dot_nt(qn[c], kn_ref[win[i], hcols[hh]]) for c, (hh, i) in enumerate(chains)]
    sc = [_dot_nt(qn[c], kn_ref[ctx, hcols[hh]]) for c, (hh, i) in enumerate(chains)]

    def with_bias(c, hh, i):
        out = []
        for r in range(rpp):
            d0 = start - (j * NA_QROWS + i * rpp + r) + NA_ROWS - 1 + NA_QROWS
            blocks = tab_ref[hh, d0, pl.ds(off[i] // NA_TAB_ROWS, NA_PART_KROWS // NA_TAB_ROWS)]
            bias = jnp.concatenate([blocks[b] for b in range(NA_PART_KROWS // NA_TAB_ROWS)], axis=1)
            out.append(sw[c][r * GRID_W:(r + 1) * GRID_W] + bias)
        return jnp.concatenate(out, axis=0)

    sb = [with_bias(c, hh, i) for c, (hh, i) in enumerate(chains)]
    m = [jnp.maximum(jnp.max(a, axis=1, keepdims=True), jnp.max(b, axis=1, keepdims=True)) for a, b in zip(sb, sc)]
    pw = [jnp.exp(a - mm) for a, mm in zip(sb, m)]
    pc = [jnp.exp(a - mm) for a, mm in zip(sc, m)]
    o = [_dot(pw[c].astype(BF16), v1_ref[hh, win[i], :]) + _dot(pc[c].astype(BF16), v1_ref[hh, ctx, :])
         for c, (hh, i) in enumerate(chains)]
    for c, (hh, i) in enumerate(chains):
        o_ref[i * pq:(i + 1) * pq, hcols[hh]] = (o[c][:, 0:dh] / o[c][:, dh:dh + 1]).astype(o_ref.dtype)


def _na_bias_table(rpb, rows):
    nd = 2 * NA_QROWS
    valid = np.zeros((nd, NA_KROWS), bool)
    seen = np.zeros((nd,), bool)
    for j in range(rows // NA_QROWS):
        start = int(np.clip(j * NA_QROWS - NA_ROWS // 2, 0, rows - NA_KROWS))
        for qr in range(NA_QROWS):
            r = j * NA_QROWS + qr
            rs = int(np.clip(r - NA_ROWS // 2, 0, rows - NA_ROWS))
            d = start - r + NA_ROWS - 1 + NA_QROWS
            v = np.array([(rs <= start + i < rs + NA_ROWS) for i in range(NA_KROWS)])
            assert 0 <= d < nd and (not seen[d] or (valid[d] == v).all())
            valid[d], seen[d] = v, True
            off = int(_na_part_offset(j, qr // (NA_QROWS // NA_PARTS), start, clip=np.clip))
            assert off % NA_TAB_ROWS == 0 and not v[:off].any() and not v[off + NA_PART_KROWS:].any()
    col = np.arange(GRID_W)
    cs = np.clip(col - NA_COLS // 2, 0, GRID_W - NA_COLS)
    in_win = (col[None, :] >= cs[:, None]) & (col[None, :] < cs[:, None] + NA_COLS)
    ndr, ndc, w = 2 * NA_ROWS - 1, 2 * NA_COLS - 1, GRID_W
    dr = (np.arange(nd)[:, None] - NA_QROWS) + np.arange(NA_KROWS)[None, :]
    ok = valid[:, None, :, None] & in_win[None, :, None, :] & (dr >= 0)[:, None, :, None] \
        & (dr < ndr)[:, None, :, None]
    ext = jnp.pad(rpb, ((0, 0), (0, 0), (w, w)), mode="edge")
    c0 = w + NA_COLS - 1
    t1 = jnp.stack([ext[:, :, c0 - qc:c0 - qc + w] for qc in range(w)], axis=2)
    t1p = jnp.pad(t1, ((0, 0), (NA_QROWS, nd + NA_KROWS - 1 - NA_QROWS - ndr), (0, 0), (0, 0)))
    wide = jnp.concatenate([t1p[:, m] for m in range(nd + NA_KROWS - 1)], axis=-1)
    wpad = LANES * pl.cdiv(wide.shape[-1], LANES)
    wide = jnp.pad(wide, ((0, 0), (0, 0), (0, wpad - wide.shape[-1])))
    nblk, bw = NA_KROWS // NA_TAB_ROWS, NA_TAB_ROWS * w
    okf = jnp.asarray(ok.reshape(nd, w, nblk, bw).transpose(0, 2, 1, 3).astype(np.float32))

    def body(wide_ref, ok_ref, o_ref):
        for d in range(nd):
            for b in range(nblk):
                lo = d * w + b * bw
                o_ref[d, b] = jnp.where(ok_ref[d, b] != 0.0, wide_ref[:, lo:lo + bw], -jnp.inf)

    return pl.pallas_call(
        body, out_shape=jax.ShapeDtypeStruct((rpb.shape[0], nd, nblk, w, bw), F32), grid=(rpb.shape[0],),
        in_specs=[pl.BlockSpec((None, w, wpad), lambda h: (h, 0, 0)),
                  pl.BlockSpec((nd, nblk, w, bw), lambda h: (0, 0, 0, 0))],
        out_specs=pl.BlockSpec((None, nd, nblk, w, bw), lambda h: (h, 0, 0, 0, 0)),
        name="na_bias_table", compiler_params=_cp(("parallel",)))(wide, okf)


def _na(pr, tab, q_w, k_w, *, n_lat, n_ctx, qcol):
    nb, t, _ = pr.shape
    dh, nh, hps = NA_HEAD_DIM, NA_HEADS, NA_HPS
    nq = NA_QROWS * GRID_W
    hw = hps * dh
    assert qcol % hw == 0 and nh % hps == 0
    qb = qcol // hw
    return pl.pallas_call(
        functools.partial(_na_body, n_lat=n_lat, n_ctx=n_ctx),
        out_shape=jax.ShapeDtypeStruct((nb, n_lat, NA_WIDTH), BF16), grid=(nb, nh // hps, n_lat // nq),
        in_specs=[pl.BlockSpec((None, nq, hw), lambda b, h, j: (b, j, qb + h)),
                  pl.BlockSpec((None, t, hw), lambda b, h, j: (b, 0, qb + nh // hps + h)),
                  pl.BlockSpec((None, t, hw), lambda b, h, j: (b, 0, qb + 2 * (nh // hps) + h)),
                  pl.BlockSpec((hps,) + tab.shape[1:], lambda b, h, j: (h, 0, 0, 0, 0)),
                  pl.BlockSpec((1, dh), lambda b, h, j: (0, 0)),
                  pl.BlockSpec((1, dh), lambda b, h, j: (0, 0))],
        out_specs=pl.BlockSpec((None, nq, hw), lambda b, h, j: (b, j, h)),
        scratch_shapes=[pltpu.VMEM((t, hw), BF16), pltpu.VMEM((hps, t, dh + LANES), BF16)], name="na",
        compiler_params=_cp(("parallel", "parallel", "arbitrary")))(pr, pr, pr, tab, q_w, k_w)


def _rope_tables(n_lat, n_ctx, width):
    dh = ML_QK_DIM
    nf = dh // 4
    t = jnp.arange(n_lat)
    inv = ROPE_THETA ** (-jnp.arange(nf, dtype=F32) / nf)
    lane = np.arange(dh)
    pos = jnp.where((lane < dh // 2)[None, :], (t // GRID_W)[:, None], (t % GRID_W)[:, None]).astype(F32)
    ang = pos * inv[lane % nf][None, :]
    sign = np.where((lane % (dh // 2)) < nf, -1.0, 1.0).astype(np.float32)
    cos = jnp.concatenate([jnp.cos(ang), jnp.ones((n_ctx, dh), F32)], axis=0)
    sin = jnp.concatenate([jnp.sin(ang) * sign[None, :], jnp.zeros((n_ctx, dh), F32)], axis=0)
    reps = width // dh
    return jnp.tile(cos, (1, reps)), jnp.tile(sin, (1, reps))


def _ada_mod(c, c_ctx, w_all, b_all, layer):
    nb, d = c.shape
    rows = 16
    cv = jnp.concatenate([c, c_ctx[None, :], jnp.zeros((rows - nb - 1, d), F32)], axis=0)
    a = (cv * _sigmoid(cv)).astype(BF16)[None]
    out = _matmul(a, w_all, w_index=layer, tm=rows, tn=1024, tk=d, out_dtype=F32, name="ada_mod")[0]
    out = out[:nb + 1] + b_all[layer][None, :]
    return out.reshape(nb + 1, 6, d).transpose(1, 0, 2)


def _lat_ctx(m, nb):
    ctx = jnp.broadcast_to(m[nb][None, :], (nb, m.shape[1]))
    return jnp.stack([m[:nb], ctx], axis=1)[:, :, None, :]


def _pad_cols(w, n):
    return jnp.pad(w, ((0, 0), (0, n - w.shape[1])))


def _col_row_forms(g, groups):
    nb, t, _ = g.shape
    g5 = g.reshape(nb, t, 2, groups, -1)
    return g5.transpose(0, 2, 3, 1, 4), g5.transpose(0, 2, 3, 4, 1)


ROW_TILES = (1088, 1024, 512, 256)


def _row_tile(rows):
    return next(tm for tm in ROW_TILES if rows % tm == 0)


def _mod_pair(mods, shift, scale, nb):
    return jnp.stack([_lat_ctx(mods[shift], nb), _lat_ctx(mods[scale], nb)], axis=1)


def _in_projection(u, w_all, layer, *, rows, name, cols=None, w_nk=False):
    tm = _row_tile(rows)
    tm = 2 * tm if rows % (2 * tm) == 0 else tm
    return _matmul(u, w_all, w_index=layer, w_nk=w_nk, cols=cols, tm=tm, tn=1024, tk=u.shape[2], out_dtype=BF16,
                   rows=rows, vmem=VMEM_LIMIT_BIG, name=name)


def _out_projection(a_list, w_all, x, gate, layer, *, rows, n_lat, next_norm, name):
    tm = _row_tile(rows)
    tn = 512
    x = _matmul_gated_residual_rows(a_list, w_all, x, gate, w_index=layer, tm=tm, tn=tn, rows=rows, n_lat=n_lat,
                                    name=name)
    if next_norm is None:
        return x, None
    return x, _rmsmod(x, *next_norm, tm=tm, rows=rows, n_lat=n_lat)


def _conv_ffn_block(x, u, mods, w_up_all, conv_w, conv_b, w_down_all, layer, *, rows, n_lat, segs, next_norm):
    nb = x.shape[0]
    up = _in_projection(u, w_up_all, layer, rows=rows, name="ffn_up")
    hid = _conv_ffn(up, conv_w, conv_b[None, :], segs=segs, rows=rows)
    return _out_projection((hid,), w_down_all, x, _lat_ctx(mods[5], nb), layer, rows=rows, n_lat=n_lat,
                           next_norm=next_norm, name="ffn_down")


def _even_layer(x, u, mods, w_in_all, conv_w, conv_b, dt_bias, a_log, d_skip, ssd_norm_w,
                hy_w1, hy_b1, hy_w2, hy_b2, hy_w3, hy_freq, hy_bias, w_out_all, j, *, n_lat, n_ctx, next_norm):
    nb, t, d = x.shape
    segs = ((0, n_lat), (n_lat, n_ctx))
    tm = _row_tile(t)
    n_main = SSD_WIDTH + EV_CONV_CH
    pr = _in_projection(u, jnp.swapaxes(w_in_all, 1, 2), j, rows=t, cols=n_main, w_nk=True, name="ev_in")
    w_dt = _pad_cols(w_in_all[j, :, n_main:], LANES).astype(BF16)
    dtp = _matmul(u, w_dt[None], tm=tm, tn=LANES, tk=d, out_dtype=F32, name="ev_in_dt")[:, :, :2 * SSD_HEADS]
    cb = conv_b[None, :]
    xbc = _conv_silu(pr, conv_w, cb, xcol=SSD_WIDTH, wcol=0, width=SSD_XBC, segs=segs)
    x0, zin_l, zin_c, zle, zlo, zce, zco = _conv_hyena(pr, conv_w, cb, xcol=SSD_WIDTH + SSD_XBC, wcol=SSD_XBC,
                                                       segs=segs)

    dt_c, dt_r = _col_row_forms(dtp, SSD_GROUPS)
    db = dt_bias.reshape(2, SSD_GROUPS, 1, SSD_HPG)
    al = a_log.reshape(2, SSD_GROUPS, 1, SSD_HPG)
    scan_args = (xbc, dt_c, dt_r, db, db.transpose(0, 1, 3, 2), al, al.transpose(0, 1, 3, 2))
    yf = _ssd(*scan_args, direction=0, n_lat=n_lat)
    dsk = jnp.repeat(d_skip, SSD_HEAD_DIM)[None, :]
    y_ssd = _ssd(*scan_args, direction=1, n_lat=n_lat, final_args=(yf, pr, dsk, ssd_norm_w[None, :]))

    hyb = hy_bias[None, :]
    filt = functools.partial(_hyena_filters, w1=hy_w1, b1=hy_b1, w2=hy_w2, b2=hy_b2, w3=hy_w3, freq=hy_freq)
    lat = _hyena_longconv(zle, zlo, filt(n_lat)) + (zin_l,)
    ctx = _hyena_longconv(zce, zco, filt(n_ctx)) + (zin_c,)
    y_hy = _hyena_out((lat, ctx), x0, hyb, segs=segs)
    return _out_projection((y_ssd, y_hy), w_out_all, x, _lat_ctx(mods[2], nb), j, rows=t, n_lat=n_lat,
                           next_norm=next_norm, name="ev_out")


def _odd_layer_latent(x, u, mods, w_in_all, conv_w, conv_b, gate_b, ml_norm_w, q_norm_w, k_norm_w, rpb,
                      w_out_all, j, *, n_lat, n_ctx, next_norm):
    nb, t, d = x.shape
    segs = ((0, n_lat), (n_lat, n_ctx))
    tm = _row_tile(t)
    o3 = 2 * ML_QK_WIDTH + 2 * ML_WIDTH
    o4 = o3 + 4 * ML_HEADS
    w_nk = jnp.swapaxes(w_in_all, 1, 2)
    pr = _in_projection(u, w_nk, j, rows=t, cols=o3, w_nk=True, name="od_in_mlstm")
    pr_na = _in_projection(u, w_nk[j:j + 1, o4:, :], 0, rows=t, w_nk=True, name="od_in_na")
    gates = _matmul(u, _pad_cols(w_in_all[j, :, o3:o4], LANES).astype(BF16)[None], tm=tm, tn=LANES, tk=d,
                    out_dtype=F32, name="od_in_gates")[:, :, :4 * ML_HEADS] + gate_b.reshape(-1)
    cos, sin = _rope_tables(n_lat, n_ctx, ROPE_TC)
    qk = _conv_rope(pr, conv_w, conv_b[None, :], cos, sin, width=2 * ML_QK_WIDTH, segs=segs)
    g5 = gates.reshape(nb, t, 2, 2 * ML_HEADS)
    g_c, g_r = g5.transpose(0, 2, 1, 3), g5.transpose(0, 2, 3, 1)
    hf = _mlstm(qk, pr, g_c, g_r, direction=0, n_lat=n_lat)
    y_ml = _mlstm(qk, pr, g_c, g_r, direction=1, n_lat=n_lat, final_args=(hf, ml_norm_w[None, :]))
    tab = _na_bias_table(rpb, n_lat // GRID_W)
    y_na = _na(pr_na, tab, q_norm_w[None, :], k_norm_w[None, :], n_lat=n_lat, n_ctx=n_ctx, qcol=0)
    return _out_projection((y_ml, y_na), w_out_all, x, _lat_ctx(mods[2], nb), j, rows=n_lat, n_lat=n_lat,
                           next_norm=next_norm, name="od_out")


def kernel(x, c, ctx, c_ctx, ada_w, ada_b, norm_w, ev_w_in, ev_conv_w, ev_conv_b, ssd_dt_bias, ssd_a_log, ssd_d,
           ssd_norm_w, hy_w1, hy_b1, hy_w2, hy_b2, hy_w3, hy_freq, hy_bias, ev_w_out, od_w_in, ml_conv_w, ml_conv_b,
           ml_gate_b, ml_norm_w, na_q_norm_w, na_k_norm_w, na_rpb, od_w_out, ffn_w_up, ffn_conv_w, ffn_conv_b,
           ffn_w_down):
    depth = ada_w.shape[0]
    assert depth == 2, "layer schedule below is written for one even and one odd layer"
    n_lat, n_ctx = x.shape[1], ctx.shape[1]
    t = n_lat + n_ctx
    both = ((0, n_lat), (n_lat, n_ctx))
    xs = jnp.concatenate([x, ctx], axis=1)

    nb = x.shape[0]
    w_down = ffn_w_down.astype(BF16)
    mods0 = _ada_mod(c, c_ctx, ada_w, ada_b, 0)
    mods1 = _ada_mod(c, c_ctx, ada_w, ada_b, 1)
    norm = lambda layer, sub, mods: (norm_w[layer, sub][None, :], _mod_pair(mods, 3 * sub, 3 * sub + 1, nb))

    w0, mod0 = norm(0, 0, mods0)
    u = _rmsmod(xs, w0, mod0, tm=_row_tile(t), rows=t, n_lat=n_lat)
    xs, u = _even_layer(xs, u, mods0, ev_w_in, ev_conv_w[0], ev_conv_b[0], ssd_dt_bias[0], ssd_a_log[0],
                        ssd_d[0], ssd_norm_w[0], hy_w1[0], hy_b1[0], hy_w2[0], hy_b2[0], hy_w3[0], hy_freq[0],
                        hy_bias[0], ev_w_out.astype(BF16), 0, n_lat=n_lat, n_ctx=n_ctx, next_norm=norm(0, 1, mods0))
    xs, u = _conv_ffn_block(xs, u, mods0, ffn_w_up, ffn_conv_w[0], ffn_conv_b[0], w_down, 0,
                            rows=t, n_lat=n_lat, segs=both, next_norm=norm(1, 0, mods1))
    xl, u = _odd_layer_latent(xs, u, mods1, od_w_in, ml_conv_w[0], ml_conv_b[0], ml_gate_b[0], ml_norm_w[0],
                              na_q_norm_w[0], na_k_norm_w[0], na_rpb[0], od_w_out.astype(BF16), 0,
                              n_lat=n_lat, n_ctx=n_ctx, next_norm=norm(1, 1, mods1))
    out, _ = _conv_ffn_block(xl, u, mods1, ffn_w_up, ffn_conv_w[1], ffn_conv_b[1], w_down, 1,
                             rows=n_lat, n_lat=n_lat, segs=((0, n_lat),), next_norm=None)
    return out
```
